```python
import jax, jax.numpy as jnp
from jax import lax
import numpy as np

D_MODEL = 1024
BATCH = 8
SEQ = 4096
DEPTH = 1

CHUNK = 64
EPS = 1e-6
N_HEADS_A = 8
HEAD_DIM_A = 64
D_A = N_HEADS_A * HEAD_DIM_A
N_PREV_CHUNKS = 8
BAND = (N_PREV_CHUNKS + 1) * CHUNK
REL_CLIP = 128
N_REL = 2 * REL_CLIP + 1
SGU_CHUNK = 128
N_GROUPS_B = 4
GROUP_DIM_B = 128
D_B = N_GROUPS_B * GROUP_DIM_B
SPLITS = (D_A, D_A, D_A, D_A, D_B, D_B, D_B, D_MODEL, D_MODEL)
D_IN = sum(SPLITS)
NEG_INF = -1e30

kernel_name = 'hybrid_chunked_attn_gmlp_gated'


def rmsnorm(x, g):
    xf = x.astype(jnp.float32)
    y = xf * lax.rsqrt(jnp.mean(xf * xf, axis=-1, keepdims=True) + EPS)
    return (y * g.astype(jnp.float32)).astype(x.dtype)


def layernorm(x, g, b):
    xf = x.astype(jnp.float32)
    mu = jnp.mean(xf, axis=-1, keepdims=True)
    var = jnp.mean(jnp.square(xf - mu), axis=-1, keepdims=True)
    y = (xf - mu) * lax.rsqrt(var + EPS)
    return (y * g.astype(jnp.float32) + b.astype(jnp.float32)).astype(x.dtype)


def chunked_rel_attention(q, k, v, rel_bias):
    b, s, _ = q.shape
    nc = s // CHUNK
    qc = q.reshape(b, nc, CHUNK, N_HEADS_A, HEAD_DIM_A)

    def band(t):
        t = t.reshape(b, nc, CHUNK, N_HEADS_A, HEAD_DIM_A)
        tp = jnp.pad(t, ((0, 0), (N_PREV_CHUNKS, 0), (0, 0), (0, 0), (0, 0)))
        return jnp.concatenate([tp[:, j:j + nc] for j in range(N_PREV_CHUNKS + 1)], axis=2)

    kb, vb = band(k), band(v)
    q_off = jnp.arange(CHUNK)
    k_off = jnp.arange(BAND) - N_PREV_CHUNKS * CHUNK
    dist = q_off[:, None] - k_off[None, :]
    bias = rel_bias[:, jnp.clip(dist, -REL_CLIP, REL_CLIP) + REL_CLIP].astype(jnp.float32)
    key_chunk = jnp.arange(nc)[:, None] + k_off[None, :] // CHUNK
    valid = key_chunk >= 0
    scale = HEAD_DIM_A ** -0.5
    scores = jnp.einsum('bnqhd,bnkhd->bhnqk', qc, kb).astype(jnp.float32) * scale
    scores = scores + bias[None, :, None, :, :]
    scores = jnp.where(valid[None, None, :, None, :], scores, NEG_INF)
    p = jax.nn.softmax(scores, axis=-1).astype(v.dtype)
    out = jnp.einsum('bhnqk,bnkhd->bnqhd', p, vb)
    return out.reshape(b, s, D_A)


def spatial_gating(u, v, ln_g, ln_b, w_s, b_s):
    b, s, _ = v.shape
    nb = s // SGU_CHUNK
    vn = layernorm(v, ln_g, ln_b).reshape(b, nb, SGU_CHUNK, N_GROUPS_B, GROUP_DIM_B)
    tri = jnp.tril(jnp.ones((SGU_CHUNK, SGU_CHUNK), dtype=bool))
    w = jnp.where(tri[None], w_s, jnp.zeros_like(w_s))
    mixed = jnp.einsum('gts,bnsgc->bntgc', w, vn) + jnp.transpose(b_s)[:, :, None]
    return u * mixed.reshape(b, s, D_B)


def hybrid_layer(x, norm_g, w_in, b_gate, rel_bias, sgu_ln_g, sgu_ln_b, w_s, b_s, w_pa, w_pb, w_out):
    h = rmsnorm(x, norm_g)
    z = jnp.einsum('bsd,de->bse', h, w_in)
    idx = list(np.cumsum(SPLITS)[:-1])
    q, k, v, g_a, u_b, v_b, g_b, gate_a, gate_b = jnp.split(z, idx, axis=-1)
    y_a = chunked_rel_attention(q, k, v, rel_bias) * jax.nn.silu(g_a)
    y_b = spatial_gating(jax.nn.gelu(u_b), jax.nn.gelu(v_b), sgu_ln_g, sgu_ln_b, w_s, b_s) * jax.nn.silu(g_b)
    p_a = jnp.einsum('bse,ed->bsd', y_a, w_pa)
    p_b = jnp.einsum('bse,ed->bsd', y_b, w_pb)
    ga = jax.nn.sigmoid(gate_a + b_gate[:D_MODEL])
    gb = jax.nn.sigmoid(gate_b + b_gate[D_MODEL:])
    merged = ga * p_a + gb * p_b
    return x + jnp.einsum('bsd,de->bse', merged, w_out)


def _fwd_setup_inputs(seed: int = 0) -> dict:
    key = jax.random.key(seed)
    ks = jax.random.split(key, 16)
    f32 = jnp.float32
    nrm = lambda k, shape, s: jax.random.normal(k, shape, f32) * s
    return {
        'x': jax.random.normal(ks[0], (BATCH, SEQ, D_MODEL), f32),
        'norm_g': 1.0 + nrm(ks[1], (DEPTH, D_MODEL), 0.05),
        'w_in': nrm(ks[2], (DEPTH, D_MODEL, D_IN), D_MODEL ** -0.5),
        'b_gate': nrm(ks[3], (DEPTH, 2 * D_MODEL), 0.1),
        'rel_bias': nrm(ks[4], (DEPTH, N_HEADS_A, N_REL), 0.5),
        'sgu_ln_g': 1.0 + nrm(ks[5], (DEPTH, D_B), 0.05),
        'sgu_ln_b': nrm(ks[6], (DEPTH, D_B), 0.05),
        'w_s': nrm(ks[7], (DEPTH, N_GROUPS_B, SGU_CHUNK, SGU_CHUNK), SGU_CHUNK ** -0.5),
        'b_s': 1.0 + nrm(ks[8], (DEPTH, N_GROUPS_B, SGU_CHUNK), 0.1),
        'w_pa': nrm(ks[9], (DEPTH, D_A, D_MODEL), D_A ** -0.5),
        'w_pb': nrm(ks[10], (DEPTH, D_B, D_MODEL), D_B ** -0.5),
        'w_out': nrm(ks[11], (DEPTH, D_MODEL, D_MODEL), D_MODEL ** -0.5),
        'final_g': 1.0 + nrm(ks[12], (D_MODEL,), 0.05),
    }


def _fwd_reference(x, norm_g, w_in, b_gate, rel_bias, sgu_ln_g, sgu_ln_b, w_s, b_s, w_pa, w_pb, w_out, final_g):
    for l in range(DEPTH):
        x = hybrid_layer(x, norm_g[l], w_in[l], b_gate[l], rel_bias[l], sgu_ln_g[l], sgu_ln_b[l],
                         w_s[l], b_s[l], w_pa[l], w_pb[l], w_out[l])
    return rmsnorm(x, final_g)


import jax as _jax
import jax.numpy as _jnp

TWIN_FORMAT = 'train_step'
FWD_PARAMS = ['x', 'norm_g', 'w_in', 'b_gate', 'rel_bias', 'sgu_ln_g', 'sgu_ln_b', 'w_s', 'b_s', 'w_pa', 'w_pb', 'w_out', 'final_g']
TWIN_WEIGHTS = ['norm_g', 'w_in', 'b_gate', 'rel_bias', 'sgu_ln_g', 'sgu_ln_b', 'w_s', 'b_s', 'w_pa', 'w_pb', 'w_out', 'final_g']
TWIN_DIFF_INPUT = 'x'
TWIN_INPUTS = ['x', 'norm_g', 'w_in', 'b_gate', 'rel_bias', 'sgu_ln_g', 'sgu_ln_b', 'w_s', 'b_s', 'w_pa', 'w_pb', 'w_out', 'final_g', 'loss_target', 'm_norm_g', 'm_w_in', 'm_b_gate', 'm_rel_bias', 'm_sgu_ln_g', 'm_sgu_ln_b', 'm_w_s', 'm_b_s', 'm_w_pa', 'm_w_pb', 'm_w_out', 'm_final_g', 'v_norm_g', 'v_w_in', 'v_b_gate', 'v_rel_bias', 'v_sgu_ln_g', 'v_sgu_ln_b', 'v_w_s', 'v_b_s', 'v_w_pa', 'v_w_pb', 'v_w_out', 'v_final_g']
TWIN_OUTPUTS = ['loss', 'grad_x', 'grad_norm_g', 'grad_w_in', 'grad_b_gate', 'grad_rel_bias', 'grad_sgu_ln_g', 'grad_sgu_ln_b', 'grad_w_s', 'grad_b_s', 'grad_w_pa', 'grad_w_pb', 'grad_w_out', 'grad_final_g', 'delta_norm_g', 'delta_w_in', 'delta_b_gate', 'delta_rel_bias', 'delta_sgu_ln_g', 'delta_sgu_ln_b', 'delta_w_s', 'delta_b_s', 'delta_w_pa', 'delta_w_pb', 'delta_w_out', 'delta_final_g', 'new_m_norm_g', 'new_m_w_in', 'new_m_b_gate', 'new_m_rel_bias', 'new_m_sgu_ln_g', 'new_m_sgu_ln_b', 'new_m_w_s', 'new_m_b_s', 'new_m_w_pa', 'new_m_w_pb', 'new_m_w_out', 'new_m_final_g', 'new_v_norm_g', 'new_v_w_in', 'new_v_b_gate', 'new_v_rel_bias', 'new_v_sgu_ln_g', 'new_v_sgu_ln_b', 'new_v_w_s', 'new_v_b_s', 'new_v_w_pa', 'new_v_w_pb', 'new_v_w_out', 'new_v_final_g']
TWIN_LEAF_KINDS = {'loss': 'loss', 'grad_x': 'grad_x', 'grad_norm_g': 'grad_w', 'grad_w_in': 'grad_w', 'grad_b_gate': 'grad_w', 'grad_rel_bias': 'grad_w', 'grad_sgu_ln_g': 'grad_w', 'grad_sgu_ln_b': 'grad_w', 'grad_w_s': 'grad_w', 'grad_b_s': 'grad_w', 'grad_w_pa': 'grad_w', 'grad_w_pb': 'grad_w', 'grad_w_out': 'grad_w', 'grad_final_g': 'grad_w', 'delta_norm_g': 'delta_w', 'delta_w_in': 'delta_w', 'delta_b_gate': 'delta_w', 'delta_rel_bias': 'delta_w', 'delta_sgu_ln_g': 'delta_w', 'delta_sgu_ln_b': 'delta_w', 'delta_w_s': 'delta_w', 'delta_b_s': 'delta_w', 'delta_w_pa': 'delta_w', 'delta_w_pb': 'delta_w', 'delta_w_out': 'delta_w', 'delta_final_g': 'delta_w', 'new_m_norm_g': 'new_m', 'new_m_w_in': 'new_m', 'new_m_b_gate': 'new_m', 'new_m_rel_bias': 'new_m', 'new_m_sgu_ln_g': 'new_m', 'new_m_sgu_ln_b': 'new_m', 'new_m_w_s': 'new_m', 'new_m_b_s': 'new_m', 'new_m_w_pa': 'new_m', 'new_m_w_pb': 'new_m', 'new_m_w_out': 'new_m', 'new_m_final_g': 'new_m', 'new_v_norm_g': 'new_v', 'new_v_w_in': 'new_v', 'new_v_b_gate': 'new_v', 'new_v_rel_bias': 'new_v', 'new_v_sgu_ln_g': 'new_v', 'new_v_sgu_ln_b': 'new_v', 'new_v_w_s': 'new_v', 'new_v_b_s': 'new_v', 'new_v_w_pa': 'new_v', 'new_v_w_pb': 'new_v', 'new_v_w_out': 'new_v', 'new_v_final_g': 'new_v'}


def _forward(args):
    return _fwd_reference(*[args[k] for k in FWD_PARAMS])


def _output_shape():
    out = _jax.eval_shape(lambda: _forward(_fwd_setup_inputs(0)))
    return out.shape, out.dtype

N_MICROBATCH = 1
ADAM_LR = 0.001
ADAM_B1 = 0.9
ADAM_B2 = 0.999
ADAM_EPS = 1e-08
ADAM_WD = 0.01
ADAM_STEP = 10
PER_EXAMPLE_BATCH_AXIS = {'x': 0, 'loss_target': 0}
SHARED_INPUTS = []
_WEIGHT_DTYPES = {'norm_g': _jnp.float32, 'w_in': _jnp.float32, 'b_gate': _jnp.float32, 'rel_bias': _jnp.float32, 'sgu_ln_g': _jnp.float32, 'sgu_ln_b': _jnp.float32, 'w_s': _jnp.float32, 'b_s': _jnp.float32, 'w_pa': _jnp.float32, 'w_pb': _jnp.float32, 'w_out': _jnp.float32, 'final_g': _jnp.float32}
MOMENT_SCALE = {'norm_g': 7.720103e-02, 'w_in': 3.267088e-02, 'b_gate': 1.249485e-02, 'rel_bias': 5.703158e-03, 'sgu_ln_g': 3.663472e-02, 'sgu_ln_b': 3.826557e-02, 'w_s': 3.664324e-02, 'b_s': 5.439255e-02, 'w_pa': 9.180731e-03, 'w_pb': 4.510183e-02, 'w_out': 4.598485e-02, 'final_g': 3.208511e+01}


def _to_microbatches(a, axis):
    t = _jnp.moveaxis(a, axis, 0)
    t = t.reshape((N_MICROBATCH, t.shape[0] // N_MICROBATCH) + t.shape[1:])
    return _jnp.moveaxis(t, 1, axis + 1)


def setup_inputs(seed: int = 0) -> dict:
    inp = _fwd_setup_inputs(seed)
    key = _jax.random.fold_in(_jax.random.key(seed), 7919)
    shape, _ = _output_shape()
    out = dict(inp)
    out["loss_target"] = _jax.random.normal(_jax.random.fold_in(key, 0), shape, _jnp.float32)
    for i, name in enumerate(TWIN_WEIGHTS):
        w = inp[name].astype(_jnp.float32)
        if MOMENT_SCALE is None:
            s = _jnp.sqrt(_jnp.mean(_jnp.square(w)) + 1e-30)
        else:
            s = MOMENT_SCALE[name]
        km, kv = _jax.random.split(_jax.random.fold_in(key, i + 1))
        out[name] = w
        out["m_" + name] = s * _jax.random.normal(km, w.shape, _jnp.float32)
        out["v_" + name] = (s * s) * _jax.random.uniform(kv, w.shape, _jnp.float32, 0.5, 1.5)
    if N_MICROBATCH > 1:
        for name, axis in PER_EXAMPLE_BATCH_AXIS.items():
            out[name] = _to_microbatches(out[name], axis)
    return {'x': out['x'], 'norm_g': out['norm_g'], 'w_in': out['w_in'], 'b_gate': out['b_gate'], 'rel_bias': out['rel_bias'], 'sgu_ln_g': out['sgu_ln_g'], 'sgu_ln_b': out['sgu_ln_b'], 'w_s': out['w_s'], 'b_s': out['b_s'], 'w_pa': out['w_pa'], 'w_pb': out['w_pb'], 'w_out': out['w_out'], 'final_g': out['final_g'], 'loss_target': out['loss_target'], 'm_norm_g': out['m_norm_g'], 'm_w_in': out['m_w_in'], 'm_b_gate': out['m_b_gate'], 'm_rel_bias': out['m_rel_bias'], 'm_sgu_ln_g': out['m_sgu_ln_g'], 'm_sgu_ln_b': out['m_sgu_ln_b'], 'm_w_s': out['m_w_s'], 'm_b_s': out['m_b_s'], 'm_w_pa': out['m_w_pa'], 'm_w_pb': out['m_w_pb'], 'm_w_out': out['m_w_out'], 'm_final_g': out['m_final_g'], 'v_norm_g': out['v_norm_g'], 'v_w_in': out['v_w_in'], 'v_b_gate': out['v_b_gate'], 'v_rel_bias': out['v_rel_bias'], 'v_sgu_ln_g': out['v_sgu_ln_g'], 'v_sgu_ln_b': out['v_sgu_ln_b'], 'v_w_s': out['v_w_s'], 'v_b_s': out['v_b_s'], 'v_w_pa': out['v_w_pa'], 'v_w_pb': out['v_w_pb'], 'v_w_out': out['v_w_out'], 'v_final_g': out['v_final_g']}


def _loss(weights, diff, rest, loss_target):
    with _jax.named_scope("forward"):
        args = {**rest, TWIN_DIFF_INPUT: diff, **{k: w.astype(_WEIGHT_DTYPES[k]) for k, w in weights.items()}}
        y = _forward(args)
    with _jax.named_scope("loss_head"):
        err = _jnp.square(y.astype(_jnp.float32) - loss_target)
        return 0.5 * _jnp.sum(_jnp.mean(err, axis=-1)) if err.ndim else 0.5 * err


def _adamw(w, g, m, v):
    m = ADAM_B1 * m + (1.0 - ADAM_B1) * g
    v = ADAM_B2 * v + (1.0 - ADAM_B2) * _jnp.square(g)
    m_hat = m / (1.0 - ADAM_B1 ** ADAM_STEP)
    v_hat = v / (1.0 - ADAM_B2 ** ADAM_STEP)
    delta = -ADAM_LR * (m_hat / (_jnp.sqrt(v_hat) + ADAM_EPS) + ADAM_WD * w)
    return delta, m, v


def reference(x, norm_g, w_in, b_gate, rel_bias, sgu_ln_g, sgu_ln_b, w_s, b_s, w_pa, w_pb, w_out, final_g, loss_target, m_norm_g, m_w_in, m_b_gate, m_rel_bias, m_sgu_ln_g, m_sgu_ln_b, m_w_s, m_b_s, m_w_pa, m_w_pb, m_w_out, m_final_g, v_norm_g, v_w_in, v_b_gate, v_rel_bias, v_sgu_ln_g, v_sgu_ln_b, v_w_s, v_b_s, v_w_pa, v_w_pb, v_w_out, v_final_g):
    given = dict(x=x, norm_g=norm_g, w_in=w_in, b_gate=b_gate, rel_bias=rel_bias, sgu_ln_g=sgu_ln_g, sgu_ln_b=sgu_ln_b, w_s=w_s, b_s=b_s, w_pa=w_pa, w_pb=w_pb, w_out=w_out, final_g=final_g, loss_target=loss_target, m_norm_g=m_norm_g, m_w_in=m_w_in, m_b_gate=m_b_gate, m_rel_bias=m_rel_bias, m_sgu_ln_g=m_sgu_ln_g, m_sgu_ln_b=m_sgu_ln_b, m_w_s=m_w_s, m_b_s=m_b_s, m_w_pa=m_w_pa, m_w_pb=m_w_pb, m_w_out=m_w_out, m_final_g=m_final_g, v_norm_g=v_norm_g, v_w_in=v_w_in, v_b_gate=v_b_gate, v_rel_bias=v_rel_bias, v_sgu_ln_g=v_sgu_ln_g, v_sgu_ln_b=v_sgu_ln_b, v_w_s=v_w_s, v_b_s=v_b_s, v_w_pa=v_w_pa, v_w_pb=v_w_pb, v_w_out=v_w_out, v_final_g=v_final_g)
    weights = {n: given[n] for n in TWIN_WEIGHTS}
    shared = {n: given[n] for n in SHARED_INPUTS}
    per_example = {n: given[n] for n in ['x']}
    grad_fn = _jax.value_and_grad(_loss, argnums=(0, 1))

    def one_microbatch(ex, loss_target):
        ex = dict(ex)
        diff = ex.pop(TWIN_DIFF_INPUT)
        return grad_fn(weights, diff, {**shared, **ex}, loss_target)

    if N_MICROBATCH == 1:
        loss, (grad_w, grad_x) = one_microbatch(per_example, given["loss_target"])
    else:
        def body(carry, xs):
            loss_sum, grad_sum = carry
            l_k, (gw_k, gx_k) = one_microbatch(xs[0], xs[1])
            with _jax.named_scope("update"):
                return (loss_sum + l_k, _jax.tree.map(_jnp.add, grad_sum, gw_k)), gx_k

        init = (_jnp.zeros((), _jnp.float32), _jax.tree.map(_jnp.zeros_like, weights))
        (loss, grad_w), grad_x = _jax.lax.scan(body, init, (per_example, given["loss_target"]))
    with _jax.named_scope("update"):
        delta_w, new_m, new_v = {}, {}, {}
        for n in TWIN_WEIGHTS:
            delta_w[n], new_m[n], new_v[n] = _adamw(weights[n], grad_w[n], given["m_" + n], given["v_" + n])
    return (loss, grad_x, *[grad_w[n] for n in TWIN_WEIGHTS], *[delta_w[n] for n in TWIN_WEIGHTS],
            *[new_m[n] for n in TWIN_WEIGHTS], *[new_v[n] for n in TWIN_WEIGHTS])
```

```python
import functools
import math

import jax
import jax.numpy as jnp
from jax import lax
from jax.experimental import pallas as pl
from jax.experimental.pallas import tpu as pltpu

F32 = jnp.float32
BF16 = jnp.bfloat16
MESH = pl.DeviceIdType.MESH
N_DEV = 8

D_MODEL = 1024
D_A = 512
D_B = 512
D_IN = 5632
N_HEADS = 8
HEAD_DIM = 64
CHUNK = 64
N_PREV = 8
REL_CLIP = 128
N_REL = 2 * REL_CLIP + 1
N_REL_PAD = 384
SGU_CHUNK = 128
N_GROUPS = 4
EPS = 1e-6
NEG_INF = -1e30
Q_SCALE = HEAD_DIM ** -0.5

Q_BLOCK = 256
K_SPAN = 768
ROLL_W = 1024
COL_BLOCK = 512
N_COL_BLOCKS = D_IN // COL_BLOCK
REST = D_IN - 3 * D_A
TOKEN_TILE = 256
V7X_VMEM_BYTES = 64 * 1024 * 1024

ADAM_LR = 0.001
ADAM_B1 = 0.9
ADAM_B2 = 0.999
ADAM_EPS = 1e-08
ADAM_WD = 0.01
ADAM_STEP = 10

GELU_C = math.sqrt(2.0 / math.pi)
GELU_A = 0.044715

NT = (((1,), (1,)), ((), ()))
TN = (((0,), (0,)), ((), ()))
HIGHEST = lax.Precision.HIGHEST


def _params(vmem_mb, **kw):
    return pltpu.CompilerParams(vmem_limit_bytes=vmem_mb * 1024 * 1024, **kw)


def _dot(a, b, dims=None):
    if dims is None:
        return jnp.dot(a, b, preferred_element_type=F32)
    return lax.dot_general(a, b, dims, preferred_element_type=F32)


def _sigmoid(x):
    return 1.0 / (1.0 + jnp.exp(-x))


def _gelu_and_grad(u):
    u2 = u * u
    t = jnp.tanh(GELU_C * (u + GELU_A * u * u2))
    half = 0.5 * (1.0 + t)
    g = u * half
    dg = half + 0.5 * u * (1.0 - t * t) * (GELU_C * (1.0 + 3.0 * GELU_A * u2))
    return g, dg


def _my_pos():
    return lax.axis_index("x"), lax.axis_index("y"), lax.axis_index("c")


def _flat_id(pos):
    return 4 * pos[0] + 2 * pos[1] + pos[2]


def _peer(pos, k):
    x, y, c = pos
    return (1 - x if k & 4 else x, 1 - y if k & 2 else y, 1 - c if k & 1 else c)


def _gather_weights(w_in, w_pa, w_pb, w_out):
    shards = (w_in, w_pa, w_pb, w_out)
    n = len(shards)

    def body(*refs):
        ins, outs = refs[:n], refs[n:2 * n]
        send_sems, recv_sems = refs[2 * n], refs[2 * n + 1]
        pos = _my_pos()
        me = _flat_id(pos)
        for src, dst in zip(ins, outs):
            dst[me] = src[...].astype(BF16)

        def copy(a, k, slot):
            return pltpu.make_async_remote_copy(
                src_ref=outs[a].at[slot], dst_ref=outs[a].at[slot],
                send_sem=send_sems.at[a, k - 1], recv_sem=recv_sems.at[a, k - 1],
                device_id=_peer(pos, k), device_id_type=MESH)

        sends = [copy(a, k, me) for k in range(1, N_DEV) for a in range(n)]
        for cp in sends:
            cp.start()
        for k in range(1, N_DEV):
            for a in range(n):
                copy(a, k, _flat_id(_peer(pos, k))).wait_recv()
        for cp in sends:
            cp.wait_send()

    vmem = pl.BlockSpec(memory_space=pltpu.VMEM)
    return pl.pallas_call(
        body, name="gather_weights",
        out_shape=tuple(jax.ShapeDtypeStruct((N_DEV,) + s.shape, BF16) for s in shards),
        in_specs=[vmem] * n, out_specs=tuple([vmem] * n),
        scratch_shapes=[pltpu.SemaphoreType.DMA((n, N_DEV - 1)), pltpu.SemaphoreType.DMA((n, N_DEV - 1))],
        compiler_params=_params(40),
    )(*shards)


def _reduce_grads(g_in, g_pa, g_pb, g_out):
    parts = (g_in, g_pa, g_pb, g_out)
    n = len(parts)

    def body(*refs):
        ins, outs = refs[:n], refs[n:2 * n]
        lands = refs[2 * n:3 * n]
        send_sems, recv_sems = refs[3 * n], refs[3 * n + 1]
        pos = _my_pos()
        me = _flat_id(pos)

        def copy(a, k):
            return pltpu.make_async_remote_copy(
                src_ref=ins[a].at[_flat_id(_peer(pos, k))], dst_ref=lands[a].at[k - 1],
                send_sem=send_sems.at[a, k - 1], recv_sem=recv_sems.at[a, k - 1],
                device_id=_peer(pos, k), device_id_type=MESH)

        sends = [copy(a, k) for k in range(1, N_DEV) for a in range(n)]
        for cp in sends:
            cp.start()
        for a in range(n):
            acc = ins[a][me].astype(F32)
            for k in range(1, N_DEV):
                copy(a, k).wait_recv()
                acc = acc + lands[a][k - 1].astype(F32)
            outs[a][...] = acc
        for cp in sends:
            cp.wait_send()

    vmem = pl.BlockSpec(memory_space=pltpu.VMEM)
    return pl.pallas_call(
        body, name="reduce_grads",
        out_shape=tuple(jax.ShapeDtypeStruct(p.shape[1:], F32) for p in parts),
        in_specs=[vmem] * n, out_specs=tuple([vmem] * n),
        scratch_shapes=[pltpu.VMEM((N_DEV - 1,) + p.shape[1:], BF16) for p in parts]
        + [pltpu.SemaphoreType.DMA((n, N_DEV - 1)), pltpu.SemaphoreType.DMA((n, N_DEV - 1))],
        compiler_params=_params(48),
    )(*parts)


def _reduce_small(slab):
    def body(slab_ref, out_ref, land, send_sems, recv_sems):
        pos = _my_pos()
        me = _flat_id(pos)

        def copy(k):
            return pltpu.make_async_remote_copy(
                src_ref=slab_ref, dst_ref=land.at[me],
                send_sem=send_sems.at[k - 1], recv_sem=recv_sems.at[k - 1],
                device_id=_peer(pos, k), device_id_type=MESH)

        def arrival(k):
            return pltpu.make_async_remote_copy(
                src_ref=slab_ref, dst_ref=land.at[_flat_id(_peer(pos, k))],
                send_sem=send_sems.at[k - 1], recv_sem=recv_sems.at[k - 1],
                device_id=_peer(pos, k), device_id_type=MESH)

        sends = [copy(k) for k in range(1, N_DEV)]
        for cp in sends:
            cp.start()
        land[me] = slab_ref[...]
        for k in range(1, N_DEV):
            arrival(k).wait_recv()
        acc = land[0]
        for d in range(1, N_DEV):
            acc = acc + land[d]
        out_ref[...] = acc
        for cp in sends:
            cp.wait_send()

    vmem = pl.BlockSpec(memory_space=pltpu.VMEM)
    return pl.pallas_call(
        body, name="reduce_small",
        out_shape=jax.ShapeDtypeStruct(slab.shape, F32),
        in_specs=[vmem], out_specs=vmem,
        scratch_shapes=[pltpu.VMEM((N_DEV,) + slab.shape, F32),
                        pltpu.SemaphoreType.DMA((N_DEV - 1,)), pltpu.SemaphoreType.DMA((N_DEV - 1,))],
        compiler_params=_params(16),
    )(slab)


def _rel_index(e):
    return jnp.where(e <= 384, 2 * REL_CLIP, jnp.where(e < 640, 640 - e, jnp.where(e <= K_SPAN, 0, 2 * REL_CLIP)))


def _bias_table(rel_bias_pad):
    def body(rb_ref, bt_ref):
        c = lax.broadcasted_iota(jnp.int32, (N_REL_PAD, ROLL_W), 1)
        r = lax.broadcasted_iota(jnp.int32, (N_REL_PAD, ROLL_W), 0)
        pick = (r == _rel_index(c)).astype(F32)
        rows = jnp.dot(rb_ref[...], pick, precision=HIGHEST, preferred_element_type=F32)
        qc = lax.broadcasted_iota(jnp.int32, (Q_BLOCK, K_SPAN), 0) >> 6
        kc = lax.broadcasted_iota(jnp.int32, (Q_BLOCK, K_SPAN), 1) >> 6
        band = (kc >= qc) & (kc <= qc + N_PREV)
        for h in range(N_HEADS):
            t = jnp.broadcast_to(rows[h:h + 1, :], (Q_BLOCK, ROLL_W))
            t = pltpu.roll(t, 0, 1, stride=1, stride_axis=0)
            bt_ref[h] = jnp.where(band, t[:, :K_SPAN], NEG_INF)

    return pl.pallas_call(
        body, name="bias_table",
        out_shape=jax.ShapeDtypeStruct((N_HEADS, Q_BLOCK, K_SPAN), F32),
        compiler_params=_params(32),
    )(rel_bias_pad)


def _bias_grad(dbias):
    def body(a_ref, o_ref):
        rr = lax.broadcasted_iota(jnp.int32, (Q_BLOCK, Q_BLOCK), 0)
        cc = lax.broadcasted_iota(jnp.int32, (Q_BLOCK, Q_BLOCK), 1)
        flip = (rr + cc == Q_BLOCK - 1).astype(F32)
        c = lax.broadcasted_iota(jnp.int32, (ROLL_W, N_REL_PAD), 0)
        r = lax.broadcasted_iota(jnp.int32, (ROLL_W, N_REL_PAD), 1)
        e = jnp.where(c >= Q_BLOCK - 1, c - (Q_BLOCK - 1), c + (ROLL_W - Q_BLOCK + 1))
        pick = (r == _rel_index(e)).astype(F32)
        sums = []
        for h in range(N_HEADS):
            a = jnp.dot(flip, a_ref[h], precision=HIGHEST, preferred_element_type=F32)
            a = jnp.concatenate([a, jnp.zeros((Q_BLOCK, ROLL_W - K_SPAN), F32)], axis=1)
            a = pltpu.roll(a, 0, 1, stride=1, stride_axis=0)
            sums.append(jnp.sum(a, axis=0, keepdims=True))
        diag = jnp.concatenate(sums, axis=0)
        o_ref[...] = jnp.dot(diag, pick, precision=HIGHEST, preferred_element_type=F32)

    return pl.pallas_call(
        body, name="bias_grad",
        out_shape=jax.ShapeDtypeStruct((N_HEADS, N_REL_PAD), F32),
        compiler_params=_params(32),
    )(dbias)


def _proj_fwd(x, norm_g, w_in):
    s = x.shape[0]
    tm = TOKEN_TILE

    def body(x_ref, g_ref, w_ref, qkv_ref, rest_ref, ht_ref):
        xf = x_ref[...]
        r = lax.rsqrt(jnp.mean(xf * xf, axis=-1, keepdims=True) + EPS)
        hf = xf * r * g_ref[...]
        h = hf.astype(BF16)
        ht_ref[...] = hf.T.astype(BF16)
        for c in range(N_COL_BLOCKS):
            blk = _dot(h, w_ref[:, c * COL_BLOCK:(c + 1) * COL_BLOCK])
            if c == 0:
                qkv_ref[0] = (blk * Q_SCALE).astype(BF16)
            elif c < 3:
                qkv_ref[c] = blk.astype(BF16)
            else:
                rest_ref[:, (c - 3) * COL_BLOCK:(c - 2) * COL_BLOCK] = blk

    return pl.pallas_call(
        body, name="proj_fwd",
        grid=(s // tm,),
        in_specs=[pl.BlockSpec((tm, D_MODEL), lambda i: (i, 0)),
                  pl.BlockSpec((1, D_MODEL), lambda i: (0, 0)),
                  pl.BlockSpec((D_MODEL, D_IN), lambda i: (0, 0))],
        out_specs=(pl.BlockSpec((3, tm, D_A), lambda i: (0, i, 0)),
                   pl.BlockSpec((tm, REST), lambda i: (i, 0)),
                   pl.BlockSpec((D_MODEL, tm), lambda i: (0, i))),
        out_shape=(jax.ShapeDtypeStruct((3, s, D_A), BF16),
                   jax.ShapeDtypeStruct((s, REST), F32),
                   jax.ShapeDtypeStruct((D_MODEL, s), BF16)),
        compiler_params=_params(56),
    )(x, norm_g, w_in)


def _attn_specs():
    def kv(which, back):
        return pl.BlockSpec((None, Q_BLOCK, 128), lambda p, b: (which, jnp.maximum(b - back, 0), p))
    return ([pl.BlockSpec((None, Q_BLOCK, 128), lambda p, b: (0, b, p))]
            + [kv(1, back) for back in (2, 1, 0)] + [kv(2, back) for back in (2, 1, 0)]
            + [pl.BlockSpec((2, Q_BLOCK, K_SPAN), lambda p, b: (p, 0, 0))])


def _head_masks():
    lane = lax.broadcasted_iota(jnp.int32, (1, 128), 1)
    first = lane < HEAD_DIM
    return (first, jnp.logical_not(first))


def _softmax_rows(qm, kcat, bias, valid):
    s = _dot(qm, kcat, NT) + bias
    s = jnp.where(valid, s, NEG_INF)
    m = jnp.max(s, axis=-1, keepdims=True)
    e = jnp.exp(s - m)
    return e * (1.0 / jnp.sum(e, axis=-1, keepdims=True))


def _attn_fwd(qkv, bias_table):
    s = qkv.shape[1]
    nb = s // Q_BLOCK

    def body(q_ref, k2_ref, k1_ref, k0_ref, v2_ref, v1_ref, v0_ref, bt_ref, o_ref):
        b = pl.program_id(1)
        q = q_ref[...]
        kcat = jnp.concatenate([k2_ref[...], k1_ref[...], k0_ref[...]], axis=0)
        vcat = jnp.concatenate([v2_ref[...], v1_ref[...], v0_ref[...]], axis=0)
        valid = lax.broadcasted_iota(jnp.int32, (1, K_SPAN), 1) >= (2 - b) * Q_BLOCK
        zero = jnp.zeros((), BF16)
        out = None
        for hh, mask in enumerate(_head_masks()):
            p = _softmax_rows(jnp.where(mask, q, zero), kcat, bt_ref[hh], valid)
            o = _dot(p.astype(BF16), jnp.where(mask, vcat, zero))
            out = o if out is None else out + o
        o_ref[...] = out

    return pl.pallas_call(
        body, name="attn_fwd",
        grid=(N_HEADS // 2, nb),
        in_specs=_attn_specs(),
        out_specs=pl.BlockSpec((Q_BLOCK, 128), lambda p, b: (b, p)),
        out_shape=jax.ShapeDtypeStruct((s, D_A), F32),
        compiler_params=_params(40),
    )(qkv, qkv, qkv, qkv, qkv, qkv, qkv, bias_table)


def _attn_bwd(qkv, bias_table, d_out):
    s = qkv.shape[1]
    nb = s // Q_BLOCK

    def body(q_ref, k2_ref, k1_ref, k0_ref, v2_ref, v1_ref, v0_ref, bt_ref, do_ref, dqkv_ref, db_ref, dk_acc, dv_acc):
        b = pl.program_id(1)

        @pl.when(b == 0)
        def _():
            dk_acc[...] = jnp.zeros(dk_acc.shape, F32)
            dv_acc[...] = jnp.zeros(dv_acc.shape, F32)
            db_ref[...] = jnp.zeros(db_ref.shape, F32)

        q = q_ref[...]
        do = do_ref[...]
        kcat = jnp.concatenate([k2_ref[...], k1_ref[...], k0_ref[...]], axis=0)
        vcat = jnp.concatenate([v2_ref[...], v1_ref[...], v0_ref[...]], axis=0)
        valid = lax.broadcasted_iota(jnp.int32, (1, K_SPAN), 1) >= (2 - b) * Q_BLOCK
        zero = jnp.zeros((), BF16)
        dq = dk = dv = None
        for hh, mask in enumerate(_head_masks()):
            qm = jnp.where(mask, q, zero)
            dom = jnp.where(mask, do, zero)
            p = _softmax_rows(qm, kcat, bt_ref[hh], valid)
            dp = _dot(dom, vcat, NT)
            ds = p * (dp - jnp.sum(p * dp, axis=-1, keepdims=True))
            db_ref[hh] += ds
            dsb = ds.astype(BF16)
            dq_h = _dot(dsb, jnp.where(mask, kcat, zero))
            dk_h = _dot(dsb, qm, TN)
            dv_h = _dot(p.astype(BF16), dom, TN)
            dq = dq_h if dq is None else dq + dq_h
            dk = dk_h if dk is None else dk + dk_h
            dv = dv_h if dv is None else dv + dv_h

        rows_b = pl.ds(pl.multiple_of(b * Q_BLOCK, Q_BLOCK), Q_BLOCK)
        dqkv_ref[0, rows_b, :] = (dq * Q_SCALE).astype(BF16)

        for j in range(3):
            slot = lax.rem(b + 1 + j, 3)
            part_k = dk[j * Q_BLOCK:(j + 1) * Q_BLOCK]
            part_v = dv[j * Q_BLOCK:(j + 1) * Q_BLOCK]
            if j == 2:
                dk_acc[slot] = part_k
                dv_acc[slot] = part_v
            else:
                dk_acc[slot] += part_k
                dv_acc[slot] += part_v

        def flush(block):
            slot = lax.rem(block + 3, 3)
            rows = pl.ds(pl.multiple_of(block * Q_BLOCK, Q_BLOCK), Q_BLOCK)
            dqkv_ref[1, rows, :] = dk_acc[slot].astype(BF16)
            dqkv_ref[2, rows, :] = dv_acc[slot].astype(BF16)

        @pl.when(b >= 2)
        def _():
            flush(b - 2)

        @pl.when(b == nb - 1)
        def _():
            flush(b - 1)
            flush(b)

    return pl.pallas_call(
        body, name="attn_bwd",
        grid=(N_HEADS // 2, nb),
        in_specs=_attn_specs() + [pl.BlockSpec((Q_BLOCK, 128), lambda p, b: (b, p))],
        out_specs=(pl.BlockSpec((3, s, 128), lambda p, b: (0, 0, p)),
                   pl.BlockSpec((2, Q_BLOCK, K_SPAN), lambda p, b: (p, 0, 0))),
        out_shape=(jax.ShapeDtypeStruct((3, s, D_A), BF16),
                   jax.ShapeDtypeStruct((N_HEADS, Q_BLOCK, K_SPAN), F32)),
        scratch_shapes=[pltpu.VMEM((3, Q_BLOCK, 128), F32), pltpu.VMEM((3, Q_BLOCK, 128), F32)],
        compiler_params=_params(48),
    )(qkv, qkv, qkv, qkv, qkv, qkv, qkv, bias_table, d_out)


def _mid_fwd_bwd(x, target, attn_out, rest, w_pa, w_pb, w_out, b_gate, ln_g, ln_b, w_s, b_s, final_g):
    s = x.shape[0]
    tm = TOKEN_TILE
    nt = s // tm
    n_sub = tm // SGU_CHUNK

    def body(x_ref, t_ref, oa_ref, z_ref, wpa_hbm, wpb_hbm, wout_hbm, bg_ref, lng_ref, lnb_ref, ws_ref, bs_ref, fg_ref,
             dx2_ref, doa_ref, dz_ref, dwout_hbm, dwpa_hbm, dwpb_hbm, dbg_ref, dfg_ref, dlng_ref, dlnb_ref, dws_ref,
             dbs_ref, loss_ref,
             wpa, wpb, wout, wmix, acc_out, acc_pa, acc_pb, mixed_s, dvn_s, sem):
        i = pl.program_id(0)

        @pl.when(i == 0)
        def _():
            loads = [pltpu.make_async_copy(src, dst, sem.at[n])
                     for n, (src, dst) in enumerate(((wpa_hbm, wpa), (wpb_hbm, wpb), (wout_hbm, wout)))]
            for cp in loads:
                cp.start()
            t_idx = lax.broadcasted_iota(jnp.int32, (SGU_CHUNK, SGU_CHUNK), 0)
            s_idx = lax.broadcasted_iota(jnp.int32, (SGU_CHUNK, SGU_CHUNK), 1)
            for g in range(N_GROUPS):
                wmix[g] = jnp.where(s_idx <= t_idx, ws_ref[g], 0.0).astype(BF16)
            for ref in (acc_out, acc_pa, acc_pb, dbg_ref, dfg_ref, dlng_ref, dlnb_ref, dws_ref, dbs_ref, loss_ref):
                ref[...] = jnp.zeros(ref.shape, F32)
            for cp in loads:
                cp.wait()

        g_a = z_ref[:, 0:512]
        u_b = z_ref[:, 512:1024]
        v_b = z_ref[:, 1024:1536]
        g_b = z_ref[:, 1536:2048]
        bg = bg_ref[...]

        sg_a = _sigmoid(g_a)
        silu_a = g_a * sg_a
        o_a = oa_ref[...]
        y_a = (o_a * silu_a).astype(BF16)

        ug, dgelu_u = _gelu_and_grad(u_b)
        vg, dgelu_v = _gelu_and_grad(v_b)
        mu = jnp.mean(vg, axis=-1, keepdims=True)
        vc = vg - mu
        rstd = lax.rsqrt(jnp.mean(vc * vc, axis=-1, keepdims=True) + EPS)
        vhat = vc * rstd
        lng = lng_ref[...]
        vn = (vhat * lng + lnb_ref[...]).astype(BF16)
        for n in range(n_sub):
            rows = slice(n * SGU_CHUNK, (n + 1) * SGU_CHUNK)
            for g in range(N_GROUPS):
                cols = slice(g * 128, (g + 1) * 128)
                mixed_s[rows, cols] = _dot(wmix[g], vn[rows, cols]) + bs_ref[g]
        mixed = mixed_s[...]
        sg_b = _sigmoid(g_b)
        silu_b = g_b * sg_b
        um = ug * mixed
        y_b = (um * silu_b).astype(BF16)

        p_a = _dot(y_a, wpa[...])
        p_b = _dot(y_b, wpb[...])
        gate_a = _sigmoid(z_ref[:, 2048:3072] + bg[:, :D_MODEL])
        gate_b = _sigmoid(z_ref[:, 3072:4096] + bg[:, D_MODEL:])
        merged = (gate_a * p_a + gate_b * p_b).astype(BF16)
        x2 = x_ref[...] + _dot(merged, wout[...])
        r2 = lax.rsqrt(jnp.mean(x2 * x2, axis=-1, keepdims=True) + EPS)
        xh = x2 * r2
        fg = fg_ref[...]
        err = xh * fg - t_ref[...]
        loss_ref[...] += jnp.sum(jnp.sum(err * err, axis=-1, keepdims=True), axis=0, keepdims=True) * (0.5 / D_MODEL)

        dy = err * (1.0 / D_MODEL)
        dfg_ref[...] += jnp.sum(dy * xh, axis=0, keepdims=True)
        gy = dy * fg
        dx2 = r2 * (gy - xh * jnp.mean(gy * xh, axis=-1, keepdims=True))
        dx2_ref[...] = dx2
        dx2b = dx2.astype(BF16)
        dmerged = _dot(dx2b, wout[...], NT)
        acc_out[...] += _dot(merged, dx2b, TN)

        dp_a = dmerged * gate_a
        dp_b = dmerged * gate_b
        dgate_a = dp_a * p_a * (1.0 - gate_a)
        dgate_b = dp_b * p_b * (1.0 - gate_b)
        dbg_ref[:, :D_MODEL] += jnp.sum(dgate_a, axis=0, keepdims=True)
        dbg_ref[:, D_MODEL:] += jnp.sum(dgate_b, axis=0, keepdims=True)
        dz_ref[:, 2048:3072] = dgate_a.astype(BF16)
        dz_ref[:, 3072:4096] = dgate_b.astype(BF16)
        dp_ab = dp_a.astype(BF16)
        dp_bb = dp_b.astype(BF16)
        dy_a = _dot(dp_ab, wpa[...], NT)
        dy_b = _dot(dp_bb, wpb[...], NT)
        acc_pa[...] += _dot(y_a, dp_ab, TN)
        acc_pb[...] += _dot(y_b, dp_bb, TN)

        doa_ref[...] = (dy_a * silu_a).astype(BF16)
        dz_ref[:, 0:512] = (dy_a * o_a * (sg_a * (1.0 + g_a * (1.0 - sg_a)))).astype(BF16)
        dz_ref[:, 1536:2048] = (dy_b * um * (sg_b * (1.0 + g_b * (1.0 - sg_b)))).astype(BF16)
        dys = dy_b * silu_b
        dz_ref[:, 512:1024] = (dys * mixed * dgelu_u).astype(BF16)
        dmixed = dys * ug
        dmb = dmixed.astype(BF16)
        for n in range(n_sub):
            rows = slice(n * SGU_CHUNK, (n + 1) * SGU_CHUNK)
            for g in range(N_GROUPS):
                cols = slice(g * 128, (g + 1) * 128)
                dws_ref[g] += _dot(dmb[rows, cols], vn[rows, cols], NT)
                dbs_ref[g] += jnp.sum(dmixed[rows, cols], axis=-1, keepdims=True)
                dvn_s[rows, cols] = _dot(wmix[g], dmb[rows, cols], TN)
        dvn = dvn_s[...]
        dlng_ref[...] += jnp.sum(dvn * vhat, axis=0, keepdims=True)
        dlnb_ref[...] += jnp.sum(dvn, axis=0, keepdims=True)
        dvh = dvn * lng
        dvg = rstd * (dvh - jnp.mean(dvh, axis=-1, keepdims=True) - vhat * jnp.mean(dvh * vhat, axis=-1, keepdims=True))
        dz_ref[:, 1024:1536] = (dvg * dgelu_v).astype(BF16)

        @pl.when(i == nt - 1)
        def _():
            t_idx = lax.broadcasted_iota(jnp.int32, (SGU_CHUNK, SGU_CHUNK), 0)
            s_idx = lax.broadcasted_iota(jnp.int32, (SGU_CHUNK, SGU_CHUNK), 1)
            for g in range(N_GROUPS):
                dws_ref[g] = jnp.where(s_idx <= t_idx, dws_ref[g], 0.0)
            stores = [pltpu.make_async_copy(src, dst, sem.at[n])
                      for n, (src, dst) in enumerate(((acc_out, dwout_hbm), (acc_pa, dwpa_hbm), (acc_pb, dwpb_hbm)))]
            for cp in stores:
                cp.start()
            for cp in stores:
                cp.wait()

    tile = lambda w: pl.BlockSpec((tm, w), lambda i: (i, 0))
    whole = lambda shape: pl.BlockSpec(shape, lambda i: (0,) * len(shape))
    hbm = pl.BlockSpec(memory_space=pl.ANY)
    return pl.pallas_call(
        body, name="mid_fwd_bwd",
        grid=(nt,),
        in_specs=[tile(D_MODEL), tile(D_MODEL), tile(D_A), tile(REST), hbm, hbm, hbm,
                  whole((1, 2 * D_MODEL)), whole((1, D_B)), whole((1, D_B)),
                  whole((N_GROUPS, SGU_CHUNK, SGU_CHUNK)), whole((N_GROUPS, SGU_CHUNK, 1)), whole((1, D_MODEL))],
        out_specs=(tile(D_MODEL), tile(D_A), tile(REST), hbm, hbm, hbm,
                   whole((1, 2 * D_MODEL)), whole((1, D_MODEL)), whole((1, D_B)), whole((1, D_B)),
                   whole((N_GROUPS, SGU_CHUNK, SGU_CHUNK)), whole((N_GROUPS, SGU_CHUNK, 1)), whole((1, 1))),
        out_shape=(jax.ShapeDtypeStruct((s, D_MODEL), F32), jax.ShapeDtypeStruct((s, D_A), BF16),
                   jax.ShapeDtypeStruct((s, REST), BF16),
                   jax.ShapeDtypeStruct((D_MODEL, D_MODEL), F32), jax.ShapeDtypeStruct((D_A, D_MODEL), F32),
                   jax.ShapeDtypeStruct((D_B, D_MODEL), F32),
                   jax.ShapeDtypeStruct((1, 2 * D_MODEL), F32), jax.ShapeDtypeStruct((1, D_MODEL), F32),
                   jax.ShapeDtypeStruct((1, D_B), F32), jax.ShapeDtypeStruct((1, D_B), F32),
                   jax.ShapeDtypeStruct((N_GROUPS, SGU_CHUNK, SGU_CHUNK), F32),
                   jax.ShapeDtypeStruct((N_GROUPS, SGU_CHUNK, 1), F32), jax.ShapeDtypeStruct((1, 1), F32)),
        scratch_shapes=[pltpu.VMEM((D_A, D_MODEL), BF16), pltpu.VMEM((D_B, D_MODEL), BF16),
                        pltpu.VMEM((D_MODEL, D_MODEL), BF16), pltpu.VMEM((N_GROUPS, SGU_CHUNK, SGU_CHUNK), BF16),
                        pltpu.VMEM((D_MODEL, D_MODEL), F32), pltpu.VMEM((D_A, D_MODEL), F32),
                        pltpu.VMEM((D_B, D_MODEL), F32),
                        pltpu.VMEM((tm, D_B), F32), pltpu.VMEM((tm, D_B), F32),
                        pltpu.SemaphoreType.DMA((3,))],
        compiler_params=_params(56),
    )(x, target, attn_out, rest, w_pa, w_pb, w_out, b_gate, ln_g, ln_b, w_s, b_s, final_g)


def _proj_bwd_x(dqkv, drest, x, dx2, norm_g, w_in):
    s = x.shape[0]
    tm = TOKEN_TILE

    def body(dqkv_ref, dr_ref, x_ref, dx2_ref, g_ref, w_hbm, dx_ref, dg_ref, w, sem):
        i = pl.program_id(0)

        @pl.when(i == 0)
        def _():
            cp = pltpu.make_async_copy(w_hbm, w, sem)
            cp.start()
            dg_ref[...] = jnp.zeros(dg_ref.shape, F32)
            cp.wait()

        dh = None
        for c in range(N_COL_BLOCKS):
            dz = dqkv_ref[c] if c < 3 else dr_ref[:, (c - 3) * COL_BLOCK:(c - 2) * COL_BLOCK]
            part = _dot(dz, w[:, c * COL_BLOCK:(c + 1) * COL_BLOCK], NT)
            dh = part if dh is None else dh + part
        xf = x_ref[...]
        r = lax.rsqrt(jnp.mean(xf * xf, axis=-1, keepdims=True) + EPS)
        xn = xf * r
        dg_ref[...] += jnp.sum(dh * xn, axis=0, keepdims=True)
        gh = dh * g_ref[...]
        dx_ref[...] = r * (gh - xn * jnp.mean(gh * xn, axis=-1, keepdims=True)) + dx2_ref[...]

    return pl.pallas_call(
        body, name="proj_bwd_x",
        grid=(s // tm,),
        in_specs=[pl.BlockSpec((3, tm, D_A), lambda i: (0, i, 0)),
                  pl.BlockSpec((tm, REST), lambda i: (i, 0)),
                  pl.BlockSpec((tm, D_MODEL), lambda i: (i, 0)),
                  pl.BlockSpec((tm, D_MODEL), lambda i: (i, 0)),
                  pl.BlockSpec((1, D_MODEL), lambda i: (0, 0)),
                  pl.BlockSpec(memory_space=pl.ANY)],
        out_specs=(pl.BlockSpec((tm, D_MODEL), lambda i: (i, 0)),
                   pl.BlockSpec((1, D_MODEL), lambda i: (0, 0))),
        out_shape=(jax.ShapeDtypeStruct((s, D_MODEL), F32), jax.ShapeDtypeStruct((1, D_MODEL), F32)),
        scratch_shapes=[pltpu.VMEM((D_MODEL, D_IN), BF16), pltpu.SemaphoreType.DMA],
        compiler_params=_params(48),
    )(dqkv, drest, x, dx2, norm_g, w_in)


def _proj_bwd_w(h_t, dqkv, drest):
    s = h_t.shape[1]
    tk = min(s, 1024)
    nk = s // tk

    def body(ht_ref, dqkv_ref, dr_ref, o_ref, acc):
        j = pl.program_id(0)
        i = pl.program_id(1)

        @pl.when(i == 0)
        def _():
            acc[...] = jnp.zeros(acc.shape, F32)

        @pl.when(j < 3)
        def _():
            acc[...] += _dot(ht_ref[...], dqkv_ref[...])

        @pl.when(j >= 3)
        def _():
            acc[...] += _dot(ht_ref[...], dr_ref[...])

        @pl.when(i == nk - 1)
        def _():
            o_ref[...] = acc[...].astype(BF16)

    return pl.pallas_call(
        body, name="proj_bwd_w",
        grid=(N_COL_BLOCKS, nk),
        in_specs=[pl.BlockSpec((D_MODEL, tk), lambda j, i: (0, i)),
                  pl.BlockSpec((None, tk, COL_BLOCK),
                               lambda j, i: (jnp.minimum(j, 2), jnp.where(j < 3, i, nk - 1), 0)),
                  pl.BlockSpec((tk, COL_BLOCK),
                               lambda j, i: (jnp.where(j >= 3, i, 0), jnp.maximum(j - 3, 0)))],
        out_specs=pl.BlockSpec((D_MODEL, COL_BLOCK), lambda j, i: (0, j)),
        out_shape=jax.ShapeDtypeStruct((D_MODEL, D_IN), BF16),
        scratch_shapes=[pltpu.VMEM((D_MODEL, COL_BLOCK), F32)],
        compiler_params=_params(40),
    )(h_t, dqkv, drest)


def _adamw(name, w, g, m, v):
    rows, cols = w.shape
    tr = rows if rows * cols <= 512 * 1024 else 256
    c1 = 1.0 - ADAM_B1 ** ADAM_STEP
    c2 = 1.0 - ADAM_B2 ** ADAM_STEP

    def body(w_ref, g_ref, m_ref, v_ref, d_ref, nm_ref, nv_ref):
        gg = g_ref[...]
        nm = ADAM_B1 * m_ref[...] + (1.0 - ADAM_B1) * gg
        nv = ADAM_B2 * v_ref[...] + (1.0 - ADAM_B2) * (gg * gg)
        d_ref[...] = -ADAM_LR * ((nm / c1) / (jnp.sqrt(nv / c2) + ADAM_EPS) + ADAM_WD * w_ref[...])
        nm_ref[...] = nm
        nv_ref[...] = nv

    spec = pl.BlockSpec((tr, cols), lambda i: (i, 0))
    shape = jax.ShapeDtypeStruct((rows, cols), F32)
    return pl.pallas_call(
        body, name=name,
        grid=(rows // tr,),
        in_specs=[spec] * 4, out_specs=(spec,) * 3, out_shape=(shape,) * 3,
        compiler_params=_params(32),
    )(w, g, m, v)


_SMALL = (("norm_g", D_MODEL), ("b_gate", 2 * D_MODEL), ("rel_bias", N_HEADS * N_REL_PAD), ("sgu_ln_g", D_B),
          ("sgu_ln_b", D_B), ("w_s", N_GROUPS * SGU_CHUNK * SGU_CHUNK), ("b_s", N_GROUPS * SGU_CHUNK),
          ("final_g", D_MODEL), ("loss", 1))


def _slab_rows(n):
    return -(-n // 1024) * 8


def _pack(parts):
    rows = []
    for (_, n), a in zip(_SMALL, parts):
        flat = a.reshape(-1).astype(F32)
        pad = _slab_rows(n) * 128 - flat.shape[0]
        rows.append(jnp.pad(flat, (0, pad)).reshape(-1, 128))
    return jnp.concatenate(rows, axis=0)


def _unpack(slab):
    out, r0 = {}, 0
    for name, n in _SMALL:
        nr = _slab_rows(n)
        out[name] = slab[r0:r0 + nr].reshape(-1)[:n]
        r0 += nr
    return out


def _pad_rel(a):
    return jnp.pad(a.reshape(N_HEADS, N_REL), ((0, 0), (0, N_REL_PAD - N_REL)))


def kernel(x, norm_g, w_in, b_gate, rel_bias, sgu_ln_g, sgu_ln_b, w_s, b_s, w_pa, w_pb, w_out, final_g, loss_target, m_norm_g, m_w_in, m_b_gate, m_rel_bias, m_sgu_ln_g, m_sgu_ln_b, m_w_s, m_b_s, m_w_pa, m_w_pb, m_w_out, m_final_g, v_norm_g, v_w_in, v_b_gate, v_rel_bias, v_sgu_ln_g, v_sgu_ln_b, v_w_s, v_b_s, v_w_pa, v_w_pb, v_w_out, v_final_g):
    s = x.shape[1]
    xs = x.reshape(s, D_MODEL)
    tgt = loss_target.reshape(s, D_MODEL)

    g_in, g_pa, g_pb, g_out = _gather_weights(w_in[0], w_pa[0], w_pb[0], w_out[0])
    w_in_full = jnp.transpose(g_in, (1, 0, 2)).reshape(D_MODEL, D_IN)
    w_pa_full = jnp.transpose(g_pa, (1, 0, 2)).reshape(D_A, D_MODEL)
    w_pb_full = jnp.transpose(g_pb, (1, 0, 2)).reshape(D_B, D_MODEL)
    w_out_full = g_out.reshape(D_MODEL, D_MODEL)

    bias_table = _bias_table(_pad_rel(rel_bias))
    qkv, rest, h_t = _proj_fwd(xs, norm_g, w_in_full)
    attn_out = _attn_fwd(qkv, bias_table)
    (dx2, d_attn, drest, dw_out, dw_pa, dw_pb, d_bgate, d_fg, d_lng, d_lnb, d_ws, d_bs, loss_part) = _mid_fwd_bwd(
        xs, tgt, attn_out, rest, w_pa_full, w_pb_full, w_out_full, b_gate, sgu_ln_g, sgu_ln_b, w_s[0],
        b_s.reshape(N_GROUPS, SGU_CHUNK, 1), final_g.reshape(1, D_MODEL))
    dqkv, dbias = _attn_bwd(qkv, bias_table, d_attn)
    d_rel = _bias_grad(dbias)
    grad_x, d_ng = _proj_bwd_x(dqkv, drest, xs, dx2, norm_g, w_in_full)
    dw_in = _proj_bwd_w(h_t, dqkv, drest)

    to_blocks = lambda a, rows: jnp.transpose(a.astype(BF16).reshape(rows, N_DEV, -1), (1, 0, 2))
    gw_in, gw_pa, gw_pb, gw_out = _reduce_grads(
        to_blocks(dw_in, D_MODEL), to_blocks(dw_pa, D_A), to_blocks(dw_pb, D_B),
        dw_out.astype(BF16).reshape(N_DEV, D_MODEL // N_DEV, D_MODEL))
    big = {}
    for name, w, g, m, v in (("w_in", w_in, gw_in, m_w_in, v_w_in), ("w_pa", w_pa, gw_pa, m_w_pa, v_w_pa),
                             ("w_pb", w_pb, gw_pb, m_w_pb, v_w_pb), ("w_out", w_out, gw_out, m_w_out, v_w_out)):
        d, nm, nv = _adamw("adamw_" + name, w[0], g, m[0], v[0])
        big[name] = (g[None], d[None], nm[None], nv[None])

    zero = jnp.zeros((1,), F32)
    g_slab = _reduce_small(_pack((d_ng, d_bgate, d_rel, d_lng, d_lnb, d_ws, d_bs, d_fg, loss_part)))
    small_in = lambda t: _pack(tuple(_pad_rel(a) if a.shape[-1] == N_REL else a for a in t) + (zero,))
    w_slab = small_in((norm_g, b_gate, rel_bias, sgu_ln_g, sgu_ln_b, w_s, b_s, final_g))
    m_slab = small_in((m_norm_g, m_b_gate, m_rel_bias, m_sgu_ln_g, m_sgu_ln_b, m_w_s, m_b_s, m_final_g))
    v_slab = small_in((v_norm_g, v_b_gate, v_rel_bias, v_sgu_ln_g, v_sgu_ln_b, v_w_s, v_b_s, v_final_g))
    d_slab, nm_slab, nv_slab = _adamw("adamw_small", w_slab, g_slab, m_slab, v_slab)
    small = [_unpack(t) for t in (g_slab, d_slab, nm_slab, nv_slab)]

    def leaf(kind, name, like):
        if name in big:
            return big[name][kind]
        a = small[kind][name]
        if name == "rel_bias":
            a = a.reshape(N_HEADS, N_REL_PAD)[:, :N_REL]
        return a.reshape(like.shape)

    weights = (("norm_g", norm_g), ("w_in", w_in), ("b_gate", b_gate), ("rel_bias", rel_bias), ("sgu_ln_g", sgu_ln_g),
               ("sgu_ln_b", sgu_ln_b), ("w_s", w_s), ("b_s", b_s), ("w_pa", w_pa), ("w_pb", w_pb), ("w_out", w_out),
               ("final_g", final_g))
    loss = small[0]["loss"].reshape(())
    outs = [loss, grad_x.reshape(x.shape)]
    for kind in range(4):
        outs.extend(leaf(kind, name, like) for name, like in weights)
    return tuple(outs)
```

```python
import functools
import math

import jax
import jax.numpy as jnp
from jax import lax
from jax.experimental import pallas as pl
from jax.experimental.pallas import tpu as pltpu

F32 = jnp.float32
BF16 = jnp.bfloat16
MESH = pl.DeviceIdType.MESH
N_DEV = 8

D_MODEL = 1024
D_A = 512
D_B = 512
D_IN = 5632
N_HEADS = 8
HEAD_DIM = 64
CHUNK = 64
N_PREV = 8
REL_CLIP = 128
N_REL = 2 * REL_CLIP + 1
N_REL_PAD = 384
SGU_CHUNK = 128
N_GROUPS = 4
EPS = 1e-6
NEG_INF = -1e30
Q_SCALE = HEAD_DIM ** -0.5

Q_BLOCK = 256
K_SPAN = 768
ROLL_W = 1024
COL_BLOCK = 512
N_COL_BLOCKS = D_IN // COL_BLOCK
REST = D_IN - 3 * D_A
TOKEN_TILE = 256
V7X_VMEM_BYTES = 64 * 1024 * 1024

ADAM_LR = 0.001
ADAM_B1 = 0.9
ADAM_B2 = 0.999
ADAM_EPS = 1e-08
ADAM_WD = 0.01
ADAM_STEP = 10

GELU_C = math.sqrt(2.0 / math.pi)
GELU_A = 0.044715

NT = (((1,), (1,)), ((), ()))
TN = (((0,), (0,)), ((), ()))
HIGHEST = lax.Precision.HIGHEST


def _params(vmem_mb, **kw):
    return pltpu.CompilerParams(vmem_limit_bytes=vmem_mb * 1024 * 1024, **kw)


def _dot(a, b, dims=None):
    if dims is None:
        return jnp.dot(a, b, preferred_element_type=F32)
    return lax.dot_general(a, b, dims, preferred_element_type=F32)


def _sigmoid(x):
    return 1.0 / (1.0 + jnp.exp(-x))


def _gelu_and_grad(u):
    u2 = u * u
    t = jnp.tanh(GELU_C * (u + GELU_A * u * u2))
    half = 0.5 * (1.0 + t)
    g = u * half
    dg = half + 0.5 * u * (1.0 - t * t) * (GELU_C * (1.0 + 3.0 * GELU_A * u2))
    return g, dg


def _my_pos():
    return lax.axis_index("x"), lax.axis_index("y"), lax.axis_index("c")


def _flat_id(pos):
    return 4 * pos[0] + 2 * pos[1] + pos[2]


def _peer(pos, k):
    x, y, c = pos
    return (1 - x if k & 4 else x, 1 - y if k & 2 else y, 1 - c if k & 1 else c)


def _other_chips(pos):
    x, y, _ = pos
    return ((1 - x, y), (x, 1 - y), (1 - x, 1 - y))


def _all_gather_slots(pos, bufs, send_sems, recv_sems):
    x, y, c = pos
    me, sib = (x, y, c), (x, y, 1 - c)
    chips = _other_chips(pos)
    n = len(bufs)

    def copy(a, k, block, to):
        slot = _flat_id(block)
        return pltpu.make_async_remote_copy(
            src_ref=bufs[a].at[slot], dst_ref=bufs[a].at[slot],
            send_sem=send_sems.at[a, k], recv_sem=recv_sems.at[a, k], device_id=to, device_id_type=MESH)

    first = [copy(a, 1 + j, me, (*chip, c)) for j, chip in enumerate(chips) for a in range(n)]
    first += [copy(a, 0, me, sib) for a in range(n)]
    for cp in first:
        cp.start()
    passed = []
    for j, chip in enumerate(chips):
        for a in range(n):
            copy(a, 1 + j, (*chip, c), me).wait_recv()
            cp = copy(a, 4 + j, (*chip, c), sib)
            cp.start()
            passed.append(cp)
    for a in range(n):
        copy(a, 0, sib, me).wait_recv()
        for j, chip in enumerate(chips):
            copy(a, 4 + j, (*chip, 1 - c), me).wait_recv()
    for cp in first + passed:
        cp.wait_send()


def _gather_weights(w_in, w_pa, w_pb, w_out):
    shards = (w_in, w_pa, w_pb, w_out)
    n = len(shards)

    def body(*refs):
        ins, outs = refs[:n], refs[n:2 * n]
        send_sems, recv_sems = refs[2 * n], refs[2 * n + 1]
        pos = _my_pos()
        me = _flat_id(pos)
        for src, dst in zip(ins, outs):
            dst[me] = src[...].astype(BF16)
        _all_gather_slots(pos, outs, send_sems, recv_sems)

    vmem = pl.BlockSpec(memory_space=pltpu.VMEM)
    return pl.pallas_call(
        body, name="gather_weights",
        out_shape=tuple(jax.ShapeDtypeStruct((N_DEV,) + s.shape, BF16) for s in shards),
        in_specs=[vmem] * n, out_specs=tuple([vmem] * n),
        scratch_shapes=[pltpu.SemaphoreType.DMA((n, N_DEV - 1)), pltpu.SemaphoreType.DMA((n, N_DEV - 1))],
        compiler_params=_params(40),
    )(*shards)


def _reduce_grads(g_in, g_pa, g_pb, g_out):
    parts = (g_in, g_pa, g_pb, g_out)
    n = len(parts)

    def body(*refs):
        ins, outs = refs[:n], refs[n:2 * n]
        from_sib, to_chip, from_chip = refs[2 * n:3 * n], refs[3 * n:4 * n], refs[4 * n:5 * n]
        send_sems, recv_sems = refs[5 * n], refs[5 * n + 1]
        x, y, c = _my_pos()
        sib = (x, y, 1 - c)
        chips = ((x, y),) + _other_chips((x, y, c))

        def to_sibling(a, r):
            return pltpu.make_async_remote_copy(
                src_ref=ins[a].at[_flat_id((*chips[r], 1 - c))], dst_ref=from_sib[a].at[r],
                send_sem=send_sems.at[a, r], recv_sem=recv_sems.at[a, r], device_id=sib, device_id_type=MESH)

        def to_owner(a, r):
            return pltpu.make_async_remote_copy(
                src_ref=to_chip[a].at[r - 1], dst_ref=from_chip[a].at[r - 1],
                send_sem=send_sems.at[a, 3 + r], recv_sem=recv_sems.at[a, 3 + r],
                device_id=(*chips[r], c), device_id_type=MESH)

        sends = [to_sibling(a, r) for r in (1, 2, 3, 0) for a in range(n)]
        for cp in sends:
            cp.start()
        for r in (1, 2, 3):
            for a in range(n):
                to_sibling(a, r).wait_recv()
                both = ins[a][_flat_id((*chips[r], c))].astype(F32) + from_sib[a][r].astype(F32)
                to_chip[a][r - 1] = both.astype(BF16)
                cp = to_owner(a, r)
                cp.start()
                sends.append(cp)
        for a in range(n):
            to_sibling(a, 0).wait_recv()
            acc = ins[a][_flat_id((x, y, c))].astype(F32) + from_sib[a][0].astype(F32)
            for r in (1, 2, 3):
                to_owner(a, r).wait_recv()
                acc = acc + from_chip[a][r - 1].astype(F32)
            outs[a][...] = acc
        for cp in sends:
            cp.wait_send()

    vmem = pl.BlockSpec(memory_space=pltpu.VMEM)
    blocks = lambda k: [pltpu.VMEM((k,) + p.shape[1:], BF16) for p in parts]
    return pl.pallas_call(
        body, name="reduce_grads",
        out_shape=tuple(jax.ShapeDtypeStruct(p.shape[1:], F32) for p in parts),
        in_specs=[vmem] * n, out_specs=tuple([vmem] * n),
        scratch_shapes=blocks(4) + blocks(3) + blocks(3)
        + [pltpu.SemaphoreType.DMA((n, N_DEV - 1)), pltpu.SemaphoreType.DMA((n, N_DEV - 1))],
        compiler_params=_params(56),
    )(*parts)


def _reduce_small(slab):
    def body(slab_ref, out_ref, land, send_sems, recv_sems):
        pos = _my_pos()
        land[_flat_id(pos)] = slab_ref[...]
        _all_gather_slots(pos, [land], send_sems, recv_sems)
        acc = land[0]
        for d in range(1, N_DEV):
            acc = acc + land[d]
        out_ref[...] = acc

    vmem = pl.BlockSpec(memory_space=pltpu.VMEM)
    return pl.pallas_call(
        body, name="reduce_small",
        out_shape=jax.ShapeDtypeStruct(slab.shape, F32),
        in_specs=[vmem], out_specs=vmem,
        scratch_shapes=[pltpu.VMEM((N_DEV,) + slab.shape, F32),
                        pltpu.SemaphoreType.DMA((1, N_DEV - 1)), pltpu.SemaphoreType.DMA((1, N_DEV - 1))],
        compiler_params=_params(16),
    )(slab)


def _rel_index(e):
    return jnp.where(e <= 384, 2 * REL_CLIP, jnp.where(e < 640, 640 - e, jnp.where(e <= K_SPAN, 0, 2 * REL_CLIP)))


def _bias_table(rel_bias_pad):
    def body(rb_ref, bt_ref):
        c = lax.broadcasted_iota(jnp.int32, (N_REL_PAD, ROLL_W), 1)
        r = lax.broadcasted_iota(jnp.int32, (N_REL_PAD, ROLL_W), 0)
        pick = (r == _rel_index(c)).astype(F32)
        rows = jnp.dot(rb_ref[...], pick, precision=HIGHEST, preferred_element_type=F32)
        qc = lax.broadcasted_iota(jnp.int32, (Q_BLOCK, K_SPAN), 0) >> 6
        kc = lax.broadcasted_iota(jnp.int32, (Q_BLOCK, K_SPAN), 1) >> 6
        band = (kc >= qc) & (kc <= qc + N_PREV)
        for h in range(N_HEADS):
            t = jnp.broadcast_to(rows[h:h + 1, :], (Q_BLOCK, ROLL_W))
            t = pltpu.roll(t, 0, 1, stride=1, stride_axis=0)
            bt_ref[h] = jnp.where(band, t[:, :K_SPAN], NEG_INF)

    return pl.pallas_call(
        body, name="bias_table",
        out_shape=jax.ShapeDtypeStruct((N_HEADS, Q_BLOCK, K_SPAN), F32),
        compiler_params=_params(32),
    )(rel_bias_pad)


def _bias_grad(dbias):
    def body(a_ref, o_ref):
        rr = lax.broadcasted_iota(jnp.int32, (Q_BLOCK, Q_BLOCK), 0)
        cc = lax.broadcasted_iota(jnp.int32, (Q_BLOCK, Q_BLOCK), 1)
        flip = (rr + cc == Q_BLOCK - 1).astype(F32)
        c = lax.broadcasted_iota(jnp.int32, (ROLL_W, N_REL_PAD), 0)
        r = lax.broadcasted_iota(jnp.int32, (ROLL_W, N_REL_PAD), 1)
        e = jnp.where(c >= Q_BLOCK - 1, c - (Q_BLOCK - 1), c + (ROLL_W - Q_BLOCK + 1))
        pick = (r == _rel_index(e)).astype(F32)
        sums = []
        for h in range(N_HEADS):
            a = jnp.dot(flip, a_ref[h], precision=HIGHEST, preferred_element_type=F32)
            a = jnp.concatenate([a, jnp.zeros((Q_BLOCK, ROLL_W - K_SPAN), F32)], axis=1)
            a = pltpu.roll(a, 0, 1, stride=1, stride_axis=0)
            sums.append(jnp.sum(a, axis=0, keepdims=True))
        diag = jnp.concatenate(sums, axis=0)
        o_ref[...] = jnp.dot(diag, pick, precision=HIGHEST, preferred_element_type=F32)

    return pl.pallas_call(
        body, name="bias_grad",
        out_shape=jax.ShapeDtypeStruct((N_HEADS, N_REL_PAD), F32),
        compiler_params=_params(32),
    )(dbias)


def _proj_fwd(x, norm_g, w_in):
    s = x.shape[0]
    tm = TOKEN_TILE

    def body(x_ref, g_ref, w_ref, qkv_ref, rest_ref, ht_ref):
        xf = x_ref[...]
        r = lax.rsqrt(jnp.mean(xf * xf, axis=-1, keepdims=True) + EPS)
        hf = xf * r * g_ref[...]
        h = hf.astype(BF16)
        ht_ref[...] = hf.T.astype(BF16)
        for c in range(N_COL_BLOCKS):
            blk = _dot(h, w_ref[:, c * COL_BLOCK:(c + 1) * COL_BLOCK])
            if c == 0:
                qkv_ref[0] = (blk * Q_SCALE).astype(BF16)
            elif c < 3:
                qkv_ref[c] = blk.astype(BF16)
            else:
                rest_ref[:, (c - 3) * COL_BLOCK:(c - 2) * COL_BLOCK] = blk

    return pl.pallas_call(
        body, name="proj_fwd",
        grid=(s // tm,),
        in_specs=[pl.BlockSpec((tm, D_MODEL), lambda i: (i, 0)),
                  pl.BlockSpec((1, D_MODEL), lambda i: (0, 0)),
                  pl.BlockSpec((D_MODEL, D_IN), lambda i: (0, 0))],
        out_specs=(pl.BlockSpec((3, tm, D_A), lambda i: (0, i, 0)),
                   pl.BlockSpec((tm, REST), lambda i: (i, 0)),
                   pl.BlockSpec((D_MODEL, tm), lambda i: (0, i))),
        out_shape=(jax.ShapeDtypeStruct((3, s, D_A), BF16),
                   jax.ShapeDtypeStruct((s, REST), F32),
                   jax.ShapeDtypeStruct((D_MODEL, s), BF16)),
        compiler_params=_params(56),
    )(x, norm_g, w_in)


def _attn_specs():
    def kv(which, back):
        return pl.BlockSpec((None, Q_BLOCK, 128), lambda p, b: (which, jnp.maximum(b - back, 0), p))
    return ([pl.BlockSpec((None, Q_BLOCK, 128), lambda p, b: (0, b, p))]
            + [kv(1, back) for back in (2, 1, 0)] + [kv(2, back) for back in (2, 1, 0)]
            + [pl.BlockSpec((2, Q_BLOCK, K_SPAN), lambda p, b: (p, 0, 0))])


def _head_masks():
    lane = lax.broadcasted_iota(jnp.int32, (1, 128), 1)
    first = lane < HEAD_DIM
    return (first, jnp.logical_not(first))


def _softmax_rows(qm, kcat, bias, valid):
    s = _dot(qm, kcat, NT) + bias
    s = jnp.where(valid, s, NEG_INF)
    m = jnp.max(s, axis=-1, keepdims=True)
    e = jnp.exp(s - m)
    return e * (1.0 / jnp.sum(e, axis=-1, keepdims=True))


def _attn_fwd(qkv, bias_table):
    s = qkv.shape[1]
    nb = s // Q_BLOCK

    def body(q_ref, k2_ref, k1_ref, k0_ref, v2_ref, v1_ref, v0_ref, bt_ref, o_ref):
        b = pl.program_id(1)
        q = q_ref[...]
        kcat = jnp.concatenate([k2_ref[...], k1_ref[...], k0_ref[...]], axis=0)
        vcat = jnp.concatenate([v2_ref[...], v1_ref[...], v0_ref[...]], axis=0)
        valid = lax.broadcasted_iota(jnp.int32, (1, K_SPAN), 1) >= (2 - b) * Q_BLOCK
        zero = jnp.zeros((), BF16)
        out = None
        for hh, mask in enumerate(_head_masks()):
            p = _softmax_rows(jnp.where(mask, q, zero), kcat, bt_ref[hh], valid)
            o = _dot(p.astype(BF16), jnp.where(mask, vcat, zero))
            out = o if out is None else out + o
        o_ref[...] = out

    return pl.pallas_call(
        body, name="attn_fwd",
        grid=(N_HEADS // 2, nb),
        in_specs=_attn_specs(),
        out_specs=pl.BlockSpec((Q_BLOCK, 128), lambda p, b: (b, p)),
        out_shape=jax.ShapeDtypeStruct((s, D_A), F32),
        compiler_params=_params(40),
    )(qkv, qkv, qkv, qkv, qkv, qkv, qkv, bias_table)


def _attn_bwd(qkv, bias_table, d_out):
    s = qkv.shape[1]
    nb = s // Q_BLOCK

    def body(q_ref, k2_ref, k1_ref, k0_ref, v2_ref, v1_ref, v0_ref, bt_ref, do_ref, dqkv_ref, db_ref, dk_acc, dv_acc):
        b = pl.program_id(1)

        @pl.when(b == 0)
        def _():
            dk_acc[...] = jnp.zeros(dk_acc.shape, F32)
            dv_acc[...] = jnp.zeros(dv_acc.shape, F32)
            db_ref[...] = jnp.zeros(db_ref.shape, F32)

        q = q_ref[...]
        do = do_ref[...]
        kcat = jnp.concatenate([k2_ref[...], k1_ref[...], k0_ref[...]], axis=0)
        vcat = jnp.concatenate([v2_ref[...], v1_ref[...], v0_ref[...]], axis=0)
        valid = lax.broadcasted_iota(jnp.int32, (1, K_SPAN), 1) >= (2 - b) * Q_BLOCK
        zero = jnp.zeros((), BF16)
        dq = dk = dv = None
        for hh, mask in enumerate(_head_masks()):
            qm = jnp.where(mask, q, zero)
            dom = jnp.where(mask, do, zero)
            p = _softmax_rows(qm, kcat, bt_ref[hh], valid)
            dp = _dot(dom, vcat, NT)
            ds = p * (dp - jnp.sum(p * dp, axis=-1, keepdims=True))
            db_ref[hh] += ds
            dsb = ds.astype(BF16)
            dq_h = _dot(dsb, jnp.where(mask, kcat, zero))
            dk_h = _dot(dsb, qm, TN)
            dv_h = _dot(p.astype(BF16), dom, TN)
            dq = dq_h if dq is None else dq + dq_h
            dk = dk_h if dk is None else dk + dk_h
            dv = dv_h if dv is None else dv + dv_h

        rows_b = pl.ds(pl.multiple_of(b * Q_BLOCK, Q_BLOCK), Q_BLOCK)
        dqkv_ref[0, rows_b, :] = (dq * Q_SCALE).astype(BF16)

        for j in range(3):
            slot = lax.rem(b + 1 + j, 3)
            part_k = dk[j * Q_BLOCK:(j + 1) * Q_BLOCK]
            part_v = dv[j * Q_BLOCK:(j + 1) * Q_BLOCK]
            if j == 2:
                dk_acc[slot] = part_k
                dv_acc[slot] = part_v
            else:
                dk_acc[slot] += part_k
                dv_acc[slot] += part_v

        def flush(block):
            slot = lax.rem(block + 3, 3)
            rows = pl.ds(pl.multiple_of(block * Q_BLOCK, Q_BLOCK), Q_BLOCK)
            dqkv_ref[1, rows, :] = dk_acc[slot].astype(BF16)
            dqkv_ref[2, rows, :] = dv_acc[slot].astype(BF16)

        @pl.when(b >= 2)
        def _():
            flush(b - 2)

        @pl.when(b == nb - 1)
        def _():
            flush(b - 1)
            flush(b)

    return pl.pallas_call(
        body, name="attn_bwd",
        grid=(N_HEADS // 2, nb),
        in_specs=_attn_specs() + [pl.BlockSpec((Q_BLOCK, 128), lambda p, b: (b, p))],
        out_specs=(pl.BlockSpec((3, s, 128), lambda p, b: (0, 0, p)),
                   pl.BlockSpec((2, Q_BLOCK, K_SPAN), lambda p, b: (p, 0, 0))),
        out_shape=(jax.ShapeDtypeStruct((3, s, D_A), BF16),
                   jax.ShapeDtypeStruct((N_HEADS, Q_BLOCK, K_SPAN), F32)),
        scratch_shapes=[pltpu.VMEM((3, Q_BLOCK, 128), F32), pltpu.VMEM((3, Q_BLOCK, 128), F32)],
        compiler_params=_params(48),
    )(qkv, qkv, qkv, qkv, qkv, qkv, qkv, bias_table, d_out)


def _mid_fwd_bwd(x, target, attn_out, rest, w_pa, w_pb, w_out, b_gate, ln_g, ln_b, w_s, b_s, final_g):
    s = x.shape[0]
    tm = TOKEN_TILE
    nt = s // tm
    n_sub = tm // SGU_CHUNK

    def body(x_ref, t_ref, oa_ref, z_ref, wpa_hbm, wpb_hbm, wout_hbm, bg_ref, lng_ref, lnb_ref, ws_ref, bs_ref, fg_ref,
             dx2_ref, doa_ref, dz_ref, dwout_hbm, dwpa_hbm, dwpb_hbm, dbg_ref, dfg_ref, dlng_ref, dlnb_ref, dws_ref,
             dbs_ref, loss_ref,
             wpa, wpb, wout, wmix, acc_out, acc_pa, acc_pb, mixed_s, dvn_s, sem):
        i = pl.program_id(0)

        @pl.when(i == 0)
        def _():
            loads = [pltpu.make_async_copy(src, dst, sem.at[n])
                     for n, (src, dst) in enumerate(((wpa_hbm, wpa), (wpb_hbm, wpb), (wout_hbm, wout)))]
            for cp in loads:
                cp.start()
            t_idx = lax.broadcasted_iota(jnp.int32, (SGU_CHUNK, SGU_CHUNK), 0)
            s_idx = lax.broadcasted_iota(jnp.int32, (SGU_CHUNK, SGU_CHUNK), 1)
            for g in range(N_GROUPS):
                wmix[g] = jnp.where(s_idx <= t_idx, ws_ref[g], 0.0).astype(BF16)
            for ref in (acc_out, acc_pa, acc_pb, dbg_ref, dfg_ref, dlng_ref, dlnb_ref, dws_ref, dbs_ref, loss_ref):
                ref[...] = jnp.zeros(ref.shape, F32)
            for cp in loads:
                cp.wait()

        g_a = z_ref[:, 0:512]
        u_b = z_ref[:, 512:1024]
        v_b = z_ref[:, 1024:1536]
        g_b = z_ref[:, 1536:2048]
        bg = bg_ref[...]

        sg_a = _sigmoid(g_a)
        silu_a = g_a * sg_a
        o_a = oa_ref[...]
        y_a = (o_a * silu_a).astype(BF16)

        ug, dgelu_u = _gelu_and_grad(u_b)
        vg, dgelu_v = _gelu_and_grad(v_b)
        mu = jnp.mean(vg, axis=-1, keepdims=True)
        vc = vg - mu
        rstd = lax.rsqrt(jnp.mean(vc * vc, axis=-1, keepdims=True) + EPS)
        vhat = vc * rstd
        lng = lng_ref[...]
        vn = (vhat * lng + lnb_ref[...]).astype(BF16)
        for n in range(n_sub):
            rows = slice(n * SGU_CHUNK, (n + 1) * SGU_CHUNK)
            for g in range(N_GROUPS):
                cols = slice(g * 128, (g + 1) * 128)
                mixed_s[rows, cols] = _dot(wmix[g], vn[rows, cols]) + bs_ref[g]
        mixed = mixed_s[...]
        sg_b = _sigmoid(g_b)
        silu_b = g_b * sg_b
        um = ug * mixed
        y_b = (um * silu_b).astype(BF16)

        p_a = _dot(y_a, wpa[...])
        p_b = _dot(y_b, wpb[...])
        gate_a = _sigmoid(z_ref[:, 2048:3072] + bg[:, :D_MODEL])
        gate_b = _sigmoid(z_ref[:, 3072:4096] + bg[:, D_MODEL:])
        merged = (gate_a * p_a + gate_b * p_b).astype(BF16)
        x2 = x_ref[...] + _dot(merged, wout[...])
        r2 = lax.rsqrt(jnp.mean(x2 * x2, axis=-1, keepdims=True) + EPS)
        xh = x2 * r2
        fg = fg_ref[...]
        err = xh * fg - t_ref[...]
        loss_ref[...] += jnp.sum(jnp.sum(err * err, axis=-1, keepdims=True), axis=0, keepdims=True) * (0.5 / D_MODEL)

        dy = err * (1.0 / D_MODEL)
        dfg_ref[...] += jnp.sum(dy * xh, axis=0, keepdims=True)
        gy = dy * fg
        dx2 = r2 * (gy - xh * jnp.mean(gy * xh, axis=-1, keepdims=True))
        dx2_ref[...] = dx2
        dx2b = dx2.astype(BF16)
        dmerged = _dot(dx2b, wout[...], NT)
        acc_out[...] += _dot(merged, dx2b, TN)

        dp_a = dmerged * gate_a
        dp_b = dmerged * gate_b
        dgate_a = dp_a * p_a * (1.0 - gate_a)
        dgate_b = dp_b * p_b * (1.0 - gate_b)
        dbg_ref[:, :D_MODEL] += jnp.sum(dgate_a, axis=0, keepdims=True)
        dbg_ref[:, D_MODEL:] += jnp.sum(dgate_b, axis=0, keepdims=True)
        dz_ref[:, 2048:3072] = dgate_a.astype(BF16)
        dz_ref[:, 3072:4096] = dgate_b.astype(BF16)
        dp_ab = dp_a.astype(BF16)
        dp_bb = dp_b.astype(BF16)
        dy_a = _dot(dp_ab, wpa[...], NT)
        dy_b = _dot(dp_bb, wpb[...], NT)
        acc_pa[...] += _dot(y_a, dp_ab, TN)
        acc_pb[...] += _dot(y_b, dp_bb, TN)

        doa_ref[...] = (dy_a * silu_a).astype(BF16)
        dz_ref[:, 0:512] = (dy_a * o_a * (sg_a * (1.0 + g_a * (1.0 - sg_a)))).astype(BF16)
        dz_ref[:, 1536:2048] = (dy_b * um * (sg_b * (1.0 + g_b * (1.0 - sg_b)))).astype(BF16)
        dys = dy_b * silu_b
        dz_ref[:, 512:1024] = (dys * mixed * dgelu_u).astype(BF16)
        dmixed = dys * ug
        dmb = dmixed.astype(BF16)
        for n in range(n_sub):
            rows = slice(n * SGU_CHUNK, (n + 1) * SGU_CHUNK)
            for g in range(N_GROUPS):
                cols = slice(g * 128, (g + 1) * 128)
                dws_ref[g] += _dot(dmb[rows, cols], vn[rows, cols], NT)
                dbs_ref[g] += jnp.sum(dmixed[rows, cols], axis=-1, keepdims=True)
                dvn_s[rows, cols] = _dot(wmix[g], dmb[rows, cols], TN)
        dvn = dvn_s[...]
        dlng_ref[...] += jnp.sum(dvn * vhat, axis=0, keepdims=True)
        dlnb_ref[...] += jnp.sum(dvn, axis=0, keepdims=True)
        dvh = dvn * lng
        dvg = rstd * (dvh - jnp.mean(dvh, axis=-1, keepdims=True) - vhat * jnp.mean(dvh * vhat, axis=-1, keepdims=True))
        dz_ref[:, 1024:1536] = (dvg * dgelu_v).astype(BF16)

        @pl.when(i == nt - 1)
        def _():
            t_idx = lax.broadcasted_iota(jnp.int32, (SGU_CHUNK, SGU_CHUNK), 0)
            s_idx = lax.broadcasted_iota(jnp.int32, (SGU_CHUNK, SGU_CHUNK), 1)
            for g in range(N_GROUPS):
                dws_ref[g] = jnp.where(s_idx <= t_idx, dws_ref[g], 0.0)
            stores = [pltpu.make_async_copy(src, dst, sem.at[n])
                      for n, (src, dst) in enumerate(((acc_out, dwout_hbm), (acc_pa, dwpa_hbm), (acc_pb, dwpb_hbm)))]
            for cp in stores:
                cp.start()
            for cp in stores:
                cp.wait()

    tile = lambda w: pl.BlockSpec((tm, w), lambda i: (i, 0))
    whole = lambda shape: pl.BlockSpec(shape, lambda i: (0,) * len(shape))
    hbm = pl.BlockSpec(memory_space=pl.ANY)
    return pl.pallas_call(
        body, name="mid_fwd_bwd",
        grid=(nt,),
        in_specs=[tile(D_MODEL), tile(D_MODEL), tile(D_A), tile(REST), hbm, hbm, hbm,
                  whole((1, 2 * D_MODEL)), whole((1, D_B)), whole((1, D_B)),
                  whole((N_GROUPS, SGU_CHUNK, SGU_CHUNK)), whole((N_GROUPS, SGU_CHUNK, 1)), whole((1, D_MODEL))],
        out_specs=(tile(D_MODEL), tile(D_A), tile(REST), hbm, hbm, hbm,
                   whole((1, 2 * D_MODEL)), whole((1, D_MODEL)), whole((1, D_B)), whole((1, D_B)),
                   whole((N_GROUPS, SGU_CHUNK, SGU_CHUNK)), whole((N_GROUPS, SGU_CHUNK, 1)), whole((1, 1))),
        out_shape=(jax.ShapeDtypeStruct((s, D_MODEL), F32), jax.ShapeDtypeStruct((s, D_A), BF16),
                   jax.ShapeDtypeStruct((s, REST), BF16),
                   jax.ShapeDtypeStruct((D_MODEL, D_MODEL), F32), jax.ShapeDtypeStruct((D_A, D_MODEL), F32),
                   jax.ShapeDtypeStruct((D_B, D_MODEL), F32),
                   jax.ShapeDtypeStruct((1, 2 * D_MODEL), F32), jax.ShapeDtypeStruct((1, D_MODEL), F32),
                   jax.ShapeDtypeStruct((1, D_B), F32), jax.ShapeDtypeStruct((1, D_B), F32),
                   jax.ShapeDtypeStruct((N_GROUPS, SGU_CHUNK, SGU_CHUNK), F32),
                   jax.ShapeDtypeStruct((N_GROUPS, SGU_CHUNK, 1), F32), jax.ShapeDtypeStruct((1, 1), F32)),
        scratch_shapes=[pltpu.VMEM((D_A, D_MODEL), BF16), pltpu.VMEM((D_B, D_MODEL), BF16),
                        pltpu.VMEM((D_MODEL, D_MODEL), BF16), pltpu.VMEM((N_GROUPS, SGU_CHUNK, SGU_CHUNK), BF16),
                        pltpu.VMEM((D_MODEL, D_MODEL), F32), pltpu.VMEM((D_A, D_MODEL), F32),
                        pltpu.VMEM((D_B, D_MODEL), F32),
                        pltpu.VMEM((tm, D_B), F32), pltpu.VMEM((tm, D_B), F32),
                        pltpu.SemaphoreType.DMA((3,))],
        compiler_params=_params(56),
    )(x, target, attn_out, rest, w_pa, w_pb, w_out, b_gate, ln_g, ln_b, w_s, b_s, final_g)


def _proj_bwd_x(dqkv, drest, x, dx2, norm_g, w_in):
    s = x.shape[0]
    tm = TOKEN_TILE

    def body(dqkv_ref, dr_ref, x_ref, dx2_ref, g_ref, w_hbm, dx_ref, dg_ref, w, sem):
        i = pl.program_id(0)

        @pl.when(i == 0)
        def _():
            cp = pltpu.make_async_copy(w_hbm, w, sem)
            cp.start()
            dg_ref[...] = jnp.zeros(dg_ref.shape, F32)
            cp.wait()

        dh = None
        for c in range(N_COL_BLOCKS):
            dz = dqkv_ref[c] if c < 3 else dr_ref[:, (c - 3) * COL_BLOCK:(c - 2) * COL_BLOCK]
            part = _dot(dz, w[:, c * COL_BLOCK:(c + 1) * COL_BLOCK], NT)
            dh = part if dh is None else dh + part
        xf = x_ref[...]
        r = lax.rsqrt(jnp.mean(xf * xf, axis=-1, keepdims=True) + EPS)
        xn = xf * r
        dg_ref[...] += jnp.sum(dh * xn, axis=0, keepdims=True)
        gh = dh * g_ref[...]
        dx_ref[...] = r * (gh - xn * jnp.mean(gh * xn, axis=-1, keepdims=True)) + dx2_ref[...]

    return pl.pallas_call(
        body, name="proj_bwd_x",
        grid=(s // tm,),
        in_specs=[pl.BlockSpec((3, tm, D_A), lambda i: (0, i, 0)),
                  pl.BlockSpec((tm, REST), lambda i: (i, 0)),
                  pl.BlockSpec((tm, D_MODEL), lambda i: (i, 0)),
                  pl.BlockSpec((tm, D_MODEL), lambda i: (i, 0)),
                  pl.BlockSpec((1, D_MODEL), lambda i: (0, 0)),
                  pl.BlockSpec(memory_space=pl.ANY)],
        out_specs=(pl.BlockSpec((tm, D_MODEL), lambda i: (i, 0)),
                   pl.BlockSpec((1, D_MODEL), lambda i: (0, 0))),
        out_shape=(jax.ShapeDtypeStruct((s, D_MODEL), F32), jax.ShapeDtypeStruct((1, D_MODEL), F32)),
        scratch_shapes=[pltpu.VMEM((D_MODEL, D_IN), BF16), pltpu.SemaphoreType.DMA],
        compiler_params=_params(48),
    )(dqkv, drest, x, dx2, norm_g, w_in)


def _proj_bwd_w(h_t, dqkv, drest):
    s = h_t.shape[1]
    tk = min(s, 1024)
    nk = s // tk

    def body(ht_ref, dqkv_ref, dr_ref, o_ref, acc):
        j = pl.program_id(0)
        i = pl.program_id(1)

        @pl.when(i == 0)
        def _():
            acc[...] = jnp.zeros(acc.shape, F32)

        @pl.when(j < 3)
        def _():
            acc[...] += _dot(ht_ref[...], dqkv_ref[...])

        @pl.when(j >= 3)
        def _():
            acc[...] += _dot(ht_ref[...], dr_ref[...])

        @pl.when(i == nk - 1)
        def _():
            o_ref[...] = acc[...].astype(BF16)

    return pl.pallas_call(
        body, name="proj_bwd_w",
        grid=(N_COL_BLOCKS, nk),
        in_specs=[pl.BlockSpec((D_MODEL, tk), lambda j, i: (0, i)),
                  pl.BlockSpec((None, tk, COL_BLOCK),
                               lambda j, i: (jnp.minimum(j, 2), jnp.where(j < 3, i, nk - 1), 0)),
                  pl.BlockSpec((tk, COL_BLOCK),
                               lambda j, i: (jnp.where(j >= 3, i, 0), jnp.maximum(j - 3, 0)))],
        out_specs=pl.BlockSpec((D_MODEL, COL_BLOCK), lambda j, i: (0, j)),
        out_shape=jax.ShapeDtypeStruct((D_MODEL, D_IN), BF16),
        scratch_shapes=[pltpu.VMEM((D_MODEL, COL_BLOCK), F32)],
        compiler_params=_params(40),
    )(h_t, dqkv, drest)


def _adamw(name, w, g, m, v):
    rows, cols = w.shape
    tr = rows if rows * cols <= 512 * 1024 else 256
    c1 = 1.0 - ADAM_B1 ** ADAM_STEP
    c2 = 1.0 - ADAM_B2 ** ADAM_STEP

    def body(w_ref, g_ref, m_ref, v_ref, d_ref, nm_ref, nv_ref):
        gg = g_ref[...]
        nm = ADAM_B1 * m_ref[...] + (1.0 - ADAM_B1) * gg
        nv = ADAM_B2 * v_ref[...] + (1.0 - ADAM_B2) * (gg * gg)
        d_ref[...] = -ADAM_LR * ((nm / c1) / (jnp.sqrt(nv / c2) + ADAM_EPS) + ADAM_WD * w_ref[...])
        nm_ref[...] = nm
        nv_ref[...] = nv

    spec = pl.BlockSpec((tr, cols), lambda i: (i, 0))
    shape = jax.ShapeDtypeStruct((rows, cols), F32)
    return pl.pallas_call(
        body, name=name,
        grid=(rows // tr,),
        in_specs=[spec] * 4, out_specs=(spec,) * 3, out_shape=(shape,) * 3,
        compiler_params=_params(32),
    )(w, g, m, v)


_SMALL = (("norm_g", D_MODEL), ("b_gate", 2 * D_MODEL), ("rel_bias", N_HEADS * N_REL_PAD), ("sgu_ln_g", D_B),
          ("sgu_ln_b", D_B), ("w_s", N_GROUPS * SGU_CHUNK * SGU_CHUNK), ("b_s", N_GROUPS * SGU_CHUNK),
          ("final_g", D_MODEL), ("loss", 1))


def _slab_rows(n):
    return -(-n // 1024) * 8


def _pack(parts):
    rows = []
    for (_, n), a in zip(_SMALL, parts):
        flat = a.reshape(-1).astype(F32)
        pad = _slab_rows(n) * 128 - flat.shape[0]
        rows.append(jnp.pad(flat, (0, pad)).reshape(-1, 128))
    return jnp.concatenate(rows, axis=0)


def _unpack(slab):
    out, r0 = {}, 0
    for name, n in _SMALL:
        nr = _slab_rows(n)
        out[name] = slab[r0:r0 + nr].reshape(-1)[:n]
        r0 += nr
    return out


def _pad_rel(a):
    return jnp.pad(a.reshape(N_HEADS, N_REL), ((0, 0), (0, N_REL_PAD - N_REL)))


def kernel(x, norm_g, w_in, b_gate, rel_bias, sgu_ln_g, sgu_ln_b, w_s, b_s, w_pa, w_pb, w_out, final_g, loss_target, m_norm_g, m_w_in, m_b_gate, m_rel_bias, m_sgu_ln_g, m_sgu_ln_b, m_w_s, m_b_s, m_w_pa, m_w_pb, m_w_out, m_final_g, v_norm_g, v_w_in, v_b_gate, v_rel_bias, v_sgu_ln_g, v_sgu_ln_b, v_w_s, v_b_s, v_w_pa, v_w_pb, v_w_out, v_final_g):
    s = x.shape[1]
    xs = x.reshape(s, D_MODEL)
    tgt = loss_target.reshape(s, D_MODEL)

    g_in, g_pa, g_pb, g_out = _gather_weights(w_in[0], w_pa[0], w_pb[0], w_out[0])
    w_in_full = jnp.transpose(g_in, (1, 0, 2)).reshape(D_MODEL, D_IN)
    w_pa_full = jnp.transpose(g_pa, (1, 0, 2)).reshape(D_A, D_MODEL)
    w_pb_full = jnp.transpose(g_pb, (1, 0, 2)).reshape(D_B, D_MODEL)
    w_out_full = g_out.reshape(D_MODEL, D_MODEL)

    bias_table = _bias_table(_pad_rel(rel_bias))
    qkv, rest, h_t = _proj_fwd(xs, norm_g, w_in_full)
    attn_out = _attn_fwd(qkv, bias_table)
    (dx2, d_attn, drest, dw_out, dw_pa, dw_pb, d_bgate, d_fg, d_lng, d_lnb, d_ws, d_bs, loss_part) = _mid_fwd_bwd(
        xs, tgt, attn_out, rest, w_pa_full, w_pb_full, w_out_full, b_gate, sgu_ln_g, sgu_ln_b, w_s[0],
        b_s.reshape(N_GROUPS, SGU_CHUNK, 1), final_g.reshape(1, D_MODEL))
    dqkv, dbias = _attn_bwd(qkv, bias_table, d_attn)
    d_rel = _bias_grad(dbias)
    grad_x, d_ng = _proj_bwd_x(dqkv, drest, xs, dx2, norm_g, w_in_full)
    dw_in = _proj_bwd_w(h_t, dqkv, drest)

    to_blocks = lambda a, rows: jnp.transpose(a.astype(BF16).reshape(rows, N_DEV, -1), (1, 0, 2))
    gw_in, gw_pa, gw_pb, gw_out = _reduce_grads(
        to_blocks(dw_in, D_MODEL), to_blocks(dw_pa, D_A), to_blocks(dw_pb, D_B),
        dw_out.astype(BF16).reshape(N_DEV, D_MODEL // N_DEV, D_MODEL))
    big = {}
    for name, w, g, m, v in (("w_in", w_in, gw_in, m_w_in, v_w_in), ("w_pa", w_pa, gw_pa, m_w_pa, v_w_pa),
                             ("w_pb", w_pb, gw_pb, m_w_pb, v_w_pb), ("w_out", w_out, gw_out, m_w_out, v_w_out)):
        d, nm, nv = _adamw("adamw_" + name, w[0], g, m[0], v[0])
        big[name] = (g[None], d[None], nm[None], nv[None])

    zero = jnp.zeros((1,), F32)
    g_slab = _reduce_small(_pack((d_ng, d_bgate, d_rel, d_lng, d_lnb, d_ws, d_bs, d_fg, loss_part)))
    small_in = lambda t: _pack(tuple(_pad_rel(a) if a.shape[-1] == N_REL else a for a in t) + (zero,))
    w_slab = small_in((norm_g, b_gate, rel_bias, sgu_ln_g, sgu_ln_b, w_s, b_s, final_g))
    m_slab = small_in((m_norm_g, m_b_gate, m_rel_bias, m_sgu_ln_g, m_sgu_ln_b, m_w_s, m_b_s, m_final_g))
    v_slab = small_in((v_norm_g, v_b_gate, v_rel_bias, v_sgu_ln_g, v_sgu_ln_b, v_w_s, v_b_s, v_final_g))
    d_slab, nm_slab, nv_slab = _adamw("adamw_small", w_slab, g_slab, m_slab, v_slab)
    small = [_unpack(t) for t in (g_slab, d_slab, nm_slab, nv_slab)]

    def leaf(kind, name, like):
        if name in big:
            return big[name][kind]
        a = small[kind][name]
        if name == "rel_bias":
            a = a.reshape(N_HEADS, N_REL_PAD)[:, :N_REL]
        return a.reshape(like.shape)

    weights = (("norm_g", norm_g), ("w_in", w_in), ("b_gate", b_gate), ("rel_bias", rel_bias), ("sgu_ln_g", sgu_ln_g),
               ("sgu_ln_b", sgu_ln_b), ("w_s", w_s), ("b_s", b_s), ("w_pa", w_pa), ("w_pb", w_pb), ("w_out", w_out),
               ("final_g", final_g))
    loss = small[0]["loss"].reshape(())
    outs = [loss, grad_x.reshape(x.shape)]
    for kind in range(4):
        outs.extend(leaf(kind, name, like) for name, like in weights)
    return tuple(outs)
```

```python
import functools
import math

import jax
import jax.numpy as jnp
from jax import lax
from jax.experimental import pallas as pl
from jax.experimental.pallas import tpu as pltpu

F32 = jnp.float32
BF16 = jnp.bfloat16
MESH = pl.DeviceIdType.MESH
N_DEV = 8

D_MODEL = 1024
D_A = 512
D_B = 512
D_IN = 5632
N_HEADS = 8
HEAD_DIM = 64
CHUNK = 64
N_PREV = 8
REL_CLIP = 128
N_REL = 2 * REL_CLIP + 1
N_REL_PAD = 384
SGU_CHUNK = 128
N_GROUPS = 4
EPS = 1e-6
NEG_INF = -1e30
Q_SCALE = HEAD_DIM ** -0.5

Q_BLOCK = 256
K_SPAN = 768
ROLL_W = 1024
COL_BLOCK = 512
N_COL_BLOCKS = D_IN // COL_BLOCK
REST = D_IN - 3 * D_A
TOKEN_TILE = 256
V7X_VMEM_BYTES = 64 * 1024 * 1024

ADAM_LR = 0.001
ADAM_B1 = 0.9
ADAM_B2 = 0.999
ADAM_EPS = 1e-08
ADAM_WD = 0.01
ADAM_STEP = 10

GELU_C = math.sqrt(2.0 / math.pi)
GELU_A = 0.044715

NT = (((1,), (1,)), ((), ()))
TN = (((0,), (0,)), ((), ()))
HIGHEST = lax.Precision.HIGHEST


def _params(vmem_mb, **kw):
    return pltpu.CompilerParams(vmem_limit_bytes=vmem_mb * 1024 * 1024, **kw)


def _dot(a, b, dims=None):
    if dims is None:
        return jnp.dot(a, b, preferred_element_type=F32)
    return lax.dot_general(a, b, dims, preferred_element_type=F32)


def _sigmoid(x):
    return 1.0 / (1.0 + jnp.exp(-x))


def _gelu_and_grad(u):
    u2 = u * u
    t = jnp.tanh(GELU_C * (u + GELU_A * u * u2))
    half = 0.5 * (1.0 + t)
    g = u * half
    dg = half + 0.5 * u * (1.0 - t * t) * (GELU_C * (1.0 + 3.0 * GELU_A * u2))
    return g, dg


def _my_pos():
    return lax.axis_index("x"), lax.axis_index("y"), lax.axis_index("c")


def _flat_id(pos):
    return 4 * pos[0] + 2 * pos[1] + pos[2]


def _peer(pos, k):
    x, y, c = pos
    return (1 - x if k & 4 else x, 1 - y if k & 2 else y, 1 - c if k & 1 else c)


def _other_chips(pos):
    x, y, _ = pos
    return ((1 - x, y), (x, 1 - y), (1 - x, 1 - y))


def _all_gather_slots(pos, bufs, send_sems, recv_sems):
    x, y, c = pos
    me, sib = (x, y, c), (x, y, 1 - c)
    chips = _other_chips(pos)
    n = len(bufs)

    def copy(a, k, block, to):
        slot = _flat_id(block)
        return pltpu.make_async_remote_copy(
            src_ref=bufs[a].at[slot], dst_ref=bufs[a].at[slot],
            send_sem=send_sems.at[a, k], recv_sem=recv_sems.at[a, k], device_id=to, device_id_type=MESH)

    first = [copy(a, 1 + j, me, (*chip, c)) for j, chip in enumerate(chips) for a in range(n)]
    first += [copy(a, 0, me, sib) for a in range(n)]
    for cp in first:
        cp.start()
    passed = []
    for j, chip in enumerate(chips):
        for a in range(n):
            copy(a, 1 + j, (*chip, c), me).wait_recv()
            cp = copy(a, 4 + j, (*chip, c), sib)
            cp.start()
            passed.append(cp)
    for a in range(n):
        copy(a, 0, sib, me).wait_recv()
        for j, chip in enumerate(chips):
            copy(a, 4 + j, (*chip, 1 - c), me).wait_recv()
    for cp in first + passed:
        cp.wait_send()


def _gather_weights(w_in, w_pa, w_pb, w_out):
    shards = (w_in, w_pa, w_pb, w_out)
    n = len(shards)

    def body(*refs):
        ins, outs = refs[:n], refs[n:2 * n]
        send_sems, recv_sems = refs[2 * n], refs[2 * n + 1]
        pos = _my_pos()
        me = _flat_id(pos)
        for src, dst in zip(ins, outs):
            dst[me] = src[...].astype(BF16)
        _all_gather_slots(pos, outs, send_sems, recv_sems)

    vmem = pl.BlockSpec(memory_space=pltpu.VMEM)
    return pl.pallas_call(
        body, name="gather_weights",
        out_shape=tuple(jax.ShapeDtypeStruct((N_DEV,) + s.shape, BF16) for s in shards),
        in_specs=[vmem] * n, out_specs=tuple([vmem] * n),
        scratch_shapes=[pltpu.SemaphoreType.DMA((n, N_DEV - 1)), pltpu.SemaphoreType.DMA((n, N_DEV - 1))],
        compiler_params=_params(40),
    )(*shards)


def _reduce_chip(name, parts):
    n = len(parts)

    def body(*refs):
        ins, own, to_chip, from_sib = refs[:n], refs[n:2 * n], refs[2 * n:3 * n], refs[3 * n:4 * n]
        send_sems, recv_sems = refs[4 * n], refs[4 * n + 1]
        x, y, c = _my_pos()
        sib = (x, y, 1 - c)
        chips = ((x, y),) + _other_chips((x, y, c))

        def to_sibling(a, r):
            return pltpu.make_async_remote_copy(
                src_ref=ins[a].at[_flat_id((*chips[r], 1 - c))], dst_ref=from_sib[a].at[r],
                send_sem=send_sems.at[a, r], recv_sem=recv_sems.at[a, r], device_id=sib, device_id_type=MESH)

        sends = [to_sibling(a, r) for r in (1, 2, 3, 0) for a in range(n)]
        for cp in sends:
            cp.start()
        for r in (1, 2, 3, 0):
            for a in range(n):
                to_sibling(a, r).wait_recv()
                both = ins[a][_flat_id((*chips[r], c))].astype(F32) + from_sib[a][r].astype(F32)
                if r == 0:
                    own[a][...] = both
                else:
                    to_chip[a][r - 1] = both.astype(BF16)
        for cp in sends:
            cp.wait_send()

    vmem = pl.BlockSpec(memory_space=pltpu.VMEM)
    return pl.pallas_call(
        body, name=name,
        out_shape=tuple(jax.ShapeDtypeStruct(p.shape[1:], F32) for p in parts)
        + tuple(jax.ShapeDtypeStruct((3,) + p.shape[1:], BF16) for p in parts),
        in_specs=[vmem] * n, out_specs=tuple([vmem] * (2 * n)),
        scratch_shapes=[pltpu.VMEM((4,) + p.shape[1:], BF16) for p in parts]
        + [pltpu.SemaphoreType.DMA((n, 4)), pltpu.SemaphoreType.DMA((n, 4))],
        compiler_params=_params(48),
    )(*parts)


def _owner_copies(to_chip, from_chip, send_sems, recv_sems):
    x, y, c = _my_pos()
    return [pltpu.make_async_remote_copy(
        src_ref=to_chip[a].at[j], dst_ref=from_chip[a].at[j],
        send_sem=send_sems.at[a, j], recv_sem=recv_sems.at[a, j], device_id=(*chip, c), device_id_type=MESH)
        for a in range(len(to_chip)) for j, chip in enumerate(_other_chips((x, y, c)))]


def _reduce_small(slab):
    def body(slab_ref, out_ref, land, send_sems, recv_sems):
        pos = _my_pos()
        land[_flat_id(pos)] = slab_ref[...]
        _all_gather_slots(pos, [land], send_sems, recv_sems)
        acc = land[0]
        for d in range(1, N_DEV):
            acc = acc + land[d]
        out_ref[...] = acc

    vmem = pl.BlockSpec(memory_space=pltpu.VMEM)
    return pl.pallas_call(
        body, name="reduce_small",
        out_shape=jax.ShapeDtypeStruct(slab.shape, F32),
        in_specs=[vmem], out_specs=vmem,
        scratch_shapes=[pltpu.VMEM((N_DEV,) + slab.shape, F32),
                        pltpu.SemaphoreType.DMA((1, N_DEV - 1)), pltpu.SemaphoreType.DMA((1, N_DEV - 1))],
        compiler_params=_params(16),
    )(slab)


def _rel_index(e):
    return jnp.where(e <= 384, 2 * REL_CLIP, jnp.where(e < 640, 640 - e, jnp.where(e <= K_SPAN, 0, 2 * REL_CLIP)))


def _bias_table(rel_bias_pad):
    def body(rb_ref, bt_ref):
        c = lax.broadcasted_iota(jnp.int32, (N_REL_PAD, ROLL_W), 1)
        r = lax.broadcasted_iota(jnp.int32, (N_REL_PAD, ROLL_W), 0)
        pick = (r == _rel_index(c)).astype(F32)
        rows = jnp.dot(rb_ref[...], pick, precision=HIGHEST, preferred_element_type=F32)
        qc = lax.broadcasted_iota(jnp.int32, (Q_BLOCK, K_SPAN), 0) >> 6
        kc = lax.broadcasted_iota(jnp.int32, (Q_BLOCK, K_SPAN), 1) >> 6
        band = (kc >= qc) & (kc <= qc + N_PREV)
        for h in range(N_HEADS):
            t = jnp.broadcast_to(rows[h:h + 1, :], (Q_BLOCK, ROLL_W))
            t = pltpu.roll(t, 0, 1, stride=1, stride_axis=0)
            bt_ref[h] = jnp.where(band, t[:, :K_SPAN], NEG_INF)

    return pl.pallas_call(
        body, name="bias_table",
        out_shape=jax.ShapeDtypeStruct((N_HEADS, Q_BLOCK, K_SPAN), F32),
        compiler_params=_params(32),
    )(rel_bias_pad)


def _bias_grad(dbias):
    def body(a_ref, o_ref):
        rr = lax.broadcasted_iota(jnp.int32, (Q_BLOCK, Q_BLOCK), 0)
        cc = lax.broadcasted_iota(jnp.int32, (Q_BLOCK, Q_BLOCK), 1)
        flip = (rr + cc == Q_BLOCK - 1).astype(F32)
        c = lax.broadcasted_iota(jnp.int32, (ROLL_W, N_REL_PAD), 0)
        r = lax.broadcasted_iota(jnp.int32, (ROLL_W, N_REL_PAD), 1)
        e = jnp.where(c >= Q_BLOCK - 1, c - (Q_BLOCK - 1), c + (ROLL_W - Q_BLOCK + 1))
        pick = (r == _rel_index(e)).astype(F32)
        sums = []
        for h in range(N_HEADS):
            a = jnp.dot(flip, a_ref[h], precision=HIGHEST, preferred_element_type=F32)
            a = jnp.concatenate([a, jnp.zeros((Q_BLOCK, ROLL_W - K_SPAN), F32)], axis=1)
            a = pltpu.roll(a, 0, 1, stride=1, stride_axis=0)
            sums.append(jnp.sum(a, axis=0, keepdims=True))
        diag = jnp.concatenate(sums, axis=0)
        o_ref[...] = jnp.dot(diag, pick, precision=HIGHEST, preferred_element_type=F32)

    return pl.pallas_call(
        body, name="bias_grad",
        out_shape=jax.ShapeDtypeStruct((N_HEADS, N_REL_PAD), F32),
        compiler_params=_params(32),
    )(dbias)


def _proj_fwd(x, norm_g, w_in):
    s = x.shape[0]
    tm = TOKEN_TILE

    def body(x_ref, g_ref, w_ref, qkv_ref, rest_ref, ht_ref):
        xf = x_ref[...]
        r = lax.rsqrt(jnp.mean(xf * xf, axis=-1, keepdims=True) + EPS)
        hf = xf * r * g_ref[...]
        h = hf.astype(BF16)
        ht_ref[...] = hf.T.astype(BF16)
        for c in range(N_COL_BLOCKS):
            blk = _dot(h, w_ref[:, c * COL_BLOCK:(c + 1) * COL_BLOCK])
            if c == 0:
                qkv_ref[0] = (blk * Q_SCALE).astype(BF16)
            elif c < 3:
                qkv_ref[c] = blk.astype(BF16)
            else:
                rest_ref[:, (c - 3) * COL_BLOCK:(c - 2) * COL_BLOCK] = blk

    return pl.pallas_call(
        body, name="proj_fwd",
        grid=(s // tm,),
        in_specs=[pl.BlockSpec((tm, D_MODEL), lambda i: (i, 0)),
                  pl.BlockSpec((1, D_MODEL), lambda i: (0, 0)),
                  pl.BlockSpec((D_MODEL, D_IN), lambda i: (0, 0))],
        out_specs=(pl.BlockSpec((3, tm, D_A), lambda i: (0, i, 0)),
                   pl.BlockSpec((tm, REST), lambda i: (i, 0)),
                   pl.BlockSpec((D_MODEL, tm), lambda i: (0, i))),
        out_shape=(jax.ShapeDtypeStruct((3, s, D_A), BF16),
                   jax.ShapeDtypeStruct((s, REST), F32),
                   jax.ShapeDtypeStruct((D_MODEL, s), BF16)),
        compiler_params=_params(56),
    )(x, norm_g, w_in)


def _attn_specs():
    def kv(which, back):
        return pl.BlockSpec((None, Q_BLOCK, 128), lambda p, b: (which, jnp.maximum(b - back, 0), p))
    return ([pl.BlockSpec((None, Q_BLOCK, 128), lambda p, b: (0, b, p))]
            + [kv(1, back) for back in (2, 1, 0)] + [kv(2, back) for back in (2, 1, 0)]
            + [pl.BlockSpec((2, Q_BLOCK, K_SPAN), lambda p, b: (p, 0, 0))])


def _head_masks():
    lane = lax.broadcasted_iota(jnp.int32, (1, 128), 1)
    first = lane < HEAD_DIM
    return (first, jnp.logical_not(first))


def _softmax_rows(qm, kcat, bias, valid):
    s = _dot(qm, kcat, NT) + bias
    s = jnp.where(valid, s, NEG_INF)
    m = jnp.max(s, axis=-1, keepdims=True)
    e = jnp.exp(s - m)
    return e * (1.0 / jnp.sum(e, axis=-1, keepdims=True))


def _attn_fwd(qkv, bias_table):
    s = qkv.shape[1]
    nb = s // Q_BLOCK

    def body(q_ref, k2_ref, k1_ref, k0_ref, v2_ref, v1_ref, v0_ref, bt_ref, o_ref):
        b = pl.program_id(1)
        q = q_ref[...]
        kcat = jnp.concatenate([k2_ref[...], k1_ref[...], k0_ref[...]], axis=0)
        vcat = jnp.concatenate([v2_ref[...], v1_ref[...], v0_ref[...]], axis=0)
        valid = lax.broadcasted_iota(jnp.int32, (1, K_SPAN), 1) >= (2 - b) * Q_BLOCK
        zero = jnp.zeros((), BF16)
        out = None
        for hh, mask in enumerate(_head_masks()):
            p = _softmax_rows(jnp.where(mask, q, zero), kcat, bt_ref[hh], valid)
            o = _dot(p.astype(BF16), jnp.where(mask, vcat, zero))
            out = o if out is None else out + o
        o_ref[...] = out

    return pl.pallas_call(
        body, name="attn_fwd",
        grid=(N_HEADS // 2, nb),
        in_specs=_attn_specs(),
        out_specs=pl.BlockSpec((Q_BLOCK, 128), lambda p, b: (b, p)),
        out_shape=jax.ShapeDtypeStruct((s, D_A), F32),
        compiler_params=_params(40),
    )(qkv, qkv, qkv, qkv, qkv, qkv, qkv, bias_table)


def _attn_bwd(qkv, bias_table, d_out, to_chip):
    s = qkv.shape[1]
    nb = s // Q_BLOCK
    n = len(to_chip)

    def body(*refs):
        q_ref, k2_ref, k1_ref, k0_ref, v2_ref, v1_ref, v0_ref, bt_ref, do_ref = refs[:9]
        to_chip_refs = refs[9:9 + n]
        dqkv_ref, db_ref = refs[9 + n:11 + n]
        from_chip_refs = refs[11 + n:11 + 2 * n]
        dk_acc, dv_acc, send_sems, recv_sems = refs[11 + 2 * n:]
        p_id = pl.program_id(0)
        b = pl.program_id(1)

        @pl.when((p_id == 0) & (b == 0))
        def _():
            for cp in _owner_copies(to_chip_refs, from_chip_refs, send_sems, recv_sems):
                cp.start()

        @pl.when(b == 0)
        def _():
            dk_acc[...] = jnp.zeros(dk_acc.shape, F32)
            dv_acc[...] = jnp.zeros(dv_acc.shape, F32)
            db_ref[...] = jnp.zeros(db_ref.shape, F32)

        q = q_ref[...]
        do = do_ref[...]
        kcat = jnp.concatenate([k2_ref[...], k1_ref[...], k0_ref[...]], axis=0)
        vcat = jnp.concatenate([v2_ref[...], v1_ref[...], v0_ref[...]], axis=0)
        valid = lax.broadcasted_iota(jnp.int32, (1, K_SPAN), 1) >= (2 - b) * Q_BLOCK
        zero = jnp.zeros((), BF16)
        dq = dk = dv = None
        for hh, mask in enumerate(_head_masks()):
            qm = jnp.where(mask, q, zero)
            dom = jnp.where(mask, do, zero)
            p = _softmax_rows(qm, kcat, bt_ref[hh], valid)
            dp = _dot(dom, vcat, NT)
            ds = p * (dp - jnp.sum(p * dp, axis=-1, keepdims=True))
            db_ref[hh] += ds
            dsb = ds.astype(BF16)
            dq_h = _dot(dsb, jnp.where(mask, kcat, zero))
            dk_h = _dot(dsb, qm, TN)
            dv_h = _dot(p.astype(BF16), dom, TN)
            dq = dq_h if dq is None else dq + dq_h
            dk = dk_h if dk is None else dk + dk_h
            dv = dv_h if dv is None else dv + dv_h

        rows_b = pl.ds(pl.multiple_of(b * Q_BLOCK, Q_BLOCK), Q_BLOCK)
        dqkv_ref[0, rows_b, :] = (dq * Q_SCALE).astype(BF16)

        for j in range(3):
            slot = lax.rem(b + 1 + j, 3)
            part_k = dk[j * Q_BLOCK:(j + 1) * Q_BLOCK]
            part_v = dv[j * Q_BLOCK:(j + 1) * Q_BLOCK]
            if j == 2:
                dk_acc[slot] = part_k
                dv_acc[slot] = part_v
            else:
                dk_acc[slot] += part_k
                dv_acc[slot] += part_v

        def flush(block):
            slot = lax.rem(block + 3, 3)
            rows = pl.ds(pl.multiple_of(block * Q_BLOCK, Q_BLOCK), Q_BLOCK)
            dqkv_ref[1, rows, :] = dk_acc[slot].astype(BF16)
            dqkv_ref[2, rows, :] = dv_acc[slot].astype(BF16)

        @pl.when(b >= 2)
        def _():
            flush(b - 2)

        @pl.when(b == nb - 1)
        def _():
            flush(b - 1)
            flush(b)

        @pl.when((p_id == N_HEADS // 2 - 1) & (b == nb - 1))
        def _():
            for cp in _owner_copies(to_chip_refs, from_chip_refs, send_sems, recv_sems):
                cp.wait_recv()
                cp.wait_send()

    hbm = pl.BlockSpec(memory_space=pl.ANY)
    return pl.pallas_call(
        body, name="attn_bwd",
        grid=(N_HEADS // 2, nb),
        in_specs=_attn_specs() + [pl.BlockSpec((Q_BLOCK, 128), lambda p, b: (b, p))] + [hbm] * n,
        out_specs=(pl.BlockSpec((3, s, 128), lambda p, b: (0, 0, p)),
                   pl.BlockSpec((2, Q_BLOCK, K_SPAN), lambda p, b: (p, 0, 0))) + (hbm,) * n,
        out_shape=(jax.ShapeDtypeStruct((3, s, D_A), BF16),
                   jax.ShapeDtypeStruct((N_HEADS, Q_BLOCK, K_SPAN), F32))
        + tuple(jax.ShapeDtypeStruct(t.shape, t.dtype) for t in to_chip),
        scratch_shapes=[pltpu.VMEM((3, Q_BLOCK, 128), F32), pltpu.VMEM((3, Q_BLOCK, 128), F32),
                        pltpu.SemaphoreType.DMA((n, 3)), pltpu.SemaphoreType.DMA((n, 3))],
        compiler_params=_params(48),
    )(qkv, qkv, qkv, qkv, qkv, qkv, qkv, bias_table, d_out, *to_chip)


def _mid_fwd_bwd(x, target, attn_out, rest, w_pa, w_pb, w_out, b_gate, ln_g, ln_b, w_s, b_s, final_g):
    s = x.shape[0]
    tm = TOKEN_TILE
    nt = s // tm
    n_sub = tm // SGU_CHUNK

    def body(x_ref, t_ref, oa_ref, z_ref, wpa_hbm, wpb_hbm, wout_hbm, bg_ref, lng_ref, lnb_ref, ws_ref, bs_ref, fg_ref,
             dx2_ref, doa_ref, dz_ref, dwout_hbm, dwpa_hbm, dwpb_hbm, dbg_ref, dfg_ref, dlng_ref, dlnb_ref, dws_ref,
             dbs_ref, loss_ref,
             wpa, wpb, wout, wmix, acc_out, acc_pa, acc_pb, mixed_s, dvn_s, sem):
        i = pl.program_id(0)

        @pl.when(i == 0)
        def _():
            loads = [pltpu.make_async_copy(src, dst, sem.at[n])
                     for n, (src, dst) in enumerate(((wpa_hbm, wpa), (wpb_hbm, wpb), (wout_hbm, wout)))]
            for cp in loads:
                cp.start()
            t_idx = lax.broadcasted_iota(jnp.int32, (SGU_CHUNK, SGU_CHUNK), 0)
            s_idx = lax.broadcasted_iota(jnp.int32, (SGU_CHUNK, SGU_CHUNK), 1)
            for g in range(N_GROUPS):
                wmix[g] = jnp.where(s_idx <= t_idx, ws_ref[g], 0.0).astype(BF16)
            for ref in (acc_out, acc_pa, acc_pb, dbg_ref, dfg_ref, dlng_ref, dlnb_ref, dws_ref, dbs_ref, loss_ref):
                ref[...] = jnp.zeros(ref.shape, F32)
            for cp in loads:
                cp.wait()

        g_a = z_ref[:, 0:512]
        u_b = z_ref[:, 512:1024]
        v_b = z_ref[:, 1024:1536]
        g_b = z_ref[:, 1536:2048]
        bg = bg_ref[...]

        sg_a = _sigmoid(g_a)
        silu_a = g_a * sg_a
        o_a = oa_ref[...]
        y_a = (o_a * silu_a).astype(BF16)

        ug, dgelu_u = _gelu_and_grad(u_b)
        vg, dgelu_v = _gelu_and_grad(v_b)
        mu = jnp.mean(vg, axis=-1, keepdims=True)
        vc = vg - mu
        rstd = lax.rsqrt(jnp.mean(vc * vc, axis=-1, keepdims=True) + EPS)
        vhat = vc * rstd
        lng = lng_ref[...]
        vn = (vhat * lng + lnb_ref[...]).astype(BF16)
        for n in range(n_sub):
            rows = slice(n * SGU_CHUNK, (n + 1) * SGU_CHUNK)
            for g in range(N_GROUPS):
                cols = slice(g * 128, (g + 1) * 128)
                mixed_s[rows, cols] = _dot(wmix[g], vn[rows, cols]) + bs_ref[g]
        mixed = mixed_s[...]
        sg_b = _sigmoid(g_b)
        silu_b = g_b * sg_b
        um = ug * mixed
        y_b = (um * silu_b).astype(BF16)

        p_a = _dot(y_a, wpa[...])
        p_b = _dot(y_b, wpb[...])
        gate_a = _sigmoid(z_ref[:, 2048:3072] + bg[:, :D_MODEL])
        gate_b = _sigmoid(z_ref[:, 3072:4096] + bg[:, D_MODEL:])
        merged = (gate_a * p_a + gate_b * p_b).astype(BF16)
        x2 = x_ref[...] + _dot(merged, wout[...])
        r2 = lax.rsqrt(jnp.mean(x2 * x2, axis=-1, keepdims=True) + EPS)
        xh = x2 * r2
        fg = fg_ref[...]
        err = xh * fg - t_ref[...]
        loss_ref[...] += jnp.sum(jnp.sum(err * err, axis=-1, keepdims=True), axis=0, keepdims=True) * (0.5 / D_MODEL)

        dy = err * (1.0 / D_MODEL)
        dfg_ref[...] += jnp.sum(dy * xh, axis=0, keepdims=True)
        gy = dy * fg
        dx2 = r2 * (gy - xh * jnp.mean(gy * xh, axis=-1, keepdims=True))
        dx2_ref[...] = dx2
        dx2b = dx2.astype(BF16)
        dmerged = _dot(dx2b, wout[...], NT)
        acc_out[...] += _dot(merged, dx2b, TN)

        dp_a = dmerged * gate_a
        dp_b = dmerged * gate_b
        dgate_a = dp_a * p_a * (1.0 - gate_a)
        dgate_b = dp_b * p_b * (1.0 - gate_b)
        dbg_ref[:, :D_MODEL] += jnp.sum(dgate_a, axis=0, keepdims=True)
        dbg_ref[:, D_MODEL:] += jnp.sum(dgate_b, axis=0, keepdims=True)
        dz_ref[:, 2048:3072] = dgate_a.astype(BF16)
        dz_ref[:, 3072:4096] = dgate_b.astype(BF16)
        dp_ab = dp_a.astype(BF16)
        dp_bb = dp_b.astype(BF16)
        dy_a = _dot(dp_ab, wpa[...], NT)
        dy_b = _dot(dp_bb, wpb[...], NT)
        acc_pa[...] += _dot(y_a, dp_ab, TN)
        acc_pb[...] += _dot(y_b, dp_bb, TN)

        doa_ref[...] = (dy_a * silu_a).astype(BF16)
        dz_ref[:, 0:512] = (dy_a * o_a * (sg_a * (1.0 + g_a * (1.0 - sg_a)))).astype(BF16)
        dz_ref[:, 1536:2048] = (dy_b * um * (sg_b * (1.0 + g_b * (1.0 - sg_b)))).astype(BF16)
        dys = dy_b * silu_b
        dz_ref[:, 512:1024] = (dys * mixed * dgelu_u).astype(BF16)
        dmixed = dys * ug
        dmb = dmixed.astype(BF16)
        for n in range(n_sub):
            rows = slice(n * SGU_CHUNK, (n + 1) * SGU_CHUNK)
            for g in range(N_GROUPS):
                cols = slice(g * 128, (g + 1) * 128)
                dws_ref[g] += _dot(dmb[rows, cols], vn[rows, cols], NT)
                dbs_ref[g] += jnp.sum(dmixed[rows, cols], axis=-1, keepdims=True)
                dvn_s[rows, cols] = _dot(wmix[g], dmb[rows, cols], TN)
        dvn = dvn_s[...]
        dlng_ref[...] += jnp.sum(dvn * vhat, axis=0, keepdims=True)
        dlnb_ref[...] += jnp.sum(dvn, axis=0, keepdims=True)
        dvh = dvn * lng
        dvg = rstd * (dvh - jnp.mean(dvh, axis=-1, keepdims=True) - vhat * jnp.mean(dvh * vhat, axis=-1, keepdims=True))
        dz_ref[:, 1024:1536] = (dvg * dgelu_v).astype(BF16)

        @pl.when(i == nt - 1)
        def _():
            t_idx = lax.broadcasted_iota(jnp.int32, (SGU_CHUNK, SGU_CHUNK), 0)
            s_idx = lax.broadcasted_iota(jnp.int32, (SGU_CHUNK, SGU_CHUNK), 1)
            for g in range(N_GROUPS):
                dws_ref[g] = jnp.where(s_idx <= t_idx, dws_ref[g], 0.0)
            stores = [pltpu.make_async_copy(src, dst, sem.at[n])
                      for n, (src, dst) in enumerate(((acc_out, dwout_hbm), (acc_pa, dwpa_hbm), (acc_pb, dwpb_hbm)))]
            for cp in stores:
                cp.start()
            for cp in stores:
                cp.wait()

    tile = lambda w: pl.BlockSpec((tm, w), lambda i: (i, 0))
    whole = lambda shape: pl.BlockSpec(shape, lambda i: (0,) * len(shape))
    hbm = pl.BlockSpec(memory_space=pl.ANY)
    return pl.pallas_call(
        body, name="mid_fwd_bwd",
        grid=(nt,),
        in_specs=[tile(D_MODEL), tile(D_MODEL), tile(D_A), tile(REST), hbm, hbm, hbm,
                  whole((1, 2 * D_MODEL)), whole((1, D_B)), whole((1, D_B)),
                  whole((N_GROUPS, SGU_CHUNK, SGU_CHUNK)), whole((N_GROUPS, SGU_CHUNK, 1)), whole((1, D_MODEL))],
        out_specs=(tile(D_MODEL), tile(D_A), tile(REST), hbm, hbm, hbm,
                   whole((1, 2 * D_MODEL)), whole((1, D_MODEL)), whole((1, D_B)), whole((1, D_B)),
                   whole((N_GROUPS, SGU_CHUNK, SGU_CHUNK)), whole((N_GROUPS, SGU_CHUNK, 1)), whole((1, 1))),
        out_shape=(jax.ShapeDtypeStruct((s, D_MODEL), F32), jax.ShapeDtypeStruct((s, D_A), BF16),
                   jax.ShapeDtypeStruct((s, REST), BF16),
                   jax.ShapeDtypeStruct((D_MODEL, D_MODEL), F32), jax.ShapeDtypeStruct((D_A, D_MODEL), F32),
                   jax.ShapeDtypeStruct((D_B, D_MODEL), F32),
                   jax.ShapeDtypeStruct((1, 2 * D_MODEL), F32), jax.ShapeDtypeStruct((1, D_MODEL), F32),
                   jax.ShapeDtypeStruct((1, D_B), F32), jax.ShapeDtypeStruct((1, D_B), F32),
                   jax.ShapeDtypeStruct((N_GROUPS, SGU_CHUNK, SGU_CHUNK), F32),
                   jax.ShapeDtypeStruct((N_GROUPS, SGU_CHUNK, 1), F32), jax.ShapeDtypeStruct((1, 1), F32)),
        scratch_shapes=[pltpu.VMEM((D_A, D_MODEL), BF16), pltpu.VMEM((D_B, D_MODEL), BF16),
                        pltpu.VMEM((D_MODEL, D_MODEL), BF16), pltpu.VMEM((N_GROUPS, SGU_CHUNK, SGU_CHUNK), BF16),
                        pltpu.VMEM((D_MODEL, D_MODEL), F32), pltpu.VMEM((D_A, D_MODEL), F32),
                        pltpu.VMEM((D_B, D_MODEL), F32),
                        pltpu.VMEM((tm, D_B), F32), pltpu.VMEM((tm, D_B), F32),
                        pltpu.SemaphoreType.DMA((3,))],
        compiler_params=_params(56),
    )(x, target, attn_out, rest, w_pa, w_pb, w_out, b_gate, ln_g, ln_b, w_s, b_s, final_g)


def _proj_bwd_x(dqkv, drest, x, dx2, norm_g, w_in, to_chip):
    s = x.shape[0]
    tm = TOKEN_TILE
    nt = s // tm
    n = len(to_chip)

    def body(*refs):
        dqkv_ref, dr_ref, x_ref, dx2_ref, g_ref, w_hbm = refs[:6]
        to_chip_refs = refs[6:6 + n]
        dx_ref, dg_ref = refs[6 + n:8 + n]
        from_chip_refs = refs[8 + n:8 + 2 * n]
        w, sem, send_sems, recv_sems = refs[8 + 2 * n:]
        i = pl.program_id(0)

        @pl.when(i == 0)
        def _():
            for rc in _owner_copies(to_chip_refs, from_chip_refs, send_sems, recv_sems):
                rc.start()
            cp = pltpu.make_async_copy(w_hbm, w, sem)
            cp.start()
            dg_ref[...] = jnp.zeros(dg_ref.shape, F32)
            cp.wait()

        dh = None
        for c in range(N_COL_BLOCKS):
            dz = dqkv_ref[c] if c < 3 else dr_ref[:, (c - 3) * COL_BLOCK:(c - 2) * COL_BLOCK]
            part = _dot(dz, w[:, c * COL_BLOCK:(c + 1) * COL_BLOCK], NT)
            dh = part if dh is None else dh + part
        xf = x_ref[...]
        r = lax.rsqrt(jnp.mean(xf * xf, axis=-1, keepdims=True) + EPS)
        xn = xf * r
        dg_ref[...] += jnp.sum(dh * xn, axis=0, keepdims=True)
        gh = dh * g_ref[...]
        dx_ref[...] = r * (gh - xn * jnp.mean(gh * xn, axis=-1, keepdims=True)) + dx2_ref[...]

        @pl.when(i == nt - 1)
        def _():
            for rc in _owner_copies(to_chip_refs, from_chip_refs, send_sems, recv_sems):
                rc.wait_recv()
                rc.wait_send()

    hbm = pl.BlockSpec(memory_space=pl.ANY)
    return pl.pallas_call(
        body, name="proj_bwd_x",
        grid=(nt,),
        in_specs=[pl.BlockSpec((3, tm, D_A), lambda i: (0, i, 0)),
                  pl.BlockSpec((tm, REST), lambda i: (i, 0)),
                  pl.BlockSpec((tm, D_MODEL), lambda i: (i, 0)),
                  pl.BlockSpec((tm, D_MODEL), lambda i: (i, 0)),
                  pl.BlockSpec((1, D_MODEL), lambda i: (0, 0)),
                  hbm] + [hbm] * n,
        out_specs=(pl.BlockSpec((tm, D_MODEL), lambda i: (i, 0)),
                   pl.BlockSpec((1, D_MODEL), lambda i: (0, 0))) + (hbm,) * n,
        out_shape=(jax.ShapeDtypeStruct((s, D_MODEL), F32), jax.ShapeDtypeStruct((1, D_MODEL), F32))
        + tuple(jax.ShapeDtypeStruct(t.shape, t.dtype) for t in to_chip),
        scratch_shapes=[pltpu.VMEM((D_MODEL, D_IN), BF16), pltpu.SemaphoreType.DMA,
                        pltpu.SemaphoreType.DMA((n, 3)), pltpu.SemaphoreType.DMA((n, 3))],
        compiler_params=_params(48),
    )(dqkv, drest, x, dx2, norm_g, w_in, *to_chip)


def _proj_bwd_w(h_t, dqkv, drest):
    s = h_t.shape[1]
    tk = min(s, 1024)
    nk = s // tk

    def body(ht_ref, dqkv_ref, dr_ref, o_ref, acc):
        j = pl.program_id(0)
        i = pl.program_id(1)

        @pl.when(i == 0)
        def _():
            acc[...] = jnp.zeros(acc.shape, F32)

        @pl.when(j < 3)
        def _():
            acc[...] += _dot(ht_ref[...], dqkv_ref[...])

        @pl.when(j >= 3)
        def _():
            acc[...] += _dot(ht_ref[...], dr_ref[...])

        @pl.when(i == nk - 1)
        def _():
            o_ref[...] = acc[...].astype(BF16)

    return pl.pallas_call(
        body, name="proj_bwd_w",
        grid=(N_COL_BLOCKS, nk),
        in_specs=[pl.BlockSpec((D_MODEL, tk), lambda j, i: (0, i)),
                  pl.BlockSpec((None, tk, COL_BLOCK),
                               lambda j, i: (jnp.minimum(j, 2), jnp.where(j < 3, i, nk - 1), 0)),
                  pl.BlockSpec((tk, COL_BLOCK),
                               lambda j, i: (jnp.where(j >= 3, i, 0), jnp.maximum(j - 3, 0)))],
        out_specs=pl.BlockSpec((D_MODEL, COL_BLOCK), lambda j, i: (0, j)),
        out_shape=jax.ShapeDtypeStruct((D_MODEL, D_IN), BF16),
        scratch_shapes=[pltpu.VMEM((D_MODEL, COL_BLOCK), F32)],
        compiler_params=_params(40),
    )(h_t, dqkv, drest)


def _adamw(name, w, g, m, v, from_chip=None):
    rows, cols = w.shape
    tr = rows if rows * cols <= 512 * 1024 else 256
    c1 = 1.0 - ADAM_B1 ** ADAM_STEP
    c2 = 1.0 - ADAM_B2 ** ADAM_STEP
    extra = [] if from_chip is None else [from_chip]

    def body(w_ref, g_ref, m_ref, v_ref, *rest):
        g_out, d_ref, nm_ref, nv_ref = rest[len(extra):]
        gg = g_ref[...]
        for t_ref in rest[:len(extra)]:
            for j in range(3):
                gg = gg + t_ref[j].astype(F32)
        nm = ADAM_B1 * m_ref[...] + (1.0 - ADAM_B1) * gg
        nv = ADAM_B2 * v_ref[...] + (1.0 - ADAM_B2) * (gg * gg)
        g_out[...] = gg
        d_ref[...] = -ADAM_LR * ((nm / c1) / (jnp.sqrt(nv / c2) + ADAM_EPS) + ADAM_WD * w_ref[...])
        nm_ref[...] = nm
        nv_ref[...] = nv

    spec = pl.BlockSpec((tr, cols), lambda i: (i, 0))
    shape = jax.ShapeDtypeStruct((rows, cols), F32)
    return pl.pallas_call(
        body, name=name,
        grid=(rows // tr,),
        in_specs=[spec] * 4 + [pl.BlockSpec((3, tr, cols), lambda i: (0, i, 0))] * len(extra),
        out_specs=(spec,) * 4, out_shape=(shape,) * 4,
        compiler_params=_params(32),
    )(w, g, m, v, *extra)


_SMALL = (("norm_g", D_MODEL), ("b_gate", 2 * D_MODEL), ("rel_bias", N_HEADS * N_REL_PAD), ("sgu_ln_g", D_B),
          ("sgu_ln_b", D_B), ("w_s", N_GROUPS * SGU_CHUNK * SGU_CHUNK), ("b_s", N_GROUPS * SGU_CHUNK),
          ("final_g", D_MODEL), ("loss", 1))


def _slab_rows(n):
    return -(-n // 1024) * 8


def _pack(parts):
    rows = []
    for (_, n), a in zip(_SMALL, parts):
        flat = a.reshape(-1).astype(F32)
        pad = _slab_rows(n) * 128 - flat.shape[0]
        rows.append(jnp.pad(flat, (0, pad)).reshape(-1, 128))
    return jnp.concatenate(rows, axis=0)


def _unpack(slab):
    out, r0 = {}, 0
    for name, n in _SMALL:
        nr = _slab_rows(n)
        out[name] = slab[r0:r0 + nr].reshape(-1)[:n]
        r0 += nr
    return out


def _pad_rel(a):
    return jnp.pad(a.reshape(N_HEADS, N_REL), ((0, 0), (0, N_REL_PAD - N_REL)))


def kernel(x, norm_g, w_in, b_gate, rel_bias, sgu_ln_g, sgu_ln_b, w_s, b_s, w_pa, w_pb, w_out, final_g, loss_target, m_norm_g, m_w_in, m_b_gate, m_rel_bias, m_sgu_ln_g, m_sgu_ln_b, m_w_s, m_b_s, m_w_pa, m_w_pb, m_w_out, m_final_g, v_norm_g, v_w_in, v_b_gate, v_rel_bias, v_sgu_ln_g, v_sgu_ln_b, v_w_s, v_b_s, v_w_pa, v_w_pb, v_w_out, v_final_g):
    s = x.shape[1]
    xs = x.reshape(s, D_MODEL)
    tgt = loss_target.reshape(s, D_MODEL)

    g_in, g_pa, g_pb, g_out = _gather_weights(w_in[0], w_pa[0], w_pb[0], w_out[0])
    w_in_full = jnp.transpose(g_in, (1, 0, 2)).reshape(D_MODEL, D_IN)
    w_pa_full = jnp.transpose(g_pa, (1, 0, 2)).reshape(D_A, D_MODEL)
    w_pb_full = jnp.transpose(g_pb, (1, 0, 2)).reshape(D_B, D_MODEL)
    w_out_full = g_out.reshape(D_MODEL, D_MODEL)

    bias_table = _bias_table(_pad_rel(rel_bias))
    qkv, rest, h_t = _proj_fwd(xs, norm_g, w_in_full)
    attn_out = _attn_fwd(qkv, bias_table)
    (dx2, d_attn, drest, dw_out, dw_pa, dw_pb, d_bgate, d_fg, d_lng, d_lnb, d_ws, d_bs, loss_part) = _mid_fwd_bwd(
        xs, tgt, attn_out, rest, w_pa_full, w_pb_full, w_out_full, b_gate, sgu_ln_g, sgu_ln_b, w_s[0],
        b_s.reshape(N_GROUPS, SGU_CHUNK, 1), final_g.reshape(1, D_MODEL))

    to_blocks = lambda a, rows: jnp.transpose(a.astype(BF16).reshape(rows, N_DEV, -1), (1, 0, 2))
    own_pa, own_pb, own_out, tc_pa, tc_pb, tc_out = _reduce_chip(
        "reduce_chip_proj", (to_blocks(dw_pa, D_A), to_blocks(dw_pb, D_B),
                             dw_out.astype(BF16).reshape(N_DEV, D_MODEL // N_DEV, D_MODEL)))
    dqkv, dbias, fc_pa, fc_pb, fc_out = _attn_bwd(qkv, bias_table, d_attn, (tc_pa, tc_pb, tc_out))
    d_rel = _bias_grad(dbias)
    dw_in = _proj_bwd_w(h_t, dqkv, drest)
    own_in, tc_in = _reduce_chip("reduce_chip_in", (to_blocks(dw_in, D_MODEL),))
    grad_x, d_ng, fc_in = _proj_bwd_x(dqkv, drest, xs, dx2, norm_g, w_in_full, (tc_in,))
    big = {}
    for name, w, g, fc, m, v in (("w_in", w_in, own_in, fc_in, m_w_in, v_w_in),
                                 ("w_pa", w_pa, own_pa, fc_pa, m_w_pa, v_w_pa),
                                 ("w_pb", w_pb, own_pb, fc_pb, m_w_pb, v_w_pb),
                                 ("w_out", w_out, own_out, fc_out, m_w_out, v_w_out)):
        big[name] = tuple(t[None] for t in _adamw("adamw_" + name, w[0], g, m[0], v[0], fc))

    zero = jnp.zeros((1,), F32)
    g_slab = _reduce_small(_pack((d_ng, d_bgate, d_rel, d_lng, d_lnb, d_ws, d_bs, d_fg, loss_part)))
    small_in = lambda t: _pack(tuple(_pad_rel(a) if a.shape[-1] == N_REL else a for a in t) + (zero,))
    w_slab = small_in((norm_g, b_gate, rel_bias, sgu_ln_g, sgu_ln_b, w_s, b_s, final_g))
    m_slab = small_in((m_norm_g, m_b_gate, m_rel_bias, m_sgu_ln_g, m_sgu_ln_b, m_w_s, m_b_s, m_final_g))
    v_slab = small_in((v_norm_g, v_b_gate, v_rel_bias, v_sgu_ln_g, v_sgu_ln_b, v_w_s, v_b_s, v_final_g))
    _, d_slab, nm_slab, nv_slab = _adamw("adamw_small", w_slab, g_slab, m_slab, v_slab)
    small = [_unpack(t) for t in (g_slab, d_slab, nm_slab, nv_slab)]

    def leaf(kind, name, like):
        if name in big:
            return big[name][kind]
        a = small[kind][name]
        if name == "rel_bias":
            a = a.reshape(N_HEADS, N_REL_PAD)[:, :N_REL]
        return a.reshape(like.shape)

    weights = (("norm_g", norm_g), ("w_in", w_in), ("b_gate", b_gate), ("rel_bias", rel_bias), ("sgu_ln_g", sgu_ln_g),
               ("sgu_ln_b", sgu_ln_b), ("w_s", w_s), ("b_s", b_s), ("w_pa", w_pa), ("w_pb", w_pb), ("w_out", w_out),
               ("final_g", final_g))
    loss = small[0]["loss"].reshape(())
    outs = [loss, grad_x.reshape(x.shape)]
    for kind in range(4):
        outs.extend(leaf(kind, name, like) for name, like in weights)
    return tuple(outs)
```

```python
import functools
import math

import jax
import jax.numpy as jnp
from jax import lax
from jax.experimental import pallas as pl
from jax.experimental.pallas import tpu as pltpu

F32 = jnp.float32
BF16 = jnp.bfloat16
MESH = pl.DeviceIdType.MESH
N_DEV = 8

D_MODEL = 1024
D_A = 512
D_B = 512
D_IN = 5632
N_HEADS = 8
HEAD_DIM = 64
CHUNK = 64
N_PREV = 8
REL_CLIP = 128
N_REL = 2 * REL_CLIP + 1
N_REL_PAD = 384
SGU_CHUNK = 128
N_GROUPS = 4
EPS = 1e-6
NEG_INF = -1e30
Q_SCALE = HEAD_DIM ** -0.5

Q_BLOCK = 256
K_SPAN = 768
ROLL_W = 1024
COL_BLOCK = 512
N_COL_BLOCKS = D_IN // COL_BLOCK
REST = D_IN - 3 * D_A
TOKEN_TILE = 256
V7X_VMEM_BYTES = 64 * 1024 * 1024

ADAM_LR = 0.001
ADAM_B1 = 0.9
ADAM_B2 = 0.999
ADAM_EPS = 1e-08
ADAM_WD = 0.01
ADAM_STEP = 10

GELU_C = math.sqrt(2.0 / math.pi)
GELU_A = 0.044715

NT = (((1,), (1,)), ((), ()))
TN = (((0,), (0,)), ((), ()))
HIGHEST = lax.Precision.HIGHEST


def _params(vmem_mb, **kw):
    return pltpu.CompilerParams(vmem_limit_bytes=vmem_mb * 1024 * 1024, **kw)


def _dot(a, b, dims=None):
    if dims is None:
        return jnp.dot(a, b, preferred_element_type=F32)
    return lax.dot_general(a, b, dims, preferred_element_type=F32)


def _sigmoid(x):
    return 1.0 / (1.0 + jnp.exp(-x))


def _gelu_and_grad(u):
    u2 = u * u
    t = jnp.tanh(GELU_C * (u + GELU_A * u * u2))
    half = 0.5 * (1.0 + t)
    g = u * half
    dg = half + 0.5 * u * (1.0 - t * t) * (GELU_C * (1.0 + 3.0 * GELU_A * u2))
    return g, dg


def _my_pos():
    return lax.axis_index("x"), lax.axis_index("y"), lax.axis_index("c")


def _flat_id(pos):
    return 4 * pos[0] + 2 * pos[1] + pos[2]


def _peer(pos, k):
    x, y, c = pos
    return (1 - x if k & 4 else x, 1 - y if k & 2 else y, 1 - c if k & 1 else c)


def _other_chips(pos):
    x, y, _ = pos
    return ((1 - x, y), (x, 1 - y), (1 - x, 1 - y))


def _all_gather_slots(pos, bufs, send_sems, recv_sems):
    x, y, c = pos
    me, sib = (x, y, c), (x, y, 1 - c)
    chips = _other_chips(pos)
    n = len(bufs)

    def copy(a, k, block, to):
        slot = _flat_id(block)
        return pltpu.make_async_remote_copy(
            src_ref=bufs[a].at[slot], dst_ref=bufs[a].at[slot],
            send_sem=send_sems.at[a, k], recv_sem=recv_sems.at[a, k], device_id=to, device_id_type=MESH)

    first = [copy(a, 1 + j, me, (*chip, c)) for j, chip in enumerate(chips) for a in range(n)]
    first += [copy(a, 0, me, sib) for a in range(n)]
    for cp in first:
        cp.start()
    passed = []
    for j, chip in enumerate(chips):
        for a in range(n):
            copy(a, 1 + j, (*chip, c), me).wait_recv()
            cp = copy(a, 4 + j, (*chip, c), sib)
            cp.start()
            passed.append(cp)
    for a in range(n):
        copy(a, 0, sib, me).wait_recv()
        for j, chip in enumerate(chips):
            copy(a, 4 + j, (*chip, 1 - c), me).wait_recv()
    for cp in first + passed:
        cp.wait_send()


def _reduce_chip(name, parts):
    n = len(parts)

    def body(*refs):
        ins, own, to_chip, from_sib = refs[:n], refs[n:2 * n], refs[2 * n:3 * n], refs[3 * n:4 * n]
        send_sems, recv_sems = refs[4 * n], refs[4 * n + 1]
        x, y, c = _my_pos()
        sib = (x, y, 1 - c)
        chips = ((x, y),) + _other_chips((x, y, c))

        def to_sibling(a, r):
            return pltpu.make_async_remote_copy(
                src_ref=ins[a].at[_flat_id((*chips[r], 1 - c))], dst_ref=from_sib[a].at[r],
                send_sem=send_sems.at[a, r], recv_sem=recv_sems.at[a, r], device_id=sib, device_id_type=MESH)

        sends = [to_sibling(a, r) for r in (1, 2, 3, 0) for a in range(n)]
        for cp in sends:
            cp.start()
        for r in (1, 2, 3, 0):
            for a in range(n):
                to_sibling(a, r).wait_recv()
                both = ins[a][_flat_id((*chips[r], c))].astype(F32) + from_sib[a][r].astype(F32)
                if r == 0:
                    own[a][...] = both
                else:
                    to_chip[a][r - 1] = both.astype(BF16)
        for cp in sends:
            cp.wait_send()

    vmem = pl.BlockSpec(memory_space=pltpu.VMEM)
    return pl.pallas_call(
        body, name=name,
        out_shape=tuple(jax.ShapeDtypeStruct(p.shape[1:], F32) for p in parts)
        + tuple(jax.ShapeDtypeStruct((3,) + p.shape[1:], BF16) for p in parts),
        in_specs=[vmem] * n, out_specs=tuple([vmem] * (2 * n)),
        scratch_shapes=[pltpu.VMEM((4,) + p.shape[1:], BF16) for p in parts]
        + [pltpu.SemaphoreType.DMA((n, 4)), pltpu.SemaphoreType.DMA((n, 4))],
        compiler_params=_params(48),
    )(*parts)


def _owner_copies(to_chip, from_chip, send_sems, recv_sems):
    x, y, c = _my_pos()
    return [pltpu.make_async_remote_copy(
        src_ref=to_chip[a].at[j], dst_ref=from_chip[a].at[j],
        send_sem=send_sems.at[a, j], recv_sem=recv_sems.at[a, j], device_id=(*chip, c), device_id_type=MESH)
        for a in range(len(to_chip)) for j, chip in enumerate(_other_chips((x, y, c)))]


def _reduce_small(slab):
    def body(slab_ref, out_ref, land, send_sems, recv_sems):
        pos = _my_pos()
        land[_flat_id(pos)] = slab_ref[...]
        _all_gather_slots(pos, [land], send_sems, recv_sems)
        acc = land[0]
        for d in range(1, N_DEV):
            acc = acc + land[d]
        out_ref[...] = acc

    vmem = pl.BlockSpec(memory_space=pltpu.VMEM)
    return pl.pallas_call(
        body, name="reduce_small",
        out_shape=jax.ShapeDtypeStruct(slab.shape, F32),
        in_specs=[vmem], out_specs=vmem,
        scratch_shapes=[pltpu.VMEM((N_DEV,) + slab.shape, F32),
                        pltpu.SemaphoreType.DMA((1, N_DEV - 1)), pltpu.SemaphoreType.DMA((1, N_DEV - 1))],
        compiler_params=_params(16),
    )(slab)


def _rel_index(e):
    return jnp.where(e <= 384, 2 * REL_CLIP, jnp.where(e < 640, 640 - e, jnp.where(e <= K_SPAN, 0, 2 * REL_CLIP)))


def _bias_table(rel_bias_pad):
    def body(rb_ref, bt_ref):
        c = lax.broadcasted_iota(jnp.int32, (N_REL_PAD, ROLL_W), 1)
        r = lax.broadcasted_iota(jnp.int32, (N_REL_PAD, ROLL_W), 0)
        pick = (r == _rel_index(c)).astype(F32)
        rows = jnp.dot(rb_ref[...], pick, precision=HIGHEST, preferred_element_type=F32)
        qc = lax.broadcasted_iota(jnp.int32, (Q_BLOCK, K_SPAN), 0) >> 6
        kc = lax.broadcasted_iota(jnp.int32, (Q_BLOCK, K_SPAN), 1) >> 6
        band = (kc >= qc) & (kc <= qc + N_PREV)
        for h in range(N_HEADS):
            t = jnp.broadcast_to(rows[h:h + 1, :], (Q_BLOCK, ROLL_W))
            t = pltpu.roll(t, 0, 1, stride=1, stride_axis=0)
            bt_ref[h] = jnp.where(band, t[:, :K_SPAN], NEG_INF)

    return pl.pallas_call(
        body, name="bias_table",
        out_shape=jax.ShapeDtypeStruct((N_HEADS, Q_BLOCK, K_SPAN), F32),
        compiler_params=_params(32),
    )(rel_bias_pad)


def _bias_grad(dbias):
    def body(a_ref, o_ref):
        rr = lax.broadcasted_iota(jnp.int32, (Q_BLOCK, Q_BLOCK), 0)
        cc = lax.broadcasted_iota(jnp.int32, (Q_BLOCK, Q_BLOCK), 1)
        flip = (rr + cc == Q_BLOCK - 1).astype(F32)
        c = lax.broadcasted_iota(jnp.int32, (ROLL_W, N_REL_PAD), 0)
        r = lax.broadcasted_iota(jnp.int32, (ROLL_W, N_REL_PAD), 1)
        e = jnp.where(c >= Q_BLOCK - 1, c - (Q_BLOCK - 1), c + (ROLL_W - Q_BLOCK + 1))
        pick = (r == _rel_index(e)).astype(F32)
        sums = []
        for h in range(N_HEADS):
            a = jnp.dot(flip, a_ref[h], precision=HIGHEST, preferred_element_type=F32)
            a = jnp.concatenate([a, jnp.zeros((Q_BLOCK, ROLL_W - K_SPAN), F32)], axis=1)
            a = pltpu.roll(a, 0, 1, stride=1, stride_axis=0)
            sums.append(jnp.sum(a, axis=0, keepdims=True))
        diag = jnp.concatenate(sums, axis=0)
        o_ref[...] = jnp.dot(diag, pick, precision=HIGHEST, preferred_element_type=F32)

    return pl.pallas_call(
        body, name="bias_grad",
        out_shape=jax.ShapeDtypeStruct((N_HEADS, N_REL_PAD), F32),
        compiler_params=_params(32),
    )(dbias)


def _gather_proj_fwd(x, norm_g, w_in, w_pa, w_pb, w_out):
    s = x.shape[0]
    tm = 512 if s % 512 == 0 else TOKEN_TILE
    nt = s // tm
    shard_w = w_in.shape[1]
    chip_w = 2 * shard_w
    n_chips = N_DEV // 2
    small = (w_pa, w_pb, w_out)

    def body(order_ref, x_ref, g_ref, win_hbm, wpa_ref, wpb_ref, wout_ref,
             z_ref, ht_ref, wnat_ref, gpa_hbm, gpb_hbm, gout_hbm,
             slots, hb, win_f32, st_pa, st_pb, st_out, send_sems, recv_sems, local_sems):
        j = pl.program_id(0)
        i = pl.program_id(1)
        x_, y_, c_ = _my_pos()
        me, sib = (x_, y_, c_), (x_, y_, 1 - c_)
        chips = _other_chips(me)
        bufs = (slots, gpa_hbm, gpb_hbm, gout_hbm)
        stages = (None, st_pa, st_pb, st_out)

        def copy(a, k, block, to):
            slot = _flat_id(block)
            own = stages[a] is not None and k < 4
            return pltpu.make_async_remote_copy(
                src_ref=stages[a] if own else bufs[a].at[slot], dst_ref=bufs[a].at[slot],
                send_sem=send_sems.at[a, k], recv_sem=recv_sems.at[a, k], device_id=to, device_id_type=MESH)

        def own_sends(a):
            return [copy(a, 0, me, sib)] + [copy(a, 1 + jj, me, (*chip, c_)) for jj, chip in enumerate(chips)]

        def keep_own(a):
            return pltpu.make_async_copy(stages[a], bufs[a].at[_flat_id(me)], local_sems.at[a])

        def assemble(chip):
            first = 2 * (2 * chip[0] + chip[1])
            wnat_ref[:, :shard_w] = slots[first]
            wnat_ref[:, shard_w:] = slots[first + 1]

        @pl.when((j == 0) & (i == 0))
        def _():
            load = pltpu.make_async_copy(win_hbm, win_f32, local_sems.at[0])
            load.start()
            for src, dst in zip((wpa_ref, wpb_ref, wout_ref), (st_pa, st_pb, st_out)):
                dst[...] = src[...].astype(BF16)
            load.wait()
            slots[_flat_id(me)] = win_f32[...].astype(BF16)
            for a in range(4):
                for cp in own_sends(a):
                    cp.start()
            for a in range(1, 4):
                keep_own(a).start()
            copy(0, 0, sib, me).wait_recv()
            assemble((x_, y_))

        for jj, chip in enumerate(chips):
            @pl.when((j == jj + 1) & (i == 0))
            def _(jj=jj, chip=chip):
                copy(0, 1 + jj, (*chip, c_), me).wait_recv()
                copy(0, 4 + jj, (*chip, c_), sib).start()
                copy(0, 4 + jj, (*chip, 1 - c_), me).wait_recv()
                assemble(chip)
                for a in range(1, 4):
                    copy(a, 1 + jj, (*chip, c_), me).wait_recv()
                    copy(a, 4 + jj, (*chip, c_), sib).start()

        rows = pl.ds(pl.multiple_of(i * tm, tm), tm)

        @pl.when(j == 0)
        def _():
            xf = x_ref[...]
            r = lax.rsqrt(jnp.mean(xf * xf, axis=-1, keepdims=True) + EPS)
            hf = xf * r * g_ref[...]
            hb[rows, :] = hf.astype(BF16)
            ht_ref[...] = hf.T.astype(BF16)

        blk = _dot(hb[rows, :], wnat_ref[...])
        q_scale = jnp.where(order_ref[j] == 0, Q_SCALE, 1.0).astype(F32)
        z_ref[:, :D_A] = (blk[:, :D_A] * q_scale).astype(BF16)
        z_ref[:, D_A:] = blk[:, D_A:].astype(BF16)

        @pl.when((j == n_chips - 1) & (i == nt - 1))
        def _():
            for a in range(1, 4):
                copy(a, 0, sib, me).wait_recv()
                for jj, chip in enumerate(chips):
                    copy(a, 4 + jj, (*chip, 1 - c_), me).wait_recv()
                keep_own(a).wait()
            for a in range(4):
                for cp in own_sends(a):
                    cp.wait_send()
                for jj, chip in enumerate(chips):
                    copy(a, 4 + jj, (*chip, c_), sib).wait_send()

    x_, y_, _ = _my_pos()
    order = jnp.stack([2 * cx + cy for cx, cy in ((x_, y_),) + _other_chips((x_, y_, 0))]).astype(jnp.int32)
    hbm = pl.BlockSpec(memory_space=pl.ANY)
    first_pass = lambda j, i: jnp.where(j == 0, i, nt - 1)
    whole = lambda a: pl.BlockSpec(a.shape, lambda j, i, o: (0,) * a.ndim)
    grid_spec = pltpu.PrefetchScalarGridSpec(
        num_scalar_prefetch=1,
        grid=(n_chips, nt),
        in_specs=[pl.BlockSpec((tm, D_MODEL), lambda j, i, o: (first_pass(j, i), 0)),
                  whole(norm_g), hbm, whole(w_pa), whole(w_pb), whole(w_out)],
        out_specs=(pl.BlockSpec((tm, chip_w), lambda j, i, o: (i, o[j])),
                   pl.BlockSpec((D_MODEL, tm), lambda j, i, o: (0, first_pass(j, i))),
                   pl.BlockSpec((D_MODEL, chip_w), lambda j, i, o: (0, o[j])),
                   hbm, hbm, hbm),
        scratch_shapes=[pltpu.VMEM((N_DEV, D_MODEL, shard_w), BF16), pltpu.VMEM((s, D_MODEL), BF16),
                        pltpu.VMEM(w_in.shape, F32)]
        + [pltpu.VMEM(a.shape, BF16) for a in small]
        + [pltpu.SemaphoreType.DMA((4, N_DEV - 1)), pltpu.SemaphoreType.DMA((4, N_DEV - 1)),
           pltpu.SemaphoreType.DMA((4,))])
    return pl.pallas_call(
        body, name="gather_proj_fwd",
        grid_spec=grid_spec,
        out_shape=(jax.ShapeDtypeStruct((s, D_IN), BF16), jax.ShapeDtypeStruct((D_MODEL, s), BF16),
                   jax.ShapeDtypeStruct((D_MODEL, D_IN), BF16))
        + tuple(jax.ShapeDtypeStruct((N_DEV,) + a.shape, BF16) for a in small),
        compiler_params=_params(60),
    )(order, x, norm_g, w_in, w_pa, w_pb, w_out)


def _attn_specs():
    pairs = N_HEADS // 2

    def kv(which, back):
        return pl.BlockSpec((Q_BLOCK, 128), lambda p, b: (jnp.maximum(b - back, 0), which * pairs + p))
    return ([pl.BlockSpec((Q_BLOCK, 128), lambda p, b: (b, p))]
            + [kv(1, back) for back in (2, 1, 0)] + [kv(2, back) for back in (2, 1, 0)]
            + [pl.BlockSpec((2, Q_BLOCK, K_SPAN), lambda p, b: (p, 0, 0))])


def _head_masks():
    lane = lax.broadcasted_iota(jnp.int32, (1, 128), 1)
    first = lane < HEAD_DIM
    return (first, jnp.logical_not(first))


def _softmax_rows(qm, kcat, bias, valid):
    s = _dot(qm, kcat, NT) + bias
    s = jnp.where(valid, s, NEG_INF)
    m = jnp.max(s, axis=-1, keepdims=True)
    e = jnp.exp(s - m)
    return e * (1.0 / jnp.sum(e, axis=-1, keepdims=True))


def _attn_fwd(qkv, bias_table):
    s = qkv.shape[0]
    nb = s // Q_BLOCK

    def body(q_ref, k2_ref, k1_ref, k0_ref, v2_ref, v1_ref, v0_ref, bt_ref, o_ref):
        b = pl.program_id(1)
        q = q_ref[...]
        kcat = jnp.concatenate([k2_ref[...], k1_ref[...], k0_ref[...]], axis=0)
        vcat = jnp.concatenate([v2_ref[...], v1_ref[...], v0_ref[...]], axis=0)
        valid = lax.broadcasted_iota(jnp.int32, (1, K_SPAN), 1) >= (2 - b) * Q_BLOCK
        zero = jnp.zeros((), BF16)
        out = None
        for hh, mask in enumerate(_head_masks()):
            p = _softmax_rows(jnp.where(mask, q, zero), kcat, bt_ref[hh], valid)
            o = _dot(p.astype(BF16), jnp.where(mask, vcat, zero))
            out = o if out is None else out + o
        o_ref[...] = out

    return pl.pallas_call(
        body, name="attn_fwd",
        grid=(N_HEADS // 2, nb),
        in_specs=_attn_specs(),
        out_specs=pl.BlockSpec((Q_BLOCK, 128), lambda p, b: (b, p)),
        out_shape=jax.ShapeDtypeStruct((s, D_A), F32),
        compiler_params=_params(40),
    )(qkv, qkv, qkv, qkv, qkv, qkv, qkv, bias_table)


def _attn_bwd(qkv, bias_table, d_out, to_chip):
    s = qkv.shape[0]
    nb = s // Q_BLOCK
    n = len(to_chip)

    def body(*refs):
        q_ref, k2_ref, k1_ref, k0_ref, v2_ref, v1_ref, v0_ref, bt_ref, do_ref = refs[:9]
        to_chip_refs = refs[9:9 + n]
        dqkv_ref, db_ref = refs[9 + n:11 + n]
        from_chip_refs = refs[11 + n:11 + 2 * n]
        dk_acc, dv_acc, send_sems, recv_sems = refs[11 + 2 * n:]
        p_id = pl.program_id(0)
        b = pl.program_id(1)

        @pl.when((p_id == 0) & (b == 0))
        def _():
            for cp in _owner_copies(to_chip_refs, from_chip_refs, send_sems, recv_sems):
                cp.start()

        @pl.when(b == 0)
        def _():
            dk_acc[...] = jnp.zeros(dk_acc.shape, F32)
            dv_acc[...] = jnp.zeros(dv_acc.shape, F32)
            db_ref[...] = jnp.zeros(db_ref.shape, F32)

        q = q_ref[...]
        do = do_ref[...]
        kcat = jnp.concatenate([k2_ref[...], k1_ref[...], k0_ref[...]], axis=0)
        vcat = jnp.concatenate([v2_ref[...], v1_ref[...], v0_ref[...]], axis=0)
        valid = lax.broadcasted_iota(jnp.int32, (1, K_SPAN), 1) >= (2 - b) * Q_BLOCK
        zero = jnp.zeros((), BF16)
        dq = dk = dv = None
        for hh, mask in enumerate(_head_masks()):
            qm = jnp.where(mask, q, zero)
            dom = jnp.where(mask, do, zero)
            p = _softmax_rows(qm, kcat, bt_ref[hh], valid)
            dp = _dot(dom, vcat, NT)
            ds = p * (dp - jnp.sum(p * dp, axis=-1, keepdims=True))
            db_ref[hh] += ds
            dsb = ds.astype(BF16)
            dq_h = _dot(dsb, jnp.where(mask, kcat, zero))
            dk_h = _dot(dsb, qm, TN)
            dv_h = _dot(p.astype(BF16), dom, TN)
            dq = dq_h if dq is None else dq + dq_h
            dk = dk_h if dk is None else dk + dk_h
            dv = dv_h if dv is None else dv + dv_h

        rows_b = pl.ds(pl.multiple_of(b * Q_BLOCK, Q_BLOCK), Q_BLOCK)
        dqkv_ref[0, rows_b, :] = (dq * Q_SCALE).astype(BF16)

        for j in range(3):
            slot = lax.rem(b + 1 + j, 3)
            part_k = dk[j * Q_BLOCK:(j + 1) * Q_BLOCK]
            part_v = dv[j * Q_BLOCK:(j + 1) * Q_BLOCK]
            if j == 2:
                dk_acc[slot] = part_k
                dv_acc[slot] = part_v
            else:
                dk_acc[slot] += part_k
                dv_acc[slot] += part_v

        def flush(block):
            slot = lax.rem(block + 3, 3)
            rows = pl.ds(pl.multiple_of(block * Q_BLOCK, Q_BLOCK), Q_BLOCK)
            dqkv_ref[1, rows, :] = dk_acc[slot].astype(BF16)
            dqkv_ref[2, rows, :] = dv_acc[slot].astype(BF16)

        @pl.when(b >= 2)
        def _():
            flush(b - 2)

        @pl.when(b == nb - 1)
        def _():
            flush(b - 1)
            flush(b)

        @pl.when((p_id == N_HEADS // 2 - 1) & (b == nb - 1))
        def _():
            for cp in _owner_copies(to_chip_refs, from_chip_refs, send_sems, recv_sems):
                cp.wait_recv()
                cp.wait_send()

    hbm = pl.BlockSpec(memory_space=pl.ANY)
    return pl.pallas_call(
        body, name="attn_bwd",
        grid=(N_HEADS // 2, nb),
        in_specs=_attn_specs() + [pl.BlockSpec((Q_BLOCK, 128), lambda p, b: (b, p))] + [hbm] * n,
        out_specs=(pl.BlockSpec((3, s, 128), lambda p, b: (0, 0, p)),
                   pl.BlockSpec((2, Q_BLOCK, K_SPAN), lambda p, b: (p, 0, 0))) + (hbm,) * n,
        out_shape=(jax.ShapeDtypeStruct((3, s, D_A), BF16),
                   jax.ShapeDtypeStruct((N_HEADS, Q_BLOCK, K_SPAN), F32))
        + tuple(jax.ShapeDtypeStruct(t.shape, t.dtype) for t in to_chip),
        scratch_shapes=[pltpu.VMEM((3, Q_BLOCK, 128), F32), pltpu.VMEM((3, Q_BLOCK, 128), F32),
                        pltpu.SemaphoreType.DMA((n, 3)), pltpu.SemaphoreType.DMA((n, 3))],
        compiler_params=_params(48),
    )(qkv, qkv, qkv, qkv, qkv, qkv, qkv, bias_table, d_out, *to_chip)


def _mid_fwd_bwd(x, target, attn_out, z, w_pa, w_pb, w_out, b_gate, ln_g, ln_b, w_s, b_s, final_g):
    s = x.shape[0]
    tm = TOKEN_TILE
    nt = s // tm
    n_sub = tm // SGU_CHUNK

    def body(x_ref, t_ref, oa_ref, ga_ref, ub_ref, vb_ref, gb_ref, ta0_ref, ta1_ref, tb0_ref, tb1_ref,
             wpa_hbm, wpb_hbm, wout_hbm, bg_ref, lng_ref, lnb_ref, ws_ref, bs_ref, fg_ref,
             dx2_ref, doa_ref, dz_ref, dwout_hbm, dwpa_hbm, dwpb_hbm, dbg_ref, dfg_ref, dlng_ref, dlnb_ref, dws_ref,
             dbs_ref, loss_ref,
             wpa, wpb, wout, wmix, acc_out, acc_pa, acc_pb, mixed_s, dvn_s, sem):
        i = pl.program_id(0)

        @pl.when(i == 0)
        def _():
            loads = [pltpu.make_async_copy(src, dst, sem.at[n])
                     for n, (src, dst) in enumerate(((wpa_hbm, wpa), (wpb_hbm, wpb), (wout_hbm, wout)))]
            for cp in loads:
                cp.start()
            t_idx = lax.broadcasted_iota(jnp.int32, (SGU_CHUNK, SGU_CHUNK), 0)
            s_idx = lax.broadcasted_iota(jnp.int32, (SGU_CHUNK, SGU_CHUNK), 1)
            for g in range(N_GROUPS):
                wmix[g] = jnp.where(s_idx <= t_idx, ws_ref[g], 0.0).astype(BF16)
            for ref in (acc_out, acc_pa, acc_pb, dbg_ref, dfg_ref, dlng_ref, dlnb_ref, dws_ref, dbs_ref, loss_ref):
                ref[...] = jnp.zeros(ref.shape, F32)
            for cp in loads:
                cp.wait()

        g_a = ga_ref[...].astype(F32)
        u_b = ub_ref[...].astype(F32)
        v_b = vb_ref[...].astype(F32)
        g_b = gb_ref[...].astype(F32)
        bg = bg_ref[...]

        sg_a = _sigmoid(g_a)
        silu_a = g_a * sg_a
        o_a = oa_ref[...]
        y_a = (o_a * silu_a).astype(BF16)

        ug, dgelu_u = _gelu_and_grad(u_b)
        vg, dgelu_v = _gelu_and_grad(v_b)
        mu = jnp.mean(vg, axis=-1, keepdims=True)
        vc = vg - mu
        rstd = lax.rsqrt(jnp.mean(vc * vc, axis=-1, keepdims=True) + EPS)
        vhat = vc * rstd
        lng = lng_ref[...]
        vn = (vhat * lng + lnb_ref[...]).astype(BF16)
        for n in range(n_sub):
            rows = slice(n * SGU_CHUNK, (n + 1) * SGU_CHUNK)
            for g in range(N_GROUPS):
                cols = slice(g * 128, (g + 1) * 128)
                mixed_s[rows, cols] = _dot(wmix[g], vn[rows, cols]) + bs_ref[g]
        mixed = mixed_s[...]
        sg_b = _sigmoid(g_b)
        silu_b = g_b * sg_b
        um = ug * mixed
        y_b = (um * silu_b).astype(BF16)

        p_a = _dot(y_a, wpa[...])
        p_b = _dot(y_b, wpb[...])
        gate_a = _sigmoid(jnp.concatenate([ta0_ref[...], ta1_ref[...]], axis=1).astype(F32) + bg[:, :D_MODEL])
        gate_b = _sigmoid(jnp.concatenate([tb0_ref[...], tb1_ref[...]], axis=1).astype(F32) + bg[:, D_MODEL:])
        merged = (gate_a * p_a + gate_b * p_b).astype(BF16)
        x2 = x_ref[...] + _dot(merged, wout[...])
        r2 = lax.rsqrt(jnp.mean(x2 * x2, axis=-1, keepdims=True) + EPS)
        xh = x2 * r2
        fg = fg_ref[...]
        err = xh * fg - t_ref[...]
        loss_ref[...] += jnp.sum(jnp.sum(err * err, axis=-1, keepdims=True), axis=0, keepdims=True) * (0.5 / D_MODEL)

        dy = err * (1.0 / D_MODEL)
        dfg_ref[...] += jnp.sum(dy * xh, axis=0, keepdims=True)
        gy = dy * fg
        dx2 = r2 * (gy - xh * jnp.mean(gy * xh, axis=-1, keepdims=True))
        dx2_ref[...] = dx2
        dx2b = dx2.astype(BF16)
        dmerged = _dot(dx2b, wout[...], NT)
        acc_out[...] += _dot(merged, dx2b, TN)

        dp_a = dmerged * gate_a
        dp_b = dmerged * gate_b
        dgate_a = dp_a * p_a * (1.0 - gate_a)
        dgate_b = dp_b * p_b * (1.0 - gate_b)
        dbg_ref[:, :D_MODEL] += jnp.sum(dgate_a, axis=0, keepdims=True)
        dbg_ref[:, D_MODEL:] += jnp.sum(dgate_b, axis=0, keepdims=True)
        dz_ref[:, 2048:3072] = dgate_a.astype(BF16)
        dz_ref[:, 3072:4096] = dgate_b.astype(BF16)
        dp_ab = dp_a.astype(BF16)
        dp_bb = dp_b.astype(BF16)
        dy_a = _dot(dp_ab, wpa[...], NT)
        dy_b = _dot(dp_bb, wpb[...], NT)
        acc_pa[...] += _dot(y_a, dp_ab, TN)
        acc_pb[...] += _dot(y_b, dp_bb, TN)

        doa_ref[...] = (dy_a * silu_a).astype(BF16)
        dz_ref[:, 0:512] = (dy_a * o_a * (sg_a * (1.0 + g_a * (1.0 - sg_a)))).astype(BF16)
        dz_ref[:, 1536:2048] = (dy_b * um * (sg_b * (1.0 + g_b * (1.0 - sg_b)))).astype(BF16)
        dys = dy_b * silu_b
        dz_ref[:, 512:1024] = (dys * mixed * dgelu_u).astype(BF16)
        dmixed = dys * ug
        dmb = dmixed.astype(BF16)
        for n in range(n_sub):
            rows = slice(n * SGU_CHUNK, (n + 1) * SGU_CHUNK)
            for g in range(N_GROUPS):
                cols = slice(g * 128, (g + 1) * 128)
                dws_ref[g] += _dot(dmb[rows, cols], vn[rows, cols], NT)
                dbs_ref[g] += jnp.sum(dmixed[rows, cols], axis=-1, keepdims=True)
                dvn_s[rows, cols] = _dot(wmix[g], dmb[rows, cols], TN)
        dvn = dvn_s[...]
        dlng_ref[...] += jnp.sum(dvn * vhat, axis=0, keepdims=True)
        dlnb_ref[...] += jnp.sum(dvn, axis=0, keepdims=True)
        dvh = dvn * lng
        dvg = rstd * (dvh - jnp.mean(dvh, axis=-1, keepdims=True) - vhat * jnp.mean(dvh * vhat, axis=-1, keepdims=True))
        dz_ref[:, 1024:1536] = (dvg * dgelu_v).astype(BF16)

        @pl.when(i == nt - 1)
        def _():
            t_idx = lax.broadcasted_iota(jnp.int32, (SGU_CHUNK, SGU_CHUNK), 0)
            s_idx = lax.broadcasted_iota(jnp.int32, (SGU_CHUNK, SGU_CHUNK), 1)
            for g in range(N_GROUPS):
                dws_ref[g] = jnp.where(s_idx <= t_idx, dws_ref[g], 0.0)
            stores = [pltpu.make_async_copy(src, dst, sem.at[n])
                      for n, (src, dst) in enumerate(((acc_out, dwout_hbm), (acc_pa, dwpa_hbm), (acc_pb, dwpb_hbm)))]
            for cp in stores:
                cp.start()
            for cp in stores:
                cp.wait()

    tile = lambda w: pl.BlockSpec((tm, w), lambda i: (i, 0))
    whole = lambda shape: pl.BlockSpec(shape, lambda i: (0,) * len(shape))
    hbm = pl.BlockSpec(memory_space=pl.ANY)
    return pl.pallas_call(
        body, name="mid_fwd_bwd",
        grid=(nt,),
        in_specs=[tile(D_MODEL), tile(D_MODEL), tile(D_A)]
        + [pl.BlockSpec((tm, COL_BLOCK), functools.partial(lambda c, i: (i, c), c)) for c in range(3, N_COL_BLOCKS)]
        + [hbm, hbm, hbm,
                  whole((1, 2 * D_MODEL)), whole((1, D_B)), whole((1, D_B)),
                  whole((N_GROUPS, SGU_CHUNK, SGU_CHUNK)), whole((N_GROUPS, SGU_CHUNK, 1)), whole((1, D_MODEL))],
        out_specs=(tile(D_MODEL), tile(D_A), tile(REST), hbm, hbm, hbm,
                   whole((1, 2 * D_MODEL)), whole((1, D_MODEL)), whole((1, D_B)), whole((1, D_B)),
                   whole((N_GROUPS, SGU_CHUNK, SGU_CHUNK)), whole((N_GROUPS, SGU_CHUNK, 1)), whole((1, 1))),
        out_shape=(jax.ShapeDtypeStruct((s, D_MODEL), F32), jax.ShapeDtypeStruct((s, D_A), BF16),
                   jax.ShapeDtypeStruct((s, REST), BF16),
                   jax.ShapeDtypeStruct((D_MODEL, D_MODEL), F32), jax.ShapeDtypeStruct((D_A, D_MODEL), F32),
                   jax.ShapeDtypeStruct((D_B, D_MODEL), F32),
                   jax.ShapeDtypeStruct((1, 2 * D_MODEL), F32), jax.ShapeDtypeStruct((1, D_MODEL), F32),
                   jax.ShapeDtypeStruct((1, D_B), F32), jax.ShapeDtypeStruct((1, D_B), F32),
                   jax.ShapeDtypeStruct((N_GROUPS, SGU_CHUNK, SGU_CHUNK), F32),
                   jax.ShapeDtypeStruct((N_GROUPS, SGU_CHUNK, 1), F32), jax.ShapeDtypeStruct((1, 1), F32)),
        scratch_shapes=[pltpu.VMEM((D_A, D_MODEL), BF16), pltpu.VMEM((D_B, D_MODEL), BF16),
                        pltpu.VMEM((D_MODEL, D_MODEL), BF16), pltpu.VMEM((N_GROUPS, SGU_CHUNK, SGU_CHUNK), BF16),
                        pltpu.VMEM((D_MODEL, D_MODEL), F32), pltpu.VMEM((D_A, D_MODEL), F32),
                        pltpu.VMEM((D_B, D_MODEL), F32),
                        pltpu.VMEM((tm, D_B), F32), pltpu.VMEM((tm, D_B), F32),
                        pltpu.SemaphoreType.DMA((3,))],
        compiler_params=_params(56),
    )(x, target, attn_out, *([z] * (N_COL_BLOCKS - 3)), w_pa, w_pb, w_out, b_gate, ln_g, ln_b, w_s, b_s, final_g)


def _proj_bwd_x(dqkv, drest, x, dx2, norm_g, w_in, to_chip):
    s = x.shape[0]
    tm = TOKEN_TILE
    nt = s // tm
    n = len(to_chip)

    def body(*refs):
        dqkv_ref, dr_ref, x_ref, dx2_ref, g_ref, w_hbm = refs[:6]
        to_chip_refs = refs[6:6 + n]
        dx_ref, dg_ref = refs[6 + n:8 + n]
        from_chip_refs = refs[8 + n:8 + 2 * n]
        w, sem, send_sems, recv_sems = refs[8 + 2 * n:]
        i = pl.program_id(0)

        @pl.when(i == 0)
        def _():
            for rc in _owner_copies(to_chip_refs, from_chip_refs, send_sems, recv_sems):
                rc.start()
            cp = pltpu.make_async_copy(w_hbm, w, sem)
            cp.start()
            dg_ref[...] = jnp.zeros(dg_ref.shape, F32)
            cp.wait()

        dh = None
        for c in range(N_COL_BLOCKS):
            dz = dqkv_ref[c] if c < 3 else dr_ref[:, (c - 3) * COL_BLOCK:(c - 2) * COL_BLOCK]
            part = _dot(dz, w[:, c * COL_BLOCK:(c + 1) * COL_BLOCK], NT)
            dh = part if dh is None else dh + part
        xf = x_ref[...]
        r = lax.rsqrt(jnp.mean(xf * xf, axis=-1, keepdims=True) + EPS)
        xn = xf * r
        dg_ref[...] += jnp.sum(dh * xn, axis=0, keepdims=True)
        gh = dh * g_ref[...]
        dx_ref[...] = r * (gh - xn * jnp.mean(gh * xn, axis=-1, keepdims=True)) + dx2_ref[...]

        @pl.when(i == nt - 1)
        def _():
            for rc in _owner_copies(to_chip_refs, from_chip_refs, send_sems, recv_sems):
                rc.wait_recv()
                rc.wait_send()

    hbm = pl.BlockSpec(memory_space=pl.ANY)
    return pl.pallas_call(
        body, name="proj_bwd_x",
        grid=(nt,),
        in_specs=[pl.BlockSpec((3, tm, D_A), lambda i: (0, i, 0)),
                  pl.BlockSpec((tm, REST), lambda i: (i, 0)),
                  pl.BlockSpec((tm, D_MODEL), lambda i: (i, 0)),
                  pl.BlockSpec((tm, D_MODEL), lambda i: (i, 0)),
                  pl.BlockSpec((1, D_MODEL), lambda i: (0, 0)),
                  hbm] + [hbm] * n,
        out_specs=(pl.BlockSpec((tm, D_MODEL), lambda i: (i, 0)),
                   pl.BlockSpec((1, D_MODEL), lambda i: (0, 0))) + (hbm,) * n,
        out_shape=(jax.ShapeDtypeStruct((s, D_MODEL), F32), jax.ShapeDtypeStruct((1, D_MODEL), F32))
        + tuple(jax.ShapeDtypeStruct(t.shape, t.dtype) for t in to_chip),
        scratch_shapes=[pltpu.VMEM((D_MODEL, D_IN), BF16), pltpu.SemaphoreType.DMA,
                        pltpu.SemaphoreType.DMA((n, 3)), pltpu.SemaphoreType.DMA((n, 3))],
        compiler_params=_params(48),
    )(dqkv, drest, x, dx2, norm_g, w_in, *to_chip)


def _proj_bwd_w(h_t, dqkv, drest):
    s = h_t.shape[1]
    tk = min(s, 1024)
    nk = s // tk

    def body(ht_ref, dqkv_ref, dr_ref, o_ref, acc):
        j = pl.program_id(0)
        i = pl.program_id(1)

        @pl.when(i == 0)
        def _():
            acc[...] = jnp.zeros(acc.shape, F32)

        @pl.when(j < 3)
        def _():
            acc[...] += _dot(ht_ref[...], dqkv_ref[...])

        @pl.when(j >= 3)
        def _():
            acc[...] += _dot(ht_ref[...], dr_ref[...])

        @pl.when(i == nk - 1)
        def _():
            o_ref[...] = acc[...].astype(BF16)

    return pl.pallas_call(
        body, name="proj_bwd_w",
        grid=(N_COL_BLOCKS, nk),
        in_specs=[pl.BlockSpec((D_MODEL, tk), lambda j, i: (0, i)),
                  pl.BlockSpec((None, tk, COL_BLOCK),
                               lambda j, i: (jnp.minimum(j, 2), jnp.where(j < 3, i, nk - 1), 0)),
                  pl.BlockSpec((tk, COL_BLOCK),
                               lambda j, i: (jnp.where(j >= 3, i, 0), jnp.maximum(j - 3, 0)))],
        out_specs=pl.BlockSpec((D_MODEL, COL_BLOCK), lambda j, i: (0, j)),
        out_shape=jax.ShapeDtypeStruct((D_MODEL, D_IN), BF16),
        scratch_shapes=[pltpu.VMEM((D_MODEL, COL_BLOCK), F32)],
        compiler_params=_params(40),
    )(h_t, dqkv, drest)


def _adamw(name, w, g, m, v, from_chip=None):
    rows, cols = w.shape
    tr = rows if rows * cols <= 512 * 1024 else 256
    c1 = 1.0 - ADAM_B1 ** ADAM_STEP
    c2 = 1.0 - ADAM_B2 ** ADAM_STEP
    extra = [] if from_chip is None else [from_chip]

    def body(w_ref, g_ref, m_ref, v_ref, *rest):
        g_out, d_ref, nm_ref, nv_ref = rest[len(extra):]
        gg = g_ref[...]
        for t_ref in rest[:len(extra)]:
            for j in range(3):
                gg = gg + t_ref[j].astype(F32)
        nm = ADAM_B1 * m_ref[...] + (1.0 - ADAM_B1) * gg
        nv = ADAM_B2 * v_ref[...] + (1.0 - ADAM_B2) * (gg * gg)
        g_out[...] = gg
        d_ref[...] = -ADAM_LR * ((nm / c1) / (jnp.sqrt(nv / c2) + ADAM_EPS) + ADAM_WD * w_ref[...])
        nm_ref[...] = nm
        nv_ref[...] = nv

    spec = pl.BlockSpec((tr, cols), lambda i: (i, 0))
    shape = jax.ShapeDtypeStruct((rows, cols), F32)
    return pl.pallas_call(
        body, name=name,
        grid=(rows // tr,),
        in_specs=[spec] * 4 + [pl.BlockSpec((3, tr, cols), lambda i: (0, i, 0))] * len(extra),
        out_specs=(spec,) * 4, out_shape=(shape,) * 4,
        compiler_params=_params(32),
    )(w, g, m, v, *extra)


_SMALL = (("norm_g", D_MODEL), ("b_gate", 2 * D_MODEL), ("rel_bias", N_HEADS * N_REL_PAD), ("sgu_ln_g", D_B),
          ("sgu_ln_b", D_B), ("w_s", N_GROUPS * SGU_CHUNK * SGU_CHUNK), ("b_s", N_GROUPS * SGU_CHUNK),
          ("final_g", D_MODEL), ("loss", 1))


def _slab_rows(n):
    return -(-n // 1024) * 8


def _pack(parts):
    rows = []
    for (_, n), a in zip(_SMALL, parts):
        flat = a.reshape(-1).astype(F32)
        pad = _slab_rows(n) * 128 - flat.shape[0]
        rows.append(jnp.pad(flat, (0, pad)).reshape(-1, 128))
    return jnp.concatenate(rows, axis=0)


def _unpack(slab):
    out, r0 = {}, 0
    for name, n in _SMALL:
        nr = _slab_rows(n)
        out[name] = slab[r0:r0 + nr].reshape(-1)[:n]
        r0 += nr
    return out


def _pad_rel(a):
    return jnp.pad(a.reshape(N_HEADS, N_REL), ((0, 0), (0, N_REL_PAD - N_REL)))


def kernel(x, norm_g, w_in, b_gate, rel_bias, sgu_ln_g, sgu_ln_b, w_s, b_s, w_pa, w_pb, w_out, final_g, loss_target, m_norm_g, m_w_in, m_b_gate, m_rel_bias, m_sgu_ln_g, m_sgu_ln_b, m_w_s, m_b_s, m_w_pa, m_w_pb, m_w_out, m_final_g, v_norm_g, v_w_in, v_b_gate, v_rel_bias, v_sgu_ln_g, v_sgu_ln_b, v_w_s, v_b_s, v_w_pa, v_w_pb, v_w_out, v_final_g):
    s = x.shape[1]
    xs = x.reshape(s, D_MODEL)
    tgt = loss_target.reshape(s, D_MODEL)

    bias_table = _bias_table(_pad_rel(rel_bias))
    qkv, h_t, w_in_full, g_pa, g_pb, g_out = _gather_proj_fwd(xs, norm_g, w_in[0], w_pa[0], w_pb[0], w_out[0])
    w_pa_full = jnp.transpose(g_pa, (1, 0, 2)).reshape(D_A, D_MODEL)
    w_pb_full = jnp.transpose(g_pb, (1, 0, 2)).reshape(D_B, D_MODEL)
    w_out_full = g_out.reshape(D_MODEL, D_MODEL)

    attn_out = _attn_fwd(qkv, bias_table)
    (dx2, d_attn, drest, dw_out, dw_pa, dw_pb, d_bgate, d_fg, d_lng, d_lnb, d_ws, d_bs, loss_part) = _mid_fwd_bwd(
        xs, tgt, attn_out, qkv, w_pa_full, w_pb_full, w_out_full, b_gate, sgu_ln_g, sgu_ln_b, w_s[0],
        b_s.reshape(N_GROUPS, SGU_CHUNK, 1), final_g.reshape(1, D_MODEL))

    to_blocks = lambda a, rows: jnp.transpose(a.astype(BF16).reshape(rows, N_DEV, -1), (1, 0, 2))
    own_pa, own_pb, own_out, tc_pa, tc_pb, tc_out = _reduce_chip(
        "reduce_chip_proj", (to_blocks(dw_pa, D_A), to_blocks(dw_pb, D_B),
                             dw_out.astype(BF16).reshape(N_DEV, D_MODEL // N_DEV, D_MODEL)))
    dqkv, dbias, fc_pa, fc_pb, fc_out = _attn_bwd(qkv, bias_table, d_attn, (tc_pa, tc_pb, tc_out))
    d_rel = _bias_grad(dbias)
    dw_in = _proj_bwd_w(h_t, dqkv, drest)
    own_in, tc_in = _reduce_chip("reduce_chip_in", (to_blocks(dw_in, D_MODEL),))
    grad_x, d_ng, fc_in = _proj_bwd_x(dqkv, drest, xs, dx2, norm_g, w_in_full, (tc_in,))
    big = {}
    for name, w, g, fc, m, v in (("w_in", w_in, own_in, fc_in, m_w_in, v_w_in),
                                 ("w_pa", w_pa, own_pa, fc_pa, m_w_pa, v_w_pa),
                                 ("w_pb", w_pb, own_pb, fc_pb, m_w_pb, v_w_pb),
                                 ("w_out", w_out, own_out, fc_out, m_w_out, v_w_out)):
        big[name] = tuple(t[None] for t in _adamw("adamw_" + name, w[0], g, m[0], v[0], fc))

    zero = jnp.zeros((1,), F32)
    g_slab = _reduce_small(_pack((d_ng, d_bgate, d_rel, d_lng, d_lnb, d_ws, d_bs, d_fg, loss_part)))
    small_in = lambda t: _pack(tuple(_pad_rel(a) if a.shape[-1] == N_REL else a for a in t) + (zero,))
    w_slab = small_in((norm_g, b_gate, rel_bias, sgu_ln_g, sgu_ln_b, w_s, b_s, final_g))
    m_slab = small_in((m_norm_g, m_b_gate, m_rel_bias, m_sgu_ln_g, m_sgu_ln_b, m_w_s, m_b_s, m_final_g))
    v_slab = small_in((v_norm_g, v_b_gate, v_rel_bias, v_sgu_ln_g, v_sgu_ln_b, v_w_s, v_b_s, v_final_g))
    _, d_slab, nm_slab, nv_slab = _adamw("adamw_small", w_slab, g_slab, m_slab, v_slab)
    small = [_unpack(t) for t in (g_slab, d_slab, nm_slab, nv_slab)]

    def leaf(kind, name, like):
        if name in big:
            return big[name][kind]
        a = small[kind][name]
        if name == "rel_bias":
            a = a.reshape(N_HEADS, N_REL_PAD)[:, :N_REL]
        return a.reshape(like.shape)

    weights = (("norm_g", norm_g), ("w_in", w_in), ("b_gate", b_gate), ("rel_bias", rel_bias), ("sgu_ln_g", sgu_ln_g),
               ("sgu_ln_b", sgu_ln_b), ("w_s", w_s), ("b_s", b_s), ("w_pa", w_pa), ("w_pb", w_pb), ("w_out", w_out),
               ("final_g", final_g))
    loss = small[0]["loss"].reshape(())
    outs = [loss, grad_x.reshape(x.shape)]
    for kind in range(4):
        outs.extend(leaf(kind, name, like) for name, like in weights)
    return tuple(outs)
```

```python
import functools
import math

import jax
import jax.numpy as jnp
from jax import lax
from jax.experimental import pallas as pl
from jax.experimental.pallas import tpu as pltpu

F32 = jnp.float32
BF16 = jnp.bfloat16
MESH = pl.DeviceIdType.MESH
N_DEV = 8

D_MODEL = 1024
D_A = 512
D_B = 512
D_IN = 5632
N_HEADS = 8
HEAD_DIM = 64
CHUNK = 64
N_PREV = 8
REL_CLIP = 128
N_REL = 2 * REL_CLIP + 1
N_REL_PAD = 384
SGU_CHUNK = 128
N_GROUPS = 4
EPS = 1e-6
NEG_INF = -1e30
Q_SCALE = HEAD_DIM ** -0.5

Q_BLOCK = 256
K_SPAN = 768
ROLL_W = 1024
COL_BLOCK = 512
N_COL_BLOCKS = D_IN // COL_BLOCK
REST = D_IN - 3 * D_A
TOKEN_TILE = 256
V7X_VMEM_BYTES = 64 * 1024 * 1024

ADAM_LR = 0.001
ADAM_B1 = 0.9
ADAM_B2 = 0.999
ADAM_EPS = 1e-08
ADAM_WD = 0.01
ADAM_STEP = 10

GELU_C = math.sqrt(2.0 / math.pi)
GELU_A = 0.044715

NT = (((1,), (1,)), ((), ()))
TN = (((0,), (0,)), ((), ()))
HIGHEST = lax.Precision.HIGHEST


def _params(vmem_mb, **kw):
    return pltpu.CompilerParams(vmem_limit_bytes=vmem_mb * 1024 * 1024, **kw)


def _dot(a, b, dims=None):
    if dims is None:
        return jnp.dot(a, b, preferred_element_type=F32)
    return lax.dot_general(a, b, dims, preferred_element_type=F32)


def _sigmoid(x):
    return 1.0 / (1.0 + jnp.exp(-x))


def _gelu_and_grad(u):
    u2 = u * u
    t = jnp.tanh(GELU_C * (u + GELU_A * u * u2))
    half = 0.5 * (1.0 + t)
    g = u * half
    dg = half + 0.5 * u * (1.0 - t * t) * (GELU_C * (1.0 + 3.0 * GELU_A * u2))
    return g, dg


def _my_pos():
    return lax.axis_index("x"), lax.axis_index("y"), lax.axis_index("c")


def _flat_id(pos):
    return 4 * pos[0] + 2 * pos[1] + pos[2]


def _peer(pos, k):
    x, y, c = pos
    return (1 - x if k & 4 else x, 1 - y if k & 2 else y, 1 - c if k & 1 else c)


def _other_chips(pos):
    x, y, _ = pos
    return ((1 - x, y), (x, 1 - y), (1 - x, 1 - y))


class _SlotGather:
    def __init__(self, bufs, send_sems, recv_sems, own=None):
        self.bufs, self.send_sems, self.recv_sems = bufs, send_sems, recv_sems
        self.own = own if own is not None else [None] * len(bufs)
        x, y, c = _my_pos()
        self.c, self.me, self.sib = c, (x, y, c), (x, y, 1 - c)
        self.chips = _other_chips(self.me)

    def _copy(self, a, k, block, to):
        slot = _flat_id(block)
        src = self.own[a] if (k < 4 and self.own[a] is not None) else self.bufs[a].at[slot]
        return pltpu.make_async_remote_copy(
            src_ref=src, dst_ref=self.bufs[a].at[slot],
            send_sem=self.send_sems.at[a, k], recv_sem=self.recv_sems.at[a, k], device_id=to, device_id_type=MESH)

    def _own_sends(self):
        n = len(self.bufs)
        return ([self._copy(a, 1 + j, self.me, (*chip, self.c)) for j, chip in enumerate(self.chips) for a in range(n)]
                + [self._copy(a, 0, self.me, self.sib) for a in range(n)])

    def _passes(self):
        return [self._copy(a, 4 + j, (*chip, self.c), self.sib)
                for j, chip in enumerate(self.chips) for a in range(len(self.bufs))]

    def start(self):
        for cp in self._own_sends():
            cp.start()

    def pass_on(self):
        for j, chip in enumerate(self.chips):
            for a in range(len(self.bufs)):
                self._copy(a, 1 + j, (*chip, self.c), self.me).wait_recv()
                self._copy(a, 4 + j, (*chip, self.c), self.sib).start()

    def finish(self):
        for a in range(len(self.bufs)):
            self._copy(a, 0, self.sib, self.me).wait_recv()
            for j, chip in enumerate(self.chips):
                self._copy(a, 4 + j, (*chip, 1 - self.c), self.me).wait_recv()
        for cp in self._own_sends() + self._passes():
            cp.wait_send()


def _reduce_chip(name, parts, sharded_dim):
    n = len(parts)
    shapes = []
    for p, dim in zip(parts, sharded_dim):
        shape = list(p.shape)
        shape[dim] //= N_DEV
        shapes.append(tuple(shape))

    def body(*refs):
        full, own, to_chip = refs[:n], refs[n:2 * n], refs[2 * n:3 * n]
        ins, from_sib = refs[3 * n:4 * n], refs[4 * n:5 * n]
        send_sems, recv_sems = refs[5 * n], refs[5 * n + 1]
        x, y, c = _my_pos()
        sib = (x, y, 1 - c)
        chips = ((x, y),) + _other_chips((x, y, c))
        for a in range(n):
            rows, cols = shapes[a]
            for d in range(N_DEV):
                if sharded_dim[a] == 0:
                    ins[a][d] = full[a][d * rows:(d + 1) * rows, :].astype(BF16)
                else:
                    ins[a][d] = full[a][:, d * cols:(d + 1) * cols].astype(BF16)

        def to_sibling(a, r):
            return pltpu.make_async_remote_copy(
                src_ref=ins[a].at[_flat_id((*chips[r], 1 - c))], dst_ref=from_sib[a].at[r],
                send_sem=send_sems.at[a, r], recv_sem=recv_sems.at[a, r], device_id=sib, device_id_type=MESH)

        sends = [to_sibling(a, r) for r in (1, 2, 3, 0) for a in range(n)]
        for cp in sends:
            cp.start()
        for r in (1, 2, 3, 0):
            for a in range(n):
                to_sibling(a, r).wait_recv()
                both = ins[a][_flat_id((*chips[r], c))].astype(F32) + from_sib[a][r].astype(F32)
                if r == 0:
                    own[a][...] = both
                else:
                    to_chip[a][r - 1] = both.astype(BF16)
        for cp in sends:
            cp.wait_send()

    vmem = pl.BlockSpec(memory_space=pltpu.VMEM)
    return pl.pallas_call(
        body, name=name,
        out_shape=tuple(jax.ShapeDtypeStruct(sh, F32) for sh in shapes)
        + tuple(jax.ShapeDtypeStruct((3,) + sh, BF16) for sh in shapes),
        in_specs=[vmem] * n, out_specs=tuple([vmem] * (2 * n)),
        scratch_shapes=[pltpu.VMEM((N_DEV,) + sh, BF16) for sh in shapes]
        + [pltpu.VMEM((4,) + sh, BF16) for sh in shapes]
        + [pltpu.SemaphoreType.DMA((n, 4)), pltpu.SemaphoreType.DMA((n, 4))],
        compiler_params=_params(56),
    )(*parts)


def _owner_copies(to_chip, from_chip, send_sems, recv_sems):
    x, y, c = _my_pos()
    return [pltpu.make_async_remote_copy(
        src_ref=to_chip[a].at[j], dst_ref=from_chip[a].at[j],
        send_sem=send_sems.at[a, j], recv_sem=recv_sems.at[a, j], device_id=(*chip, c), device_id_type=MESH)
        for a in range(len(to_chip)) for j, chip in enumerate(_other_chips((x, y, c)))]


def _reduce_small(slab):
    def body(slab_ref, out_ref, land, send_sems, recv_sems):
        land[_flat_id(_my_pos())] = slab_ref[...]
        gather = _SlotGather([land], send_sems, recv_sems)
        gather.start()
        gather.pass_on()
        gather.finish()
        acc = land[0]
        for d in range(1, N_DEV):
            acc = acc + land[d]
        out_ref[...] = acc

    vmem = pl.BlockSpec(memory_space=pltpu.VMEM)
    return pl.pallas_call(
        body, name="reduce_small",
        out_shape=jax.ShapeDtypeStruct(slab.shape, F32),
        in_specs=[vmem], out_specs=vmem,
        scratch_shapes=[pltpu.VMEM((N_DEV,) + slab.shape, F32),
                        pltpu.SemaphoreType.DMA((1, N_DEV - 1)), pltpu.SemaphoreType.DMA((1, N_DEV - 1))],
        compiler_params=_params(16),
    )(slab)


def _rel_index(e):
    return jnp.where(e <= 384, 2 * REL_CLIP, jnp.where(e < 640, 640 - e, jnp.where(e <= K_SPAN, 0, 2 * REL_CLIP)))


def _bias_table(rel_bias_pad):
    def body(rb_ref, bt_ref):
        c = lax.broadcasted_iota(jnp.int32, (N_REL_PAD, ROLL_W), 1)
        r = lax.broadcasted_iota(jnp.int32, (N_REL_PAD, ROLL_W), 0)
        pick = (r == _rel_index(c)).astype(F32)
        rows = jnp.dot(rb_ref[...], pick, precision=HIGHEST, preferred_element_type=F32)
        qc = lax.broadcasted_iota(jnp.int32, (Q_BLOCK, K_SPAN), 0) >> 6
        kc = lax.broadcasted_iota(jnp.int32, (Q_BLOCK, K_SPAN), 1) >> 6
        band = (kc >= qc) & (kc <= qc + N_PREV)
        for h in range(N_HEADS):
            t = jnp.broadcast_to(rows[h:h + 1, :], (Q_BLOCK, ROLL_W))
            t = pltpu.roll(t, 0, 1, stride=1, stride_axis=0)
            bt_ref[h] = jnp.where(band, t[:, :K_SPAN], NEG_INF)

    return pl.pallas_call(
        body, name="bias_table",
        out_shape=jax.ShapeDtypeStruct((N_HEADS, Q_BLOCK, K_SPAN), F32),
        compiler_params=_params(32),
    )(rel_bias_pad)


def _bias_grad(dbias):
    def body(a_ref, o_ref):
        rr = lax.broadcasted_iota(jnp.int32, (Q_BLOCK, Q_BLOCK), 0)
        cc = lax.broadcasted_iota(jnp.int32, (Q_BLOCK, Q_BLOCK), 1)
        flip = (rr + cc == Q_BLOCK - 1).astype(F32)
        c = lax.broadcasted_iota(jnp.int32, (ROLL_W, N_REL_PAD), 0)
        r = lax.broadcasted_iota(jnp.int32, (ROLL_W, N_REL_PAD), 1)
        e = jnp.where(c >= Q_BLOCK - 1, c - (Q_BLOCK - 1), c + (ROLL_W - Q_BLOCK + 1))
        pick = (r == _rel_index(e)).astype(F32)
        sums = []
        for h in range(N_HEADS):
            a = jnp.dot(flip, a_ref[h], precision=HIGHEST, preferred_element_type=F32)
            a = jnp.concatenate([a, jnp.zeros((Q_BLOCK, ROLL_W - K_SPAN), F32)], axis=1)
            a = pltpu.roll(a, 0, 1, stride=1, stride_axis=0)
            sums.append(jnp.sum(a, axis=0, keepdims=True))
        diag = jnp.concatenate(sums, axis=0)
        o_ref[...] = jnp.dot(diag, pick, precision=HIGHEST, preferred_element_type=F32)

    return pl.pallas_call(
        body, name="bias_grad",
        out_shape=jax.ShapeDtypeStruct((N_HEADS, N_REL_PAD), F32),
        compiler_params=_params(32),
    )(dbias)


def _gather_proj_fwd(x, norm_g, w_in):
    s = x.shape[0]
    tm = 512 if s % 512 == 0 else TOKEN_TILE
    nt = s // tm
    shard_w = w_in.shape[1]
    chip_w = 2 * shard_w
    n_chips = N_DEV // 2

    def body(order_ref, x_ref, g_ref, win_hbm, z_ref, ht_ref, wnat_ref, slots, hb, win_f32, send_sems, recv_sems,
             load_sem):
        j = pl.program_id(0)
        i = pl.program_id(1)
        x_, y_, c_ = _my_pos()
        me, sib = (x_, y_, c_), (x_, y_, 1 - c_)
        near = _other_chips(me)
        pick = lambda a, b: tuple(jnp.where(c_ == 0, u, v) for u, v in zip(a, b))
        passed_from, passed_to = pick(near[0], near[1]), pick(near[1], near[0])

        def copy(k, block, to):
            slot = _flat_id(block)
            return pltpu.make_async_remote_copy(
                src_ref=slots.at[slot], dst_ref=slots.at[slot],
                send_sem=send_sems.at[k], recv_sem=recv_sems.at[k], device_id=to, device_id_type=MESH)

        def sends():
            return ([copy(0, me, sib), copy(1, me, (*near[0], c_)), copy(2, me, (*near[1], c_)),
                     copy(3, (*passed_from, c_), (*passed_to, c_))]
                    + [copy(4 + n, (*near[n], c_), sib) for n in range(3)])

        def assemble(chip):
            first = 2 * (2 * chip[0] + chip[1])
            wnat_ref[:, :shard_w] = slots[first]
            wnat_ref[:, shard_w:] = slots[first + 1]

        @pl.when((j == 0) & (i == 0))
        def _():
            load = pltpu.make_async_copy(win_hbm, win_f32, load_sem)
            load.start()
            load.wait()
            slots[_flat_id(me)] = win_f32[...].astype(BF16)
            for cp in sends()[:3]:
                cp.start()
            copy(0, sib, me).wait_recv()
            assemble((x_, y_))

        @pl.when((j == 1) & (i == 0))
        def _():
            copy(1, (*near[0], c_), me).wait_recv()
            copy(2, (*near[1], c_), me).wait_recv()
            for cp in sends()[3:6]:
                cp.start()
            copy(4, (*near[0], 1 - c_), me).wait_recv()
            assemble(near[0])

        @pl.when((j == 2) & (i == 0))
        def _():
            copy(5, (*near[1], 1 - c_), me).wait_recv()
            assemble(near[1])

        @pl.when((j == 3) & (i == 0))
        def _():
            copy(3, (*near[2], c_), me).wait_recv()
            copy(6, (*near[2], c_), sib).start()
            copy(6, (*near[2], 1 - c_), me).wait_recv()
            assemble(near[2])

        rows = pl.ds(pl.multiple_of(i * tm, tm), tm)

        @pl.when(j == 0)
        def _():
            xf = x_ref[...]
            r = lax.rsqrt(jnp.mean(xf * xf, axis=-1, keepdims=True) + EPS)
            hf = xf * r * g_ref[...]
            hb[rows, :] = hf.astype(BF16)
            ht_ref[...] = hf.T.astype(BF16)

        blk = _dot(hb[rows, :], wnat_ref[...])
        q_scale = jnp.where(order_ref[j] == 0, Q_SCALE, 1.0).astype(F32)
        z_ref[:, :D_A] = (blk[:, :D_A] * q_scale).astype(BF16)
        z_ref[:, D_A:] = blk[:, D_A:].astype(BF16)

        @pl.when((j == n_chips - 1) & (i == nt - 1))
        def _():
            for cp in sends():
                cp.wait_send()

    pos = _my_pos()
    order = jnp.stack([2 * cx + cy for cx, cy in ((pos[0], pos[1]),) + _other_chips(pos)]).astype(jnp.int32)
    first_pass = lambda j, i: jnp.where(j == 0, i, nt - 1)
    grid_spec = pltpu.PrefetchScalarGridSpec(
        num_scalar_prefetch=1,
        grid=(n_chips, nt),
        in_specs=[pl.BlockSpec((tm, D_MODEL), lambda j, i, o: (first_pass(j, i), 0)),
                  pl.BlockSpec((1, D_MODEL), lambda j, i, o: (0, 0)),
                  pl.BlockSpec(memory_space=pl.ANY)],
        out_specs=(pl.BlockSpec((tm, chip_w), lambda j, i, o: (i, o[j])),
                   pl.BlockSpec((D_MODEL, tm), lambda j, i, o: (0, first_pass(j, i))),
                   pl.BlockSpec((D_MODEL, chip_w), lambda j, i, o: (0, o[j]))),
        scratch_shapes=[pltpu.VMEM((N_DEV, D_MODEL, shard_w), BF16), pltpu.VMEM((s, D_MODEL), BF16),
                        pltpu.VMEM(w_in.shape, F32),
                        pltpu.SemaphoreType.DMA((N_DEV - 1,)), pltpu.SemaphoreType.DMA((N_DEV - 1,)),
                        pltpu.SemaphoreType.DMA])
    return pl.pallas_call(
        body, name="gather_proj_fwd",
        grid_spec=grid_spec,
        out_shape=(jax.ShapeDtypeStruct((s, D_IN), BF16), jax.ShapeDtypeStruct((D_MODEL, s), BF16),
                   jax.ShapeDtypeStruct((D_MODEL, D_IN), BF16)),
        compiler_params=_params(60),
    )(order, x, norm_g, w_in)


def _attn_specs():
    pairs = N_HEADS // 2

    def kv(which, back):
        return pl.BlockSpec((Q_BLOCK, 128), lambda p, b: (jnp.maximum(b - back, 0), which * pairs + p))
    return ([pl.BlockSpec((Q_BLOCK, 128), lambda p, b: (b, p))]
            + [kv(1, back) for back in (2, 1, 0)] + [kv(2, back) for back in (2, 1, 0)]
            + [pl.BlockSpec((2, Q_BLOCK, K_SPAN), lambda p, b: (p, 0, 0))])


def _head_masks():
    lane = lax.broadcasted_iota(jnp.int32, (1, 128), 1)
    first = lane < HEAD_DIM
    return (first, jnp.logical_not(first))


def _softmax_rows(qm, kcat, bias, valid):
    s = _dot(qm, kcat, NT) + bias
    s = jnp.where(valid, s, NEG_INF)
    m = jnp.max(s, axis=-1, keepdims=True)
    e = jnp.exp(s - m)
    return e * (1.0 / jnp.sum(e, axis=-1, keepdims=True))


def _attn_fwd(qkv, bias_table, shards):
    s = qkv.shape[0]
    nb = s // Q_BLOCK
    n = len(shards)
    pairs = N_HEADS // 2

    def body(*refs):
        q_ref, k2_ref, k1_ref, k0_ref, v2_ref, v1_ref, v0_ref, bt_ref = refs[:8]
        shard_refs = refs[8:8 + n]
        o_ref = refs[8 + n]
        slot_refs = refs[9 + n:9 + 2 * n]
        stages = refs[9 + 2 * n:9 + 3 * n]
        send_sems, recv_sems, local_sems = refs[9 + 3 * n:]
        p_id = pl.program_id(0)
        b = pl.program_id(1)
        gather = _SlotGather(slot_refs, send_sems, recv_sems, own=stages)
        keep = [pltpu.make_async_copy(stages[a], slot_refs[a].at[_flat_id(_my_pos())], local_sems.at[a])
                for a in range(n)]

        @pl.when((p_id == 0) & (b == 0))
        def _():
            for a in range(n):
                stages[a][...] = shard_refs[a][...].astype(BF16)
                keep[a].start()
            gather.start()

        @pl.when((p_id == 1) & (b == 0))
        def _():
            gather.pass_on()

        q = q_ref[...]
        kcat = jnp.concatenate([k2_ref[...], k1_ref[...], k0_ref[...]], axis=0)
        vcat = jnp.concatenate([v2_ref[...], v1_ref[...], v0_ref[...]], axis=0)
        valid = lax.broadcasted_iota(jnp.int32, (1, K_SPAN), 1) >= (2 - b) * Q_BLOCK
        zero = jnp.zeros((), BF16)
        out = None
        for hh, mask in enumerate(_head_masks()):
            p = _softmax_rows(jnp.where(mask, q, zero), kcat, bt_ref[hh], valid)
            o = _dot(p.astype(BF16), jnp.where(mask, vcat, zero))
            out = o if out is None else out + o
        o_ref[...] = out

        @pl.when((p_id == pairs - 1) & (b == nb - 1))
        def _():
            gather.finish()
            for cp in keep:
                cp.wait()

    hbm = pl.BlockSpec(memory_space=pl.ANY)
    return pl.pallas_call(
        body, name="attn_fwd",
        grid=(pairs, nb),
        in_specs=_attn_specs() + [pl.BlockSpec(a.shape, lambda p, b: (0, 0)) for a in shards],
        out_specs=(pl.BlockSpec((Q_BLOCK, 128), lambda p, b: (b, p)),) + (hbm,) * n,
        out_shape=(jax.ShapeDtypeStruct((s, D_A), F32),)
        + tuple(jax.ShapeDtypeStruct((N_DEV,) + a.shape, BF16) for a in shards),
        scratch_shapes=[pltpu.VMEM(a.shape, BF16) for a in shards]
        + [pltpu.SemaphoreType.DMA((n, N_DEV - 1)), pltpu.SemaphoreType.DMA((n, N_DEV - 1)),
           pltpu.SemaphoreType.DMA((n,))],
        compiler_params=_params(40),
    )(qkv, qkv, qkv, qkv, qkv, qkv, qkv, bias_table, *shards)


def _attn_bwd(qkv, bias_table, d_out, to_chip):
    s = qkv.shape[0]
    nb = s // Q_BLOCK
    n = len(to_chip)

    def body(*refs):
        q_ref, k2_ref, k1_ref, k0_ref, v2_ref, v1_ref, v0_ref, bt_ref, do_ref = refs[:9]
        to_chip_refs = refs[9:9 + n]
        dqkv_ref, db_ref = refs[9 + n:11 + n]
        from_chip_refs = refs[11 + n:11 + 2 * n]
        dk_acc, dv_acc, send_sems, recv_sems = refs[11 + 2 * n:]
        p_id = pl.program_id(0)
        b = pl.program_id(1)

        @pl.when((p_id == 0) & (b == 0))
        def _():
            for cp in _owner_copies(to_chip_refs, from_chip_refs, send_sems, recv_sems):
                cp.start()

        @pl.when(b == 0)
        def _():
            dk_acc[...] = jnp.zeros(dk_acc.shape, F32)
            dv_acc[...] = jnp.zeros(dv_acc.shape, F32)
            db_ref[...] = jnp.zeros(db_ref.shape, F32)

        q = q_ref[...]
        do = do_ref[...]
        kcat = jnp.concatenate([k2_ref[...], k1_ref[...], k0_ref[...]], axis=0)
        vcat = jnp.concatenate([v2_ref[...], v1_ref[...], v0_ref[...]], axis=0)
        valid = lax.broadcasted_iota(jnp.int32, (1, K_SPAN), 1) >= (2 - b) * Q_BLOCK
        zero = jnp.zeros((), BF16)
        dq = dk = dv = None
        for hh, mask in enumerate(_head_masks()):
            qm = jnp.where(mask, q, zero)
            dom = jnp.where(mask, do, zero)
            p = _softmax_rows(qm, kcat, bt_ref[hh], valid)
            dp = _dot(dom, vcat, NT)
            ds = p * (dp - jnp.sum(p * dp, axis=-1, keepdims=True))
            db_ref[hh] += ds
            dsb = ds.astype(BF16)
            dq_h = _dot(dsb, jnp.where(mask, kcat, zero))
            dk_h = _dot(dsb, qm, TN)
            dv_h = _dot(p.astype(BF16), dom, TN)
            dq = dq_h if dq is None else dq + dq_h
            dk = dk_h if dk is None else dk + dk_h
            dv = dv_h if dv is None else dv + dv_h

        rows_b = pl.ds(pl.multiple_of(b * Q_BLOCK, Q_BLOCK), Q_BLOCK)
        dqkv_ref[0, rows_b, :] = (dq * Q_SCALE).astype(BF16)

        for j in range(3):
            slot = lax.rem(b + 1 + j, 3)
            part_k = dk[j * Q_BLOCK:(j + 1) * Q_BLOCK]
            part_v = dv[j * Q_BLOCK:(j + 1) * Q_BLOCK]
            if j == 2:
                dk_acc[slot] = part_k
                dv_acc[slot] = part_v
            else:
                dk_acc[slot] += part_k
                dv_acc[slot] += part_v

        def flush(block):
            slot = lax.rem(block + 3, 3)
            rows = pl.ds(pl.multiple_of(block * Q_BLOCK, Q_BLOCK), Q_BLOCK)
            dqkv_ref[1, rows, :] = dk_acc[slot].astype(BF16)
            dqkv_ref[2, rows, :] = dv_acc[slot].astype(BF16)

        @pl.when(b >= 2)
        def _():
            flush(b - 2)

        @pl.when(b == nb - 1)
        def _():
            flush(b - 1)
            flush(b)

        @pl.when((p_id == N_HEADS // 2 - 1) & (b == nb - 1))
        def _():
            for cp in _owner_copies(to_chip_refs, from_chip_refs, send_sems, recv_sems):
                cp.wait_recv()
                cp.wait_send()

    hbm = pl.BlockSpec(memory_space=pl.ANY)
    return pl.pallas_call(
        body, name="attn_bwd",
        grid=(N_HEADS // 2, nb),
        in_specs=_attn_specs() + [pl.BlockSpec((Q_BLOCK, 128), lambda p, b: (b, p))] + [hbm] * n,
        out_specs=(pl.BlockSpec((3, s, 128), lambda p, b: (0, 0, p)),
                   pl.BlockSpec((2, Q_BLOCK, K_SPAN), lambda p, b: (p, 0, 0))) + (hbm,) * n,
        out_shape=(jax.ShapeDtypeStruct((3, s, D_A), BF16),
                   jax.ShapeDtypeStruct((N_HEADS, Q_BLOCK, K_SPAN), F32))
        + tuple(jax.ShapeDtypeStruct(t.shape, t.dtype) for t in to_chip),
        scratch_shapes=[pltpu.VMEM((3, Q_BLOCK, 128), F32), pltpu.VMEM((3, Q_BLOCK, 128), F32),
                        pltpu.SemaphoreType.DMA((n, 3)), pltpu.SemaphoreType.DMA((n, 3))],
        compiler_params=_params(48),
    )(qkv, qkv, qkv, qkv, qkv, qkv, qkv, bias_table, d_out, *to_chip)


def _mid_fwd_bwd(x, target, attn_out, z, w_pa, w_pb, w_out, b_gate, ln_g, ln_b, w_s, b_s, final_g):
    s = x.shape[0]
    tm = TOKEN_TILE
    nt = s // tm
    n_sub = tm // SGU_CHUNK

    def body(x_ref, t_ref, oa_ref, ga_ref, ub_ref, vb_ref, gb_ref, ta0_ref, ta1_ref, tb0_ref, tb1_ref,
             wpa_hbm, wpb_hbm, wout_hbm, bg_ref, lng_ref, lnb_ref, ws_ref, bs_ref, fg_ref,
             dx2_ref, doa_ref, dz_ref, dwout_hbm, dwpa_hbm, dwpb_hbm, dbg_ref, dfg_ref, dlng_ref, dlnb_ref, dws_ref,
             dbs_ref, loss_ref,
             wpa, wpb, wout, wmix, acc_out, acc_pa, acc_pb, mixed_s, dvn_s, sem):
        i = pl.program_id(0)

        @pl.when(i == 0)
        def _():
            loads = [pltpu.make_async_copy(src, dst, sem.at[n])
                     for n, (src, dst) in enumerate(((wpa_hbm, wpa), (wpb_hbm, wpb), (wout_hbm, wout)))]
            for cp in loads:
                cp.start()
            t_idx = lax.broadcasted_iota(jnp.int32, (SGU_CHUNK, SGU_CHUNK), 0)
            s_idx = lax.broadcasted_iota(jnp.int32, (SGU_CHUNK, SGU_CHUNK), 1)
            for g in range(N_GROUPS):
                wmix[g] = jnp.where(s_idx <= t_idx, ws_ref[g], 0.0).astype(BF16)
            for ref in (acc_out, acc_pa, acc_pb, dbg_ref, dfg_ref, dlng_ref, dlnb_ref, dws_ref, dbs_ref, loss_ref):
                ref[...] = jnp.zeros(ref.shape, F32)
            for cp in loads:
                cp.wait()

        g_a = ga_ref[...].astype(F32)
        u_b = ub_ref[...].astype(F32)
        v_b = vb_ref[...].astype(F32)
        g_b = gb_ref[...].astype(F32)
        bg = bg_ref[...]

        sg_a = _sigmoid(g_a)
        silu_a = g_a * sg_a
        o_a = oa_ref[...]
        y_a = (o_a * silu_a).astype(BF16)

        ug, dgelu_u = _gelu_and_grad(u_b)
        vg, dgelu_v = _gelu_and_grad(v_b)
        mu = jnp.mean(vg, axis=-1, keepdims=True)
        vc = vg - mu
        rstd = lax.rsqrt(jnp.mean(vc * vc, axis=-1, keepdims=True) + EPS)
        vhat = vc * rstd
        lng = lng_ref[...]
        vn = (vhat * lng + lnb_ref[...]).astype(BF16)
        for n in range(n_sub):
            rows = slice(n * SGU_CHUNK, (n + 1) * SGU_CHUNK)
            for g in range(N_GROUPS):
                cols = slice(g * 128, (g + 1) * 128)
                mixed_s[rows, cols] = _dot(wmix[g], vn[rows, cols]) + bs_ref[g]
        mixed = mixed_s[...]
        sg_b = _sigmoid(g_b)
        silu_b = g_b * sg_b
        um = ug * mixed
        y_b = (um * silu_b).astype(BF16)

        p_a = _dot(y_a, wpa[...])
        p_b = _dot(y_b, wpb[...])
        gate_a = _sigmoid(jnp.concatenate([ta0_ref[...], ta1_ref[...]], axis=1).astype(F32) + bg[:, :D_MODEL])
        gate_b = _sigmoid(jnp.concatenate([tb0_ref[...], tb1_ref[...]], axis=1).astype(F32) + bg[:, D_MODEL:])
        merged = (gate_a * p_a + gate_b * p_b).astype(BF16)
        x2 = x_ref[...] + _dot(merged, wout[...])
        r2 = lax.rsqrt(jnp.mean(x2 * x2, axis=-1, keepdims=True) + EPS)
        xh = x2 * r2
        fg = fg_ref[...]
        err = xh * fg - t_ref[...]
        loss_ref[...] += jnp.sum(jnp.sum(err * err, axis=-1, keepdims=True), axis=0, keepdims=True) * (0.5 / D_MODEL)

        dy = err * (1.0 / D_MODEL)
        dfg_ref[...] += jnp.sum(dy * xh, axis=0, keepdims=True)
        gy = dy * fg
        dx2 = r2 * (gy - xh * jnp.mean(gy * xh, axis=-1, keepdims=True))
        dx2_ref[...] = dx2
        dx2b = dx2.astype(BF16)
        dmerged = _dot(dx2b, wout[...], NT)
        acc_out[...] += _dot(merged, dx2b, TN)

        dp_a = dmerged * gate_a
        dp_b = dmerged * gate_b
        dgate_a = dp_a * p_a * (1.0 - gate_a)
        dgate_b = dp_b * p_b * (1.0 - gate_b)
        dbg_ref[:, :D_MODEL] += jnp.sum(dgate_a, axis=0, keepdims=True)
        dbg_ref[:, D_MODEL:] += jnp.sum(dgate_b, axis=0, keepdims=True)
        dz_ref[:, 2048:3072] = dgate_a.astype(BF16)
        dz_ref[:, 3072:4096] = dgate_b.astype(BF16)
        dp_ab = dp_a.astype(BF16)
        dp_bb = dp_b.astype(BF16)
        dy_a = _dot(dp_ab, wpa[...], NT)
        dy_b = _dot(dp_bb, wpb[...], NT)
        acc_pa[...] += _dot(y_a, dp_ab, TN)
        acc_pb[...] += _dot(y_b, dp_bb, TN)

        doa_ref[...] = (dy_a * silu_a).astype(BF16)
        dz_ref[:, 0:512] = (dy_a * o_a * (sg_a * (1.0 + g_a * (1.0 - sg_a)))).astype(BF16)
        dz_ref[:, 1536:2048] = (dy_b * um * (sg_b * (1.0 + g_b * (1.0 - sg_b)))).astype(BF16)
        dys = dy_b * silu_b
        dz_ref[:, 512:1024] = (dys * mixed * dgelu_u).astype(BF16)
        dmixed = dys * ug
        dmb = dmixed.astype(BF16)
        for n in range(n_sub):
            rows = slice(n * SGU_CHUNK, (n + 1) * SGU_CHUNK)
            for g in range(N_GROUPS):
                cols = slice(g * 128, (g + 1) * 128)
                dws_ref[g] += _dot(dmb[rows, cols], vn[rows, cols], NT)
                dbs_ref[g] += jnp.sum(dmixed[rows, cols], axis=-1, keepdims=True)
                dvn_s[rows, cols] = _dot(wmix[g], dmb[rows, cols], TN)
        dvn = dvn_s[...]
        dlng_ref[...] += jnp.sum(dvn * vhat, axis=0, keepdims=True)
        dlnb_ref[...] += jnp.sum(dvn, axis=0, keepdims=True)
        dvh = dvn * lng
        dvg = rstd * (dvh - jnp.mean(dvh, axis=-1, keepdims=True) - vhat * jnp.mean(dvh * vhat, axis=-1, keepdims=True))
        dz_ref[:, 1024:1536] = (dvg * dgelu_v).astype(BF16)

        @pl.when(i == nt - 1)
        def _():
            t_idx = lax.broadcasted_iota(jnp.int32, (SGU_CHUNK, SGU_CHUNK), 0)
            s_idx = lax.broadcasted_iota(jnp.int32, (SGU_CHUNK, SGU_CHUNK), 1)
            for g in range(N_GROUPS):
                dws_ref[g] = jnp.where(s_idx <= t_idx, dws_ref[g], 0.0)
            stores = [pltpu.make_async_copy(src, dst, sem.at[n])
                      for n, (src, dst) in enumerate(((acc_out, dwout_hbm), (acc_pa, dwpa_hbm), (acc_pb, dwpb_hbm)))]
            for cp in stores:
                cp.start()
            for cp in stores:
                cp.wait()

    tile = lambda w: pl.BlockSpec((tm, w), lambda i: (i, 0))
    whole = lambda shape: pl.BlockSpec(shape, lambda i: (0,) * len(shape))
    hbm = pl.BlockSpec(memory_space=pl.ANY)
    return pl.pallas_call(
        body, name="mid_fwd_bwd",
        grid=(nt,),
        in_specs=[tile(D_MODEL), tile(D_MODEL), tile(D_A)]
        + [pl.BlockSpec((tm, COL_BLOCK), functools.partial(lambda c, i: (i, c), c)) for c in range(3, N_COL_BLOCKS)]
        + [hbm, hbm, hbm,
                  whole((1, 2 * D_MODEL)), whole((1, D_B)), whole((1, D_B)),
                  whole((N_GROUPS, SGU_CHUNK, SGU_CHUNK)), whole((N_GROUPS, SGU_CHUNK, 1)), whole((1, D_MODEL))],
        out_specs=(tile(D_MODEL), tile(D_A), tile(REST), hbm, hbm, hbm,
                   whole((1, 2 * D_MODEL)), whole((1, D_MODEL)), whole((1, D_B)), whole((1, D_B)),
                   whole((N_GROUPS, SGU_CHUNK, SGU_CHUNK)), whole((N_GROUPS, SGU_CHUNK, 1)), whole((1, 1))),
        out_shape=(jax.ShapeDtypeStruct((s, D_MODEL), F32), jax.ShapeDtypeStruct((s, D_A), BF16),
                   jax.ShapeDtypeStruct((s, REST), BF16),
                   jax.ShapeDtypeStruct((D_MODEL, D_MODEL), F32), jax.ShapeDtypeStruct((D_A, D_MODEL), F32),
                   jax.ShapeDtypeStruct((D_B, D_MODEL), F32),
                   jax.ShapeDtypeStruct((1, 2 * D_MODEL), F32), jax.ShapeDtypeStruct((1, D_MODEL), F32),
                   jax.ShapeDtypeStruct((1, D_B), F32), jax.ShapeDtypeStruct((1, D_B), F32),
                   jax.ShapeDtypeStruct((N_GROUPS, SGU_CHUNK, SGU_CHUNK), F32),
                   jax.ShapeDtypeStruct((N_GROUPS, SGU_CHUNK, 1), F32), jax.ShapeDtypeStruct((1, 1), F32)),
        scratch_shapes=[pltpu.VMEM((D_A, D_MODEL), BF16), pltpu.VMEM((D_B, D_MODEL), BF16),
                        pltpu.VMEM((D_MODEL, D_MODEL), BF16), pltpu.VMEM((N_GROUPS, SGU_CHUNK, SGU_CHUNK), BF16),
                        pltpu.VMEM((D_MODEL, D_MODEL), F32), pltpu.VMEM((D_A, D_MODEL), F32),
                        pltpu.VMEM((D_B, D_MODEL), F32),
                        pltpu.VMEM((tm, D_B), F32), pltpu.VMEM((tm, D_B), F32),
                        pltpu.SemaphoreType.DMA((3,))],
        compiler_params=_params(56),
    )(x, target, attn_out, *([z] * (N_COL_BLOCKS - 3)), w_pa, w_pb, w_out, b_gate, ln_g, ln_b, w_s, b_s, final_g)


def _proj_bwd_x(dqkv, drest, x, dx2, norm_g, w_in, to_chip):
    s = x.shape[0]
    tm = TOKEN_TILE
    nt = s // tm
    n = len(to_chip)

    def body(*refs):
        dqkv_ref, dr_ref, x_ref, dx2_ref, g_ref, w_hbm = refs[:6]
        to_chip_refs = refs[6:6 + n]
        dx_ref, dg_ref = refs[6 + n:8 + n]
        from_chip_refs = refs[8 + n:8 + 2 * n]
        w, sem, send_sems, recv_sems = refs[8 + 2 * n:]
        i = pl.program_id(0)

        @pl.when(i == 0)
        def _():
            for rc in _owner_copies(to_chip_refs, from_chip_refs, send_sems, recv_sems):
                rc.start()
            cp = pltpu.make_async_copy(w_hbm, w, sem)
            cp.start()
            dg_ref[...] = jnp.zeros(dg_ref.shape, F32)
            cp.wait()

        dh = None
        for c in range(N_COL_BLOCKS):
            dz = dqkv_ref[c] if c < 3 else dr_ref[:, (c - 3) * COL_BLOCK:(c - 2) * COL_BLOCK]
            part = _dot(dz, w[:, c * COL_BLOCK:(c + 1) * COL_BLOCK], NT)
            dh = part if dh is None else dh + part
        xf = x_ref[...]
        r = lax.rsqrt(jnp.mean(xf * xf, axis=-1, keepdims=True) + EPS)
        xn = xf * r
        dg_ref[...] += jnp.sum(dh * xn, axis=0, keepdims=True)
        gh = dh * g_ref[...]
        dx_ref[...] = r * (gh - xn * jnp.mean(gh * xn, axis=-1, keepdims=True)) + dx2_ref[...]

        @pl.when(i == nt - 1)
        def _():
            for rc in _owner_copies(to_chip_refs, from_chip_refs, send_sems, recv_sems):
                rc.wait_recv()
                rc.wait_send()

    hbm = pl.BlockSpec(memory_space=pl.ANY)
    return pl.pallas_call(
        body, name="proj_bwd_x",
        grid=(nt,),
        in_specs=[pl.BlockSpec((3, tm, D_A), lambda i: (0, i, 0)),
                  pl.BlockSpec((tm, REST), lambda i: (i, 0)),
                  pl.BlockSpec((tm, D_MODEL), lambda i: (i, 0)),
                  pl.BlockSpec((tm, D_MODEL), lambda i: (i, 0)),
                  pl.BlockSpec((1, D_MODEL), lambda i: (0, 0)),
                  hbm] + [hbm] * n,
        out_specs=(pl.BlockSpec((tm, D_MODEL), lambda i: (i, 0)),
                   pl.BlockSpec((1, D_MODEL), lambda i: (0, 0))) + (hbm,) * n,
        out_shape=(jax.ShapeDtypeStruct((s, D_MODEL), F32), jax.ShapeDtypeStruct((1, D_MODEL), F32))
        + tuple(jax.ShapeDtypeStruct(t.shape, t.dtype) for t in to_chip),
        scratch_shapes=[pltpu.VMEM((D_MODEL, D_IN), BF16), pltpu.SemaphoreType.DMA,
                        pltpu.SemaphoreType.DMA((n, 3)), pltpu.SemaphoreType.DMA((n, 3))],
        compiler_params=_params(48),
    )(dqkv, drest, x, dx2, norm_g, w_in, *to_chip)


def _proj_bwd_w(h_t, dqkv, drest):
    s = h_t.shape[1]
    tk = min(s, 1024)
    nk = s // tk

    def body(ht_ref, dqkv_ref, dr_ref, o_ref, acc):
        j = pl.program_id(0)
        i = pl.program_id(1)

        @pl.when(i == 0)
        def _():
            acc[...] = jnp.zeros(acc.shape, F32)

        @pl.when(j < 3)
        def _():
            acc[...] += _dot(ht_ref[...], dqkv_ref[...])

        @pl.when(j >= 3)
        def _():
            acc[...] += _dot(ht_ref[...], dr_ref[...])

        @pl.when(i == nk - 1)
        def _():
            o_ref[...] = acc[...].astype(BF16)

    return pl.pallas_call(
        body, name="proj_bwd_w",
        grid=(N_COL_BLOCKS, nk),
        in_specs=[pl.BlockSpec((D_MODEL, tk), lambda j, i: (0, i)),
                  pl.BlockSpec((None, tk, COL_BLOCK),
                               lambda j, i: (jnp.minimum(j, 2), jnp.where(j < 3, i, nk - 1), 0)),
                  pl.BlockSpec((tk, COL_BLOCK),
                               lambda j, i: (jnp.where(j >= 3, i, 0), jnp.maximum(j - 3, 0)))],
        out_specs=pl.BlockSpec((D_MODEL, COL_BLOCK), lambda j, i: (0, j)),
        out_shape=jax.ShapeDtypeStruct((D_MODEL, D_IN), BF16),
        scratch_shapes=[pltpu.VMEM((D_MODEL, COL_BLOCK), F32)],
        compiler_params=_params(40),
    )(h_t, dqkv, drest)


def _adamw(name, w, g, m, v, from_chip=None):
    rows, cols = w.shape
    tr = rows if rows * cols <= 512 * 1024 else 256
    c1 = 1.0 - ADAM_B1 ** ADAM_STEP
    c2 = 1.0 - ADAM_B2 ** ADAM_STEP
    extra = [] if from_chip is None else [from_chip]

    def body(w_ref, g_ref, m_ref, v_ref, *rest):
        g_out, d_ref, nm_ref, nv_ref = rest[len(extra):]
        gg = g_ref[...]
        for t_ref in rest[:len(extra)]:
            for j in range(3):
                gg = gg + t_ref[j].astype(F32)
        nm = ADAM_B1 * m_ref[...] + (1.0 - ADAM_B1) * gg
        nv = ADAM_B2 * v_ref[...] + (1.0 - ADAM_B2) * (gg * gg)
        g_out[...] = gg
        d_ref[...] = -ADAM_LR * ((nm / c1) / (jnp.sqrt(nv / c2) + ADAM_EPS) + ADAM_WD * w_ref[...])
        nm_ref[...] = nm
        nv_ref[...] = nv

    spec = pl.BlockSpec((tr, cols), lambda i: (i, 0))
    shape = jax.ShapeDtypeStruct((rows, cols), F32)
    return pl.pallas_call(
        body, name=name,
        grid=(rows // tr,),
        in_specs=[spec] * 4 + [pl.BlockSpec((3, tr, cols), lambda i: (0, i, 0))] * len(extra),
        out_specs=(spec,) * 4, out_shape=(shape,) * 4,
        compiler_params=_params(32),
    )(w, g, m, v, *extra)


_SMALL = (("norm_g", D_MODEL), ("b_gate", 2 * D_MODEL), ("rel_bias", N_HEADS * N_REL_PAD), ("sgu_ln_g", D_B),
          ("sgu_ln_b", D_B), ("w_s", N_GROUPS * SGU_CHUNK * SGU_CHUNK), ("b_s", N_GROUPS * SGU_CHUNK),
          ("final_g", D_MODEL), ("loss", 1))


def _slab_rows(n):
    return -(-n // 1024) * 8


def _pack(parts):
    rows = []
    for (_, n), a in zip(_SMALL, parts):
        flat = a.reshape(-1).astype(F32)
        pad = _slab_rows(n) * 128 - flat.shape[0]
        rows.append(jnp.pad(flat, (0, pad)).reshape(-1, 128))
    return jnp.concatenate(rows, axis=0)


def _unpack(slab):
    out, r0 = {}, 0
    for name, n in _SMALL:
        nr = _slab_rows(n)
        out[name] = slab[r0:r0 + nr].reshape(-1)[:n]
        r0 += nr
    return out


def _pad_rel(a):
    return jnp.pad(a.reshape(N_HEADS, N_REL), ((0, 0), (0, N_REL_PAD - N_REL)))


def kernel(x, norm_g, w_in, b_gate, rel_bias, sgu_ln_g, sgu_ln_b, w_s, b_s, w_pa, w_pb, w_out, final_g, loss_target, m_norm_g, m_w_in, m_b_gate, m_rel_bias, m_sgu_ln_g, m_sgu_ln_b, m_w_s, m_b_s, m_w_pa, m_w_pb, m_w_out, m_final_g, v_norm_g, v_w_in, v_b_gate, v_rel_bias, v_sgu_ln_g, v_sgu_ln_b, v_w_s, v_b_s, v_w_pa, v_w_pb, v_w_out, v_final_g):
    s = x.shape[1]
    xs = x.reshape(s, D_MODEL)
    tgt = loss_target.reshape(s, D_MODEL)

    bias_table = _bias_table(_pad_rel(rel_bias))
    qkv, h_t, w_in_full = _gather_proj_fwd(xs, norm_g, w_in[0])
    attn_out, g_pa, g_pb, g_out = _attn_fwd(qkv, bias_table, (w_pa[0], w_pb[0], w_out[0]))
    w_pa_full = jnp.transpose(g_pa, (1, 0, 2)).reshape(D_A, D_MODEL)
    w_pb_full = jnp.transpose(g_pb, (1, 0, 2)).reshape(D_B, D_MODEL)
    w_out_full = g_out.reshape(D_MODEL, D_MODEL)

    (dx2, d_attn, drest, dw_out, dw_pa, dw_pb, d_bgate, d_fg, d_lng, d_lnb, d_ws, d_bs, loss_part) = _mid_fwd_bwd(
        xs, tgt, attn_out, qkv, w_pa_full, w_pb_full, w_out_full, b_gate, sgu_ln_g, sgu_ln_b, w_s[0],
        b_s.reshape(N_GROUPS, SGU_CHUNK, 1), final_g.reshape(1, D_MODEL))

    own_pa, own_pb, own_out, tc_pa, tc_pb, tc_out = _reduce_chip(
        "reduce_chip_proj", (dw_pa, dw_pb, dw_out), (1, 1, 0))
    dqkv, dbias, fc_pa, fc_pb, fc_out = _attn_bwd(qkv, bias_table, d_attn, (tc_pa, tc_pb, tc_out))
    d_rel = _bias_grad(dbias)
    dw_in = _proj_bwd_w(h_t, dqkv, drest)
    own_in, tc_in = _reduce_chip("reduce_chip_in", (dw_in,), (1,))
    grad_x, d_ng, fc_in = _proj_bwd_x(dqkv, drest, xs, dx2, norm_g, w_in_full, (tc_in,))
    big = {}
    for name, w, g, fc, m, v in (("w_in", w_in, own_in, fc_in, m_w_in, v_w_in),
                                 ("w_pa", w_pa, own_pa, fc_pa, m_w_pa, v_w_pa),
                                 ("w_pb", w_pb, own_pb, fc_pb, m_w_pb, v_w_pb),
                                 ("w_out", w_out, own_out, fc_out, m_w_out, v_w_out)):
        big[name] = tuple(t[None] for t in _adamw("adamw_" + name, w[0], g, m[0], v[0], fc))

    zero = jnp.zeros((1,), F32)
    g_slab = _reduce_small(_pack((d_ng, d_bgate, d_rel, d_lng, d_lnb, d_ws, d_bs, d_fg, loss_part)))
    small_in = lambda t: _pack(tuple(_pad_rel(a) if a.shape[-1] == N_REL else a for a in t) + (zero,))
    w_slab = small_in((norm_g, b_gate, rel_bias, sgu_ln_g, sgu_ln_b, w_s, b_s, final_g))
    m_slab = small_in((m_norm_g, m_b_gate, m_rel_bias, m_sgu_ln_g, m_sgu_ln_b, m_w_s, m_b_s, m_final_g))
    v_slab = small_in((v_norm_g, v_b_gate, v_rel_bias, v_sgu_ln_g, v_sgu_ln_b, v_w_s, v_b_s, v_final_g))
    _, d_slab, nm_slab, nv_slab = _adamw("adamw_small", w_slab, g_slab, m_slab, v_slab)
    small = [_unpack(t) for t in (g_slab, d_slab, nm_slab, nv_slab)]

    def leaf(kind, name, like):
        if name in big:
            return big[name][kind]
        a = small[kind][name]
        if name == "rel_bias":
            a = a.reshape(N_HEADS, N_REL_PAD)[:, :N_REL]
        return a.reshape(like.shape)

    weights = (("norm_g", norm_g), ("w_in", w_in), ("b_gate", b_gate), ("rel_bias", rel_bias), ("sgu_ln_g", sgu_ln_g),
               ("sgu_ln_b", sgu_ln_b), ("w_s", w_s), ("b_s", b_s), ("w_pa", w_pa), ("w_pb", w_pb), ("w_out", w_out),
               ("final_g", final_g))
    loss = small[0]["loss"].reshape(())
    outs = [loss, grad_x.reshape(x.shape)]
    for kind in range(4):
        outs.extend(leaf(kind, name, like) for name, like in weights)
    return tuple(outs)
```

```python
import functools
import math

import jax
import jax.numpy as jnp
from jax import lax
from jax.experimental import pallas as pl
from jax.experimental.pallas import tpu as pltpu

F32 = jnp.float32
BF16 = jnp.bfloat16
MESH = pl.DeviceIdType.MESH
N_DEV = 8

D_MODEL = 1024
D_A = 512
D_B = 512
D_IN = 5632
N_HEADS = 8
HEAD_DIM = 64
CHUNK = 64
N_PREV = 8
REL_CLIP = 128
N_REL = 2 * REL_CLIP + 1
N_REL_PAD = 384
SGU_CHUNK = 128
N_GROUPS = 4
EPS = 1e-6
NEG_INF = -1e30
Q_SCALE = HEAD_DIM ** -0.5

Q_BLOCK = 256
K_SPAN = 768
ROLL_W = 1024
COL_BLOCK = 512
N_COL_BLOCKS = D_IN // COL_BLOCK
REST = D_IN - 3 * D_A
TOKEN_TILE = 256
V7X_VMEM_BYTES = 64 * 1024 * 1024

ADAM_LR = 0.001
ADAM_B1 = 0.9
ADAM_B2 = 0.999
ADAM_EPS = 1e-08
ADAM_WD = 0.01
ADAM_STEP = 10

GELU_C = math.sqrt(2.0 / math.pi)
GELU_A = 0.044715

NT = (((1,), (1,)), ((), ()))
TN = (((0,), (0,)), ((), ()))
HIGHEST = lax.Precision.HIGHEST


def _params(vmem_mb, **kw):
    return pltpu.CompilerParams(vmem_limit_bytes=vmem_mb * 1024 * 1024, **kw)


def _dot(a, b, dims=None):
    if dims is None:
        return jnp.dot(a, b, preferred_element_type=F32)
    return lax.dot_general(a, b, dims, preferred_element_type=F32)


def _sigmoid(x):
    return 1.0 / (1.0 + jnp.exp(-x))


def _gelu_and_grad(u):
    u2 = u * u
    t = jnp.tanh(GELU_C * (u + GELU_A * u * u2))
    half = 0.5 * (1.0 + t)
    g = u * half
    dg = half + 0.5 * u * (1.0 - t * t) * (GELU_C * (1.0 + 3.0 * GELU_A * u2))
    return g, dg


def _my_pos():
    return lax.axis_index("x"), lax.axis_index("y"), lax.axis_index("c")


def _flat_id(pos):
    return 4 * pos[0] + 2 * pos[1] + pos[2]


def _peer(pos, k):
    x, y, c = pos
    return (1 - x if k & 4 else x, 1 - y if k & 2 else y, 1 - c if k & 1 else c)


def _other_chips(pos):
    x, y, _ = pos
    return ((1 - x, y), (x, 1 - y), (1 - x, 1 - y))


class _SlotGather:
    def __init__(self, bufs, send_sems, recv_sems, own=None):
        self.bufs, self.send_sems, self.recv_sems = bufs, send_sems, recv_sems
        self.own = own if own is not None else [None] * len(bufs)
        x, y, c = _my_pos()
        self.c, self.me, self.sib = c, (x, y, c), (x, y, 1 - c)
        self.chips = _other_chips(self.me)

    def _copy(self, a, k, block, to):
        slot = _flat_id(block)
        src = self.own[a] if (k < 4 and self.own[a] is not None) else self.bufs[a].at[slot]
        return pltpu.make_async_remote_copy(
            src_ref=src, dst_ref=self.bufs[a].at[slot],
            send_sem=self.send_sems.at[a, k], recv_sem=self.recv_sems.at[a, k], device_id=to, device_id_type=MESH)

    def _own_sends(self):
        n = len(self.bufs)
        return ([self._copy(a, 1 + j, self.me, (*chip, self.c)) for j, chip in enumerate(self.chips) for a in range(n)]
                + [self._copy(a, 0, self.me, self.sib) for a in range(n)])

    def _passes(self):
        return [self._copy(a, 4 + j, (*chip, self.c), self.sib)
                for j, chip in enumerate(self.chips) for a in range(len(self.bufs))]

    def start(self):
        for cp in self._own_sends():
            cp.start()

    def pass_on(self):
        for j, chip in enumerate(self.chips):
            for a in range(len(self.bufs)):
                self._copy(a, 1 + j, (*chip, self.c), self.me).wait_recv()
                self._copy(a, 4 + j, (*chip, self.c), self.sib).start()

    def finish(self):
        for a in range(len(self.bufs)):
            self._copy(a, 0, self.sib, self.me).wait_recv()
            for j, chip in enumerate(self.chips):
                self._copy(a, 4 + j, (*chip, 1 - self.c), self.me).wait_recv()
        for cp in self._own_sends() + self._passes():
            cp.wait_send()


def _reduce_chip(name, parts, sharded_dim):
    n = len(parts)
    shapes = []
    for p, dim in zip(parts, sharded_dim):
        shape = list(p.shape)
        shape[dim] //= N_DEV
        shapes.append(tuple(shape))

    def body(*refs):
        full, own, to_chip = refs[:n], refs[n:2 * n], refs[2 * n:3 * n]
        ins, from_sib = refs[3 * n:4 * n], refs[4 * n:5 * n]
        send_sems, recv_sems = refs[5 * n], refs[5 * n + 1]
        x, y, c = _my_pos()
        sib = (x, y, 1 - c)
        chips = ((x, y),) + _other_chips((x, y, c))
        for a in range(n):
            rows, cols = shapes[a]
            for d in range(N_DEV):
                if sharded_dim[a] == 0:
                    ins[a][d] = full[a][d * rows:(d + 1) * rows, :].astype(BF16)
                else:
                    ins[a][d] = full[a][:, d * cols:(d + 1) * cols].astype(BF16)

        def to_sibling(a, r):
            return pltpu.make_async_remote_copy(
                src_ref=ins[a].at[_flat_id((*chips[r], 1 - c))], dst_ref=from_sib[a].at[r],
                send_sem=send_sems.at[a, r], recv_sem=recv_sems.at[a, r], device_id=sib, device_id_type=MESH)

        sends = [to_sibling(a, r) for r in (1, 2, 3, 0) for a in range(n)]
        for cp in sends:
            cp.start()
        for r in (1, 2, 3, 0):
            for a in range(n):
                to_sibling(a, r).wait_recv()
                both = ins[a][_flat_id((*chips[r], c))].astype(F32) + from_sib[a][r].astype(F32)
                if r == 0:
                    own[a][...] = both
                else:
                    to_chip[a][r - 1] = both.astype(BF16)
        for cp in sends:
            cp.wait_send()

    vmem = pl.BlockSpec(memory_space=pltpu.VMEM)
    return pl.pallas_call(
        body, name=name,
        out_shape=tuple(jax.ShapeDtypeStruct(sh, F32) for sh in shapes)
        + tuple(jax.ShapeDtypeStruct((3,) + sh, BF16) for sh in shapes),
        in_specs=[vmem] * n, out_specs=tuple([vmem] * (2 * n)),
        scratch_shapes=[pltpu.VMEM((N_DEV,) + sh, BF16) for sh in shapes]
        + [pltpu.VMEM((4,) + sh, BF16) for sh in shapes]
        + [pltpu.SemaphoreType.DMA((n, 4)), pltpu.SemaphoreType.DMA((n, 4))],
        compiler_params=_params(56),
    )(*parts)


def _owner_copies(to_chip, from_chip, send_sems, recv_sems):
    x, y, c = _my_pos()
    return [pltpu.make_async_remote_copy(
        src_ref=to_chip[a].at[j], dst_ref=from_chip[a].at[j],
        send_sem=send_sems.at[a, j], recv_sem=recv_sems.at[a, j], device_id=(*chip, c), device_id_type=MESH)
        for a in range(len(to_chip)) for j, chip in enumerate(_other_chips((x, y, c)))]


ROW_NORM_G, ROW_B_GATE, ROW_LN_G, ROW_LN_B, ROW_FINAL_G, ROW_LOSS, ROW_REL, ROW_B_S, SLAB_ROWS = 0, 1, 3, 4, 5, 6, 8, 16, 24


def _reduce_small(d_ng, d_bgate, d_rel, d_lng, d_lnb, d_fg, loss, d_bs, d_ws):
    def body(ng_ref, bg_ref, rel_ref, lng_ref, lnb_ref, fg_ref, loss_ref, bs_ref, ws_ref, slab_out, ws_out,
             slab_land, ws_land, send_sems, recv_sems):
        me = _flat_id(_my_pos())
        slab_land[me] = jnp.zeros((SLAB_ROWS, D_MODEL), F32)
        slab_land[me, ROW_NORM_G:ROW_NORM_G + 1, :] = ng_ref[...]
        slab_land[me, ROW_B_GATE:ROW_B_GATE + 1, :] = bg_ref[:, :D_MODEL]
        slab_land[me, ROW_B_GATE + 1:ROW_B_GATE + 2, :] = bg_ref[:, D_MODEL:]
        slab_land[me, ROW_LN_G:ROW_LN_G + 1, :D_B] = lng_ref[...]
        slab_land[me, ROW_LN_B:ROW_LN_B + 1, :D_B] = lnb_ref[...]
        slab_land[me, ROW_FINAL_G:ROW_FINAL_G + 1, :] = fg_ref[...]
        slab_land[me, ROW_LOSS:ROW_LOSS + 1, :1] = loss_ref[...]
        slab_land[me, ROW_REL:ROW_REL + N_HEADS, :N_REL_PAD] = rel_ref[...]
        eye = (lax.broadcasted_iota(jnp.int32, (SGU_CHUNK, SGU_CHUNK), 0)
               == lax.broadcasted_iota(jnp.int32, (SGU_CHUNK, SGU_CHUNK), 1))
        for g in range(N_GROUPS):
            row = jnp.sum(jnp.where(eye, bs_ref[g], 0.0), axis=0, keepdims=True)
            slab_land[me, ROW_B_S + g:ROW_B_S + g + 1, :SGU_CHUNK] = row
        ws_land[me] = ws_ref[...]
        gather = _SlotGather([slab_land, ws_land], send_sems, recv_sems)
        gather.start()
        gather.pass_on()
        gather.finish()
        for land, out in ((slab_land, slab_out), (ws_land, ws_out)):
            acc = land[0]
            for d in range(1, N_DEV):
                acc = acc + land[d]
            out[...] = acc

    vmem = pl.BlockSpec(memory_space=pltpu.VMEM)
    ws_shape = (N_GROUPS * SGU_CHUNK, SGU_CHUNK)
    return pl.pallas_call(
        body, name="reduce_small",
        out_shape=(jax.ShapeDtypeStruct((SLAB_ROWS, D_MODEL), F32), jax.ShapeDtypeStruct(ws_shape, F32)),
        in_specs=[vmem] * 9, out_specs=(vmem, vmem),
        scratch_shapes=[pltpu.VMEM((N_DEV, SLAB_ROWS, D_MODEL), F32), pltpu.VMEM((N_DEV,) + ws_shape, F32),
                        pltpu.SemaphoreType.DMA((2, N_DEV - 1)), pltpu.SemaphoreType.DMA((2, N_DEV - 1))],
        compiler_params=_params(16),
    )(d_ng, d_bgate, d_rel, d_lng, d_lnb, d_fg, loss, d_bs, d_ws.reshape(ws_shape))


def _rel_index(e):
    return jnp.where(e <= 384, 2 * REL_CLIP, jnp.where(e < 640, 640 - e, jnp.where(e <= K_SPAN, 0, 2 * REL_CLIP)))


def _bias_table(rel_bias_pad):
    def body(rb_ref, bt_ref):
        c = lax.broadcasted_iota(jnp.int32, (N_REL_PAD, ROLL_W), 1)
        r = lax.broadcasted_iota(jnp.int32, (N_REL_PAD, ROLL_W), 0)
        pick = (r == _rel_index(c)).astype(F32)
        rows = jnp.dot(rb_ref[...], pick, precision=HIGHEST, preferred_element_type=F32)
        qc = lax.broadcasted_iota(jnp.int32, (Q_BLOCK, K_SPAN), 0) >> 6
        kc = lax.broadcasted_iota(jnp.int32, (Q_BLOCK, K_SPAN), 1) >> 6
        band = (kc >= qc) & (kc <= qc + N_PREV)
        for h in range(N_HEADS):
            t = jnp.broadcast_to(rows[h:h + 1, :], (Q_BLOCK, ROLL_W))
            t = pltpu.roll(t, 0, 1, stride=1, stride_axis=0)
            bt_ref[h] = jnp.where(band, t[:, :K_SPAN], NEG_INF)

    return pl.pallas_call(
        body, name="bias_table",
        out_shape=jax.ShapeDtypeStruct((N_HEADS, Q_BLOCK, K_SPAN), F32),
        compiler_params=_params(32),
    )(rel_bias_pad)


def _bias_grad(dbias):
    def body(a_ref, o_ref):
        rr = lax.broadcasted_iota(jnp.int32, (Q_BLOCK, Q_BLOCK), 0)
        cc = lax.broadcasted_iota(jnp.int32, (Q_BLOCK, Q_BLOCK), 1)
        flip = (rr + cc == Q_BLOCK - 1).astype(F32)
        c = lax.broadcasted_iota(jnp.int32, (ROLL_W, N_REL_PAD), 0)
        r = lax.broadcasted_iota(jnp.int32, (ROLL_W, N_REL_PAD), 1)
        e = jnp.where(c >= Q_BLOCK - 1, c - (Q_BLOCK - 1), c + (ROLL_W - Q_BLOCK + 1))
        pick = (r == _rel_index(e)).astype(F32)
        sums = []
        for h in range(N_HEADS):
            a = jnp.dot(flip, a_ref[h], precision=HIGHEST, preferred_element_type=F32)
            a = jnp.concatenate([a, jnp.zeros((Q_BLOCK, ROLL_W - K_SPAN), F32)], axis=1)
            a = pltpu.roll(a, 0, 1, stride=1, stride_axis=0)
            sums.append(jnp.sum(a, axis=0, keepdims=True))
        diag = jnp.concatenate(sums, axis=0)
        o_ref[...] = jnp.dot(diag, pick, precision=HIGHEST, preferred_element_type=F32)

    return pl.pallas_call(
        body, name="bias_grad",
        out_shape=jax.ShapeDtypeStruct((N_HEADS, N_REL_PAD), F32),
        compiler_params=_params(32),
    )(dbias)


def _gather_proj_fwd(x, norm_g, w_in):
    s = x.shape[0]
    tm = 512 if s % 512 == 0 else TOKEN_TILE
    nt = s // tm
    shard_w = w_in.shape[1]
    chip_w = 2 * shard_w
    n_chips = N_DEV // 2

    def body(order_ref, x_ref, g_ref, win_hbm, z_ref, ht_ref, wnat_ref, slots, hb, win_f32, send_sems, recv_sems,
             load_sem):
        j = pl.program_id(0)
        i = pl.program_id(1)
        x_, y_, c_ = _my_pos()
        me, sib = (x_, y_, c_), (x_, y_, 1 - c_)
        near = _other_chips(me)
        pick = lambda a, b: tuple(jnp.where(c_ == 0, u, v) for u, v in zip(a, b))
        passed_from, passed_to = pick(near[0], near[1]), pick(near[1], near[0])

        def copy(k, block, to):
            slot = _flat_id(block)
            return pltpu.make_async_remote_copy(
                src_ref=slots.at[slot], dst_ref=slots.at[slot],
                send_sem=send_sems.at[k], recv_sem=recv_sems.at[k], device_id=to, device_id_type=MESH)

        def sends():
            return ([copy(0, me, sib), copy(1, me, (*near[0], c_)), copy(2, me, (*near[1], c_)),
                     copy(3, (*passed_from, c_), (*passed_to, c_))]
                    + [copy(4 + n, (*near[n], c_), sib) for n in range(3)])

        def assemble(chip):
            first = 2 * (2 * chip[0] + chip[1])
            wnat_ref[:, :shard_w] = slots[first]
            wnat_ref[:, shard_w:] = slots[first + 1]

        @pl.when((j == 0) & (i == 0))
        def _():
            load = pltpu.make_async_copy(win_hbm, win_f32, load_sem)
            load.start()
            load.wait()
            slots[_flat_id(me)] = win_f32[...].astype(BF16)
            for cp in sends()[:3]:
                cp.start()
            copy(0, sib, me).wait_recv()
            assemble((x_, y_))

        @pl.when((j == 1) & (i == 0))
        def _():
            copy(1, (*near[0], c_), me).wait_recv()
            copy(2, (*near[1], c_), me).wait_recv()
            for cp in sends()[3:6]:
                cp.start()
            copy(4, (*near[0], 1 - c_), me).wait_recv()
            assemble(near[0])

        @pl.when((j == 2) & (i == 0))
        def _():
            copy(5, (*near[1], 1 - c_), me).wait_recv()
            assemble(near[1])

        @pl.when((j == 3) & (i == 0))
        def _():
            copy(3, (*near[2], c_), me).wait_recv()
            copy(6, (*near[2], c_), sib).start()
            copy(6, (*near[2], 1 - c_), me).wait_recv()
            assemble(near[2])

        rows = pl.ds(pl.multiple_of(i * tm, tm), tm)

        @pl.when(j == 0)
        def _():
            xf = x_ref[...]
            r = lax.rsqrt(jnp.mean(xf * xf, axis=-1, keepdims=True) + EPS)
            hf = xf * r * g_ref[...]
            hb[rows, :] = hf.astype(BF16)
            ht_ref[...] = hf.T.astype(BF16)

        blk = _dot(hb[rows, :], wnat_ref[...])
        q_scale = jnp.where(order_ref[j] == 0, Q_SCALE, 1.0).astype(F32)
        z_ref[:, :D_A] = (blk[:, :D_A] * q_scale).astype(BF16)
        z_ref[:, D_A:] = blk[:, D_A:].astype(BF16)

        @pl.when((j == n_chips - 1) & (i == nt - 1))
        def _():
            for cp in sends():
                cp.wait_send()

    pos = _my_pos()
    order = jnp.stack([2 * cx + cy for cx, cy in ((pos[0], pos[1]),) + _other_chips(pos)]).astype(jnp.int32)
    first_pass = lambda j, i: jnp.where(j == 0, i, nt - 1)
    grid_spec = pltpu.PrefetchScalarGridSpec(
        num_scalar_prefetch=1,
        grid=(n_chips, nt),
        in_specs=[pl.BlockSpec((tm, D_MODEL), lambda j, i, o: (first_pass(j, i), 0)),
                  pl.BlockSpec((1, D_MODEL), lambda j, i, o: (0, 0)),
                  pl.BlockSpec(memory_space=pl.ANY)],
        out_specs=(pl.BlockSpec((tm, chip_w), lambda j, i, o: (i, o[j])),
                   pl.BlockSpec((D_MODEL, tm), lambda j, i, o: (0, first_pass(j, i))),
                   pl.BlockSpec((D_MODEL, chip_w), lambda j, i, o: (0, o[j]))),
        scratch_shapes=[pltpu.VMEM((N_DEV, D_MODEL, shard_w), BF16), pltpu.VMEM((s, D_MODEL), BF16),
                        pltpu.VMEM(w_in.shape, F32),
                        pltpu.SemaphoreType.DMA((N_DEV - 1,)), pltpu.SemaphoreType.DMA((N_DEV - 1,)),
                        pltpu.SemaphoreType.DMA])
    return pl.pallas_call(
        body, name="gather_proj_fwd",
        grid_spec=grid_spec,
        out_shape=(jax.ShapeDtypeStruct((s, D_IN), BF16), jax.ShapeDtypeStruct((D_MODEL, s), BF16),
                   jax.ShapeDtypeStruct((D_MODEL, D_IN), BF16)),
        compiler_params=_params(60),
    )(order, x, norm_g, w_in)


def _attn_specs():
    pairs = N_HEADS // 2

    def kv(which, back):
        return pl.BlockSpec((Q_BLOCK, 128), lambda p, b: (jnp.maximum(b - back, 0), which * pairs + p))
    return ([pl.BlockSpec((Q_BLOCK, 128), lambda p, b: (b, p))]
            + [kv(1, back) for back in (2, 1, 0)] + [kv(2, back) for back in (2, 1, 0)]
            + [pl.BlockSpec((2, Q_BLOCK, K_SPAN), lambda p, b: (p, 0, 0))])


def _head_masks():
    lane = lax.broadcasted_iota(jnp.int32, (1, 128), 1)
    first = lane < HEAD_DIM
    return (first, jnp.logical_not(first))


def _softmax_rows(qm, kcat, bias, valid):
    s = _dot(qm, kcat, NT) + bias
    s = jnp.where(valid, s, NEG_INF)
    m = jnp.max(s, axis=-1, keepdims=True)
    e = jnp.exp(s - m)
    return e * (1.0 / jnp.sum(e, axis=-1, keepdims=True))


def _attn_fwd(qkv, bias_table, shards):
    s = qkv.shape[0]
    nb = s // Q_BLOCK
    n = len(shards)
    pairs = N_HEADS // 2

    def body(*refs):
        q_ref, k2_ref, k1_ref, k0_ref, v2_ref, v1_ref, v0_ref, bt_ref = refs[:8]
        shard_refs = refs[8:8 + n]
        o_ref = refs[8 + n]
        slot_refs = refs[9 + n:9 + 2 * n]
        stages = refs[9 + 2 * n:9 + 3 * n]
        send_sems, recv_sems, local_sems = refs[9 + 3 * n:]
        p_id = pl.program_id(0)
        b = pl.program_id(1)
        gather = _SlotGather(slot_refs, send_sems, recv_sems, own=stages)
        keep = [pltpu.make_async_copy(stages[a], slot_refs[a].at[_flat_id(_my_pos())], local_sems.at[a])
                for a in range(n)]

        @pl.when((p_id == 0) & (b == 0))
        def _():
            for a in range(n):
                stages[a][...] = shard_refs[a][...].astype(BF16)
                keep[a].start()
            gather.start()

        @pl.when((p_id == 1) & (b == 0))
        def _():
            gather.pass_on()

        q = q_ref[...]
        kcat = jnp.concatenate([k2_ref[...], k1_ref[...], k0_ref[...]], axis=0)
        vcat = jnp.concatenate([v2_ref[...], v1_ref[...], v0_ref[...]], axis=0)
        valid = lax.broadcasted_iota(jnp.int32, (1, K_SPAN), 1) >= (2 - b) * Q_BLOCK
        zero = jnp.zeros((), BF16)
        out = None
        for hh, mask in enumerate(_head_masks()):
            p = _softmax_rows(jnp.where(mask, q, zero), kcat, bt_ref[hh], valid)
            o = _dot(p.astype(BF16), jnp.where(mask, vcat, zero))
            out = o if out is None else out + o
        o_ref[...] = out

        @pl.when((p_id == pairs - 1) & (b == nb - 1))
        def _():
            gather.finish()
            for cp in keep:
                cp.wait()

    hbm = pl.BlockSpec(memory_space=pl.ANY)
    return pl.pallas_call(
        body, name="attn_fwd",
        grid=(pairs, nb),
        in_specs=_attn_specs() + [pl.BlockSpec(a.shape, lambda p, b: (0, 0)) for a in shards],
        out_specs=(pl.BlockSpec((Q_BLOCK, 128), lambda p, b: (b, p)),) + (hbm,) * n,
        out_shape=(jax.ShapeDtypeStruct((s, D_A), F32),)
        + tuple(jax.ShapeDtypeStruct((N_DEV,) + a.shape, BF16) for a in shards),
        scratch_shapes=[pltpu.VMEM(a.shape, BF16) for a in shards]
        + [pltpu.SemaphoreType.DMA((n, N_DEV - 1)), pltpu.SemaphoreType.DMA((n, N_DEV - 1)),
           pltpu.SemaphoreType.DMA((n,))],
        compiler_params=_params(40),
    )(qkv, qkv, qkv, qkv, qkv, qkv, qkv, bias_table, *shards)


def _attn_bwd(qkv, bias_table, d_out, to_chip):
    s = qkv.shape[0]
    nb = s // Q_BLOCK
    n = len(to_chip)

    def body(*refs):
        q_ref, k2_ref, k1_ref, k0_ref, v2_ref, v1_ref, v0_ref, bt_ref, do_ref = refs[:9]
        to_chip_refs = refs[9:9 + n]
        dqkv_ref, db_ref = refs[9 + n:11 + n]
        from_chip_refs = refs[11 + n:11 + 2 * n]
        dk_acc, dv_acc, send_sems, recv_sems = refs[11 + 2 * n:]
        p_id = pl.program_id(0)
        b = pl.program_id(1)

        @pl.when((p_id == 0) & (b == 0))
        def _():
            for cp in _owner_copies(to_chip_refs, from_chip_refs, send_sems, recv_sems):
                cp.start()

        @pl.when(b == 0)
        def _():
            dk_acc[...] = jnp.zeros(dk_acc.shape, F32)
            dv_acc[...] = jnp.zeros(dv_acc.shape, F32)
            db_ref[...] = jnp.zeros(db_ref.shape, F32)

        q = q_ref[...]
        do = do_ref[...]
        kcat = jnp.concatenate([k2_ref[...], k1_ref[...], k0_ref[...]], axis=0)
        vcat = jnp.concatenate([v2_ref[...], v1_ref[...], v0_ref[...]], axis=0)
        valid = lax.broadcasted_iota(jnp.int32, (1, K_SPAN), 1) >= (2 - b) * Q_BLOCK
        zero = jnp.zeros((), BF16)
        dq = dk = dv = None
        for hh, mask in enumerate(_head_masks()):
            qm = jnp.where(mask, q, zero)
            dom = jnp.where(mask, do, zero)
            p = _softmax_rows(qm, kcat, bt_ref[hh], valid)
            dp = _dot(dom, vcat, NT)
            ds = p * (dp - jnp.sum(p * dp, axis=-1, keepdims=True))
            db_ref[hh] += ds
            dsb = ds.astype(BF16)
            dq_h = _dot(dsb, jnp.where(mask, kcat, zero))
            dk_h = _dot(dsb, qm, TN)
            dv_h = _dot(p.astype(BF16), dom, TN)
            dq = dq_h if dq is None else dq + dq_h
            dk = dk_h if dk is None else dk + dk_h
            dv = dv_h if dv is None else dv + dv_h

        rows_b = pl.ds(pl.multiple_of(b * Q_BLOCK, Q_BLOCK), Q_BLOCK)
        dqkv_ref[0, rows_b, :] = (dq * Q_SCALE).astype(BF16)

        for j in range(3):
            slot = lax.rem(b + 1 + j, 3)
            part_k = dk[j * Q_BLOCK:(j + 1) * Q_BLOCK]
            part_v = dv[j * Q_BLOCK:(j + 1) * Q_BLOCK]
            if j == 2:
                dk_acc[slot] = part_k
                dv_acc[slot] = part_v
            else:
                dk_acc[slot] += part_k
                dv_acc[slot] += part_v

        def flush(block):
            slot = lax.rem(block + 3, 3)
            rows = pl.ds(pl.multiple_of(block * Q_BLOCK, Q_BLOCK), Q_BLOCK)
            dqkv_ref[1, rows, :] = dk_acc[slot].astype(BF16)
            dqkv_ref[2, rows, :] = dv_acc[slot].astype(BF16)

        @pl.when(b >= 2)
        def _():
            flush(b - 2)

        @pl.when(b == nb - 1)
        def _():
            flush(b - 1)
            flush(b)

        @pl.when((p_id == N_HEADS // 2 - 1) & (b == nb - 1))
        def _():
            for cp in _owner_copies(to_chip_refs, from_chip_refs, send_sems, recv_sems):
                cp.wait_recv()
                cp.wait_send()

    hbm = pl.BlockSpec(memory_space=pl.ANY)
    return pl.pallas_call(
        body, name="attn_bwd",
        grid=(N_HEADS // 2, nb),
        in_specs=_attn_specs() + [pl.BlockSpec((Q_BLOCK, 128), lambda p, b: (b, p))] + [hbm] * n,
        out_specs=(pl.BlockSpec((3, s, 128), lambda p, b: (0, 0, p)),
                   pl.BlockSpec((2, Q_BLOCK, K_SPAN), lambda p, b: (p, 0, 0))) + (hbm,) * n,
        out_shape=(jax.ShapeDtypeStruct((3, s, D_A), BF16),
                   jax.ShapeDtypeStruct((N_HEADS, Q_BLOCK, K_SPAN), F32))
        + tuple(jax.ShapeDtypeStruct(t.shape, t.dtype) for t in to_chip),
        scratch_shapes=[pltpu.VMEM((3, Q_BLOCK, 128), F32), pltpu.VMEM((3, Q_BLOCK, 128), F32),
                        pltpu.SemaphoreType.DMA((n, 3)), pltpu.SemaphoreType.DMA((n, 3))],
        compiler_params=_params(48),
    )(qkv, qkv, qkv, qkv, qkv, qkv, qkv, bias_table, d_out, *to_chip)


def _mid_fwd_bwd(x, target, attn_out, z, w_pa, w_pb, w_out, b_gate, ln_g, ln_b, w_s, b_s, final_g):
    s = x.shape[0]
    tm = TOKEN_TILE
    nt = s // tm
    n_sub = tm // SGU_CHUNK

    def body(x_ref, t_ref, oa_ref, ga_ref, ub_ref, vb_ref, gb_ref, ta0_ref, ta1_ref, tb0_ref, tb1_ref,
             wpa_hbm, wpb_hbm, wout_hbm, bg_ref, lng_ref, lnb_ref, ws_ref, bs_ref, fg_ref,
             dx2_ref, doa_ref, dz_ref, dwout_hbm, dwpa_hbm, dwpb_hbm, dbg_ref, dfg_ref, dlng_ref, dlnb_ref, dws_ref,
             dbs_ref, loss_ref,
             wpa, wpb, wout, wmix, acc_out, acc_pa, acc_pb, mixed_s, dvn_s, sem):
        i = pl.program_id(0)

        @pl.when(i == 0)
        def _():
            loads = [pltpu.make_async_copy(src, dst, sem.at[n])
                     for n, (src, dst) in enumerate(((wpa_hbm, wpa), (wpb_hbm, wpb), (wout_hbm, wout)))]
            for cp in loads:
                cp.start()
            t_idx = lax.broadcasted_iota(jnp.int32, (SGU_CHUNK, SGU_CHUNK), 0)
            s_idx = lax.broadcasted_iota(jnp.int32, (SGU_CHUNK, SGU_CHUNK), 1)
            for g in range(N_GROUPS):
                wmix[g] = jnp.where(s_idx <= t_idx, ws_ref[g], 0.0).astype(BF16)
            for ref in (acc_out, acc_pa, acc_pb, dbg_ref, dfg_ref, dlng_ref, dlnb_ref, dws_ref, dbs_ref, loss_ref):
                ref[...] = jnp.zeros(ref.shape, F32)
            for cp in loads:
                cp.wait()

        g_a = ga_ref[...].astype(F32)
        u_b = ub_ref[...].astype(F32)
        v_b = vb_ref[...].astype(F32)
        g_b = gb_ref[...].astype(F32)
        bg = bg_ref[...]

        sg_a = _sigmoid(g_a)
        silu_a = g_a * sg_a
        o_a = oa_ref[...]
        y_a = (o_a * silu_a).astype(BF16)

        ug, dgelu_u = _gelu_and_grad(u_b)
        vg, dgelu_v = _gelu_and_grad(v_b)
        mu = jnp.mean(vg, axis=-1, keepdims=True)
        vc = vg - mu
        rstd = lax.rsqrt(jnp.mean(vc * vc, axis=-1, keepdims=True) + EPS)
        vhat = vc * rstd
        lng = lng_ref[...]
        vn = (vhat * lng + lnb_ref[...]).astype(BF16)
        for n in range(n_sub):
            rows = slice(n * SGU_CHUNK, (n + 1) * SGU_CHUNK)
            for g in range(N_GROUPS):
                cols = slice(g * 128, (g + 1) * 128)
                mixed_s[rows, cols] = _dot(wmix[g], vn[rows, cols]) + bs_ref[g]
        mixed = mixed_s[...]
        sg_b = _sigmoid(g_b)
        silu_b = g_b * sg_b
        um = ug * mixed
        y_b = (um * silu_b).astype(BF16)

        p_a = _dot(y_a, wpa[...])
        p_b = _dot(y_b, wpb[...])
        gate_a = _sigmoid(jnp.concatenate([ta0_ref[...], ta1_ref[...]], axis=1).astype(F32) + bg[:, :D_MODEL])
        gate_b = _sigmoid(jnp.concatenate([tb0_ref[...], tb1_ref[...]], axis=1).astype(F32) + bg[:, D_MODEL:])
        merged = (gate_a * p_a + gate_b * p_b).astype(BF16)
        x2 = x_ref[...] + _dot(merged, wout[...])
        r2 = lax.rsqrt(jnp.mean(x2 * x2, axis=-1, keepdims=True) + EPS)
        xh = x2 * r2
        fg = fg_ref[...]
        err = xh * fg - t_ref[...]
        loss_ref[...] += jnp.sum(jnp.sum(err * err, axis=-1, keepdims=True), axis=0, keepdims=True) * (0.5 / D_MODEL)

        dy = err * (1.0 / D_MODEL)
        dfg_ref[...] += jnp.sum(dy * xh, axis=0, keepdims=True)
        gy = dy * fg
        dx2 = r2 * (gy - xh * jnp.mean(gy * xh, axis=-1, keepdims=True))
        dx2_ref[...] = dx2
        dx2b = dx2.astype(BF16)
        dmerged = _dot(dx2b, wout[...], NT)
        acc_out[...] += _dot(merged, dx2b, TN)

        dp_a = dmerged * gate_a
        dp_b = dmerged * gate_b
        dgate_a = dp_a * p_a * (1.0 - gate_a)
        dgate_b = dp_b * p_b * (1.0 - gate_b)
        dbg_ref[:, :D_MODEL] += jnp.sum(dgate_a, axis=0, keepdims=True)
        dbg_ref[:, D_MODEL:] += jnp.sum(dgate_b, axis=0, keepdims=True)
        dz_ref[:, 2048:3072] = dgate_a.astype(BF16)
        dz_ref[:, 3072:4096] = dgate_b.astype(BF16)
        dp_ab = dp_a.astype(BF16)
        dp_bb = dp_b.astype(BF16)
        dy_a = _dot(dp_ab, wpa[...], NT)
        dy_b = _dot(dp_bb, wpb[...], NT)
        acc_pa[...] += _dot(y_a, dp_ab, TN)
        acc_pb[...] += _dot(y_b, dp_bb, TN)

        doa_ref[...] = (dy_a * silu_a).astype(BF16)
        dz_ref[:, 0:512] = (dy_a * o_a * (sg_a * (1.0 + g_a * (1.0 - sg_a)))).astype(BF16)
        dz_ref[:, 1536:2048] = (dy_b * um * (sg_b * (1.0 + g_b * (1.0 - sg_b)))).astype(BF16)
        dys = dy_b * silu_b
        dz_ref[:, 512:1024] = (dys * mixed * dgelu_u).astype(BF16)
        dmixed = dys * ug
        dmb = dmixed.astype(BF16)
        for n in range(n_sub):
            rows = slice(n * SGU_CHUNK, (n + 1) * SGU_CHUNK)
            for g in range(N_GROUPS):
                cols = slice(g * 128, (g + 1) * 128)
                dws_ref[g] += _dot(dmb[rows, cols], vn[rows, cols], NT)
                dbs_ref[g] += jnp.sum(dmixed[rows, cols], axis=-1, keepdims=True)
                dvn_s[rows, cols] = _dot(wmix[g], dmb[rows, cols], TN)
        dvn = dvn_s[...]
        dlng_ref[...] += jnp.sum(dvn * vhat, axis=0, keepdims=True)
        dlnb_ref[...] += jnp.sum(dvn, axis=0, keepdims=True)
        dvh = dvn * lng
        dvg = rstd * (dvh - jnp.mean(dvh, axis=-1, keepdims=True) - vhat * jnp.mean(dvh * vhat, axis=-1, keepdims=True))
        dz_ref[:, 1024:1536] = (dvg * dgelu_v).astype(BF16)

        @pl.when(i == nt - 1)
        def _():
            t_idx = lax.broadcasted_iota(jnp.int32, (SGU_CHUNK, SGU_CHUNK), 0)
            s_idx = lax.broadcasted_iota(jnp.int32, (SGU_CHUNK, SGU_CHUNK), 1)
            for g in range(N_GROUPS):
                dws_ref[g] = jnp.where(s_idx <= t_idx, dws_ref[g], 0.0)
            stores = [pltpu.make_async_copy(src, dst, sem.at[n])
                      for n, (src, dst) in enumerate(((acc_out, dwout_hbm), (acc_pa, dwpa_hbm), (acc_pb, dwpb_hbm)))]
            for cp in stores:
                cp.start()
            for cp in stores:
                cp.wait()

    tile = lambda w: pl.BlockSpec((tm, w), lambda i: (i, 0))
    whole = lambda shape: pl.BlockSpec(shape, lambda i: (0,) * len(shape))
    hbm = pl.BlockSpec(memory_space=pl.ANY)
    return pl.pallas_call(
        body, name="mid_fwd_bwd",
        grid=(nt,),
        in_specs=[tile(D_MODEL), tile(D_MODEL), tile(D_A)]
        + [pl.BlockSpec((tm, COL_BLOCK), functools.partial(lambda c, i: (i, c), c)) for c in range(3, N_COL_BLOCKS)]
        + [hbm, hbm, hbm,
                  whole((1, 2 * D_MODEL)), whole((1, D_B)), whole((1, D_B)),
                  whole((N_GROUPS, SGU_CHUNK, SGU_CHUNK)), whole((N_GROUPS, SGU_CHUNK, 1)), whole((1, D_MODEL))],
        out_specs=(tile(D_MODEL), tile(D_A), tile(REST), hbm, hbm, hbm,
                   whole((1, 2 * D_MODEL)), whole((1, D_MODEL)), whole((1, D_B)), whole((1, D_B)),
                   whole((N_GROUPS, SGU_CHUNK, SGU_CHUNK)), whole((N_GROUPS, SGU_CHUNK, 1)), whole((1, 1))),
        out_shape=(jax.ShapeDtypeStruct((s, D_MODEL), F32), jax.ShapeDtypeStruct((s, D_A), BF16),
                   jax.ShapeDtypeStruct((s, REST), BF16),
                   jax.ShapeDtypeStruct((D_MODEL, D_MODEL), F32), jax.ShapeDtypeStruct((D_A, D_MODEL), F32),
                   jax.ShapeDtypeStruct((D_B, D_MODEL), F32),
                   jax.ShapeDtypeStruct((1, 2 * D_MODEL), F32), jax.ShapeDtypeStruct((1, D_MODEL), F32),
                   jax.ShapeDtypeStruct((1, D_B), F32), jax.ShapeDtypeStruct((1, D_B), F32),
                   jax.ShapeDtypeStruct((N_GROUPS, SGU_CHUNK, SGU_CHUNK), F32),
                   jax.ShapeDtypeStruct((N_GROUPS, SGU_CHUNK, 1), F32), jax.ShapeDtypeStruct((1, 1), F32)),
        scratch_shapes=[pltpu.VMEM((D_A, D_MODEL), BF16), pltpu.VMEM((D_B, D_MODEL), BF16),
                        pltpu.VMEM((D_MODEL, D_MODEL), BF16), pltpu.VMEM((N_GROUPS, SGU_CHUNK, SGU_CHUNK), BF16),
                        pltpu.VMEM((D_MODEL, D_MODEL), F32), pltpu.VMEM((D_A, D_MODEL), F32),
                        pltpu.VMEM((D_B, D_MODEL), F32),
                        pltpu.VMEM((tm, D_B), F32), pltpu.VMEM((tm, D_B), F32),
                        pltpu.SemaphoreType.DMA((3,))],
        compiler_params=_params(56),
    )(x, target, attn_out, *([z] * (N_COL_BLOCKS - 3)), w_pa, w_pb, w_out, b_gate, ln_g, ln_b, w_s, b_s, final_g)


def _proj_bwd_x(dqkv, drest, x, dx2, norm_g, w_in, to_chip):
    s = x.shape[0]
    tm = TOKEN_TILE
    nt = s // tm
    n = len(to_chip)

    def body(*refs):
        dqkv_ref, dr_ref, x_ref, dx2_ref, g_ref, w_hbm = refs[:6]
        to_chip_refs = refs[6:6 + n]
        dx_ref, dg_ref = refs[6 + n:8 + n]
        from_chip_refs = refs[8 + n:8 + 2 * n]
        w, sem, send_sems, recv_sems = refs[8 + 2 * n:]
        i = pl.program_id(0)

        @pl.when(i == 0)
        def _():
            for rc in _owner_copies(to_chip_refs, from_chip_refs, send_sems, recv_sems):
                rc.start()
            cp = pltpu.make_async_copy(w_hbm, w, sem)
            cp.start()
            dg_ref[...] = jnp.zeros(dg_ref.shape, F32)
            cp.wait()

        dh = None
        for c in range(N_COL_BLOCKS):
            dz = dqkv_ref[c] if c < 3 else dr_ref[:, (c - 3) * COL_BLOCK:(c - 2) * COL_BLOCK]
            part = _dot(dz, w[:, c * COL_BLOCK:(c + 1) * COL_BLOCK], NT)
            dh = part if dh is None else dh + part
        xf = x_ref[...]
        r = lax.rsqrt(jnp.mean(xf * xf, axis=-1, keepdims=True) + EPS)
        xn = xf * r
        dg_ref[...] += jnp.sum(dh * xn, axis=0, keepdims=True)
        gh = dh * g_ref[...]
        dx_ref[...] = r * (gh - xn * jnp.mean(gh * xn, axis=-1, keepdims=True)) + dx2_ref[...]

        @pl.when(i == nt - 1)
        def _():
            for rc in _owner_copies(to_chip_refs, from_chip_refs, send_sems, recv_sems):
                rc.wait_recv()
                rc.wait_send()

    hbm = pl.BlockSpec(memory_space=pl.ANY)
    return pl.pallas_call(
        body, name="proj_bwd_x",
        grid=(nt,),
        in_specs=[pl.BlockSpec((3, tm, D_A), lambda i: (0, i, 0)),
                  pl.BlockSpec((tm, REST), lambda i: (i, 0)),
                  pl.BlockSpec((tm, D_MODEL), lambda i: (i, 0)),
                  pl.BlockSpec((tm, D_MODEL), lambda i: (i, 0)),
                  pl.BlockSpec((1, D_MODEL), lambda i: (0, 0)),
                  hbm] + [hbm] * n,
        out_specs=(pl.BlockSpec((tm, D_MODEL), lambda i: (i, 0)),
                   pl.BlockSpec((1, D_MODEL), lambda i: (0, 0))) + (hbm,) * n,
        out_shape=(jax.ShapeDtypeStruct((s, D_MODEL), F32), jax.ShapeDtypeStruct((1, D_MODEL), F32))
        + tuple(jax.ShapeDtypeStruct(t.shape, t.dtype) for t in to_chip),
        scratch_shapes=[pltpu.VMEM((D_MODEL, D_IN), BF16), pltpu.SemaphoreType.DMA,
                        pltpu.SemaphoreType.DMA((n, 3)), pltpu.SemaphoreType.DMA((n, 3))],
        compiler_params=_params(48),
    )(dqkv, drest, x, dx2, norm_g, w_in, *to_chip)


def _proj_bwd_w(h_t, dqkv, drest):
    s = h_t.shape[1]
    tk = min(s, 1024)
    nk = s // tk

    def body(ht_ref, dqkv_ref, dr_ref, o_ref, acc):
        j = pl.program_id(0)
        i = pl.program_id(1)

        @pl.when(i == 0)
        def _():
            acc[...] = jnp.zeros(acc.shape, F32)

        @pl.when(j < 3)
        def _():
            acc[...] += _dot(ht_ref[...], dqkv_ref[...])

        @pl.when(j >= 3)
        def _():
            acc[...] += _dot(ht_ref[...], dr_ref[...])

        @pl.when(i == nk - 1)
        def _():
            o_ref[...] = acc[...].astype(BF16)

    return pl.pallas_call(
        body, name="proj_bwd_w",
        grid=(N_COL_BLOCKS, nk),
        in_specs=[pl.BlockSpec((D_MODEL, tk), lambda j, i: (0, i)),
                  pl.BlockSpec((None, tk, COL_BLOCK),
                               lambda j, i: (jnp.minimum(j, 2), jnp.where(j < 3, i, nk - 1), 0)),
                  pl.BlockSpec((tk, COL_BLOCK),
                               lambda j, i: (jnp.where(j >= 3, i, 0), jnp.maximum(j - 3, 0)))],
        out_specs=pl.BlockSpec((D_MODEL, COL_BLOCK), lambda j, i: (0, j)),
        out_shape=jax.ShapeDtypeStruct((D_MODEL, D_IN), BF16),
        scratch_shapes=[pltpu.VMEM((D_MODEL, COL_BLOCK), F32)],
        compiler_params=_params(40),
    )(h_t, dqkv, drest)


def _adamw_math(w, g, m, v):
    c1 = 1.0 - ADAM_B1 ** ADAM_STEP
    c2 = 1.0 - ADAM_B2 ** ADAM_STEP
    nm = ADAM_B1 * m + (1.0 - ADAM_B1) * g
    nv = ADAM_B2 * v + (1.0 - ADAM_B2) * (g * g)
    return -ADAM_LR * ((nm / c1) / (jnp.sqrt(nv / c2) + ADAM_EPS) + ADAM_WD * w), nm, nv


def _adamw(name, w, g, m, v, from_chip):
    rows, cols = w.shape
    tr = rows if rows * cols <= 512 * 1024 else 256

    def body(w_ref, g_ref, m_ref, v_ref, t_ref, g_out, d_ref, nm_ref, nv_ref):
        gg = g_ref[...]
        for j in range(3):
            gg = gg + t_ref[j].astype(F32)
        g_out[...] = gg
        d_ref[...], nm_ref[...], nv_ref[...] = _adamw_math(w_ref[...], gg, m_ref[...], v_ref[...])

    spec = pl.BlockSpec((tr, cols), lambda i: (i, 0))
    shape = jax.ShapeDtypeStruct((rows, cols), F32)
    return pl.pallas_call(
        body, name=name,
        grid=(rows // tr,),
        in_specs=[spec] * 4 + [pl.BlockSpec((3, tr, cols), lambda i: (0, i, 0))],
        out_specs=(spec,) * 4, out_shape=(shape,) * 4,
        compiler_params=_params(32),
    )(w, g, m, v, from_chip)


_SMALL = (("norm_g", (1, D_MODEL)), ("b_gate", (1, 2 * D_MODEL)), ("rel_bias", (N_HEADS, N_REL)),
          ("sgu_ln_g", (1, D_B)), ("sgu_ln_b", (1, D_B)), ("w_s", (N_GROUPS * SGU_CHUNK, SGU_CHUNK)),
          ("b_s", (N_GROUPS, SGU_CHUNK)), ("final_g", (1, D_MODEL)))


def _adamw_small(slab, g_ws, weights, moments_m, moments_v):
    k = len(_SMALL)

    def grad_of(name, slab_ref, ws_ref):
        if name == "norm_g":
            return slab_ref[ROW_NORM_G:ROW_NORM_G + 1, :]
        if name == "b_gate":
            return jnp.concatenate([slab_ref[ROW_B_GATE:ROW_B_GATE + 1, :], slab_ref[ROW_B_GATE + 1:ROW_B_GATE + 2, :]],
                                   axis=1)
        if name == "rel_bias":
            return slab_ref[ROW_REL:ROW_REL + N_HEADS, :N_REL]
        if name == "sgu_ln_g":
            return slab_ref[ROW_LN_G:ROW_LN_G + 1, :D_B]
        if name == "sgu_ln_b":
            return slab_ref[ROW_LN_B:ROW_LN_B + 1, :D_B]
        if name == "w_s":
            return ws_ref[...]
        if name == "b_s":
            return slab_ref[ROW_B_S:ROW_B_S + N_GROUPS, :SGU_CHUNK]
        return slab_ref[ROW_FINAL_G:ROW_FINAL_G + 1, :]

    def body(*refs):
        slab_ref, ws_ref = refs[:2]
        w_refs, m_refs, v_refs = refs[2:2 + k], refs[2 + k:2 + 2 * k], refs[2 + 2 * k:2 + 3 * k]
        outs = refs[2 + 3 * k:]
        for n, (name, _) in enumerate(_SMALL):
            g = grad_of(name, slab_ref, ws_ref)
            outs[n][...] = g
            outs[k + n][...], outs[2 * k + n][...], outs[3 * k + n][...] = _adamw_math(
                w_refs[n][...], g, m_refs[n][...], v_refs[n][...])
        outs[4 * k][...] = slab_ref[ROW_LOSS:ROW_LOSS + 1, :1]

    vmem = pl.BlockSpec(memory_space=pltpu.VMEM)
    shapes = tuple(jax.ShapeDtypeStruct(shape, F32) for _, shape in _SMALL)
    return pl.pallas_call(
        body, name="adamw_small",
        out_shape=shapes * 4 + (jax.ShapeDtypeStruct((1, 1), F32),),
        in_specs=[vmem] * (2 + 3 * k), out_specs=tuple([vmem] * (4 * k + 1)),
        compiler_params=_params(16),
    )(slab, g_ws, *weights, *moments_m, *moments_v)


def _pad_rel(a):
    return jnp.pad(a.reshape(N_HEADS, N_REL), ((0, 0), (0, N_REL_PAD - N_REL)))


def kernel(x, norm_g, w_in, b_gate, rel_bias, sgu_ln_g, sgu_ln_b, w_s, b_s, w_pa, w_pb, w_out, final_g, loss_target, m_norm_g, m_w_in, m_b_gate, m_rel_bias, m_sgu_ln_g, m_sgu_ln_b, m_w_s, m_b_s, m_w_pa, m_w_pb, m_w_out, m_final_g, v_norm_g, v_w_in, v_b_gate, v_rel_bias, v_sgu_ln_g, v_sgu_ln_b, v_w_s, v_b_s, v_w_pa, v_w_pb, v_w_out, v_final_g):
    s = x.shape[1]
    xs = x.reshape(s, D_MODEL)
    tgt = loss_target.reshape(s, D_MODEL)

    bias_table = _bias_table(_pad_rel(rel_bias))
    qkv, h_t, w_in_full = _gather_proj_fwd(xs, norm_g, w_in[0])
    attn_out, g_pa, g_pb, g_out = _attn_fwd(qkv, bias_table, (w_pa[0], w_pb[0], w_out[0]))
    w_pa_full = jnp.transpose(g_pa, (1, 0, 2)).reshape(D_A, D_MODEL)
    w_pb_full = jnp.transpose(g_pb, (1, 0, 2)).reshape(D_B, D_MODEL)
    w_out_full = g_out.reshape(D_MODEL, D_MODEL)

    (dx2, d_attn, drest, dw_out, dw_pa, dw_pb, d_bgate, d_fg, d_lng, d_lnb, d_ws, d_bs, loss_part) = _mid_fwd_bwd(
        xs, tgt, attn_out, qkv, w_pa_full, w_pb_full, w_out_full, b_gate, sgu_ln_g, sgu_ln_b, w_s[0],
        b_s.reshape(N_GROUPS, SGU_CHUNK, 1), final_g.reshape(1, D_MODEL))

    own_pa, own_pb, own_out, tc_pa, tc_pb, tc_out = _reduce_chip(
        "reduce_chip_proj", (dw_pa, dw_pb, dw_out), (1, 1, 0))
    dqkv, dbias, fc_pa, fc_pb, fc_out = _attn_bwd(qkv, bias_table, d_attn, (tc_pa, tc_pb, tc_out))
    d_rel = _bias_grad(dbias)
    dw_in = _proj_bwd_w(h_t, dqkv, drest)
    own_in, tc_in = _reduce_chip("reduce_chip_in", (dw_in,), (1,))
    grad_x, d_ng, fc_in = _proj_bwd_x(dqkv, drest, xs, dx2, norm_g, w_in_full, (tc_in,))
    big = {}
    for name, w, g, fc, m, v in (("w_in", w_in, own_in, fc_in, m_w_in, v_w_in),
                                 ("w_pa", w_pa, own_pa, fc_pa, m_w_pa, v_w_pa),
                                 ("w_pb", w_pb, own_pb, fc_pb, m_w_pb, v_w_pb),
                                 ("w_out", w_out, own_out, fc_out, m_w_out, v_w_out)):
        big[name] = tuple(t[None] for t in _adamw("adamw_" + name, w[0], g, m[0], v[0], fc))

    slab, g_ws = _reduce_small(d_ng, d_bgate, d_rel, d_lng, d_lnb, d_fg, loss_part, d_bs, d_ws)
    as_2d = lambda leaves: [a.reshape(shape) for a, (_, shape) in zip(leaves, _SMALL)]
    small_out = _adamw_small(
        slab, g_ws, as_2d((norm_g, b_gate, rel_bias, sgu_ln_g, sgu_ln_b, w_s, b_s, final_g)),
        as_2d((m_norm_g, m_b_gate, m_rel_bias, m_sgu_ln_g, m_sgu_ln_b, m_w_s, m_b_s, m_final_g)),
        as_2d((v_norm_g, v_b_gate, v_rel_bias, v_sgu_ln_g, v_sgu_ln_b, v_w_s, v_b_s, v_final_g)))
    small_index = {name: n for n, (name, _) in enumerate(_SMALL)}

    def leaf(kind, name, like):
        if name in big:
            return big[name][kind]
        return small_out[kind * len(_SMALL) + small_index[name]].reshape(like.shape)

    weights = (("norm_g", norm_g), ("w_in", w_in), ("b_gate", b_gate), ("rel_bias", rel_bias), ("sgu_ln_g", sgu_ln_g),
               ("sgu_ln_b", sgu_ln_b), ("w_s", w_s), ("b_s", b_s), ("w_pa", w_pa), ("w_pb", w_pb), ("w_out", w_out),
               ("final_g", final_g))
    outs = [small_out[-1].reshape(()), grad_x.reshape(x.shape)]
    for kind in range(4):
        outs.extend(leaf(kind, name, like) for name, like in weights)
    return tuple(outs)
```

```python
import functools
import math

import jax
import jax.numpy as jnp
from jax import lax
from jax.experimental import pallas as pl
from jax.experimental.pallas import tpu as pltpu

F32 = jnp.float32
BF16 = jnp.bfloat16
MESH = pl.DeviceIdType.MESH
N_DEV = 8

D_MODEL = 1024
D_A = 512
D_B = 512
D_IN = 5632
N_HEADS = 8
HEAD_DIM = 64
CHUNK = 64
N_PREV = 8
REL_CLIP = 128
N_REL = 2 * REL_CLIP + 1
N_REL_PAD = 384
SGU_CHUNK = 128
N_GROUPS = 4
EPS = 1e-6
NEG_INF = -1e30
Q_SCALE = HEAD_DIM ** -0.5

Q_BLOCK = 256
K_SPAN = 768
Z_PAD = K_SPAN - Q_BLOCK
ROLL_W = 1024
COL_BLOCK = 512
N_COL_BLOCKS = D_IN // COL_BLOCK
REST = D_IN - 3 * D_A
TOKEN_TILE = 256
V7X_VMEM_BYTES = 64 * 1024 * 1024

ADAM_LR = 0.001
ADAM_B1 = 0.9
ADAM_B2 = 0.999
ADAM_EPS = 1e-08
ADAM_WD = 0.01
ADAM_STEP = 10

GELU_C = math.sqrt(2.0 / math.pi)
GELU_A = 0.044715

NT = (((1,), (1,)), ((), ()))
TN = (((0,), (0,)), ((), ()))
HIGHEST = lax.Precision.HIGHEST


def _params(vmem_mb, **kw):
    return pltpu.CompilerParams(vmem_limit_bytes=vmem_mb * 1024 * 1024, **kw)


def _dot(a, b, dims=None):
    if dims is None:
        return jnp.dot(a, b, preferred_element_type=F32)
    return lax.dot_general(a, b, dims, preferred_element_type=F32)


def _sigmoid(x):
    return 1.0 / (1.0 + jnp.exp(-x))


def _gelu_and_grad(u):
    u2 = u * u
    t = jnp.tanh(GELU_C * (u + GELU_A * u * u2))
    half = 0.5 * (1.0 + t)
    g = u * half
    dg = half + 0.5 * u * (1.0 - t * t) * (GELU_C * (1.0 + 3.0 * GELU_A * u2))
    return g, dg


def _my_pos():
    return lax.axis_index("x"), lax.axis_index("y"), lax.axis_index("c")


def _flat_id(pos):
    return 4 * pos[0] + 2 * pos[1] + pos[2]


def _peer(pos, k):
    x, y, c = pos
    return (1 - x if k & 4 else x, 1 - y if k & 2 else y, 1 - c if k & 1 else c)


def _other_chips(pos):
    x, y, _ = pos
    return ((1 - x, y), (x, 1 - y), (1 - x, 1 - y))


class _SlotGather:
    def __init__(self, bufs, send_sems, recv_sems, own=None):
        self.bufs, self.send_sems, self.recv_sems = bufs, send_sems, recv_sems
        self.own = own if own is not None else [None] * len(bufs)
        x, y, c = _my_pos()
        self.c, self.me, self.sib = c, (x, y, c), (x, y, 1 - c)
        self.chips = _other_chips(self.me)

    def _copy(self, a, k, block, to):
        slot = _flat_id(block)
        src = self.own[a] if (k < 4 and self.own[a] is not None) else self.bufs[a].at[slot]
        return pltpu.make_async_remote_copy(
            src_ref=src, dst_ref=self.bufs[a].at[slot],
            send_sem=self.send_sems.at[a, k], recv_sem=self.recv_sems.at[a, k], device_id=to, device_id_type=MESH)

    def _own_sends(self):
        n = len(self.bufs)
        return ([self._copy(a, 1 + j, self.me, (*chip, self.c)) for j, chip in enumerate(self.chips) for a in range(n)]
                + [self._copy(a, 0, self.me, self.sib) for a in range(n)])

    def _passes(self):
        return [self._copy(a, 4 + j, (*chip, self.c), self.sib)
                for j, chip in enumerate(self.chips) for a in range(len(self.bufs))]

    def start(self):
        for cp in self._own_sends():
            cp.start()

    def pass_on(self):
        for j, chip in enumerate(self.chips):
            for a in range(len(self.bufs)):
                self._copy(a, 1 + j, (*chip, self.c), self.me).wait_recv()
                self._copy(a, 4 + j, (*chip, self.c), self.sib).start()

    def finish(self):
        for a in range(len(self.bufs)):
            self._copy(a, 0, self.sib, self.me).wait_recv()
            for j, chip in enumerate(self.chips):
                self._copy(a, 4 + j, (*chip, 1 - self.c), self.me).wait_recv()
        for cp in self._own_sends() + self._passes():
            cp.wait_send()


def _reduce_chip(name, parts, sharded_dim):
    n = len(parts)
    shapes = []
    for p, dim in zip(parts, sharded_dim):
        shape = list(p.shape)
        shape[dim] //= N_DEV
        shapes.append(tuple(shape))

    def body(*refs):
        full, own, to_chip = refs[:n], refs[n:2 * n], refs[2 * n:3 * n]
        ins, from_sib = refs[3 * n:4 * n], refs[4 * n:5 * n]
        send_sems, recv_sems = refs[5 * n], refs[5 * n + 1]
        x, y, c = _my_pos()
        sib = (x, y, 1 - c)
        chips = ((x, y),) + _other_chips((x, y, c))
        for a in range(n):
            rows, cols = shapes[a]
            for d in range(N_DEV):
                if sharded_dim[a] == 0:
                    ins[a][d] = full[a][d * rows:(d + 1) * rows, :].astype(BF16)
                else:
                    ins[a][d] = full[a][:, d * cols:(d + 1) * cols].astype(BF16)

        def to_sibling(a, r):
            return pltpu.make_async_remote_copy(
                src_ref=ins[a].at[_flat_id((*chips[r], 1 - c))], dst_ref=from_sib[a].at[r],
                send_sem=send_sems.at[a, r], recv_sem=recv_sems.at[a, r], device_id=sib, device_id_type=MESH)

        sends = [to_sibling(a, r) for r in (1, 2, 3, 0) for a in range(n)]
        for cp in sends:
            cp.start()
        for r in (1, 2, 3, 0):
            for a in range(n):
                to_sibling(a, r).wait_recv()
                both = ins[a][_flat_id((*chips[r], c))].astype(F32) + from_sib[a][r].astype(F32)
                if r == 0:
                    own[a][...] = both
                else:
                    to_chip[a][r - 1] = both.astype(BF16)
        for cp in sends:
            cp.wait_send()

    vmem = pl.BlockSpec(memory_space=pltpu.VMEM)
    return pl.pallas_call(
        body, name=name,
        out_shape=tuple(jax.ShapeDtypeStruct(sh, F32) for sh in shapes)
        + tuple(jax.ShapeDtypeStruct((3,) + sh, BF16) for sh in shapes),
        in_specs=[vmem] * n, out_specs=tuple([vmem] * (2 * n)),
        scratch_shapes=[pltpu.VMEM((N_DEV,) + sh, BF16) for sh in shapes]
        + [pltpu.VMEM((4,) + sh, BF16) for sh in shapes]
        + [pltpu.SemaphoreType.DMA((n, 4)), pltpu.SemaphoreType.DMA((n, 4))],
        compiler_params=_params(56),
    )(*parts)


def _owner_copies(to_chip, from_chip, send_sems, recv_sems):
    x, y, c = _my_pos()
    return [pltpu.make_async_remote_copy(
        src_ref=to_chip[a].at[j], dst_ref=from_chip[a].at[j],
        send_sem=send_sems.at[a, j], recv_sem=recv_sems.at[a, j], device_id=(*chip, c), device_id_type=MESH)
        for a in range(len(to_chip)) for j, chip in enumerate(_other_chips((x, y, c)))]


ROW_NORM_G, ROW_B_GATE, ROW_LN_G, ROW_LN_B, ROW_FINAL_G, ROW_LOSS, ROW_REL, ROW_B_S, SLAB_ROWS = 0, 1, 3, 4, 5, 6, 8, 16, 24


def _reduce_small(d_ng, d_bgate, d_rel, d_lng, d_lnb, d_fg, loss, d_bs, d_ws):
    def body(ng_ref, bg_ref, rel_ref, lng_ref, lnb_ref, fg_ref, loss_ref, bs_ref, ws_ref, slab_out, ws_out,
             slab_land, ws_land, send_sems, recv_sems):
        me = _flat_id(_my_pos())
        slab_land[me] = jnp.zeros((SLAB_ROWS, D_MODEL), F32)
        slab_land[me, ROW_NORM_G:ROW_NORM_G + 1, :] = ng_ref[...]
        slab_land[me, ROW_B_GATE:ROW_B_GATE + 1, :] = bg_ref[:, :D_MODEL]
        slab_land[me, ROW_B_GATE + 1:ROW_B_GATE + 2, :] = bg_ref[:, D_MODEL:]
        slab_land[me, ROW_LN_G:ROW_LN_G + 1, :D_B] = lng_ref[...]
        slab_land[me, ROW_LN_B:ROW_LN_B + 1, :D_B] = lnb_ref[...]
        slab_land[me, ROW_FINAL_G:ROW_FINAL_G + 1, :] = fg_ref[...]
        slab_land[me, ROW_LOSS:ROW_LOSS + 1, :1] = loss_ref[...]
        slab_land[me, ROW_REL:ROW_REL + N_HEADS, :N_REL_PAD] = rel_ref[...]
        eye = (lax.broadcasted_iota(jnp.int32, (SGU_CHUNK, SGU_CHUNK), 0)
               == lax.broadcasted_iota(jnp.int32, (SGU_CHUNK, SGU_CHUNK), 1))
        for g in range(N_GROUPS):
            row = jnp.sum(jnp.where(eye, bs_ref[g], 0.0), axis=0, keepdims=True)
            slab_land[me, ROW_B_S + g:ROW_B_S + g + 1, :SGU_CHUNK] = row
        ws_land[me] = ws_ref[...]
        gather = _SlotGather([slab_land, ws_land], send_sems, recv_sems)
        gather.start()
        gather.pass_on()
        gather.finish()
        for land, out in ((slab_land, slab_out), (ws_land, ws_out)):
            acc = land[0]
            for d in range(1, N_DEV):
                acc = acc + land[d]
            out[...] = acc

    vmem = pl.BlockSpec(memory_space=pltpu.VMEM)
    ws_shape = (N_GROUPS * SGU_CHUNK, SGU_CHUNK)
    return pl.pallas_call(
        body, name="reduce_small",
        out_shape=(jax.ShapeDtypeStruct((SLAB_ROWS, D_MODEL), F32), jax.ShapeDtypeStruct(ws_shape, F32)),
        in_specs=[vmem] * 9, out_specs=(vmem, vmem),
        scratch_shapes=[pltpu.VMEM((N_DEV, SLAB_ROWS, D_MODEL), F32), pltpu.VMEM((N_DEV,) + ws_shape, F32),
                        pltpu.SemaphoreType.DMA((2, N_DEV - 1)), pltpu.SemaphoreType.DMA((2, N_DEV - 1))],
        compiler_params=_params(16),
    )(d_ng, d_bgate, d_rel, d_lng, d_lnb, d_fg, loss, d_bs, d_ws.reshape(ws_shape))


def _rel_index(e):
    return jnp.where(e <= 384, 2 * REL_CLIP, jnp.where(e < 640, 640 - e, jnp.where(e <= K_SPAN, 0, 2 * REL_CLIP)))


def _bias_table(rel_bias_pad):
    def body(rb_ref, bt_ref):
        c = lax.broadcasted_iota(jnp.int32, (N_REL_PAD, ROLL_W), 1)
        r = lax.broadcasted_iota(jnp.int32, (N_REL_PAD, ROLL_W), 0)
        pick = (r == _rel_index(c)).astype(F32)
        rows = jnp.dot(rb_ref[...], pick, precision=HIGHEST, preferred_element_type=F32)
        qc = lax.broadcasted_iota(jnp.int32, (Q_BLOCK, K_SPAN), 0) >> 6
        kc = lax.broadcasted_iota(jnp.int32, (Q_BLOCK, K_SPAN), 1) >> 6
        band = (kc >= qc) & (kc <= qc + N_PREV)
        for h in range(N_HEADS):
            t = jnp.broadcast_to(rows[h:h + 1, :], (Q_BLOCK, ROLL_W))
            t = pltpu.roll(t, 0, 1, stride=1, stride_axis=0)
            bt_ref[h] = jnp.where(band, t[:, :K_SPAN], NEG_INF)

    return pl.pallas_call(
        body, name="bias_table",
        out_shape=jax.ShapeDtypeStruct((N_HEADS, Q_BLOCK, K_SPAN), F32),
        compiler_params=_params(32),
    )(rel_bias_pad)


def _bias_grad(dbias):
    def body(a_ref, o_ref):
        rr = lax.broadcasted_iota(jnp.int32, (Q_BLOCK, Q_BLOCK), 0)
        cc = lax.broadcasted_iota(jnp.int32, (Q_BLOCK, Q_BLOCK), 1)
        flip = (rr + cc == Q_BLOCK - 1).astype(F32)
        c = lax.broadcasted_iota(jnp.int32, (ROLL_W, N_REL_PAD), 0)
        r = lax.broadcasted_iota(jnp.int32, (ROLL_W, N_REL_PAD), 1)
        e = jnp.where(c >= Q_BLOCK - 1, c - (Q_BLOCK - 1), c + (ROLL_W - Q_BLOCK + 1))
        pick = (r == _rel_index(e)).astype(F32)
        sums = []
        for h in range(N_HEADS):
            a = jnp.dot(flip, a_ref[h], precision=HIGHEST, preferred_element_type=F32)
            a = jnp.concatenate([a, jnp.zeros((Q_BLOCK, ROLL_W - K_SPAN), F32)], axis=1)
            a = pltpu.roll(a, 0, 1, stride=1, stride_axis=0)
            sums.append(jnp.sum(a, axis=0, keepdims=True))
        diag = jnp.concatenate(sums, axis=0)
        o_ref[...] = jnp.dot(diag, pick, precision=HIGHEST, preferred_element_type=F32)

    return pl.pallas_call(
        body, name="bias_grad",
        out_shape=jax.ShapeDtypeStruct((N_HEADS, N_REL_PAD), F32),
        compiler_params=_params(32),
    )(dbias)


def _gather_proj_fwd(x, norm_g, w_in):
    s = x.shape[0]
    tm = 512 if s % 512 == 0 else TOKEN_TILE
    nt = s // tm
    n_pad = Z_PAD // tm
    shard_w = w_in.shape[1]
    chip_w = 2 * shard_w
    n_chips = N_DEV // 2

    def body(order_ref, x_ref, g_ref, win_hbm, z_ref, ht_ref, wnat_ref, slots, hb, win_f32, send_sems, recv_sems,
             load_sem):
        j = pl.program_id(0)
        i = pl.program_id(1)
        x_, y_, c_ = _my_pos()
        me, sib = (x_, y_, c_), (x_, y_, 1 - c_)
        near = _other_chips(me)
        pick = lambda a, b: tuple(jnp.where(c_ == 0, u, v) for u, v in zip(a, b))
        passed_from, passed_to = pick(near[0], near[1]), pick(near[1], near[0])

        def copy(k, block, to):
            slot = _flat_id(block)
            return pltpu.make_async_remote_copy(
                src_ref=slots.at[slot], dst_ref=slots.at[slot],
                send_sem=send_sems.at[k], recv_sem=recv_sems.at[k], device_id=to, device_id_type=MESH)

        def sends():
            return ([copy(0, me, sib), copy(1, me, (*near[0], c_)), copy(2, me, (*near[1], c_)),
                     copy(3, (*passed_from, c_), (*passed_to, c_))]
                    + [copy(4 + n, (*near[n], c_), sib) for n in range(3)])

        def assemble(chip):
            first = 2 * (2 * chip[0] + chip[1])
            wnat_ref[:, :shard_w] = slots[first]
            wnat_ref[:, shard_w:] = slots[first + 1]

        @pl.when((j == 0) & (i == 0))
        def _():
            load = pltpu.make_async_copy(win_hbm, win_f32, load_sem)
            load.start()
            load.wait()
            slots[_flat_id(me)] = win_f32[...].astype(BF16)
            for cp in sends()[:3]:
                cp.start()
            copy(0, sib, me).wait_recv()
            assemble((x_, y_))

        @pl.when((j == 1) & (i == 0))
        def _():
            copy(1, (*near[0], c_), me).wait_recv()
            copy(2, (*near[1], c_), me).wait_recv()
            for cp in sends()[3:6]:
                cp.start()
            copy(4, (*near[0], 1 - c_), me).wait_recv()
            assemble(near[0])

        @pl.when((j == 2) & (i == 0))
        def _():
            copy(5, (*near[1], 1 - c_), me).wait_recv()
            assemble(near[1])

        @pl.when((j == 3) & (i == 0))
        def _():
            copy(3, (*near[2], c_), me).wait_recv()
            copy(6, (*near[2], c_), sib).start()
            copy(6, (*near[2], 1 - c_), me).wait_recv()
            assemble(near[2])

        @pl.when(i < n_pad)
        def _():
            z_ref[...] = jnp.zeros(z_ref.shape, BF16)

        @pl.when(i >= n_pad)
        def _():
            rows = pl.ds(pl.multiple_of((i - n_pad) * tm, tm), tm)

            @pl.when(j == 0)
            def _():
                xf = x_ref[...]
                r = lax.rsqrt(jnp.mean(xf * xf, axis=-1, keepdims=True) + EPS)
                hf = xf * r * g_ref[...]
                hb[rows, :] = hf.astype(BF16)
                ht_ref[...] = hf.T.astype(BF16)

            blk = _dot(hb[rows, :], wnat_ref[...])
            q_scale = jnp.where(order_ref[j] == 0, Q_SCALE, 1.0).astype(F32)
            z_ref[:, :D_A] = (blk[:, :D_A] * q_scale).astype(BF16)
            z_ref[:, D_A:] = blk[:, D_A:].astype(BF16)

        @pl.when((j == n_chips - 1) & (i == n_pad + nt - 1))
        def _():
            for cp in sends():
                cp.wait_send()

    pos = _my_pos()
    order = jnp.stack([2 * cx + cy for cx, cy in ((pos[0], pos[1]),) + _other_chips(pos)]).astype(jnp.int32)
    first_pass = lambda j, i: jnp.where(j == 0, jnp.maximum(i - n_pad, 0), nt - 1)
    grid_spec = pltpu.PrefetchScalarGridSpec(
        num_scalar_prefetch=1,
        grid=(n_chips, n_pad + nt),
        in_specs=[pl.BlockSpec((tm, D_MODEL), lambda j, i, o: (first_pass(j, i), 0)),
                  pl.BlockSpec((1, D_MODEL), lambda j, i, o: (0, 0)),
                  pl.BlockSpec(memory_space=pl.ANY)],
        out_specs=(pl.BlockSpec((tm, chip_w), lambda j, i, o: (i, o[j])),
                   pl.BlockSpec((D_MODEL, tm), lambda j, i, o: (0, first_pass(j, i))),
                   pl.BlockSpec((D_MODEL, chip_w), lambda j, i, o: (0, o[j]))),
        scratch_shapes=[pltpu.VMEM((N_DEV, D_MODEL, shard_w), BF16), pltpu.VMEM((s, D_MODEL), BF16),
                        pltpu.VMEM(w_in.shape, F32),
                        pltpu.SemaphoreType.DMA((N_DEV - 1,)), pltpu.SemaphoreType.DMA((N_DEV - 1,)),
                        pltpu.SemaphoreType.DMA])
    return pl.pallas_call(
        body, name="gather_proj_fwd",
        grid_spec=grid_spec,
        out_shape=(jax.ShapeDtypeStruct((Z_PAD + s, D_IN), BF16), jax.ShapeDtypeStruct((D_MODEL, s), BF16),
                   jax.ShapeDtypeStruct((D_MODEL, D_IN), BF16)),
        compiler_params=_params(60),
    )(order, x, norm_g, w_in)


def _attn_specs(rows):
    pairs = N_HEADS // 2
    return ([pl.BlockSpec((rows, 128), functools.partial(lambda which, p: (0, which * pairs + p), which))
             for which in range(3)]
            + [pl.BlockSpec((2, Q_BLOCK, K_SPAN), lambda p: (p, 0, 0))])


def _head_masks():
    lane = lax.broadcasted_iota(jnp.int32, (1, 128), 1)
    first = lane < HEAD_DIM
    return (first, jnp.logical_not(first))


def _softmax_rows(qm, kcat, bias, valid):
    s = _dot(qm, kcat, NT) + bias
    s = jnp.where(valid, s, NEG_INF)
    m = jnp.max(s, axis=-1, keepdims=True)
    e = jnp.exp(s - m)
    return e * (1.0 / jnp.sum(e, axis=-1, keepdims=True))


def _attn_fwd(qkv, bias_table, shards):
    s = qkv.shape[0] - Z_PAD
    nb = s // Q_BLOCK
    n = len(shards)
    pairs = N_HEADS // 2

    def body(*refs):
        q_ref, k_ref, v_ref, bt_ref = refs[:4]
        shard_refs = refs[4:4 + n]
        o_ref = refs[4 + n]
        slot_refs = refs[5 + n:5 + 2 * n]
        stages = refs[5 + 2 * n:5 + 3 * n]
        send_sems, recv_sems, local_sems = refs[5 + 3 * n:]
        p_id = pl.program_id(0)
        gather = _SlotGather(slot_refs, send_sems, recv_sems, own=stages)
        keep = [pltpu.make_async_copy(stages[a], slot_refs[a].at[_flat_id(_my_pos())], local_sems.at[a])
                for a in range(n)]

        @pl.when(p_id == 0)
        def _():
            for a in range(n):
                stages[a][...] = shard_refs[a][...].astype(BF16)
                keep[a].start()
            gather.start()

        @pl.when(p_id == 1)
        def _():
            gather.pass_on()

        masks = _head_masks()
        zero = jnp.zeros((), BF16)

        def block(b, carry):
            r0 = pl.multiple_of(b * Q_BLOCK, Q_BLOCK)
            q = q_ref[pl.ds(r0 + Z_PAD, Q_BLOCK), :]
            kcat = k_ref[pl.ds(r0, K_SPAN), :]
            vcat = v_ref[pl.ds(r0, K_SPAN), :]
            valid = lax.broadcasted_iota(jnp.int32, (1, K_SPAN), 1) >= (2 - b) * Q_BLOCK
            out = None
            for hh, mask in enumerate(masks):
                p = _softmax_rows(jnp.where(mask, q, zero), kcat, bt_ref[hh], valid)
                o = _dot(p.astype(BF16), jnp.where(mask, vcat, zero))
                out = o if out is None else out + o
            o_ref[pl.ds(r0, Q_BLOCK), :] = out
            return carry

        lax.fori_loop(0, nb, block, 0, unroll=2)

        @pl.when(p_id == pairs - 1)
        def _():
            gather.finish()
            for cp in keep:
                cp.wait()

    hbm = pl.BlockSpec(memory_space=pl.ANY)
    return pl.pallas_call(
        body, name="attn_fwd",
        grid=(pairs,),
        in_specs=_attn_specs(s + Z_PAD) + [pl.BlockSpec(a.shape, lambda p: (0, 0)) for a in shards],
        out_specs=(pl.BlockSpec((s, 128), lambda p: (0, p)),) + (hbm,) * n,
        out_shape=(jax.ShapeDtypeStruct((s, D_A), F32),)
        + tuple(jax.ShapeDtypeStruct((N_DEV,) + a.shape, BF16) for a in shards),
        scratch_shapes=[pltpu.VMEM(a.shape, BF16) for a in shards]
        + [pltpu.SemaphoreType.DMA((n, N_DEV - 1)), pltpu.SemaphoreType.DMA((n, N_DEV - 1)),
           pltpu.SemaphoreType.DMA((n,))],
        compiler_params=_params(48),
    )(qkv, qkv, qkv, bias_table, *shards)


def _attn_bwd(qkv, bias_table, d_out, to_chip):
    s = qkv.shape[0] - Z_PAD
    nb = s // Q_BLOCK
    n = len(to_chip)
    pairs = N_HEADS // 2

    def body(*refs):
        q_ref, k_ref, v_ref, bt_ref, do_ref = refs[:5]
        to_chip_refs = refs[5:5 + n]
        dqkv_ref, db_ref = refs[5 + n:7 + n]
        from_chip_refs = refs[7 + n:7 + 2 * n]
        dk_acc, dv_acc, send_sems, recv_sems = refs[7 + 2 * n:]
        p_id = pl.program_id(0)

        @pl.when(p_id == 0)
        def _():
            for cp in _owner_copies(to_chip_refs, from_chip_refs, send_sems, recv_sems):
                cp.start()

        dk_acc[...] = jnp.zeros(dk_acc.shape, F32)
        dv_acc[...] = jnp.zeros(dv_acc.shape, F32)
        db_ref[...] = jnp.zeros(db_ref.shape, F32)
        masks = _head_masks()
        zero = jnp.zeros((), BF16)

        def block(b, carry):
            r0 = pl.multiple_of(b * Q_BLOCK, Q_BLOCK)
            q = q_ref[pl.ds(r0 + Z_PAD, Q_BLOCK), :]
            do = do_ref[pl.ds(r0, Q_BLOCK), :]
            kcat = k_ref[pl.ds(r0, K_SPAN), :]
            vcat = v_ref[pl.ds(r0, K_SPAN), :]
            valid = lax.broadcasted_iota(jnp.int32, (1, K_SPAN), 1) >= (2 - b) * Q_BLOCK
            dq = dk = dv = None
            for hh, mask in enumerate(masks):
                qm = jnp.where(mask, q, zero)
                dom = jnp.where(mask, do, zero)
                p = _softmax_rows(qm, kcat, bt_ref[hh], valid)
                dp = _dot(dom, vcat, NT)
                ds = p * (dp - jnp.sum(p * dp, axis=-1, keepdims=True))
                db_ref[hh] += ds
                dsb = ds.astype(BF16)
                dq_h = _dot(dsb, jnp.where(mask, kcat, zero))
                dk_h = _dot(dsb, qm, TN)
                dv_h = _dot(p.astype(BF16), dom, TN)
                dq = dq_h if dq is None else dq + dq_h
                dk = dk_h if dk is None else dk + dk_h
                dv = dv_h if dv is None else dv + dv_h
            dqkv_ref[0, pl.ds(r0, Q_BLOCK), :] = (dq * Q_SCALE).astype(BF16)
            dk_acc[pl.ds(r0, K_SPAN), :] += dk
            dv_acc[pl.ds(r0, K_SPAN), :] += dv
            return carry

        lax.fori_loop(0, nb, block, 0, unroll=2)
        dqkv_ref[1] = dk_acc[Z_PAD:, :].astype(BF16)
        dqkv_ref[2] = dv_acc[Z_PAD:, :].astype(BF16)

        @pl.when(p_id == pairs - 1)
        def _():
            for cp in _owner_copies(to_chip_refs, from_chip_refs, send_sems, recv_sems):
                cp.wait_recv()
                cp.wait_send()

    hbm = pl.BlockSpec(memory_space=pl.ANY)
    return pl.pallas_call(
        body, name="attn_bwd",
        grid=(pairs,),
        in_specs=_attn_specs(s + Z_PAD) + [pl.BlockSpec((s, 128), lambda p: (0, p))] + [hbm] * n,
        out_specs=(pl.BlockSpec((3, s, 128), lambda p: (0, 0, p)),
                   pl.BlockSpec((2, Q_BLOCK, K_SPAN), lambda p: (p, 0, 0))) + (hbm,) * n,
        out_shape=(jax.ShapeDtypeStruct((3, s, D_A), BF16),
                   jax.ShapeDtypeStruct((N_HEADS, Q_BLOCK, K_SPAN), F32))
        + tuple(jax.ShapeDtypeStruct(t.shape, t.dtype) for t in to_chip),
        scratch_shapes=[pltpu.VMEM((s + Z_PAD, 128), F32), pltpu.VMEM((s + Z_PAD, 128), F32),
                        pltpu.SemaphoreType.DMA((n, 3)), pltpu.SemaphoreType.DMA((n, 3))],
        compiler_params=_params(56),
    )(qkv, qkv, qkv, bias_table, d_out, *to_chip)


def _mid_fwd_bwd(x, target, attn_out, z, w_pa, w_pb, w_out, b_gate, ln_g, ln_b, w_s, b_s, final_g):
    s = x.shape[0]
    tm = TOKEN_TILE
    nt = s // tm
    n_sub = tm // SGU_CHUNK

    def body(x_ref, t_ref, oa_ref, ga_ref, ub_ref, vb_ref, gb_ref, ta0_ref, ta1_ref, tb0_ref, tb1_ref,
             wpa_hbm, wpb_hbm, wout_hbm, bg_ref, lng_ref, lnb_ref, ws_ref, bs_ref, fg_ref,
             dx2_ref, doa_ref, dz_ref, dwout_hbm, dwpa_hbm, dwpb_hbm, dbg_ref, dfg_ref, dlng_ref, dlnb_ref, dws_ref,
             dbs_ref, loss_ref,
             wpa, wpb, wout, wmix, acc_out, acc_pa, acc_pb, mixed_s, dvn_s, sem):
        i = pl.program_id(0)

        @pl.when(i == 0)
        def _():
            loads = [pltpu.make_async_copy(src, dst, sem.at[n])
                     for n, (src, dst) in enumerate(((wpa_hbm, wpa), (wpb_hbm, wpb), (wout_hbm, wout)))]
            for cp in loads:
                cp.start()
            t_idx = lax.broadcasted_iota(jnp.int32, (SGU_CHUNK, SGU_CHUNK), 0)
            s_idx = lax.broadcasted_iota(jnp.int32, (SGU_CHUNK, SGU_CHUNK), 1)
            for g in range(N_GROUPS):
                wmix[g] = jnp.where(s_idx <= t_idx, ws_ref[g], 0.0).astype(BF16)
            for ref in (acc_out, acc_pa, acc_pb, dbg_ref, dfg_ref, dlng_ref, dlnb_ref, dws_ref, dbs_ref, loss_ref):
                ref[...] = jnp.zeros(ref.shape, F32)
            for cp in loads:
                cp.wait()

        g_a = ga_ref[...].astype(F32)
        u_b = ub_ref[...].astype(F32)
        v_b = vb_ref[...].astype(F32)
        g_b = gb_ref[...].astype(F32)
        bg = bg_ref[...]

        sg_a = _sigmoid(g_a)
        silu_a = g_a * sg_a
        o_a = oa_ref[...]
        y_a = (o_a * silu_a).astype(BF16)

        ug, dgelu_u = _gelu_and_grad(u_b)
        vg, dgelu_v = _gelu_and_grad(v_b)
        mu = jnp.mean(vg, axis=-1, keepdims=True)
        vc = vg - mu
        rstd = lax.rsqrt(jnp.mean(vc * vc, axis=-1, keepdims=True) + EPS)
        vhat = vc * rstd
        lng = lng_ref[...]
        vn = (vhat * lng + lnb_ref[...]).astype(BF16)
        for n in range(n_sub):
            rows = slice(n * SGU_CHUNK, (n + 1) * SGU_CHUNK)
            for g in range(N_GROUPS):
                cols = slice(g * 128, (g + 1) * 128)
                mixed_s[rows, cols] = _dot(wmix[g], vn[rows, cols]) + bs_ref[g]
        mixed = mixed_s[...]
        sg_b = _sigmoid(g_b)
        silu_b = g_b * sg_b
        um = ug * mixed
        y_b = (um * silu_b).astype(BF16)

        p_a = _dot(y_a, wpa[...])
        p_b = _dot(y_b, wpb[...])
        gate_a = _sigmoid(jnp.concatenate([ta0_ref[...], ta1_ref[...]], axis=1).astype(F32) + bg[:, :D_MODEL])
        gate_b = _sigmoid(jnp.concatenate([tb0_ref[...], tb1_ref[...]], axis=1).astype(F32) + bg[:, D_MODEL:])
        merged = (gate_a * p_a + gate_b * p_b).astype(BF16)
        x2 = x_ref[...] + _dot(merged, wout[...])
        r2 = lax.rsqrt(jnp.mean(x2 * x2, axis=-1, keepdims=True) + EPS)
        xh = x2 * r2
        fg = fg_ref[...]
        err = xh * fg - t_ref[...]
        loss_ref[...] += jnp.sum(jnp.sum(err * err, axis=-1, keepdims=True), axis=0, keepdims=True) * (0.5 / D_MODEL)

        dy = err * (1.0 / D_MODEL)
        dfg_ref[...] += jnp.sum(dy * xh, axis=0, keepdims=True)
        gy = dy * fg
        dx2 = r2 * (gy - xh * jnp.mean(gy * xh, axis=-1, keepdims=True))
        dx2_ref[...] = dx2
        dx2b = dx2.astype(BF16)
        dmerged = _dot(dx2b, wout[...], NT)
        acc_out[...] += _dot(merged, dx2b, TN)

        dp_a = dmerged * gate_a
        dp_b = dmerged * gate_b
        dgate_a = dp_a * p_a * (1.0 - gate_a)
        dgate_b = dp_b * p_b * (1.0 - gate_b)
        dbg_ref[:, :D_MODEL] += jnp.sum(dgate_a, axis=0, keepdims=True)
        dbg_ref[:, D_MODEL:] += jnp.sum(dgate_b, axis=0, keepdims=True)
        dz_ref[:, 2048:3072] = dgate_a.astype(BF16)
        dz_ref[:, 3072:4096] = dgate_b.astype(BF16)
        dp_ab = dp_a.astype(BF16)
        dp_bb = dp_b.astype(BF16)
        dy_a = _dot(dp_ab, wpa[...], NT)
        dy_b = _dot(dp_bb, wpb[...], NT)
        acc_pa[...] += _dot(y_a, dp_ab, TN)
        acc_pb[...] += _dot(y_b, dp_bb, TN)

        doa_ref[...] = (dy_a * silu_a).astype(BF16)
        dz_ref[:, 0:512] = (dy_a * o_a * (sg_a * (1.0 + g_a * (1.0 - sg_a)))).astype(BF16)
        dz_ref[:, 1536:2048] = (dy_b * um * (sg_b * (1.0 + g_b * (1.0 - sg_b)))).astype(BF16)
        dys = dy_b * silu_b
        dz_ref[:, 512:1024] = (dys * mixed * dgelu_u).astype(BF16)
        dmixed = dys * ug
        dmb = dmixed.astype(BF16)
        for n in range(n_sub):
            rows = slice(n * SGU_CHUNK, (n + 1) * SGU_CHUNK)
            for g in range(N_GROUPS):
                cols = slice(g * 128, (g + 1) * 128)
                dws_ref[g] += _dot(dmb[rows, cols], vn[rows, cols], NT)
                dbs_ref[g] += jnp.sum(dmixed[rows, cols], axis=-1, keepdims=True)
                dvn_s[rows, cols] = _dot(wmix[g], dmb[rows, cols], TN)
        dvn = dvn_s[...]
        dlng_ref[...] += jnp.sum(dvn * vhat, axis=0, keepdims=True)
        dlnb_ref[...] += jnp.sum(dvn, axis=0, keepdims=True)
        dvh = dvn * lng
        dvg = rstd * (dvh - jnp.mean(dvh, axis=-1, keepdims=True) - vhat * jnp.mean(dvh * vhat, axis=-1, keepdims=True))
        dz_ref[:, 1024:1536] = (dvg * dgelu_v).astype(BF16)

        @pl.when(i == nt - 1)
        def _():
            t_idx = lax.broadcasted_iota(jnp.int32, (SGU_CHUNK, SGU_CHUNK), 0)
            s_idx = lax.broadcasted_iota(jnp.int32, (SGU_CHUNK, SGU_CHUNK), 1)
            for g in range(N_GROUPS):
                dws_ref[g] = jnp.where(s_idx <= t_idx, dws_ref[g], 0.0)
            stores = [pltpu.make_async_copy(src, dst, sem.at[n])
                      for n, (src, dst) in enumerate(((acc_out, dwout_hbm), (acc_pa, dwpa_hbm), (acc_pb, dwpb_hbm)))]
            for cp in stores:
                cp.start()
            for cp in stores:
                cp.wait()

    tile = lambda w: pl.BlockSpec((tm, w), lambda i: (i, 0))
    whole = lambda shape: pl.BlockSpec(shape, lambda i: (0,) * len(shape))
    hbm = pl.BlockSpec(memory_space=pl.ANY)
    return pl.pallas_call(
        body, name="mid_fwd_bwd",
        grid=(nt,),
        in_specs=[tile(D_MODEL), tile(D_MODEL), tile(D_A)]
        + [pl.BlockSpec((tm, COL_BLOCK), functools.partial(lambda c, i: (i + Z_PAD // tm, c), c))
           for c in range(3, N_COL_BLOCKS)]
        + [hbm, hbm, hbm,
                  whole((1, 2 * D_MODEL)), whole((1, D_B)), whole((1, D_B)),
                  whole((N_GROUPS, SGU_CHUNK, SGU_CHUNK)), whole((N_GROUPS, SGU_CHUNK, 1)), whole((1, D_MODEL))],
        out_specs=(tile(D_MODEL), tile(D_A), tile(REST), hbm, hbm, hbm,
                   whole((1, 2 * D_MODEL)), whole((1, D_MODEL)), whole((1, D_B)), whole((1, D_B)),
                   whole((N_GROUPS, SGU_CHUNK, SGU_CHUNK)), whole((N_GROUPS, SGU_CHUNK, 1)), whole((1, 1))),
        out_shape=(jax.ShapeDtypeStruct((s, D_MODEL), F32), jax.ShapeDtypeStruct((s, D_A), BF16),
                   jax.ShapeDtypeStruct((s, REST), BF16),
                   jax.ShapeDtypeStruct((D_MODEL, D_MODEL), F32), jax.ShapeDtypeStruct((D_A, D_MODEL), F32),
                   jax.ShapeDtypeStruct((D_B, D_MODEL), F32),
                   jax.ShapeDtypeStruct((1, 2 * D_MODEL), F32), jax.ShapeDtypeStruct((1, D_MODEL), F32),
                   jax.ShapeDtypeStruct((1, D_B), F32), jax.ShapeDtypeStruct((1, D_B), F32),
                   jax.ShapeDtypeStruct((N_GROUPS, SGU_CHUNK, SGU_CHUNK), F32),
                   jax.ShapeDtypeStruct((N_GROUPS, SGU_CHUNK, 1), F32), jax.ShapeDtypeStruct((1, 1), F32)),
        scratch_shapes=[pltpu.VMEM((D_A, D_MODEL), BF16), pltpu.VMEM((D_B, D_MODEL), BF16),
                        pltpu.VMEM((D_MODEL, D_MODEL), BF16), pltpu.VMEM((N_GROUPS, SGU_CHUNK, SGU_CHUNK), BF16),
                        pltpu.VMEM((D_MODEL, D_MODEL), F32), pltpu.VMEM((D_A, D_MODEL), F32),
                        pltpu.VMEM((D_B, D_MODEL), F32),
                        pltpu.VMEM((tm, D_B), F32), pltpu.VMEM((tm, D_B), F32),
                        pltpu.SemaphoreType.DMA((3,))],
        compiler_params=_params(56),
    )(x, target, attn_out, *([z] * (N_COL_BLOCKS - 3)), w_pa, w_pb, w_out, b_gate, ln_g, ln_b, w_s, b_s, final_g)


def _proj_bwd_x(dqkv, drest, x, dx2, norm_g, w_in, to_chip):
    s = x.shape[0]
    tm = TOKEN_TILE
    nt = s // tm
    n = len(to_chip)

    def body(*refs):
        dqkv_ref, dr_ref, x_ref, dx2_ref, g_ref, w_hbm = refs[:6]
        to_chip_refs = refs[6:6 + n]
        dx_ref, dg_ref = refs[6 + n:8 + n]
        from_chip_refs = refs[8 + n:8 + 2 * n]
        w, sem, send_sems, recv_sems = refs[8 + 2 * n:]
        i = pl.program_id(0)

        @pl.when(i == 0)
        def _():
            for rc in _owner_copies(to_chip_refs, from_chip_refs, send_sems, recv_sems):
                rc.start()
            cp = pltpu.make_async_copy(w_hbm, w, sem)
            cp.start()
            dg_ref[...] = jnp.zeros(dg_ref.shape, F32)
            cp.wait()

        dh = None
        for c in range(N_COL_BLOCKS):
            dz = dqkv_ref[c] if c < 3 else dr_ref[:, (c - 3) * COL_BLOCK:(c - 2) * COL_BLOCK]
            part = _dot(dz, w[:, c * COL_BLOCK:(c + 1) * COL_BLOCK], NT)
            dh = part if dh is None else dh + part
        xf = x_ref[...]
        r = lax.rsqrt(jnp.mean(xf * xf, axis=-1, keepdims=True) + EPS)
        xn = xf * r
        dg_ref[...] += jnp.sum(dh * xn, axis=0, keepdims=True)
        gh = dh * g_ref[...]
        dx_ref[...] = r * (gh - xn * jnp.mean(gh * xn, axis=-1, keepdims=True)) + dx2_ref[...]

        @pl.when(i == nt - 1)
        def _():
            for rc in _owner_copies(to_chip_refs, from_chip_refs, send_sems, recv_sems):
                rc.wait_recv()
                rc.wait_send()

    hbm = pl.BlockSpec(memory_space=pl.ANY)
    return pl.pallas_call(
        body, name="proj_bwd_x",
        grid=(nt,),
        in_specs=[pl.BlockSpec((3, tm, D_A), lambda i: (0, i, 0)),
                  pl.BlockSpec((tm, REST), lambda i: (i, 0)),
                  pl.BlockSpec((tm, D_MODEL), lambda i: (i, 0)),
                  pl.BlockSpec((tm, D_MODEL), lambda i: (i, 0)),
                  pl.BlockSpec((1, D_MODEL), lambda i: (0, 0)),
                  hbm] + [hbm] * n,
        out_specs=(pl.BlockSpec((tm, D_MODEL), lambda i: (i, 0)),
                   pl.BlockSpec((1, D_MODEL), lambda i: (0, 0))) + (hbm,) * n,
        out_shape=(jax.ShapeDtypeStruct((s, D_MODEL), F32), jax.ShapeDtypeStruct((1, D_MODEL), F32))
        + tuple(jax.ShapeDtypeStruct(t.shape, t.dtype) for t in to_chip),
        scratch_shapes=[pltpu.VMEM((D_MODEL, D_IN), BF16), pltpu.SemaphoreType.DMA,
                        pltpu.SemaphoreType.DMA((n, 3)), pltpu.SemaphoreType.DMA((n, 3))],
        compiler_params=_params(48),
    )(dqkv, drest, x, dx2, norm_g, w_in, *to_chip)


def _proj_bwd_w(h_t, dqkv, drest):
    s = h_t.shape[1]
    tk = min(s, 1024)
    nk = s // tk

    def body(ht_ref, dqkv_ref, dr_ref, o_ref, acc):
        j = pl.program_id(0)
        i = pl.program_id(1)

        @pl.when(i == 0)
        def _():
            acc[...] = jnp.zeros(acc.shape, F32)

        @pl.when(j < 3)
        def _():
            acc[...] += _dot(ht_ref[...], dqkv_ref[...])

        @pl.when(j >= 3)
        def _():
            acc[...] += _dot(ht_ref[...], dr_ref[...])

        @pl.when(i == nk - 1)
        def _():
            o_ref[...] = acc[...].astype(BF16)

    return pl.pallas_call(
        body, name="proj_bwd_w",
        grid=(N_COL_BLOCKS, nk),
        in_specs=[pl.BlockSpec((D_MODEL, tk), lambda j, i: (0, i)),
                  pl.BlockSpec((None, tk, COL_BLOCK),
                               lambda j, i: (jnp.minimum(j, 2), jnp.where(j < 3, i, nk - 1), 0)),
                  pl.BlockSpec((tk, COL_BLOCK),
                               lambda j, i: (jnp.where(j >= 3, i, 0), jnp.maximum(j - 3, 0)))],
        out_specs=pl.BlockSpec((D_MODEL, COL_BLOCK), lambda j, i: (0, j)),
        out_shape=jax.ShapeDtypeStruct((D_MODEL, D_IN), BF16),
        scratch_shapes=[pltpu.VMEM((D_MODEL, COL_BLOCK), F32)],
        compiler_params=_params(40),
    )(h_t, dqkv, drest)


def _adamw_math(w, g, m, v):
    c1 = 1.0 - ADAM_B1 ** ADAM_STEP
    c2 = 1.0 - ADAM_B2 ** ADAM_STEP
    nm = ADAM_B1 * m + (1.0 - ADAM_B1) * g
    nv = ADAM_B2 * v + (1.0 - ADAM_B2) * (g * g)
    return -ADAM_LR * ((nm / c1) / (jnp.sqrt(nv / c2) + ADAM_EPS) + ADAM_WD * w), nm, nv


def _adamw(name, w, g, m, v, from_chip):
    rows, cols = w.shape
    tr = rows if rows * cols <= 512 * 1024 else 256

    def body(w_ref, g_ref, m_ref, v_ref, t_ref, g_out, d_ref, nm_ref, nv_ref):
        gg = g_ref[...]
        for j in range(3):
            gg = gg + t_ref[j].astype(F32)
        g_out[...] = gg
        d_ref[...], nm_ref[...], nv_ref[...] = _adamw_math(w_ref[...], gg, m_ref[...], v_ref[...])

    spec = pl.BlockSpec((tr, cols), lambda i: (i, 0))
    shape = jax.ShapeDtypeStruct((rows, cols), F32)
    return pl.pallas_call(
        body, name=name,
        grid=(rows // tr,),
        in_specs=[spec] * 4 + [pl.BlockSpec((3, tr, cols), lambda i: (0, i, 0))],
        out_specs=(spec,) * 4, out_shape=(shape,) * 4,
        compiler_params=_params(32),
    )(w, g, m, v, from_chip)


_SMALL = (("norm_g", (1, D_MODEL)), ("b_gate", (1, 2 * D_MODEL)), ("rel_bias", (N_HEADS, N_REL)),
          ("sgu_ln_g", (1, D_B)), ("sgu_ln_b", (1, D_B)), ("w_s", (N_GROUPS * SGU_CHUNK, SGU_CHUNK)),
          ("b_s", (N_GROUPS, SGU_CHUNK)), ("final_g", (1, D_MODEL)))


def _adamw_small(slab, g_ws, weights, moments_m, moments_v):
    k = len(_SMALL)

    def grad_of(name, slab_ref, ws_ref):
        if name == "norm_g":
            return slab_ref[ROW_NORM_G:ROW_NORM_G + 1, :]
        if name == "b_gate":
            return jnp.concatenate([slab_ref[ROW_B_GATE:ROW_B_GATE + 1, :], slab_ref[ROW_B_GATE + 1:ROW_B_GATE + 2, :]],
                                   axis=1)
        if name == "rel_bias":
            return slab_ref[ROW_REL:ROW_REL + N_HEADS, :N_REL]
        if name == "sgu_ln_g":
            return slab_ref[ROW_LN_G:ROW_LN_G + 1, :D_B]
        if name == "sgu_ln_b":
            return slab_ref[ROW_LN_B:ROW_LN_B + 1, :D_B]
        if name == "w_s":
            return ws_ref[...]
        if name == "b_s":
            return slab_ref[ROW_B_S:ROW_B_S + N_GROUPS, :SGU_CHUNK]
        return slab_ref[ROW_FINAL_G:ROW_FINAL_G + 1, :]

    def body(*refs):
        slab_ref, ws_ref = refs[:2]
        w_refs, m_refs, v_refs = refs[2:2 + k], refs[2 + k:2 + 2 * k], refs[2 + 2 * k:2 + 3 * k]
        outs = refs[2 + 3 * k:]
        for n, (name, _) in enumerate(_SMALL):
            g = grad_of(name, slab_ref, ws_ref)
            outs[n][...] = g
            outs[k + n][...], outs[2 * k + n][...], outs[3 * k + n][...] = _adamw_math(
                w_refs[n][...], g, m_refs[n][...], v_refs[n][...])
        outs[4 * k][...] = slab_ref[ROW_LOSS:ROW_LOSS + 1, :1]

    vmem = pl.BlockSpec(memory_space=pltpu.VMEM)
    shapes = tuple(jax.ShapeDtypeStruct(shape, F32) for _, shape in _SMALL)
    return pl.pallas_call(
        body, name="adamw_small",
        out_shape=shapes * 4 + (jax.ShapeDtypeStruct((1, 1), F32),),
        in_specs=[vmem] * (2 + 3 * k), out_specs=tuple([vmem] * (4 * k + 1)),
        compiler_params=_params(16),
    )(slab, g_ws, *weights, *moments_m, *moments_v)


def _pad_rel(a):
    return jnp.pad(a.reshape(N_HEADS, N_REL), ((0, 0), (0, N_REL_PAD - N_REL)))


def kernel(x, norm_g, w_in, b_gate, rel_bias, sgu_ln_g, sgu_ln_b, w_s, b_s, w_pa, w_pb, w_out, final_g, loss_target, m_norm_g, m_w_in, m_b_gate, m_rel_bias, m_sgu_ln_g, m_sgu_ln_b, m_w_s, m_b_s, m_w_pa, m_w_pb, m_w_out, m_final_g, v_norm_g, v_w_in, v_b_gate, v_rel_bias, v_sgu_ln_g, v_sgu_ln_b, v_w_s, v_b_s, v_w_pa, v_w_pb, v_w_out, v_final_g):
    s = x.shape[1]
    xs = x.reshape(s, D_MODEL)
    tgt = loss_target.reshape(s, D_MODEL)

    bias_table = _bias_table(_pad_rel(rel_bias))
    qkv, h_t, w_in_full = _gather_proj_fwd(xs, norm_g, w_in[0])
    attn_out, g_pa, g_pb, g_out = _attn_fwd(qkv, bias_table, (w_pa[0], w_pb[0], w_out[0]))
    w_pa_full = jnp.transpose(g_pa, (1, 0, 2)).reshape(D_A, D_MODEL)
    w_pb_full = jnp.transpose(g_pb, (1, 0, 2)).reshape(D_B, D_MODEL)
    w_out_full = g_out.reshape(D_MODEL, D_MODEL)

    (dx2, d_attn, drest, dw_out, dw_pa, dw_pb, d_bgate, d_fg, d_lng, d_lnb, d_ws, d_bs, loss_part) = _mid_fwd_bwd(
        xs, tgt, attn_out, qkv, w_pa_full, w_pb_full, w_out_full, b_gate, sgu_ln_g, sgu_ln_b, w_s[0],
        b_s.reshape(N_GROUPS, SGU_CHUNK, 1), final_g.reshape(1, D_MODEL))

    own_pa, own_pb, own_out, tc_pa, tc_pb, tc_out = _reduce_chip(
        "reduce_chip_proj", (dw_pa, dw_pb, dw_out), (1, 1, 0))
    dqkv, dbias, fc_pa, fc_pb, fc_out = _attn_bwd(qkv, bias_table, d_attn, (tc_pa, tc_pb, tc_out))
    d_rel = _bias_grad(dbias)
    dw_in = _proj_bwd_w(h_t, dqkv, drest)
    own_in, tc_in = _reduce_chip("reduce_chip_in", (dw_in,), (1,))
    grad_x, d_ng, fc_in = _proj_bwd_x(dqkv, drest, xs, dx2, norm_g, w_in_full, (tc_in,))
    big = {}
    for name, w, g, fc, m, v in (("w_in", w_in, own_in, fc_in, m_w_in, v_w_in),
                                 ("w_pa", w_pa, own_pa, fc_pa, m_w_pa, v_w_pa),
                                 ("w_pb", w_pb, own_pb, fc_pb, m_w_pb, v_w_pb),
                                 ("w_out", w_out, own_out, fc_out, m_w_out, v_w_out)):
        big[name] = tuple(t[None] for t in _adamw("adamw_" + name, w[0], g, m[0], v[0], fc))

    slab, g_ws = _reduce_small(d_ng, d_bgate, d_rel, d_lng, d_lnb, d_fg, loss_part, d_bs, d_ws)
    as_2d = lambda leaves: [a.reshape(shape) for a, (_, shape) in zip(leaves, _SMALL)]
    small_out = _adamw_small(
        slab, g_ws, as_2d((norm_g, b_gate, rel_bias, sgu_ln_g, sgu_ln_b, w_s, b_s, final_g)),
        as_2d((m_norm_g, m_b_gate, m_rel_bias, m_sgu_ln_g, m_sgu_ln_b, m_w_s, m_b_s, m_final_g)),
        as_2d((v_norm_g, v_b_gate, v_rel_bias, v_sgu_ln_g, v_sgu_ln_b, v_w_s, v_b_s, v_final_g)))
    small_index = {name: n for n, (name, _) in enumerate(_SMALL)}

    def leaf(kind, name, like):
        if name in big:
            return big[name][kind]
        return small_out[kind * len(_SMALL) + small_index[name]].reshape(like.shape)

    weights = (("norm_g", norm_g), ("w_in", w_in), ("b_gate", b_gate), ("rel_bias", rel_bias), ("sgu_ln_g", sgu_ln_g),
               ("sgu_ln_b", sgu_ln_b), ("w_s", w_s), ("b_s", b_s), ("w_pa", w_pa), ("w_pb", w_pb), ("w_out", w_out),
               ("final_g", final_g))
    outs = [small_out[-1].reshape(()), grad_x.reshape(x.shape)]
    for kind in range(4):
        outs.extend(leaf(kind, name, like) for name, like in weights)
    return tuple(outs)
```

```python
import functools
import math

import jax
import jax.numpy as jnp
from jax import lax
from jax.experimental import pallas as pl
from jax.experimental.pallas import tpu as pltpu

F32 = jnp.float32
BF16 = jnp.bfloat16
MESH = pl.DeviceIdType.MESH
N_DEV = 8

D_MODEL = 1024
D_A = 512
D_B = 512
D_IN = 5632
N_HEADS = 8
HEAD_DIM = 64
CHUNK = 64
N_PREV = 8
REL_CLIP = 128
N_REL = 2 * REL_CLIP + 1
N_REL_PAD = 384
SGU_CHUNK = 128
N_GROUPS = 4
EPS = 1e-6
NEG_INF = -1e30
Q_SCALE = HEAD_DIM ** -0.5

Q_BLOCK = 256
K_SPAN = 768
Z_PAD = K_SPAN - Q_BLOCK
ROLL_W = 1024
COL_BLOCK = 512
N_COL_BLOCKS = D_IN // COL_BLOCK
REST = D_IN - 3 * D_A
TOKEN_TILE = 256
V7X_VMEM_BYTES = 64 * 1024 * 1024

ADAM_LR = 0.001
ADAM_B1 = 0.9
ADAM_B2 = 0.999
ADAM_EPS = 1e-08
ADAM_WD = 0.01
ADAM_STEP = 10

GELU_C = math.sqrt(2.0 / math.pi)
GELU_A = 0.044715

NT = (((1,), (1,)), ((), ()))
TN = (((0,), (0,)), ((), ()))
HIGHEST = lax.Precision.HIGHEST


def _params(vmem_mb, **kw):
    return pltpu.CompilerParams(vmem_limit_bytes=vmem_mb * 1024 * 1024, **kw)


def _dot(a, b, dims=None):
    if dims is None:
        return jnp.dot(a, b, preferred_element_type=F32)
    return lax.dot_general(a, b, dims, preferred_element_type=F32)


def _sigmoid(x):
    return 1.0 / (1.0 + jnp.exp(-x))


def _gelu_and_grad(u):
    u2 = u * u
    t = jnp.tanh(GELU_C * (u + GELU_A * u * u2))
    half = 0.5 * (1.0 + t)
    g = u * half
    dg = half + 0.5 * u * (1.0 - t * t) * (GELU_C * (1.0 + 3.0 * GELU_A * u2))
    return g, dg


def _my_pos():
    return lax.axis_index("x"), lax.axis_index("y"), lax.axis_index("c")


def _flat_id(pos):
    return 4 * pos[0] + 2 * pos[1] + pos[2]


def _peer(pos, k):
    x, y, c = pos
    return (1 - x if k & 4 else x, 1 - y if k & 2 else y, 1 - c if k & 1 else c)


def _other_chips(pos):
    x, y, _ = pos
    return ((1 - x, y), (x, 1 - y), (1 - x, 1 - y))


class _SlotGather:
    def __init__(self, bufs, send_sems, recv_sems, own=None):
        self.bufs, self.send_sems, self.recv_sems = bufs, send_sems, recv_sems
        self.own = own if own is not None else [None] * len(bufs)
        x, y, c = _my_pos()
        self.c, self.me, self.sib = c, (x, y, c), (x, y, 1 - c)
        self.chips = _other_chips(self.me)

    def _copy(self, a, k, block, to):
        slot = _flat_id(block)
        src = self.own[a] if (k < 4 and self.own[a] is not None) else self.bufs[a].at[slot]
        return pltpu.make_async_remote_copy(
            src_ref=src, dst_ref=self.bufs[a].at[slot],
            send_sem=self.send_sems.at[a, k], recv_sem=self.recv_sems.at[a, k], device_id=to, device_id_type=MESH)

    def _own_sends(self):
        n = len(self.bufs)
        return ([self._copy(a, 1 + j, self.me, (*chip, self.c)) for j, chip in enumerate(self.chips) for a in range(n)]
                + [self._copy(a, 0, self.me, self.sib) for a in range(n)])

    def _passes(self):
        return [self._copy(a, 4 + j, (*chip, self.c), self.sib)
                for j, chip in enumerate(self.chips) for a in range(len(self.bufs))]

    def start(self):
        for cp in self._own_sends():
            cp.start()

    def pass_on(self):
        for j, chip in enumerate(self.chips):
            for a in range(len(self.bufs)):
                self._copy(a, 1 + j, (*chip, self.c), self.me).wait_recv()
                self._copy(a, 4 + j, (*chip, self.c), self.sib).start()

    def finish(self):
        for a in range(len(self.bufs)):
            self._copy(a, 0, self.sib, self.me).wait_recv()
            for j, chip in enumerate(self.chips):
                self._copy(a, 4 + j, (*chip, 1 - self.c), self.me).wait_recv()
        for cp in self._own_sends() + self._passes():
            cp.wait_send()


def _reduce_chip(name, parts, sharded_dim):
    n = len(parts)
    shapes = []
    for p, dim in zip(parts, sharded_dim):
        shape = list(p.shape)
        shape[dim] //= N_DEV
        shapes.append(tuple(shape))

    def body(*refs):
        full, own, to_chip = refs[:n], refs[n:2 * n], refs[2 * n:3 * n]
        ins, from_sib = refs[3 * n:4 * n], refs[4 * n:5 * n]
        send_sems, recv_sems = refs[5 * n], refs[5 * n + 1]
        x, y, c = _my_pos()
        sib = (x, y, 1 - c)
        chips = ((x, y),) + _other_chips((x, y, c))
        for a in range(n):
            rows, cols = shapes[a]
            for d in range(N_DEV):
                if sharded_dim[a] == 0:
                    ins[a][d] = full[a][d * rows:(d + 1) * rows, :].astype(BF16)
                else:
                    ins[a][d] = full[a][:, d * cols:(d + 1) * cols].astype(BF16)

        def to_sibling(a, r):
            return pltpu.make_async_remote_copy(
                src_ref=ins[a].at[_flat_id((*chips[r], 1 - c))], dst_ref=from_sib[a].at[r],
                send_sem=send_sems.at[a, r], recv_sem=recv_sems.at[a, r], device_id=sib, device_id_type=MESH)

        sends = [to_sibling(a, r) for r in (1, 2, 3, 0) for a in range(n)]
        for cp in sends:
            cp.start()
        for r in (1, 2, 3, 0):
            for a in range(n):
                to_sibling(a, r).wait_recv()
                both = ins[a][_flat_id((*chips[r], c))].astype(F32) + from_sib[a][r].astype(F32)
                if r == 0:
                    own[a][...] = both
                else:
                    to_chip[a][r - 1] = both.astype(BF16)
        for cp in sends:
            cp.wait_send()

    vmem = pl.BlockSpec(memory_space=pltpu.VMEM)
    return pl.pallas_call(
        body, name=name,
        out_shape=tuple(jax.ShapeDtypeStruct(sh, F32) for sh in shapes)
        + tuple(jax.ShapeDtypeStruct((3,) + sh, BF16) for sh in shapes),
        in_specs=[vmem] * n, out_specs=tuple([vmem] * (2 * n)),
        scratch_shapes=[pltpu.VMEM((N_DEV,) + sh, BF16) for sh in shapes]
        + [pltpu.VMEM((4,) + sh, BF16) for sh in shapes]
        + [pltpu.SemaphoreType.DMA((n, 4)), pltpu.SemaphoreType.DMA((n, 4))],
        compiler_params=_params(56),
    )(*parts)


def _owner_copies(to_chip, from_chip, send_sems, recv_sems):
    x, y, c = _my_pos()
    return [pltpu.make_async_remote_copy(
        src_ref=to_chip[a].at[j], dst_ref=from_chip[a].at[j],
        send_sem=send_sems.at[a, j], recv_sem=recv_sems.at[a, j], device_id=(*chip, c), device_id_type=MESH)
        for a in range(len(to_chip)) for j, chip in enumerate(_other_chips((x, y, c)))]


ROW_NORM_G, ROW_B_GATE, ROW_LN_G, ROW_LN_B, ROW_FINAL_G, ROW_LOSS, ROW_REL, ROW_B_S, SLAB_ROWS = 0, 1, 3, 4, 5, 6, 8, 16, 24


def _reduce_small(d_ng, d_bgate, d_rel, d_lng, d_lnb, d_fg, loss, d_bs, d_ws):
    def body(ng_ref, bg_ref, rel_ref, lng_ref, lnb_ref, fg_ref, loss_ref, bs_ref, ws_ref, slab_out, ws_out,
             slab_land, ws_land, send_sems, recv_sems):
        me = _flat_id(_my_pos())
        slab_land[me] = jnp.zeros((SLAB_ROWS, D_MODEL), F32)
        slab_land[me, ROW_NORM_G:ROW_NORM_G + 1, :] = ng_ref[...]
        slab_land[me, ROW_B_GATE:ROW_B_GATE + 1, :] = bg_ref[:, :D_MODEL]
        slab_land[me, ROW_B_GATE + 1:ROW_B_GATE + 2, :] = bg_ref[:, D_MODEL:]
        slab_land[me, ROW_LN_G:ROW_LN_G + 1, :D_B] = lng_ref[...]
        slab_land[me, ROW_LN_B:ROW_LN_B + 1, :D_B] = lnb_ref[...]
        slab_land[me, ROW_FINAL_G:ROW_FINAL_G + 1, :] = fg_ref[...]
        slab_land[me, ROW_LOSS:ROW_LOSS + 1, :1] = loss_ref[...]
        slab_land[me, ROW_REL:ROW_REL + N_HEADS, :N_REL_PAD] = rel_ref[...]
        eye = (lax.broadcasted_iota(jnp.int32, (SGU_CHUNK, SGU_CHUNK), 0)
               == lax.broadcasted_iota(jnp.int32, (SGU_CHUNK, SGU_CHUNK), 1))
        for g in range(N_GROUPS):
            row = jnp.sum(jnp.where(eye, bs_ref[g], 0.0), axis=0, keepdims=True)
            slab_land[me, ROW_B_S + g:ROW_B_S + g + 1, :SGU_CHUNK] = row
        ws_land[me] = ws_ref[...]
        gather = _SlotGather([slab_land, ws_land], send_sems, recv_sems)
        gather.start()
        gather.pass_on()
        gather.finish()
        for land, out in ((slab_land, slab_out), (ws_land, ws_out)):
            acc = land[0]
            for d in range(1, N_DEV):
                acc = acc + land[d]
            out[...] = acc

    vmem = pl.BlockSpec(memory_space=pltpu.VMEM)
    ws_shape = (N_GROUPS * SGU_CHUNK, SGU_CHUNK)
    return pl.pallas_call(
        body, name="reduce_small",
        out_shape=(jax.ShapeDtypeStruct((SLAB_ROWS, D_MODEL), F32), jax.ShapeDtypeStruct(ws_shape, F32)),
        in_specs=[vmem] * 9, out_specs=(vmem, vmem),
        scratch_shapes=[pltpu.VMEM((N_DEV, SLAB_ROWS, D_MODEL), F32), pltpu.VMEM((N_DEV,) + ws_shape, F32),
                        pltpu.SemaphoreType.DMA((2, N_DEV - 1)), pltpu.SemaphoreType.DMA((2, N_DEV - 1))],
        compiler_params=_params(16),
    )(d_ng, d_bgate, d_rel, d_lng, d_lnb, d_fg, loss, d_bs, d_ws.reshape(ws_shape))


def _rel_index(e):
    return jnp.where(e <= 384, 2 * REL_CLIP, jnp.where(e < 640, 640 - e, jnp.where(e <= K_SPAN, 0, 2 * REL_CLIP)))


def _bias_table(rel_bias_pad):
    def body(rb_ref, bt_ref):
        c = lax.broadcasted_iota(jnp.int32, (N_REL_PAD, ROLL_W), 1)
        r = lax.broadcasted_iota(jnp.int32, (N_REL_PAD, ROLL_W), 0)
        pick = (r == _rel_index(c)).astype(F32)
        rows = jnp.dot(rb_ref[...], pick, precision=HIGHEST, preferred_element_type=F32)
        qc = lax.broadcasted_iota(jnp.int32, (Q_BLOCK, K_SPAN), 0) >> 6
        kc = lax.broadcasted_iota(jnp.int32, (Q_BLOCK, K_SPAN), 1) >> 6
        band = (kc >= qc) & (kc <= qc + N_PREV)
        for h in range(N_HEADS):
            t = jnp.broadcast_to(rows[h:h + 1, :], (Q_BLOCK, ROLL_W))
            t = pltpu.roll(t, 0, 1, stride=1, stride_axis=0)
            bt_ref[h] = jnp.where(band, t[:, :K_SPAN], NEG_INF)

    return pl.pallas_call(
        body, name="bias_table",
        out_shape=jax.ShapeDtypeStruct((N_HEADS, Q_BLOCK, K_SPAN), F32),
        compiler_params=_params(32),
    )(rel_bias_pad)


def _bias_grad(dbias):
    def body(a_ref, o_ref):
        rr = lax.broadcasted_iota(jnp.int32, (Q_BLOCK, Q_BLOCK), 0)
        cc = lax.broadcasted_iota(jnp.int32, (Q_BLOCK, Q_BLOCK), 1)
        flip = (rr + cc == Q_BLOCK - 1).astype(F32)
        c = lax.broadcasted_iota(jnp.int32, (ROLL_W, N_REL_PAD), 0)
        r = lax.broadcasted_iota(jnp.int32, (ROLL_W, N_REL_PAD), 1)
        e = jnp.where(c >= Q_BLOCK - 1, c - (Q_BLOCK - 1), c + (ROLL_W - Q_BLOCK + 1))
        pick = (r == _rel_index(e)).astype(F32)
        sums = []
        for h in range(N_HEADS):
            a = jnp.dot(flip, a_ref[h], precision=HIGHEST, preferred_element_type=F32)
            a = jnp.concatenate([a, jnp.zeros((Q_BLOCK, ROLL_W - K_SPAN), F32)], axis=1)
            a = pltpu.roll(a, 0, 1, stride=1, stride_axis=0)
            sums.append(jnp.sum(a, axis=0, keepdims=True))
        diag = jnp.concatenate(sums, axis=0)
        o_ref[...] = jnp.dot(diag, pick, precision=HIGHEST, preferred_element_type=F32)

    return pl.pallas_call(
        body, name="bias_grad",
        out_shape=jax.ShapeDtypeStruct((N_HEADS, N_REL_PAD), F32),
        compiler_params=_params(32),
    )(dbias)


def _gather_proj_fwd(x, norm_g, w_in_t):
    s = x.shape[0]
    tm = 512 if s % 512 == 0 else TOKEN_TILE
    nt = s // tm
    n_pad = Z_PAD // tm
    shard_w = w_in_t.shape[0]
    chip_w = 2 * shard_w
    n_chips = N_DEV // 2

    def body(order_ref, x_ref, g_ref, win_hbm, z_ref, h_ref, wt_hbm, stage, wchip, hb, win_f32, send_sems, recv_sems,
             local_sems):
        j = pl.program_id(0)
        i = pl.program_id(1)
        x_, y_, c_ = _my_pos()
        me, sib = (x_, y_, c_), (x_, y_, 1 - c_)
        near = _other_chips(me)
        pick = lambda a, b: tuple(jnp.where(c_ == 0, u, v) for u, v in zip(a, b))
        passed_from, passed_to = pick(near[0], near[1]), pick(near[1], near[0])

        def rows_of(block):
            return wt_hbm.at[pl.ds(pl.multiple_of(_flat_id(block) * shard_w, 16), shard_w), :]

        def copy(k, block, to, own=False):
            return pltpu.make_async_remote_copy(
                src_ref=stage if own else rows_of(block), dst_ref=rows_of(block),
                send_sem=send_sems.at[k], recv_sem=recv_sems.at[k], device_id=to, device_id_type=MESH)

        def sends():
            return ([copy(0, me, sib, True), copy(1, me, (*near[0], c_), True), copy(2, me, (*near[1], c_), True),
                     copy(3, (*passed_from, c_), (*passed_to, c_))]
                    + [copy(4 + n, (*near[n], c_), sib) for n in range(3)])

        keep = pltpu.make_async_copy(stage, rows_of(me), local_sems.at[0])

        def fetch(chip):
            first = pl.multiple_of((2 * chip[0] + chip[1]) * chip_w, 16)
            cp = pltpu.make_async_copy(wt_hbm.at[pl.ds(first, chip_w), :], wchip, local_sems.at[1])
            cp.start()
            cp.wait()

        @pl.when((j == 0) & (i == 0))
        def _():
            load = pltpu.make_async_copy(win_hbm, win_f32, local_sems.at[1])
            load.start()
            load.wait()
            stage[...] = win_f32[...].astype(BF16)
            keep.start()
            for cp in sends()[:3]:
                cp.start()
            copy(0, sib, me).wait_recv()
            keep.wait()
            fetch((x_, y_))

        @pl.when((j == 1) & (i == 0))
        def _():
            copy(1, (*near[0], c_), me).wait_recv()
            copy(2, (*near[1], c_), me).wait_recv()
            for cp in sends()[3:6]:
                cp.start()
            copy(4, (*near[0], 1 - c_), me).wait_recv()
            fetch(near[0])

        @pl.when((j == 2) & (i == 0))
        def _():
            copy(5, (*near[1], 1 - c_), me).wait_recv()
            fetch(near[1])

        @pl.when((j == 3) & (i == 0))
        def _():
            copy(3, (*near[2], c_), me).wait_recv()
            copy(6, (*near[2], c_), sib).start()
            copy(6, (*near[2], 1 - c_), me).wait_recv()
            fetch(near[2])

        @pl.when(i < n_pad)
        def _():
            z_ref[...] = jnp.zeros(z_ref.shape, BF16)

        @pl.when(i >= n_pad)
        def _():
            rows = pl.ds(pl.multiple_of((i - n_pad) * tm, tm), tm)

            @pl.when(j == 0)
            def _():
                xf = x_ref[...]
                r = lax.rsqrt(jnp.mean(xf * xf, axis=-1, keepdims=True) + EPS)
                hf = (xf * r * g_ref[...]).astype(BF16)
                hb[rows, :] = hf
                h_ref[...] = hf

            blk = _dot(hb[rows, :], wchip[...], NT)
            q_scale = jnp.where(order_ref[j] == 0, Q_SCALE, 1.0).astype(F32)
            z_ref[:, :D_A] = (blk[:, :D_A] * q_scale).astype(BF16)
            z_ref[:, D_A:] = blk[:, D_A:].astype(BF16)

        @pl.when((j == n_chips - 1) & (i == n_pad + nt - 1))
        def _():
            for cp in sends():
                cp.wait_send()

    pos = _my_pos()
    order = jnp.stack([2 * cx + cy for cx, cy in ((pos[0], pos[1]),) + _other_chips(pos)]).astype(jnp.int32)
    first_pass = lambda j, i: jnp.where(j == 0, jnp.maximum(i - n_pad, 0), nt - 1)
    grid_spec = pltpu.PrefetchScalarGridSpec(
        num_scalar_prefetch=1,
        grid=(n_chips, n_pad + nt),
        in_specs=[pl.BlockSpec((tm, D_MODEL), lambda j, i, o: (first_pass(j, i), 0)),
                  pl.BlockSpec((1, D_MODEL), lambda j, i, o: (0, 0)),
                  pl.BlockSpec(memory_space=pl.ANY)],
        out_specs=(pl.BlockSpec((tm, chip_w), lambda j, i, o: (i, o[j])),
                   pl.BlockSpec((tm, D_MODEL), lambda j, i, o: (first_pass(j, i), 0)),
                   pl.BlockSpec(memory_space=pl.ANY)),
        scratch_shapes=[pltpu.VMEM((shard_w, D_MODEL), BF16), pltpu.VMEM((chip_w, D_MODEL), BF16),
                        pltpu.VMEM((s, D_MODEL), BF16), pltpu.VMEM(w_in_t.shape, F32),
                        pltpu.SemaphoreType.DMA((N_DEV - 1,)), pltpu.SemaphoreType.DMA((N_DEV - 1,)),
                        pltpu.SemaphoreType.DMA((2,))])
    return pl.pallas_call(
        body, name="gather_proj_fwd",
        grid_spec=grid_spec,
        out_shape=(jax.ShapeDtypeStruct((Z_PAD + s, D_IN), BF16), jax.ShapeDtypeStruct((s, D_MODEL), BF16),
                   jax.ShapeDtypeStruct((D_IN, D_MODEL), BF16)),
        compiler_params=_params(60),
    )(order, x, norm_g, w_in_t)


def _attn_specs(rows):
    pairs = N_HEADS // 2
    return ([pl.BlockSpec((rows, 128), functools.partial(lambda which, p: (0, which * pairs + p), which))
             for which in range(3)]
            + [pl.BlockSpec((2, Q_BLOCK, K_SPAN), lambda p: (p, 0, 0))])


def _head_masks():
    lane = lax.broadcasted_iota(jnp.int32, (1, 128), 1)
    first = lane < HEAD_DIM
    return (first, jnp.logical_not(first))


def _softmax_rows(qm, kcat, bias, valid):
    s = _dot(qm, kcat, NT) + bias
    s = jnp.where(valid, s, NEG_INF)
    m = jnp.max(s, axis=-1, keepdims=True)
    e = jnp.exp(s - m)
    return e * (1.0 / jnp.sum(e, axis=-1, keepdims=True))


def _attn_fwd(qkv, bias_table, shards):
    s = qkv.shape[0] - Z_PAD
    nb = s // Q_BLOCK
    n = len(shards)
    pairs = N_HEADS // 2

    def body(*refs):
        q_ref, k_ref, v_ref, bt_ref = refs[:4]
        shard_refs = refs[4:4 + n]
        o_ref = refs[4 + n]
        slot_refs = refs[5 + n:5 + 2 * n]
        stages = refs[5 + 2 * n:5 + 3 * n]
        send_sems, recv_sems, local_sems = refs[5 + 3 * n:]
        p_id = pl.program_id(0)
        gather = _SlotGather(slot_refs, send_sems, recv_sems, own=stages)
        keep = [pltpu.make_async_copy(stages[a], slot_refs[a].at[_flat_id(_my_pos())], local_sems.at[a])
                for a in range(n)]

        @pl.when(p_id == 0)
        def _():
            for a in range(n):
                stages[a][...] = shard_refs[a][...].astype(BF16)
                keep[a].start()
            gather.start()

        @pl.when(p_id == 1)
        def _():
            gather.pass_on()

        masks = _head_masks()
        zero = jnp.zeros((), BF16)

        def block(b, carry):
            r0 = pl.multiple_of(b * Q_BLOCK, Q_BLOCK)
            q = q_ref[pl.ds(r0 + Z_PAD, Q_BLOCK), :]
            kcat = k_ref[pl.ds(r0, K_SPAN), :]
            vcat = v_ref[pl.ds(r0, K_SPAN), :]
            valid = lax.broadcasted_iota(jnp.int32, (1, K_SPAN), 1) >= (2 - b) * Q_BLOCK
            out = None
            for hh, mask in enumerate(masks):
                p = _softmax_rows(jnp.where(mask, q, zero), kcat, bt_ref[hh], valid)
                o = _dot(p.astype(BF16), jnp.where(mask, vcat, zero))
                out = o if out is None else out + o
            o_ref[pl.ds(r0, Q_BLOCK), :] = out
            return carry

        lax.fori_loop(0, nb, block, 0, unroll=2)

        @pl.when(p_id == pairs - 1)
        def _():
            gather.finish()
            for cp in keep:
                cp.wait()

    hbm = pl.BlockSpec(memory_space=pl.ANY)
    return pl.pallas_call(
        body, name="attn_fwd",
        grid=(pairs,),
        in_specs=_attn_specs(s + Z_PAD) + [pl.BlockSpec(a.shape, lambda p: (0, 0)) for a in shards],
        out_specs=(pl.BlockSpec((s, 128), lambda p: (0, p)),) + (hbm,) * n,
        out_shape=(jax.ShapeDtypeStruct((s, D_A), F32),)
        + tuple(jax.ShapeDtypeStruct((N_DEV,) + a.shape, BF16) for a in shards),
        scratch_shapes=[pltpu.VMEM(a.shape, BF16) for a in shards]
        + [pltpu.SemaphoreType.DMA((n, N_DEV - 1)), pltpu.SemaphoreType.DMA((n, N_DEV - 1)),
           pltpu.SemaphoreType.DMA((n,))],
        compiler_params=_params(48),
    )(qkv, qkv, qkv, bias_table, *shards)


def _attn_bwd(qkv, bias_table, d_out, to_chip):
    s = qkv.shape[0] - Z_PAD
    nb = s // Q_BLOCK
    n = len(to_chip)
    pairs = N_HEADS // 2

    def body(*refs):
        q_ref, k_ref, v_ref, bt_ref, do_ref = refs[:5]
        to_chip_refs = refs[5:5 + n]
        dqkv_ref, db_ref = refs[5 + n:7 + n]
        from_chip_refs = refs[7 + n:7 + 2 * n]
        dk_acc, dv_acc, send_sems, recv_sems = refs[7 + 2 * n:]
        p_id = pl.program_id(0)

        @pl.when(p_id == 0)
        def _():
            for cp in _owner_copies(to_chip_refs, from_chip_refs, send_sems, recv_sems):
                cp.start()

        dk_acc[...] = jnp.zeros(dk_acc.shape, F32)
        dv_acc[...] = jnp.zeros(dv_acc.shape, F32)
        db_ref[...] = jnp.zeros(db_ref.shape, F32)
        masks = _head_masks()
        zero = jnp.zeros((), BF16)

        def block(b, carry):
            r0 = pl.multiple_of(b * Q_BLOCK, Q_BLOCK)
            q = q_ref[pl.ds(r0 + Z_PAD, Q_BLOCK), :]
            do = do_ref[pl.ds(r0, Q_BLOCK), :]
            kcat = k_ref[pl.ds(r0, K_SPAN), :]
            vcat = v_ref[pl.ds(r0, K_SPAN), :]
            valid = lax.broadcasted_iota(jnp.int32, (1, K_SPAN), 1) >= (2 - b) * Q_BLOCK
            dq = dk = dv = None
            for hh, mask in enumerate(masks):
                qm = jnp.where(mask, q, zero)
                dom = jnp.where(mask, do, zero)
                p = _softmax_rows(qm, kcat, bt_ref[hh], valid)
                dp = _dot(dom, vcat, NT)
                ds = p * (dp - jnp.sum(p * dp, axis=-1, keepdims=True))
                db_ref[hh] += ds
                dsb = ds.astype(BF16)
                dq_h = _dot(dsb, jnp.where(mask, kcat, zero))
                dk_h = _dot(dsb, qm, TN)
                dv_h = _dot(p.astype(BF16), dom, TN)
                dq = dq_h if dq is None else dq + dq_h
                dk = dk_h if dk is None else dk + dk_h
                dv = dv_h if dv is None else dv + dv_h
            dqkv_ref[0, pl.ds(r0, Q_BLOCK), :] = (dq * Q_SCALE).astype(BF16)
            dk_acc[pl.ds(r0, K_SPAN), :] += dk
            dv_acc[pl.ds(r0, K_SPAN), :] += dv
            return carry

        lax.fori_loop(0, nb, block, 0, unroll=2)
        dqkv_ref[1] = dk_acc[Z_PAD:, :].astype(BF16)
        dqkv_ref[2] = dv_acc[Z_PAD:, :].astype(BF16)

        @pl.when(p_id == pairs - 1)
        def _():
            for cp in _owner_copies(to_chip_refs, from_chip_refs, send_sems, recv_sems):
                cp.wait_recv()
                cp.wait_send()

    hbm = pl.BlockSpec(memory_space=pl.ANY)
    return pl.pallas_call(
        body, name="attn_bwd",
        grid=(pairs,),
        in_specs=_attn_specs(s + Z_PAD) + [pl.BlockSpec((s, 128), lambda p: (0, p))] + [hbm] * n,
        out_specs=(pl.BlockSpec((3, s, 128), lambda p: (0, 0, p)),
                   pl.BlockSpec((2, Q_BLOCK, K_SPAN), lambda p: (p, 0, 0))) + (hbm,) * n,
        out_shape=(jax.ShapeDtypeStruct((3, s, D_A), BF16),
                   jax.ShapeDtypeStruct((N_HEADS, Q_BLOCK, K_SPAN), F32))
        + tuple(jax.ShapeDtypeStruct(t.shape, t.dtype) for t in to_chip),
        scratch_shapes=[pltpu.VMEM((s + Z_PAD, 128), F32), pltpu.VMEM((s + Z_PAD, 128), F32),
                        pltpu.SemaphoreType.DMA((n, 3)), pltpu.SemaphoreType.DMA((n, 3))],
        compiler_params=_params(56),
    )(qkv, qkv, qkv, bias_table, d_out, *to_chip)


def _mid_fwd_bwd(x, target, attn_out, z, w_pa, w_pb, w_out, b_gate, ln_g, ln_b, w_s, b_s, final_g):
    s = x.shape[0]
    tm = TOKEN_TILE
    nt = s // tm
    n_sub = tm // SGU_CHUNK

    def body(x_ref, t_ref, oa_ref, ga_ref, ub_ref, vb_ref, gb_ref, ta0_ref, ta1_ref, tb0_ref, tb1_ref,
             wpa_hbm, wpb_hbm, wout_hbm, bg_ref, lng_ref, lnb_ref, ws_ref, bs_ref, fg_ref,
             dx2_ref, doa_ref, dz_ref, dwout_hbm, dwpa_hbm, dwpb_hbm, dbg_ref, dfg_ref, dlng_ref, dlnb_ref, dws_ref,
             dbs_ref, loss_ref,
             wpa, wpb, wout, wmix, acc_out, acc_pa, acc_pb, mixed_s, dvn_s, sem):
        i = pl.program_id(0)

        @pl.when(i == 0)
        def _():
            loads = [pltpu.make_async_copy(src, dst, sem.at[n])
                     for n, (src, dst) in enumerate(((wpa_hbm, wpa), (wpb_hbm, wpb), (wout_hbm, wout)))]
            for cp in loads:
                cp.start()
            t_idx = lax.broadcasted_iota(jnp.int32, (SGU_CHUNK, SGU_CHUNK), 0)
            s_idx = lax.broadcasted_iota(jnp.int32, (SGU_CHUNK, SGU_CHUNK), 1)
            for g in range(N_GROUPS):
                wmix[g] = jnp.where(s_idx <= t_idx, ws_ref[g], 0.0).astype(BF16)
            for ref in (acc_out, acc_pa, acc_pb, dbg_ref, dfg_ref, dlng_ref, dlnb_ref, dws_ref, dbs_ref, loss_ref):
                ref[...] = jnp.zeros(ref.shape, F32)
            for cp in loads:
                cp.wait()

        g_a = ga_ref[...].astype(F32)
        u_b = ub_ref[...].astype(F32)
        v_b = vb_ref[...].astype(F32)
        g_b = gb_ref[...].astype(F32)
        bg = bg_ref[...]

        sg_a = _sigmoid(g_a)
        silu_a = g_a * sg_a
        o_a = oa_ref[...]
        y_a = (o_a * silu_a).astype(BF16)

        ug, dgelu_u = _gelu_and_grad(u_b)
        vg, dgelu_v = _gelu_and_grad(v_b)
        mu = jnp.mean(vg, axis=-1, keepdims=True)
        vc = vg - mu
        rstd = lax.rsqrt(jnp.mean(vc * vc, axis=-1, keepdims=True) + EPS)
        vhat = vc * rstd
        lng = lng_ref[...]
        vn = (vhat * lng + lnb_ref[...]).astype(BF16)
        for n in range(n_sub):
            rows = slice(n * SGU_CHUNK, (n + 1) * SGU_CHUNK)
            for g in range(N_GROUPS):
                cols = slice(g * 128, (g + 1) * 128)
                mixed_s[rows, cols] = _dot(wmix[g], vn[rows, cols]) + bs_ref[g]
        mixed = mixed_s[...]
        sg_b = _sigmoid(g_b)
        silu_b = g_b * sg_b
        um = ug * mixed
        y_b = (um * silu_b).astype(BF16)

        p_a = _dot(y_a, wpa[...])
        p_b = _dot(y_b, wpb[...])
        gate_a = _sigmoid(jnp.concatenate([ta0_ref[...], ta1_ref[...]], axis=1).astype(F32) + bg[:, :D_MODEL])
        gate_b = _sigmoid(jnp.concatenate([tb0_ref[...], tb1_ref[...]], axis=1).astype(F32) + bg[:, D_MODEL:])
        merged = (gate_a * p_a + gate_b * p_b).astype(BF16)
        x2 = x_ref[...] + _dot(merged, wout[...])
        r2 = lax.rsqrt(jnp.mean(x2 * x2, axis=-1, keepdims=True) + EPS)
        xh = x2 * r2
        fg = fg_ref[...]
        err = xh * fg - t_ref[...]
        loss_ref[...] += jnp.sum(jnp.sum(err * err, axis=-1, keepdims=True), axis=0, keepdims=True) * (0.5 / D_MODEL)

        dy = err * (1.0 / D_MODEL)
        dfg_ref[...] += jnp.sum(dy * xh, axis=0, keepdims=True)
        gy = dy * fg
        dx2 = r2 * (gy - xh * jnp.mean(gy * xh, axis=-1, keepdims=True))
        dx2_ref[...] = dx2
        dx2b = dx2.astype(BF16)
        dmerged = _dot(dx2b, wout[...], NT)
        acc_out[...] += _dot(merged, dx2b, TN)

        dp_a = dmerged * gate_a
        dp_b = dmerged * gate_b
        dgate_a = dp_a * p_a * (1.0 - gate_a)
        dgate_b = dp_b * p_b * (1.0 - gate_b)
        dbg_ref[:, :D_MODEL] += jnp.sum(dgate_a, axis=0, keepdims=True)
        dbg_ref[:, D_MODEL:] += jnp.sum(dgate_b, axis=0, keepdims=True)
        dz_ref[:, 2048:3072] = dgate_a.astype(BF16)
        dz_ref[:, 3072:4096] = dgate_b.astype(BF16)
        dp_ab = dp_a.astype(BF16)
        dp_bb = dp_b.astype(BF16)
        dy_a = _dot(dp_ab, wpa[...], NT)
        dy_b = _dot(dp_bb, wpb[...], NT)
        acc_pa[...] += _dot(y_a, dp_ab, TN)
        acc_pb[...] += _dot(y_b, dp_bb, TN)

        doa_ref[...] = (dy_a * silu_a).astype(BF16)
        dz_ref[:, 0:512] = (dy_a * o_a * (sg_a * (1.0 + g_a * (1.0 - sg_a)))).astype(BF16)
        dz_ref[:, 1536:2048] = (dy_b * um * (sg_b * (1.0 + g_b * (1.0 - sg_b)))).astype(BF16)
        dys = dy_b * silu_b
        dz_ref[:, 512:1024] = (dys * mixed * dgelu_u).astype(BF16)
        dmixed = dys * ug
        dmb = dmixed.astype(BF16)
        for n in range(n_sub):
            rows = slice(n * SGU_CHUNK, (n + 1) * SGU_CHUNK)
            for g in range(N_GROUPS):
                cols = slice(g * 128, (g + 1) * 128)
                dws_ref[g] += _dot(dmb[rows, cols], vn[rows, cols], NT)
                dbs_ref[g] += jnp.sum(dmixed[rows, cols], axis=-1, keepdims=True)
                dvn_s[rows, cols] = _dot(wmix[g], dmb[rows, cols], TN)
        dvn = dvn_s[...]
        dlng_ref[...] += jnp.sum(dvn * vhat, axis=0, keepdims=True)
        dlnb_ref[...] += jnp.sum(dvn, axis=0, keepdims=True)
        dvh = dvn * lng
        dvg = rstd * (dvh - jnp.mean(dvh, axis=-1, keepdims=True) - vhat * jnp.mean(dvh * vhat, axis=-1, keepdims=True))
        dz_ref[:, 1024:1536] = (dvg * dgelu_v).astype(BF16)

        @pl.when(i == nt - 1)
        def _():
            t_idx = lax.broadcasted_iota(jnp.int32, (SGU_CHUNK, SGU_CHUNK), 0)
            s_idx = lax.broadcasted_iota(jnp.int32, (SGU_CHUNK, SGU_CHUNK), 1)
            for g in range(N_GROUPS):
                dws_ref[g] = jnp.where(s_idx <= t_idx, dws_ref[g], 0.0)
            stores = [pltpu.make_async_copy(src, dst, sem.at[n])
                      for n, (src, dst) in enumerate(((acc_out, dwout_hbm), (acc_pa, dwpa_hbm), (acc_pb, dwpb_hbm)))]
            for cp in stores:
                cp.start()
            for cp in stores:
                cp.wait()

    tile = lambda w: pl.BlockSpec((tm, w), lambda i: (i, 0))
    whole = lambda shape: pl.BlockSpec(shape, lambda i: (0,) * len(shape))
    hbm = pl.BlockSpec(memory_space=pl.ANY)
    return pl.pallas_call(
        body, name="mid_fwd_bwd",
        grid=(nt,),
        in_specs=[tile(D_MODEL), tile(D_MODEL), tile(D_A)]
        + [pl.BlockSpec((tm, COL_BLOCK), functools.partial(lambda c, i: (i + Z_PAD // tm, c), c))
           for c in range(3, N_COL_BLOCKS)]
        + [hbm, hbm, hbm,
                  whole((1, 2 * D_MODEL)), whole((1, D_B)), whole((1, D_B)),
                  whole((N_GROUPS, SGU_CHUNK, SGU_CHUNK)), whole((N_GROUPS, SGU_CHUNK, 1)), whole((1, D_MODEL))],
        out_specs=(tile(D_MODEL), tile(D_A), tile(REST), hbm, hbm, hbm,
                   whole((1, 2 * D_MODEL)), whole((1, D_MODEL)), whole((1, D_B)), whole((1, D_B)),
                   whole((N_GROUPS, SGU_CHUNK, SGU_CHUNK)), whole((N_GROUPS, SGU_CHUNK, 1)), whole((1, 1))),
        out_shape=(jax.ShapeDtypeStruct((s, D_MODEL), F32), jax.ShapeDtypeStruct((s, D_A), BF16),
                   jax.ShapeDtypeStruct((s, REST), BF16),
                   jax.ShapeDtypeStruct((D_MODEL, D_MODEL), F32), jax.ShapeDtypeStruct((D_A, D_MODEL), F32),
                   jax.ShapeDtypeStruct((D_B, D_MODEL), F32),
                   jax.ShapeDtypeStruct((1, 2 * D_MODEL), F32), jax.ShapeDtypeStruct((1, D_MODEL), F32),
                   jax.ShapeDtypeStruct((1, D_B), F32), jax.ShapeDtypeStruct((1, D_B), F32),
                   jax.ShapeDtypeStruct((N_GROUPS, SGU_CHUNK, SGU_CHUNK), F32),
                   jax.ShapeDtypeStruct((N_GROUPS, SGU_CHUNK, 1), F32), jax.ShapeDtypeStruct((1, 1), F32)),
        scratch_shapes=[pltpu.VMEM((D_A, D_MODEL), BF16), pltpu.VMEM((D_B, D_MODEL), BF16),
                        pltpu.VMEM((D_MODEL, D_MODEL), BF16), pltpu.VMEM((N_GROUPS, SGU_CHUNK, SGU_CHUNK), BF16),
                        pltpu.VMEM((D_MODEL, D_MODEL), F32), pltpu.VMEM((D_A, D_MODEL), F32),
                        pltpu.VMEM((D_B, D_MODEL), F32),
                        pltpu.VMEM((tm, D_B), F32), pltpu.VMEM((tm, D_B), F32),
                        pltpu.SemaphoreType.DMA((3,))],
        compiler_params=_params(56),
    )(x, target, attn_out, *([z] * (N_COL_BLOCKS - 3)), w_pa, w_pb, w_out, b_gate, ln_g, ln_b, w_s, b_s, final_g)


def _proj_bwd_x(dqkv, drest, x, dx2, norm_g, w_in_t, to_chip):
    s = x.shape[0]
    tm = TOKEN_TILE
    nt = s // tm
    n = len(to_chip)

    def body(*refs):
        dqkv_ref, dr_ref, x_ref, dx2_ref, g_ref, w_hbm = refs[:6]
        to_chip_refs = refs[6:6 + n]
        dx_ref, dg_ref = refs[6 + n:8 + n]
        from_chip_refs = refs[8 + n:8 + 2 * n]
        w, sem, send_sems, recv_sems = refs[8 + 2 * n:]
        i = pl.program_id(0)

        @pl.when(i == 0)
        def _():
            for rc in _owner_copies(to_chip_refs, from_chip_refs, send_sems, recv_sems):
                rc.start()
            cp = pltpu.make_async_copy(w_hbm, w, sem)
            cp.start()
            dg_ref[...] = jnp.zeros(dg_ref.shape, F32)
            cp.wait()

        dh = None
        for c in range(N_COL_BLOCKS):
            dz = dqkv_ref[c] if c < 3 else dr_ref[:, (c - 3) * COL_BLOCK:(c - 2) * COL_BLOCK]
            part = _dot(dz, w[c * COL_BLOCK:(c + 1) * COL_BLOCK, :])
            dh = part if dh is None else dh + part
        xf = x_ref[...]
        r = lax.rsqrt(jnp.mean(xf * xf, axis=-1, keepdims=True) + EPS)
        xn = xf * r
        dg_ref[...] += jnp.sum(dh * xn, axis=0, keepdims=True)
        gh = dh * g_ref[...]
        dx_ref[...] = r * (gh - xn * jnp.mean(gh * xn, axis=-1, keepdims=True)) + dx2_ref[...]

        @pl.when(i == nt - 1)
        def _():
            for rc in _owner_copies(to_chip_refs, from_chip_refs, send_sems, recv_sems):
                rc.wait_recv()
                rc.wait_send()

    hbm = pl.BlockSpec(memory_space=pl.ANY)
    return pl.pallas_call(
        body, name="proj_bwd_x",
        grid=(nt,),
        in_specs=[pl.BlockSpec((3, tm, D_A), lambda i: (0, i, 0)),
                  pl.BlockSpec((tm, REST), lambda i: (i, 0)),
                  pl.BlockSpec((tm, D_MODEL), lambda i: (i, 0)),
                  pl.BlockSpec((tm, D_MODEL), lambda i: (i, 0)),
                  pl.BlockSpec((1, D_MODEL), lambda i: (0, 0)),
                  hbm] + [hbm] * n,
        out_specs=(pl.BlockSpec((tm, D_MODEL), lambda i: (i, 0)),
                   pl.BlockSpec((1, D_MODEL), lambda i: (0, 0))) + (hbm,) * n,
        out_shape=(jax.ShapeDtypeStruct((s, D_MODEL), F32), jax.ShapeDtypeStruct((1, D_MODEL), F32))
        + tuple(jax.ShapeDtypeStruct(t.shape, t.dtype) for t in to_chip),
        scratch_shapes=[pltpu.VMEM((D_IN, D_MODEL), BF16), pltpu.SemaphoreType.DMA,
                        pltpu.SemaphoreType.DMA((n, 3)), pltpu.SemaphoreType.DMA((n, 3))],
        compiler_params=_params(48),
    )(dqkv, drest, x, dx2, norm_g, w_in_t, *to_chip)


def _proj_bwd_w(h, dqkv, drest):
    s = h.shape[0]
    tk = min(s, 1024)
    nk = s // tk

    def body(h_ref, dqkv_ref, dr_ref, o_ref, acc):
        j = pl.program_id(0)
        i = pl.program_id(1)

        @pl.when(i == 0)
        def _():
            acc[...] = jnp.zeros(acc.shape, F32)

        @pl.when(j < 3)
        def _():
            acc[...] += _dot(dqkv_ref[...], h_ref[...], TN)

        @pl.when(j >= 3)
        def _():
            acc[...] += _dot(dr_ref[...], h_ref[...], TN)

        @pl.when(i == nk - 1)
        def _():
            o_ref[...] = acc[...].astype(BF16)

    return pl.pallas_call(
        body, name="proj_bwd_w",
        grid=(N_COL_BLOCKS, nk),
        in_specs=[pl.BlockSpec((tk, D_MODEL), lambda j, i: (i, 0)),
                  pl.BlockSpec((None, tk, COL_BLOCK),
                               lambda j, i: (jnp.minimum(j, 2), jnp.where(j < 3, i, nk - 1), 0)),
                  pl.BlockSpec((tk, COL_BLOCK),
                               lambda j, i: (jnp.where(j >= 3, i, 0), jnp.maximum(j - 3, 0)))],
        out_specs=pl.BlockSpec((COL_BLOCK, D_MODEL), lambda j, i: (j, 0)),
        out_shape=jax.ShapeDtypeStruct((D_IN, D_MODEL), BF16),
        scratch_shapes=[pltpu.VMEM((COL_BLOCK, D_MODEL), F32)],
        compiler_params=_params(40),
    )(h, dqkv, drest)


def _adamw_math(w, g, m, v):
    c1 = 1.0 - ADAM_B1 ** ADAM_STEP
    c2 = 1.0 - ADAM_B2 ** ADAM_STEP
    nm = ADAM_B1 * m + (1.0 - ADAM_B1) * g
    nv = ADAM_B2 * v + (1.0 - ADAM_B2) * (g * g)
    return -ADAM_LR * ((nm / c1) / (jnp.sqrt(nv / c2) + ADAM_EPS) + ADAM_WD * w), nm, nv


def _adamw(name, w, g, m, v, from_chip):
    rows, cols = w.shape
    tr = rows if rows * cols <= 512 * 1024 else next(t for t in range(256, 7, -8) if rows % t == 0)

    def body(w_ref, g_ref, m_ref, v_ref, t_ref, g_out, d_ref, nm_ref, nv_ref):
        gg = g_ref[...]
        for j in range(3):
            gg = gg + t_ref[j].astype(F32)
        g_out[...] = gg
        d_ref[...], nm_ref[...], nv_ref[...] = _adamw_math(w_ref[...], gg, m_ref[...], v_ref[...])

    spec = pl.BlockSpec((tr, cols), lambda i: (i, 0))
    shape = jax.ShapeDtypeStruct((rows, cols), F32)
    return pl.pallas_call(
        body, name=name,
        grid=(rows // tr,),
        in_specs=[spec] * 4 + [pl.BlockSpec((3, tr, cols), lambda i: (0, i, 0))],
        out_specs=(spec,) * 4, out_shape=(shape,) * 4,
        compiler_params=_params(32),
    )(w, g, m, v, from_chip)


_SMALL = (("norm_g", (1, D_MODEL)), ("b_gate", (1, 2 * D_MODEL)), ("rel_bias", (N_HEADS, N_REL)),
          ("sgu_ln_g", (1, D_B)), ("sgu_ln_b", (1, D_B)), ("w_s", (N_GROUPS * SGU_CHUNK, SGU_CHUNK)),
          ("b_s", (N_GROUPS, SGU_CHUNK)), ("final_g", (1, D_MODEL)))


def _adamw_small(slab, g_ws, weights, moments_m, moments_v):
    k = len(_SMALL)

    def grad_of(name, slab_ref, ws_ref):
        if name == "norm_g":
            return slab_ref[ROW_NORM_G:ROW_NORM_G + 1, :]
        if name == "b_gate":
            return jnp.concatenate([slab_ref[ROW_B_GATE:ROW_B_GATE + 1, :], slab_ref[ROW_B_GATE + 1:ROW_B_GATE + 2, :]],
                                   axis=1)
        if name == "rel_bias":
            return slab_ref[ROW_REL:ROW_REL + N_HEADS, :N_REL]
        if name == "sgu_ln_g":
            return slab_ref[ROW_LN_G:ROW_LN_G + 1, :D_B]
        if name == "sgu_ln_b":
            return slab_ref[ROW_LN_B:ROW_LN_B + 1, :D_B]
        if name == "w_s":
            return ws_ref[...]
        if name == "b_s":
            return slab_ref[ROW_B_S:ROW_B_S + N_GROUPS, :SGU_CHUNK]
        return slab_ref[ROW_FINAL_G:ROW_FINAL_G + 1, :]

    def body(*refs):
        slab_ref, ws_ref = refs[:2]
        w_refs, m_refs, v_refs = refs[2:2 + k], refs[2 + k:2 + 2 * k], refs[2 + 2 * k:2 + 3 * k]
        outs = refs[2 + 3 * k:]
        for n, (name, _) in enumerate(_SMALL):
            g = grad_of(name, slab_ref, ws_ref)
            outs[n][...] = g
            outs[k + n][...], outs[2 * k + n][...], outs[3 * k + n][...] = _adamw_math(
                w_refs[n][...], g, m_refs[n][...], v_refs[n][...])
        outs[4 * k][...] = slab_ref[ROW_LOSS:ROW_LOSS + 1, :1]

    vmem = pl.BlockSpec(memory_space=pltpu.VMEM)
    shapes = tuple(jax.ShapeDtypeStruct(shape, F32) for _, shape in _SMALL)
    return pl.pallas_call(
        body, name="adamw_small",
        out_shape=shapes * 4 + (jax.ShapeDtypeStruct((1, 1), F32),),
        in_specs=[vmem] * (2 + 3 * k), out_specs=tuple([vmem] * (4 * k + 1)),
        compiler_params=_params(16),
    )(slab, g_ws, *weights, *moments_m, *moments_v)


def _pad_rel(a):
    return jnp.pad(a.reshape(N_HEADS, N_REL), ((0, 0), (0, N_REL_PAD - N_REL)))


def kernel(x, norm_g, w_in, b_gate, rel_bias, sgu_ln_g, sgu_ln_b, w_s, b_s, w_pa, w_pb, w_out, final_g, loss_target, m_norm_g, m_w_in, m_b_gate, m_rel_bias, m_sgu_ln_g, m_sgu_ln_b, m_w_s, m_b_s, m_w_pa, m_w_pb, m_w_out, m_final_g, v_norm_g, v_w_in, v_b_gate, v_rel_bias, v_sgu_ln_g, v_sgu_ln_b, v_w_s, v_b_s, v_w_pa, v_w_pb, v_w_out, v_final_g):
    s = x.shape[1]
    xs = x.reshape(s, D_MODEL)
    tgt = loss_target.reshape(s, D_MODEL)

    bias_table = _bias_table(_pad_rel(rel_bias))
    w_in_t = jnp.swapaxes(w_in[0], 0, 1)
    qkv, h, w_in_t_full = _gather_proj_fwd(xs, norm_g, w_in_t)
    attn_out, g_pa, g_pb, g_out = _attn_fwd(qkv, bias_table, (w_pa[0], w_pb[0], w_out[0]))
    w_pa_full = jnp.transpose(g_pa, (1, 0, 2)).reshape(D_A, D_MODEL)
    w_pb_full = jnp.transpose(g_pb, (1, 0, 2)).reshape(D_B, D_MODEL)
    w_out_full = g_out.reshape(D_MODEL, D_MODEL)

    (dx2, d_attn, drest, dw_out, dw_pa, dw_pb, d_bgate, d_fg, d_lng, d_lnb, d_ws, d_bs, loss_part) = _mid_fwd_bwd(
        xs, tgt, attn_out, qkv, w_pa_full, w_pb_full, w_out_full, b_gate, sgu_ln_g, sgu_ln_b, w_s[0],
        b_s.reshape(N_GROUPS, SGU_CHUNK, 1), final_g.reshape(1, D_MODEL))

    own_pa, own_pb, own_out, tc_pa, tc_pb, tc_out = _reduce_chip(
        "reduce_chip_proj", (dw_pa, dw_pb, dw_out), (1, 1, 0))
    dqkv, dbias, fc_pa, fc_pb, fc_out = _attn_bwd(qkv, bias_table, d_attn, (tc_pa, tc_pb, tc_out))
    d_rel = _bias_grad(dbias)
    dw_in_t = _proj_bwd_w(h, dqkv, drest)
    own_in, tc_in = _reduce_chip("reduce_chip_in", (dw_in_t,), (0,))
    grad_x, d_ng, fc_in = _proj_bwd_x(dqkv, drest, xs, dx2, norm_g, w_in_t_full, (tc_in,))
    big = {"w_in": tuple(jnp.swapaxes(t, 0, 1)[None] for t in _adamw(
        "adamw_w_in", w_in_t, own_in, jnp.swapaxes(m_w_in[0], 0, 1), jnp.swapaxes(v_w_in[0], 0, 1), fc_in))}
    for name, w, g, fc, m, v in (("w_pa", w_pa, own_pa, fc_pa, m_w_pa, v_w_pa),
                                 ("w_pb", w_pb, own_pb, fc_pb, m_w_pb, v_w_pb),
                                 ("w_out", w_out, own_out, fc_out, m_w_out, v_w_out)):
        big[name] = tuple(t[None] for t in _adamw("adamw_" + name, w[0], g, m[0], v[0], fc))

    slab, g_ws = _reduce_small(d_ng, d_bgate, d_rel, d_lng, d_lnb, d_fg, loss_part, d_bs, d_ws)
    as_2d = lambda leaves: [a.reshape(shape) for a, (_, shape) in zip(leaves, _SMALL)]
    small_out = _adamw_small(
        slab, g_ws, as_2d((norm_g, b_gate, rel_bias, sgu_ln_g, sgu_ln_b, w_s, b_s, final_g)),
        as_2d((m_norm_g, m_b_gate, m_rel_bias, m_sgu_ln_g, m_sgu_ln_b, m_w_s, m_b_s, m_final_g)),
        as_2d((v_norm_g, v_b_gate, v_rel_bias, v_sgu_ln_g, v_sgu_ln_b, v_w_s, v_b_s, v_final_g)))
    small_index = {name: n for n, (name, _) in enumerate(_SMALL)}

    def leaf(kind, name, like):
        if name in big:
            return big[name][kind]
        return small_out[kind * len(_SMALL) + small_index[name]].reshape(like.shape)

    weights = (("norm_g", norm_g), ("w_in", w_in), ("b_gate", b_gate), ("rel_bias", rel_bias), ("sgu_ln_g", sgu_ln_g),
               ("sgu_ln_b", sgu_ln_b), ("w_s", w_s), ("b_s", b_s), ("w_pa", w_pa), ("w_pb", w_pb), ("w_out", w_out),
               ("final_g", final_g))
    outs = [small_out[-1].reshape(()), grad_x.reshape(x.shape)]
    for kind in range(4):
        outs.extend(leaf(kind, name, like) for name, like in weights)
    return tuple(outs)
```

```python
import functools
import math

import jax
import jax.numpy as jnp
from jax import lax
from jax.experimental import pallas as pl
from jax.experimental.pallas import tpu as pltpu

F32 = jnp.float32
BF16 = jnp.bfloat16
MESH = pl.DeviceIdType.MESH
N_DEV = 8

D_MODEL = 1024
D_A = 512
D_B = 512
D_IN = 5632
N_HEADS = 8
HEAD_DIM = 64
CHUNK = 64
N_PREV = 8
REL_CLIP = 128
N_REL = 2 * REL_CLIP + 1
N_REL_PAD = 384
SGU_CHUNK = 128
N_GROUPS = 4
EPS = 1e-6
NEG_INF = -1e30
Q_SCALE = HEAD_DIM ** -0.5

Q_BLOCK = 128
K_SPAN = 640
Z_PAD = K_SPAN - Q_BLOCK
ROLL_W = 1024
COL_BLOCK = 512
N_COL_BLOCKS = D_IN // COL_BLOCK
REST = D_IN - 3 * D_A
TOKEN_TILE = 256
V7X_VMEM_BYTES = 64 * 1024 * 1024

ADAM_LR = 0.001
ADAM_B1 = 0.9
ADAM_B2 = 0.999
ADAM_EPS = 1e-08
ADAM_WD = 0.01
ADAM_STEP = 10

GELU_C = math.sqrt(2.0 / math.pi)
GELU_A = 0.044715

NT = (((1,), (1,)), ((), ()))
TN = (((0,), (0,)), ((), ()))
HIGHEST = lax.Precision.HIGHEST


def _params(vmem_mb, **kw):
    return pltpu.CompilerParams(vmem_limit_bytes=vmem_mb * 1024 * 1024, **kw)


def _dot(a, b, dims=None):
    if dims is None:
        return jnp.dot(a, b, preferred_element_type=F32)
    return lax.dot_general(a, b, dims, preferred_element_type=F32)


def _sigmoid(x):
    return 1.0 / (1.0 + jnp.exp(-x))


def _gelu_and_grad(u):
    u2 = u * u
    t = jnp.tanh(GELU_C * (u + GELU_A * u * u2))
    half = 0.5 * (1.0 + t)
    g = u * half
    dg = half + 0.5 * u * (1.0 - t * t) * (GELU_C * (1.0 + 3.0 * GELU_A * u2))
    return g, dg


def _my_pos():
    return lax.axis_index("x"), lax.axis_index("y"), lax.axis_index("c")


def _flat_id(pos):
    return 4 * pos[0] + 2 * pos[1] + pos[2]


def _peer(pos, k):
    x, y, c = pos
    return (1 - x if k & 4 else x, 1 - y if k & 2 else y, 1 - c if k & 1 else c)


def _other_chips(pos):
    x, y, _ = pos
    return ((1 - x, y), (x, 1 - y), (1 - x, 1 - y))


class _SlotGather:
    def __init__(self, bufs, send_sems, recv_sems, own=None):
        self.bufs, self.send_sems, self.recv_sems = bufs, send_sems, recv_sems
        self.own = own if own is not None else [None] * len(bufs)
        x, y, c = _my_pos()
        self.c, self.me, self.sib = c, (x, y, c), (x, y, 1 - c)
        self.chips = _other_chips(self.me)

    def _copy(self, a, k, block, to):
        slot = _flat_id(block)
        src = self.own[a] if (k < 4 and self.own[a] is not None) else self.bufs[a].at[slot]
        return pltpu.make_async_remote_copy(
            src_ref=src, dst_ref=self.bufs[a].at[slot],
            send_sem=self.send_sems.at[a, k], recv_sem=self.recv_sems.at[a, k], device_id=to, device_id_type=MESH)

    def _own_sends(self):
        n = len(self.bufs)
        return ([self._copy(a, 1 + j, self.me, (*chip, self.c)) for j, chip in enumerate(self.chips) for a in range(n)]
                + [self._copy(a, 0, self.me, self.sib) for a in range(n)])

    def _passes(self):
        return [self._copy(a, 4 + j, (*chip, self.c), self.sib)
                for j, chip in enumerate(self.chips) for a in range(len(self.bufs))]

    def start(self):
        for cp in self._own_sends():
            cp.start()

    def pass_on(self):
        for j, chip in enumerate(self.chips):
            for a in range(len(self.bufs)):
                self._copy(a, 1 + j, (*chip, self.c), self.me).wait_recv()
                self._copy(a, 4 + j, (*chip, self.c), self.sib).start()

    def finish(self):
        for a in range(len(self.bufs)):
            self._copy(a, 0, self.sib, self.me).wait_recv()
            for j, chip in enumerate(self.chips):
                self._copy(a, 4 + j, (*chip, 1 - self.c), self.me).wait_recv()
        for cp in self._own_sends() + self._passes():
            cp.wait_send()


def _reduce_chip(name, parts, sharded_dim):
    n = len(parts)
    shapes = []
    for p, dim in zip(parts, sharded_dim):
        shape = list(p.shape)
        shape[dim] //= N_DEV
        shapes.append(tuple(shape))

    def body(*refs):
        full, own, to_chip = refs[:n], refs[n:2 * n], refs[2 * n:3 * n]
        ins, from_sib = refs[3 * n:4 * n], refs[4 * n:5 * n]
        send_sems, recv_sems = refs[5 * n], refs[5 * n + 1]
        x, y, c = _my_pos()
        sib = (x, y, 1 - c)
        chips = ((x, y),) + _other_chips((x, y, c))
        for a in range(n):
            rows, cols = shapes[a]
            for d in range(N_DEV):
                if sharded_dim[a] == 0:
                    ins[a][d] = full[a][d * rows:(d + 1) * rows, :].astype(BF16)
                else:
                    ins[a][d] = full[a][:, d * cols:(d + 1) * cols].astype(BF16)

        def to_sibling(a, r):
            return pltpu.make_async_remote_copy(
                src_ref=ins[a].at[_flat_id((*chips[r], 1 - c))], dst_ref=from_sib[a].at[r],
                send_sem=send_sems.at[a, r], recv_sem=recv_sems.at[a, r], device_id=sib, device_id_type=MESH)

        sends = [to_sibling(a, r) for r in (1, 2, 3, 0) for a in range(n)]
        for cp in sends:
            cp.start()
        for r in (1, 2, 3, 0):
            for a in range(n):
                to_sibling(a, r).wait_recv()
                both = ins[a][_flat_id((*chips[r], c))].astype(F32) + from_sib[a][r].astype(F32)
                if r == 0:
                    own[a][...] = both
                else:
                    to_chip[a][r - 1] = both.astype(BF16)
        for cp in sends:
            cp.wait_send()

    vmem = pl.BlockSpec(memory_space=pltpu.VMEM)
    return pl.pallas_call(
        body, name=name,
        out_shape=tuple(jax.ShapeDtypeStruct(sh, F32) for sh in shapes)
        + tuple(jax.ShapeDtypeStruct((3,) + sh, BF16) for sh in shapes),
        in_specs=[vmem] * n, out_specs=tuple([vmem] * (2 * n)),
        scratch_shapes=[pltpu.VMEM((N_DEV,) + sh, BF16) for sh in shapes]
        + [pltpu.VMEM((4,) + sh, BF16) for sh in shapes]
        + [pltpu.SemaphoreType.DMA((n, 4)), pltpu.SemaphoreType.DMA((n, 4))],
        compiler_params=_params(56),
    )(*parts)


def _owner_copies(to_chip, from_chip, send_sems, recv_sems):
    x, y, c = _my_pos()
    return [pltpu.make_async_remote_copy(
        src_ref=to_chip[a].at[j], dst_ref=from_chip[a].at[j],
        send_sem=send_sems.at[a, j], recv_sem=recv_sems.at[a, j], device_id=(*chip, c), device_id_type=MESH)
        for a in range(len(to_chip)) for j, chip in enumerate(_other_chips((x, y, c)))]


ROW_NORM_G, ROW_B_GATE, ROW_LN_G, ROW_LN_B, ROW_FINAL_G, ROW_LOSS, ROW_REL, ROW_B_S, SLAB_ROWS = 0, 1, 3, 4, 5, 6, 8, 16, 24


def _reduce_small(d_ng, d_bgate, d_rel, d_lng, d_lnb, d_fg, loss, d_bs, d_ws):
    def body(ng_ref, bg_ref, rel_ref, lng_ref, lnb_ref, fg_ref, loss_ref, bs_ref, ws_ref, slab_out, ws_out,
             slab_land, ws_land, send_sems, recv_sems):
        me = _flat_id(_my_pos())
        slab_land[me] = jnp.zeros((SLAB_ROWS, D_MODEL), F32)
        slab_land[me, ROW_NORM_G:ROW_NORM_G + 1, :] = ng_ref[...]
        slab_land[me, ROW_B_GATE:ROW_B_GATE + 1, :] = bg_ref[:, :D_MODEL]
        slab_land[me, ROW_B_GATE + 1:ROW_B_GATE + 2, :] = bg_ref[:, D_MODEL:]
        slab_land[me, ROW_LN_G:ROW_LN_G + 1, :D_B] = lng_ref[...]
        slab_land[me, ROW_LN_B:ROW_LN_B + 1, :D_B] = lnb_ref[...]
        slab_land[me, ROW_FINAL_G:ROW_FINAL_G + 1, :] = fg_ref[...]
        slab_land[me, ROW_LOSS:ROW_LOSS + 1, :1] = loss_ref[...]
        slab_land[me, ROW_REL:ROW_REL + N_HEADS, :N_REL_PAD] = rel_ref[...]
        eye = (lax.broadcasted_iota(jnp.int32, (SGU_CHUNK, SGU_CHUNK), 0)
               == lax.broadcasted_iota(jnp.int32, (SGU_CHUNK, SGU_CHUNK), 1))
        for g in range(N_GROUPS):
            row = jnp.sum(jnp.where(eye, bs_ref[g], 0.0), axis=0, keepdims=True)
            slab_land[me, ROW_B_S + g:ROW_B_S + g + 1, :SGU_CHUNK] = row
        ws_land[me] = ws_ref[...]
        gather = _SlotGather([slab_land, ws_land], send_sems, recv_sems)
        gather.start()
        gather.pass_on()
        gather.finish()
        for land, out in ((slab_land, slab_out), (ws_land, ws_out)):
            acc = land[0]
            for d in range(1, N_DEV):
                acc = acc + land[d]
            out[...] = acc

    vmem = pl.BlockSpec(memory_space=pltpu.VMEM)
    ws_shape = (N_GROUPS * SGU_CHUNK, SGU_CHUNK)
    return pl.pallas_call(
        body, name="reduce_small",
        out_shape=(jax.ShapeDtypeStruct((SLAB_ROWS, D_MODEL), F32), jax.ShapeDtypeStruct(ws_shape, F32)),
        in_specs=[vmem] * 9, out_specs=(vmem, vmem),
        scratch_shapes=[pltpu.VMEM((N_DEV, SLAB_ROWS, D_MODEL), F32), pltpu.VMEM((N_DEV,) + ws_shape, F32),
                        pltpu.SemaphoreType.DMA((2, N_DEV - 1)), pltpu.SemaphoreType.DMA((2, N_DEV - 1))],
        compiler_params=_params(16),
    )(d_ng, d_bgate, d_rel, d_lng, d_lnb, d_fg, loss, d_bs, d_ws.reshape(ws_shape))


def _rel_index(e):
    lo, hi = Z_PAD - REL_CLIP, Z_PAD + REL_CLIP
    return jnp.where(e <= lo, 2 * REL_CLIP, jnp.where(e < hi, hi - e, jnp.where(e <= K_SPAN, 0, 2 * REL_CLIP)))


def _bias_table(rel_bias_pad):
    def body(rb_ref, bt_ref):
        c = lax.broadcasted_iota(jnp.int32, (N_REL_PAD, ROLL_W), 1)
        r = lax.broadcasted_iota(jnp.int32, (N_REL_PAD, ROLL_W), 0)
        pick = (r == _rel_index(c)).astype(F32)
        rows = jnp.dot(rb_ref[...], pick, precision=HIGHEST, preferred_element_type=F32)
        qc = lax.broadcasted_iota(jnp.int32, (Q_BLOCK, K_SPAN), 0) >> 6
        kc = lax.broadcasted_iota(jnp.int32, (Q_BLOCK, K_SPAN), 1) >> 6
        band = (kc >= qc) & (kc <= qc + N_PREV)
        for h in range(N_HEADS):
            t = jnp.broadcast_to(rows[h:h + 1, :], (Q_BLOCK, ROLL_W))
            t = pltpu.roll(t, 0, 1, stride=1, stride_axis=0)
            bt_ref[h] = jnp.where(band, t[:, :K_SPAN], NEG_INF)

    return pl.pallas_call(
        body, name="bias_table",
        out_shape=jax.ShapeDtypeStruct((N_HEADS, Q_BLOCK, K_SPAN), F32),
        compiler_params=_params(32),
    )(rel_bias_pad)


def _bias_grad(dbias):
    def body(a_ref, o_ref):
        rr = lax.broadcasted_iota(jnp.int32, (Q_BLOCK, Q_BLOCK), 0)
        cc = lax.broadcasted_iota(jnp.int32, (Q_BLOCK, Q_BLOCK), 1)
        flip = (rr + cc == Q_BLOCK - 1).astype(F32)
        c = lax.broadcasted_iota(jnp.int32, (ROLL_W, N_REL_PAD), 0)
        r = lax.broadcasted_iota(jnp.int32, (ROLL_W, N_REL_PAD), 1)
        e = jnp.where(c >= Q_BLOCK - 1, c - (Q_BLOCK - 1), c + (ROLL_W - Q_BLOCK + 1))
        pick = (r == _rel_index(e)).astype(F32)
        sums = []
        for h in range(N_HEADS):
            a = jnp.dot(flip, a_ref[h], precision=HIGHEST, preferred_element_type=F32)
            a = jnp.concatenate([a, jnp.zeros((Q_BLOCK, ROLL_W - K_SPAN), F32)], axis=1)
            a = pltpu.roll(a, 0, 1, stride=1, stride_axis=0)
            sums.append(jnp.sum(a, axis=0, keepdims=True))
        diag = jnp.concatenate(sums, axis=0)
        o_ref[...] = jnp.dot(diag, pick, precision=HIGHEST, preferred_element_type=F32)

    return pl.pallas_call(
        body, name="bias_grad",
        out_shape=jax.ShapeDtypeStruct((N_HEADS, N_REL_PAD), F32),
        compiler_params=_params(32),
    )(dbias)


def _gather_proj_fwd(x, norm_g, w_in_t):
    s = x.shape[0]
    tm = 512 if s % 512 == 0 else TOKEN_TILE
    nt = s // tm
    n_pad = Z_PAD // tm
    shard_w = w_in_t.shape[0]
    chip_w = 2 * shard_w
    n_chips = N_DEV // 2

    def body(order_ref, x_ref, g_ref, win_hbm, z_ref, h_ref, wt_hbm, stage, wchip, hb, win_f32, send_sems, recv_sems,
             local_sems):
        j = pl.program_id(0)
        i = pl.program_id(1)
        x_, y_, c_ = _my_pos()
        me, sib = (x_, y_, c_), (x_, y_, 1 - c_)
        near = _other_chips(me)
        pick = lambda a, b: tuple(jnp.where(c_ == 0, u, v) for u, v in zip(a, b))
        passed_from, passed_to = pick(near[0], near[1]), pick(near[1], near[0])

        def rows_of(block):
            return wt_hbm.at[pl.ds(pl.multiple_of(_flat_id(block) * shard_w, 16), shard_w), :]

        def copy(k, block, to, own=False):
            return pltpu.make_async_remote_copy(
                src_ref=stage if own else rows_of(block), dst_ref=rows_of(block),
                send_sem=send_sems.at[k], recv_sem=recv_sems.at[k], device_id=to, device_id_type=MESH)

        def sends():
            return ([copy(0, me, sib, True), copy(1, me, (*near[0], c_), True), copy(2, me, (*near[1], c_), True),
                     copy(3, (*passed_from, c_), (*passed_to, c_))]
                    + [copy(4 + n, (*near[n], c_), sib) for n in range(3)])

        keep = pltpu.make_async_copy(stage, rows_of(me), local_sems.at[0])

        def fetch(chip):
            first = pl.multiple_of((2 * chip[0] + chip[1]) * chip_w, 16)
            cp = pltpu.make_async_copy(wt_hbm.at[pl.ds(first, chip_w), :], wchip, local_sems.at[1])
            cp.start()
            cp.wait()

        @pl.when((j == 0) & (i == 0))
        def _():
            load = pltpu.make_async_copy(win_hbm, win_f32, local_sems.at[1])
            load.start()
            load.wait()
            stage[...] = win_f32[...].astype(BF16)
            keep.start()
            for cp in sends()[:3]:
                cp.start()
            copy(0, sib, me).wait_recv()
            keep.wait()
            fetch((x_, y_))

        @pl.when((j == 1) & (i == 0))
        def _():
            copy(1, (*near[0], c_), me).wait_recv()
            copy(2, (*near[1], c_), me).wait_recv()
            for cp in sends()[3:6]:
                cp.start()
            copy(4, (*near[0], 1 - c_), me).wait_recv()
            fetch(near[0])

        @pl.when((j == 2) & (i == 0))
        def _():
            copy(5, (*near[1], 1 - c_), me).wait_recv()
            fetch(near[1])

        @pl.when((j == 3) & (i == 0))
        def _():
            copy(3, (*near[2], c_), me).wait_recv()
            copy(6, (*near[2], c_), sib).start()
            copy(6, (*near[2], 1 - c_), me).wait_recv()
            fetch(near[2])

        @pl.when(i < n_pad)
        def _():
            z_ref[...] = jnp.zeros(z_ref.shape, BF16)

        @pl.when(i >= n_pad)
        def _():
            rows = pl.ds(pl.multiple_of((i - n_pad) * tm, tm), tm)

            @pl.when(j == 0)
            def _():
                xf = x_ref[...]
                r = lax.rsqrt(jnp.mean(xf * xf, axis=-1, keepdims=True) + EPS)
                hf = (xf * r * g_ref[...]).astype(BF16)
                hb[rows, :] = hf
                h_ref[...] = hf

            blk = _dot(hb[rows, :], wchip[...], NT)
            q_scale = jnp.where(order_ref[j] == 0, Q_SCALE, 1.0).astype(F32)
            z_ref[:, :D_A] = (blk[:, :D_A] * q_scale).astype(BF16)
            z_ref[:, D_A:] = blk[:, D_A:].astype(BF16)

        @pl.when((j == n_chips - 1) & (i == n_pad + nt - 1))
        def _():
            for cp in sends():
                cp.wait_send()

    pos = _my_pos()
    order = jnp.stack([2 * cx + cy for cx, cy in ((pos[0], pos[1]),) + _other_chips(pos)]).astype(jnp.int32)
    first_pass = lambda j, i: jnp.where(j == 0, jnp.maximum(i - n_pad, 0), nt - 1)
    grid_spec = pltpu.PrefetchScalarGridSpec(
        num_scalar_prefetch=1,
        grid=(n_chips, n_pad + nt),
        in_specs=[pl.BlockSpec((tm, D_MODEL), lambda j, i, o: (first_pass(j, i), 0)),
                  pl.BlockSpec((1, D_MODEL), lambda j, i, o: (0, 0)),
                  pl.BlockSpec(memory_space=pl.ANY)],
        out_specs=(pl.BlockSpec((tm, chip_w), lambda j, i, o: (i, o[j])),
                   pl.BlockSpec((tm, D_MODEL), lambda j, i, o: (first_pass(j, i), 0)),
                   pl.BlockSpec(memory_space=pl.ANY)),
        scratch_shapes=[pltpu.VMEM((shard_w, D_MODEL), BF16), pltpu.VMEM((chip_w, D_MODEL), BF16),
                        pltpu.VMEM((s, D_MODEL), BF16), pltpu.VMEM(w_in_t.shape, F32),
                        pltpu.SemaphoreType.DMA((N_DEV - 1,)), pltpu.SemaphoreType.DMA((N_DEV - 1,)),
                        pltpu.SemaphoreType.DMA((2,))])
    return pl.pallas_call(
        body, name="gather_proj_fwd",
        grid_spec=grid_spec,
        out_shape=(jax.ShapeDtypeStruct((Z_PAD + s, D_IN), BF16), jax.ShapeDtypeStruct((s, D_MODEL), BF16),
                   jax.ShapeDtypeStruct((D_IN, D_MODEL), BF16)),
        compiler_params=_params(60),
    )(order, x, norm_g, w_in_t)


def _attn_specs(rows):
    pairs = N_HEADS // 2
    return ([pl.BlockSpec((rows, 128), functools.partial(lambda which, p: (0, which * pairs + p), which))
             for which in range(3)]
            + [pl.BlockSpec((2, Q_BLOCK, K_SPAN), lambda p: (p, 0, 0))])


def _head_masks():
    lane = lax.broadcasted_iota(jnp.int32, (1, 128), 1)
    first = lane < HEAD_DIM
    return (first, jnp.logical_not(first))


def _softmax_rows(qm, kcat, bias, valid):
    s = _dot(qm, kcat, NT) + bias
    s = jnp.where(valid, s, NEG_INF)
    m = jnp.max(s, axis=-1, keepdims=True)
    e = jnp.exp(s - m)
    return e * (1.0 / jnp.sum(e, axis=-1, keepdims=True))


def _attn_fwd(qkv, bias_table, shards):
    s = qkv.shape[0] - Z_PAD
    nb = s // Q_BLOCK
    n = len(shards)
    pairs = N_HEADS // 2

    def body(*refs):
        q_ref, k_ref, v_ref, bt_ref = refs[:4]
        shard_refs = refs[4:4 + n]
        o_ref = refs[4 + n]
        slot_refs = refs[5 + n:5 + 2 * n]
        stages = refs[5 + 2 * n:5 + 3 * n]
        send_sems, recv_sems, local_sems = refs[5 + 3 * n:]
        p_id = pl.program_id(0)
        gather = _SlotGather(slot_refs, send_sems, recv_sems, own=stages)
        keep = [pltpu.make_async_copy(stages[a], slot_refs[a].at[_flat_id(_my_pos())], local_sems.at[a])
                for a in range(n)]

        @pl.when(p_id == 0)
        def _():
            for a in range(n):
                stages[a][...] = shard_refs[a][...].astype(BF16)
                keep[a].start()
            gather.start()

        @pl.when(p_id == 1)
        def _():
            gather.pass_on()

        masks = _head_masks()
        zero = jnp.zeros((), BF16)

        def block(b, carry):
            r0 = pl.multiple_of(b * Q_BLOCK, Q_BLOCK)
            q = q_ref[pl.ds(r0 + Z_PAD, Q_BLOCK), :]
            kcat = k_ref[pl.ds(r0, K_SPAN), :]
            vcat = v_ref[pl.ds(r0, K_SPAN), :]
            valid = lax.broadcasted_iota(jnp.int32, (1, K_SPAN), 1) >= Z_PAD - b * Q_BLOCK
            out = None
            for hh, mask in enumerate(masks):
                p = _softmax_rows(jnp.where(mask, q, zero), kcat, bt_ref[hh], valid)
                o = _dot(p.astype(BF16), jnp.where(mask, vcat, zero))
                out = o if out is None else out + o
            o_ref[pl.ds(r0, Q_BLOCK), :] = out
            return carry

        lax.fori_loop(0, nb, block, 0, unroll=4)

        @pl.when(p_id == pairs - 1)
        def _():
            gather.finish()
            for cp in keep:
                cp.wait()

    hbm = pl.BlockSpec(memory_space=pl.ANY)
    return pl.pallas_call(
        body, name="attn_fwd",
        grid=(pairs,),
        in_specs=_attn_specs(s + Z_PAD) + [pl.BlockSpec(a.shape, lambda p: (0, 0)) for a in shards],
        out_specs=(pl.BlockSpec((s, 128), lambda p: (0, p)),) + (hbm,) * n,
        out_shape=(jax.ShapeDtypeStruct((s, D_A), F32),)
        + tuple(jax.ShapeDtypeStruct((N_DEV,) + a.shape, BF16) for a in shards),
        scratch_shapes=[pltpu.VMEM(a.shape, BF16) for a in shards]
        + [pltpu.SemaphoreType.DMA((n, N_DEV - 1)), pltpu.SemaphoreType.DMA((n, N_DEV - 1)),
           pltpu.SemaphoreType.DMA((n,))],
        compiler_params=_params(48),
    )(qkv, qkv, qkv, bias_table, *shards)


def _attn_bwd(qkv, bias_table, d_out, to_chip):
    s = qkv.shape[0] - Z_PAD
    nb = s // Q_BLOCK
    n = len(to_chip)
    pairs = N_HEADS // 2

    def body(*refs):
        q_ref, k_ref, v_ref, bt_ref, do_ref = refs[:5]
        to_chip_refs = refs[5:5 + n]
        dqkv_ref, db_ref = refs[5 + n:7 + n]
        from_chip_refs = refs[7 + n:7 + 2 * n]
        dk_acc, dv_acc, send_sems, recv_sems = refs[7 + 2 * n:]
        p_id = pl.program_id(0)

        @pl.when(p_id == 0)
        def _():
            for cp in _owner_copies(to_chip_refs, from_chip_refs, send_sems, recv_sems):
                cp.start()

        dk_acc[...] = jnp.zeros(dk_acc.shape, F32)
        dv_acc[...] = jnp.zeros(dv_acc.shape, F32)
        db_ref[...] = jnp.zeros(db_ref.shape, F32)
        masks = _head_masks()
        zero = jnp.zeros((), BF16)

        def block(b, carry):
            r0 = pl.multiple_of(b * Q_BLOCK, Q_BLOCK)
            q = q_ref[pl.ds(r0 + Z_PAD, Q_BLOCK), :]
            do = do_ref[pl.ds(r0, Q_BLOCK), :]
            kcat = k_ref[pl.ds(r0, K_SPAN), :]
            vcat = v_ref[pl.ds(r0, K_SPAN), :]
            valid = lax.broadcasted_iota(jnp.int32, (1, K_SPAN), 1) >= Z_PAD - b * Q_BLOCK
            dq = dk = dv = None
            for hh, mask in enumerate(masks):
                qm = jnp.where(mask, q, zero)
                dom = jnp.where(mask, do, zero)
                p = _softmax_rows(qm, kcat, bt_ref[hh], valid)
                dp = _dot(dom, vcat, NT)
                ds = p * (dp - jnp.sum(p * dp, axis=-1, keepdims=True))
                db_ref[hh] += ds
                dsb = ds.astype(BF16)
                dq_h = _dot(dsb, jnp.where(mask, kcat, zero))
                dk_h = _dot(dsb, qm, TN)
                dv_h = _dot(p.astype(BF16), dom, TN)
                dq = dq_h if dq is None else dq + dq_h
                dk = dk_h if dk is None else dk + dk_h
                dv = dv_h if dv is None else dv + dv_h
            dqkv_ref[0, pl.ds(r0, Q_BLOCK), :] = (dq * Q_SCALE).astype(BF16)
            dk_acc[pl.ds(r0, K_SPAN), :] += dk
            dv_acc[pl.ds(r0, K_SPAN), :] += dv
            return carry

        lax.fori_loop(0, nb, block, 0, unroll=2)
        dqkv_ref[1] = dk_acc[Z_PAD:, :].astype(BF16)
        dqkv_ref[2] = dv_acc[Z_PAD:, :].astype(BF16)

        @pl.when(p_id == pairs - 1)
        def _():
            for cp in _owner_copies(to_chip_refs, from_chip_refs, send_sems, recv_sems):
                cp.wait_recv()
                cp.wait_send()

    hbm = pl.BlockSpec(memory_space=pl.ANY)
    return pl.pallas_call(
        body, name="attn_bwd",
        grid=(pairs,),
        in_specs=_attn_specs(s + Z_PAD) + [pl.BlockSpec((s, 128), lambda p: (0, p))] + [hbm] * n,
        out_specs=(pl.BlockSpec((3, s, 128), lambda p: (0, 0, p)),
                   pl.BlockSpec((2, Q_BLOCK, K_SPAN), lambda p: (p, 0, 0))) + (hbm,) * n,
        out_shape=(jax.ShapeDtypeStruct((3, s, D_A), BF16),
                   jax.ShapeDtypeStruct((N_HEADS, Q_BLOCK, K_SPAN), F32))
        + tuple(jax.ShapeDtypeStruct(t.shape, t.dtype) for t in to_chip),
        scratch_shapes=[pltpu.VMEM((s + Z_PAD, 128), F32), pltpu.VMEM((s + Z_PAD, 128), F32),
                        pltpu.SemaphoreType.DMA((n, 3)), pltpu.SemaphoreType.DMA((n, 3))],
        compiler_params=_params(56),
    )(qkv, qkv, qkv, bias_table, d_out, *to_chip)


def _mid_fwd_bwd(x, target, attn_out, z, w_pa, w_pb, w_out, b_gate, ln_g, ln_b, w_s, b_s, final_g):
    s = x.shape[0]
    tm = TOKEN_TILE
    nt = s // tm
    n_sub = tm // SGU_CHUNK

    def body(x_ref, t_ref, oa_ref, ga_ref, ub_ref, vb_ref, gb_ref, ta0_ref, ta1_ref, tb0_ref, tb1_ref,
             wpa_hbm, wpb_hbm, wout_hbm, bg_ref, lng_ref, lnb_ref, ws_ref, bs_ref, fg_ref,
             dx2_ref, doa_ref, dz_ref, dwout_hbm, dwpa_hbm, dwpb_hbm, dbg_ref, dfg_ref, dlng_ref, dlnb_ref, dws_ref,
             dbs_ref, loss_ref,
             wpa, wpb, wout, wmix, acc_out, acc_pa, acc_pb, mixed_s, dvn_s, sem):
        i = pl.program_id(0)

        @pl.when(i == 0)
        def _():
            loads = [pltpu.make_async_copy(src, dst, sem.at[n])
                     for n, (src, dst) in enumerate(((wpa_hbm, wpa), (wpb_hbm, wpb), (wout_hbm, wout)))]
            for cp in loads:
                cp.start()
            t_idx = lax.broadcasted_iota(jnp.int32, (SGU_CHUNK, SGU_CHUNK), 0)
            s_idx = lax.broadcasted_iota(jnp.int32, (SGU_CHUNK, SGU_CHUNK), 1)
            for g in range(N_GROUPS):
                wmix[g] = jnp.where(s_idx <= t_idx, ws_ref[g], 0.0).astype(BF16)
            for ref in (acc_out, acc_pa, acc_pb, dbg_ref, dfg_ref, dlng_ref, dlnb_ref, dws_ref, dbs_ref, loss_ref):
                ref[...] = jnp.zeros(ref.shape, F32)
            for cp in loads:
                cp.wait()

        g_a = ga_ref[...].astype(F32)
        u_b = ub_ref[...].astype(F32)
        v_b = vb_ref[...].astype(F32)
        g_b = gb_ref[...].astype(F32)
        bg = bg_ref[...]

        sg_a = _sigmoid(g_a)
        silu_a = g_a * sg_a
        o_a = oa_ref[...]
        y_a = (o_a * silu_a).astype(BF16)

        ug, dgelu_u = _gelu_and_grad(u_b)
        vg, dgelu_v = _gelu_and_grad(v_b)
        mu = jnp.mean(vg, axis=-1, keepdims=True)
        vc = vg - mu
        rstd = lax.rsqrt(jnp.mean(vc * vc, axis=-1, keepdims=True) + EPS)
        vhat = vc * rstd
        lng = lng_ref[...]
        vn = (vhat * lng + lnb_ref[...]).astype(BF16)
        for n in range(n_sub):
            rows = slice(n * SGU_CHUNK, (n + 1) * SGU_CHUNK)
            for g in range(N_GROUPS):
                cols = slice(g * 128, (g + 1) * 128)
                mixed_s[rows, cols] = _dot(wmix[g], vn[rows, cols]) + bs_ref[g]
        mixed = mixed_s[...]
        sg_b = _sigmoid(g_b)
        silu_b = g_b * sg_b
        um = ug * mixed
        y_b = (um * silu_b).astype(BF16)

        p_a = _dot(y_a, wpa[...])
        p_b = _dot(y_b, wpb[...])
        gate_a = _sigmoid(jnp.concatenate([ta0_ref[...], ta1_ref[...]], axis=1).astype(F32) + bg[:, :D_MODEL])
        gate_b = _sigmoid(jnp.concatenate([tb0_ref[...], tb1_ref[...]], axis=1).astype(F32) + bg[:, D_MODEL:])
        merged = (gate_a * p_a + gate_b * p_b).astype(BF16)
        x2 = x_ref[...] + _dot(merged, wout[...])
        r2 = lax.rsqrt(jnp.mean(x2 * x2, axis=-1, keepdims=True) + EPS)
        xh = x2 * r2
        fg = fg_ref[...]
        err = xh * fg - t_ref[...]
        loss_ref[...] += jnp.sum(jnp.sum(err * err, axis=-1, keepdims=True), axis=0, keepdims=True) * (0.5 / D_MODEL)

        dy = err * (1.0 / D_MODEL)
        dfg_ref[...] += jnp.sum(dy * xh, axis=0, keepdims=True)
        gy = dy * fg
        dx2 = r2 * (gy - xh * jnp.mean(gy * xh, axis=-1, keepdims=True))
        dx2_ref[...] = dx2
        dx2b = dx2.astype(BF16)
        dmerged = _dot(dx2b, wout[...], NT)
        acc_out[...] += _dot(merged, dx2b, TN)

        dp_a = dmerged * gate_a
        dp_b = dmerged * gate_b
        dgate_a = dp_a * p_a * (1.0 - gate_a)
        dgate_b = dp_b * p_b * (1.0 - gate_b)
        dbg_ref[:, :D_MODEL] += jnp.sum(dgate_a, axis=0, keepdims=True)
        dbg_ref[:, D_MODEL:] += jnp.sum(dgate_b, axis=0, keepdims=True)
        dz_ref[:, 2048:3072] = dgate_a.astype(BF16)
        dz_ref[:, 3072:4096] = dgate_b.astype(BF16)
        dp_ab = dp_a.astype(BF16)
        dp_bb = dp_b.astype(BF16)
        dy_a = _dot(dp_ab, wpa[...], NT)
        dy_b = _dot(dp_bb, wpb[...], NT)
        acc_pa[...] += _dot(y_a, dp_ab, TN)
        acc_pb[...] += _dot(y_b, dp_bb, TN)

        doa_ref[...] = (dy_a * silu_a).astype(BF16)
        dz_ref[:, 0:512] = (dy_a * o_a * (sg_a * (1.0 + g_a * (1.0 - sg_a)))).astype(BF16)
        dz_ref[:, 1536:2048] = (dy_b * um * (sg_b * (1.0 + g_b * (1.0 - sg_b)))).astype(BF16)
        dys = dy_b * silu_b
        dz_ref[:, 512:1024] = (dys * mixed * dgelu_u).astype(BF16)
        dmixed = dys * ug
        dmb = dmixed.astype(BF16)
        for n in range(n_sub):
            rows = slice(n * SGU_CHUNK, (n + 1) * SGU_CHUNK)
            for g in range(N_GROUPS):
                cols = slice(g * 128, (g + 1) * 128)
                dws_ref[g] += _dot(dmb[rows, cols], vn[rows, cols], NT)
                dbs_ref[g] += jnp.sum(dmixed[rows, cols], axis=-1, keepdims=True)
                dvn_s[rows, cols] = _dot(wmix[g], dmb[rows, cols], TN)
        dvn = dvn_s[...]
        dlng_ref[...] += jnp.sum(dvn * vhat, axis=0, keepdims=True)
        dlnb_ref[...] += jnp.sum(dvn, axis=0, keepdims=True)
        dvh = dvn * lng
        dvg = rstd * (dvh - jnp.mean(dvh, axis=-1, keepdims=True) - vhat * jnp.mean(dvh * vhat, axis=-1, keepdims=True))
        dz_ref[:, 1024:1536] = (dvg * dgelu_v).astype(BF16)

        @pl.when(i == nt - 1)
        def _():
            t_idx = lax.broadcasted_iota(jnp.int32, (SGU_CHUNK, SGU_CHUNK), 0)
            s_idx = lax.broadcasted_iota(jnp.int32, (SGU_CHUNK, SGU_CHUNK), 1)
            for g in range(N_GROUPS):
                dws_ref[g] = jnp.where(s_idx <= t_idx, dws_ref[g], 0.0)
            stores = [pltpu.make_async_copy(src, dst, sem.at[n])
                      for n, (src, dst) in enumerate(((acc_out, dwout_hbm), (acc_pa, dwpa_hbm), (acc_pb, dwpb_hbm)))]
            for cp in stores:
                cp.start()
            for cp in stores:
                cp.wait()

    tile = lambda w: pl.BlockSpec((tm, w), lambda i: (i, 0))
    whole = lambda shape: pl.BlockSpec(shape, lambda i: (0,) * len(shape))
    hbm = pl.BlockSpec(memory_space=pl.ANY)
    return pl.pallas_call(
        body, name="mid_fwd_bwd",
        grid=(nt,),
        in_specs=[tile(D_MODEL), tile(D_MODEL), tile(D_A)]
        + [pl.BlockSpec((tm, COL_BLOCK), functools.partial(lambda c, i: (i + Z_PAD // tm, c), c))
           for c in range(3, N_COL_BLOCKS)]
        + [hbm, hbm, hbm,
                  whole((1, 2 * D_MODEL)), whole((1, D_B)), whole((1, D_B)),
                  whole((N_GROUPS, SGU_CHUNK, SGU_CHUNK)), whole((N_GROUPS, SGU_CHUNK, 1)), whole((1, D_MODEL))],
        out_specs=(tile(D_MODEL), tile(D_A), tile(REST), hbm, hbm, hbm,
                   whole((1, 2 * D_MODEL)), whole((1, D_MODEL)), whole((1, D_B)), whole((1, D_B)),
                   whole((N_GROUPS, SGU_CHUNK, SGU_CHUNK)), whole((N_GROUPS, SGU_CHUNK, 1)), whole((1, 1))),
        out_shape=(jax.ShapeDtypeStruct((s, D_MODEL), F32), jax.ShapeDtypeStruct((s, D_A), BF16),
                   jax.ShapeDtypeStruct((s, REST), BF16),
                   jax.ShapeDtypeStruct((D_MODEL, D_MODEL), F32), jax.ShapeDtypeStruct((D_A, D_MODEL), F32),
                   jax.ShapeDtypeStruct((D_B, D_MODEL), F32),
                   jax.ShapeDtypeStruct((1, 2 * D_MODEL), F32), jax.ShapeDtypeStruct((1, D_MODEL), F32),
                   jax.ShapeDtypeStruct((1, D_B), F32), jax.ShapeDtypeStruct((1, D_B), F32),
                   jax.ShapeDtypeStruct((N_GROUPS, SGU_CHUNK, SGU_CHUNK), F32),
                   jax.ShapeDtypeStruct((N_GROUPS, SGU_CHUNK, 1), F32), jax.ShapeDtypeStruct((1, 1), F32)),
        scratch_shapes=[pltpu.VMEM((D_A, D_MODEL), BF16), pltpu.VMEM((D_B, D_MODEL), BF16),
                        pltpu.VMEM((D_MODEL, D_MODEL), BF16), pltpu.VMEM((N_GROUPS, SGU_CHUNK, SGU_CHUNK), BF16),
                        pltpu.VMEM((D_MODEL, D_MODEL), F32), pltpu.VMEM((D_A, D_MODEL), F32),
                        pltpu.VMEM((D_B, D_MODEL), F32),
                        pltpu.VMEM((tm, D_B), F32), pltpu.VMEM((tm, D_B), F32),
                        pltpu.SemaphoreType.DMA((3,))],
        compiler_params=_params(56),
    )(x, target, attn_out, *([z] * (N_COL_BLOCKS - 3)), w_pa, w_pb, w_out, b_gate, ln_g, ln_b, w_s, b_s, final_g)


def _proj_bwd_x(dqkv, drest, x, dx2, norm_g, w_in_t, to_chip):
    s = x.shape[0]
    tm = TOKEN_TILE
    nt = s // tm
    n = len(to_chip)

    def body(*refs):
        dqkv_ref, dr_ref, x_ref, dx2_ref, g_ref, w_hbm = refs[:6]
        to_chip_refs = refs[6:6 + n]
        dx_ref, dg_ref = refs[6 + n:8 + n]
        from_chip_refs = refs[8 + n:8 + 2 * n]
        w, sem, send_sems, recv_sems = refs[8 + 2 * n:]
        i = pl.program_id(0)

        @pl.when(i == 0)
        def _():
            for rc in _owner_copies(to_chip_refs, from_chip_refs, send_sems, recv_sems):
                rc.start()
            cp = pltpu.make_async_copy(w_hbm, w, sem)
            cp.start()
            dg_ref[...] = jnp.zeros(dg_ref.shape, F32)
            cp.wait()

        dh = None
        for c in range(N_COL_BLOCKS):
            dz = dqkv_ref[c] if c < 3 else dr_ref[:, (c - 3) * COL_BLOCK:(c - 2) * COL_BLOCK]
            part = _dot(dz, w[c * COL_BLOCK:(c + 1) * COL_BLOCK, :])
            dh = part if dh is None else dh + part
        xf = x_ref[...]
        r = lax.rsqrt(jnp.mean(xf * xf, axis=-1, keepdims=True) + EPS)
        xn = xf * r
        dg_ref[...] += jnp.sum(dh * xn, axis=0, keepdims=True)
        gh = dh * g_ref[...]
        dx_ref[...] = r * (gh - xn * jnp.mean(gh * xn, axis=-1, keepdims=True)) + dx2_ref[...]

        @pl.when(i == nt - 1)
        def _():
            for rc in _owner_copies(to_chip_refs, from_chip_refs, send_sems, recv_sems):
                rc.wait_recv()
                rc.wait_send()

    hbm = pl.BlockSpec(memory_space=pl.ANY)
    return pl.pallas_call(
        body, name="proj_bwd_x",
        grid=(nt,),
        in_specs=[pl.BlockSpec((3, tm, D_A), lambda i: (0, i, 0)),
                  pl.BlockSpec((tm, REST), lambda i: (i, 0)),
                  pl.BlockSpec((tm, D_MODEL), lambda i: (i, 0)),
                  pl.BlockSpec((tm, D_MODEL), lambda i: (i, 0)),
                  pl.BlockSpec((1, D_MODEL), lambda i: (0, 0)),
                  hbm] + [hbm] * n,
        out_specs=(pl.BlockSpec((tm, D_MODEL), lambda i: (i, 0)),
                   pl.BlockSpec((1, D_MODEL), lambda i: (0, 0))) + (hbm,) * n,
        out_shape=(jax.ShapeDtypeStruct((s, D_MODEL), F32), jax.ShapeDtypeStruct((1, D_MODEL), F32))
        + tuple(jax.ShapeDtypeStruct(t.shape, t.dtype) for t in to_chip),
        scratch_shapes=[pltpu.VMEM((D_IN, D_MODEL), BF16), pltpu.SemaphoreType.DMA,
                        pltpu.SemaphoreType.DMA((n, 3)), pltpu.SemaphoreType.DMA((n, 3))],
        compiler_params=_params(48),
    )(dqkv, drest, x, dx2, norm_g, w_in_t, *to_chip)


def _proj_bwd_w(h, dqkv, drest):
    s = h.shape[0]
    tk = min(s, 1024)
    nk = s // tk

    def body(h_ref, dqkv_ref, dr_ref, o_ref, acc):
        j = pl.program_id(0)
        i = pl.program_id(1)

        @pl.when(i == 0)
        def _():
            acc[...] = jnp.zeros(acc.shape, F32)

        @pl.when(j < 3)
        def _():
            acc[...] += _dot(dqkv_ref[...], h_ref[...], TN)

        @pl.when(j >= 3)
        def _():
            acc[...] += _dot(dr_ref[...], h_ref[...], TN)

        @pl.when(i == nk - 1)
        def _():
            o_ref[...] = acc[...].astype(BF16)

    return pl.pallas_call(
        body, name="proj_bwd_w",
        grid=(N_COL_BLOCKS, nk),
        in_specs=[pl.BlockSpec((tk, D_MODEL), lambda j, i: (i, 0)),
                  pl.BlockSpec((None, tk, COL_BLOCK),
                               lambda j, i: (jnp.minimum(j, 2), jnp.where(j < 3, i, nk - 1), 0)),
                  pl.BlockSpec((tk, COL_BLOCK),
                               lambda j, i: (jnp.where(j >= 3, i, 0), jnp.maximum(j - 3, 0)))],
        out_specs=pl.BlockSpec((COL_BLOCK, D_MODEL), lambda j, i: (j, 0)),
        out_shape=jax.ShapeDtypeStruct((D_IN, D_MODEL), BF16),
        scratch_shapes=[pltpu.VMEM((COL_BLOCK, D_MODEL), F32)],
        compiler_params=_params(40),
    )(h, dqkv, drest)


def _adamw_math(w, g, m, v):
    c1 = 1.0 - ADAM_B1 ** ADAM_STEP
    c2 = 1.0 - ADAM_B2 ** ADAM_STEP
    nm = ADAM_B1 * m + (1.0 - ADAM_B1) * g
    nv = ADAM_B2 * v + (1.0 - ADAM_B2) * (g * g)
    return -ADAM_LR * ((nm / c1) / (jnp.sqrt(nv / c2) + ADAM_EPS) + ADAM_WD * w), nm, nv


def _adamw(name, w, g, m, v, from_chip):
    rows, cols = w.shape
    tr = rows if rows * cols <= 512 * 1024 else next(t for t in range(256, 7, -8) if rows % t == 0)

    def body(w_ref, g_ref, m_ref, v_ref, t_ref, g_out, d_ref, nm_ref, nv_ref):
        gg = g_ref[...]
        for j in range(3):
            gg = gg + t_ref[j].astype(F32)
        g_out[...] = gg
        d_ref[...], nm_ref[...], nv_ref[...] = _adamw_math(w_ref[...], gg, m_ref[...], v_ref[...])

    spec = pl.BlockSpec((tr, cols), lambda i: (i, 0))
    shape = jax.ShapeDtypeStruct((rows, cols), F32)
    return pl.pallas_call(
        body, name=name,
        grid=(rows // tr,),
        in_specs=[spec] * 4 + [pl.BlockSpec((3, tr, cols), lambda i: (0, i, 0))],
        out_specs=(spec,) * 4, out_shape=(shape,) * 4,
        compiler_params=_params(32),
    )(w, g, m, v, from_chip)


_SMALL = (("norm_g", (1, D_MODEL)), ("b_gate", (1, 2 * D_MODEL)), ("rel_bias", (N_HEADS, N_REL)),
          ("sgu_ln_g", (1, D_B)), ("sgu_ln_b", (1, D_B)), ("w_s", (N_GROUPS * SGU_CHUNK, SGU_CHUNK)),
          ("b_s", (N_GROUPS, SGU_CHUNK)), ("final_g", (1, D_MODEL)))


def _adamw_small(slab, g_ws, weights, moments_m, moments_v):
    k = len(_SMALL)

    def grad_of(name, slab_ref, ws_ref):
        if name == "norm_g":
            return slab_ref[ROW_NORM_G:ROW_NORM_G + 1, :]
        if name == "b_gate":
            return jnp.concatenate([slab_ref[ROW_B_GATE:ROW_B_GATE + 1, :], slab_ref[ROW_B_GATE + 1:ROW_B_GATE + 2, :]],
                                   axis=1)
        if name == "rel_bias":
            return slab_ref[ROW_REL:ROW_REL + N_HEADS, :N_REL]
        if name == "sgu_ln_g":
            return slab_ref[ROW_LN_G:ROW_LN_G + 1, :D_B]
        if name == "sgu_ln_b":
            return slab_ref[ROW_LN_B:ROW_LN_B + 1, :D_B]
        if name == "w_s":
            return ws_ref[...]
        if name == "b_s":
            return slab_ref[ROW_B_S:ROW_B_S + N_GROUPS, :SGU_CHUNK]
        return slab_ref[ROW_FINAL_G:ROW_FINAL_G + 1, :]

    def body(*refs):
        slab_ref, ws_ref = refs[:2]
        w_refs, m_refs, v_refs = refs[2:2 + k], refs[2 + k:2 + 2 * k], refs[2 + 2 * k:2 + 3 * k]
        outs = refs[2 + 3 * k:]
        for n, (name, _) in enumerate(_SMALL):
            g = grad_of(name, slab_ref, ws_ref)
            outs[n][...] = g
            outs[k + n][...], outs[2 * k + n][...], outs[3 * k + n][...] = _adamw_math(
                w_refs[n][...], g, m_refs[n][...], v_refs[n][...])
        outs[4 * k][...] = slab_ref[ROW_LOSS:ROW_LOSS + 1, :1]

    vmem = pl.BlockSpec(memory_space=pltpu.VMEM)
    shapes = tuple(jax.ShapeDtypeStruct(shape, F32) for _, shape in _SMALL)
    return pl.pallas_call(
        body, name="adamw_small",
        out_shape=shapes * 4 + (jax.ShapeDtypeStruct((1, 1), F32),),
        in_specs=[vmem] * (2 + 3 * k), out_specs=tuple([vmem] * (4 * k + 1)),
        compiler_params=_params(16),
    )(slab, g_ws, *weights, *moments_m, *moments_v)


def _pad_rel(a):
    return jnp.pad(a.reshape(N_HEADS, N_REL), ((0, 0), (0, N_REL_PAD - N_REL)))


def kernel(x, norm_g, w_in, b_gate, rel_bias, sgu_ln_g, sgu_ln_b, w_s, b_s, w_pa, w_pb, w_out, final_g, loss_target, m_norm_g, m_w_in, m_b_gate, m_rel_bias, m_sgu_ln_g, m_sgu_ln_b, m_w_s, m_b_s, m_w_pa, m_w_pb, m_w_out, m_final_g, v_norm_g, v_w_in, v_b_gate, v_rel_bias, v_sgu_ln_g, v_sgu_ln_b, v_w_s, v_b_s, v_w_pa, v_w_pb, v_w_out, v_final_g):
    s = x.shape[1]
    xs = x.reshape(s, D_MODEL)
    tgt = loss_target.reshape(s, D_MODEL)

    bias_table = _bias_table(_pad_rel(rel_bias))
    w_in_t = jnp.swapaxes(w_in[0], 0, 1)
    qkv, h, w_in_t_full = _gather_proj_fwd(xs, norm_g, w_in_t)
    attn_out, g_pa, g_pb, g_out = _attn_fwd(qkv, bias_table, (w_pa[0], w_pb[0], w_out[0]))
    w_pa_full = jnp.transpose(g_pa, (1, 0, 2)).reshape(D_A, D_MODEL)
    w_pb_full = jnp.transpose(g_pb, (1, 0, 2)).reshape(D_B, D_MODEL)
    w_out_full = g_out.reshape(D_MODEL, D_MODEL)

    (dx2, d_attn, drest, dw_out, dw_pa, dw_pb, d_bgate, d_fg, d_lng, d_lnb, d_ws, d_bs, loss_part) = _mid_fwd_bwd(
        xs, tgt, attn_out, qkv, w_pa_full, w_pb_full, w_out_full, b_gate, sgu_ln_g, sgu_ln_b, w_s[0],
        b_s.reshape(N_GROUPS, SGU_CHUNK, 1), final_g.reshape(1, D_MODEL))

    own_pa, own_pb, own_out, tc_pa, tc_pb, tc_out = _reduce_chip(
        "reduce_chip_proj", (dw_pa, dw_pb, dw_out), (1, 1, 0))
    dqkv, dbias, fc_pa, fc_pb, fc_out = _attn_bwd(qkv, bias_table, d_attn, (tc_pa, tc_pb, tc_out))
    d_rel = _bias_grad(dbias)
    dw_in_t = _proj_bwd_w(h, dqkv, drest)
    own_in, tc_in = _reduce_chip("reduce_chip_in", (dw_in_t,), (0,))
    grad_x, d_ng, fc_in = _proj_bwd_x(dqkv, drest, xs, dx2, norm_g, w_in_t_full, (tc_in,))
    big = {"w_in": tuple(jnp.swapaxes(t, 0, 1)[None] for t in _adamw(
        "adamw_w_in", w_in_t, own_in, jnp.swapaxes(m_w_in[0], 0, 1), jnp.swapaxes(v_w_in[0], 0, 1), fc_in))}
    for name, w, g, fc, m, v in (("w_pa", w_pa, own_pa, fc_pa, m_w_pa, v_w_pa),
                                 ("w_pb", w_pb, own_pb, fc_pb, m_w_pb, v_w_pb),
                                 ("w_out", w_out, own_out, fc_out, m_w_out, v_w_out)):
        big[name] = tuple(t[None] for t in _adamw("adamw_" + name, w[0], g, m[0], v[0], fc))

    slab, g_ws = _reduce_small(d_ng, d_bgate, d_rel, d_lng, d_lnb, d_fg, loss_part, d_bs, d_ws)
    as_2d = lambda leaves: [a.reshape(shape) for a, (_, shape) in zip(leaves, _SMALL)]
    small_out = _adamw_small(
        slab, g_ws, as_2d((norm_g, b_gate, rel_bias, sgu_ln_g, sgu_ln_b, w_s, b_s, final_g)),
        as_2d((m_norm_g, m_b_gate, m_rel_bias, m_sgu_ln_g, m_sgu_ln_b, m_w_s, m_b_s, m_final_g)),
        as_2d((v_norm_g, v_b_gate, v_rel_bias, v_sgu_ln_g, v_sgu_ln_b, v_w_s, v_b_s, v_final_g)))
    small_index = {name: n for n, (name, _) in enumerate(_SMALL)}

    def leaf(kind, name, like):
        if name in big:
            return big[name][kind]
        return small_out[kind * len(_SMALL) + small_index[name]].reshape(like.shape)

    weights = (("norm_g", norm_g), ("w_in", w_in), ("b_gate", b_gate), ("rel_bias", rel_bias), ("sgu_ln_g", sgu_ln_g),
               ("sgu_ln_b", sgu_ln_b), ("w_s", w_s), ("b_s", b_s), ("w_pa", w_pa), ("w_pb", w_pb), ("w_out", w_out),
               ("final_g", final_g))
    outs = [small_out[-1].reshape(()), grad_x.reshape(x.shape)]
    for kind in range(4):
        outs.extend(leaf(kind, name, like) for name, like in weights)
    return tuple(outs)
```

```python
import functools
import math

import jax
import jax.numpy as jnp
from jax import lax
from jax.experimental import pallas as pl
from jax.experimental.pallas import tpu as pltpu

F32 = jnp.float32
BF16 = jnp.bfloat16
MESH = pl.DeviceIdType.MESH
N_DEV = 8

D_MODEL = 1024
D_A = 512
D_B = 512
D_IN = 5632
N_HEADS = 8
HEAD_DIM = 64
CHUNK = 64
N_PREV = 8
REL_CLIP = 128
N_REL = 2 * REL_CLIP + 1
N_REL_PAD = 384
SGU_CHUNK = 128
N_GROUPS = 4
EPS = 1e-6
NEG_INF = -1e30
Q_SCALE = HEAD_DIM ** -0.5

Q_BLOCK = 256
K_SPAN = 768
Z_PAD = K_SPAN - Q_BLOCK
ROLL_W = 1024
COL_BLOCK = 512
N_COL_BLOCKS = D_IN // COL_BLOCK
REST = D_IN - 3 * D_A
TOKEN_TILE = 256
V7X_VMEM_BYTES = 64 * 1024 * 1024

ADAM_LR = 0.001
ADAM_B1 = 0.9
ADAM_B2 = 0.999
ADAM_EPS = 1e-08
ADAM_WD = 0.01
ADAM_STEP = 10

GELU_C = math.sqrt(2.0 / math.pi)
GELU_A = 0.044715

NT = (((1,), (1,)), ((), ()))
TN = (((0,), (0,)), ((), ()))
HIGHEST = lax.Precision.HIGHEST


def _params(vmem_mb, **kw):
    return pltpu.CompilerParams(vmem_limit_bytes=vmem_mb * 1024 * 1024, **kw)


def _dot(a, b, dims=None):
    if dims is None:
        return jnp.dot(a, b, preferred_element_type=F32)
    return lax.dot_general(a, b, dims, preferred_element_type=F32)


def _sigmoid(x):
    return 1.0 / (1.0 + jnp.exp(-x))


def _gelu_and_grad(u):
    u2 = u * u
    t = jnp.tanh(GELU_C * (u + GELU_A * u * u2))
    half = 0.5 * (1.0 + t)
    g = u * half
    dg = half + 0.5 * u * (1.0 - t * t) * (GELU_C * (1.0 + 3.0 * GELU_A * u2))
    return g, dg


def _my_pos():
    return lax.axis_index("x"), lax.axis_index("y"), lax.axis_index("c")


def _flat_id(pos):
    return 4 * pos[0] + 2 * pos[1] + pos[2]


def _peer(pos, k):
    x, y, c = pos
    return (1 - x if k & 4 else x, 1 - y if k & 2 else y, 1 - c if k & 1 else c)


def _other_chips(pos):
    x, y, _ = pos
    return ((1 - x, y), (x, 1 - y), (1 - x, 1 - y))


class _SlotGather:
    def __init__(self, bufs, send_sems, recv_sems, own=None):
        self.bufs, self.send_sems, self.recv_sems = bufs, send_sems, recv_sems
        self.own = own if own is not None else [None] * len(bufs)
        x, y, c = _my_pos()
        self.c, self.me, self.sib = c, (x, y, c), (x, y, 1 - c)
        self.chips = _other_chips(self.me)

    def _copy(self, a, k, block, to):
        slot = _flat_id(block)
        src = self.own[a] if (k < 4 and self.own[a] is not None) else self.bufs[a].at[slot]
        return pltpu.make_async_remote_copy(
            src_ref=src, dst_ref=self.bufs[a].at[slot],
            send_sem=self.send_sems.at[a, k], recv_sem=self.recv_sems.at[a, k], device_id=to, device_id_type=MESH)

    def _own_sends(self):
        n = len(self.bufs)
        return ([self._copy(a, 1 + j, self.me, (*chip, self.c)) for j, chip in enumerate(self.chips) for a in range(n)]
                + [self._copy(a, 0, self.me, self.sib) for a in range(n)])

    def _passes(self):
        return [self._copy(a, 4 + j, (*chip, self.c), self.sib)
                for j, chip in enumerate(self.chips) for a in range(len(self.bufs))]

    def start(self):
        for cp in self._own_sends():
            cp.start()

    def pass_on(self):
        for j, chip in enumerate(self.chips):
            for a in range(len(self.bufs)):
                self._copy(a, 1 + j, (*chip, self.c), self.me).wait_recv()
                self._copy(a, 4 + j, (*chip, self.c), self.sib).start()

    def finish(self):
        for a in range(len(self.bufs)):
            self._copy(a, 0, self.sib, self.me).wait_recv()
            for j, chip in enumerate(self.chips):
                self._copy(a, 4 + j, (*chip, 1 - self.c), self.me).wait_recv()
        for cp in self._own_sends() + self._passes():
            cp.wait_send()


def _reduce_chip(name, parts, sharded_dim):
    n = len(parts)
    shapes = []
    for p, dim in zip(parts, sharded_dim):
        shape = list(p.shape)
        shape[dim] //= N_DEV
        shapes.append(tuple(shape))

    def body(*refs):
        full, own, to_chip = refs[:n], refs[n:2 * n], refs[2 * n:3 * n]
        ins, from_sib = refs[3 * n:4 * n], refs[4 * n:5 * n]
        send_sems, recv_sems = refs[5 * n], refs[5 * n + 1]
        x, y, c = _my_pos()
        sib = (x, y, 1 - c)
        chips = ((x, y),) + _other_chips((x, y, c))
        for a in range(n):
            rows, cols = shapes[a]
            for d in range(N_DEV):
                if sharded_dim[a] == 0:
                    ins[a][d] = full[a][d * rows:(d + 1) * rows, :].astype(BF16)
                else:
                    ins[a][d] = full[a][:, d * cols:(d + 1) * cols].astype(BF16)

        def to_sibling(a, r):
            return pltpu.make_async_remote_copy(
                src_ref=ins[a].at[_flat_id((*chips[r], 1 - c))], dst_ref=from_sib[a].at[r],
                send_sem=send_sems.at[a, r], recv_sem=recv_sems.at[a, r], device_id=sib, device_id_type=MESH)

        sends = [to_sibling(a, r) for r in (1, 2, 3, 0) for a in range(n)]
        for cp in sends:
            cp.start()
        for r in (1, 2, 3, 0):
            for a in range(n):
                to_sibling(a, r).wait_recv()
                both = ins[a][_flat_id((*chips[r], c))].astype(F32) + from_sib[a][r].astype(F32)
                if r == 0:
                    own[a][...] = both
                else:
                    to_chip[a][r - 1] = both.astype(BF16)
        for cp in sends:
            cp.wait_send()

    vmem = pl.BlockSpec(memory_space=pltpu.VMEM)
    return pl.pallas_call(
        body, name=name,
        out_shape=tuple(jax.ShapeDtypeStruct(sh, F32) for sh in shapes)
        + tuple(jax.ShapeDtypeStruct((3,) + sh, BF16) for sh in shapes),
        in_specs=[vmem] * n, out_specs=tuple([vmem] * (2 * n)),
        scratch_shapes=[pltpu.VMEM((N_DEV,) + sh, BF16) for sh in shapes]
        + [pltpu.VMEM((4,) + sh, BF16) for sh in shapes]
        + [pltpu.SemaphoreType.DMA((n, 4)), pltpu.SemaphoreType.DMA((n, 4))],
        compiler_params=_params(56),
    )(*parts)


def _owner_copies(to_chip, from_chip, send_sems, recv_sems):
    x, y, c = _my_pos()
    return [pltpu.make_async_remote_copy(
        src_ref=to_chip[a].at[j], dst_ref=from_chip[a].at[j],
        send_sem=send_sems.at[a, j], recv_sem=recv_sems.at[a, j], device_id=(*chip, c), device_id_type=MESH)
        for a in range(len(to_chip)) for j, chip in enumerate(_other_chips((x, y, c)))]


ROW_NORM_G, ROW_B_GATE, ROW_LN_G, ROW_LN_B, ROW_FINAL_G, ROW_LOSS, ROW_REL, ROW_B_S, SLAB_ROWS = 0, 1, 3, 4, 5, 6, 8, 16, 24


def _reduce_small(d_ng, d_bgate, d_rel, d_lng, d_lnb, d_fg, loss, d_bs, d_ws):
    def body(ng_ref, bg_ref, rel_ref, lng_ref, lnb_ref, fg_ref, loss_ref, bs_ref, ws_ref, slab_out, ws_out,
             slab_land, ws_land, send_sems, recv_sems):
        me = _flat_id(_my_pos())
        slab_land[me] = jnp.zeros((SLAB_ROWS, D_MODEL), F32)
        slab_land[me, ROW_NORM_G:ROW_NORM_G + 1, :] = ng_ref[...]
        slab_land[me, ROW_B_GATE:ROW_B_GATE + 1, :] = bg_ref[:, :D_MODEL]
        slab_land[me, ROW_B_GATE + 1:ROW_B_GATE + 2, :] = bg_ref[:, D_MODEL:]
        slab_land[me, ROW_LN_G:ROW_LN_G + 1, :D_B] = lng_ref[...]
        slab_land[me, ROW_LN_B:ROW_LN_B + 1, :D_B] = lnb_ref[...]
        slab_land[me, ROW_FINAL_G:ROW_FINAL_G + 1, :] = fg_ref[...]
        slab_land[me, ROW_LOSS:ROW_LOSS + 1, :1] = loss_ref[...]
        slab_land[me, ROW_REL:ROW_REL + N_HEADS, :N_REL_PAD] = rel_ref[...]
        eye = (lax.broadcasted_iota(jnp.int32, (SGU_CHUNK, SGU_CHUNK), 0)
               == lax.broadcasted_iota(jnp.int32, (SGU_CHUNK, SGU_CHUNK), 1))
        for g in range(N_GROUPS):
            row = jnp.sum(jnp.where(eye, bs_ref[g], 0.0), axis=0, keepdims=True)
            slab_land[me, ROW_B_S + g:ROW_B_S + g + 1, :SGU_CHUNK] = row
        ws_land[me] = ws_ref[...]
        gather = _SlotGather([slab_land, ws_land], send_sems, recv_sems)
        gather.start()
        gather.pass_on()
        gather.finish()
        for land, out in ((slab_land, slab_out), (ws_land, ws_out)):
            acc = land[0]
            for d in range(1, N_DEV):
                acc = acc + land[d]
            out[...] = acc

    vmem = pl.BlockSpec(memory_space=pltpu.VMEM)
    ws_shape = (N_GROUPS * SGU_CHUNK, SGU_CHUNK)
    return pl.pallas_call(
        body, name="reduce_small",
        out_shape=(jax.ShapeDtypeStruct((SLAB_ROWS, D_MODEL), F32), jax.ShapeDtypeStruct(ws_shape, F32)),
        in_specs=[vmem] * 9, out_specs=(vmem, vmem),
        scratch_shapes=[pltpu.VMEM((N_DEV, SLAB_ROWS, D_MODEL), F32), pltpu.VMEM((N_DEV,) + ws_shape, F32),
                        pltpu.SemaphoreType.DMA((2, N_DEV - 1)), pltpu.SemaphoreType.DMA((2, N_DEV - 1))],
        compiler_params=_params(16),
    )(d_ng, d_bgate, d_rel, d_lng, d_lnb, d_fg, loss, d_bs, d_ws.reshape(ws_shape))


def _rel_index(e):
    lo, hi = Z_PAD - REL_CLIP, Z_PAD + REL_CLIP
    return jnp.where(e <= lo, 2 * REL_CLIP, jnp.where(e < hi, hi - e, jnp.where(e <= K_SPAN, 0, 2 * REL_CLIP)))


def _bias_table(rel_bias_pad):
    def body(rb_ref, bt_ref):
        c = lax.broadcasted_iota(jnp.int32, (N_REL_PAD, ROLL_W), 1)
        r = lax.broadcasted_iota(jnp.int32, (N_REL_PAD, ROLL_W), 0)
        pick = (r == _rel_index(c)).astype(F32)
        rows = jnp.dot(rb_ref[...], pick, precision=HIGHEST, preferred_element_type=F32)
        qc = lax.broadcasted_iota(jnp.int32, (Q_BLOCK, K_SPAN), 0) >> 6
        kc = lax.broadcasted_iota(jnp.int32, (Q_BLOCK, K_SPAN), 1) >> 6
        band = (kc >= qc) & (kc <= qc + N_PREV)
        for h in range(N_HEADS):
            t = jnp.broadcast_to(rows[h:h + 1, :], (Q_BLOCK, ROLL_W))
            t = pltpu.roll(t, 0, 1, stride=1, stride_axis=0)
            bt_ref[h] = jnp.where(band, t[:, :K_SPAN], NEG_INF)

    return pl.pallas_call(
        body, name="bias_table",
        out_shape=jax.ShapeDtypeStruct((N_HEADS, Q_BLOCK, K_SPAN), F32),
        compiler_params=_params(32),
    )(rel_bias_pad)


def _bias_grad(dbias):
    def body(a_ref, o_ref):
        rr = lax.broadcasted_iota(jnp.int32, (Q_BLOCK, Q_BLOCK), 0)
        cc = lax.broadcasted_iota(jnp.int32, (Q_BLOCK, Q_BLOCK), 1)
        flip = (rr + cc == Q_BLOCK - 1).astype(F32)
        c = lax.broadcasted_iota(jnp.int32, (ROLL_W, N_REL_PAD), 0)
        r = lax.broadcasted_iota(jnp.int32, (ROLL_W, N_REL_PAD), 1)
        e = jnp.where(c >= Q_BLOCK - 1, c - (Q_BLOCK - 1), c + (ROLL_W - Q_BLOCK + 1))
        pick = (r == _rel_index(e)).astype(F32)
        sums = []
        for h in range(N_HEADS):
            a = jnp.dot(flip, a_ref[h], precision=HIGHEST, preferred_element_type=F32)
            a = jnp.concatenate([a, jnp.zeros((Q_BLOCK, ROLL_W - K_SPAN), F32)], axis=1)
            a = pltpu.roll(a, 0, 1, stride=1, stride_axis=0)
            sums.append(jnp.sum(a, axis=0, keepdims=True))
        diag = jnp.concatenate(sums, axis=0)
        o_ref[...] = jnp.dot(diag, pick, precision=HIGHEST, preferred_element_type=F32)

    return pl.pallas_call(
        body, name="bias_grad",
        out_shape=jax.ShapeDtypeStruct((N_HEADS, N_REL_PAD), F32),
        compiler_params=_params(32),
    )(dbias)


def _gather_proj_fwd(x, norm_g, w_in_t):
    s = x.shape[0]
    tm = 512 if s % 512 == 0 else TOKEN_TILE
    nt = s // tm
    n_pad = Z_PAD // tm
    shard_w = w_in_t.shape[0]
    chip_w = 2 * shard_w
    n_chips = N_DEV // 2

    def body(order_ref, x_ref, g_ref, win_hbm, z_ref, h_ref, wt_hbm, stage, wchip, hb, win_f32, send_sems, recv_sems,
             local_sems):
        j = pl.program_id(0)
        i = pl.program_id(1)
        x_, y_, c_ = _my_pos()
        me, sib = (x_, y_, c_), (x_, y_, 1 - c_)
        near = _other_chips(me)
        pick = lambda a, b: tuple(jnp.where(c_ == 0, u, v) for u, v in zip(a, b))
        passed_from, passed_to = pick(near[0], near[1]), pick(near[1], near[0])

        def rows_of(block):
            return wt_hbm.at[pl.ds(pl.multiple_of(_flat_id(block) * shard_w, 16), shard_w), :]

        def copy(k, block, to, own=False):
            return pltpu.make_async_remote_copy(
                src_ref=stage if own else rows_of(block), dst_ref=rows_of(block),
                send_sem=send_sems.at[k], recv_sem=recv_sems.at[k], device_id=to, device_id_type=MESH)

        def sends():
            return ([copy(0, me, sib, True), copy(1, me, (*near[0], c_), True), copy(2, me, (*near[1], c_), True),
                     copy(3, (*passed_from, c_), (*passed_to, c_))]
                    + [copy(4 + n, (*near[n], c_), sib) for n in range(3)])

        keep = pltpu.make_async_copy(stage, rows_of(me), local_sems.at[0])

        def fetch(chip):
            first = pl.multiple_of((2 * chip[0] + chip[1]) * chip_w, 16)
            cp = pltpu.make_async_copy(wt_hbm.at[pl.ds(first, chip_w), :], wchip, local_sems.at[1])
            cp.start()
            cp.wait()

        @pl.when((j == 0) & (i == 0))
        def _():
            load = pltpu.make_async_copy(win_hbm, win_f32, local_sems.at[1])
            load.start()
            load.wait()
            stage[...] = win_f32[...].astype(BF16)
            keep.start()
            for cp in sends()[:3]:
                cp.start()
            copy(0, sib, me).wait_recv()
            keep.wait()
            fetch((x_, y_))

        @pl.when((j == 1) & (i == 0))
        def _():
            copy(1, (*near[0], c_), me).wait_recv()
            copy(2, (*near[1], c_), me).wait_recv()
            for cp in sends()[3:6]:
                cp.start()
            copy(4, (*near[0], 1 - c_), me).wait_recv()
            fetch(near[0])

        @pl.when((j == 2) & (i == 0))
        def _():
            copy(5, (*near[1], 1 - c_), me).wait_recv()
            fetch(near[1])

        @pl.when((j == 3) & (i == 0))
        def _():
            copy(3, (*near[2], c_), me).wait_recv()
            copy(6, (*near[2], c_), sib).start()
            copy(6, (*near[2], 1 - c_), me).wait_recv()
            fetch(near[2])

        @pl.when(i < n_pad)
        def _():
            z_ref[...] = jnp.zeros(z_ref.shape, BF16)

        @pl.when(i >= n_pad)
        def _():
            rows = pl.ds(pl.multiple_of((i - n_pad) * tm, tm), tm)

            @pl.when(j == 0)
            def _():
                xf = x_ref[...]
                r = lax.rsqrt(jnp.mean(xf * xf, axis=-1, keepdims=True) + EPS)
                hf = (xf * r * g_ref[...]).astype(BF16)
                hb[rows, :] = hf
                h_ref[...] = hf

            blk = _dot(hb[rows, :], wchip[...], NT)
            q_scale = jnp.where(order_ref[j] == 0, Q_SCALE, 1.0).astype(F32)
            z_ref[:, :D_A] = (blk[:, :D_A] * q_scale).astype(BF16)
            z_ref[:, D_A:] = blk[:, D_A:].astype(BF16)

        @pl.when((j == n_chips - 1) & (i == n_pad + nt - 1))
        def _():
            for cp in sends():
                cp.wait_send()

    pos = _my_pos()
    order = jnp.stack([2 * cx + cy for cx, cy in ((pos[0], pos[1]),) + _other_chips(pos)]).astype(jnp.int32)
    first_pass = lambda j, i: jnp.where(j == 0, jnp.maximum(i - n_pad, 0), nt - 1)
    grid_spec = pltpu.PrefetchScalarGridSpec(
        num_scalar_prefetch=1,
        grid=(n_chips, n_pad + nt),
        in_specs=[pl.BlockSpec((tm, D_MODEL), lambda j, i, o: (first_pass(j, i), 0)),
                  pl.BlockSpec((1, D_MODEL), lambda j, i, o: (0, 0)),
                  pl.BlockSpec(memory_space=pl.ANY)],
        out_specs=(pl.BlockSpec((tm, chip_w), lambda j, i, o: (i, o[j])),
                   pl.BlockSpec((tm, D_MODEL), lambda j, i, o: (first_pass(j, i), 0)),
                   pl.BlockSpec(memory_space=pl.ANY)),
        scratch_shapes=[pltpu.VMEM((shard_w, D_MODEL), BF16), pltpu.VMEM((chip_w, D_MODEL), BF16),
                        pltpu.VMEM((s, D_MODEL), BF16), pltpu.VMEM(w_in_t.shape, F32),
                        pltpu.SemaphoreType.DMA((N_DEV - 1,)), pltpu.SemaphoreType.DMA((N_DEV - 1,)),
                        pltpu.SemaphoreType.DMA((2,))])
    return pl.pallas_call(
        body, name="gather_proj_fwd",
        grid_spec=grid_spec,
        out_shape=(jax.ShapeDtypeStruct((Z_PAD + s, D_IN), BF16), jax.ShapeDtypeStruct((s, D_MODEL), BF16),
                   jax.ShapeDtypeStruct((D_IN, D_MODEL), BF16)),
        compiler_params=_params(60),
    )(order, x, norm_g, w_in_t)


def _attn_specs(rows):
    pairs = N_HEADS // 2
    return ([pl.BlockSpec((rows, 128), functools.partial(lambda which, p: (0, which * pairs + p), which))
             for which in range(3)]
            + [pl.BlockSpec((2, Q_BLOCK, K_SPAN), lambda p: (p, 0, 0))])


def _head_masks():
    lane = lax.broadcasted_iota(jnp.int32, (1, 128), 1)
    first = lane < HEAD_DIM
    return (first, jnp.logical_not(first))


N_EDGE_BLOCKS = Z_PAD // Q_BLOCK
SPARE_LANE = (127, 0)


def _exp_scores(qm, kcat, bias, b, edge):
    s = _dot(qm, kcat, NT) + bias
    if edge:
        valid = lax.broadcasted_iota(jnp.int32, (1, K_SPAN), 1) >= Z_PAD - b * Q_BLOCK
        s = jnp.where(valid, s, NEG_INF)
    return jnp.exp(s - jnp.max(s, axis=-1, keepdims=True))


def _for_query_blocks(nb, block):
    for b in range(min(N_EDGE_BLOCKS, nb)):
        block(b, True)

    def step(b, carry):
        block(b, False)
        return carry

    lax.fori_loop(N_EDGE_BLOCKS, nb, step, 0, unroll=2)


def _attn_fwd(qkv, bias_table, shards):
    s = qkv.shape[0] - Z_PAD
    nb = s // Q_BLOCK
    n = len(shards)
    pairs = N_HEADS // 2

    def body(*refs):
        q_ref, k_ref, v_ref, bt_ref = refs[:4]
        shard_refs = refs[4:4 + n]
        o_ref = refs[4 + n]
        slot_refs = refs[5 + n:5 + 2 * n]
        stages = refs[5 + 2 * n:5 + 3 * n]
        send_sems, recv_sems, local_sems = refs[5 + 3 * n:]
        p_id = pl.program_id(0)
        gather = _SlotGather(slot_refs, send_sems, recv_sems, own=stages)
        keep = [pltpu.make_async_copy(stages[a], slot_refs[a].at[_flat_id(_my_pos())], local_sems.at[a])
                for a in range(n)]

        @pl.when(p_id == 0)
        def _():
            for a in range(n):
                stages[a][...] = shard_refs[a][...].astype(BF16)
                keep[a].start()
            gather.start()

        @pl.when(p_id == 1)
        def _():
            gather.pass_on()

        masks = _head_masks()
        lane = lax.broadcasted_iota(jnp.int32, (1, 128), 1)
        zero, one = jnp.zeros((), BF16), jnp.ones((), BF16)

        def block(b, edge):
            r0 = b * Q_BLOCK if edge else pl.multiple_of(b * Q_BLOCK, Q_BLOCK)
            q = q_ref[pl.ds(r0 + Z_PAD, Q_BLOCK), :]
            kcat = k_ref[pl.ds(r0, K_SPAN), :]
            vcat = v_ref[pl.ds(r0, K_SPAN), :]
            outs = []
            for hh, mask in enumerate(masks):
                e = _exp_scores(jnp.where(mask, q, zero), kcat, bt_ref[hh], b, edge)
                v_ones = jnp.where(mask, vcat, jnp.where(lane == SPARE_LANE[hh], one, zero))
                o = _dot(e.astype(BF16), v_ones)
                outs.append(o * (1.0 / o[:, SPARE_LANE[hh]:SPARE_LANE[hh] + 1]))
            o_ref[pl.ds(r0, Q_BLOCK), :] = jnp.where(masks[0], outs[0], outs[1])

        _for_query_blocks(nb, block)

        @pl.when(p_id == pairs - 1)
        def _():
            gather.finish()
            for cp in keep:
                cp.wait()

    hbm = pl.BlockSpec(memory_space=pl.ANY)
    return pl.pallas_call(
        body, name="attn_fwd",
        grid=(pairs,),
        in_specs=_attn_specs(s + Z_PAD) + [pl.BlockSpec(a.shape, lambda p: (0, 0)) for a in shards],
        out_specs=(pl.BlockSpec((s, 128), lambda p: (0, p)),) + (hbm,) * n,
        out_shape=(jax.ShapeDtypeStruct((s, D_A), F32),)
        + tuple(jax.ShapeDtypeStruct((N_DEV,) + a.shape, BF16) for a in shards),
        scratch_shapes=[pltpu.VMEM(a.shape, BF16) for a in shards]
        + [pltpu.SemaphoreType.DMA((n, N_DEV - 1)), pltpu.SemaphoreType.DMA((n, N_DEV - 1)),
           pltpu.SemaphoreType.DMA((n,))],
        compiler_params=_params(48),
    )(qkv, qkv, qkv, bias_table, *shards)


def _attn_bwd(qkv, bias_table, d_out, attn_out, to_chip):
    s = qkv.shape[0] - Z_PAD
    nb = s // Q_BLOCK
    n = len(to_chip)
    pairs = N_HEADS // 2

    def body(*refs):
        q_ref, k_ref, v_ref, bt_ref, do_ref, o_ref = refs[:6]
        to_chip_refs = refs[6:6 + n]
        dqkv_ref, db_ref = refs[6 + n:8 + n]
        from_chip_refs = refs[8 + n:8 + 2 * n]
        dk_acc, dv_acc, send_sems, recv_sems = refs[8 + 2 * n:]
        p_id = pl.program_id(0)

        @pl.when(p_id == 0)
        def _():
            for cp in _owner_copies(to_chip_refs, from_chip_refs, send_sems, recv_sems):
                cp.start()

        dk_acc[...] = jnp.zeros(dk_acc.shape, F32)
        dv_acc[...] = jnp.zeros(dv_acc.shape, F32)
        db_ref[...] = jnp.zeros(db_ref.shape, F32)
        masks = _head_masks()
        zero = jnp.zeros((), BF16)
        ones = jnp.ones((K_SPAN, 128), BF16)

        def block(b, edge):
            r0 = b * Q_BLOCK if edge else pl.multiple_of(b * Q_BLOCK, Q_BLOCK)
            q = q_ref[pl.ds(r0 + Z_PAD, Q_BLOCK), :]
            do = do_ref[pl.ds(r0, Q_BLOCK), :]
            kcat = k_ref[pl.ds(r0, K_SPAN), :]
            vcat = v_ref[pl.ds(r0, K_SPAN), :]
            do_o = do.astype(F32) * o_ref[pl.ds(r0, Q_BLOCK), :]
            dq = dk = dv = None
            for hh, mask in enumerate(masks):
                qm = jnp.where(mask, q, zero)
                dom = jnp.where(mask, do, zero)
                e = _exp_scores(qm, kcat, bt_ref[hh], b, edge)
                row_sum = _dot(e.astype(BF16), ones)[:, :1]
                p = e * (1.0 / row_sum)
                delta = jnp.sum(jnp.where(mask, do_o, 0.0), axis=-1, keepdims=True)
                ds = p * (_dot(dom, vcat, NT) - delta)
                db_ref[hh] += ds
                dsb = ds.astype(BF16)
                dq_h = _dot(dsb, jnp.where(mask, kcat, zero))
                dk_h = _dot(dsb, qm, TN)
                dv_h = _dot(p.astype(BF16), dom, TN)
                dq = dq_h if dq is None else dq + dq_h
                dk = dk_h if dk is None else dk + dk_h
                dv = dv_h if dv is None else dv + dv_h
            dqkv_ref[0, pl.ds(r0, Q_BLOCK), :] = (dq * Q_SCALE).astype(BF16)
            dk_acc[pl.ds(r0, K_SPAN), :] += dk
            dv_acc[pl.ds(r0, K_SPAN), :] += dv

        _for_query_blocks(nb, block)
        dqkv_ref[1] = dk_acc[Z_PAD:, :].astype(BF16)
        dqkv_ref[2] = dv_acc[Z_PAD:, :].astype(BF16)

        @pl.when(p_id == pairs - 1)
        def _():
            for cp in _owner_copies(to_chip_refs, from_chip_refs, send_sems, recv_sems):
                cp.wait_recv()
                cp.wait_send()

    hbm = pl.BlockSpec(memory_space=pl.ANY)
    return pl.pallas_call(
        body, name="attn_bwd",
        grid=(pairs,),
        in_specs=_attn_specs(s + Z_PAD) + [pl.BlockSpec((s, 128), lambda p: (0, p))] * 2 + [hbm] * n,
        out_specs=(pl.BlockSpec((3, s, 128), lambda p: (0, 0, p)),
                   pl.BlockSpec((2, Q_BLOCK, K_SPAN), lambda p: (p, 0, 0))) + (hbm,) * n,
        out_shape=(jax.ShapeDtypeStruct((3, s, D_A), BF16),
                   jax.ShapeDtypeStruct((N_HEADS, Q_BLOCK, K_SPAN), F32))
        + tuple(jax.ShapeDtypeStruct(t.shape, t.dtype) for t in to_chip),
        scratch_shapes=[pltpu.VMEM((s + Z_PAD, 128), F32), pltpu.VMEM((s + Z_PAD, 128), F32),
                        pltpu.SemaphoreType.DMA((n, 3)), pltpu.SemaphoreType.DMA((n, 3))],
        compiler_params=_params(56),
    )(qkv, qkv, qkv, bias_table, d_out, attn_out, *to_chip)


def _mid_fwd_bwd(x, target, attn_out, z, w_pa, w_pb, w_out, b_gate, ln_g, ln_b, w_s, b_s, final_g):
    s = x.shape[0]
    tm = TOKEN_TILE
    nt = s // tm
    n_sub = tm // SGU_CHUNK

    def body(x_ref, t_ref, oa_ref, ga_ref, ub_ref, vb_ref, gb_ref, ta0_ref, ta1_ref, tb0_ref, tb1_ref,
             wpa_hbm, wpb_hbm, wout_hbm, bg_ref, lng_ref, lnb_ref, ws_ref, bs_ref, fg_ref,
             dx2_ref, doa_ref, dz_ref, dwout_hbm, dwpa_hbm, dwpb_hbm, dbg_ref, dfg_ref, dlng_ref, dlnb_ref, dws_ref,
             dbs_ref, loss_ref,
             wpa, wpb, wout, wmix, acc_out, acc_pa, acc_pb, mixed_s, dvn_s, sem):
        i = pl.program_id(0)

        @pl.when(i == 0)
        def _():
            loads = [pltpu.make_async_copy(src, dst, sem.at[n])
                     for n, (src, dst) in enumerate(((wpa_hbm, wpa), (wpb_hbm, wpb), (wout_hbm, wout)))]
            for cp in loads:
                cp.start()
            t_idx = lax.broadcasted_iota(jnp.int32, (SGU_CHUNK, SGU_CHUNK), 0)
            s_idx = lax.broadcasted_iota(jnp.int32, (SGU_CHUNK, SGU_CHUNK), 1)
            for g in range(N_GROUPS):
                wmix[g] = jnp.where(s_idx <= t_idx, ws_ref[g], 0.0).astype(BF16)
            for ref in (acc_out, acc_pa, acc_pb, dbg_ref, dfg_ref, dlng_ref, dlnb_ref, dws_ref, dbs_ref, loss_ref):
                ref[...] = jnp.zeros(ref.shape, F32)
            for cp in loads:
                cp.wait()

        g_a = ga_ref[...].astype(F32)
        u_b = ub_ref[...].astype(F32)
        v_b = vb_ref[...].astype(F32)
        g_b = gb_ref[...].astype(F32)
        bg = bg_ref[...]

        sg_a = _sigmoid(g_a)
        silu_a = g_a * sg_a
        o_a = oa_ref[...]
        y_a = (o_a * silu_a).astype(BF16)

        ug, dgelu_u = _gelu_and_grad(u_b)
        vg, dgelu_v = _gelu_and_grad(v_b)
        mu = jnp.mean(vg, axis=-1, keepdims=True)
        vc = vg - mu
        rstd = lax.rsqrt(jnp.mean(vc * vc, axis=-1, keepdims=True) + EPS)
        vhat = vc * rstd
        lng = lng_ref[...]
        vn = (vhat * lng + lnb_ref[...]).astype(BF16)
        for n in range(n_sub):
            rows = slice(n * SGU_CHUNK, (n + 1) * SGU_CHUNK)
            for g in range(N_GROUPS):
                cols = slice(g * 128, (g + 1) * 128)
                mixed_s[rows, cols] = _dot(wmix[g], vn[rows, cols]) + bs_ref[g]
        mixed = mixed_s[...]
        sg_b = _sigmoid(g_b)
        silu_b = g_b * sg_b
        um = ug * mixed
        y_b = (um * silu_b).astype(BF16)

        p_a = _dot(y_a, wpa[...])
        p_b = _dot(y_b, wpb[...])
        gate_a = _sigmoid(jnp.concatenate([ta0_ref[...], ta1_ref[...]], axis=1).astype(F32) + bg[:, :D_MODEL])
        gate_b = _sigmoid(jnp.concatenate([tb0_ref[...], tb1_ref[...]], axis=1).astype(F32) + bg[:, D_MODEL:])
        merged = (gate_a * p_a + gate_b * p_b).astype(BF16)
        x2 = x_ref[...] + _dot(merged, wout[...])
        r2 = lax.rsqrt(jnp.mean(x2 * x2, axis=-1, keepdims=True) + EPS)
        xh = x2 * r2
        fg = fg_ref[...]
        err = xh * fg - t_ref[...]
        loss_ref[...] += jnp.sum(jnp.sum(err * err, axis=-1, keepdims=True), axis=0, keepdims=True) * (0.5 / D_MODEL)

        dy = err * (1.0 / D_MODEL)
        dfg_ref[...] += jnp.sum(dy * xh, axis=0, keepdims=True)
        gy = dy * fg
        dx2 = r2 * (gy - xh * jnp.mean(gy * xh, axis=-1, keepdims=True))
        dx2_ref[...] = dx2
        dx2b = dx2.astype(BF16)
        dmerged = _dot(dx2b, wout[...], NT)
        acc_out[...] += _dot(merged, dx2b, TN)

        dp_a = dmerged * gate_a
        dp_b = dmerged * gate_b
        dgate_a = dp_a * p_a * (1.0 - gate_a)
        dgate_b = dp_b * p_b * (1.0 - gate_b)
        dbg_ref[:, :D_MODEL] += jnp.sum(dgate_a, axis=0, keepdims=True)
        dbg_ref[:, D_MODEL:] += jnp.sum(dgate_b, axis=0, keepdims=True)
        dz_ref[:, 2048:3072] = dgate_a.astype(BF16)
        dz_ref[:, 3072:4096] = dgate_b.astype(BF16)
        dp_ab = dp_a.astype(BF16)
        dp_bb = dp_b.astype(BF16)
        dy_a = _dot(dp_ab, wpa[...], NT)
        dy_b = _dot(dp_bb, wpb[...], NT)
        acc_pa[...] += _dot(y_a, dp_ab, TN)
        acc_pb[...] += _dot(y_b, dp_bb, TN)

        doa_ref[...] = (dy_a * silu_a).astype(BF16)
        dz_ref[:, 0:512] = (dy_a * o_a * (sg_a * (1.0 + g_a * (1.0 - sg_a)))).astype(BF16)
        dz_ref[:, 1536:2048] = (dy_b * um * (sg_b * (1.0 + g_b * (1.0 - sg_b)))).astype(BF16)
        dys = dy_b * silu_b
        dz_ref[:, 512:1024] = (dys * mixed * dgelu_u).astype(BF16)
        dmixed = dys * ug
        dmb = dmixed.astype(BF16)
        for n in range(n_sub):
            rows = slice(n * SGU_CHUNK, (n + 1) * SGU_CHUNK)
            for g in range(N_GROUPS):
                cols = slice(g * 128, (g + 1) * 128)
                dws_ref[g] += _dot(dmb[rows, cols], vn[rows, cols], NT)
                dbs_ref[g] += jnp.sum(dmixed[rows, cols], axis=-1, keepdims=True)
                dvn_s[rows, cols] = _dot(wmix[g], dmb[rows, cols], TN)
        dvn = dvn_s[...]
        dlng_ref[...] += jnp.sum(dvn * vhat, axis=0, keepdims=True)
        dlnb_ref[...] += jnp.sum(dvn, axis=0, keepdims=True)
        dvh = dvn * lng
        dvg = rstd * (dvh - jnp.mean(dvh, axis=-1, keepdims=True) - vhat * jnp.mean(dvh * vhat, axis=-1, keepdims=True))
        dz_ref[:, 1024:1536] = (dvg * dgelu_v).astype(BF16)

        @pl.when(i == nt - 1)
        def _():
            t_idx = lax.broadcasted_iota(jnp.int32, (SGU_CHUNK, SGU_CHUNK), 0)
            s_idx = lax.broadcasted_iota(jnp.int32, (SGU_CHUNK, SGU_CHUNK), 1)
            for g in range(N_GROUPS):
                dws_ref[g] = jnp.where(s_idx <= t_idx, dws_ref[g], 0.0)
            stores = [pltpu.make_async_copy(src, dst, sem.at[n])
                      for n, (src, dst) in enumerate(((acc_out, dwout_hbm), (acc_pa, dwpa_hbm), (acc_pb, dwpb_hbm)))]
            for cp in stores:
                cp.start()
            for cp in stores:
                cp.wait()

    tile = lambda w: pl.BlockSpec((tm, w), lambda i: (i, 0))
    whole = lambda shape: pl.BlockSpec(shape, lambda i: (0,) * len(shape))
    hbm = pl.BlockSpec(memory_space=pl.ANY)
    return pl.pallas_call(
        body, name="mid_fwd_bwd",
        grid=(nt,),
        in_specs=[tile(D_MODEL), tile(D_MODEL), tile(D_A)]
        + [pl.BlockSpec((tm, COL_BLOCK), functools.partial(lambda c, i: (i + Z_PAD // tm, c), c))
           for c in range(3, N_COL_BLOCKS)]
        + [hbm, hbm, hbm,
                  whole((1, 2 * D_MODEL)), whole((1, D_B)), whole((1, D_B)),
                  whole((N_GROUPS, SGU_CHUNK, SGU_CHUNK)), whole((N_GROUPS, SGU_CHUNK, 1)), whole((1, D_MODEL))],
        out_specs=(tile(D_MODEL), tile(D_A), tile(REST), hbm, hbm, hbm,
                   whole((1, 2 * D_MODEL)), whole((1, D_MODEL)), whole((1, D_B)), whole((1, D_B)),
                   whole((N_GROUPS, SGU_CHUNK, SGU_CHUNK)), whole((N_GROUPS, SGU_CHUNK, 1)), whole((1, 1))),
        out_shape=(jax.ShapeDtypeStruct((s, D_MODEL), F32), jax.ShapeDtypeStruct((s, D_A), BF16),
                   jax.ShapeDtypeStruct((s, REST), BF16),
                   jax.ShapeDtypeStruct((D_MODEL, D_MODEL), F32), jax.ShapeDtypeStruct((D_A, D_MODEL), F32),
                   jax.ShapeDtypeStruct((D_B, D_MODEL), F32),
                   jax.ShapeDtypeStruct((1, 2 * D_MODEL), F32), jax.ShapeDtypeStruct((1, D_MODEL), F32),
                   jax.ShapeDtypeStruct((1, D_B), F32), jax.ShapeDtypeStruct((1, D_B), F32),
                   jax.ShapeDtypeStruct((N_GROUPS, SGU_CHUNK, SGU_CHUNK), F32),
                   jax.ShapeDtypeStruct((N_GROUPS, SGU_CHUNK, 1), F32), jax.ShapeDtypeStruct((1, 1), F32)),
        scratch_shapes=[pltpu.VMEM((D_A, D_MODEL), BF16), pltpu.VMEM((D_B, D_MODEL), BF16),
                        pltpu.VMEM((D_MODEL, D_MODEL), BF16), pltpu.VMEM((N_GROUPS, SGU_CHUNK, SGU_CHUNK), BF16),
                        pltpu.VMEM((D_MODEL, D_MODEL), F32), pltpu.VMEM((D_A, D_MODEL), F32),
                        pltpu.VMEM((D_B, D_MODEL), F32),
                        pltpu.VMEM((tm, D_B), F32), pltpu.VMEM((tm, D_B), F32),
                        pltpu.SemaphoreType.DMA((3,))],
        compiler_params=_params(56),
    )(x, target, attn_out, *([z] * (N_COL_BLOCKS - 3)), w_pa, w_pb, w_out, b_gate, ln_g, ln_b, w_s, b_s, final_g)


def _proj_bwd_x(dqkv, drest, x, dx2, norm_g, w_in_t, to_chip):
    s = x.shape[0]
    tm = TOKEN_TILE
    nt = s // tm
    n = len(to_chip)

    def body(*refs):
        dqkv_ref, dr_ref, x_ref, dx2_ref, g_ref, w_hbm = refs[:6]
        to_chip_refs = refs[6:6 + n]
        dx_ref, dg_ref = refs[6 + n:8 + n]
        from_chip_refs = refs[8 + n:8 + 2 * n]
        w, sem, send_sems, recv_sems = refs[8 + 2 * n:]
        i = pl.program_id(0)

        @pl.when(i == 0)
        def _():
            for rc in _owner_copies(to_chip_refs, from_chip_refs, send_sems, recv_sems):
                rc.start()
            cp = pltpu.make_async_copy(w_hbm, w, sem)
            cp.start()
            dg_ref[...] = jnp.zeros(dg_ref.shape, F32)
            cp.wait()

        dh = None
        for c in range(N_COL_BLOCKS):
            dz = dqkv_ref[c] if c < 3 else dr_ref[:, (c - 3) * COL_BLOCK:(c - 2) * COL_BLOCK]
            part = _dot(dz, w[c * COL_BLOCK:(c + 1) * COL_BLOCK, :])
            dh = part if dh is None else dh + part
        xf = x_ref[...]
        r = lax.rsqrt(jnp.mean(xf * xf, axis=-1, keepdims=True) + EPS)
        xn = xf * r
        dg_ref[...] += jnp.sum(dh * xn, axis=0, keepdims=True)
        gh = dh * g_ref[...]
        dx_ref[...] = r * (gh - xn * jnp.mean(gh * xn, axis=-1, keepdims=True)) + dx2_ref[...]

        @pl.when(i == nt - 1)
        def _():
            for rc in _owner_copies(to_chip_refs, from_chip_refs, send_sems, recv_sems):
                rc.wait_recv()
                rc.wait_send()

    hbm = pl.BlockSpec(memory_space=pl.ANY)
    return pl.pallas_call(
        body, name="proj_bwd_x",
        grid=(nt,),
        in_specs=[pl.BlockSpec((3, tm, D_A), lambda i: (0, i, 0)),
                  pl.BlockSpec((tm, REST), lambda i: (i, 0)),
                  pl.BlockSpec((tm, D_MODEL), lambda i: (i, 0)),
                  pl.BlockSpec((tm, D_MODEL), lambda i: (i, 0)),
                  pl.BlockSpec((1, D_MODEL), lambda i: (0, 0)),
                  hbm] + [hbm] * n,
        out_specs=(pl.BlockSpec((tm, D_MODEL), lambda i: (i, 0)),
                   pl.BlockSpec((1, D_MODEL), lambda i: (0, 0))) + (hbm,) * n,
        out_shape=(jax.ShapeDtypeStruct((s, D_MODEL), F32), jax.ShapeDtypeStruct((1, D_MODEL), F32))
        + tuple(jax.ShapeDtypeStruct(t.shape, t.dtype) for t in to_chip),
        scratch_shapes=[pltpu.VMEM((D_IN, D_MODEL), BF16), pltpu.SemaphoreType.DMA,
                        pltpu.SemaphoreType.DMA((n, 3)), pltpu.SemaphoreType.DMA((n, 3))],
        compiler_params=_params(48),
    )(dqkv, drest, x, dx2, norm_g, w_in_t, *to_chip)


def _proj_bwd_w(h, dqkv, drest):
    s = h.shape[0]
    tk = min(s, 1024)
    nk = s // tk

    def body(h_ref, dqkv_ref, dr_ref, o_ref, acc):
        j = pl.program_id(0)
        i = pl.program_id(1)

        @pl.when(i == 0)
        def _():
            acc[...] = jnp.zeros(acc.shape, F32)

        @pl.when(j < 3)
        def _():
            acc[...] += _dot(dqkv_ref[...], h_ref[...], TN)

        @pl.when(j >= 3)
        def _():
            acc[...] += _dot(dr_ref[...], h_ref[...], TN)

        @pl.when(i == nk - 1)
        def _():
            o_ref[...] = acc[...].astype(BF16)

    return pl.pallas_call(
        body, name="proj_bwd_w",
        grid=(N_COL_BLOCKS, nk),
        in_specs=[pl.BlockSpec((tk, D_MODEL), lambda j, i: (i, 0)),
                  pl.BlockSpec((None, tk, COL_BLOCK),
                               lambda j, i: (jnp.minimum(j, 2), jnp.where(j < 3, i, nk - 1), 0)),
                  pl.BlockSpec((tk, COL_BLOCK),
                               lambda j, i: (jnp.where(j >= 3, i, 0), jnp.maximum(j - 3, 0)))],
        out_specs=pl.BlockSpec((COL_BLOCK, D_MODEL), lambda j, i: (j, 0)),
        out_shape=jax.ShapeDtypeStruct((D_IN, D_MODEL), BF16),
        scratch_shapes=[pltpu.VMEM((COL_BLOCK, D_MODEL), F32)],
        compiler_params=_params(40),
    )(h, dqkv, drest)


def _adamw_math(w, g, m, v):
    c1 = 1.0 - ADAM_B1 ** ADAM_STEP
    c2 = 1.0 - ADAM_B2 ** ADAM_STEP
    nm = ADAM_B1 * m + (1.0 - ADAM_B1) * g
    nv = ADAM_B2 * v + (1.0 - ADAM_B2) * (g * g)
    return -ADAM_LR * ((nm / c1) / (jnp.sqrt(nv / c2) + ADAM_EPS) + ADAM_WD * w), nm, nv


def _adamw(name, w, g, m, v, from_chip):
    rows, cols = w.shape
    tr = rows if rows * cols <= 512 * 1024 else next(t for t in range(256, 7, -8) if rows % t == 0)

    def body(w_ref, g_ref, m_ref, v_ref, t_ref, g_out, d_ref, nm_ref, nv_ref):
        gg = g_ref[...]
        for j in range(3):
            gg = gg + t_ref[j].astype(F32)
        g_out[...] = gg
        d_ref[...], nm_ref[...], nv_ref[...] = _adamw_math(w_ref[...], gg, m_ref[...], v_ref[...])

    spec = pl.BlockSpec((tr, cols), lambda i: (i, 0))
    shape = jax.ShapeDtypeStruct((rows, cols), F32)
    return pl.pallas_call(
        body, name=name,
        grid=(rows // tr,),
        in_specs=[spec] * 4 + [pl.BlockSpec((3, tr, cols), lambda i: (0, i, 0))],
        out_specs=(spec,) * 4, out_shape=(shape,) * 4,
        compiler_params=_params(32),
    )(w, g, m, v, from_chip)


_SMALL = (("norm_g", (1, D_MODEL)), ("b_gate", (1, 2 * D_MODEL)), ("rel_bias", (N_HEADS, N_REL)),
          ("sgu_ln_g", (1, D_B)), ("sgu_ln_b", (1, D_B)), ("w_s", (N_GROUPS * SGU_CHUNK, SGU_CHUNK)),
          ("b_s", (N_GROUPS, SGU_CHUNK)), ("final_g", (1, D_MODEL)))


def _adamw_small(slab, g_ws, weights, moments_m, moments_v):
    k = len(_SMALL)

    def grad_of(name, slab_ref, ws_ref):
        if name == "norm_g":
            return slab_ref[ROW_NORM_G:ROW_NORM_G + 1, :]
        if name == "b_gate":
            return jnp.concatenate([slab_ref[ROW_B_GATE:ROW_B_GATE + 1, :], slab_ref[ROW_B_GATE + 1:ROW_B_GATE + 2, :]],
                                   axis=1)
        if name == "rel_bias":
            return slab_ref[ROW_REL:ROW_REL + N_HEADS, :N_REL]
        if name == "sgu_ln_g":
            return slab_ref[ROW_LN_G:ROW_LN_G + 1, :D_B]
        if name == "sgu_ln_b":
            return slab_ref[ROW_LN_B:ROW_LN_B + 1, :D_B]
        if name == "w_s":
            return ws_ref[...]
        if name == "b_s":
            return slab_ref[ROW_B_S:ROW_B_S + N_GROUPS, :SGU_CHUNK]
        return slab_ref[ROW_FINAL_G:ROW_FINAL_G + 1, :]

    def body(*refs):
        slab_ref, ws_ref = refs[:2]
        w_refs, m_refs, v_refs = refs[2:2 + k], refs[2 + k:2 + 2 * k], refs[2 + 2 * k:2 + 3 * k]
        outs = refs[2 + 3 * k:]
        for n, (name, _) in enumerate(_SMALL):
            g = grad_of(name, slab_ref, ws_ref)
            outs[n][...] = g
            outs[k + n][...], outs[2 * k + n][...], outs[3 * k + n][...] = _adamw_math(
                w_refs[n][...], g, m_refs[n][...], v_refs[n][...])
        outs[4 * k][...] = slab_ref[ROW_LOSS:ROW_LOSS + 1, :1]

    vmem = pl.BlockSpec(memory_space=pltpu.VMEM)
    shapes = tuple(jax.ShapeDtypeStruct(shape, F32) for _, shape in _SMALL)
    return pl.pallas_call(
        body, name="adamw_small",
        out_shape=shapes * 4 + (jax.ShapeDtypeStruct((1, 1), F32),),
        in_specs=[vmem] * (2 + 3 * k), out_specs=tuple([vmem] * (4 * k + 1)),
        compiler_params=_params(16),
    )(slab, g_ws, *weights, *moments_m, *moments_v)


def _pad_rel(a):
    return jnp.pad(a.reshape(N_HEADS, N_REL), ((0, 0), (0, N_REL_PAD - N_REL)))


def kernel(x, norm_g, w_in, b_gate, rel_bias, sgu_ln_g, sgu_ln_b, w_s, b_s, w_pa, w_pb, w_out, final_g, loss_target, m_norm_g, m_w_in, m_b_gate, m_rel_bias, m_sgu_ln_g, m_sgu_ln_b, m_w_s, m_b_s, m_w_pa, m_w_pb, m_w_out, m_final_g, v_norm_g, v_w_in, v_b_gate, v_rel_bias, v_sgu_ln_g, v_sgu_ln_b, v_w_s, v_b_s, v_w_pa, v_w_pb, v_w_out, v_final_g):
    s = x.shape[1]
    xs = x.reshape(s, D_MODEL)
    tgt = loss_target.reshape(s, D_MODEL)

    bias_table = _bias_table(_pad_rel(rel_bias))
    w_in_t = jnp.swapaxes(w_in[0], 0, 1)
    qkv, h, w_in_t_full = _gather_proj_fwd(xs, norm_g, w_in_t)
    attn_out, g_pa, g_pb, g_out = _attn_fwd(qkv, bias_table, (w_pa[0], w_pb[0], w_out[0]))
    w_pa_full = jnp.transpose(g_pa, (1, 0, 2)).reshape(D_A, D_MODEL)
    w_pb_full = jnp.transpose(g_pb, (1, 0, 2)).reshape(D_B, D_MODEL)
    w_out_full = g_out.reshape(D_MODEL, D_MODEL)

    (dx2, d_attn, drest, dw_out, dw_pa, dw_pb, d_bgate, d_fg, d_lng, d_lnb, d_ws, d_bs, loss_part) = _mid_fwd_bwd(
        xs, tgt, attn_out, qkv, w_pa_full, w_pb_full, w_out_full, b_gate, sgu_ln_g, sgu_ln_b, w_s[0],
        b_s.reshape(N_GROUPS, SGU_CHUNK, 1), final_g.reshape(1, D_MODEL))

    own_pa, own_pb, own_out, tc_pa, tc_pb, tc_out = _reduce_chip(
        "reduce_chip_proj", (dw_pa, dw_pb, dw_out), (1, 1, 0))
    dqkv, dbias, fc_pa, fc_pb, fc_out = _attn_bwd(qkv, bias_table, d_attn, attn_out, (tc_pa, tc_pb, tc_out))
    d_rel = _bias_grad(dbias)
    dw_in_t = _proj_bwd_w(h, dqkv, drest)
    own_in, tc_in = _reduce_chip("reduce_chip_in", (dw_in_t,), (0,))
    grad_x, d_ng, fc_in = _proj_bwd_x(dqkv, drest, xs, dx2, norm_g, w_in_t_full, (tc_in,))
    big = {"w_in": tuple(jnp.swapaxes(t, 0, 1)[None] for t in _adamw(
        "adamw_w_in", w_in_t, own_in, jnp.swapaxes(m_w_in[0], 0, 1), jnp.swapaxes(v_w_in[0], 0, 1), fc_in))}
    for name, w, g, fc, m, v in (("w_pa", w_pa, own_pa, fc_pa, m_w_pa, v_w_pa),
                                 ("w_pb", w_pb, own_pb, fc_pb, m_w_pb, v_w_pb),
                                 ("w_out", w_out, own_out, fc_out, m_w_out, v_w_out)):
        big[name] = tuple(t[None] for t in _adamw("adamw_" + name, w[0], g, m[0], v[0], fc))

    slab, g_ws = _reduce_small(d_ng, d_bgate, d_rel, d_lng, d_lnb, d_fg, loss_part, d_bs, d_ws)
    as_2d = lambda leaves: [a.reshape(shape) for a, (_, shape) in zip(leaves, _SMALL)]
    small_out = _adamw_small(
        slab, g_ws, as_2d((norm_g, b_gate, rel_bias, sgu_ln_g, sgu_ln_b, w_s, b_s, final_g)),
        as_2d((m_norm_g, m_b_gate, m_rel_bias, m_sgu_ln_g, m_sgu_ln_b, m_w_s, m_b_s, m_final_g)),
        as_2d((v_norm_g, v_b_gate, v_rel_bias, v_sgu_ln_g, v_sgu_ln_b, v_w_s, v_b_s, v_final_g)))
    small_index = {name: n for n, (name, _) in enumerate(_SMALL)}

    def leaf(kind, name, like):
        if name in big:
            return big[name][kind]
        return small_out[kind * len(_SMALL) + small_index[name]].reshape(like.shape)

    weights = (("norm_g", norm_g), ("w_in", w_in), ("b_gate", b_gate), ("rel_bias", rel_bias), ("sgu_ln_g", sgu_ln_g),
               ("sgu_ln_b", sgu_ln_b), ("w_s", w_s), ("b_s", b_s), ("w_pa", w_pa), ("w_pb", w_pb), ("w_out", w_out),
               ("final_g", final_g))
    outs = [small_out[-1].reshape(()), grad_x.reshape(x.shape)]
    for kind in range(4):
        outs.extend(leaf(kind, name, like) for name, like in weights)
    return tuple(outs)
```

```python
import functools
import math

import jax
import jax.numpy as jnp
from jax import lax
from jax.experimental import pallas as pl
from jax.experimental.pallas import tpu as pltpu

F32 = jnp.float32
BF16 = jnp.bfloat16
MESH = pl.DeviceIdType.MESH
N_DEV = 8

D_MODEL = 1024
D_A = 512
D_B = 512
D_IN = 5632
N_HEADS = 8
HEAD_DIM = 64
CHUNK = 64
N_PREV = 8
REL_CLIP = 128
N_REL = 2 * REL_CLIP + 1
N_REL_PAD = 384
SGU_CHUNK = 128
N_GROUPS = 4
EPS = 1e-6
NEG_INF = -1e30
Q_SCALE = HEAD_DIM ** -0.5

Q_BLOCK = 256
K_SPAN = 768
Z_PAD = K_SPAN - Q_BLOCK
ROLL_W = 1024
COL_BLOCK = 512
N_COL_BLOCKS = D_IN // COL_BLOCK
REST = D_IN - 3 * D_A
TOKEN_TILE = 256
V7X_VMEM_BYTES = 64 * 1024 * 1024

ADAM_LR = 0.001
ADAM_B1 = 0.9
ADAM_B2 = 0.999
ADAM_EPS = 1e-08
ADAM_WD = 0.01
ADAM_STEP = 10

GELU_C = math.sqrt(2.0 / math.pi)
GELU_A = 0.044715

NT = (((1,), (1,)), ((), ()))
TN = (((0,), (0,)), ((), ()))
HIGHEST = lax.Precision.HIGHEST


def _params(vmem_mb, **kw):
    return pltpu.CompilerParams(vmem_limit_bytes=vmem_mb * 1024 * 1024, **kw)


def _dot(a, b, dims=None):
    if dims is None:
        return jnp.dot(a, b, preferred_element_type=F32)
    return lax.dot_general(a, b, dims, preferred_element_type=F32)


def _sigmoid(x):
    return 1.0 / (1.0 + jnp.exp(-x))


def _gelu_and_grad(u):
    u2 = u * u
    t = jnp.tanh(GELU_C * (u + GELU_A * u * u2))
    half = 0.5 * (1.0 + t)
    g = u * half
    dg = half + 0.5 * u * (1.0 - t * t) * (GELU_C * (1.0 + 3.0 * GELU_A * u2))
    return g, dg


def _my_pos():
    return lax.axis_index("x"), lax.axis_index("y"), lax.axis_index("c")


def _flat_id(pos):
    return 4 * pos[0] + 2 * pos[1] + pos[2]


def _peer(pos, k):
    x, y, c = pos
    return (1 - x if k & 4 else x, 1 - y if k & 2 else y, 1 - c if k & 1 else c)


def _other_chips(pos):
    x, y, _ = pos
    return ((1 - x, y), (x, 1 - y), (1 - x, 1 - y))


class _SlotGather:
    def __init__(self, bufs, send_sems, recv_sems, own=None):
        self.bufs, self.send_sems, self.recv_sems = bufs, send_sems, recv_sems
        self.own = own if own is not None else [None] * len(bufs)
        x, y, c = _my_pos()
        self.c, self.me, self.sib = c, (x, y, c), (x, y, 1 - c)
        self.chips = _other_chips(self.me)

    def _copy(self, a, k, block, to):
        slot = _flat_id(block)
        src = self.own[a] if (k < 4 and self.own[a] is not None) else self.bufs[a].at[slot]
        return pltpu.make_async_remote_copy(
            src_ref=src, dst_ref=self.bufs[a].at[slot],
            send_sem=self.send_sems.at[a, k], recv_sem=self.recv_sems.at[a, k], device_id=to, device_id_type=MESH)

    def _own_sends(self):
        n = len(self.bufs)
        return ([self._copy(a, 1 + j, self.me, (*chip, self.c)) for j, chip in enumerate(self.chips) for a in range(n)]
                + [self._copy(a, 0, self.me, self.sib) for a in range(n)])

    def _passes(self):
        return [self._copy(a, 4 + j, (*chip, self.c), self.sib)
                for j, chip in enumerate(self.chips) for a in range(len(self.bufs))]

    def start(self):
        for cp in self._own_sends():
            cp.start()

    def pass_on(self):
        for j, chip in enumerate(self.chips):
            for a in range(len(self.bufs)):
                self._copy(a, 1 + j, (*chip, self.c), self.me).wait_recv()
                self._copy(a, 4 + j, (*chip, self.c), self.sib).start()

    def finish(self):
        for a in range(len(self.bufs)):
            self._copy(a, 0, self.sib, self.me).wait_recv()
            for j, chip in enumerate(self.chips):
                self._copy(a, 4 + j, (*chip, 1 - self.c), self.me).wait_recv()
        for cp in self._own_sends() + self._passes():
            cp.wait_send()


def _reduce_chip(name, parts, sharded_dim):
    n = len(parts)
    shapes = []
    for p, dim in zip(parts, sharded_dim):
        shape = list(p.shape)
        shape[dim] //= N_DEV
        shapes.append(tuple(shape))

    def body(*refs):
        full, own, to_chip = refs[:n], refs[n:2 * n], refs[2 * n:3 * n]
        ins, from_sib = refs[3 * n:4 * n], refs[4 * n:5 * n]
        send_sems, recv_sems = refs[5 * n], refs[5 * n + 1]
        x, y, c = _my_pos()
        sib = (x, y, 1 - c)
        chips = ((x, y),) + _other_chips((x, y, c))
        for a in range(n):
            rows, cols = shapes[a]
            for d in range(N_DEV):
                if sharded_dim[a] == 0:
                    ins[a][d] = full[a][d * rows:(d + 1) * rows, :].astype(BF16)
                else:
                    ins[a][d] = full[a][:, d * cols:(d + 1) * cols].astype(BF16)

        def to_sibling(a, r):
            return pltpu.make_async_remote_copy(
                src_ref=ins[a].at[_flat_id((*chips[r], 1 - c))], dst_ref=from_sib[a].at[r],
                send_sem=send_sems.at[a, r], recv_sem=recv_sems.at[a, r], device_id=sib, device_id_type=MESH)

        sends = [to_sibling(a, r) for r in (1, 2, 3, 0) for a in range(n)]
        for cp in sends:
            cp.start()
        for r in (1, 2, 3, 0):
            for a in range(n):
                to_sibling(a, r).wait_recv()
                both = ins[a][_flat_id((*chips[r], c))].astype(F32) + from_sib[a][r].astype(F32)
                if r == 0:
                    own[a][...] = both
                else:
                    to_chip[a][r - 1] = both.astype(BF16)
        for cp in sends:
            cp.wait_send()

    vmem = pl.BlockSpec(memory_space=pltpu.VMEM)
    return pl.pallas_call(
        body, name=name,
        out_shape=tuple(jax.ShapeDtypeStruct(sh, F32) for sh in shapes)
        + tuple(jax.ShapeDtypeStruct((3,) + sh, BF16) for sh in shapes),
        in_specs=[vmem] * n, out_specs=tuple([vmem] * (2 * n)),
        scratch_shapes=[pltpu.VMEM((N_DEV,) + sh, BF16) for sh in shapes]
        + [pltpu.VMEM((4,) + sh, BF16) for sh in shapes]
        + [pltpu.SemaphoreType.DMA((n, 4)), pltpu.SemaphoreType.DMA((n, 4))],
        compiler_params=_params(56),
    )(*parts)


def _owner_copies(to_chip, from_chip, send_sems, recv_sems):
    x, y, c = _my_pos()
    return [pltpu.make_async_remote_copy(
        src_ref=to_chip[a].at[j], dst_ref=from_chip[a].at[j],
        send_sem=send_sems.at[a, j], recv_sem=recv_sems.at[a, j], device_id=(*chip, c), device_id_type=MESH)
        for a in range(len(to_chip)) for j, chip in enumerate(_other_chips((x, y, c)))]


ROW_NORM_G, ROW_B_GATE, ROW_LN_G, ROW_LN_B, ROW_FINAL_G, ROW_LOSS, ROW_REL, ROW_B_S, SLAB_ROWS = 0, 1, 3, 4, 5, 6, 8, 16, 24


def _reduce_small(d_ng, d_bgate, d_rel, d_lng, d_lnb, d_fg, loss, d_bs, d_ws):
    def body(ng_ref, bg_ref, rel_ref, lng_ref, lnb_ref, fg_ref, loss_ref, bs_ref, ws_ref, slab_out, ws_out,
             slab_land, ws_land, send_sems, recv_sems):
        me = _flat_id(_my_pos())
        slab_land[me] = jnp.zeros((SLAB_ROWS, D_MODEL), F32)
        slab_land[me, ROW_NORM_G:ROW_NORM_G + 1, :] = ng_ref[...]
        slab_land[me, ROW_B_GATE:ROW_B_GATE + 1, :] = bg_ref[:, :D_MODEL]
        slab_land[me, ROW_B_GATE + 1:ROW_B_GATE + 2, :] = bg_ref[:, D_MODEL:]
        slab_land[me, ROW_LN_G:ROW_LN_G + 1, :D_B] = lng_ref[...]
        slab_land[me, ROW_LN_B:ROW_LN_B + 1, :D_B] = lnb_ref[...]
        slab_land[me, ROW_FINAL_G:ROW_FINAL_G + 1, :] = fg_ref[...]
        slab_land[me, ROW_LOSS:ROW_LOSS + 1, :1] = loss_ref[...]
        slab_land[me, ROW_REL:ROW_REL + N_HEADS, :N_REL_PAD] = rel_ref[...]
        eye = (lax.broadcasted_iota(jnp.int32, (SGU_CHUNK, SGU_CHUNK), 0)
               == lax.broadcasted_iota(jnp.int32, (SGU_CHUNK, SGU_CHUNK), 1))
        for g in range(N_GROUPS):
            row = jnp.sum(jnp.where(eye, bs_ref[g], 0.0), axis=0, keepdims=True)
            slab_land[me, ROW_B_S + g:ROW_B_S + g + 1, :SGU_CHUNK] = row
        ws_land[me] = ws_ref[...]
        gather = _SlotGather([slab_land, ws_land], send_sems, recv_sems)
        gather.start()
        gather.pass_on()
        gather.finish()
        for land, out in ((slab_land, slab_out), (ws_land, ws_out)):
            acc = land[0]
            for d in range(1, N_DEV):
                acc = acc + land[d]
            out[...] = acc

    vmem = pl.BlockSpec(memory_space=pltpu.VMEM)
    ws_shape = (N_GROUPS * SGU_CHUNK, SGU_CHUNK)
    return pl.pallas_call(
        body, name="reduce_small",
        out_shape=(jax.ShapeDtypeStruct((SLAB_ROWS, D_MODEL), F32), jax.ShapeDtypeStruct(ws_shape, F32)),
        in_specs=[vmem] * 9, out_specs=(vmem, vmem),
        scratch_shapes=[pltpu.VMEM((N_DEV, SLAB_ROWS, D_MODEL), F32), pltpu.VMEM((N_DEV,) + ws_shape, F32),
                        pltpu.SemaphoreType.DMA((2, N_DEV - 1)), pltpu.SemaphoreType.DMA((2, N_DEV - 1))],
        compiler_params=_params(16),
    )(d_ng, d_bgate, d_rel, d_lng, d_lnb, d_fg, loss, d_bs, d_ws.reshape(ws_shape))


def _rel_index(e):
    lo, hi = Z_PAD - REL_CLIP, Z_PAD + REL_CLIP
    return jnp.where(e <= lo, 2 * REL_CLIP, jnp.where(e < hi, hi - e, jnp.where(e <= K_SPAN, 0, 2 * REL_CLIP)))


def _bias_table(rel_bias_pad):
    def body(rb_ref, bt_ref):
        c = lax.broadcasted_iota(jnp.int32, (N_REL_PAD, ROLL_W), 1)
        r = lax.broadcasted_iota(jnp.int32, (N_REL_PAD, ROLL_W), 0)
        pick = (r == _rel_index(c)).astype(F32)
        rows = jnp.dot(rb_ref[...], pick, precision=HIGHEST, preferred_element_type=F32)
        qc = lax.broadcasted_iota(jnp.int32, (Q_BLOCK, K_SPAN), 0) >> 6
        kc = lax.broadcasted_iota(jnp.int32, (Q_BLOCK, K_SPAN), 1) >> 6
        band = (kc >= qc) & (kc <= qc + N_PREV)
        for h in range(N_HEADS):
            t = jnp.broadcast_to(rows[h:h + 1, :], (Q_BLOCK, ROLL_W))
            t = pltpu.roll(t, 0, 1, stride=1, stride_axis=0)
            bt_ref[h] = jnp.where(band, t[:, :K_SPAN], NEG_INF)

    return pl.pallas_call(
        body, name="bias_table",
        out_shape=jax.ShapeDtypeStruct((N_HEADS, Q_BLOCK, K_SPAN), F32),
        compiler_params=_params(32),
    )(rel_bias_pad)


def _bias_grad(dbias):
    def body(a_ref, o_ref):
        rr = lax.broadcasted_iota(jnp.int32, (Q_BLOCK, Q_BLOCK), 0)
        cc = lax.broadcasted_iota(jnp.int32, (Q_BLOCK, Q_BLOCK), 1)
        flip = (rr + cc == Q_BLOCK - 1).astype(F32)
        c = lax.broadcasted_iota(jnp.int32, (ROLL_W, N_REL_PAD), 0)
        r = lax.broadcasted_iota(jnp.int32, (ROLL_W, N_REL_PAD), 1)
        e = jnp.where(c >= Q_BLOCK - 1, c - (Q_BLOCK - 1), c + (ROLL_W - Q_BLOCK + 1))
        pick = (r == _rel_index(e)).astype(F32)
        sums = []
        for h in range(N_HEADS):
            a = jnp.dot(flip, a_ref[h], precision=HIGHEST, preferred_element_type=F32)
            a = jnp.concatenate([a, jnp.zeros((Q_BLOCK, ROLL_W - K_SPAN), F32)], axis=1)
            a = pltpu.roll(a, 0, 1, stride=1, stride_axis=0)
            sums.append(jnp.sum(a, axis=0, keepdims=True))
        diag = jnp.concatenate(sums, axis=0)
        o_ref[...] = jnp.dot(diag, pick, precision=HIGHEST, preferred_element_type=F32)

    return pl.pallas_call(
        body, name="bias_grad",
        out_shape=jax.ShapeDtypeStruct((N_HEADS, N_REL_PAD), F32),
        compiler_params=_params(32),
    )(dbias)


def _gather_proj_fwd(x, norm_g, w_in_t):
    s = x.shape[0]
    tm = 512 if s % 512 == 0 else TOKEN_TILE
    nt = s // tm
    n_pad = Z_PAD // tm
    shard_w = w_in_t.shape[0]
    chip_w = 2 * shard_w
    n_chips = N_DEV // 2

    def body(order_ref, x_ref, g_ref, win_hbm, z_ref, h_ref, wt_hbm, stage, wchip, hb, win_f32, send_sems, recv_sems,
             local_sems):
        j = pl.program_id(0)
        i = pl.program_id(1)
        x_, y_, c_ = _my_pos()
        me, sib = (x_, y_, c_), (x_, y_, 1 - c_)
        near = _other_chips(me)
        pick = lambda a, b: tuple(jnp.where(c_ == 0, u, v) for u, v in zip(a, b))
        passed_from, passed_to = pick(near[0], near[1]), pick(near[1], near[0])

        def rows_of(block):
            return wt_hbm.at[pl.ds(pl.multiple_of(_flat_id(block) * shard_w, 16), shard_w), :]

        def copy(k, block, to, own=False):
            return pltpu.make_async_remote_copy(
                src_ref=stage if own else rows_of(block), dst_ref=rows_of(block),
                send_sem=send_sems.at[k], recv_sem=recv_sems.at[k], device_id=to, device_id_type=MESH)

        def sends():
            return ([copy(0, me, sib, True), copy(1, me, (*near[0], c_), True), copy(2, me, (*near[1], c_), True),
                     copy(3, (*passed_from, c_), (*passed_to, c_))]
                    + [copy(4 + n, (*near[n], c_), sib) for n in range(3)])

        keep = pltpu.make_async_copy(stage, rows_of(me), local_sems.at[0])

        def fetch(chip):
            first = pl.multiple_of((2 * chip[0] + chip[1]) * chip_w, 16)
            cp = pltpu.make_async_copy(wt_hbm.at[pl.ds(first, chip_w), :], wchip, local_sems.at[1])
            cp.start()
            cp.wait()

        @pl.when((j == 0) & (i == 0))
        def _():
            load = pltpu.make_async_copy(win_hbm, win_f32, local_sems.at[1])
            load.start()
            load.wait()
            stage[...] = win_f32[...].astype(BF16)
            keep.start()
            for cp in sends()[:3]:
                cp.start()
            copy(0, sib, me).wait_recv()
            keep.wait()
            fetch((x_, y_))

        @pl.when((j == 1) & (i == 0))
        def _():
            copy(1, (*near[0], c_), me).wait_recv()
            copy(2, (*near[1], c_), me).wait_recv()
            for cp in sends()[3:6]:
                cp.start()
            copy(4, (*near[0], 1 - c_), me).wait_recv()
            fetch(near[0])

        @pl.when((j == 2) & (i == 0))
        def _():
            copy(5, (*near[1], 1 - c_), me).wait_recv()
            fetch(near[1])

        @pl.when((j == 3) & (i == 0))
        def _():
            copy(3, (*near[2], c_), me).wait_recv()
            copy(6, (*near[2], c_), sib).start()
            copy(6, (*near[2], 1 - c_), me).wait_recv()
            fetch(near[2])

        @pl.when(i < n_pad)
        def _():
            z_ref[...] = jnp.zeros(z_ref.shape, BF16)

        @pl.when(i >= n_pad)
        def _():
            rows = pl.ds(pl.multiple_of((i - n_pad) * tm, tm), tm)

            @pl.when(j == 0)
            def _():
                xf = x_ref[...]
                r = lax.rsqrt(jnp.mean(xf * xf, axis=-1, keepdims=True) + EPS)
                hf = (xf * r * g_ref[...]).astype(BF16)
                hb[rows, :] = hf
                h_ref[...] = hf

            blk = _dot(hb[rows, :], wchip[...], NT)
            q_scale = jnp.where(order_ref[j] == 0, Q_SCALE, 1.0).astype(F32)
            z_ref[:, :D_A] = (blk[:, :D_A] * q_scale).astype(BF16)
            z_ref[:, D_A:] = blk[:, D_A:].astype(BF16)

        @pl.when((j == n_chips - 1) & (i == n_pad + nt - 1))
        def _():
            for cp in sends():
                cp.wait_send()

    pos = _my_pos()
    order = jnp.stack([2 * cx + cy for cx, cy in ((pos[0], pos[1]),) + _other_chips(pos)]).astype(jnp.int32)
    first_pass = lambda j, i: jnp.where(j == 0, jnp.maximum(i - n_pad, 0), nt - 1)
    grid_spec = pltpu.PrefetchScalarGridSpec(
        num_scalar_prefetch=1,
        grid=(n_chips, n_pad + nt),
        in_specs=[pl.BlockSpec((tm, D_MODEL), lambda j, i, o: (first_pass(j, i), 0)),
                  pl.BlockSpec((1, D_MODEL), lambda j, i, o: (0, 0)),
                  pl.BlockSpec(memory_space=pl.ANY)],
        out_specs=(pl.BlockSpec((tm, chip_w), lambda j, i, o: (i, o[j])),
                   pl.BlockSpec((tm, D_MODEL), lambda j, i, o: (first_pass(j, i), 0)),
                   pl.BlockSpec(memory_space=pl.ANY)),
        scratch_shapes=[pltpu.VMEM((shard_w, D_MODEL), BF16), pltpu.VMEM((chip_w, D_MODEL), BF16),
                        pltpu.VMEM((s, D_MODEL), BF16), pltpu.VMEM(w_in_t.shape, F32),
                        pltpu.SemaphoreType.DMA((N_DEV - 1,)), pltpu.SemaphoreType.DMA((N_DEV - 1,)),
                        pltpu.SemaphoreType.DMA((2,))])
    return pl.pallas_call(
        body, name="gather_proj_fwd",
        grid_spec=grid_spec,
        out_shape=(jax.ShapeDtypeStruct((Z_PAD + s, D_IN), BF16), jax.ShapeDtypeStruct((s, D_MODEL), BF16),
                   jax.ShapeDtypeStruct((D_IN, D_MODEL), BF16)),
        compiler_params=_params(60),
    )(order, x, norm_g, w_in_t)


def _attn_specs(rows):
    pairs = N_HEADS // 2
    return ([pl.BlockSpec((rows, 128), functools.partial(lambda which, p: (0, which * pairs + p), which))
             for which in range(3)]
            + [pl.BlockSpec((2, Q_BLOCK, K_SPAN), lambda p: (p, 0, 0))])


def _head_masks():
    lane = lax.broadcasted_iota(jnp.int32, (1, 128), 1)
    first = lane < HEAD_DIM
    return (first, jnp.logical_not(first))


def _stack_heads(x, masks):
    zero = jnp.zeros((), x.dtype)
    return jnp.concatenate([jnp.where(m, x, zero) for m in masks], axis=0)


def _side_by_side(x):
    rows = x.shape[0] // 2
    return jnp.concatenate([x[:rows], x[rows:]], axis=1)


def _softmax_rows(q2, kcat, bias, b):
    s = _dot(q2, kcat, NT) + bias
    valid = lax.broadcasted_iota(jnp.int32, (1, K_SPAN), 1) >= Z_PAD - b * Q_BLOCK
    s = jnp.where(valid, s, NEG_INF)
    e = jnp.exp(s - jnp.max(s, axis=-1, keepdims=True))
    return e * (1.0 / jnp.sum(e, axis=-1, keepdims=True))


def _attn_fwd(qkv, bias_table, shards):
    s = qkv.shape[0] - Z_PAD
    nb = s // Q_BLOCK
    n = len(shards)
    pairs = N_HEADS // 2

    def body(*refs):
        q_ref, k_ref, v_ref, bt_ref = refs[:4]
        shard_refs = refs[4:4 + n]
        o_ref = refs[4 + n]
        slot_refs = refs[5 + n:5 + 2 * n]
        stages = refs[5 + 2 * n:5 + 3 * n]
        send_sems, recv_sems, local_sems = refs[5 + 3 * n:]
        p_id = pl.program_id(0)
        gather = _SlotGather(slot_refs, send_sems, recv_sems, own=stages)
        keep = [pltpu.make_async_copy(stages[a], slot_refs[a].at[_flat_id(_my_pos())], local_sems.at[a])
                for a in range(n)]

        @pl.when(p_id == 0)
        def _():
            for a in range(n):
                stages[a][...] = shard_refs[a][...].astype(BF16)
                keep[a].start()
            gather.start()

        @pl.when(p_id == 1)
        def _():
            gather.pass_on()

        masks = _head_masks()

        def block(b, carry):
            r0 = pl.multiple_of(b * Q_BLOCK, Q_BLOCK)
            q2 = _stack_heads(q_ref[pl.ds(r0 + Z_PAD, Q_BLOCK), :], masks)
            kcat = k_ref[pl.ds(r0, K_SPAN), :]
            v2 = _stack_heads(v_ref[pl.ds(r0, K_SPAN), :], masks)
            p = _softmax_rows(q2, kcat, bt_ref[...].reshape(2 * Q_BLOCK, K_SPAN), b)
            o_ref[pl.ds(r0, Q_BLOCK), :] = _dot(_side_by_side(p.astype(BF16)), v2)
            return carry

        lax.fori_loop(0, nb, block, 0, unroll=2)

        @pl.when(p_id == pairs - 1)
        def _():
            gather.finish()
            for cp in keep:
                cp.wait()

    hbm = pl.BlockSpec(memory_space=pl.ANY)
    return pl.pallas_call(
        body, name="attn_fwd",
        grid=(pairs,),
        in_specs=_attn_specs(s + Z_PAD) + [pl.BlockSpec(a.shape, lambda p: (0, 0)) for a in shards],
        out_specs=(pl.BlockSpec((s, 128), lambda p: (0, p)),) + (hbm,) * n,
        out_shape=(jax.ShapeDtypeStruct((s, D_A), F32),)
        + tuple(jax.ShapeDtypeStruct((N_DEV,) + a.shape, BF16) for a in shards),
        scratch_shapes=[pltpu.VMEM(a.shape, BF16) for a in shards]
        + [pltpu.SemaphoreType.DMA((n, N_DEV - 1)), pltpu.SemaphoreType.DMA((n, N_DEV - 1)),
           pltpu.SemaphoreType.DMA((n,))],
        compiler_params=_params(48),
    )(qkv, qkv, qkv, bias_table, *shards)


def _attn_bwd(qkv, bias_table, d_out, to_chip):
    s = qkv.shape[0] - Z_PAD
    nb = s // Q_BLOCK
    n = len(to_chip)
    pairs = N_HEADS // 2

    def body(*refs):
        q_ref, k_ref, v_ref, bt_ref, do_ref = refs[:5]
        to_chip_refs = refs[5:5 + n]
        dqkv_ref, db_ref = refs[5 + n:7 + n]
        from_chip_refs = refs[7 + n:7 + 2 * n]
        dk_acc, dv_acc, send_sems, recv_sems = refs[7 + 2 * n:]
        p_id = pl.program_id(0)

        @pl.when(p_id == 0)
        def _():
            for cp in _owner_copies(to_chip_refs, from_chip_refs, send_sems, recv_sems):
                cp.start()

        dk_acc[...] = jnp.zeros(dk_acc.shape, F32)
        dv_acc[...] = jnp.zeros(dv_acc.shape, F32)
        db_ref[...] = jnp.zeros(db_ref.shape, F32)
        masks = _head_masks()

        def block(b, carry):
            r0 = pl.multiple_of(b * Q_BLOCK, Q_BLOCK)
            q2 = _stack_heads(q_ref[pl.ds(r0 + Z_PAD, Q_BLOCK), :], masks)
            do2 = _stack_heads(do_ref[pl.ds(r0, Q_BLOCK), :], masks)
            kcat = k_ref[pl.ds(r0, K_SPAN), :]
            vcat = v_ref[pl.ds(r0, K_SPAN), :]
            p = _softmax_rows(q2, kcat, bt_ref[...].reshape(2 * Q_BLOCK, K_SPAN), b)
            dp = _dot(do2, vcat, NT)
            ds = p * (dp - jnp.sum(p * dp, axis=-1, keepdims=True))
            db_ref[...] += ds.reshape(2, Q_BLOCK, K_SPAN)
            dsb = ds.astype(BF16)
            dq = _dot(_side_by_side(dsb), _stack_heads(kcat, masks))
            dqkv_ref[0, pl.ds(r0, Q_BLOCK), :] = (dq * Q_SCALE).astype(BF16)
            dk_acc[pl.ds(r0, K_SPAN), :] += _dot(dsb, q2, TN)
            dv_acc[pl.ds(r0, K_SPAN), :] += _dot(p.astype(BF16), do2, TN)
            return carry

        lax.fori_loop(0, nb, block, 0, unroll=2)
        dqkv_ref[1] = dk_acc[Z_PAD:, :].astype(BF16)
        dqkv_ref[2] = dv_acc[Z_PAD:, :].astype(BF16)

        @pl.when(p_id == pairs - 1)
        def _():
            for cp in _owner_copies(to_chip_refs, from_chip_refs, send_sems, recv_sems):
                cp.wait_recv()
                cp.wait_send()

    hbm = pl.BlockSpec(memory_space=pl.ANY)
    return pl.pallas_call(
        body, name="attn_bwd",
        grid=(pairs,),
        in_specs=_attn_specs(s + Z_PAD) + [pl.BlockSpec((s, 128), lambda p: (0, p))] + [hbm] * n,
        out_specs=(pl.BlockSpec((3, s, 128), lambda p: (0, 0, p)),
                   pl.BlockSpec((2, Q_BLOCK, K_SPAN), lambda p: (p, 0, 0))) + (hbm,) * n,
        out_shape=(jax.ShapeDtypeStruct((3, s, D_A), BF16),
                   jax.ShapeDtypeStruct((N_HEADS, Q_BLOCK, K_SPAN), F32))
        + tuple(jax.ShapeDtypeStruct(t.shape, t.dtype) for t in to_chip),
        scratch_shapes=[pltpu.VMEM((s + Z_PAD, 128), F32), pltpu.VMEM((s + Z_PAD, 128), F32),
                        pltpu.SemaphoreType.DMA((n, 3)), pltpu.SemaphoreType.DMA((n, 3))],
        compiler_params=_params(56),
    )(qkv, qkv, qkv, bias_table, d_out, *to_chip)


def _mid_fwd_bwd(x, target, attn_out, z, w_pa, w_pb, w_out, b_gate, ln_g, ln_b, w_s, b_s, final_g):
    s = x.shape[0]
    tm = TOKEN_TILE
    nt = s // tm
    n_sub = tm // SGU_CHUNK

    def body(x_ref, t_ref, oa_ref, ga_ref, ub_ref, vb_ref, gb_ref, ta0_ref, ta1_ref, tb0_ref, tb1_ref,
             wpa_hbm, wpb_hbm, wout_hbm, bg_ref, lng_ref, lnb_ref, ws_ref, bs_ref, fg_ref,
             dx2_ref, doa_ref, dz_ref, dwout_hbm, dwpa_hbm, dwpb_hbm, dbg_ref, dfg_ref, dlng_ref, dlnb_ref, dws_ref,
             dbs_ref, loss_ref,
             wpa, wpb, wout, wmix, acc_out, acc_pa, acc_pb, mixed_s, dvn_s, sem):
        i = pl.program_id(0)

        @pl.when(i == 0)
        def _():
            loads = [pltpu.make_async_copy(src, dst, sem.at[n])
                     for n, (src, dst) in enumerate(((wpa_hbm, wpa), (wpb_hbm, wpb), (wout_hbm, wout)))]
            for cp in loads:
                cp.start()
            t_idx = lax.broadcasted_iota(jnp.int32, (SGU_CHUNK, SGU_CHUNK), 0)
            s_idx = lax.broadcasted_iota(jnp.int32, (SGU_CHUNK, SGU_CHUNK), 1)
            for g in range(N_GROUPS):
                wmix[g] = jnp.where(s_idx <= t_idx, ws_ref[g], 0.0).astype(BF16)
            for ref in (acc_out, acc_pa, acc_pb, dbg_ref, dfg_ref, dlng_ref, dlnb_ref, dws_ref, dbs_ref, loss_ref):
                ref[...] = jnp.zeros(ref.shape, F32)
            for cp in loads:
                cp.wait()

        g_a = ga_ref[...].astype(F32)
        u_b = ub_ref[...].astype(F32)
        v_b = vb_ref[...].astype(F32)
        g_b = gb_ref[...].astype(F32)
        bg = bg_ref[...]

        sg_a = _sigmoid(g_a)
        silu_a = g_a * sg_a
        o_a = oa_ref[...]
        y_a = (o_a * silu_a).astype(BF16)

        ug, dgelu_u = _gelu_and_grad(u_b)
        vg, dgelu_v = _gelu_and_grad(v_b)
        mu = jnp.mean(vg, axis=-1, keepdims=True)
        vc = vg - mu
        rstd = lax.rsqrt(jnp.mean(vc * vc, axis=-1, keepdims=True) + EPS)
        vhat = vc * rstd
        lng = lng_ref[...]
        vn = (vhat * lng + lnb_ref[...]).astype(BF16)
        for n in range(n_sub):
            rows = slice(n * SGU_CHUNK, (n + 1) * SGU_CHUNK)
            for g in range(N_GROUPS):
                cols = slice(g * 128, (g + 1) * 128)
                mixed_s[rows, cols] = _dot(wmix[g], vn[rows, cols]) + bs_ref[g]
        mixed = mixed_s[...]
        sg_b = _sigmoid(g_b)
        silu_b = g_b * sg_b
        um = ug * mixed
        y_b = (um * silu_b).astype(BF16)

        p_a = _dot(y_a, wpa[...])
        p_b = _dot(y_b, wpb[...])
        gate_a = _sigmoid(jnp.concatenate([ta0_ref[...], ta1_ref[...]], axis=1).astype(F32) + bg[:, :D_MODEL])
        gate_b = _sigmoid(jnp.concatenate([tb0_ref[...], tb1_ref[...]], axis=1).astype(F32) + bg[:, D_MODEL:])
        merged = (gate_a * p_a + gate_b * p_b).astype(BF16)
        x2 = x_ref[...] + _dot(merged, wout[...])
        r2 = lax.rsqrt(jnp.mean(x2 * x2, axis=-1, keepdims=True) + EPS)
        xh = x2 * r2
        fg = fg_ref[...]
        err = xh * fg - t_ref[...]
        loss_ref[...] += jnp.sum(jnp.sum(err * err, axis=-1, keepdims=True), axis=0, keepdims=True) * (0.5 / D_MODEL)

        dy = err * (1.0 / D_MODEL)
        dfg_ref[...] += jnp.sum(dy * xh, axis=0, keepdims=True)
        gy = dy * fg
        dx2 = r2 * (gy - xh * jnp.mean(gy * xh, axis=-1, keepdims=True))
        dx2_ref[...] = dx2
        dx2b = dx2.astype(BF16)
        dmerged = _dot(dx2b, wout[...], NT)
        acc_out[...] += _dot(merged, dx2b, TN)

        dp_a = dmerged * gate_a
        dp_b = dmerged * gate_b
        dgate_a = dp_a * p_a * (1.0 - gate_a)
        dgate_b = dp_b * p_b * (1.0 - gate_b)
        dbg_ref[:, :D_MODEL] += jnp.sum(dgate_a, axis=0, keepdims=True)
        dbg_ref[:, D_MODEL:] += jnp.sum(dgate_b, axis=0, keepdims=True)
        dz_ref[:, 2048:3072] = dgate_a.astype(BF16)
        dz_ref[:, 3072:4096] = dgate_b.astype(BF16)
        dp_ab = dp_a.astype(BF16)
        dp_bb = dp_b.astype(BF16)
        dy_a = _dot(dp_ab, wpa[...], NT)
        dy_b = _dot(dp_bb, wpb[...], NT)
        acc_pa[...] += _dot(y_a, dp_ab, TN)
        acc_pb[...] += _dot(y_b, dp_bb, TN)

        doa_ref[...] = (dy_a * silu_a).astype(BF16)
        dz_ref[:, 0:512] = (dy_a * o_a * (sg_a * (1.0 + g_a * (1.0 - sg_a)))).astype(BF16)
        dz_ref[:, 1536:2048] = (dy_b * um * (sg_b * (1.0 + g_b * (1.0 - sg_b)))).astype(BF16)
        dys = dy_b * silu_b
        dz_ref[:, 512:1024] = (dys * mixed * dgelu_u).astype(BF16)
        dmixed = dys * ug
        dmb = dmixed.astype(BF16)
        for n in range(n_sub):
            rows = slice(n * SGU_CHUNK, (n + 1) * SGU_CHUNK)
            for g in range(N_GROUPS):
                cols = slice(g * 128, (g + 1) * 128)
                dws_ref[g] += _dot(dmb[rows, cols], vn[rows, cols], NT)
                dbs_ref[g] += jnp.sum(dmixed[rows, cols], axis=-1, keepdims=True)
                dvn_s[rows, cols] = _dot(wmix[g], dmb[rows, cols], TN)
        dvn = dvn_s[...]
        dlng_ref[...] += jnp.sum(dvn * vhat, axis=0, keepdims=True)
        dlnb_ref[...] += jnp.sum(dvn, axis=0, keepdims=True)
        dvh = dvn * lng
        dvg = rstd * (dvh - jnp.mean(dvh, axis=-1, keepdims=True) - vhat * jnp.mean(dvh * vhat, axis=-1, keepdims=True))
        dz_ref[:, 1024:1536] = (dvg * dgelu_v).astype(BF16)

        @pl.when(i == nt - 1)
        def _():
            t_idx = lax.broadcasted_iota(jnp.int32, (SGU_CHUNK, SGU_CHUNK), 0)
            s_idx = lax.broadcasted_iota(jnp.int32, (SGU_CHUNK, SGU_CHUNK), 1)
            for g in range(N_GROUPS):
                dws_ref[g] = jnp.where(s_idx <= t_idx, dws_ref[g], 0.0)
            stores = [pltpu.make_async_copy(src, dst, sem.at[n])
                      for n, (src, dst) in enumerate(((acc_out, dwout_hbm), (acc_pa, dwpa_hbm), (acc_pb, dwpb_hbm)))]
            for cp in stores:
                cp.start()
            for cp in stores:
                cp.wait()

    tile = lambda w: pl.BlockSpec((tm, w), lambda i: (i, 0))
    whole = lambda shape: pl.BlockSpec(shape, lambda i: (0,) * len(shape))
    hbm = pl.BlockSpec(memory_space=pl.ANY)
    return pl.pallas_call(
        body, name="mid_fwd_bwd",
        grid=(nt,),
        in_specs=[tile(D_MODEL), tile(D_MODEL), tile(D_A)]
        + [pl.BlockSpec((tm, COL_BLOCK), functools.partial(lambda c, i: (i + Z_PAD // tm, c), c))
           for c in range(3, N_COL_BLOCKS)]
        + [hbm, hbm, hbm,
                  whole((1, 2 * D_MODEL)), whole((1, D_B)), whole((1, D_B)),
                  whole((N_GROUPS, SGU_CHUNK, SGU_CHUNK)), whole((N_GROUPS, SGU_CHUNK, 1)), whole((1, D_MODEL))],
        out_specs=(tile(D_MODEL), tile(D_A), tile(REST), hbm, hbm, hbm,
                   whole((1, 2 * D_MODEL)), whole((1, D_MODEL)), whole((1, D_B)), whole((1, D_B)),
                   whole((N_GROUPS, SGU_CHUNK, SGU_CHUNK)), whole((N_GROUPS, SGU_CHUNK, 1)), whole((1, 1))),
        out_shape=(jax.ShapeDtypeStruct((s, D_MODEL), F32), jax.ShapeDtypeStruct((s, D_A), BF16),
                   jax.ShapeDtypeStruct((s, REST), BF16),
                   jax.ShapeDtypeStruct((D_MODEL, D_MODEL), F32), jax.ShapeDtypeStruct((D_A, D_MODEL), F32),
                   jax.ShapeDtypeStruct((D_B, D_MODEL), F32),
                   jax.ShapeDtypeStruct((1, 2 * D_MODEL), F32), jax.ShapeDtypeStruct((1, D_MODEL), F32),
                   jax.ShapeDtypeStruct((1, D_B), F32), jax.ShapeDtypeStruct((1, D_B), F32),
                   jax.ShapeDtypeStruct((N_GROUPS, SGU_CHUNK, SGU_CHUNK), F32),
                   jax.ShapeDtypeStruct((N_GROUPS, SGU_CHUNK, 1), F32), jax.ShapeDtypeStruct((1, 1), F32)),
        scratch_shapes=[pltpu.VMEM((D_A, D_MODEL), BF16), pltpu.VMEM((D_B, D_MODEL), BF16),
                        pltpu.VMEM((D_MODEL, D_MODEL), BF16), pltpu.VMEM((N_GROUPS, SGU_CHUNK, SGU_CHUNK), BF16),
                        pltpu.VMEM((D_MODEL, D_MODEL), F32), pltpu.VMEM((D_A, D_MODEL), F32),
                        pltpu.VMEM((D_B, D_MODEL), F32),
                        pltpu.VMEM((tm, D_B), F32), pltpu.VMEM((tm, D_B), F32),
                        pltpu.SemaphoreType.DMA((3,))],
        compiler_params=_params(56),
    )(x, target, attn_out, *([z] * (N_COL_BLOCKS - 3)), w_pa, w_pb, w_out, b_gate, ln_g, ln_b, w_s, b_s, final_g)


def _proj_bwd_x(dqkv, drest, x, dx2, norm_g, w_in_t, to_chip):
    s = x.shape[0]
    tm = TOKEN_TILE
    nt = s // tm
    n = len(to_chip)

    def body(*refs):
        dqkv_ref, dr_ref, x_ref, dx2_ref, g_ref, w_hbm = refs[:6]
        to_chip_refs = refs[6:6 + n]
        dx_ref, dg_ref = refs[6 + n:8 + n]
        from_chip_refs = refs[8 + n:8 + 2 * n]
        w, sem, send_sems, recv_sems = refs[8 + 2 * n:]
        i = pl.program_id(0)

        @pl.when(i == 0)
        def _():
            for rc in _owner_copies(to_chip_refs, from_chip_refs, send_sems, recv_sems):
                rc.start()
            cp = pltpu.make_async_copy(w_hbm, w, sem)
            cp.start()
            dg_ref[...] = jnp.zeros(dg_ref.shape, F32)
            cp.wait()

        dh = None
        for c in range(N_COL_BLOCKS):
            dz = dqkv_ref[c] if c < 3 else dr_ref[:, (c - 3) * COL_BLOCK:(c - 2) * COL_BLOCK]
            part = _dot(dz, w[c * COL_BLOCK:(c + 1) * COL_BLOCK, :])
            dh = part if dh is None else dh + part
        xf = x_ref[...]
        r = lax.rsqrt(jnp.mean(xf * xf, axis=-1, keepdims=True) + EPS)
        xn = xf * r
        dg_ref[...] += jnp.sum(dh * xn, axis=0, keepdims=True)
        gh = dh * g_ref[...]
        dx_ref[...] = r * (gh - xn * jnp.mean(gh * xn, axis=-1, keepdims=True)) + dx2_ref[...]

        @pl.when(i == nt - 1)
        def _():
            for rc in _owner_copies(to_chip_refs, from_chip_refs, send_sems, recv_sems):
                rc.wait_recv()
                rc.wait_send()

    hbm = pl.BlockSpec(memory_space=pl.ANY)
    return pl.pallas_call(
        body, name="proj_bwd_x",
        grid=(nt,),
        in_specs=[pl.BlockSpec((3, tm, D_A), lambda i: (0, i, 0)),
                  pl.BlockSpec((tm, REST), lambda i: (i, 0)),
                  pl.BlockSpec((tm, D_MODEL), lambda i: (i, 0)),
                  pl.BlockSpec((tm, D_MODEL), lambda i: (i, 0)),
                  pl.BlockSpec((1, D_MODEL), lambda i: (0, 0)),
                  hbm] + [hbm] * n,
        out_specs=(pl.BlockSpec((tm, D_MODEL), lambda i: (i, 0)),
                   pl.BlockSpec((1, D_MODEL), lambda i: (0, 0))) + (hbm,) * n,
        out_shape=(jax.ShapeDtypeStruct((s, D_MODEL), F32), jax.ShapeDtypeStruct((1, D_MODEL), F32))
        + tuple(jax.ShapeDtypeStruct(t.shape, t.dtype) for t in to_chip),
        scratch_shapes=[pltpu.VMEM((D_IN, D_MODEL), BF16), pltpu.SemaphoreType.DMA,
                        pltpu.SemaphoreType.DMA((n, 3)), pltpu.SemaphoreType.DMA((n, 3))],
        compiler_params=_params(48),
    )(dqkv, drest, x, dx2, norm_g, w_in_t, *to_chip)


def _proj_bwd_w(h, dqkv, drest):
    s = h.shape[0]
    tk = min(s, 1024)
    nk = s // tk

    def body(h_ref, dqkv_ref, dr_ref, o_ref, acc):
        j = pl.program_id(0)
        i = pl.program_id(1)

        @pl.when(i == 0)
        def _():
            acc[...] = jnp.zeros(acc.shape, F32)

        @pl.when(j < 3)
        def _():
            acc[...] += _dot(dqkv_ref[...], h_ref[...], TN)

        @pl.when(j >= 3)
        def _():
            acc[...] += _dot(dr_ref[...], h_ref[...], TN)

        @pl.when(i == nk - 1)
        def _():
            o_ref[...] = acc[...].astype(BF16)

    return pl.pallas_call(
        body, name="proj_bwd_w",
        grid=(N_COL_BLOCKS, nk),
        in_specs=[pl.BlockSpec((tk, D_MODEL), lambda j, i: (i, 0)),
                  pl.BlockSpec((None, tk, COL_BLOCK),
                               lambda j, i: (jnp.minimum(j, 2), jnp.where(j < 3, i, nk - 1), 0)),
                  pl.BlockSpec((tk, COL_BLOCK),
                               lambda j, i: (jnp.where(j >= 3, i, 0), jnp.maximum(j - 3, 0)))],
        out_specs=pl.BlockSpec((COL_BLOCK, D_MODEL), lambda j, i: (j, 0)),
        out_shape=jax.ShapeDtypeStruct((D_IN, D_MODEL), BF16),
        scratch_shapes=[pltpu.VMEM((COL_BLOCK, D_MODEL), F32)],
        compiler_params=_params(40),
    )(h, dqkv, drest)


def _adamw_math(w, g, m, v):
    c1 = 1.0 - ADAM_B1 ** ADAM_STEP
    c2 = 1.0 - ADAM_B2 ** ADAM_STEP
    nm = ADAM_B1 * m + (1.0 - ADAM_B1) * g
    nv = ADAM_B2 * v + (1.0 - ADAM_B2) * (g * g)
    return -ADAM_LR * ((nm / c1) / (jnp.sqrt(nv / c2) + ADAM_EPS) + ADAM_WD * w), nm, nv


def _adamw(name, w, g, m, v, from_chip):
    rows, cols = w.shape
    tr = rows if rows * cols <= 512 * 1024 else next(t for t in range(256, 7, -8) if rows % t == 0)

    def body(w_ref, g_ref, m_ref, v_ref, t_ref, g_out, d_ref, nm_ref, nv_ref):
        gg = g_ref[...]
        for j in range(3):
            gg = gg + t_ref[j].astype(F32)
        g_out[...] = gg
        d_ref[...], nm_ref[...], nv_ref[...] = _adamw_math(w_ref[...], gg, m_ref[...], v_ref[...])

    spec = pl.BlockSpec((tr, cols), lambda i: (i, 0))
    shape = jax.ShapeDtypeStruct((rows, cols), F32)
    return pl.pallas_call(
        body, name=name,
        grid=(rows // tr,),
        in_specs=[spec] * 4 + [pl.BlockSpec((3, tr, cols), lambda i: (0, i, 0))],
        out_specs=(spec,) * 4, out_shape=(shape,) * 4,
        compiler_params=_params(32),
    )(w, g, m, v, from_chip)


_SMALL = (("norm_g", (1, D_MODEL)), ("b_gate", (1, 2 * D_MODEL)), ("rel_bias", (N_HEADS, N_REL)),
          ("sgu_ln_g", (1, D_B)), ("sgu_ln_b", (1, D_B)), ("w_s", (N_GROUPS * SGU_CHUNK, SGU_CHUNK)),
          ("b_s", (N_GROUPS, SGU_CHUNK)), ("final_g", (1, D_MODEL)))


def _adamw_small(slab, g_ws, weights, moments_m, moments_v):
    k = len(_SMALL)

    def grad_of(name, slab_ref, ws_ref):
        if name == "norm_g":
            return slab_ref[ROW_NORM_G:ROW_NORM_G + 1, :]
        if name == "b_gate":
            return jnp.concatenate([slab_ref[ROW_B_GATE:ROW_B_GATE + 1, :], slab_ref[ROW_B_GATE + 1:ROW_B_GATE + 2, :]],
                                   axis=1)
        if name == "rel_bias":
            return slab_ref[ROW_REL:ROW_REL + N_HEADS, :N_REL]
        if name == "sgu_ln_g":
            return slab_ref[ROW_LN_G:ROW_LN_G + 1, :D_B]
        if name == "sgu_ln_b":
            return slab_ref[ROW_LN_B:ROW_LN_B + 1, :D_B]
        if name == "w_s":
            return ws_ref[...]
        if name == "b_s":
            return slab_ref[ROW_B_S:ROW_B_S + N_GROUPS, :SGU_CHUNK]
        return slab_ref[ROW_FINAL_G:ROW_FINAL_G + 1, :]

    def body(*refs):
        slab_ref, ws_ref = refs[:2]
        w_refs, m_refs, v_refs = refs[2:2 + k], refs[2 + k:2 + 2 * k], refs[2 + 2 * k:2 + 3 * k]
        outs = refs[2 + 3 * k:]
        for n, (name, _) in enumerate(_SMALL):
            g = grad_of(name, slab_ref, ws_ref)
            outs[n][...] = g
            outs[k + n][...], outs[2 * k + n][...], outs[3 * k + n][...] = _adamw_math(
                w_refs[n][...], g, m_refs[n][...], v_refs[n][...])
        outs[4 * k][...] = slab_ref[ROW_LOSS:ROW_LOSS + 1, :1]

    vmem = pl.BlockSpec(memory_space=pltpu.VMEM)
    shapes = tuple(jax.ShapeDtypeStruct(shape, F32) for _, shape in _SMALL)
    return pl.pallas_call(
        body, name="adamw_small",
        out_shape=shapes * 4 + (jax.ShapeDtypeStruct((1, 1), F32),),
        in_specs=[vmem] * (2 + 3 * k), out_specs=tuple([vmem] * (4 * k + 1)),
        compiler_params=_params(16),
    )(slab, g_ws, *weights, *moments_m, *moments_v)


def _pad_rel(a):
    return jnp.pad(a.reshape(N_HEADS, N_REL), ((0, 0), (0, N_REL_PAD - N_REL)))


def kernel(x, norm_g, w_in, b_gate, rel_bias, sgu_ln_g, sgu_ln_b, w_s, b_s, w_pa, w_pb, w_out, final_g, loss_target, m_norm_g, m_w_in, m_b_gate, m_rel_bias, m_sgu_ln_g, m_sgu_ln_b, m_w_s, m_b_s, m_w_pa, m_w_pb, m_w_out, m_final_g, v_norm_g, v_w_in, v_b_gate, v_rel_bias, v_sgu_ln_g, v_sgu_ln_b, v_w_s, v_b_s, v_w_pa, v_w_pb, v_w_out, v_final_g):
    s = x.shape[1]
    xs = x.reshape(s, D_MODEL)
    tgt = loss_target.reshape(s, D_MODEL)

    bias_table = _bias_table(_pad_rel(rel_bias))
    w_in_t = jnp.swapaxes(w_in[0], 0, 1)
    qkv, h, w_in_t_full = _gather_proj_fwd(xs, norm_g, w_in_t)
    attn_out, g_pa, g_pb, g_out = _attn_fwd(qkv, bias_table, (w_pa[0], w_pb[0], w_out[0]))
    w_pa_full = jnp.transpose(g_pa, (1, 0, 2)).reshape(D_A, D_MODEL)
    w_pb_full = jnp.transpose(g_pb, (1, 0, 2)).reshape(D_B, D_MODEL)
    w_out_full = g_out.reshape(D_MODEL, D_MODEL)

    (dx2, d_attn, drest, dw_out, dw_pa, dw_pb, d_bgate, d_fg, d_lng, d_lnb, d_ws, d_bs, loss_part) = _mid_fwd_bwd(
        xs, tgt, attn_out, qkv, w_pa_full, w_pb_full, w_out_full, b_gate, sgu_ln_g, sgu_ln_b, w_s[0],
        b_s.reshape(N_GROUPS, SGU_CHUNK, 1), final_g.reshape(1, D_MODEL))

    own_pa, own_pb, own_out, tc_pa, tc_pb, tc_out = _reduce_chip(
        "reduce_chip_proj", (dw_pa, dw_pb, dw_out), (1, 1, 0))
    dqkv, dbias, fc_pa, fc_pb, fc_out = _attn_bwd(qkv, bias_table, d_attn, (tc_pa, tc_pb, tc_out))
    d_rel = _bias_grad(dbias)
    dw_in_t = _proj_bwd_w(h, dqkv, drest)
    own_in, tc_in = _reduce_chip("reduce_chip_in", (dw_in_t,), (0,))
    grad_x, d_ng, fc_in = _proj_bwd_x(dqkv, drest, xs, dx2, norm_g, w_in_t_full, (tc_in,))
    big = {"w_in": tuple(jnp.swapaxes(t, 0, 1)[None] for t in _adamw(
        "adamw_w_in", w_in_t, own_in, jnp.swapaxes(m_w_in[0], 0, 1), jnp.swapaxes(v_w_in[0], 0, 1), fc_in))}
    for name, w, g, fc, m, v in (("w_pa", w_pa, own_pa, fc_pa, m_w_pa, v_w_pa),
                                 ("w_pb", w_pb, own_pb, fc_pb, m_w_pb, v_w_pb),
                                 ("w_out", w_out, own_out, fc_out, m_w_out, v_w_out)):
        big[name] = tuple(t[None] for t in _adamw("adamw_" + name, w[0], g, m[0], v[0], fc))

    slab, g_ws = _reduce_small(d_ng, d_bgate, d_rel, d_lng, d_lnb, d_fg, loss_part, d_bs, d_ws)
    as_2d = lambda leaves: [a.reshape(shape) for a, (_, shape) in zip(leaves, _SMALL)]
    small_out = _adamw_small(
        slab, g_ws, as_2d((norm_g, b_gate, rel_bias, sgu_ln_g, sgu_ln_b, w_s, b_s, final_g)),
        as_2d((m_norm_g, m_b_gate, m_rel_bias, m_sgu_ln_g, m_sgu_ln_b, m_w_s, m_b_s, m_final_g)),
        as_2d((v_norm_g, v_b_gate, v_rel_bias, v_sgu_ln_g, v_sgu_ln_b, v_w_s, v_b_s, v_final_g)))
    small_index = {name: n for n, (name, _) in enumerate(_SMALL)}

    def leaf(kind, name, like):
        if name in big:
            return big[name][kind]
        return small_out[kind * len(_SMALL) + small_index[name]].reshape(like.shape)

    weights = (("norm_g", norm_g), ("w_in", w_in), ("b_gate", b_gate), ("rel_bias", rel_bias), ("sgu_ln_g", sgu_ln_g),
               ("sgu_ln_b", sgu_ln_b), ("w_s", w_s), ("b_s", b_s), ("w_pa", w_pa), ("w_pb", w_pb), ("w_out", w_out),
               ("final_g", final_g))
    outs = [small_out[-1].reshape(()), grad_x.reshape(x.shape)]
    for kind in range(4):
        outs.extend(leaf(kind, name, like) for name, like in weights)
    return tuple(outs)
```

```python
import functools
import math

import jax
import jax.numpy as jnp
from jax import lax
from jax.experimental import pallas as pl
from jax.experimental.pallas import tpu as pltpu

F32 = jnp.float32
BF16 = jnp.bfloat16
MESH = pl.DeviceIdType.MESH
N_DEV = 8

D_MODEL = 1024
D_A = 512
D_B = 512
D_IN = 5632
N_HEADS = 8
HEAD_DIM = 64
CHUNK = 64
N_PREV = 8
REL_CLIP = 128
N_REL = 2 * REL_CLIP + 1
N_REL_PAD = 384
SGU_CHUNK = 128
N_GROUPS = 4
EPS = 1e-6
NEG_INF = -1e30
Q_SCALE = HEAD_DIM ** -0.5

Q_BLOCK = 256
K_SPAN = 768
Z_PAD = K_SPAN - Q_BLOCK
ROLL_W = 1024
COL_BLOCK = 512
N_COL_BLOCKS = D_IN // COL_BLOCK
REST = D_IN - 3 * D_A
TOKEN_TILE = 256
V7X_VMEM_BYTES = 64 * 1024 * 1024

ADAM_LR = 0.001
ADAM_B1 = 0.9
ADAM_B2 = 0.999
ADAM_EPS = 1e-08
ADAM_WD = 0.01
ADAM_STEP = 10

GELU_C = math.sqrt(2.0 / math.pi)
GELU_A = 0.044715

NT = (((1,), (1,)), ((), ()))
TN = (((0,), (0,)), ((), ()))
HIGHEST = lax.Precision.HIGHEST


def _params(vmem_mb, **kw):
    return pltpu.CompilerParams(vmem_limit_bytes=vmem_mb * 1024 * 1024, **kw)


def _dot(a, b, dims=None):
    if dims is None:
        return jnp.dot(a, b, preferred_element_type=F32)
    return lax.dot_general(a, b, dims, preferred_element_type=F32)


def _sigmoid(x):
    return 1.0 / (1.0 + jnp.exp(-x))


def _gelu_and_grad(u):
    u2 = u * u
    t = jnp.tanh(GELU_C * (u + GELU_A * u * u2))
    half = 0.5 * (1.0 + t)
    g = u * half
    dg = half + 0.5 * u * (1.0 - t * t) * (GELU_C * (1.0 + 3.0 * GELU_A * u2))
    return g, dg


def _my_pos():
    return lax.axis_index("x"), lax.axis_index("y"), lax.axis_index("c")


def _flat_id(pos):
    return 4 * pos[0] + 2 * pos[1] + pos[2]


def _peer(pos, k):
    x, y, c = pos
    return (1 - x if k & 4 else x, 1 - y if k & 2 else y, 1 - c if k & 1 else c)


def _other_chips(pos):
    x, y, _ = pos
    return ((1 - x, y), (x, 1 - y), (1 - x, 1 - y))


class _SlotGather:
    def __init__(self, bufs, send_sems, recv_sems, own=None):
        self.bufs, self.send_sems, self.recv_sems = bufs, send_sems, recv_sems
        self.own = own if own is not None else [None] * len(bufs)
        x, y, c = _my_pos()
        self.c, self.me, self.sib = c, (x, y, c), (x, y, 1 - c)
        self.chips = _other_chips(self.me)

    def _copy(self, a, k, block, to):
        slot = _flat_id(block)
        src = self.own[a] if (k < 4 and self.own[a] is not None) else self.bufs[a].at[slot]
        return pltpu.make_async_remote_copy(
            src_ref=src, dst_ref=self.bufs[a].at[slot],
            send_sem=self.send_sems.at[a, k], recv_sem=self.recv_sems.at[a, k], device_id=to, device_id_type=MESH)

    def _own_sends(self):
        n = len(self.bufs)
        return ([self._copy(a, 1 + j, self.me, (*chip, self.c)) for j, chip in enumerate(self.chips) for a in range(n)]
                + [self._copy(a, 0, self.me, self.sib) for a in range(n)])

    def _passes(self):
        return [self._copy(a, 4 + j, (*chip, self.c), self.sib)
                for j, chip in enumerate(self.chips) for a in range(len(self.bufs))]

    def start(self):
        for cp in self._own_sends():
            cp.start()

    def pass_on(self):
        for j, chip in enumerate(self.chips):
            for a in range(len(self.bufs)):
                self._copy(a, 1 + j, (*chip, self.c), self.me).wait_recv()
                self._copy(a, 4 + j, (*chip, self.c), self.sib).start()

    def finish(self):
        for a in range(len(self.bufs)):
            self._copy(a, 0, self.sib, self.me).wait_recv()
            for j, chip in enumerate(self.chips):
                self._copy(a, 4 + j, (*chip, 1 - self.c), self.me).wait_recv()
        for cp in self._own_sends() + self._passes():
            cp.wait_send()


def _reduce_chip(name, parts, sharded_dim):
    n = len(parts)
    shapes = []
    for p, dim in zip(parts, sharded_dim):
        shape = list(p.shape)
        shape[dim] //= N_DEV
        shapes.append(tuple(shape))

    def body(*refs):
        full, own, to_chip = refs[:n], refs[n:2 * n], refs[2 * n:3 * n]
        ins, from_sib = refs[3 * n:4 * n], refs[4 * n:5 * n]
        send_sems, recv_sems = refs[5 * n], refs[5 * n + 1]
        x, y, c = _my_pos()
        sib = (x, y, 1 - c)
        chips = ((x, y),) + _other_chips((x, y, c))
        for a in range(n):
            rows, cols = shapes[a]
            for d in range(N_DEV):
                if sharded_dim[a] == 0:
                    ins[a][d] = full[a][d * rows:(d + 1) * rows, :].astype(BF16)
                else:
                    ins[a][d] = full[a][:, d * cols:(d + 1) * cols].astype(BF16)

        def to_sibling(a, r):
            return pltpu.make_async_remote_copy(
                src_ref=ins[a].at[_flat_id((*chips[r], 1 - c))], dst_ref=from_sib[a].at[r],
                send_sem=send_sems.at[a, r], recv_sem=recv_sems.at[a, r], device_id=sib, device_id_type=MESH)

        sends = [to_sibling(a, r) for r in (1, 2, 3, 0) for a in range(n)]
        for cp in sends:
            cp.start()
        for r in (1, 2, 3, 0):
            for a in range(n):
                to_sibling(a, r).wait_recv()
                both = ins[a][_flat_id((*chips[r], c))].astype(F32) + from_sib[a][r].astype(F32)
                if r == 0:
                    own[a][...] = both
                else:
                    to_chip[a][r - 1] = both.astype(BF16)
        for cp in sends:
            cp.wait_send()

    vmem = pl.BlockSpec(memory_space=pltpu.VMEM)
    return pl.pallas_call(
        body, name=name,
        out_shape=tuple(jax.ShapeDtypeStruct(sh, F32) for sh in shapes)
        + tuple(jax.ShapeDtypeStruct((3,) + sh, BF16) for sh in shapes),
        in_specs=[vmem] * n, out_specs=tuple([vmem] * (2 * n)),
        scratch_shapes=[pltpu.VMEM((N_DEV,) + sh, BF16) for sh in shapes]
        + [pltpu.VMEM((4,) + sh, BF16) for sh in shapes]
        + [pltpu.SemaphoreType.DMA((n, 4)), pltpu.SemaphoreType.DMA((n, 4))],
        compiler_params=_params(56),
    )(*parts)


def _owner_copies(to_chip, from_chip, send_sems, recv_sems):
    x, y, c = _my_pos()
    return [pltpu.make_async_remote_copy(
        src_ref=to_chip[a].at[j], dst_ref=from_chip[a].at[j],
        send_sem=send_sems.at[a, j], recv_sem=recv_sems.at[a, j], device_id=(*chip, c), device_id_type=MESH)
        for a in range(len(to_chip)) for j, chip in enumerate(_other_chips((x, y, c)))]


ROW_NORM_G, ROW_B_GATE, ROW_LN_G, ROW_LN_B, ROW_FINAL_G, ROW_LOSS, ROW_REL, ROW_B_S, SLAB_ROWS = 0, 1, 3, 4, 5, 6, 8, 16, 24


def _reduce_small(d_ng, d_bgate, d_rel, d_lng, d_lnb, d_fg, loss, d_bs, d_ws):
    def body(ng_ref, bg_ref, rel_ref, lng_ref, lnb_ref, fg_ref, loss_ref, bs_ref, ws_ref, slab_out, ws_out,
             slab_land, ws_land, send_sems, recv_sems):
        me = _flat_id(_my_pos())
        slab_land[me] = jnp.zeros((SLAB_ROWS, D_MODEL), F32)
        slab_land[me, ROW_NORM_G:ROW_NORM_G + 1, :] = ng_ref[...]
        slab_land[me, ROW_B_GATE:ROW_B_GATE + 1, :] = bg_ref[:, :D_MODEL]
        slab_land[me, ROW_B_GATE + 1:ROW_B_GATE + 2, :] = bg_ref[:, D_MODEL:]
        slab_land[me, ROW_LN_G:ROW_LN_G + 1, :D_B] = lng_ref[...]
        slab_land[me, ROW_LN_B:ROW_LN_B + 1, :D_B] = lnb_ref[...]
        slab_land[me, ROW_FINAL_G:ROW_FINAL_G + 1, :] = fg_ref[...]
        slab_land[me, ROW_LOSS:ROW_LOSS + 1, :1] = loss_ref[...]
        slab_land[me, ROW_REL:ROW_REL + N_HEADS, :N_REL_PAD] = rel_ref[...]
        eye = (lax.broadcasted_iota(jnp.int32, (SGU_CHUNK, SGU_CHUNK), 0)
               == lax.broadcasted_iota(jnp.int32, (SGU_CHUNK, SGU_CHUNK), 1))
        for g in range(N_GROUPS):
            row = jnp.sum(jnp.where(eye, bs_ref[g], 0.0), axis=0, keepdims=True)
            slab_land[me, ROW_B_S + g:ROW_B_S + g + 1, :SGU_CHUNK] = row
        ws_land[me] = ws_ref[...]
        gather = _SlotGather([slab_land, ws_land], send_sems, recv_sems)
        gather.start()
        gather.pass_on()
        gather.finish()
        for land, out in ((slab_land, slab_out), (ws_land, ws_out)):
            acc = land[0]
            for d in range(1, N_DEV):
                acc = acc + land[d]
            out[...] = acc

    vmem = pl.BlockSpec(memory_space=pltpu.VMEM)
    ws_shape = (N_GROUPS * SGU_CHUNK, SGU_CHUNK)
    return pl.pallas_call(
        body, name="reduce_small",
        out_shape=(jax.ShapeDtypeStruct((SLAB_ROWS, D_MODEL), F32), jax.ShapeDtypeStruct(ws_shape, F32)),
        in_specs=[vmem] * 9, out_specs=(vmem, vmem),
        scratch_shapes=[pltpu.VMEM((N_DEV, SLAB_ROWS, D_MODEL), F32), pltpu.VMEM((N_DEV,) + ws_shape, F32),
                        pltpu.SemaphoreType.DMA((2, N_DEV - 1)), pltpu.SemaphoreType.DMA((2, N_DEV - 1))],
        compiler_params=_params(16),
    )(d_ng, d_bgate, d_rel, d_lng, d_lnb, d_fg, loss, d_bs, d_ws.reshape(ws_shape))


def _rel_index(e):
    lo, hi = Z_PAD - REL_CLIP, Z_PAD + REL_CLIP
    return jnp.where(e <= lo, 2 * REL_CLIP, jnp.where(e < hi, hi - e, jnp.where(e <= K_SPAN, 0, 2 * REL_CLIP)))


def _bias_table(rel_bias_pad):
    def body(rb_ref, bt_ref):
        c = lax.broadcasted_iota(jnp.int32, (N_REL_PAD, ROLL_W), 1)
        r = lax.broadcasted_iota(jnp.int32, (N_REL_PAD, ROLL_W), 0)
        pick = (r == _rel_index(c)).astype(F32)
        rows = jnp.dot(rb_ref[...], pick, precision=HIGHEST, preferred_element_type=F32)
        qc = lax.broadcasted_iota(jnp.int32, (Q_BLOCK, K_SPAN), 0) >> 6
        kc = lax.broadcasted_iota(jnp.int32, (Q_BLOCK, K_SPAN), 1) >> 6
        band = (kc >= qc) & (kc <= qc + N_PREV)
        for h in range(N_HEADS):
            t = jnp.broadcast_to(rows[h:h + 1, :], (Q_BLOCK, ROLL_W))
            t = pltpu.roll(t, 0, 1, stride=1, stride_axis=0)
            bt_ref[h] = jnp.where(band, t[:, :K_SPAN], NEG_INF)

    return pl.pallas_call(
        body, name="bias_table",
        out_shape=jax.ShapeDtypeStruct((N_HEADS, Q_BLOCK, K_SPAN), F32),
        compiler_params=_params(32),
    )(rel_bias_pad)


def _bias_grad(dbias):
    def body(a_ref, o_ref):
        rr = lax.broadcasted_iota(jnp.int32, (Q_BLOCK, Q_BLOCK), 0)
        cc = lax.broadcasted_iota(jnp.int32, (Q_BLOCK, Q_BLOCK), 1)
        flip = (rr + cc == Q_BLOCK - 1).astype(F32)
        c = lax.broadcasted_iota(jnp.int32, (ROLL_W, N_REL_PAD), 0)
        r = lax.broadcasted_iota(jnp.int32, (ROLL_W, N_REL_PAD), 1)
        e = jnp.where(c >= Q_BLOCK - 1, c - (Q_BLOCK - 1), c + (ROLL_W - Q_BLOCK + 1))
        pick = (r == _rel_index(e)).astype(F32)
        sums = []
        for h in range(N_HEADS):
            a = jnp.dot(flip, a_ref[h], precision=HIGHEST, preferred_element_type=F32)
            a = jnp.concatenate([a, jnp.zeros((Q_BLOCK, ROLL_W - K_SPAN), F32)], axis=1)
            a = pltpu.roll(a, 0, 1, stride=1, stride_axis=0)
            sums.append(jnp.sum(a, axis=0, keepdims=True))
        diag = jnp.concatenate(sums, axis=0)
        o_ref[...] = jnp.dot(diag, pick, precision=HIGHEST, preferred_element_type=F32)

    return pl.pallas_call(
        body, name="bias_grad",
        out_shape=jax.ShapeDtypeStruct((N_HEADS, N_REL_PAD), F32),
        compiler_params=_params(32),
    )(dbias)


def _gather_proj_fwd(x, norm_g, w_in_t):
    s = x.shape[0]
    tm = 512 if s % 512 == 0 else TOKEN_TILE
    nt = s // tm
    n_pad = Z_PAD // tm
    shard_w = w_in_t.shape[0]
    chip_w = 2 * shard_w
    n_chips = N_DEV // 2

    def body(order_ref, x_ref, g_ref, win_hbm, z_ref, h_ref, wt_hbm, stage, wchip, hb, win_f32, send_sems, recv_sems,
             local_sems):
        j = pl.program_id(0)
        i = pl.program_id(1)
        x_, y_, c_ = _my_pos()
        me, sib = (x_, y_, c_), (x_, y_, 1 - c_)
        near = _other_chips(me)
        pick = lambda a, b: tuple(jnp.where(c_ == 0, u, v) for u, v in zip(a, b))
        passed_from, passed_to = pick(near[0], near[1]), pick(near[1], near[0])

        def rows_of(block):
            return wt_hbm.at[pl.ds(pl.multiple_of(_flat_id(block) * shard_w, 16), shard_w), :]

        def copy(k, block, to, own=False):
            return pltpu.make_async_remote_copy(
                src_ref=stage if own else rows_of(block), dst_ref=rows_of(block),
                send_sem=send_sems.at[k], recv_sem=recv_sems.at[k], device_id=to, device_id_type=MESH)

        def sends():
            return ([copy(0, me, sib, True), copy(1, me, (*near[0], c_), True), copy(2, me, (*near[1], c_), True),
                     copy(3, (*passed_from, c_), (*passed_to, c_))]
                    + [copy(4 + n, (*near[n], c_), sib) for n in range(3)])

        keep = pltpu.make_async_copy(stage, rows_of(me), local_sems.at[0])

        def fetch(chip):
            first = pl.multiple_of((2 * chip[0] + chip[1]) * chip_w, 16)
            cp = pltpu.make_async_copy(wt_hbm.at[pl.ds(first, chip_w), :], wchip, local_sems.at[1])
            cp.start()
            cp.wait()

        @pl.when((j == 0) & (i == 0))
        def _():
            load = pltpu.make_async_copy(win_hbm, win_f32, local_sems.at[1])
            load.start()
            load.wait()
            stage[...] = win_f32[...].astype(BF16)
            keep.start()
            for cp in sends()[:3]:
                cp.start()
            copy(0, sib, me).wait_recv()
            keep.wait()
            fetch((x_, y_))

        @pl.when((j == 1) & (i == 0))
        def _():
            copy(1, (*near[0], c_), me).wait_recv()
            copy(2, (*near[1], c_), me).wait_recv()
            for cp in sends()[3:6]:
                cp.start()
            copy(4, (*near[0], 1 - c_), me).wait_recv()
            fetch(near[0])

        @pl.when((j == 2) & (i == 0))
        def _():
            copy(5, (*near[1], 1 - c_), me).wait_recv()
            fetch(near[1])

        @pl.when((j == 3) & (i == 0))
        def _():
            copy(3, (*near[2], c_), me).wait_recv()
            copy(6, (*near[2], c_), sib).start()
            copy(6, (*near[2], 1 - c_), me).wait_recv()
            fetch(near[2])

        @pl.when(i < n_pad)
        def _():
            z_ref[...] = jnp.zeros(z_ref.shape, BF16)

        @pl.when(i >= n_pad)
        def _():
            rows = pl.ds(pl.multiple_of((i - n_pad) * tm, tm), tm)

            @pl.when(j == 0)
            def _():
                xf = x_ref[...]
                r = lax.rsqrt(jnp.mean(xf * xf, axis=-1, keepdims=True) + EPS)
                hf = (xf * r * g_ref[...]).astype(BF16)
                hb[rows, :] = hf
                h_ref[...] = hf

            blk = _dot(hb[rows, :], wchip[...], NT)
            q_scale = jnp.where(order_ref[j] == 0, Q_SCALE, 1.0).astype(F32)
            z_ref[:, :D_A] = (blk[:, :D_A] * q_scale).astype(BF16)
            z_ref[:, D_A:] = blk[:, D_A:].astype(BF16)

        @pl.when((j == n_chips - 1) & (i == n_pad + nt - 1))
        def _():
            for cp in sends():
                cp.wait_send()

    pos = _my_pos()
    order = jnp.stack([2 * cx + cy for cx, cy in ((pos[0], pos[1]),) + _other_chips(pos)]).astype(jnp.int32)
    first_pass = lambda j, i: jnp.where(j == 0, jnp.maximum(i - n_pad, 0), nt - 1)
    grid_spec = pltpu.PrefetchScalarGridSpec(
        num_scalar_prefetch=1,
        grid=(n_chips, n_pad + nt),
        in_specs=[pl.BlockSpec((tm, D_MODEL), lambda j, i, o: (first_pass(j, i), 0)),
                  pl.BlockSpec((1, D_MODEL), lambda j, i, o: (0, 0)),
                  pl.BlockSpec(memory_space=pl.ANY)],
        out_specs=(pl.BlockSpec((tm, chip_w), lambda j, i, o: (i, o[j])),
                   pl.BlockSpec((tm, D_MODEL), lambda j, i, o: (first_pass(j, i), 0)),
                   pl.BlockSpec(memory_space=pl.ANY)),
        scratch_shapes=[pltpu.VMEM((shard_w, D_MODEL), BF16), pltpu.VMEM((chip_w, D_MODEL), BF16),
                        pltpu.VMEM((s, D_MODEL), BF16), pltpu.VMEM(w_in_t.shape, F32),
                        pltpu.SemaphoreType.DMA((N_DEV - 1,)), pltpu.SemaphoreType.DMA((N_DEV - 1,)),
                        pltpu.SemaphoreType.DMA((2,))])
    return pl.pallas_call(
        body, name="gather_proj_fwd",
        grid_spec=grid_spec,
        out_shape=(jax.ShapeDtypeStruct((Z_PAD + s, D_IN), BF16), jax.ShapeDtypeStruct((s, D_MODEL), BF16),
                   jax.ShapeDtypeStruct((D_IN, D_MODEL), BF16)),
        compiler_params=_params(60),
    )(order, x, norm_g, w_in_t)


def _attn_specs(rows):
    pairs = N_HEADS // 2
    return ([pl.BlockSpec((rows, 128), functools.partial(lambda which, p: (0, which * pairs + p), which))
             for which in range(3)]
            + [pl.BlockSpec((2, Q_BLOCK, K_SPAN), lambda p: (p, 0, 0))])


def _head_masks():
    lane = lax.broadcasted_iota(jnp.int32, (1, 128), 1)
    first = lane < HEAD_DIM
    return (first, jnp.logical_not(first))


def _stack_heads(x, masks):
    zero = jnp.zeros((), x.dtype)
    return jnp.concatenate([jnp.where(m, x, zero) for m in masks], axis=0)


STRIP = 16


def _softmax_strips(s_ref, bias_ref, b):
    valid = lax.broadcasted_iota(jnp.int32, (1, K_SPAN), 1) >= Z_PAD - b * Q_BLOCK
    for t in range(2 * Q_BLOCK // STRIP):
        hh, r = divmod(t * STRIP, Q_BLOCK)
        st = s_ref[t * STRIP:(t + 1) * STRIP, :] + bias_ref[hh, r:r + STRIP, :]
        st = jnp.where(valid, st, NEG_INF)
        e = jnp.exp(st - jnp.max(st, axis=-1, keepdims=True))
        yield e * (1.0 / jnp.sum(e, axis=-1, keepdims=True))


def _side_by_side_strips(strips):
    half = len(strips) // 2
    return jnp.concatenate([jnp.concatenate([a, c], axis=1) for a, c in zip(strips[:half], strips[half:])], axis=0)


def _attn_fwd(qkv, bias_table, shards):
    s = qkv.shape[0] - Z_PAD
    nb = s // Q_BLOCK
    n = len(shards)
    pairs = N_HEADS // 2

    def body(*refs):
        q_ref, k_ref, v_ref, bt_ref = refs[:4]
        shard_refs = refs[4:4 + n]
        o_ref = refs[4 + n]
        slot_refs = refs[5 + n:5 + 2 * n]
        stages = refs[5 + 2 * n:5 + 3 * n]
        s_scr, send_sems, recv_sems, local_sems = refs[5 + 3 * n:]
        p_id = pl.program_id(0)
        gather = _SlotGather(slot_refs, send_sems, recv_sems, own=stages)
        keep = [pltpu.make_async_copy(stages[a], slot_refs[a].at[_flat_id(_my_pos())], local_sems.at[a])
                for a in range(n)]

        @pl.when(p_id == 0)
        def _():
            for a in range(n):
                stages[a][...] = shard_refs[a][...].astype(BF16)
                keep[a].start()
            gather.start()

        @pl.when(p_id == 1)
        def _():
            gather.pass_on()

        masks = _head_masks()

        def scores(b, half):
            r0 = pl.multiple_of(b * Q_BLOCK, Q_BLOCK)
            q2 = _stack_heads(q_ref[pl.ds(r0 + Z_PAD, Q_BLOCK), :], masks)
            s_scr[half] = _dot(q2, k_ref[pl.ds(r0, K_SPAN), :], NT)

        def finish(b, half):
            r0 = pl.multiple_of(b * Q_BLOCK, Q_BLOCK)
            v2 = _stack_heads(v_ref[pl.ds(r0, K_SPAN), :], masks)
            p = [st.astype(BF16) for st in _softmax_strips(s_scr.at[half], bt_ref, b)]
            o_ref[pl.ds(r0, Q_BLOCK), :] = _dot(_side_by_side_strips(p), v2)

        def two_blocks(i, carry):
            b = 2 * i
            scores(b + 1, 1)
            finish(b, 0)
            scores(jnp.minimum(b + 2, nb - 1), 0)
            finish(b + 1, 1)
            return carry

        scores(0, 0)
        lax.fori_loop(0, nb // 2, two_blocks, 0)

        @pl.when(p_id == pairs - 1)
        def _():
            gather.finish()
            for cp in keep:
                cp.wait()

    hbm = pl.BlockSpec(memory_space=pl.ANY)
    return pl.pallas_call(
        body, name="attn_fwd",
        grid=(pairs,),
        in_specs=_attn_specs(s + Z_PAD) + [pl.BlockSpec(a.shape, lambda p: (0, 0)) for a in shards],
        out_specs=(pl.BlockSpec((s, 128), lambda p: (0, p)),) + (hbm,) * n,
        out_shape=(jax.ShapeDtypeStruct((s, D_A), F32),)
        + tuple(jax.ShapeDtypeStruct((N_DEV,) + a.shape, BF16) for a in shards),
        scratch_shapes=[pltpu.VMEM(a.shape, BF16) for a in shards]
        + [pltpu.VMEM((2, 2 * Q_BLOCK, K_SPAN), F32),
           pltpu.SemaphoreType.DMA((n, N_DEV - 1)), pltpu.SemaphoreType.DMA((n, N_DEV - 1)),
           pltpu.SemaphoreType.DMA((n,))],
        compiler_params=_params(48),
    )(qkv, qkv, qkv, bias_table, *shards)


def _attn_bwd(qkv, bias_table, d_out, to_chip):
    s = qkv.shape[0] - Z_PAD
    nb = s // Q_BLOCK
    n = len(to_chip)
    pairs = N_HEADS // 2

    def body(*refs):
        q_ref, k_ref, v_ref, bt_ref, do_ref = refs[:5]
        to_chip_refs = refs[5:5 + n]
        dqkv_ref, db_ref = refs[5 + n:7 + n]
        from_chip_refs = refs[7 + n:7 + 2 * n]
        dk_acc, dv_acc, s_scr, dp_scr, send_sems, recv_sems = refs[7 + 2 * n:]
        p_id = pl.program_id(0)

        @pl.when(p_id == 0)
        def _():
            for cp in _owner_copies(to_chip_refs, from_chip_refs, send_sems, recv_sems):
                cp.start()

        dk_acc[...] = jnp.zeros(dk_acc.shape, F32)
        dv_acc[...] = jnp.zeros(dv_acc.shape, F32)
        db_ref[...] = jnp.zeros(db_ref.shape, F32)
        masks = _head_masks()

        def operands(b):
            r0 = pl.multiple_of(b * Q_BLOCK, Q_BLOCK)
            q2 = _stack_heads(q_ref[pl.ds(r0 + Z_PAD, Q_BLOCK), :], masks)
            do2 = _stack_heads(do_ref[pl.ds(r0, Q_BLOCK), :], masks)
            return r0, q2, do2, k_ref[pl.ds(r0, K_SPAN), :]

        def ahead(b, half):
            r0, q2, do2, kcat = operands(b)
            s_scr[half] = _dot(q2, kcat, NT)
            dp_scr[half] = _dot(do2, v_ref[pl.ds(r0, K_SPAN), :], NT)

        def finish(b, half):
            r0, q2, do2, kcat = operands(b)
            p_strips, ds_strips = [], []
            for t, p in enumerate(_softmax_strips(s_scr.at[half], bt_ref, b)):
                hh, r = divmod(t * STRIP, Q_BLOCK)
                dp_t = dp_scr[half, t * STRIP:(t + 1) * STRIP, :]
                ds = p * (dp_t - jnp.sum(p * dp_t, axis=-1, keepdims=True))
                db_ref[hh, r:r + STRIP, :] += ds
                p_strips.append(p.astype(BF16))
                ds_strips.append(ds.astype(BF16))
            dq = _dot(_side_by_side_strips(ds_strips), _stack_heads(kcat, masks))
            dqkv_ref[0, pl.ds(r0, Q_BLOCK), :] = (dq * Q_SCALE).astype(BF16)
            dk_acc[pl.ds(r0, K_SPAN), :] += _dot(jnp.concatenate(ds_strips, axis=0), q2, TN)
            dv_acc[pl.ds(r0, K_SPAN), :] += _dot(jnp.concatenate(p_strips, axis=0), do2, TN)

        def two_blocks(i, carry):
            b = 2 * i
            ahead(b + 1, 1)
            finish(b, 0)
            ahead(jnp.minimum(b + 2, nb - 1), 0)
            finish(b + 1, 1)
            return carry

        ahead(0, 0)
        lax.fori_loop(0, nb // 2, two_blocks, 0)
        dqkv_ref[1] = dk_acc[Z_PAD:, :].astype(BF16)
        dqkv_ref[2] = dv_acc[Z_PAD:, :].astype(BF16)

        @pl.when(p_id == pairs - 1)
        def _():
            for cp in _owner_copies(to_chip_refs, from_chip_refs, send_sems, recv_sems):
                cp.wait_recv()
                cp.wait_send()

    hbm = pl.BlockSpec(memory_space=pl.ANY)
    return pl.pallas_call(
        body, name="attn_bwd",
        grid=(pairs,),
        in_specs=_attn_specs(s + Z_PAD) + [pl.BlockSpec((s, 128), lambda p: (0, p))] + [hbm] * n,
        out_specs=(pl.BlockSpec((3, s, 128), lambda p: (0, 0, p)),
                   pl.BlockSpec((2, Q_BLOCK, K_SPAN), lambda p: (p, 0, 0))) + (hbm,) * n,
        out_shape=(jax.ShapeDtypeStruct((3, s, D_A), BF16),
                   jax.ShapeDtypeStruct((N_HEADS, Q_BLOCK, K_SPAN), F32))
        + tuple(jax.ShapeDtypeStruct(t.shape, t.dtype) for t in to_chip),
        scratch_shapes=[pltpu.VMEM((s + Z_PAD, 128), F32), pltpu.VMEM((s + Z_PAD, 128), F32),
                        pltpu.VMEM((2, 2 * Q_BLOCK, K_SPAN), F32), pltpu.VMEM((2, 2 * Q_BLOCK, K_SPAN), F32),
                        pltpu.SemaphoreType.DMA((n, 3)), pltpu.SemaphoreType.DMA((n, 3))],
        compiler_params=_params(56),
    )(qkv, qkv, qkv, bias_table, d_out, *to_chip)


def _mid_fwd_bwd(x, target, attn_out, z, w_pa, w_pb, w_out, b_gate, ln_g, ln_b, w_s, b_s, final_g):
    s = x.shape[0]
    tm = TOKEN_TILE
    nt = s // tm
    n_sub = tm // SGU_CHUNK

    def body(x_ref, t_ref, oa_ref, ga_ref, ub_ref, vb_ref, gb_ref, ta0_ref, ta1_ref, tb0_ref, tb1_ref,
             wpa_hbm, wpb_hbm, wout_hbm, bg_ref, lng_ref, lnb_ref, ws_ref, bs_ref, fg_ref,
             dx2_ref, doa_ref, dz_ref, dwout_hbm, dwpa_hbm, dwpb_hbm, dbg_ref, dfg_ref, dlng_ref, dlnb_ref, dws_ref,
             dbs_ref, loss_ref,
             wpa, wpb, wout, wmix, acc_out, acc_pa, acc_pb, mixed_s, dvn_s, sem):
        i = pl.program_id(0)

        @pl.when(i == 0)
        def _():
            loads = [pltpu.make_async_copy(src, dst, sem.at[n])
                     for n, (src, dst) in enumerate(((wpa_hbm, wpa), (wpb_hbm, wpb), (wout_hbm, wout)))]
            for cp in loads:
                cp.start()
            t_idx = lax.broadcasted_iota(jnp.int32, (SGU_CHUNK, SGU_CHUNK), 0)
            s_idx = lax.broadcasted_iota(jnp.int32, (SGU_CHUNK, SGU_CHUNK), 1)
            for g in range(N_GROUPS):
                wmix[g] = jnp.where(s_idx <= t_idx, ws_ref[g], 0.0).astype(BF16)
            for ref in (acc_out, acc_pa, acc_pb, dbg_ref, dfg_ref, dlng_ref, dlnb_ref, dws_ref, dbs_ref, loss_ref):
                ref[...] = jnp.zeros(ref.shape, F32)
            for cp in loads:
                cp.wait()

        g_a = ga_ref[...].astype(F32)
        u_b = ub_ref[...].astype(F32)
        v_b = vb_ref[...].astype(F32)
        g_b = gb_ref[...].astype(F32)
        bg = bg_ref[...]

        sg_a = _sigmoid(g_a)
        silu_a = g_a * sg_a
        o_a = oa_ref[...]
        y_a = (o_a * silu_a).astype(BF16)

        ug, dgelu_u = _gelu_and_grad(u_b)
        vg, dgelu_v = _gelu_and_grad(v_b)
        mu = jnp.mean(vg, axis=-1, keepdims=True)
        vc = vg - mu
        rstd = lax.rsqrt(jnp.mean(vc * vc, axis=-1, keepdims=True) + EPS)
        vhat = vc * rstd
        lng = lng_ref[...]
        vn = (vhat * lng + lnb_ref[...]).astype(BF16)
        for n in range(n_sub):
            rows = slice(n * SGU_CHUNK, (n + 1) * SGU_CHUNK)
            for g in range(N_GROUPS):
                cols = slice(g * 128, (g + 1) * 128)
                mixed_s[rows, cols] = _dot(wmix[g], vn[rows, cols]) + bs_ref[g]
        mixed = mixed_s[...]
        sg_b = _sigmoid(g_b)
        silu_b = g_b * sg_b
        um = ug * mixed
        y_b = (um * silu_b).astype(BF16)

        p_a = _dot(y_a, wpa[...])
        p_b = _dot(y_b, wpb[...])
        gate_a = _sigmoid(jnp.concatenate([ta0_ref[...], ta1_ref[...]], axis=1).astype(F32) + bg[:, :D_MODEL])
        gate_b = _sigmoid(jnp.concatenate([tb0_ref[...], tb1_ref[...]], axis=1).astype(F32) + bg[:, D_MODEL:])
        merged = (gate_a * p_a + gate_b * p_b).astype(BF16)
        x2 = x_ref[...] + _dot(merged, wout[...])
        r2 = lax.rsqrt(jnp.mean(x2 * x2, axis=-1, keepdims=True) + EPS)
        xh = x2 * r2
        fg = fg_ref[...]
        err = xh * fg - t_ref[...]
        loss_ref[...] += jnp.sum(jnp.sum(err * err, axis=-1, keepdims=True), axis=0, keepdims=True) * (0.5 / D_MODEL)

        dy = err * (1.0 / D_MODEL)
        dfg_ref[...] += jnp.sum(dy * xh, axis=0, keepdims=True)
        gy = dy * fg
        dx2 = r2 * (gy - xh * jnp.mean(gy * xh, axis=-1, keepdims=True))
        dx2_ref[...] = dx2
        dx2b = dx2.astype(BF16)
        dmerged = _dot(dx2b, wout[...], NT)
        acc_out[...] += _dot(merged, dx2b, TN)

        dp_a = dmerged * gate_a
        dp_b = dmerged * gate_b
        dgate_a = dp_a * p_a * (1.0 - gate_a)
        dgate_b = dp_b * p_b * (1.0 - gate_b)
        dbg_ref[:, :D_MODEL] += jnp.sum(dgate_a, axis=0, keepdims=True)
        dbg_ref[:, D_MODEL:] += jnp.sum(dgate_b, axis=0, keepdims=True)
        dz_ref[:, 2048:3072] = dgate_a.astype(BF16)
        dz_ref[:, 3072:4096] = dgate_b.astype(BF16)
        dp_ab = dp_a.astype(BF16)
        dp_bb = dp_b.astype(BF16)
        dy_a = _dot(dp_ab, wpa[...], NT)
        dy_b = _dot(dp_bb, wpb[...], NT)
        acc_pa[...] += _dot(y_a, dp_ab, TN)
        acc_pb[...] += _dot(y_b, dp_bb, TN)

        doa_ref[...] = (dy_a * silu_a).astype(BF16)
        dz_ref[:, 0:512] = (dy_a * o_a * (sg_a * (1.0 + g_a * (1.0 - sg_a)))).astype(BF16)
        dz_ref[:, 1536:2048] = (dy_b * um * (sg_b * (1.0 + g_b * (1.0 - sg_b)))).astype(BF16)
        dys = dy_b * silu_b
        dz_ref[:, 512:1024] = (dys * mixed * dgelu_u).astype(BF16)
        dmixed = dys * ug
        dmb = dmixed.astype(BF16)
        for n in range(n_sub):
            rows = slice(n * SGU_CHUNK, (n + 1) * SGU_CHUNK)
            for g in range(N_GROUPS):
                cols = slice(g * 128, (g + 1) * 128)
                dws_ref[g] += _dot(dmb[rows, cols], vn[rows, cols], NT)
                dbs_ref[g] += jnp.sum(dmixed[rows, cols], axis=-1, keepdims=True)
                dvn_s[rows, cols] = _dot(wmix[g], dmb[rows, cols], TN)
        dvn = dvn_s[...]
        dlng_ref[...] += jnp.sum(dvn * vhat, axis=0, keepdims=True)
        dlnb_ref[...] += jnp.sum(dvn, axis=0, keepdims=True)
        dvh = dvn * lng
        dvg = rstd * (dvh - jnp.mean(dvh, axis=-1, keepdims=True) - vhat * jnp.mean(dvh * vhat, axis=-1, keepdims=True))
        dz_ref[:, 1024:1536] = (dvg * dgelu_v).astype(BF16)

        @pl.when(i == nt - 1)
        def _():
            t_idx = lax.broadcasted_iota(jnp.int32, (SGU_CHUNK, SGU_CHUNK), 0)
            s_idx = lax.broadcasted_iota(jnp.int32, (SGU_CHUNK, SGU_CHUNK), 1)
            for g in range(N_GROUPS):
                dws_ref[g] = jnp.where(s_idx <= t_idx, dws_ref[g], 0.0)
            stores = [pltpu.make_async_copy(src, dst, sem.at[n])
                      for n, (src, dst) in enumerate(((acc_out, dwout_hbm), (acc_pa, dwpa_hbm), (acc_pb, dwpb_hbm)))]
            for cp in stores:
                cp.start()
            for cp in stores:
                cp.wait()

    tile = lambda w: pl.BlockSpec((tm, w), lambda i: (i, 0))
    whole = lambda shape: pl.BlockSpec(shape, lambda i: (0,) * len(shape))
    hbm = pl.BlockSpec(memory_space=pl.ANY)
    return pl.pallas_call(
        body, name="mid_fwd_bwd",
        grid=(nt,),
        in_specs=[tile(D_MODEL), tile(D_MODEL), tile(D_A)]
        + [pl.BlockSpec((tm, COL_BLOCK), functools.partial(lambda c, i: (i + Z_PAD // tm, c), c))
           for c in range(3, N_COL_BLOCKS)]
        + [hbm, hbm, hbm,
                  whole((1, 2 * D_MODEL)), whole((1, D_B)), whole((1, D_B)),
                  whole((N_GROUPS, SGU_CHUNK, SGU_CHUNK)), whole((N_GROUPS, SGU_CHUNK, 1)), whole((1, D_MODEL))],
        out_specs=(tile(D_MODEL), tile(D_A), tile(REST), hbm, hbm, hbm,
                   whole((1, 2 * D_MODEL)), whole((1, D_MODEL)), whole((1, D_B)), whole((1, D_B)),
                   whole((N_GROUPS, SGU_CHUNK, SGU_CHUNK)), whole((N_GROUPS, SGU_CHUNK, 1)), whole((1, 1))),
        out_shape=(jax.ShapeDtypeStruct((s, D_MODEL), F32), jax.ShapeDtypeStruct((s, D_A), BF16),
                   jax.ShapeDtypeStruct((s, REST), BF16),
                   jax.ShapeDtypeStruct((D_MODEL, D_MODEL), F32), jax.ShapeDtypeStruct((D_A, D_MODEL), F32),
                   jax.ShapeDtypeStruct((D_B, D_MODEL), F32),
                   jax.ShapeDtypeStruct((1, 2 * D_MODEL), F32), jax.ShapeDtypeStruct((1, D_MODEL), F32),
                   jax.ShapeDtypeStruct((1, D_B), F32), jax.ShapeDtypeStruct((1, D_B), F32),
                   jax.ShapeDtypeStruct((N_GROUPS, SGU_CHUNK, SGU_CHUNK), F32),
                   jax.ShapeDtypeStruct((N_GROUPS, SGU_CHUNK, 1), F32), jax.ShapeDtypeStruct((1, 1), F32)),
        scratch_shapes=[pltpu.VMEM((D_A, D_MODEL), BF16), pltpu.VMEM((D_B, D_MODEL), BF16),
                        pltpu.VMEM((D_MODEL, D_MODEL), BF16), pltpu.VMEM((N_GROUPS, SGU_CHUNK, SGU_CHUNK), BF16),
                        pltpu.VMEM((D_MODEL, D_MODEL), F32), pltpu.VMEM((D_A, D_MODEL), F32),
                        pltpu.VMEM((D_B, D_MODEL), F32),
                        pltpu.VMEM((tm, D_B), F32), pltpu.VMEM((tm, D_B), F32),
                        pltpu.SemaphoreType.DMA((3,))],
        compiler_params=_params(56),
    )(x, target, attn_out, *([z] * (N_COL_BLOCKS - 3)), w_pa, w_pb, w_out, b_gate, ln_g, ln_b, w_s, b_s, final_g)


def _proj_bwd_x(dqkv, drest, x, dx2, norm_g, w_in_t, to_chip):
    s = x.shape[0]
    tm = TOKEN_TILE
    nt = s // tm
    n = len(to_chip)

    def body(*refs):
        dqkv_ref, dr_ref, x_ref, dx2_ref, g_ref, w_hbm = refs[:6]
        to_chip_refs = refs[6:6 + n]
        dx_ref, dg_ref = refs[6 + n:8 + n]
        from_chip_refs = refs[8 + n:8 + 2 * n]
        w, sem, send_sems, recv_sems = refs[8 + 2 * n:]
        i = pl.program_id(0)

        @pl.when(i == 0)
        def _():
            for rc in _owner_copies(to_chip_refs, from_chip_refs, send_sems, recv_sems):
                rc.start()
            cp = pltpu.make_async_copy(w_hbm, w, sem)
            cp.start()
            dg_ref[...] = jnp.zeros(dg_ref.shape, F32)
            cp.wait()

        dh = None
        for c in range(N_COL_BLOCKS):
            dz = dqkv_ref[c] if c < 3 else dr_ref[:, (c - 3) * COL_BLOCK:(c - 2) * COL_BLOCK]
            part = _dot(dz, w[c * COL_BLOCK:(c + 1) * COL_BLOCK, :])
            dh = part if dh is None else dh + part
        xf = x_ref[...]
        r = lax.rsqrt(jnp.mean(xf * xf, axis=-1, keepdims=True) + EPS)
        xn = xf * r
        dg_ref[...] += jnp.sum(dh * xn, axis=0, keepdims=True)
        gh = dh * g_ref[...]
        dx_ref[...] = r * (gh - xn * jnp.mean(gh * xn, axis=-1, keepdims=True)) + dx2_ref[...]

        @pl.when(i == nt - 1)
        def _():
            for rc in _owner_copies(to_chip_refs, from_chip_refs, send_sems, recv_sems):
                rc.wait_recv()
                rc.wait_send()

    hbm = pl.BlockSpec(memory_space=pl.ANY)
    return pl.pallas_call(
        body, name="proj_bwd_x",
        grid=(nt,),
        in_specs=[pl.BlockSpec((3, tm, D_A), lambda i: (0, i, 0)),
                  pl.BlockSpec((tm, REST), lambda i: (i, 0)),
                  pl.BlockSpec((tm, D_MODEL), lambda i: (i, 0)),
                  pl.BlockSpec((tm, D_MODEL), lambda i: (i, 0)),
                  pl.BlockSpec((1, D_MODEL), lambda i: (0, 0)),
                  hbm] + [hbm] * n,
        out_specs=(pl.BlockSpec((tm, D_MODEL), lambda i: (i, 0)),
                   pl.BlockSpec((1, D_MODEL), lambda i: (0, 0))) + (hbm,) * n,
        out_shape=(jax.ShapeDtypeStruct((s, D_MODEL), F32), jax.ShapeDtypeStruct((1, D_MODEL), F32))
        + tuple(jax.ShapeDtypeStruct(t.shape, t.dtype) for t in to_chip),
        scratch_shapes=[pltpu.VMEM((D_IN, D_MODEL), BF16), pltpu.SemaphoreType.DMA,
                        pltpu.SemaphoreType.DMA((n, 3)), pltpu.SemaphoreType.DMA((n, 3))],
        compiler_params=_params(48),
    )(dqkv, drest, x, dx2, norm_g, w_in_t, *to_chip)


def _proj_bwd_w(h, dqkv, drest):
    s = h.shape[0]
    tk = min(s, 1024)
    nk = s // tk

    def body(h_ref, dqkv_ref, dr_ref, o_ref, acc):
        j = pl.program_id(0)
        i = pl.program_id(1)

        @pl.when(i == 0)
        def _():
            acc[...] = jnp.zeros(acc.shape, F32)

        @pl.when(j < 3)
        def _():
            acc[...] += _dot(dqkv_ref[...], h_ref[...], TN)

        @pl.when(j >= 3)
        def _():
            acc[...] += _dot(dr_ref[...], h_ref[...], TN)

        @pl.when(i == nk - 1)
        def _():
            o_ref[...] = acc[...].astype(BF16)

    return pl.pallas_call(
        body, name="proj_bwd_w",
        grid=(N_COL_BLOCKS, nk),
        in_specs=[pl.BlockSpec((tk, D_MODEL), lambda j, i: (i, 0)),
                  pl.BlockSpec((None, tk, COL_BLOCK),
                               lambda j, i: (jnp.minimum(j, 2), jnp.where(j < 3, i, nk - 1), 0)),
                  pl.BlockSpec((tk, COL_BLOCK),
                               lambda j, i: (jnp.where(j >= 3, i, 0), jnp.maximum(j - 3, 0)))],
        out_specs=pl.BlockSpec((COL_BLOCK, D_MODEL), lambda j, i: (j, 0)),
        out_shape=jax.ShapeDtypeStruct((D_IN, D_MODEL), BF16),
        scratch_shapes=[pltpu.VMEM((COL_BLOCK, D_MODEL), F32)],
        compiler_params=_params(40),
    )(h, dqkv, drest)


def _adamw_math(w, g, m, v):
    c1 = 1.0 - ADAM_B1 ** ADAM_STEP
    c2 = 1.0 - ADAM_B2 ** ADAM_STEP
    nm = ADAM_B1 * m + (1.0 - ADAM_B1) * g
    nv = ADAM_B2 * v + (1.0 - ADAM_B2) * (g * g)
    return -ADAM_LR * ((nm / c1) / (jnp.sqrt(nv / c2) + ADAM_EPS) + ADAM_WD * w), nm, nv


def _adamw(name, w, g, m, v, from_chip):
    rows, cols = w.shape
    tr = rows if rows * cols <= 512 * 1024 else next(t for t in range(256, 7, -8) if rows % t == 0)

    def body(w_ref, g_ref, m_ref, v_ref, t_ref, g_out, d_ref, nm_ref, nv_ref):
        gg = g_ref[...]
        for j in range(3):
            gg = gg + t_ref[j].astype(F32)
        g_out[...] = gg
        d_ref[...], nm_ref[...], nv_ref[...] = _adamw_math(w_ref[...], gg, m_ref[...], v_ref[...])

    spec = pl.BlockSpec((tr, cols), lambda i: (i, 0))
    shape = jax.ShapeDtypeStruct((rows, cols), F32)
    return pl.pallas_call(
        body, name=name,
        grid=(rows // tr,),
        in_specs=[spec] * 4 + [pl.BlockSpec((3, tr, cols), lambda i: (0, i, 0))],
        out_specs=(spec,) * 4, out_shape=(shape,) * 4,
        compiler_params=_params(32),
    )(w, g, m, v, from_chip)


_SMALL = (("norm_g", (1, D_MODEL)), ("b_gate", (1, 2 * D_MODEL)), ("rel_bias", (N_HEADS, N_REL)),
          ("sgu_ln_g", (1, D_B)), ("sgu_ln_b", (1, D_B)), ("w_s", (N_GROUPS * SGU_CHUNK, SGU_CHUNK)),
          ("b_s", (N_GROUPS, SGU_CHUNK)), ("final_g", (1, D_MODEL)))


def _adamw_small(slab, g_ws, weights, moments_m, moments_v):
    k = len(_SMALL)

    def grad_of(name, slab_ref, ws_ref):
        if name == "norm_g":
            return slab_ref[ROW_NORM_G:ROW_NORM_G + 1, :]
        if name == "b_gate":
            return jnp.concatenate([slab_ref[ROW_B_GATE:ROW_B_GATE + 1, :], slab_ref[ROW_B_GATE + 1:ROW_B_GATE + 2, :]],
                                   axis=1)
        if name == "rel_bias":
            return slab_ref[ROW_REL:ROW_REL + N_HEADS, :N_REL]
        if name == "sgu_ln_g":
            return slab_ref[ROW_LN_G:ROW_LN_G + 1, :D_B]
        if name == "sgu_ln_b":
            return slab_ref[ROW_LN_B:ROW_LN_B + 1, :D_B]
        if name == "w_s":
            return ws_ref[...]
        if name == "b_s":
            return slab_ref[ROW_B_S:ROW_B_S + N_GROUPS, :SGU_CHUNK]
        return slab_ref[ROW_FINAL_G:ROW_FINAL_G + 1, :]

    def body(*refs):
        slab_ref, ws_ref = refs[:2]
        w_refs, m_refs, v_refs = refs[2:2 + k], refs[2 + k:2 + 2 * k], refs[2 + 2 * k:2 + 3 * k]
        outs = refs[2 + 3 * k:]
        for n, (name, _) in enumerate(_SMALL):
            g = grad_of(name, slab_ref, ws_ref)
            outs[n][...] = g
            outs[k + n][...], outs[2 * k + n][...], outs[3 * k + n][...] = _adamw_math(
                w_refs[n][...], g, m_refs[n][...], v_refs[n][...])
        outs[4 * k][...] = slab_ref[ROW_LOSS:ROW_LOSS + 1, :1]

    vmem = pl.BlockSpec(memory_space=pltpu.VMEM)
    shapes = tuple(jax.ShapeDtypeStruct(shape, F32) for _, shape in _SMALL)
    return pl.pallas_call(
        body, name="adamw_small",
        out_shape=shapes * 4 + (jax.ShapeDtypeStruct((1, 1), F32),),
        in_specs=[vmem] * (2 + 3 * k), out_specs=tuple([vmem] * (4 * k + 1)),
        compiler_params=_params(16),
    )(slab, g_ws, *weights, *moments_m, *moments_v)


def _pad_rel(a):
    return jnp.pad(a.reshape(N_HEADS, N_REL), ((0, 0), (0, N_REL_PAD - N_REL)))


def kernel(x, norm_g, w_in, b_gate, rel_bias, sgu_ln_g, sgu_ln_b, w_s, b_s, w_pa, w_pb, w_out, final_g, loss_target, m_norm_g, m_w_in, m_b_gate, m_rel_bias, m_sgu_ln_g, m_sgu_ln_b, m_w_s, m_b_s, m_w_pa, m_w_pb, m_w_out, m_final_g, v_norm_g, v_w_in, v_b_gate, v_rel_bias, v_sgu_ln_g, v_sgu_ln_b, v_w_s, v_b_s, v_w_pa, v_w_pb, v_w_out, v_final_g):
    s = x.shape[1]
    xs = x.reshape(s, D_MODEL)
    tgt = loss_target.reshape(s, D_MODEL)

    bias_table = _bias_table(_pad_rel(rel_bias))
    w_in_t = jnp.swapaxes(w_in[0], 0, 1)
    qkv, h, w_in_t_full = _gather_proj_fwd(xs, norm_g, w_in_t)
    attn_out, g_pa, g_pb, g_out = _attn_fwd(qkv, bias_table, (w_pa[0], w_pb[0], w_out[0]))
    w_pa_full = jnp.transpose(g_pa, (1, 0, 2)).reshape(D_A, D_MODEL)
    w_pb_full = jnp.transpose(g_pb, (1, 0, 2)).reshape(D_B, D_MODEL)
    w_out_full = g_out.reshape(D_MODEL, D_MODEL)

    (dx2, d_attn, drest, dw_out, dw_pa, dw_pb, d_bgate, d_fg, d_lng, d_lnb, d_ws, d_bs, loss_part) = _mid_fwd_bwd(
        xs, tgt, attn_out, qkv, w_pa_full, w_pb_full, w_out_full, b_gate, sgu_ln_g, sgu_ln_b, w_s[0],
        b_s.reshape(N_GROUPS, SGU_CHUNK, 1), final_g.reshape(1, D_MODEL))

    own_pa, own_pb, own_out, tc_pa, tc_pb, tc_out = _reduce_chip(
        "reduce_chip_proj", (dw_pa, dw_pb, dw_out), (1, 1, 0))
    dqkv, dbias, fc_pa, fc_pb, fc_out = _attn_bwd(qkv, bias_table, d_attn, (tc_pa, tc_pb, tc_out))
    d_rel = _bias_grad(dbias)
    dw_in_t = _proj_bwd_w(h, dqkv, drest)
    own_in, tc_in = _reduce_chip("reduce_chip_in", (dw_in_t,), (0,))
    grad_x, d_ng, fc_in = _proj_bwd_x(dqkv, drest, xs, dx2, norm_g, w_in_t_full, (tc_in,))
    big = {"w_in": tuple(jnp.swapaxes(t, 0, 1)[None] for t in _adamw(
        "adamw_w_in", w_in_t, own_in, jnp.swapaxes(m_w_in[0], 0, 1), jnp.swapaxes(v_w_in[0], 0, 1), fc_in))}
    for name, w, g, fc, m, v in (("w_pa", w_pa, own_pa, fc_pa, m_w_pa, v_w_pa),
                                 ("w_pb", w_pb, own_pb, fc_pb, m_w_pb, v_w_pb),
                                 ("w_out", w_out, own_out, fc_out, m_w_out, v_w_out)):
        big[name] = tuple(t[None] for t in _adamw("adamw_" + name, w[0], g, m[0], v[0], fc))

    slab, g_ws = _reduce_small(d_ng, d_bgate, d_rel, d_lng, d_lnb, d_fg, loss_part, d_bs, d_ws)
    as_2d = lambda leaves: [a.reshape(shape) for a, (_, shape) in zip(leaves, _SMALL)]
    small_out = _adamw_small(
        slab, g_ws, as_2d((norm_g, b_gate, rel_bias, sgu_ln_g, sgu_ln_b, w_s, b_s, final_g)),
        as_2d((m_norm_g, m_b_gate, m_rel_bias, m_sgu_ln_g, m_sgu_ln_b, m_w_s, m_b_s, m_final_g)),
        as_2d((v_norm_g, v_b_gate, v_rel_bias, v_sgu_ln_g, v_sgu_ln_b, v_w_s, v_b_s, v_final_g)))
    small_index = {name: n for n, (name, _) in enumerate(_SMALL)}

    def leaf(kind, name, like):
        if name in big:
            return big[name][kind]
        return small_out[kind * len(_SMALL) + small_index[name]].reshape(like.shape)

    weights = (("norm_g", norm_g), ("w_in", w_in), ("b_gate", b_gate), ("rel_bias", rel_bias), ("sgu_ln_g", sgu_ln_g),
               ("sgu_ln_b", sgu_ln_b), ("w_s", w_s), ("b_s", b_s), ("w_pa", w_pa), ("w_pb", w_pb), ("w_out", w_out),
               ("final_g", final_g))
    outs = [small_out[-1].reshape(()), grad_x.reshape(x.shape)]
    for kind in range(4):
        outs.extend(leaf(kind, name, like) for name, like in weights)
    return tuple(outs)
```

```python
import functools
import math

import jax
import jax.numpy as jnp
from jax import lax
from jax.experimental import pallas as pl
from jax.experimental.pallas import tpu as pltpu

F32 = jnp.float32
BF16 = jnp.bfloat16
MESH = pl.DeviceIdType.MESH
N_DEV = 8

D_MODEL = 1024
D_A = 512
D_B = 512
D_IN = 5632
N_HEADS = 8
HEAD_DIM = 64
CHUNK = 64
N_PREV = 8
REL_CLIP = 128
N_REL = 2 * REL_CLIP + 1
N_REL_PAD = 384
SGU_CHUNK = 128
N_GROUPS = 4
EPS = 1e-6
NEG_INF = -1e30
Q_SCALE = HEAD_DIM ** -0.5

Q_BLOCK = 256
K_SPAN = 768
Z_PAD = K_SPAN - Q_BLOCK
ROLL_W = 1024
COL_BLOCK = 512
N_COL_BLOCKS = D_IN // COL_BLOCK
REST = D_IN - 3 * D_A
TOKEN_TILE = 256
V7X_VMEM_BYTES = 64 * 1024 * 1024

ADAM_LR = 0.001
ADAM_B1 = 0.9
ADAM_B2 = 0.999
ADAM_EPS = 1e-08
ADAM_WD = 0.01
ADAM_STEP = 10

GELU_C = math.sqrt(2.0 / math.pi)
GELU_A = 0.044715

NT = (((1,), (1,)), ((), ()))
TN = (((0,), (0,)), ((), ()))
HIGHEST = lax.Precision.HIGHEST


def _params(vmem_mb, **kw):
    return pltpu.CompilerParams(vmem_limit_bytes=vmem_mb * 1024 * 1024, **kw)


def _dot(a, b, dims=None):
    if dims is None:
        return jnp.dot(a, b, preferred_element_type=F32)
    return lax.dot_general(a, b, dims, preferred_element_type=F32)


def _sigmoid(x):
    return 1.0 / (1.0 + jnp.exp(-x))


def _gelu_and_grad(u):
    u2 = u * u
    t = jnp.tanh(GELU_C * (u + GELU_A * u * u2))
    half = 0.5 * (1.0 + t)
    g = u * half
    dg = half + 0.5 * u * (1.0 - t * t) * (GELU_C * (1.0 + 3.0 * GELU_A * u2))
    return g, dg


def _my_pos():
    return lax.axis_index("x"), lax.axis_index("y"), lax.axis_index("c")


def _flat_id(pos):
    return 4 * pos[0] + 2 * pos[1] + pos[2]


def _peer(pos, k):
    x, y, c = pos
    return (1 - x if k & 4 else x, 1 - y if k & 2 else y, 1 - c if k & 1 else c)


def _other_chips(pos):
    x, y, _ = pos
    return ((1 - x, y), (x, 1 - y), (1 - x, 1 - y))


class _SlotGather:
    def __init__(self, bufs, send_sems, recv_sems, own=None):
        self.bufs, self.send_sems, self.recv_sems = bufs, send_sems, recv_sems
        self.own = own if own is not None else [None] * len(bufs)
        x, y, c = _my_pos()
        self.c, self.me, self.sib = c, (x, y, c), (x, y, 1 - c)
        self.chips = _other_chips(self.me)

    def _copy(self, a, k, block, to):
        slot = _flat_id(block)
        src = self.own[a] if (k < 4 and self.own[a] is not None) else self.bufs[a].at[slot]
        return pltpu.make_async_remote_copy(
            src_ref=src, dst_ref=self.bufs[a].at[slot],
            send_sem=self.send_sems.at[a, k], recv_sem=self.recv_sems.at[a, k], device_id=to, device_id_type=MESH)

    def _own_sends(self):
        n = len(self.bufs)
        return ([self._copy(a, 1 + j, self.me, (*chip, self.c)) for j, chip in enumerate(self.chips) for a in range(n)]
                + [self._copy(a, 0, self.me, self.sib) for a in range(n)])

    def _passes(self):
        return [self._copy(a, 4 + j, (*chip, self.c), self.sib)
                for j, chip in enumerate(self.chips) for a in range(len(self.bufs))]

    def start(self):
        for cp in self._own_sends():
            cp.start()

    def pass_on(self):
        for j, chip in enumerate(self.chips):
            for a in range(len(self.bufs)):
                self._copy(a, 1 + j, (*chip, self.c), self.me).wait_recv()
                self._copy(a, 4 + j, (*chip, self.c), self.sib).start()

    def finish(self):
        for a in range(len(self.bufs)):
            self._copy(a, 0, self.sib, self.me).wait_recv()
            for j, chip in enumerate(self.chips):
                self._copy(a, 4 + j, (*chip, 1 - self.c), self.me).wait_recv()
        for cp in self._own_sends() + self._passes():
            cp.wait_send()


def _reduce_chip(name, parts, sharded_dim):
    n = len(parts)
    shapes = []
    for p, dim in zip(parts, sharded_dim):
        shape = list(p.shape)
        shape[dim] //= N_DEV
        shapes.append(tuple(shape))

    def body(*refs):
        full, own, to_chip = refs[:n], refs[n:2 * n], refs[2 * n:3 * n]
        ins, from_sib = refs[3 * n:4 * n], refs[4 * n:5 * n]
        send_sems, recv_sems = refs[5 * n], refs[5 * n + 1]
        x, y, c = _my_pos()
        sib = (x, y, 1 - c)
        chips = ((x, y),) + _other_chips((x, y, c))
        for a in range(n):
            rows, cols = shapes[a]
            for d in range(N_DEV):
                if sharded_dim[a] == 0:
                    ins[a][d] = full[a][d * rows:(d + 1) * rows, :].astype(BF16)
                else:
                    ins[a][d] = full[a][:, d * cols:(d + 1) * cols].astype(BF16)

        def to_sibling(a, r):
            return pltpu.make_async_remote_copy(
                src_ref=ins[a].at[_flat_id((*chips[r], 1 - c))], dst_ref=from_sib[a].at[r],
                send_sem=send_sems.at[a, r], recv_sem=recv_sems.at[a, r], device_id=sib, device_id_type=MESH)

        sends = [to_sibling(a, r) for r in (1, 2, 3, 0) for a in range(n)]
        for cp in sends:
            cp.start()
        for r in (1, 2, 3, 0):
            for a in range(n):
                to_sibling(a, r).wait_recv()
                both = ins[a][_flat_id((*chips[r], c))].astype(F32) + from_sib[a][r].astype(F32)
                if r == 0:
                    own[a][...] = both
                else:
                    to_chip[a][r - 1] = both.astype(BF16)
        for cp in sends:
            cp.wait_send()

    vmem = pl.BlockSpec(memory_space=pltpu.VMEM)
    return pl.pallas_call(
        body, name=name,
        out_shape=tuple(jax.ShapeDtypeStruct(sh, F32) for sh in shapes)
        + tuple(jax.ShapeDtypeStruct((3,) + sh, BF16) for sh in shapes),
        in_specs=[vmem] * n, out_specs=tuple([vmem] * (2 * n)),
        scratch_shapes=[pltpu.VMEM((N_DEV,) + sh, BF16) for sh in shapes]
        + [pltpu.VMEM((4,) + sh, BF16) for sh in shapes]
        + [pltpu.SemaphoreType.DMA((n, 4)), pltpu.SemaphoreType.DMA((n, 4))],
        compiler_params=_params(56),
    )(*parts)


def _owner_copies(to_chip, from_chip, send_sems, recv_sems):
    x, y, c = _my_pos()
    return [pltpu.make_async_remote_copy(
        src_ref=to_chip[a].at[j], dst_ref=from_chip[a].at[j],
        send_sem=send_sems.at[a, j], recv_sem=recv_sems.at[a, j], device_id=(*chip, c), device_id_type=MESH)
        for a in range(len(to_chip)) for j, chip in enumerate(_other_chips((x, y, c)))]


ROW_NORM_G, ROW_B_GATE, ROW_LN_G, ROW_LN_B, ROW_FINAL_G, ROW_LOSS, ROW_REL, ROW_B_S, SLAB_ROWS = 0, 1, 3, 4, 5, 6, 8, 16, 24


def _rel_index(e):
    lo, hi = Z_PAD - REL_CLIP, Z_PAD + REL_CLIP
    return jnp.where(e <= lo, 2 * REL_CLIP, jnp.where(e < hi, hi - e, jnp.where(e <= K_SPAN, 0, 2 * REL_CLIP)))


def _bias_table(rel_bias_pad):
    def body(rb_ref, bt_ref):
        c = lax.broadcasted_iota(jnp.int32, (N_REL_PAD, ROLL_W), 1)
        r = lax.broadcasted_iota(jnp.int32, (N_REL_PAD, ROLL_W), 0)
        pick = (r == _rel_index(c)).astype(F32)
        rows = jnp.dot(rb_ref[...], pick, precision=HIGHEST, preferred_element_type=F32)
        qc = lax.broadcasted_iota(jnp.int32, (Q_BLOCK, K_SPAN), 0) >> 6
        kc = lax.broadcasted_iota(jnp.int32, (Q_BLOCK, K_SPAN), 1) >> 6
        band = (kc >= qc) & (kc <= qc + N_PREV)
        for h in range(N_HEADS):
            t = jnp.broadcast_to(rows[h:h + 1, :], (Q_BLOCK, ROLL_W))
            t = pltpu.roll(t, 0, 1, stride=1, stride_axis=0)
            bt_ref[h] = jnp.where(band, t[:, :K_SPAN], NEG_INF)

    return pl.pallas_call(
        body, name="bias_table",
        out_shape=jax.ShapeDtypeStruct((N_HEADS, Q_BLOCK, K_SPAN), F32),
        compiler_params=_params(32),
    )(rel_bias_pad)


def _bias_grad(dbias):
    def body(a_ref, o_ref):
        rr = lax.broadcasted_iota(jnp.int32, (Q_BLOCK, Q_BLOCK), 0)
        cc = lax.broadcasted_iota(jnp.int32, (Q_BLOCK, Q_BLOCK), 1)
        flip = (rr + cc == Q_BLOCK - 1).astype(F32)
        c = lax.broadcasted_iota(jnp.int32, (ROLL_W, N_REL_PAD), 0)
        r = lax.broadcasted_iota(jnp.int32, (ROLL_W, N_REL_PAD), 1)
        e = jnp.where(c >= Q_BLOCK - 1, c - (Q_BLOCK - 1), c + (ROLL_W - Q_BLOCK + 1))
        pick = (r == _rel_index(e)).astype(F32)
        sums = []
        for h in range(N_HEADS):
            a = jnp.dot(flip, a_ref[h], precision=HIGHEST, preferred_element_type=F32)
            a = jnp.concatenate([a, jnp.zeros((Q_BLOCK, ROLL_W - K_SPAN), F32)], axis=1)
            a = pltpu.roll(a, 0, 1, stride=1, stride_axis=0)
            sums.append(jnp.sum(a, axis=0, keepdims=True))
        diag = jnp.concatenate(sums, axis=0)
        o_ref[...] = jnp.dot(diag, pick, precision=HIGHEST, preferred_element_type=F32)

    return pl.pallas_call(
        body, name="bias_grad",
        out_shape=jax.ShapeDtypeStruct((N_HEADS, N_REL_PAD), F32),
        compiler_params=_params(32),
    )(dbias)


def _gather_proj_fwd(x, norm_g, w_in_t):
    s = x.shape[0]
    tm = 512 if s % 512 == 0 else TOKEN_TILE
    nt = s // tm
    n_pad = Z_PAD // tm
    shard_w = w_in_t.shape[0]
    chip_w = 2 * shard_w
    n_chips = N_DEV // 2

    def body(order_ref, x_ref, g_ref, win_hbm, z_ref, h_ref, wt_hbm, stage, wchip, hb, win_f32, send_sems, recv_sems,
             local_sems):
        j = pl.program_id(0)
        i = pl.program_id(1)
        x_, y_, c_ = _my_pos()
        me, sib = (x_, y_, c_), (x_, y_, 1 - c_)
        near = _other_chips(me)
        pick = lambda a, b: tuple(jnp.where(c_ == 0, u, v) for u, v in zip(a, b))
        passed_from, passed_to = pick(near[0], near[1]), pick(near[1], near[0])

        def rows_of(block):
            return wt_hbm.at[pl.ds(pl.multiple_of(_flat_id(block) * shard_w, 16), shard_w), :]

        def copy(k, block, to, own=False):
            return pltpu.make_async_remote_copy(
                src_ref=stage if own else rows_of(block), dst_ref=rows_of(block),
                send_sem=send_sems.at[k], recv_sem=recv_sems.at[k], device_id=to, device_id_type=MESH)

        def sends():
            return ([copy(0, me, sib, True), copy(1, me, (*near[0], c_), True), copy(2, me, (*near[1], c_), True),
                     copy(3, (*passed_from, c_), (*passed_to, c_))]
                    + [copy(4 + n, (*near[n], c_), sib) for n in range(3)])

        keep = pltpu.make_async_copy(stage, rows_of(me), local_sems.at[0])

        def fetch(chip):
            first = pl.multiple_of((2 * chip[0] + chip[1]) * chip_w, 16)
            cp = pltpu.make_async_copy(wt_hbm.at[pl.ds(first, chip_w), :], wchip, local_sems.at[1])
            cp.start()
            cp.wait()

        @pl.when((j == 0) & (i == 0))
        def _():
            load = pltpu.make_async_copy(win_hbm, win_f32, local_sems.at[1])
            load.start()
            load.wait()
            stage[...] = win_f32[...].astype(BF16)
            keep.start()
            for cp in sends()[:3]:
                cp.start()
            copy(0, sib, me).wait_recv()
            keep.wait()
            fetch((x_, y_))

        @pl.when((j == 1) & (i == 0))
        def _():
            copy(1, (*near[0], c_), me).wait_recv()
            copy(2, (*near[1], c_), me).wait_recv()
            for cp in sends()[3:6]:
                cp.start()
            copy(4, (*near[0], 1 - c_), me).wait_recv()
            fetch(near[0])

        @pl.when((j == 2) & (i == 0))
        def _():
            copy(5, (*near[1], 1 - c_), me).wait_recv()
            fetch(near[1])

        @pl.when((j == 3) & (i == 0))
        def _():
            copy(3, (*near[2], c_), me).wait_recv()
            copy(6, (*near[2], c_), sib).start()
            copy(6, (*near[2], 1 - c_), me).wait_recv()
            fetch(near[2])

        @pl.when(i < n_pad)
        def _():
            z_ref[...] = jnp.zeros(z_ref.shape, BF16)

        @pl.when(i >= n_pad)
        def _():
            rows = pl.ds(pl.multiple_of((i - n_pad) * tm, tm), tm)

            @pl.when(j == 0)
            def _():
                xf = x_ref[...]
                r = lax.rsqrt(jnp.mean(xf * xf, axis=-1, keepdims=True) + EPS)
                hf = (xf * r * g_ref[...]).astype(BF16)
                hb[rows, :] = hf
                h_ref[...] = hf

            blk = _dot(hb[rows, :], wchip[...], NT)
            q_scale = jnp.where(order_ref[j] == 0, Q_SCALE, 1.0).astype(F32)
            z_ref[:, :D_A] = (blk[:, :D_A] * q_scale).astype(BF16)
            z_ref[:, D_A:] = blk[:, D_A:].astype(BF16)

        @pl.when((j == n_chips - 1) & (i == n_pad + nt - 1))
        def _():
            for cp in sends():
                cp.wait_send()

    pos = _my_pos()
    order = jnp.stack([2 * cx + cy for cx, cy in ((pos[0], pos[1]),) + _other_chips(pos)]).astype(jnp.int32)
    first_pass = lambda j, i: jnp.where(j == 0, jnp.maximum(i - n_pad, 0), nt - 1)
    grid_spec = pltpu.PrefetchScalarGridSpec(
        num_scalar_prefetch=1,
        grid=(n_chips, n_pad + nt),
        in_specs=[pl.BlockSpec((tm, D_MODEL), lambda j, i, o: (first_pass(j, i), 0)),
                  pl.BlockSpec((1, D_MODEL), lambda j, i, o: (0, 0)),
                  pl.BlockSpec(memory_space=pl.ANY)],
        out_specs=(pl.BlockSpec((tm, chip_w), lambda j, i, o: (i, o[j])),
                   pl.BlockSpec((tm, D_MODEL), lambda j, i, o: (first_pass(j, i), 0)),
                   pl.BlockSpec(memory_space=pl.ANY)),
        scratch_shapes=[pltpu.VMEM((shard_w, D_MODEL), BF16), pltpu.VMEM((chip_w, D_MODEL), BF16),
                        pltpu.VMEM((s, D_MODEL), BF16), pltpu.VMEM(w_in_t.shape, F32),
                        pltpu.SemaphoreType.DMA((N_DEV - 1,)), pltpu.SemaphoreType.DMA((N_DEV - 1,)),
                        pltpu.SemaphoreType.DMA((2,))])
    return pl.pallas_call(
        body, name="gather_proj_fwd",
        grid_spec=grid_spec,
        out_shape=(jax.ShapeDtypeStruct((Z_PAD + s, D_IN), BF16), jax.ShapeDtypeStruct((s, D_MODEL), BF16),
                   jax.ShapeDtypeStruct((D_IN, D_MODEL), BF16)),
        compiler_params=_params(60),
    )(order, x, norm_g, w_in_t)


def _attn_specs(rows):
    pairs = N_HEADS // 2
    return ([pl.BlockSpec((rows, 128), functools.partial(lambda which, p: (0, which * pairs + p), which))
             for which in range(3)]
            + [pl.BlockSpec((2, Q_BLOCK, K_SPAN), lambda p: (p, 0, 0))])


def _head_masks():
    lane = lax.broadcasted_iota(jnp.int32, (1, 128), 1)
    first = lane < HEAD_DIM
    return (first, jnp.logical_not(first))


def _stack_heads(x, masks):
    zero = jnp.zeros((), x.dtype)
    return jnp.concatenate([jnp.where(m, x, zero) for m in masks], axis=0)


STRIP = 16


def _softmax_strips(s_ref, bias_ref, b):
    valid = lax.broadcasted_iota(jnp.int32, (1, K_SPAN), 1) >= Z_PAD - b * Q_BLOCK
    for t in range(2 * Q_BLOCK // STRIP):
        hh, r = divmod(t * STRIP, Q_BLOCK)
        st = s_ref[t * STRIP:(t + 1) * STRIP, :] + bias_ref[hh, r:r + STRIP, :]
        st = jnp.where(valid, st, NEG_INF)
        e = jnp.exp(st - jnp.max(st, axis=-1, keepdims=True))
        yield e * (1.0 / jnp.sum(e, axis=-1, keepdims=True))


def _side_by_side_strips(strips):
    half = len(strips) // 2
    return jnp.concatenate([jnp.concatenate([a, c], axis=1) for a, c in zip(strips[:half], strips[half:])], axis=0)


def _attn_fwd(qkv, bias_table, shards):
    s = qkv.shape[0] - Z_PAD
    nb = s // Q_BLOCK
    n = len(shards)
    pairs = N_HEADS // 2

    def body(*refs):
        q_ref, k_ref, v_ref, bt_ref = refs[:4]
        shard_refs = refs[4:4 + n]
        o_ref = refs[4 + n]
        slot_refs = refs[5 + n:5 + 2 * n]
        stages = refs[5 + 2 * n:5 + 3 * n]
        s_scr, send_sems, recv_sems, local_sems = refs[5 + 3 * n:]
        p_id = pl.program_id(0)
        gather = _SlotGather(slot_refs, send_sems, recv_sems, own=stages)
        keep = [pltpu.make_async_copy(stages[a], slot_refs[a].at[_flat_id(_my_pos())], local_sems.at[a])
                for a in range(n)]

        @pl.when(p_id == 0)
        def _():
            for a in range(n):
                stages[a][...] = shard_refs[a][...].astype(BF16)
                keep[a].start()
            gather.start()

        @pl.when(p_id == 1)
        def _():
            gather.pass_on()

        masks = _head_masks()

        def scores(b, half):
            r0 = pl.multiple_of(b * Q_BLOCK, Q_BLOCK)
            q2 = _stack_heads(q_ref[pl.ds(r0 + Z_PAD, Q_BLOCK), :], masks)
            s_scr[half] = _dot(q2, k_ref[pl.ds(r0, K_SPAN), :], NT)

        def finish(b, half):
            r0 = pl.multiple_of(b * Q_BLOCK, Q_BLOCK)
            v2 = _stack_heads(v_ref[pl.ds(r0, K_SPAN), :], masks)
            p = [st.astype(BF16) for st in _softmax_strips(s_scr.at[half], bt_ref, b)]
            o_ref[pl.ds(r0, Q_BLOCK), :] = _dot(_side_by_side_strips(p), v2)

        def two_blocks(i, carry):
            b = 2 * i
            scores(b + 1, 1)
            finish(b, 0)
            scores(jnp.minimum(b + 2, nb - 1), 0)
            finish(b + 1, 1)
            return carry

        scores(0, 0)
        lax.fori_loop(0, nb // 2, two_blocks, 0)

        @pl.when(p_id == pairs - 1)
        def _():
            gather.finish()
            for cp in keep:
                cp.wait()

    hbm = pl.BlockSpec(memory_space=pl.ANY)
    return pl.pallas_call(
        body, name="attn_fwd",
        grid=(pairs,),
        in_specs=_attn_specs(s + Z_PAD) + [pl.BlockSpec(a.shape, lambda p: (0, 0)) for a in shards],
        out_specs=(pl.BlockSpec((s, 128), lambda p: (0, p)),) + (hbm,) * n,
        out_shape=(jax.ShapeDtypeStruct((s, D_A), F32),)
        + tuple(jax.ShapeDtypeStruct((N_DEV,) + a.shape, BF16) for a in shards),
        scratch_shapes=[pltpu.VMEM(a.shape, BF16) for a in shards]
        + [pltpu.VMEM((2, 2 * Q_BLOCK, K_SPAN), F32),
           pltpu.SemaphoreType.DMA((n, N_DEV - 1)), pltpu.SemaphoreType.DMA((n, N_DEV - 1)),
           pltpu.SemaphoreType.DMA((n,))],
        compiler_params=_params(48),
    )(qkv, qkv, qkv, bias_table, *shards)


def _attn_bwd(qkv, bias_table, d_out, to_chip):
    s = qkv.shape[0] - Z_PAD
    nb = s // Q_BLOCK
    n = len(to_chip)
    pairs = N_HEADS // 2

    def body(*refs):
        q_ref, k_ref, v_ref, bt_ref, do_ref = refs[:5]
        to_chip_refs = refs[5:5 + n]
        dqkv_ref, db_ref = refs[5 + n:7 + n]
        from_chip_refs = refs[7 + n:7 + 2 * n]
        dk_acc, dv_acc, s_scr, dp_scr, send_sems, recv_sems = refs[7 + 2 * n:]
        p_id = pl.program_id(0)

        @pl.when(p_id == 0)
        def _():
            for cp in _owner_copies(to_chip_refs, from_chip_refs, send_sems, recv_sems):
                cp.start()

        dk_acc[...] = jnp.zeros(dk_acc.shape, F32)
        dv_acc[...] = jnp.zeros(dv_acc.shape, F32)
        db_ref[...] = jnp.zeros(db_ref.shape, F32)
        masks = _head_masks()

        def operands(b):
            r0 = pl.multiple_of(b * Q_BLOCK, Q_BLOCK)
            q2 = _stack_heads(q_ref[pl.ds(r0 + Z_PAD, Q_BLOCK), :], masks)
            do2 = _stack_heads(do_ref[pl.ds(r0, Q_BLOCK), :], masks)
            return r0, q2, do2, k_ref[pl.ds(r0, K_SPAN), :]

        def ahead(b, half):
            r0, q2, do2, kcat = operands(b)
            s_scr[half] = _dot(q2, kcat, NT)
            dp_scr[half] = _dot(do2, v_ref[pl.ds(r0, K_SPAN), :], NT)

        def finish(b, half):
            r0, q2, do2, kcat = operands(b)
            p_strips, ds_strips = [], []
            for t, p in enumerate(_softmax_strips(s_scr.at[half], bt_ref, b)):
                hh, r = divmod(t * STRIP, Q_BLOCK)
                dp_t = dp_scr[half, t * STRIP:(t + 1) * STRIP, :]
                ds = p * (dp_t - jnp.sum(p * dp_t, axis=-1, keepdims=True))
                db_ref[hh, r:r + STRIP, :] += ds
                p_strips.append(p.astype(BF16))
                ds_strips.append(ds.astype(BF16))
            dq = _dot(_side_by_side_strips(ds_strips), _stack_heads(kcat, masks))
            dqkv_ref[0, pl.ds(r0, Q_BLOCK), :] = (dq * Q_SCALE).astype(BF16)
            dk_acc[pl.ds(r0, K_SPAN), :] += _dot(jnp.concatenate(ds_strips, axis=0), q2, TN)
            dv_acc[pl.ds(r0, K_SPAN), :] += _dot(jnp.concatenate(p_strips, axis=0), do2, TN)

        def two_blocks(i, carry):
            b = 2 * i
            ahead(b + 1, 1)
            finish(b, 0)
            ahead(jnp.minimum(b + 2, nb - 1), 0)
            finish(b + 1, 1)
            return carry

        ahead(0, 0)
        lax.fori_loop(0, nb // 2, two_blocks, 0)
        dqkv_ref[1] = dk_acc[Z_PAD:, :].astype(BF16)
        dqkv_ref[2] = dv_acc[Z_PAD:, :].astype(BF16)

        @pl.when(p_id == pairs - 1)
        def _():
            for cp in _owner_copies(to_chip_refs, from_chip_refs, send_sems, recv_sems):
                cp.wait_recv()
                cp.wait_send()

    hbm = pl.BlockSpec(memory_space=pl.ANY)
    return pl.pallas_call(
        body, name="attn_bwd",
        grid=(pairs,),
        in_specs=_attn_specs(s + Z_PAD) + [pl.BlockSpec((s, 128), lambda p: (0, p))] + [hbm] * n,
        out_specs=(pl.BlockSpec((3, s, 128), lambda p: (0, 0, p)),
                   pl.BlockSpec((2, Q_BLOCK, K_SPAN), lambda p: (p, 0, 0))) + (hbm,) * n,
        out_shape=(jax.ShapeDtypeStruct((3, s, D_A), BF16),
                   jax.ShapeDtypeStruct((N_HEADS, Q_BLOCK, K_SPAN), F32))
        + tuple(jax.ShapeDtypeStruct(t.shape, t.dtype) for t in to_chip),
        scratch_shapes=[pltpu.VMEM((s + Z_PAD, 128), F32), pltpu.VMEM((s + Z_PAD, 128), F32),
                        pltpu.VMEM((2, 2 * Q_BLOCK, K_SPAN), F32), pltpu.VMEM((2, 2 * Q_BLOCK, K_SPAN), F32),
                        pltpu.SemaphoreType.DMA((n, 3)), pltpu.SemaphoreType.DMA((n, 3))],
        compiler_params=_params(56),
    )(qkv, qkv, qkv, bias_table, d_out, *to_chip)


def _mid_fwd_bwd(x, target, attn_out, z, w_pa, w_pb, w_out, b_gate, ln_g, ln_b, w_s, b_s, final_g):
    s = x.shape[0]
    tm = TOKEN_TILE
    nt = s // tm

    def body(x_ref, t_ref, oa_ref, ga_ref, ub_ref, vb_ref, gb_ref, ta0_ref, ta1_ref, tb0_ref, tb1_ref,
             wpa_hbm, wpb_hbm, wout_hbm, bg_ref, lng_ref, lnb_ref, ws_ref, bs_ref, fg_ref,
             dx2_ref, doa_ref, dz_ref, dwout_hbm, dwpa_hbm, dwpb_hbm, dbg_ref, dfg_ref, dlng_ref, dlnb_ref, dws_ref,
             dbs_ref, loss_ref,
             wpa, wpb, wout, wmix, acc_out, acc_pa, acc_pb, sem):
        i = pl.program_id(0)

        @pl.when(i == 0)
        def _():
            loads = [pltpu.make_async_copy(src, dst, sem.at[n])
                     for n, (src, dst) in enumerate(((wpa_hbm, wpa), (wpb_hbm, wpb), (wout_hbm, wout)))]
            for cp in loads:
                cp.start()
            t_idx = lax.broadcasted_iota(jnp.int32, (SGU_CHUNK, SGU_CHUNK), 0)
            s_idx = lax.broadcasted_iota(jnp.int32, (SGU_CHUNK, SGU_CHUNK), 1)
            for g in range(N_GROUPS):
                wmix[g] = jnp.where(s_idx <= t_idx, ws_ref[g], 0.0).astype(BF16)
            for ref in (acc_out, acc_pa, acc_pb, dbg_ref, dfg_ref, dlng_ref, dlnb_ref, dws_ref, dbs_ref, loss_ref):
                ref[...] = jnp.zeros(ref.shape, F32)
            for cp in loads:
                cp.wait()

        def tile_fwd_bwd(rows):
            g_a = ga_ref[rows, :].astype(F32)
            u_b = ub_ref[rows, :].astype(F32)
            v_b = vb_ref[rows, :].astype(F32)
            g_b = gb_ref[rows, :].astype(F32)
            bg = bg_ref[...]
            sg_a = _sigmoid(g_a)
            silu_a = g_a * sg_a
            o_a = oa_ref[rows, :]
            y_a = (o_a * silu_a).astype(BF16)
            ug, dgelu_u = _gelu_and_grad(u_b)
            vg, dgelu_v = _gelu_and_grad(v_b)
            mu = jnp.mean(vg, axis=-1, keepdims=True)
            vc = vg - mu
            rstd = lax.rsqrt(jnp.mean(vc * vc, axis=-1, keepdims=True) + EPS)
            vhat = vc * rstd
            lng = lng_ref[...]
            vn = (vhat * lng + lnb_ref[...]).astype(BF16)
            sg_b = _sigmoid(g_b)
            silu_b = g_b * sg_b
            subs = [slice(n * SGU_CHUNK, (n + 1) * SGU_CHUNK) for n in range(tm // SGU_CHUNK)]
            mixed = jnp.concatenate([jnp.concatenate(
                [_dot(wmix[g], vn[sub, g * 128:(g + 1) * 128]) + bs_ref[g] for g in range(N_GROUPS)], axis=1)
                for sub in subs], axis=0)
            um = ug * mixed
            y_b = (um * silu_b).astype(BF16)
            gate_a = _sigmoid(jnp.concatenate([ta0_ref[rows, :], ta1_ref[rows, :]], axis=1).astype(F32)
                              + bg[:, :D_MODEL])
            gate_b = _sigmoid(jnp.concatenate([tb0_ref[rows, :], tb1_ref[rows, :]], axis=1).astype(F32)
                              + bg[:, D_MODEL:])
            p_a = _dot(y_a, wpa[...])
            p_b = _dot(y_b, wpb[...])
            merged = (gate_a * p_a + gate_b * p_b).astype(BF16)
            x2 = x_ref[rows, :] + _dot(merged, wout[...])
            r2 = lax.rsqrt(jnp.mean(x2 * x2, axis=-1, keepdims=True) + EPS)
            xh = x2 * r2
            fg = fg_ref[...]
            err = xh * fg - t_ref[rows, :]
            loss_ref[...] += jnp.sum(jnp.sum(err * err, axis=-1, keepdims=True), axis=0, keepdims=True) * (0.5 / D_MODEL)
            dy = err * (1.0 / D_MODEL)
            dfg_ref[...] += jnp.sum(dy * xh, axis=0, keepdims=True)
            gy = dy * fg
            dx2 = r2 * (gy - xh * jnp.mean(gy * xh, axis=-1, keepdims=True))
            dx2_ref[rows, :] = dx2
            dx2b = dx2.astype(BF16)
            dmerged = _dot(dx2b, wout[...], NT)
            acc_out[...] += _dot(merged, dx2b, TN)
            dp_a = dmerged * gate_a
            dp_b = dmerged * gate_b
            dgate_a = dp_a * p_a * (1.0 - gate_a)
            dgate_b = dp_b * p_b * (1.0 - gate_b)
            dbg_ref[:, :D_MODEL] += jnp.sum(dgate_a, axis=0, keepdims=True)
            dbg_ref[:, D_MODEL:] += jnp.sum(dgate_b, axis=0, keepdims=True)
            dz_ref[rows, 2048:3072] = dgate_a.astype(BF16)
            dz_ref[rows, 3072:4096] = dgate_b.astype(BF16)
            dp_ab = dp_a.astype(BF16)
            dp_bb = dp_b.astype(BF16)
            dy_a = _dot(dp_ab, wpa[...], NT)
            dy_b = _dot(dp_bb, wpb[...], NT)
            acc_pa[...] += _dot(y_a, dp_ab, TN)
            acc_pb[...] += _dot(y_b, dp_bb, TN)
            doa_ref[rows, :] = (dy_a * silu_a).astype(BF16)
            dz_ref[rows, 0:512] = (dy_a * o_a * (sg_a * (1.0 + g_a * (1.0 - sg_a)))).astype(BF16)
            dz_ref[rows, 1536:2048] = (dy_b * um * (sg_b * (1.0 + g_b * (1.0 - sg_b)))).astype(BF16)
            dys = dy_b * silu_b
            dz_ref[rows, 512:1024] = (dys * mixed * dgelu_u).astype(BF16)
            dmixed = dys * ug
            dmb = dmixed.astype(BF16)
            dvn_rows = []
            for sub in subs:
                dvn_parts = []
                for g in range(N_GROUPS):
                    cols = slice(g * 128, (g + 1) * 128)
                    dws_ref[g] += _dot(dmb[sub, cols], vn[sub, cols], NT)
                    dbs_ref[g] += jnp.sum(dmixed[sub, cols], axis=-1, keepdims=True)
                    dvn_parts.append(_dot(wmix[g], dmb[sub, cols], TN))
                dvn_rows.append(jnp.concatenate(dvn_parts, axis=1))
            dvn = jnp.concatenate(dvn_rows, axis=0)
            dlng_ref[...] += jnp.sum(dvn * vhat, axis=0, keepdims=True)
            dlnb_ref[...] += jnp.sum(dvn, axis=0, keepdims=True)
            dvh = dvn * lng
            dvg = rstd * (dvh - jnp.mean(dvh, axis=-1, keepdims=True)
                          - vhat * jnp.mean(dvh * vhat, axis=-1, keepdims=True))
            dz_ref[rows, 1024:1536] = (dvg * dgelu_v).astype(BF16)

        tile_fwd_bwd(slice(0, tm))

        @pl.when(i == nt - 1)
        def _():
            t_idx = lax.broadcasted_iota(jnp.int32, (SGU_CHUNK, SGU_CHUNK), 0)
            s_idx = lax.broadcasted_iota(jnp.int32, (SGU_CHUNK, SGU_CHUNK), 1)
            for g in range(N_GROUPS):
                dws_ref[g] = jnp.where(s_idx <= t_idx, dws_ref[g], 0.0)
            stores = [pltpu.make_async_copy(src, dst, sem.at[n])
                      for n, (src, dst) in enumerate(((acc_out, dwout_hbm), (acc_pa, dwpa_hbm), (acc_pb, dwpb_hbm)))]
            for cp in stores:
                cp.start()
            for cp in stores:
                cp.wait()

    tile = lambda w: pl.BlockSpec((tm, w), lambda i: (i, 0))
    whole = lambda shape: pl.BlockSpec(shape, lambda i: (0,) * len(shape))
    hbm = pl.BlockSpec(memory_space=pl.ANY)
    return pl.pallas_call(
        body, name="mid_fwd_bwd",
        grid=(nt,),
        in_specs=[tile(D_MODEL), tile(D_MODEL), tile(D_A)]
        + [pl.BlockSpec((tm, COL_BLOCK), functools.partial(lambda c, i: (i + Z_PAD // tm, c), c))
           for c in range(3, N_COL_BLOCKS)]
        + [hbm, hbm, hbm,
                  whole((1, 2 * D_MODEL)), whole((1, D_B)), whole((1, D_B)),
                  whole((N_GROUPS, SGU_CHUNK, SGU_CHUNK)), whole((N_GROUPS, SGU_CHUNK, 1)), whole((1, D_MODEL))],
        out_specs=(tile(D_MODEL), tile(D_A), tile(REST), hbm, hbm, hbm,
                   whole((1, 2 * D_MODEL)), whole((1, D_MODEL)), whole((1, D_B)), whole((1, D_B)),
                   whole((N_GROUPS, SGU_CHUNK, SGU_CHUNK)), whole((N_GROUPS, SGU_CHUNK, 1)), whole((1, 1))),
        out_shape=(jax.ShapeDtypeStruct((s, D_MODEL), F32), jax.ShapeDtypeStruct((s, D_A), BF16),
                   jax.ShapeDtypeStruct((s, REST), BF16),
                   jax.ShapeDtypeStruct((D_MODEL, D_MODEL), F32), jax.ShapeDtypeStruct((D_A, D_MODEL), F32),
                   jax.ShapeDtypeStruct((D_B, D_MODEL), F32),
                   jax.ShapeDtypeStruct((1, 2 * D_MODEL), F32), jax.ShapeDtypeStruct((1, D_MODEL), F32),
                   jax.ShapeDtypeStruct((1, D_B), F32), jax.ShapeDtypeStruct((1, D_B), F32),
                   jax.ShapeDtypeStruct((N_GROUPS, SGU_CHUNK, SGU_CHUNK), F32),
                   jax.ShapeDtypeStruct((N_GROUPS, SGU_CHUNK, 1), F32), jax.ShapeDtypeStruct((1, 1), F32)),
        scratch_shapes=[pltpu.VMEM((D_A, D_MODEL), BF16), pltpu.VMEM((D_B, D_MODEL), BF16),
                        pltpu.VMEM((D_MODEL, D_MODEL), BF16), pltpu.VMEM((N_GROUPS, SGU_CHUNK, SGU_CHUNK), BF16),
                        pltpu.VMEM((D_MODEL, D_MODEL), F32), pltpu.VMEM((D_A, D_MODEL), F32),
                        pltpu.VMEM((D_B, D_MODEL), F32),
                        pltpu.SemaphoreType.DMA((3,))],
        compiler_params=_params(56),
    )(x, target, attn_out, *([z] * (N_COL_BLOCKS - 3)), w_pa, w_pb, w_out, b_gate, ln_g, ln_b, w_s, b_s, final_g)


def _proj_bwd_x(dqkv, drest, x, dx2, norm_g, w_in_t, to_chip, small):
    s = x.shape[0]
    tm = 512 if s % 512 == 0 else TOKEN_TILE
    nt = s // tm
    n = len(to_chip)
    ws_shape = (N_GROUPS * SGU_CHUNK, SGU_CHUNK)

    def body(*refs):
        dqkv_ref, dr_ref, x_ref, dx2_ref, g_ref, w_hbm = refs[:6]
        to_chip_refs = refs[6:6 + n]
        bg_ref, rel_ref, lng_ref, lnb_ref, fg_ref, loss_ref, bs_ref, ws_ref = refs[6 + n:14 + n]
        dx_ref, dg_ref = refs[14 + n:16 + n]
        from_chip_refs = refs[16 + n:16 + 2 * n]
        slab_land, ws_land, ng_land = refs[16 + 2 * n:19 + 2 * n]
        (w, slab_stage, ws_stage, ng_stage, sem, send_sems, recv_sems, early_send, early_recv, late_send, late_recv,
         keep_sems) = refs[19 + 2 * n:]
        i = pl.program_id(0)
        me = _flat_id(_my_pos())
        early = _SlotGather([slab_land, ws_land], early_send, early_recv, own=[slab_stage, ws_stage])
        late = _SlotGather([ng_land], late_send, late_recv, own=[ng_stage])
        keep = [pltpu.make_async_copy(stage, land.at[me], keep_sems.at[k]) for k, (stage, land) in enumerate(
            ((slab_stage, slab_land), (ws_stage, ws_land), (ng_stage, ng_land)))]

        @pl.when(i == 0)
        def _():
            for rc in _owner_copies(to_chip_refs, from_chip_refs, send_sems, recv_sems):
                rc.start()
            cp = pltpu.make_async_copy(w_hbm, w, sem)
            cp.start()
            dg_ref[...] = jnp.zeros(dg_ref.shape, F32)
            slab_stage[...] = jnp.zeros(slab_stage.shape, F32)
            slab_stage[ROW_B_GATE:ROW_B_GATE + 1, :] = bg_ref[:, :D_MODEL]
            slab_stage[ROW_B_GATE + 1:ROW_B_GATE + 2, :] = bg_ref[:, D_MODEL:]
            slab_stage[ROW_LN_G:ROW_LN_G + 1, :D_B] = lng_ref[...]
            slab_stage[ROW_LN_B:ROW_LN_B + 1, :D_B] = lnb_ref[...]
            slab_stage[ROW_FINAL_G:ROW_FINAL_G + 1, :] = fg_ref[...]
            slab_stage[ROW_LOSS:ROW_LOSS + 1, :1] = loss_ref[...]
            slab_stage[ROW_REL:ROW_REL + N_HEADS, :N_REL_PAD] = rel_ref[...]
            eye = (lax.broadcasted_iota(jnp.int32, (SGU_CHUNK, SGU_CHUNK), 0)
                   == lax.broadcasted_iota(jnp.int32, (SGU_CHUNK, SGU_CHUNK), 1))
            for g in range(N_GROUPS):
                row = jnp.sum(jnp.where(eye, bs_ref[g], 0.0), axis=0, keepdims=True)
                slab_stage[ROW_B_S + g:ROW_B_S + g + 1, :SGU_CHUNK] = row
            ws_stage[...] = ws_ref[...]
            keep[0].start()
            keep[1].start()
            early.start()
            cp.wait()

        @pl.when(i == 1)
        def _():
            early.pass_on()

        dh = None
        for c in range(N_COL_BLOCKS):
            dz = dqkv_ref[c] if c < 3 else dr_ref[:, (c - 3) * COL_BLOCK:(c - 2) * COL_BLOCK]
            part = _dot(dz, w[c * COL_BLOCK:(c + 1) * COL_BLOCK, :])
            dh = part if dh is None else dh + part
        xf = x_ref[...]
        r = lax.rsqrt(jnp.mean(xf * xf, axis=-1, keepdims=True) + EPS)
        xn = xf * r
        dg_ref[...] += jnp.sum(dh * xn, axis=0, keepdims=True)
        gh = dh * g_ref[...]
        dx_ref[...] = r * (gh - xn * jnp.mean(gh * xn, axis=-1, keepdims=True)) + dx2_ref[...]

        @pl.when(i == nt - 1)
        def _():
            ng_stage[...] = jnp.zeros(ng_stage.shape, F32)
            ng_stage[0:1, :] = dg_ref[...]
            keep[2].start()
            late.start()
            late.pass_on()
            late.finish()
            early.finish()
            for cp in keep:
                cp.wait()
            for rc in _owner_copies(to_chip_refs, from_chip_refs, send_sems, recv_sems):
                rc.wait_recv()
                rc.wait_send()

    hbm = pl.BlockSpec(memory_space=pl.ANY)
    whole = lambda a: pl.BlockSpec(a.shape, lambda i: (0,) * a.ndim)
    lands = ((N_DEV, SLAB_ROWS, D_MODEL), (N_DEV,) + ws_shape, (N_DEV, 8, D_MODEL))
    return pl.pallas_call(
        body, name="proj_bwd_x",
        grid=(nt,),
        in_specs=[pl.BlockSpec((3, tm, D_A), lambda i: (0, i, 0)),
                  pl.BlockSpec((tm, REST), lambda i: (i, 0)),
                  pl.BlockSpec((tm, D_MODEL), lambda i: (i, 0)),
                  pl.BlockSpec((tm, D_MODEL), lambda i: (i, 0)),
                  pl.BlockSpec((1, D_MODEL), lambda i: (0, 0)),
                  hbm] + [hbm] * n + [whole(a) for a in small],
        out_specs=(pl.BlockSpec((tm, D_MODEL), lambda i: (i, 0)),
                   pl.BlockSpec((1, D_MODEL), lambda i: (0, 0))) + (hbm,) * (n + 3),
        out_shape=(jax.ShapeDtypeStruct((s, D_MODEL), F32), jax.ShapeDtypeStruct((1, D_MODEL), F32))
        + tuple(jax.ShapeDtypeStruct(t.shape, t.dtype) for t in to_chip)
        + tuple(jax.ShapeDtypeStruct(shape, F32) for shape in lands),
        scratch_shapes=[pltpu.VMEM((D_IN, D_MODEL), BF16)]
        + [pltpu.VMEM(shape[1:], F32) for shape in lands]
        + [pltpu.SemaphoreType.DMA, pltpu.SemaphoreType.DMA((n, 3)), pltpu.SemaphoreType.DMA((n, 3)),
           pltpu.SemaphoreType.DMA((2, N_DEV - 1)), pltpu.SemaphoreType.DMA((2, N_DEV - 1)),
           pltpu.SemaphoreType.DMA((1, N_DEV - 1)), pltpu.SemaphoreType.DMA((1, N_DEV - 1)),
           pltpu.SemaphoreType.DMA((3,))],
        compiler_params=_params(56),
    )(dqkv, drest, x, dx2, norm_g, w_in_t, *to_chip, *small)


def _proj_bwd_w(h, dqkv, drest):
    s = h.shape[0]
    tk = min(s, 1024)
    nk = s // tk

    def body(h_ref, dqkv_ref, dr_ref, o_ref, acc):
        j = pl.program_id(0)
        i = pl.program_id(1)

        @pl.when(i == 0)
        def _():
            acc[...] = jnp.zeros(acc.shape, F32)

        @pl.when(j < 3)
        def _():
            acc[...] += _dot(dqkv_ref[...], h_ref[...], TN)

        @pl.when(j >= 3)
        def _():
            acc[...] += _dot(dr_ref[...], h_ref[...], TN)

        @pl.when(i == nk - 1)
        def _():
            o_ref[...] = acc[...].astype(BF16)

    return pl.pallas_call(
        body, name="proj_bwd_w",
        grid=(N_COL_BLOCKS, nk),
        in_specs=[pl.BlockSpec((tk, D_MODEL), lambda j, i: (i, 0)),
                  pl.BlockSpec((None, tk, COL_BLOCK),
                               lambda j, i: (jnp.minimum(j, 2), jnp.where(j < 3, i, nk - 1), 0)),
                  pl.BlockSpec((tk, COL_BLOCK),
                               lambda j, i: (jnp.where(j >= 3, i, 0), jnp.maximum(j - 3, 0)))],
        out_specs=pl.BlockSpec((COL_BLOCK, D_MODEL), lambda j, i: (j, 0)),
        out_shape=jax.ShapeDtypeStruct((D_IN, D_MODEL), BF16),
        scratch_shapes=[pltpu.VMEM((COL_BLOCK, D_MODEL), F32)],
        compiler_params=_params(40),
    )(h, dqkv, drest)


def _adamw_math(w, g, m, v):
    c1 = 1.0 - ADAM_B1 ** ADAM_STEP
    c2 = 1.0 - ADAM_B2 ** ADAM_STEP
    nm = ADAM_B1 * m + (1.0 - ADAM_B1) * g
    nv = ADAM_B2 * v + (1.0 - ADAM_B2) * (g * g)
    return -ADAM_LR * ((nm / c1) / (jnp.sqrt(nv / c2) + ADAM_EPS) + ADAM_WD * w), nm, nv


def _adamw(name, w, g, m, v, from_chip):
    rows, cols = w.shape
    tr = rows if rows * cols <= 512 * 1024 else next(t for t in range(256, 7, -8) if rows % t == 0)

    def body(w_ref, g_ref, m_ref, v_ref, t_ref, g_out, d_ref, nm_ref, nv_ref):
        gg = g_ref[...]
        for j in range(3):
            gg = gg + t_ref[j].astype(F32)
        g_out[...] = gg
        d_ref[...], nm_ref[...], nv_ref[...] = _adamw_math(w_ref[...], gg, m_ref[...], v_ref[...])

    spec = pl.BlockSpec((tr, cols), lambda i: (i, 0))
    shape = jax.ShapeDtypeStruct((rows, cols), F32)
    return pl.pallas_call(
        body, name=name,
        grid=(rows // tr,),
        in_specs=[spec] * 4 + [pl.BlockSpec((3, tr, cols), lambda i: (0, i, 0))],
        out_specs=(spec,) * 4, out_shape=(shape,) * 4,
        compiler_params=_params(32),
    )(w, g, m, v, from_chip)


_SMALL = (("norm_g", (1, D_MODEL)), ("b_gate", (1, 2 * D_MODEL)), ("rel_bias", (N_HEADS, N_REL)),
          ("sgu_ln_g", (1, D_B)), ("sgu_ln_b", (1, D_B)), ("w_s", (N_GROUPS * SGU_CHUNK, SGU_CHUNK)),
          ("b_s", (N_GROUPS, SGU_CHUNK)), ("final_g", (1, D_MODEL)))


def _adamw_small(slabs, ws_all, ng_all, weights, moments_m, moments_v):
    k = len(_SMALL)

    def total(ref):
        acc = ref[0]
        for d in range(1, N_DEV):
            acc = acc + ref[d]
        return acc

    def body(*refs):
        slab_ref, ws_ref, ng_ref = refs[:3]
        w_refs, m_refs, v_refs = refs[3:3 + k], refs[3 + k:3 + 2 * k], refs[3 + 2 * k:3 + 3 * k]
        outs = refs[3 + 3 * k:]
        slab = total(slab_ref)
        grads = {
            "norm_g": total(ng_ref)[0:1, :],
            "b_gate": jnp.concatenate([slab[ROW_B_GATE:ROW_B_GATE + 1, :], slab[ROW_B_GATE + 1:ROW_B_GATE + 2, :]], axis=1),
            "rel_bias": slab[ROW_REL:ROW_REL + N_HEADS, :N_REL],
            "sgu_ln_g": slab[ROW_LN_G:ROW_LN_G + 1, :D_B],
            "sgu_ln_b": slab[ROW_LN_B:ROW_LN_B + 1, :D_B],
            "w_s": total(ws_ref),
            "b_s": slab[ROW_B_S:ROW_B_S + N_GROUPS, :SGU_CHUNK],
            "final_g": slab[ROW_FINAL_G:ROW_FINAL_G + 1, :],
        }
        for n, (name, _) in enumerate(_SMALL):
            g = grads[name]
            outs[n][...] = g
            outs[k + n][...], outs[2 * k + n][...], outs[3 * k + n][...] = _adamw_math(
                w_refs[n][...], g, m_refs[n][...], v_refs[n][...])
        outs[4 * k][...] = slab[ROW_LOSS:ROW_LOSS + 1, :1]

    vmem = pl.BlockSpec(memory_space=pltpu.VMEM)
    shapes = tuple(jax.ShapeDtypeStruct(shape, F32) for _, shape in _SMALL)
    return pl.pallas_call(
        body, name="adamw_small",
        out_shape=shapes * 4 + (jax.ShapeDtypeStruct((1, 1), F32),),
        in_specs=[vmem] * (3 + 3 * k), out_specs=tuple([vmem] * (4 * k + 1)),
        compiler_params=_params(16),
    )(slabs, ws_all, ng_all, *weights, *moments_m, *moments_v)


def _pad_rel(a):
    return jnp.pad(a.reshape(N_HEADS, N_REL), ((0, 0), (0, N_REL_PAD - N_REL)))


def kernel(x, norm_g, w_in, b_gate, rel_bias, sgu_ln_g, sgu_ln_b, w_s, b_s, w_pa, w_pb, w_out, final_g, loss_target, m_norm_g, m_w_in, m_b_gate, m_rel_bias, m_sgu_ln_g, m_sgu_ln_b, m_w_s, m_b_s, m_w_pa, m_w_pb, m_w_out, m_final_g, v_norm_g, v_w_in, v_b_gate, v_rel_bias, v_sgu_ln_g, v_sgu_ln_b, v_w_s, v_b_s, v_w_pa, v_w_pb, v_w_out, v_final_g):
    s = x.shape[1]
    xs = x.reshape(s, D_MODEL)
    tgt = loss_target.reshape(s, D_MODEL)

    bias_table = _bias_table(_pad_rel(rel_bias))
    w_in_t = jnp.swapaxes(w_in[0], 0, 1)
    qkv, h, w_in_t_full = _gather_proj_fwd(xs, norm_g, w_in_t)
    attn_out, g_pa, g_pb, g_out = _attn_fwd(qkv, bias_table, (w_pa[0], w_pb[0], w_out[0]))
    w_pa_full = jnp.transpose(g_pa, (1, 0, 2)).reshape(D_A, D_MODEL)
    w_pb_full = jnp.transpose(g_pb, (1, 0, 2)).reshape(D_B, D_MODEL)
    w_out_full = g_out.reshape(D_MODEL, D_MODEL)

    (dx2, d_attn, drest, dw_out, dw_pa, dw_pb, d_bgate, d_fg, d_lng, d_lnb, d_ws, d_bs, loss_part) = _mid_fwd_bwd(
        xs, tgt, attn_out, qkv, w_pa_full, w_pb_full, w_out_full, b_gate, sgu_ln_g, sgu_ln_b, w_s[0],
        b_s.reshape(N_GROUPS, SGU_CHUNK, 1), final_g.reshape(1, D_MODEL))

    own_pa, own_pb, own_out, tc_pa, tc_pb, tc_out = _reduce_chip(
        "reduce_chip_proj", (dw_pa, dw_pb, dw_out), (1, 1, 0))
    dqkv, dbias, fc_pa, fc_pb, fc_out = _attn_bwd(qkv, bias_table, d_attn, (tc_pa, tc_pb, tc_out))
    d_rel = _bias_grad(dbias)
    dw_in_t = _proj_bwd_w(h, dqkv, drest)
    own_in, tc_in = _reduce_chip("reduce_chip_in", (dw_in_t,), (0,))
    grad_x, _, fc_in, slabs, ws_all, ng_all = _proj_bwd_x(
        dqkv, drest, xs, dx2, norm_g, w_in_t_full, (tc_in,),
        (d_bgate, d_rel, d_lng, d_lnb, d_fg, loss_part, d_bs, d_ws.reshape(N_GROUPS * SGU_CHUNK, SGU_CHUNK)))
    big = {"w_in": tuple(jnp.swapaxes(t, 0, 1)[None] for t in _adamw(
        "adamw_w_in", w_in_t, own_in, jnp.swapaxes(m_w_in[0], 0, 1), jnp.swapaxes(v_w_in[0], 0, 1), fc_in))}
    for name, w, g, fc, m, v in (("w_pa", w_pa, own_pa, fc_pa, m_w_pa, v_w_pa),
                                 ("w_pb", w_pb, own_pb, fc_pb, m_w_pb, v_w_pb),
                                 ("w_out", w_out, own_out, fc_out, m_w_out, v_w_out)):
        big[name] = tuple(t[None] for t in _adamw("adamw_" + name, w[0], g, m[0], v[0], fc))

    as_2d = lambda leaves: [a.reshape(shape) for a, (_, shape) in zip(leaves, _SMALL)]
    small_out = _adamw_small(
        slabs, ws_all, ng_all, as_2d((norm_g, b_gate, rel_bias, sgu_ln_g, sgu_ln_b, w_s, b_s, final_g)),
        as_2d((m_norm_g, m_b_gate, m_rel_bias, m_sgu_ln_g, m_sgu_ln_b, m_w_s, m_b_s, m_final_g)),
        as_2d((v_norm_g, v_b_gate, v_rel_bias, v_sgu_ln_g, v_sgu_ln_b, v_w_s, v_b_s, v_final_g)))
    small_index = {name: n for n, (name, _) in enumerate(_SMALL)}

    def leaf(kind, name, like):
        if name in big:
            return big[name][kind]
        return small_out[kind * len(_SMALL) + small_index[name]].reshape(like.shape)

    weights = (("norm_g", norm_g), ("w_in", w_in), ("b_gate", b_gate), ("rel_bias", rel_bias), ("sgu_ln_g", sgu_ln_g),
               ("sgu_ln_b", sgu_ln_b), ("w_s", w_s), ("b_s", b_s), ("w_pa", w_pa), ("w_pb", w_pb), ("w_out", w_out),
               ("final_g", final_g))
    outs = [small_out[-1].reshape(()), grad_x.reshape(x.shape)]
    for kind in range(4):
        outs.extend(leaf(kind, name, like) for name, like in weights)
    return tuple(outs)
```

```python
import functools
import math

import jax
import jax.numpy as jnp
from jax import lax
from jax.experimental import pallas as pl
from jax.experimental.pallas import tpu as pltpu

F32 = jnp.float32
BF16 = jnp.bfloat16
MESH = pl.DeviceIdType.MESH
N_DEV = 8

D_MODEL = 1024
D_A = 512
D_B = 512
D_IN = 5632
N_HEADS = 8
HEAD_DIM = 64
CHUNK = 64
N_PREV = 8
REL_CLIP = 128
N_REL = 2 * REL_CLIP + 1
N_REL_PAD = 384
SGU_CHUNK = 128
N_GROUPS = 4
EPS = 1e-6
NEG_INF = -1e30
Q_SCALE = HEAD_DIM ** -0.5

Q_BLOCK = 256
K_SPAN = 768
Z_PAD = K_SPAN - Q_BLOCK
ROLL_W = 1024
COL_BLOCK = 512
N_COL_BLOCKS = D_IN // COL_BLOCK
REST = D_IN - 3 * D_A
TOKEN_TILE = 256
V7X_VMEM_BYTES = 64 * 1024 * 1024

ADAM_LR = 0.001
ADAM_B1 = 0.9
ADAM_B2 = 0.999
ADAM_EPS = 1e-08
ADAM_WD = 0.01
ADAM_STEP = 10

GELU_C = math.sqrt(2.0 / math.pi)
GELU_A = 0.044715

NT = (((1,), (1,)), ((), ()))
TN = (((0,), (0,)), ((), ()))
HIGHEST = lax.Precision.HIGHEST


def _params(vmem_mb, **kw):
    return pltpu.CompilerParams(vmem_limit_bytes=vmem_mb * 1024 * 1024, **kw)


def _dot(a, b, dims=None):
    if dims is None:
        return jnp.dot(a, b, preferred_element_type=F32)
    return lax.dot_general(a, b, dims, preferred_element_type=F32)


def _sigmoid(x):
    return 1.0 / (1.0 + jnp.exp(-x))


def _gelu_and_grad(u):
    u2 = u * u
    t = jnp.tanh(GELU_C * (u + GELU_A * u * u2))
    half = 0.5 * (1.0 + t)
    g = u * half
    dg = half + 0.5 * u * (1.0 - t * t) * (GELU_C * (1.0 + 3.0 * GELU_A * u2))
    return g, dg


def _my_pos():
    return lax.axis_index("x"), lax.axis_index("y"), lax.axis_index("c")


def _flat_id(pos):
    return 4 * pos[0] + 2 * pos[1] + pos[2]


def _peer(pos, k):
    x, y, c = pos
    return (1 - x if k & 4 else x, 1 - y if k & 2 else y, 1 - c if k & 1 else c)


def _other_chips(pos):
    x, y, _ = pos
    return ((1 - x, y), (x, 1 - y), (1 - x, 1 - y))


class _SlotGather:
    def __init__(self, bufs, send_sems, recv_sems, own=None):
        self.bufs, self.send_sems, self.recv_sems = bufs, send_sems, recv_sems
        self.own = own if own is not None else [None] * len(bufs)
        x, y, c = _my_pos()
        self.c, self.me, self.sib = c, (x, y, c), (x, y, 1 - c)
        self.chips = _other_chips(self.me)

    def _copy(self, a, k, block, to):
        slot = _flat_id(block)
        src = self.own[a] if (k < 4 and self.own[a] is not None) else self.bufs[a].at[slot]
        return pltpu.make_async_remote_copy(
            src_ref=src, dst_ref=self.bufs[a].at[slot],
            send_sem=self.send_sems.at[a, k], recv_sem=self.recv_sems.at[a, k], device_id=to, device_id_type=MESH)

    def _own_sends(self):
        n = len(self.bufs)
        return ([self._copy(a, 1 + j, self.me, (*chip, self.c)) for j, chip in enumerate(self.chips) for a in range(n)]
                + [self._copy(a, 0, self.me, self.sib) for a in range(n)])

    def _passes(self):
        return [self._copy(a, 4 + j, (*chip, self.c), self.sib)
                for j, chip in enumerate(self.chips) for a in range(len(self.bufs))]

    def start(self):
        for cp in self._own_sends():
            cp.start()

    def pass_on(self):
        for j, chip in enumerate(self.chips):
            for a in range(len(self.bufs)):
                self._copy(a, 1 + j, (*chip, self.c), self.me).wait_recv()
                self._copy(a, 4 + j, (*chip, self.c), self.sib).start()

    def finish(self):
        for a in range(len(self.bufs)):
            self._copy(a, 0, self.sib, self.me).wait_recv()
            for j, chip in enumerate(self.chips):
                self._copy(a, 4 + j, (*chip, 1 - self.c), self.me).wait_recv()
        for cp in self._own_sends() + self._passes():
            cp.wait_send()


def _reduce_chip(name, parts, sharded_dim):
    n = len(parts)
    shapes = []
    for p, dim in zip(parts, sharded_dim):
        shape = list(p.shape)
        shape[dim] //= N_DEV
        shapes.append(tuple(shape))

    def body(*refs):
        full, own, to_chip = refs[:n], refs[n:2 * n], refs[2 * n:3 * n]
        ins, from_sib = refs[3 * n:4 * n], refs[4 * n:5 * n]
        send_sems, recv_sems = refs[5 * n], refs[5 * n + 1]
        x, y, c = _my_pos()
        sib = (x, y, 1 - c)
        chips = ((x, y),) + _other_chips((x, y, c))
        for a in range(n):
            rows, cols = shapes[a]
            for d in range(N_DEV):
                if sharded_dim[a] == 0:
                    ins[a][d] = full[a][d * rows:(d + 1) * rows, :].astype(BF16)
                else:
                    ins[a][d] = full[a][:, d * cols:(d + 1) * cols].astype(BF16)

        def to_sibling(a, r):
            return pltpu.make_async_remote_copy(
                src_ref=ins[a].at[_flat_id((*chips[r], 1 - c))], dst_ref=from_sib[a].at[r],
                send_sem=send_sems.at[a, r], recv_sem=recv_sems.at[a, r], device_id=sib, device_id_type=MESH)

        sends = [to_sibling(a, r) for r in (1, 2, 3, 0) for a in range(n)]
        for cp in sends:
            cp.start()
        for r in (1, 2, 3, 0):
            for a in range(n):
                to_sibling(a, r).wait_recv()
                both = ins[a][_flat_id((*chips[r], c))].astype(F32) + from_sib[a][r].astype(F32)
                if r == 0:
                    own[a][...] = both
                else:
                    to_chip[a][r - 1] = both.astype(BF16)
        for cp in sends:
            cp.wait_send()

    vmem = pl.BlockSpec(memory_space=pltpu.VMEM)
    return pl.pallas_call(
        body, name=name,
        out_shape=tuple(jax.ShapeDtypeStruct(sh, F32) for sh in shapes)
        + tuple(jax.ShapeDtypeStruct((3,) + sh, BF16) for sh in shapes),
        in_specs=[vmem] * n, out_specs=tuple([vmem] * (2 * n)),
        scratch_shapes=[pltpu.VMEM((N_DEV,) + sh, BF16) for sh in shapes]
        + [pltpu.VMEM((4,) + sh, BF16) for sh in shapes]
        + [pltpu.SemaphoreType.DMA((n, 4)), pltpu.SemaphoreType.DMA((n, 4))],
        compiler_params=_params(56),
    )(*parts)


def _owner_copies(to_chip, from_chip, send_sems, recv_sems):
    x, y, c = _my_pos()
    return [pltpu.make_async_remote_copy(
        src_ref=to_chip[a].at[j], dst_ref=from_chip[a].at[j],
        send_sem=send_sems.at[a, j], recv_sem=recv_sems.at[a, j], device_id=(*chip, c), device_id_type=MESH)
        for a in range(len(to_chip)) for j, chip in enumerate(_other_chips((x, y, c)))]


ROW_NORM_G, ROW_B_GATE, ROW_LN_G, ROW_LN_B, ROW_FINAL_G, ROW_LOSS, ROW_REL, ROW_B_S, SLAB_ROWS = 0, 1, 3, 4, 5, 6, 8, 16, 24


def _rel_index(e):
    lo, hi = Z_PAD - REL_CLIP, Z_PAD + REL_CLIP
    return jnp.where(e <= lo, 2 * REL_CLIP, jnp.where(e < hi, hi - e, jnp.where(e <= K_SPAN, 0, 2 * REL_CLIP)))


def _bias_table(rel_bias_pad):
    def body(rb_ref, bt_ref):
        c = lax.broadcasted_iota(jnp.int32, (N_REL_PAD, ROLL_W), 1)
        r = lax.broadcasted_iota(jnp.int32, (N_REL_PAD, ROLL_W), 0)
        pick = (r == _rel_index(c)).astype(F32)
        rows = jnp.dot(rb_ref[...], pick, precision=HIGHEST, preferred_element_type=F32)
        qc = lax.broadcasted_iota(jnp.int32, (Q_BLOCK, K_SPAN), 0) >> 6
        kc = lax.broadcasted_iota(jnp.int32, (Q_BLOCK, K_SPAN), 1) >> 6
        band = (kc >= qc) & (kc <= qc + N_PREV)
        for h in range(N_HEADS):
            t = jnp.broadcast_to(rows[h:h + 1, :], (Q_BLOCK, ROLL_W))
            t = pltpu.roll(t, 0, 1, stride=1, stride_axis=0)
            bt_ref[h] = jnp.where(band, t[:, :K_SPAN], NEG_INF)

    return pl.pallas_call(
        body, name="bias_table",
        out_shape=jax.ShapeDtypeStruct((N_HEADS, Q_BLOCK, K_SPAN), F32),
        compiler_params=_params(32),
    )(rel_bias_pad)


def _bias_grad(dbias):
    def body(a_ref, o_ref):
        rr = lax.broadcasted_iota(jnp.int32, (Q_BLOCK, Q_BLOCK), 0)
        cc = lax.broadcasted_iota(jnp.int32, (Q_BLOCK, Q_BLOCK), 1)
        flip = (rr + cc == Q_BLOCK - 1).astype(F32)
        c = lax.broadcasted_iota(jnp.int32, (ROLL_W, N_REL_PAD), 0)
        r = lax.broadcasted_iota(jnp.int32, (ROLL_W, N_REL_PAD), 1)
        e = jnp.where(c >= Q_BLOCK - 1, c - (Q_BLOCK - 1), c + (ROLL_W - Q_BLOCK + 1))
        pick = (r == _rel_index(e)).astype(F32)
        sums = []
        for h in range(N_HEADS):
            a = jnp.dot(flip, a_ref[h], precision=HIGHEST, preferred_element_type=F32)
            a = jnp.concatenate([a, jnp.zeros((Q_BLOCK, ROLL_W - K_SPAN), F32)], axis=1)
            a = pltpu.roll(a, 0, 1, stride=1, stride_axis=0)
            sums.append(jnp.sum(a, axis=0, keepdims=True))
        diag = jnp.concatenate(sums, axis=0)
        o_ref[...] = jnp.dot(diag, pick, precision=HIGHEST, preferred_element_type=F32)

    return pl.pallas_call(
        body, name="bias_grad",
        out_shape=jax.ShapeDtypeStruct((N_HEADS, N_REL_PAD), F32),
        compiler_params=_params(32),
    )(dbias)


def _gather_proj_fwd(x, norm_g, w_in_t):
    s = x.shape[0]
    tm = 512 if s % 512 == 0 else TOKEN_TILE
    nt = s // tm
    n_pad = Z_PAD // tm
    shard_w = w_in_t.shape[0]
    chip_w = 2 * shard_w
    n_chips = N_DEV // 2

    def body(order_ref, x_ref, g_ref, win_hbm, z_ref, h_ref, wt_hbm, stage, wchip, hb, win_f32, send_sems, recv_sems,
             local_sems):
        j = pl.program_id(0)
        i = pl.program_id(1)
        x_, y_, c_ = _my_pos()
        me, sib = (x_, y_, c_), (x_, y_, 1 - c_)
        near = _other_chips(me)
        pick = lambda a, b: tuple(jnp.where(c_ == 0, u, v) for u, v in zip(a, b))
        passed_from, passed_to = pick(near[0], near[1]), pick(near[1], near[0])

        def rows_of(block):
            return wt_hbm.at[pl.ds(pl.multiple_of(_flat_id(block) * shard_w, 16), shard_w), :]

        def copy(k, block, to, own=False):
            return pltpu.make_async_remote_copy(
                src_ref=stage if own else rows_of(block), dst_ref=rows_of(block),
                send_sem=send_sems.at[k], recv_sem=recv_sems.at[k], device_id=to, device_id_type=MESH)

        def sends():
            return ([copy(0, me, sib, True), copy(1, me, (*near[0], c_), True), copy(2, me, (*near[1], c_), True),
                     copy(3, (*passed_from, c_), (*passed_to, c_))]
                    + [copy(4 + n, (*near[n], c_), sib) for n in range(3)])

        keep = pltpu.make_async_copy(stage, rows_of(me), local_sems.at[0])

        def fetch(chip):
            first = pl.multiple_of((2 * chip[0] + chip[1]) * chip_w, 16)
            cp = pltpu.make_async_copy(wt_hbm.at[pl.ds(first, chip_w), :], wchip, local_sems.at[1])
            cp.start()
            cp.wait()

        @pl.when((j == 0) & (i == 0))
        def _():
            load = pltpu.make_async_copy(win_hbm, win_f32, local_sems.at[1])
            load.start()
            load.wait()
            stage[...] = win_f32[...].astype(BF16)
            keep.start()
            for cp in sends()[:3]:
                cp.start()
            copy(0, sib, me).wait_recv()
            keep.wait()
            fetch((x_, y_))

        @pl.when((j == 1) & (i == 0))
        def _():
            copy(1, (*near[0], c_), me).wait_recv()
            copy(2, (*near[1], c_), me).wait_recv()
            for cp in sends()[3:6]:
                cp.start()
            copy(4, (*near[0], 1 - c_), me).wait_recv()
            fetch(near[0])

        @pl.when((j == 2) & (i == 0))
        def _():
            copy(5, (*near[1], 1 - c_), me).wait_recv()
            fetch(near[1])

        @pl.when((j == 3) & (i == 0))
        def _():
            copy(3, (*near[2], c_), me).wait_recv()
            copy(6, (*near[2], c_), sib).start()
            copy(6, (*near[2], 1 - c_), me).wait_recv()
            fetch(near[2])

        @pl.when(i < n_pad)
        def _():
            z_ref[...] = jnp.zeros(z_ref.shape, BF16)

        @pl.when(i >= n_pad)
        def _():
            rows = pl.ds(pl.multiple_of((i - n_pad) * tm, tm), tm)

            @pl.when(j == 0)
            def _():
                xf = x_ref[...]
                r = lax.rsqrt(jnp.mean(xf * xf, axis=-1, keepdims=True) + EPS)
                hf = (xf * r * g_ref[...]).astype(BF16)
                hb[rows, :] = hf
                h_ref[...] = hf

            blk = _dot(hb[rows, :], wchip[...], NT)
            q_scale = jnp.where(order_ref[j] == 0, Q_SCALE, 1.0).astype(F32)
            z_ref[:, :D_A] = (blk[:, :D_A] * q_scale).astype(BF16)
            z_ref[:, D_A:] = blk[:, D_A:].astype(BF16)

        @pl.when((j == n_chips - 1) & (i == n_pad + nt - 1))
        def _():
            for cp in sends():
                cp.wait_send()

    pos = _my_pos()
    order = jnp.stack([2 * cx + cy for cx, cy in ((pos[0], pos[1]),) + _other_chips(pos)]).astype(jnp.int32)
    first_pass = lambda j, i: jnp.where(j == 0, jnp.maximum(i - n_pad, 0), nt - 1)
    grid_spec = pltpu.PrefetchScalarGridSpec(
        num_scalar_prefetch=1,
        grid=(n_chips, n_pad + nt),
        in_specs=[pl.BlockSpec((tm, D_MODEL), lambda j, i, o: (first_pass(j, i), 0)),
                  pl.BlockSpec((1, D_MODEL), lambda j, i, o: (0, 0)),
                  pl.BlockSpec(memory_space=pl.ANY)],
        out_specs=(pl.BlockSpec((tm, chip_w), lambda j, i, o: (i, o[j])),
                   pl.BlockSpec((tm, D_MODEL), lambda j, i, o: (first_pass(j, i), 0)),
                   pl.BlockSpec(memory_space=pl.ANY)),
        scratch_shapes=[pltpu.VMEM((shard_w, D_MODEL), BF16), pltpu.VMEM((chip_w, D_MODEL), BF16),
                        pltpu.VMEM((s, D_MODEL), BF16), pltpu.VMEM(w_in_t.shape, F32),
                        pltpu.SemaphoreType.DMA((N_DEV - 1,)), pltpu.SemaphoreType.DMA((N_DEV - 1,)),
                        pltpu.SemaphoreType.DMA((2,))])
    return pl.pallas_call(
        body, name="gather_proj_fwd",
        grid_spec=grid_spec,
        out_shape=(jax.ShapeDtypeStruct((Z_PAD + s, D_IN), BF16), jax.ShapeDtypeStruct((s, D_MODEL), BF16),
                   jax.ShapeDtypeStruct((D_IN, D_MODEL), BF16)),
        compiler_params=_params(60),
    )(order, x, norm_g, w_in_t)


def _attn_specs(rows):
    pairs = N_HEADS // 2
    return ([pl.BlockSpec((rows, 128), functools.partial(lambda which, p: (0, which * pairs + p), which))
             for which in range(3)]
            + [pl.BlockSpec((2, Q_BLOCK, K_SPAN), lambda p: (p, 0, 0))])


def _head_masks():
    lane = lax.broadcasted_iota(jnp.int32, (1, 128), 1)
    first = lane < HEAD_DIM
    return (first, jnp.logical_not(first))


def _stack_heads(x, masks):
    zero = jnp.zeros((), x.dtype)
    return jnp.concatenate([jnp.where(m, x, zero) for m in masks], axis=0)


STRIP = 16


def _softmax_strips(s_ref, bias_ref, b):
    valid = lax.broadcasted_iota(jnp.int32, (1, K_SPAN), 1) >= Z_PAD - b * Q_BLOCK
    for t in range(2 * Q_BLOCK // STRIP):
        hh, r = divmod(t * STRIP, Q_BLOCK)
        st = s_ref[t * STRIP:(t + 1) * STRIP, :] + bias_ref[hh, r:r + STRIP, :]
        st = jnp.where(valid, st, NEG_INF)
        e = jnp.exp(st - jnp.max(st, axis=-1, keepdims=True))
        yield e * (1.0 / jnp.sum(e, axis=-1, keepdims=True))


def _side_by_side_strips(strips):
    half = len(strips) // 2
    return jnp.concatenate([jnp.concatenate([a, c], axis=1) for a, c in zip(strips[:half], strips[half:])], axis=0)


def _attn_fwd(qkv, bias_table, shards):
    s = qkv.shape[0] - Z_PAD
    nb = s // Q_BLOCK
    n = len(shards)
    pairs = N_HEADS // 2

    def body(*refs):
        q_ref, k_ref, v_ref, bt_ref = refs[:4]
        shard_refs = refs[4:4 + n]
        o_ref = refs[4 + n]
        slot_refs = refs[5 + n:5 + 2 * n]
        stages = refs[5 + 2 * n:5 + 3 * n]
        s_scr, send_sems, recv_sems, local_sems = refs[5 + 3 * n:]
        p_id = pl.program_id(0)
        gather = _SlotGather(slot_refs, send_sems, recv_sems, own=stages)
        keep = [pltpu.make_async_copy(stages[a], slot_refs[a].at[_flat_id(_my_pos())], local_sems.at[a])
                for a in range(n)]

        @pl.when(p_id == 0)
        def _():
            for a in range(n):
                stages[a][...] = shard_refs[a][...].astype(BF16)
                keep[a].start()
            gather.start()

        @pl.when(p_id == 1)
        def _():
            gather.pass_on()

        masks = _head_masks()

        def scores(b, half):
            r0 = pl.multiple_of(b * Q_BLOCK, Q_BLOCK)
            q2 = _stack_heads(q_ref[pl.ds(r0 + Z_PAD, Q_BLOCK), :], masks)
            s_scr[half] = _dot(q2, k_ref[pl.ds(r0, K_SPAN), :], NT)

        def finish(b, half):
            r0 = pl.multiple_of(b * Q_BLOCK, Q_BLOCK)
            v2 = _stack_heads(v_ref[pl.ds(r0, K_SPAN), :], masks)
            p = [st.astype(BF16) for st in _softmax_strips(s_scr.at[half], bt_ref, b)]
            o_ref[pl.ds(r0, Q_BLOCK), :] = _dot(_side_by_side_strips(p), v2)

        def two_blocks(i, carry):
            b = 2 * i
            scores(b + 1, 1)
            finish(b, 0)
            scores(jnp.minimum(b + 2, nb - 1), 0)
            finish(b + 1, 1)
            return carry

        scores(0, 0)
        lax.fori_loop(0, nb // 2, two_blocks, 0)

        @pl.when(p_id == pairs - 1)
        def _():
            gather.finish()
            for cp in keep:
                cp.wait()

    hbm = pl.BlockSpec(memory_space=pl.ANY)
    return pl.pallas_call(
        body, name="attn_fwd",
        grid=(pairs,),
        in_specs=_attn_specs(s + Z_PAD) + [pl.BlockSpec(a.shape, lambda p: (0, 0)) for a in shards],
        out_specs=(pl.BlockSpec((s, 128), lambda p: (0, p)),) + (hbm,) * n,
        out_shape=(jax.ShapeDtypeStruct((s, D_A), F32),)
        + tuple(jax.ShapeDtypeStruct((N_DEV,) + a.shape, BF16) for a in shards),
        scratch_shapes=[pltpu.VMEM(a.shape, BF16) for a in shards]
        + [pltpu.VMEM((2, 2 * Q_BLOCK, K_SPAN), F32),
           pltpu.SemaphoreType.DMA((n, N_DEV - 1)), pltpu.SemaphoreType.DMA((n, N_DEV - 1)),
           pltpu.SemaphoreType.DMA((n,))],
        compiler_params=_params(48),
    )(qkv, qkv, qkv, bias_table, *shards)


def _attn_bwd(qkv, bias_table, d_out, to_chip):
    s = qkv.shape[0] - Z_PAD
    nb = s // Q_BLOCK
    n = len(to_chip)
    pairs = N_HEADS // 2

    def body(*refs):
        q_ref, k_ref, v_ref, bt_ref, do_ref = refs[:5]
        to_chip_refs = refs[5:5 + n]
        dqkv_ref, db_ref = refs[5 + n:7 + n]
        from_chip_refs = refs[7 + n:7 + 2 * n]
        dk_acc, dv_acc, s_scr, dp_scr, send_sems, recv_sems = refs[7 + 2 * n:]
        p_id = pl.program_id(0)

        @pl.when(p_id == 0)
        def _():
            for cp in _owner_copies(to_chip_refs, from_chip_refs, send_sems, recv_sems):
                cp.start()

        dk_acc[...] = jnp.zeros(dk_acc.shape, F32)
        dv_acc[...] = jnp.zeros(dv_acc.shape, F32)
        db_ref[...] = jnp.zeros(db_ref.shape, F32)
        masks = _head_masks()

        def operands(b):
            r0 = pl.multiple_of(b * Q_BLOCK, Q_BLOCK)
            q2 = _stack_heads(q_ref[pl.ds(r0 + Z_PAD, Q_BLOCK), :], masks)
            do2 = _stack_heads(do_ref[pl.ds(r0, Q_BLOCK), :], masks)
            return r0, q2, do2, k_ref[pl.ds(r0, K_SPAN), :]

        def ahead(b, half):
            r0, q2, do2, kcat = operands(b)
            s_scr[half] = _dot(q2, kcat, NT)
            dp_scr[half] = _dot(do2, v_ref[pl.ds(r0, K_SPAN), :], NT)

        def finish(b, half):
            r0, q2, do2, kcat = operands(b)
            p_strips, ds_strips = [], []
            for t, p in enumerate(_softmax_strips(s_scr.at[half], bt_ref, b)):
                hh, r = divmod(t * STRIP, Q_BLOCK)
                dp_t = dp_scr[half, t * STRIP:(t + 1) * STRIP, :]
                ds = p * (dp_t - jnp.sum(p * dp_t, axis=-1, keepdims=True))
                db_ref[hh, r:r + STRIP, :] += ds
                p_strips.append(p.astype(BF16))
                ds_strips.append(ds.astype(BF16))
            dq = _dot(_side_by_side_strips(ds_strips), _stack_heads(kcat, masks))
            dqkv_ref[0, pl.ds(r0, Q_BLOCK), :] = (dq * Q_SCALE).astype(BF16)
            dk_acc[pl.ds(r0, K_SPAN), :] += _dot(jnp.concatenate(ds_strips, axis=0), q2, TN)
            dv_acc[pl.ds(r0, K_SPAN), :] += _dot(jnp.concatenate(p_strips, axis=0), do2, TN)

        def two_blocks(i, carry):
            b = 2 * i
            ahead(b + 1, 1)
            finish(b, 0)
            ahead(jnp.minimum(b + 2, nb - 1), 0)
            finish(b + 1, 1)
            return carry

        ahead(0, 0)
        lax.fori_loop(0, nb // 2, two_blocks, 0)
        dqkv_ref[1] = dk_acc[Z_PAD:, :].astype(BF16)
        dqkv_ref[2] = dv_acc[Z_PAD:, :].astype(BF16)

        @pl.when(p_id == pairs - 1)
        def _():
            for cp in _owner_copies(to_chip_refs, from_chip_refs, send_sems, recv_sems):
                cp.wait_recv()
                cp.wait_send()

    hbm = pl.BlockSpec(memory_space=pl.ANY)
    return pl.pallas_call(
        body, name="attn_bwd",
        grid=(pairs,),
        in_specs=_attn_specs(s + Z_PAD) + [pl.BlockSpec((s, 128), lambda p: (0, p))] + [hbm] * n,
        out_specs=(pl.BlockSpec((3, s, 128), lambda p: (0, 0, p)),
                   pl.BlockSpec((2, Q_BLOCK, K_SPAN), lambda p: (p, 0, 0))) + (hbm,) * n,
        out_shape=(jax.ShapeDtypeStruct((3, s, D_A), BF16),
                   jax.ShapeDtypeStruct((N_HEADS, Q_BLOCK, K_SPAN), F32))
        + tuple(jax.ShapeDtypeStruct(t.shape, t.dtype) for t in to_chip),
        scratch_shapes=[pltpu.VMEM((s + Z_PAD, 128), F32), pltpu.VMEM((s + Z_PAD, 128), F32),
                        pltpu.VMEM((2, 2 * Q_BLOCK, K_SPAN), F32), pltpu.VMEM((2, 2 * Q_BLOCK, K_SPAN), F32),
                        pltpu.SemaphoreType.DMA((n, 3)), pltpu.SemaphoreType.DMA((n, 3))],
        compiler_params=_params(56),
    )(qkv, qkv, qkv, bias_table, d_out, *to_chip)


def _mid_fwd_bwd(x, target, attn_out, z, w_pa, w_pb, w_out, b_gate, ln_g, ln_b, w_s, b_s, final_g):
    s = x.shape[0]
    tm = TOKEN_TILE
    nt = s // tm

    def body(x_ref, t_ref, oa_ref, ga_ref, ub_ref, vb_ref, gb_ref, ta0_ref, ta1_ref, tb0_ref, tb1_ref,
             wpa_hbm, wpb_hbm, wout_hbm, bg_ref, lng_ref, lnb_ref, ws_ref, bs_ref, fg_ref,
             dx2_ref, doa_ref, dz_ref, dwout_hbm, dwpa_hbm, dwpb_hbm, dbg_ref, dfg_ref, dlng_ref, dlnb_ref, dws_ref,
             dbs_ref, loss_ref,
             wpa, wpb, wout, wmix, acc_out, acc_pa, acc_pb, sem):
        i = pl.program_id(0)

        @pl.when(i == 0)
        def _():
            loads = [pltpu.make_async_copy(src, dst, sem.at[n])
                     for n, (src, dst) in enumerate(((wpa_hbm, wpa), (wpb_hbm, wpb), (wout_hbm, wout)))]
            for cp in loads:
                cp.start()
            t_idx = lax.broadcasted_iota(jnp.int32, (SGU_CHUNK, SGU_CHUNK), 0)
            s_idx = lax.broadcasted_iota(jnp.int32, (SGU_CHUNK, SGU_CHUNK), 1)
            for g in range(N_GROUPS):
                wmix[g] = jnp.where(s_idx <= t_idx, ws_ref[g], 0.0).astype(BF16)
            for ref in (acc_out, acc_pa, acc_pb, dbg_ref, dfg_ref, dlng_ref, dlnb_ref, dws_ref, dbs_ref, loss_ref):
                ref[...] = jnp.zeros(ref.shape, F32)
            for cp in loads:
                cp.wait()

        def tile_fwd_bwd(rows):
            g_a = ga_ref[rows, :].astype(F32)
            u_b = ub_ref[rows, :].astype(F32)
            v_b = vb_ref[rows, :].astype(F32)
            g_b = gb_ref[rows, :].astype(F32)
            bg = bg_ref[...]
            sg_a = _sigmoid(g_a)
            silu_a = g_a * sg_a
            o_a = oa_ref[rows, :]
            y_a = (o_a * silu_a).astype(BF16)
            ug, dgelu_u = _gelu_and_grad(u_b)
            vg, dgelu_v = _gelu_and_grad(v_b)
            mu = jnp.mean(vg, axis=-1, keepdims=True)
            vc = vg - mu
            rstd = lax.rsqrt(jnp.mean(vc * vc, axis=-1, keepdims=True) + EPS)
            vhat = vc * rstd
            lng = lng_ref[...]
            vn = (vhat * lng + lnb_ref[...]).astype(BF16)
            sg_b = _sigmoid(g_b)
            silu_b = g_b * sg_b
            subs = [slice(n * SGU_CHUNK, (n + 1) * SGU_CHUNK) for n in range(tm // SGU_CHUNK)]
            mixed = jnp.concatenate([jnp.concatenate(
                [_dot(wmix[g], vn[sub, g * 128:(g + 1) * 128]) + bs_ref[g] for g in range(N_GROUPS)], axis=1)
                for sub in subs], axis=0)
            um = ug * mixed
            y_b = (um * silu_b).astype(BF16)
            gate_a = _sigmoid(jnp.concatenate([ta0_ref[rows, :], ta1_ref[rows, :]], axis=1).astype(F32)
                              + bg[:, :D_MODEL])
            gate_b = _sigmoid(jnp.concatenate([tb0_ref[rows, :], tb1_ref[rows, :]], axis=1).astype(F32)
                              + bg[:, D_MODEL:])
            p_a = _dot(y_a, wpa[...])
            p_b = _dot(y_b, wpb[...])
            merged = (gate_a * p_a + gate_b * p_b).astype(BF16)
            x2 = x_ref[rows, :] + _dot(merged, wout[...])
            r2 = lax.rsqrt(jnp.mean(x2 * x2, axis=-1, keepdims=True) + EPS)
            xh = x2 * r2
            fg = fg_ref[...]
            err = xh * fg - t_ref[rows, :]
            loss_ref[...] += jnp.sum(jnp.sum(err * err, axis=-1, keepdims=True), axis=0, keepdims=True) * (0.5 / D_MODEL)
            dy = err * (1.0 / D_MODEL)
            dfg_ref[...] += jnp.sum(dy * xh, axis=0, keepdims=True)
            gy = dy * fg
            dx2 = r2 * (gy - xh * jnp.mean(gy * xh, axis=-1, keepdims=True))
            dx2_ref[rows, :] = dx2
            dx2b = dx2.astype(BF16)
            dmerged = _dot(dx2b, wout[...], NT)
            acc_out[...] += _dot(merged, dx2b, TN)
            dp_a = dmerged * gate_a
            dp_b = dmerged * gate_b
            dgate_a = dp_a * p_a * (1.0 - gate_a)
            dgate_b = dp_b * p_b * (1.0 - gate_b)
            dbg_ref[:, :D_MODEL] += jnp.sum(dgate_a, axis=0, keepdims=True)
            dbg_ref[:, D_MODEL:] += jnp.sum(dgate_b, axis=0, keepdims=True)
            dz_ref[rows, 2048:3072] = dgate_a.astype(BF16)
            dz_ref[rows, 3072:4096] = dgate_b.astype(BF16)
            dp_ab = dp_a.astype(BF16)
            dp_bb = dp_b.astype(BF16)
            dy_a = _dot(dp_ab, wpa[...], NT)
            dy_b = _dot(dp_bb, wpb[...], NT)
            acc_pa[...] += _dot(y_a, dp_ab, TN)
            acc_pb[...] += _dot(y_b, dp_bb, TN)
            doa_ref[rows, :] = (dy_a * silu_a).astype(BF16)
            dz_ref[rows, 0:512] = (dy_a * o_a * (sg_a * (1.0 + g_a * (1.0 - sg_a)))).astype(BF16)
            dz_ref[rows, 1536:2048] = (dy_b * um * (sg_b * (1.0 + g_b * (1.0 - sg_b)))).astype(BF16)
            dys = dy_b * silu_b
            dz_ref[rows, 512:1024] = (dys * mixed * dgelu_u).astype(BF16)
            dmixed = dys * ug
            dmb = dmixed.astype(BF16)
            dvn_rows = []
            for sub in subs:
                dvn_parts = []
                for g in range(N_GROUPS):
                    cols = slice(g * 128, (g + 1) * 128)
                    dws_ref[g] += _dot(dmb[sub, cols], vn[sub, cols], NT)
                    dbs_ref[g] += jnp.sum(dmixed[sub, cols], axis=-1, keepdims=True)
                    dvn_parts.append(_dot(wmix[g], dmb[sub, cols], TN))
                dvn_rows.append(jnp.concatenate(dvn_parts, axis=1))
            dvn = jnp.concatenate(dvn_rows, axis=0)
            dlng_ref[...] += jnp.sum(dvn * vhat, axis=0, keepdims=True)
            dlnb_ref[...] += jnp.sum(dvn, axis=0, keepdims=True)
            dvh = dvn * lng
            dvg = rstd * (dvh - jnp.mean(dvh, axis=-1, keepdims=True)
                          - vhat * jnp.mean(dvh * vhat, axis=-1, keepdims=True))
            dz_ref[rows, 1024:1536] = (dvg * dgelu_v).astype(BF16)

        tile_fwd_bwd(slice(0, tm))

        @pl.when(i == nt - 1)
        def _():
            t_idx = lax.broadcasted_iota(jnp.int32, (SGU_CHUNK, SGU_CHUNK), 0)
            s_idx = lax.broadcasted_iota(jnp.int32, (SGU_CHUNK, SGU_CHUNK), 1)
            for g in range(N_GROUPS):
                dws_ref[g] = jnp.where(s_idx <= t_idx, dws_ref[g], 0.0)
            stores = [pltpu.make_async_copy(src, dst, sem.at[n])
                      for n, (src, dst) in enumerate(((acc_out, dwout_hbm), (acc_pa, dwpa_hbm), (acc_pb, dwpb_hbm)))]
            for cp in stores:
                cp.start()
            for cp in stores:
                cp.wait()

    tile = lambda w: pl.BlockSpec((tm, w), lambda i: (i, 0))
    whole = lambda shape: pl.BlockSpec(shape, lambda i: (0,) * len(shape))
    hbm = pl.BlockSpec(memory_space=pl.ANY)
    return pl.pallas_call(
        body, name="mid_fwd_bwd",
        grid=(nt,),
        in_specs=[tile(D_MODEL), tile(D_MODEL), tile(D_A)]
        + [pl.BlockSpec((tm, COL_BLOCK), functools.partial(lambda c, i: (i + Z_PAD // tm, c), c))
           for c in range(3, N_COL_BLOCKS)]
        + [hbm, hbm, hbm,
                  whole((1, 2 * D_MODEL)), whole((1, D_B)), whole((1, D_B)),
                  whole((N_GROUPS, SGU_CHUNK, SGU_CHUNK)), whole((N_GROUPS, SGU_CHUNK, 1)), whole((1, D_MODEL))],
        out_specs=(tile(D_MODEL), tile(D_A), tile(REST), hbm, hbm, hbm,
                   whole((1, 2 * D_MODEL)), whole((1, D_MODEL)), whole((1, D_B)), whole((1, D_B)),
                   whole((N_GROUPS, SGU_CHUNK, SGU_CHUNK)), whole((N_GROUPS, SGU_CHUNK, 1)), whole((1, 1))),
        out_shape=(jax.ShapeDtypeStruct((s, D_MODEL), F32), jax.ShapeDtypeStruct((s, D_A), BF16),
                   jax.ShapeDtypeStruct((s, REST), BF16),
                   jax.ShapeDtypeStruct((D_MODEL, D_MODEL), F32), jax.ShapeDtypeStruct((D_A, D_MODEL), F32),
                   jax.ShapeDtypeStruct((D_B, D_MODEL), F32),
                   jax.ShapeDtypeStruct((1, 2 * D_MODEL), F32), jax.ShapeDtypeStruct((1, D_MODEL), F32),
                   jax.ShapeDtypeStruct((1, D_B), F32), jax.ShapeDtypeStruct((1, D_B), F32),
                   jax.ShapeDtypeStruct((N_GROUPS, SGU_CHUNK, SGU_CHUNK), F32),
                   jax.ShapeDtypeStruct((N_GROUPS, SGU_CHUNK, 1), F32), jax.ShapeDtypeStruct((1, 1), F32)),
        scratch_shapes=[pltpu.VMEM((D_A, D_MODEL), BF16), pltpu.VMEM((D_B, D_MODEL), BF16),
                        pltpu.VMEM((D_MODEL, D_MODEL), BF16), pltpu.VMEM((N_GROUPS, SGU_CHUNK, SGU_CHUNK), BF16),
                        pltpu.VMEM((D_MODEL, D_MODEL), F32), pltpu.VMEM((D_A, D_MODEL), F32),
                        pltpu.VMEM((D_B, D_MODEL), F32),
                        pltpu.SemaphoreType.DMA((3,))],
        compiler_params=_params(56),
    )(x, target, attn_out, *([z] * (N_COL_BLOCKS - 3)), w_pa, w_pb, w_out, b_gate, ln_g, ln_b, w_s, b_s, final_g)


def _proj_bwd_x(dqkv, drest, x, dx2, norm_g, w_in_t, to_chip, small):
    s = x.shape[0]
    tm = 512 if s % 512 == 0 else TOKEN_TILE
    nt = s // tm
    n = len(to_chip)
    ws_shape = (N_GROUPS * SGU_CHUNK, SGU_CHUNK)

    def body(*refs):
        dqkv_ref, dr_ref, x_ref, dx2_ref, g_ref, w_hbm = refs[:6]
        to_chip_refs = refs[6:6 + n]
        bg_ref, rel_ref, lng_ref, lnb_ref, fg_ref, loss_ref, bs_ref, ws_ref = refs[6 + n:14 + n]
        dx_ref, dg_ref = refs[14 + n:16 + n]
        from_chip_refs = refs[16 + n:16 + 2 * n]
        slab_land, ws_land, ng_land = refs[16 + 2 * n:19 + 2 * n]
        (w, slab_stage, ws_stage, ng_stage, sem, send_sems, recv_sems, early_send, early_recv, late_send, late_recv,
         keep_sems) = refs[19 + 2 * n:]
        i = pl.program_id(0)
        me = _flat_id(_my_pos())
        early = _SlotGather([slab_land, ws_land], early_send, early_recv, own=[slab_stage, ws_stage])
        late = _SlotGather([ng_land], late_send, late_recv, own=[ng_stage])
        keep = [pltpu.make_async_copy(stage, land.at[me], keep_sems.at[k]) for k, (stage, land) in enumerate(
            ((slab_stage, slab_land), (ws_stage, ws_land), (ng_stage, ng_land)))]

        @pl.when(i == 0)
        def _():
            cp = pltpu.make_async_copy(w_hbm, w, sem)
            cp.start()
            dg_ref[...] = jnp.zeros(dg_ref.shape, F32)
            slab_stage[...] = jnp.zeros(slab_stage.shape, F32)
            slab_stage[ROW_B_GATE:ROW_B_GATE + 1, :] = bg_ref[:, :D_MODEL]
            slab_stage[ROW_B_GATE + 1:ROW_B_GATE + 2, :] = bg_ref[:, D_MODEL:]
            slab_stage[ROW_LN_G:ROW_LN_G + 1, :D_B] = lng_ref[...]
            slab_stage[ROW_LN_B:ROW_LN_B + 1, :D_B] = lnb_ref[...]
            slab_stage[ROW_FINAL_G:ROW_FINAL_G + 1, :] = fg_ref[...]
            slab_stage[ROW_LOSS:ROW_LOSS + 1, :1] = loss_ref[...]
            slab_stage[ROW_REL:ROW_REL + N_HEADS, :N_REL_PAD] = rel_ref[...]
            eye = (lax.broadcasted_iota(jnp.int32, (SGU_CHUNK, SGU_CHUNK), 0)
                   == lax.broadcasted_iota(jnp.int32, (SGU_CHUNK, SGU_CHUNK), 1))
            for g in range(N_GROUPS):
                row = jnp.sum(jnp.where(eye, bs_ref[g], 0.0), axis=0, keepdims=True)
                slab_stage[ROW_B_S + g:ROW_B_S + g + 1, :SGU_CHUNK] = row
            ws_stage[...] = ws_ref[...]
            keep[0].start()
            keep[1].start()
            early.start()
            for rc in _owner_copies(to_chip_refs, from_chip_refs, send_sems, recv_sems):
                rc.start()
            cp.wait()

        @pl.when(i == nt // 2)
        def _():
            early.pass_on()

        dh = None
        for c in range(N_COL_BLOCKS):
            dz = dqkv_ref[c] if c < 3 else dr_ref[:, (c - 3) * COL_BLOCK:(c - 2) * COL_BLOCK]
            part = _dot(dz, w[c * COL_BLOCK:(c + 1) * COL_BLOCK, :])
            dh = part if dh is None else dh + part
        xf = x_ref[...]
        r = lax.rsqrt(jnp.mean(xf * xf, axis=-1, keepdims=True) + EPS)
        xn = xf * r
        dg_ref[...] += jnp.sum(dh * xn, axis=0, keepdims=True)
        gh = dh * g_ref[...]
        dx_ref[...] = r * (gh - xn * jnp.mean(gh * xn, axis=-1, keepdims=True)) + dx2_ref[...]

        @pl.when(i == nt - 1)
        def _():
            ng_stage[...] = jnp.zeros(ng_stage.shape, F32)
            ng_stage[0:1, :] = dg_ref[...]
            keep[2].start()
            late.start()
            late.pass_on()
            late.finish()
            early.finish()
            for cp in keep:
                cp.wait()
            for rc in _owner_copies(to_chip_refs, from_chip_refs, send_sems, recv_sems):
                rc.wait_recv()
                rc.wait_send()

    hbm = pl.BlockSpec(memory_space=pl.ANY)
    whole = lambda a: pl.BlockSpec(a.shape, lambda i: (0,) * a.ndim)
    lands = ((N_DEV, SLAB_ROWS, D_MODEL), (N_DEV,) + ws_shape, (N_DEV, 8, D_MODEL))
    return pl.pallas_call(
        body, name="proj_bwd_x",
        grid=(nt,),
        in_specs=[pl.BlockSpec((3, tm, D_A), lambda i: (0, i, 0)),
                  pl.BlockSpec((tm, REST), lambda i: (i, 0)),
                  pl.BlockSpec((tm, D_MODEL), lambda i: (i, 0)),
                  pl.BlockSpec((tm, D_MODEL), lambda i: (i, 0)),
                  pl.BlockSpec((1, D_MODEL), lambda i: (0, 0)),
                  hbm] + [hbm] * n + [whole(a) for a in small],
        out_specs=(pl.BlockSpec((tm, D_MODEL), lambda i: (i, 0)),
                   pl.BlockSpec((1, D_MODEL), lambda i: (0, 0))) + (hbm,) * (n + 3),
        out_shape=(jax.ShapeDtypeStruct((s, D_MODEL), F32), jax.ShapeDtypeStruct((1, D_MODEL), F32))
        + tuple(jax.ShapeDtypeStruct(t.shape, t.dtype) for t in to_chip)
        + tuple(jax.ShapeDtypeStruct(shape, F32) for shape in lands),
        scratch_shapes=[pltpu.VMEM((D_IN, D_MODEL), BF16)]
        + [pltpu.VMEM(shape[1:], F32) for shape in lands]
        + [pltpu.SemaphoreType.DMA, pltpu.SemaphoreType.DMA((n, 3)), pltpu.SemaphoreType.DMA((n, 3)),
           pltpu.SemaphoreType.DMA((2, N_DEV - 1)), pltpu.SemaphoreType.DMA((2, N_DEV - 1)),
           pltpu.SemaphoreType.DMA((1, N_DEV - 1)), pltpu.SemaphoreType.DMA((1, N_DEV - 1)),
           pltpu.SemaphoreType.DMA((3,))],
        compiler_params=_params(56),
    )(dqkv, drest, x, dx2, norm_g, w_in_t, *to_chip, *small)


def _proj_bwd_w(h, dqkv, drest):
    s = h.shape[0]
    tk = min(s, 1024)
    nk = s // tk

    def body(h_ref, dqkv_ref, dr_ref, o_ref, acc):
        j = pl.program_id(0)
        i = pl.program_id(1)

        @pl.when(i == 0)
        def _():
            acc[...] = jnp.zeros(acc.shape, F32)

        @pl.when(j < 3)
        def _():
            acc[...] += _dot(dqkv_ref[...], h_ref[...], TN)

        @pl.when(j >= 3)
        def _():
            acc[...] += _dot(dr_ref[...], h_ref[...], TN)

        @pl.when(i == nk - 1)
        def _():
            o_ref[...] = acc[...].astype(BF16)

    return pl.pallas_call(
        body, name="proj_bwd_w",
        grid=(N_COL_BLOCKS, nk),
        in_specs=[pl.BlockSpec((tk, D_MODEL), lambda j, i: (i, 0)),
                  pl.BlockSpec((None, tk, COL_BLOCK),
                               lambda j, i: (jnp.minimum(j, 2), jnp.where(j < 3, i, nk - 1), 0)),
                  pl.BlockSpec((tk, COL_BLOCK),
                               lambda j, i: (jnp.where(j >= 3, i, 0), jnp.maximum(j - 3, 0)))],
        out_specs=pl.BlockSpec((COL_BLOCK, D_MODEL), lambda j, i: (j, 0)),
        out_shape=jax.ShapeDtypeStruct((D_IN, D_MODEL), BF16),
        scratch_shapes=[pltpu.VMEM((COL_BLOCK, D_MODEL), F32)],
        compiler_params=_params(40),
    )(h, dqkv, drest)


def _adamw_math(w, g, m, v):
    c1 = 1.0 - ADAM_B1 ** ADAM_STEP
    c2 = 1.0 - ADAM_B2 ** ADAM_STEP
    nm = ADAM_B1 * m + (1.0 - ADAM_B1) * g
    nv = ADAM_B2 * v + (1.0 - ADAM_B2) * (g * g)
    return -ADAM_LR * ((nm / c1) / (jnp.sqrt(nv / c2) + ADAM_EPS) + ADAM_WD * w), nm, nv


def _adamw(name, w, g, m, v, from_chip):
    rows, cols = w.shape
    tr = rows if rows * cols <= 512 * 1024 else next(t for t in range(256, 7, -8) if rows % t == 0)

    def body(w_ref, g_ref, m_ref, v_ref, t_ref, g_out, d_ref, nm_ref, nv_ref):
        gg = g_ref[...]
        for j in range(3):
            gg = gg + t_ref[j].astype(F32)
        g_out[...] = gg
        d_ref[...], nm_ref[...], nv_ref[...] = _adamw_math(w_ref[...], gg, m_ref[...], v_ref[...])

    spec = pl.BlockSpec((tr, cols), lambda i: (i, 0))
    shape = jax.ShapeDtypeStruct((rows, cols), F32)
    return pl.pallas_call(
        body, name=name,
        grid=(rows // tr,),
        in_specs=[spec] * 4 + [pl.BlockSpec((3, tr, cols), lambda i: (0, i, 0))],
        out_specs=(spec,) * 4, out_shape=(shape,) * 4,
        compiler_params=_params(32),
    )(w, g, m, v, from_chip)


_SMALL = (("norm_g", (1, D_MODEL)), ("b_gate", (1, 2 * D_MODEL)), ("rel_bias", (N_HEADS, N_REL)),
          ("sgu_ln_g", (1, D_B)), ("sgu_ln_b", (1, D_B)), ("w_s", (N_GROUPS * SGU_CHUNK, SGU_CHUNK)),
          ("b_s", (N_GROUPS, SGU_CHUNK)), ("final_g", (1, D_MODEL)))


def _adamw_small(slabs, ws_all, ng_all, weights, moments_m, moments_v):
    k = len(_SMALL)

    def total(ref):
        acc = ref[0]
        for d in range(1, N_DEV):
            acc = acc + ref[d]
        return acc

    def body(*refs):
        slab_ref, ws_ref, ng_ref = refs[:3]
        w_refs, m_refs, v_refs = refs[3:3 + k], refs[3 + k:3 + 2 * k], refs[3 + 2 * k:3 + 3 * k]
        outs = refs[3 + 3 * k:]
        slab = total(slab_ref)
        grads = {
            "norm_g": total(ng_ref)[0:1, :],
            "b_gate": jnp.concatenate([slab[ROW_B_GATE:ROW_B_GATE + 1, :], slab[ROW_B_GATE + 1:ROW_B_GATE + 2, :]], axis=1),
            "rel_bias": slab[ROW_REL:ROW_REL + N_HEADS, :N_REL],
            "sgu_ln_g": slab[ROW_LN_G:ROW_LN_G + 1, :D_B],
            "sgu_ln_b": slab[ROW_LN_B:ROW_LN_B + 1, :D_B],
            "w_s": total(ws_ref),
            "b_s": slab[ROW_B_S:ROW_B_S + N_GROUPS, :SGU_CHUNK],
            "final_g": slab[ROW_FINAL_G:ROW_FINAL_G + 1, :],
        }
        for n, (name, _) in enumerate(_SMALL):
            g = grads[name]
            outs[n][...] = g
            outs[k + n][...], outs[2 * k + n][...], outs[3 * k + n][...] = _adamw_math(
                w_refs[n][...], g, m_refs[n][...], v_refs[n][...])
        outs[4 * k][...] = slab[ROW_LOSS:ROW_LOSS + 1, :1]

    vmem = pl.BlockSpec(memory_space=pltpu.VMEM)
    shapes = tuple(jax.ShapeDtypeStruct(shape, F32) for _, shape in _SMALL)
    return pl.pallas_call(
        body, name="adamw_small",
        out_shape=shapes * 4 + (jax.ShapeDtypeStruct((1, 1), F32),),
        in_specs=[vmem] * (3 + 3 * k), out_specs=tuple([vmem] * (4 * k + 1)),
        compiler_params=_params(16),
    )(slabs, ws_all, ng_all, *weights, *moments_m, *moments_v)


def _pad_rel(a):
    return jnp.pad(a.reshape(N_HEADS, N_REL), ((0, 0), (0, N_REL_PAD - N_REL)))


def kernel(x, norm_g, w_in, b_gate, rel_bias, sgu_ln_g, sgu_ln_b, w_s, b_s, w_pa, w_pb, w_out, final_g, loss_target, m_norm_g, m_w_in, m_b_gate, m_rel_bias, m_sgu_ln_g, m_sgu_ln_b, m_w_s, m_b_s, m_w_pa, m_w_pb, m_w_out, m_final_g, v_norm_g, v_w_in, v_b_gate, v_rel_bias, v_sgu_ln_g, v_sgu_ln_b, v_w_s, v_b_s, v_w_pa, v_w_pb, v_w_out, v_final_g):
    s = x.shape[1]
    xs = x.reshape(s, D_MODEL)
    tgt = loss_target.reshape(s, D_MODEL)

    bias_table = _bias_table(_pad_rel(rel_bias))
    w_in_t = jnp.swapaxes(w_in[0], 0, 1)
    qkv, h, w_in_t_full = _gather_proj_fwd(xs, norm_g, w_in_t)
    attn_out, g_pa, g_pb, g_out = _attn_fwd(qkv, bias_table, (w_pa[0], w_pb[0], w_out[0]))
    w_pa_full = jnp.transpose(g_pa, (1, 0, 2)).reshape(D_A, D_MODEL)
    w_pb_full = jnp.transpose(g_pb, (1, 0, 2)).reshape(D_B, D_MODEL)
    w_out_full = g_out.reshape(D_MODEL, D_MODEL)

    (dx2, d_attn, drest, dw_out, dw_pa, dw_pb, d_bgate, d_fg, d_lng, d_lnb, d_ws, d_bs, loss_part) = _mid_fwd_bwd(
        xs, tgt, attn_out, qkv, w_pa_full, w_pb_full, w_out_full, b_gate, sgu_ln_g, sgu_ln_b, w_s[0],
        b_s.reshape(N_GROUPS, SGU_CHUNK, 1), final_g.reshape(1, D_MODEL))

    own_pa, own_pb, own_out, tc_pa, tc_pb, tc_out = _reduce_chip(
        "reduce_chip_proj", (dw_pa, dw_pb, dw_out), (1, 1, 0))
    dqkv, dbias, fc_pa, fc_pb, fc_out = _attn_bwd(qkv, bias_table, d_attn, (tc_pa, tc_pb, tc_out))
    d_rel = _bias_grad(dbias)
    dw_in_t = _proj_bwd_w(h, dqkv, drest)
    own_in, tc_in = _reduce_chip("reduce_chip_in", (dw_in_t,), (0,))
    grad_x, _, fc_in, slabs, ws_all, ng_all = _proj_bwd_x(
        dqkv, drest, xs, dx2, norm_g, w_in_t_full, (tc_in,),
        (d_bgate, d_rel, d_lng, d_lnb, d_fg, loss_part, d_bs, d_ws.reshape(N_GROUPS * SGU_CHUNK, SGU_CHUNK)))
    big = {"w_in": tuple(jnp.swapaxes(t, 0, 1)[None] for t in _adamw(
        "adamw_w_in", w_in_t, own_in, jnp.swapaxes(m_w_in[0], 0, 1), jnp.swapaxes(v_w_in[0], 0, 1), fc_in))}
    for name, w, g, fc, m, v in (("w_pa", w_pa, own_pa, fc_pa, m_w_pa, v_w_pa),
                                 ("w_pb", w_pb, own_pb, fc_pb, m_w_pb, v_w_pb),
                                 ("w_out", w_out, own_out, fc_out, m_w_out, v_w_out)):
        big[name] = tuple(t[None] for t in _adamw("adamw_" + name, w[0], g, m[0], v[0], fc))

    as_2d = lambda leaves: [a.reshape(shape) for a, (_, shape) in zip(leaves, _SMALL)]
    small_out = _adamw_small(
        slabs, ws_all, ng_all, as_2d((norm_g, b_gate, rel_bias, sgu_ln_g, sgu_ln_b, w_s, b_s, final_g)),
        as_2d((m_norm_g, m_b_gate, m_rel_bias, m_sgu_ln_g, m_sgu_ln_b, m_w_s, m_b_s, m_final_g)),
        as_2d((v_norm_g, v_b_gate, v_rel_bias, v_sgu_ln_g, v_sgu_ln_b, v_w_s, v_b_s, v_final_g)))
    small_index = {name: n for n, (name, _) in enumerate(_SMALL)}

    def leaf(kind, name, like):
        if name in big:
            return big[name][kind]
        return small_out[kind * len(_SMALL) + small_index[name]].reshape(like.shape)

    weights = (("norm_g", norm_g), ("w_in", w_in), ("b_gate", b_gate), ("rel_bias", rel_bias), ("sgu_ln_g", sgu_ln_g),
               ("sgu_ln_b", sgu_ln_b), ("w_s", w_s), ("b_s", b_s), ("w_pa", w_pa), ("w_pb", w_pb), ("w_out", w_out),
               ("final_g", final_g))
    outs = [small_out[-1].reshape(()), grad_x.reshape(x.shape)]
    for kind in range(4):
        outs.extend(leaf(kind, name, like) for name, like in weights)
    return tuple(outs)
```

```python
import functools
import math

import jax
import jax.numpy as jnp
from jax import lax
from jax.experimental import pallas as pl
from jax.experimental.pallas import tpu as pltpu

F32 = jnp.float32
BF16 = jnp.bfloat16
MESH = pl.DeviceIdType.MESH
N_DEV = 8

D_MODEL = 1024
D_A = 512
D_B = 512
D_IN = 5632
N_HEADS = 8
HEAD_DIM = 64
N_PREV = 8
REL_CLIP = 128
N_REL = 2 * REL_CLIP + 1
N_REL_PAD = 384
SGU_CHUNK = 128
N_GROUPS = 4
EPS = 1e-6
NEG_INF = -1e30
Q_SCALE = HEAD_DIM ** -0.5

Q_BLOCK = 256
K_SPAN = 768
Z_PAD = K_SPAN - Q_BLOCK
ROLL_W = 1024
COL_BLOCK = 512
N_COL_BLOCKS = D_IN // COL_BLOCK
REST = D_IN - 3 * D_A
TOKEN_TILE = 256

ADAM_LR = 0.001
ADAM_B1 = 0.9
ADAM_B2 = 0.999
ADAM_EPS = 1e-08
ADAM_WD = 0.01
ADAM_STEP = 10

GELU_C = math.sqrt(2.0 / math.pi)
GELU_A = 0.044715

NT = (((1,), (1,)), ((), ()))
TN = (((0,), (0,)), ((), ()))
HIGHEST = lax.Precision.HIGHEST


def _params(vmem_mb, **kw):
    return pltpu.CompilerParams(vmem_limit_bytes=vmem_mb * 1024 * 1024, **kw)


def _dot(a, b, dims=None):
    if dims is None:
        return jnp.dot(a, b, preferred_element_type=F32)
    return lax.dot_general(a, b, dims, preferred_element_type=F32)


def _sigmoid(x):
    return 0.5 * jnp.tanh(0.5 * x) + 0.5


def _gelu_and_grad(u):
    u2 = u * u
    t = jnp.tanh(GELU_C * (u + GELU_A * u * u2))
    half = 0.5 * (1.0 + t)
    g = u * half
    dg = half + 0.5 * u * (1.0 - t * t) * (GELU_C * (1.0 + 3.0 * GELU_A * u2))
    return g, dg


def _my_pos():
    return lax.axis_index("x"), lax.axis_index("y"), lax.axis_index("c")


def _flat_id(pos):
    return 4 * pos[0] + 2 * pos[1] + pos[2]


def _other_chips(pos):
    x, y, _ = pos
    return ((1 - x, y), (x, 1 - y), (1 - x, 1 - y))


class _SlotGather:
    def __init__(self, bufs, send_sems, recv_sems, own=None):
        self.bufs, self.send_sems, self.recv_sems = bufs, send_sems, recv_sems
        self.own = own if own is not None else [None] * len(bufs)
        x, y, c = _my_pos()
        self.c, self.me, self.sib = c, (x, y, c), (x, y, 1 - c)
        self.chips = _other_chips(self.me)

    def _copy(self, a, k, block, to):
        slot = _flat_id(block)
        src = self.own[a] if (k < 4 and self.own[a] is not None) else self.bufs[a].at[slot]
        return pltpu.make_async_remote_copy(
            src_ref=src, dst_ref=self.bufs[a].at[slot],
            send_sem=self.send_sems.at[a, k], recv_sem=self.recv_sems.at[a, k], device_id=to, device_id_type=MESH)

    def _own_sends(self):
        n = len(self.bufs)
        return ([self._copy(a, 1 + j, self.me, (*chip, self.c)) for j, chip in enumerate(self.chips) for a in range(n)]
                + [self._copy(a, 0, self.me, self.sib) for a in range(n)])

    def _passes(self):
        return [self._copy(a, 4 + j, (*chip, self.c), self.sib)
                for j, chip in enumerate(self.chips) for a in range(len(self.bufs))]

    def start(self):
        for cp in self._own_sends():
            cp.start()

    def pass_on(self):
        for j, chip in enumerate(self.chips):
            for a in range(len(self.bufs)):
                self._copy(a, 1 + j, (*chip, self.c), self.me).wait_recv()
                self._copy(a, 4 + j, (*chip, self.c), self.sib).start()

    def finish(self):
        for a in range(len(self.bufs)):
            self._copy(a, 0, self.sib, self.me).wait_recv()
            for j, chip in enumerate(self.chips):
                self._copy(a, 4 + j, (*chip, 1 - self.c), self.me).wait_recv()
        for cp in self._own_sends() + self._passes():
            cp.wait_send()


def _reduce_chip(name, parts, sharded_dim):
    n = len(parts)
    shapes = []
    for p, dim in zip(parts, sharded_dim):
        shape = list(p.shape)
        shape[dim] //= N_DEV
        shapes.append(tuple(shape))

    def body(*refs):
        full, own, to_chip = refs[:n], refs[n:2 * n], refs[2 * n:3 * n]
        ins, from_sib = refs[3 * n:4 * n], refs[4 * n:5 * n]
        send_sems, recv_sems = refs[5 * n], refs[5 * n + 1]
        x, y, c = _my_pos()
        sib = (x, y, 1 - c)
        chips = ((x, y),) + _other_chips((x, y, c))
        for a in range(n):
            rows, cols = shapes[a]
            for d in range(N_DEV):
                if sharded_dim[a] == 0:
                    ins[a][d] = full[a][d * rows:(d + 1) * rows, :].astype(BF16)
                else:
                    ins[a][d] = full[a][:, d * cols:(d + 1) * cols].astype(BF16)

        def to_sibling(a, r):
            return pltpu.make_async_remote_copy(
                src_ref=ins[a].at[_flat_id((*chips[r], 1 - c))], dst_ref=from_sib[a].at[r],
                send_sem=send_sems.at[a, r], recv_sem=recv_sems.at[a, r], device_id=sib, device_id_type=MESH)

        sends = [to_sibling(a, r) for r in (1, 2, 3, 0) for a in range(n)]
        for cp in sends:
            cp.start()
        for r in (1, 2, 3, 0):
            for a in range(n):
                to_sibling(a, r).wait_recv()
                both = ins[a][_flat_id((*chips[r], c))].astype(F32) + from_sib[a][r].astype(F32)
                if r == 0:
                    own[a][...] = both
                else:
                    to_chip[a][r - 1] = both.astype(BF16)
        for cp in sends:
            cp.wait_send()

    vmem = pl.BlockSpec(memory_space=pltpu.VMEM)
    return pl.pallas_call(
        body, name=name,
        out_shape=tuple(jax.ShapeDtypeStruct(sh, F32) for sh in shapes)
        + tuple(jax.ShapeDtypeStruct((3,) + sh, BF16) for sh in shapes),
        in_specs=[vmem] * n, out_specs=tuple([vmem] * (2 * n)),
        scratch_shapes=[pltpu.VMEM((N_DEV,) + sh, BF16) for sh in shapes]
        + [pltpu.VMEM((4,) + sh, BF16) for sh in shapes]
        + [pltpu.SemaphoreType.DMA((n, 4)), pltpu.SemaphoreType.DMA((n, 4))],
        compiler_params=_params(56),
    )(*parts)


def _owner_copies(to_chip, from_chip, send_sems, recv_sems):
    x, y, c = _my_pos()
    return [pltpu.make_async_remote_copy(
        src_ref=to_chip[a].at[j], dst_ref=from_chip[a].at[j],
        send_sem=send_sems.at[a, j], recv_sem=recv_sems.at[a, j], device_id=(*chip, c), device_id_type=MESH)
        for a in range(len(to_chip)) for j, chip in enumerate(_other_chips((x, y, c)))]


ROW_NORM_G, ROW_B_GATE, ROW_LN_G, ROW_LN_B, ROW_FINAL_G, ROW_LOSS, ROW_REL, ROW_B_S, SLAB_ROWS = 0, 1, 3, 4, 5, 6, 8, 16, 24


def _rel_index(e):
    lo, hi = Z_PAD - REL_CLIP, Z_PAD + REL_CLIP
    return jnp.where(e <= lo, 2 * REL_CLIP, jnp.where(e < hi, hi - e, jnp.where(e <= K_SPAN, 0, 2 * REL_CLIP)))


def _bias_table(rel_bias_pad):
    def body(rb_ref, bt_ref):
        c = lax.broadcasted_iota(jnp.int32, (N_REL_PAD, ROLL_W), 1)
        r = lax.broadcasted_iota(jnp.int32, (N_REL_PAD, ROLL_W), 0)
        pick = (r == _rel_index(c)).astype(F32)
        rows = jnp.dot(rb_ref[...], pick, precision=HIGHEST, preferred_element_type=F32)
        qc = lax.broadcasted_iota(jnp.int32, (Q_BLOCK, K_SPAN), 0) >> 6
        kc = lax.broadcasted_iota(jnp.int32, (Q_BLOCK, K_SPAN), 1) >> 6
        band = (kc >= qc) & (kc <= qc + N_PREV)
        for h in range(N_HEADS):
            t = jnp.broadcast_to(rows[h:h + 1, :], (Q_BLOCK, ROLL_W))
            t = pltpu.roll(t, 0, 1, stride=1, stride_axis=0)
            bt_ref[h] = jnp.where(band, t[:, :K_SPAN], NEG_INF)

    return pl.pallas_call(
        body, name="bias_table",
        out_shape=jax.ShapeDtypeStruct((N_HEADS, Q_BLOCK, K_SPAN), F32),
        compiler_params=_params(32),
    )(rel_bias_pad)


def _bias_grad(dbias):
    def body(a_ref, o_ref):
        rr = lax.broadcasted_iota(jnp.int32, (Q_BLOCK, Q_BLOCK), 0)
        cc = lax.broadcasted_iota(jnp.int32, (Q_BLOCK, Q_BLOCK), 1)
        flip = (rr + cc == Q_BLOCK - 1).astype(F32)
        c = lax.broadcasted_iota(jnp.int32, (ROLL_W, N_REL_PAD), 0)
        r = lax.broadcasted_iota(jnp.int32, (ROLL_W, N_REL_PAD), 1)
        e = jnp.where(c >= Q_BLOCK - 1, c - (Q_BLOCK - 1), c + (ROLL_W - Q_BLOCK + 1))
        pick = (r == _rel_index(e)).astype(F32)
        sums = []
        for h in range(N_HEADS):
            a = jnp.dot(flip, a_ref[h], precision=HIGHEST, preferred_element_type=F32)
            a = jnp.concatenate([a, jnp.zeros((Q_BLOCK, ROLL_W - K_SPAN), F32)], axis=1)
            a = pltpu.roll(a, 0, 1, stride=1, stride_axis=0)
            sums.append(jnp.sum(a, axis=0, keepdims=True))
        diag = jnp.concatenate(sums, axis=0)
        o_ref[...] = jnp.dot(diag, pick, precision=HIGHEST, preferred_element_type=F32)

    return pl.pallas_call(
        body, name="bias_grad",
        out_shape=jax.ShapeDtypeStruct((N_HEADS, N_REL_PAD), F32),
        compiler_params=_params(32),
    )(dbias)


def _gather_proj_fwd(x, norm_g, w_in_t):
    s = x.shape[0]
    tm = 512 if s % 512 == 0 else TOKEN_TILE
    nt = s // tm
    n_pad = Z_PAD // tm
    shard_w = w_in_t.shape[0]
    chip_w = 2 * shard_w
    n_chips = N_DEV // 2

    def body(order_ref, x_ref, g_ref, win_hbm, z_ref, xn_ref, wt_hbm, stage, wchip, hb, win_f32, send_sems, recv_sems,
             local_sems):
        j = pl.program_id(0)
        i = pl.program_id(1)
        x_, y_, c_ = _my_pos()
        me, sib = (x_, y_, c_), (x_, y_, 1 - c_)
        near = _other_chips(me)
        pick = lambda a, b: tuple(jnp.where(c_ == 0, u, v) for u, v in zip(a, b))
        passed_from, passed_to = pick(near[0], near[1]), pick(near[1], near[0])

        def rows_of(block):
            return wt_hbm.at[pl.ds(pl.multiple_of(_flat_id(block) * shard_w, 16), shard_w), :]

        def copy(k, block, to, own=False):
            return pltpu.make_async_remote_copy(
                src_ref=stage if own else rows_of(block), dst_ref=rows_of(block),
                send_sem=send_sems.at[k], recv_sem=recv_sems.at[k], device_id=to, device_id_type=MESH)

        def sends():
            return ([copy(0, me, sib, True), copy(1, me, (*near[0], c_), True), copy(2, me, (*near[1], c_), True),
                     copy(3, (*passed_from, c_), (*passed_to, c_))]
                    + [copy(4 + n, (*near[n], c_), sib) for n in range(3)])

        keep = pltpu.make_async_copy(stage, rows_of(me), local_sems.at[0])

        def fetch(chip):
            first = pl.multiple_of((2 * chip[0] + chip[1]) * chip_w, 16)
            cp = pltpu.make_async_copy(wt_hbm.at[pl.ds(first, chip_w), :], wchip, local_sems.at[1])
            cp.start()
            cp.wait()

        @pl.when((j == 0) & (i == 0))
        def _():
            load = pltpu.make_async_copy(win_hbm, win_f32, local_sems.at[1])
            load.start()
            load.wait()
            stage[...] = win_f32[...].astype(BF16)
            keep.start()
            for cp in sends()[:3]:
                cp.start()
            copy(0, sib, me).wait_recv()
            keep.wait()
            fetch((x_, y_))

        @pl.when((j == 1) & (i == 0))
        def _():
            copy(1, (*near[0], c_), me).wait_recv()
            copy(2, (*near[1], c_), me).wait_recv()
            for cp in sends()[3:6]:
                cp.start()
            copy(4, (*near[0], 1 - c_), me).wait_recv()
            fetch(near[0])

        @pl.when((j == 2) & (i == 0))
        def _():
            copy(5, (*near[1], 1 - c_), me).wait_recv()
            fetch(near[1])

        @pl.when((j == 3) & (i == 0))
        def _():
            copy(3, (*near[2], c_), me).wait_recv()
            copy(6, (*near[2], c_), sib).start()
            copy(6, (*near[2], 1 - c_), me).wait_recv()
            fetch(near[2])

        @pl.when(i < n_pad)
        def _():
            z_ref[...] = jnp.zeros(z_ref.shape, BF16)

        @pl.when(i >= n_pad)
        def _():
            rows = pl.ds(pl.multiple_of((i - n_pad) * tm, tm), tm)

            @pl.when(j == 0)
            def _():
                xf = x_ref[...]
                xn = xf * lax.rsqrt(jnp.mean(xf * xf, axis=-1, keepdims=True) + EPS)
                hb[rows, :] = (xn * g_ref[...]).astype(BF16)
                xn_ref[...] = xn.astype(BF16)

            blk = _dot(hb[rows, :], wchip[...], NT)
            q_scale = jnp.where(order_ref[j] == 0, Q_SCALE, 1.0).astype(F32)
            z_ref[:, :D_A] = (blk[:, :D_A] * q_scale).astype(BF16)
            z_ref[:, D_A:] = blk[:, D_A:].astype(BF16)

        @pl.when((j == n_chips - 1) & (i == n_pad + nt - 1))
        def _():
            for cp in sends():
                cp.wait_send()

    pos = _my_pos()
    order = jnp.stack([2 * cx + cy for cx, cy in ((pos[0], pos[1]),) + _other_chips(pos)]).astype(jnp.int32)
    first_pass = lambda j, i: jnp.where(j == 0, jnp.maximum(i - n_pad, 0), nt - 1)
    grid_spec = pltpu.PrefetchScalarGridSpec(
        num_scalar_prefetch=1,
        grid=(n_chips, n_pad + nt),
        in_specs=[pl.BlockSpec((tm, D_MODEL), lambda j, i, o: (first_pass(j, i), 0)),
                  pl.BlockSpec((1, D_MODEL), lambda j, i, o: (0, 0)),
                  pl.BlockSpec(memory_space=pl.ANY)],
        out_specs=(pl.BlockSpec((tm, chip_w), lambda j, i, o: (i, o[j])),
                   pl.BlockSpec((tm, D_MODEL), lambda j, i, o: (first_pass(j, i), 0)),
                   pl.BlockSpec(memory_space=pl.ANY)),
        scratch_shapes=[pltpu.VMEM((shard_w, D_MODEL), BF16), pltpu.VMEM((chip_w, D_MODEL), BF16),
                        pltpu.VMEM((s, D_MODEL), BF16), pltpu.VMEM(w_in_t.shape, F32),
                        pltpu.SemaphoreType.DMA((N_DEV - 1,)), pltpu.SemaphoreType.DMA((N_DEV - 1,)),
                        pltpu.SemaphoreType.DMA((2,))])
    return pl.pallas_call(
        body, name="gather_proj_fwd",
        grid_spec=grid_spec,
        out_shape=(jax.ShapeDtypeStruct((Z_PAD + s, D_IN), BF16), jax.ShapeDtypeStruct((s, D_MODEL), BF16),
                   jax.ShapeDtypeStruct((D_IN, D_MODEL), BF16)),
        compiler_params=_params(60),
    )(order, x, norm_g, w_in_t)


def _attn_specs(rows):
    pairs = N_HEADS // 2
    return ([pl.BlockSpec((rows, 128), functools.partial(lambda which, p: (0, which * pairs + p), which))
             for which in range(3)]
            + [pl.BlockSpec((2, Q_BLOCK, K_SPAN), lambda p: (p, 0, 0))])


def _head_masks():
    lane = lax.broadcasted_iota(jnp.int32, (1, 128), 1)
    first = lane < HEAD_DIM
    return (first, jnp.logical_not(first))


def _stack_heads(x, masks):
    zero = jnp.zeros((), x.dtype)
    return jnp.concatenate([jnp.where(m, x, zero) for m in masks], axis=0)


STRIP = 16


def _softmax_strips(s_ref, bias_ref, b):
    valid = lax.broadcasted_iota(jnp.int32, (1, K_SPAN), 1) >= Z_PAD - b * Q_BLOCK
    for t in range(2 * Q_BLOCK // STRIP):
        hh, r = divmod(t * STRIP, Q_BLOCK)
        st = s_ref[t * STRIP:(t + 1) * STRIP, :] + bias_ref[hh, r:r + STRIP, :]
        st = jnp.where(valid, st, NEG_INF)
        e = jnp.exp(st - jnp.max(st, axis=-1, keepdims=True))
        yield e * (1.0 / jnp.sum(e, axis=-1, keepdims=True))


def _side_by_side_strips(strips):
    half = len(strips) // 2
    return jnp.concatenate([jnp.concatenate([a, c], axis=1) for a, c in zip(strips[:half], strips[half:])], axis=0)


def _attn_fwd(qkv, bias_table, shards):
    s = qkv.shape[0] - Z_PAD
    nb = s // Q_BLOCK
    n = len(shards)
    pairs = N_HEADS // 2

    def body(*refs):
        q_ref, k_ref, v_ref, bt_ref = refs[:4]
        shard_refs = refs[4:4 + n]
        o_ref = refs[4 + n]
        slot_refs = refs[5 + n:5 + 2 * n]
        stages = refs[5 + 2 * n:5 + 3 * n]
        s_scr, send_sems, recv_sems, local_sems = refs[5 + 3 * n:]
        p_id = pl.program_id(0)
        gather = _SlotGather(slot_refs, send_sems, recv_sems, own=stages)
        keep = [pltpu.make_async_copy(stages[a], slot_refs[a].at[_flat_id(_my_pos())], local_sems.at[a])
                for a in range(n)]

        @pl.when(p_id == 0)
        def _():
            for a in range(n):
                stages[a][...] = shard_refs[a][...].astype(BF16)
                keep[a].start()
            gather.start()

        @pl.when(p_id == 1)
        def _():
            gather.pass_on()

        masks = _head_masks()

        def scores(b, half):
            r0 = pl.multiple_of(b * Q_BLOCK, Q_BLOCK)
            q2 = _stack_heads(q_ref[pl.ds(r0 + Z_PAD, Q_BLOCK), :], masks)
            s_scr[half] = _dot(q2, k_ref[pl.ds(r0, K_SPAN), :], NT)

        def finish(b, half):
            r0 = pl.multiple_of(b * Q_BLOCK, Q_BLOCK)
            v2 = _stack_heads(v_ref[pl.ds(r0, K_SPAN), :], masks)
            p = [st.astype(BF16) for st in _softmax_strips(s_scr.at[half], bt_ref, b)]
            o_ref[pl.ds(r0, Q_BLOCK), :] = _dot(_side_by_side_strips(p), v2)

        def two_blocks(i, carry):
            b = 2 * i
            scores(b + 1, 1)
            finish(b, 0)
            scores(jnp.minimum(b + 2, nb - 1), 0)
            finish(b + 1, 1)
            return carry

        scores(0, 0)
        lax.fori_loop(0, nb // 2, two_blocks, 0)

        @pl.when(p_id == pairs - 1)
        def _():
            gather.finish()
            for cp in keep:
                cp.wait()

    hbm = pl.BlockSpec(memory_space=pl.ANY)
    return pl.pallas_call(
        body, name="attn_fwd",
        grid=(pairs,),
        in_specs=_attn_specs(s + Z_PAD) + [pl.BlockSpec(a.shape, lambda p: (0, 0)) for a in shards],
        out_specs=(pl.BlockSpec((s, 128), lambda p: (0, p)),) + (hbm,) * n,
        out_shape=(jax.ShapeDtypeStruct((s, D_A), F32),)
        + tuple(jax.ShapeDtypeStruct((N_DEV,) + a.shape, BF16) for a in shards),
        scratch_shapes=[pltpu.VMEM(a.shape, BF16) for a in shards]
        + [pltpu.VMEM((2, 2 * Q_BLOCK, K_SPAN), F32),
           pltpu.SemaphoreType.DMA((n, N_DEV - 1)), pltpu.SemaphoreType.DMA((n, N_DEV - 1)),
           pltpu.SemaphoreType.DMA((n,))],
        compiler_params=_params(48),
    )(qkv, qkv, qkv, bias_table, *shards)


def _attn_bwd(qkv, bias_table, d_out, to_chip):
    s = qkv.shape[0] - Z_PAD
    nb = s // Q_BLOCK
    n = len(to_chip)
    pairs = N_HEADS // 2

    def body(*refs):
        q_ref, k_ref, v_ref, bt_ref, do_ref = refs[:5]
        to_chip_refs = refs[5:5 + n]
        dqkv_ref, db_ref = refs[5 + n:7 + n]
        from_chip_refs = refs[7 + n:7 + 2 * n]
        dk_acc, dv_acc, s_scr, dp_scr, send_sems, recv_sems = refs[7 + 2 * n:]
        p_id = pl.program_id(0)

        @pl.when(p_id == 0)
        def _():
            for cp in _owner_copies(to_chip_refs, from_chip_refs, send_sems, recv_sems):
                cp.start()

        dk_acc[...] = jnp.zeros(dk_acc.shape, F32)
        dv_acc[...] = jnp.zeros(dv_acc.shape, F32)
        db_ref[...] = jnp.zeros(db_ref.shape, F32)
        masks = _head_masks()

        def operands(b):
            r0 = pl.multiple_of(b * Q_BLOCK, Q_BLOCK)
            q2 = _stack_heads(q_ref[pl.ds(r0 + Z_PAD, Q_BLOCK), :], masks)
            do2 = _stack_heads(do_ref[pl.ds(r0, Q_BLOCK), :], masks)
            return r0, q2, do2, k_ref[pl.ds(r0, K_SPAN), :]

        def ahead(b, half):
            r0, q2, do2, kcat = operands(b)
            s_scr[half] = _dot(q2, kcat, NT)
            dp_scr[half] = _dot(do2, v_ref[pl.ds(r0, K_SPAN), :], NT)

        def finish(b, half):
            r0, q2, do2, kcat = operands(b)
            p_strips, ds_strips = [], []
            for t, p in enumerate(_softmax_strips(s_scr.at[half], bt_ref, b)):
                hh, r = divmod(t * STRIP, Q_BLOCK)
                dp_t = dp_scr[half, t * STRIP:(t + 1) * STRIP, :]
                ds = p * (dp_t - jnp.sum(p * dp_t, axis=-1, keepdims=True))
                db_ref[hh, r:r + STRIP, :] += ds
                p_strips.append(p.astype(BF16))
                ds_strips.append(ds.astype(BF16))
            dq = _dot(_side_by_side_strips(ds_strips), _stack_heads(kcat, masks))
            dqkv_ref[0, pl.ds(r0, Q_BLOCK), :] = (dq * Q_SCALE).astype(BF16)
            dk_acc[pl.ds(r0, K_SPAN), :] += _dot(jnp.concatenate(ds_strips, axis=0), q2, TN)
            dv_acc[pl.ds(r0, K_SPAN), :] += _dot(jnp.concatenate(p_strips, axis=0), do2, TN)

        def two_blocks(i, carry):
            b = 2 * i
            ahead(b + 1, 1)
            finish(b, 0)
            ahead(jnp.minimum(b + 2, nb - 1), 0)
            finish(b + 1, 1)
            return carry

        ahead(0, 0)
        lax.fori_loop(0, nb // 2, two_blocks, 0)
        dqkv_ref[1] = dk_acc[Z_PAD:, :].astype(BF16)
        dqkv_ref[2] = dv_acc[Z_PAD:, :].astype(BF16)

        @pl.when(p_id == pairs - 1)
        def _():
            for cp in _owner_copies(to_chip_refs, from_chip_refs, send_sems, recv_sems):
                cp.wait_recv()
                cp.wait_send()

    hbm = pl.BlockSpec(memory_space=pl.ANY)
    return pl.pallas_call(
        body, name="attn_bwd",
        grid=(pairs,),
        in_specs=_attn_specs(s + Z_PAD) + [pl.BlockSpec((s, 128), lambda p: (0, p))] + [hbm] * n,
        out_specs=(pl.BlockSpec((3, s, 128), lambda p: (0, 0, p)),
                   pl.BlockSpec((2, Q_BLOCK, K_SPAN), lambda p: (p, 0, 0))) + (hbm,) * n,
        out_shape=(jax.ShapeDtypeStruct((3, s, D_A), BF16),
                   jax.ShapeDtypeStruct((N_HEADS, Q_BLOCK, K_SPAN), F32))
        + tuple(jax.ShapeDtypeStruct(t.shape, t.dtype) for t in to_chip),
        scratch_shapes=[pltpu.VMEM((s + Z_PAD, 128), F32), pltpu.VMEM((s + Z_PAD, 128), F32),
                        pltpu.VMEM((2, 2 * Q_BLOCK, K_SPAN), F32), pltpu.VMEM((2, 2 * Q_BLOCK, K_SPAN), F32),
                        pltpu.SemaphoreType.DMA((n, 3)), pltpu.SemaphoreType.DMA((n, 3))],
        compiler_params=_params(56),
    )(qkv, qkv, qkv, bias_table, d_out, *to_chip)


def _mid_fwd_bwd(x, target, attn_out, z, w_pa, w_pb, w_out, b_gate, ln_g, ln_b, w_s, b_s, final_g):
    s = x.shape[0]
    tm = TOKEN_TILE
    nt = s // tm

    def body(x_ref, t_ref, oa_ref, ga_ref, ub_ref, vb_ref, gb_ref, ta0_ref, ta1_ref, tb0_ref, tb1_ref,
             wpa_hbm, wpb_hbm, wout_hbm, bg_ref, lng_ref, lnb_ref, ws_ref, bs_ref, fg_ref,
             dx2_ref, doa_ref, dz_ref, dwout_hbm, dwpa_hbm, dwpb_hbm, dbg_ref, dfg_ref, dlng_ref, dlnb_ref, dws_ref,
             dbs_ref, loss_ref,
             wpa, wpb, wout, wmix, acc_out, acc_pa, acc_pb, sem):
        i = pl.program_id(0)

        @pl.when(i == 0)
        def _():
            loads = [pltpu.make_async_copy(src, dst, sem.at[n])
                     for n, (src, dst) in enumerate(((wpa_hbm, wpa), (wpb_hbm, wpb), (wout_hbm, wout)))]
            for cp in loads:
                cp.start()
            t_idx = lax.broadcasted_iota(jnp.int32, (SGU_CHUNK, SGU_CHUNK), 0)
            s_idx = lax.broadcasted_iota(jnp.int32, (SGU_CHUNK, SGU_CHUNK), 1)
            for g in range(N_GROUPS):
                wmix[g] = jnp.where(s_idx <= t_idx, ws_ref[g], 0.0).astype(BF16)
            for ref in (acc_out, acc_pa, acc_pb, dbg_ref, dfg_ref, dlng_ref, dlnb_ref, dws_ref, dbs_ref, loss_ref):
                ref[...] = jnp.zeros(ref.shape, F32)
            for cp in loads:
                cp.wait()

        def tile_fwd_bwd(rows):
            g_a = ga_ref[rows, :].astype(F32)
            u_b = ub_ref[rows, :].astype(F32)
            v_b = vb_ref[rows, :].astype(F32)
            g_b = gb_ref[rows, :].astype(F32)
            bg = bg_ref[...]
            sg_a = _sigmoid(g_a)
            silu_a = g_a * sg_a
            o_a = oa_ref[rows, :]
            y_a = (o_a * silu_a).astype(BF16)
            ug, dgelu_u = _gelu_and_grad(u_b)
            vg, dgelu_v = _gelu_and_grad(v_b)
            mu = jnp.mean(vg, axis=-1, keepdims=True)
            vc = vg - mu
            rstd = lax.rsqrt(jnp.mean(vc * vc, axis=-1, keepdims=True) + EPS)
            vhat = vc * rstd
            lng = lng_ref[...]
            vn = (vhat * lng + lnb_ref[...]).astype(BF16)
            sg_b = _sigmoid(g_b)
            silu_b = g_b * sg_b
            subs = [slice(n * SGU_CHUNK, (n + 1) * SGU_CHUNK) for n in range(tm // SGU_CHUNK)]
            mixed = jnp.concatenate([jnp.concatenate(
                [_dot(wmix[g], vn[sub, g * 128:(g + 1) * 128]) + bs_ref[g] for g in range(N_GROUPS)], axis=1)
                for sub in subs], axis=0)
            um = ug * mixed
            y_b = (um * silu_b).astype(BF16)
            gate_a = _sigmoid(jnp.concatenate([ta0_ref[rows, :], ta1_ref[rows, :]], axis=1).astype(F32)
                              + bg[:, :D_MODEL])
            gate_b = _sigmoid(jnp.concatenate([tb0_ref[rows, :], tb1_ref[rows, :]], axis=1).astype(F32)
                              + bg[:, D_MODEL:])
            p_a = _dot(y_a, wpa[...])
            p_b = _dot(y_b, wpb[...])
            merged = (gate_a * p_a + gate_b * p_b).astype(BF16)
            x2 = x_ref[rows, :] + _dot(merged, wout[...])
            r2 = lax.rsqrt(jnp.mean(x2 * x2, axis=-1, keepdims=True) + EPS)
            xh = x2 * r2
            fg = fg_ref[...]
            err = xh * fg - t_ref[rows, :]
            loss_ref[...] += jnp.sum(jnp.sum(err * err, axis=-1, keepdims=True), axis=0, keepdims=True) * (0.5 / D_MODEL)
            dy = err * (1.0 / D_MODEL)
            dfg_ref[...] += jnp.sum(dy * xh, axis=0, keepdims=True)
            gy = dy * fg
            dx2 = r2 * (gy - xh * jnp.mean(gy * xh, axis=-1, keepdims=True))
            dx2_ref[rows, :] = dx2
            dx2b = dx2.astype(BF16)
            dmerged = _dot(dx2b, wout[...], NT)
            acc_out[...] += _dot(merged, dx2b, TN)
            dp_a = dmerged * gate_a
            dp_b = dmerged * gate_b
            dgate_a = dp_a * p_a * (1.0 - gate_a)
            dgate_b = dp_b * p_b * (1.0 - gate_b)
            dbg_ref[:, :D_MODEL] += jnp.sum(dgate_a, axis=0, keepdims=True)
            dbg_ref[:, D_MODEL:] += jnp.sum(dgate_b, axis=0, keepdims=True)
            dz_ref[rows, 2048:3072] = dgate_a.astype(BF16)
            dz_ref[rows, 3072:4096] = dgate_b.astype(BF16)
            dp_ab = dp_a.astype(BF16)
            dp_bb = dp_b.astype(BF16)
            dy_a = _dot(dp_ab, wpa[...], NT)
            dy_b = _dot(dp_bb, wpb[...], NT)
            acc_pa[...] += _dot(y_a, dp_ab, TN)
            acc_pb[...] += _dot(y_b, dp_bb, TN)
            doa_ref[rows, :] = (dy_a * silu_a).astype(BF16)
            dz_ref[rows, 0:512] = (dy_a * o_a * (sg_a * (1.0 + g_a * (1.0 - sg_a)))).astype(BF16)
            dz_ref[rows, 1536:2048] = (dy_b * um * (sg_b * (1.0 + g_b * (1.0 - sg_b)))).astype(BF16)
            dys = dy_b * silu_b
            dz_ref[rows, 512:1024] = (dys * mixed * dgelu_u).astype(BF16)
            dmixed = dys * ug
            dmb = dmixed.astype(BF16)
            dvn_rows = []
            for sub in subs:
                dvn_parts = []
                for g in range(N_GROUPS):
                    cols = slice(g * 128, (g + 1) * 128)
                    dws_ref[g] += _dot(dmb[sub, cols], vn[sub, cols], NT)
                    dbs_ref[g] += jnp.sum(dmixed[sub, cols], axis=-1, keepdims=True)
                    dvn_parts.append(_dot(wmix[g], dmb[sub, cols], TN))
                dvn_rows.append(jnp.concatenate(dvn_parts, axis=1))
            dvn = jnp.concatenate(dvn_rows, axis=0)
            dlng_ref[...] += jnp.sum(dvn * vhat, axis=0, keepdims=True)
            dlnb_ref[...] += jnp.sum(dvn, axis=0, keepdims=True)
            dvh = dvn * lng
            dvg = rstd * (dvh - jnp.mean(dvh, axis=-1, keepdims=True)
                          - vhat * jnp.mean(dvh * vhat, axis=-1, keepdims=True))
            dz_ref[rows, 1024:1536] = (dvg * dgelu_v).astype(BF16)

        tile_fwd_bwd(slice(0, tm))

        @pl.when(i == nt - 1)
        def _():
            t_idx = lax.broadcasted_iota(jnp.int32, (SGU_CHUNK, SGU_CHUNK), 0)
            s_idx = lax.broadcasted_iota(jnp.int32, (SGU_CHUNK, SGU_CHUNK), 1)
            for g in range(N_GROUPS):
                dws_ref[g] = jnp.where(s_idx <= t_idx, dws_ref[g], 0.0)
            stores = [pltpu.make_async_copy(src, dst, sem.at[n])
                      for n, (src, dst) in enumerate(((acc_out, dwout_hbm), (acc_pa, dwpa_hbm), (acc_pb, dwpb_hbm)))]
            for cp in stores:
                cp.start()
            for cp in stores:
                cp.wait()

    tile = lambda w: pl.BlockSpec((tm, w), lambda i: (i, 0))
    whole = lambda shape: pl.BlockSpec(shape, lambda i: (0,) * len(shape))
    hbm = pl.BlockSpec(memory_space=pl.ANY)
    return pl.pallas_call(
        body, name="mid_fwd_bwd",
        grid=(nt,),
        in_specs=[tile(D_MODEL), tile(D_MODEL), tile(D_A)]
        + [pl.BlockSpec((tm, COL_BLOCK), functools.partial(lambda c, i: (i + Z_PAD // tm, c), c))
           for c in range(3, N_COL_BLOCKS)]
        + [hbm, hbm, hbm,
                  whole((1, 2 * D_MODEL)), whole((1, D_B)), whole((1, D_B)),
                  whole((N_GROUPS, SGU_CHUNK, SGU_CHUNK)), whole((N_GROUPS, SGU_CHUNK, 1)), whole((1, D_MODEL))],
        out_specs=(tile(D_MODEL), tile(D_A), tile(REST), hbm, hbm, hbm,
                   whole((1, 2 * D_MODEL)), whole((1, D_MODEL)), whole((1, D_B)), whole((1, D_B)),
                   whole((N_GROUPS, SGU_CHUNK, SGU_CHUNK)), whole((N_GROUPS, SGU_CHUNK, 1)), whole((1, 1))),
        out_shape=(jax.ShapeDtypeStruct((s, D_MODEL), F32), jax.ShapeDtypeStruct((s, D_A), BF16),
                   jax.ShapeDtypeStruct((s, REST), BF16),
                   jax.ShapeDtypeStruct((D_MODEL, D_MODEL), F32), jax.ShapeDtypeStruct((D_A, D_MODEL), F32),
                   jax.ShapeDtypeStruct((D_B, D_MODEL), F32),
                   jax.ShapeDtypeStruct((1, 2 * D_MODEL), F32), jax.ShapeDtypeStruct((1, D_MODEL), F32),
                   jax.ShapeDtypeStruct((1, D_B), F32), jax.ShapeDtypeStruct((1, D_B), F32),
                   jax.ShapeDtypeStruct((N_GROUPS, SGU_CHUNK, SGU_CHUNK), F32),
                   jax.ShapeDtypeStruct((N_GROUPS, SGU_CHUNK, 1), F32), jax.ShapeDtypeStruct((1, 1), F32)),
        scratch_shapes=[pltpu.VMEM((D_A, D_MODEL), BF16), pltpu.VMEM((D_B, D_MODEL), BF16),
                        pltpu.VMEM((D_MODEL, D_MODEL), BF16), pltpu.VMEM((N_GROUPS, SGU_CHUNK, SGU_CHUNK), BF16),
                        pltpu.VMEM((D_MODEL, D_MODEL), F32), pltpu.VMEM((D_A, D_MODEL), F32),
                        pltpu.VMEM((D_B, D_MODEL), F32),
                        pltpu.SemaphoreType.DMA((3,))],
        compiler_params=_params(56),
    )(x, target, attn_out, *([z] * (N_COL_BLOCKS - 3)), w_pa, w_pb, w_out, b_gate, ln_g, ln_b, w_s, b_s, final_g)


def _proj_bwd_x(dqkv, drest, x, dx2, norm_g, w_in_t, to_chip, small):
    s = x.shape[0]
    tm = 512 if s % 512 == 0 else TOKEN_TILE
    nt = s // tm
    n = len(to_chip)
    ws_shape = (N_GROUPS * SGU_CHUNK, SGU_CHUNK)

    def body(*refs):
        dqkv_ref, dr_ref, x_ref, dx2_ref, g_ref, w_hbm = refs[:6]
        to_chip_refs = refs[6:6 + n]
        ng_ref, bg_ref, rel_ref, lng_ref, lnb_ref, fg_ref, loss_ref, bs_ref, ws_ref = refs[6 + n:15 + n]
        dx_ref = refs[15 + n]
        from_chip_refs = refs[16 + n:16 + 2 * n]
        slab_land, ws_land = refs[16 + 2 * n:18 + 2 * n]
        w, slab_stage, ws_stage, sem, send_sems, recv_sems, gather_send, gather_recv, keep_sems = refs[18 + 2 * n:]
        i = pl.program_id(0)
        me = _flat_id(_my_pos())
        gather = _SlotGather([slab_land, ws_land], gather_send, gather_recv, own=[slab_stage, ws_stage])
        keep = [pltpu.make_async_copy(stage, land.at[me], keep_sems.at[k]) for k, (stage, land) in enumerate(
            ((slab_stage, slab_land), (ws_stage, ws_land)))]

        @pl.when(i == 0)
        def _():
            cp = pltpu.make_async_copy(w_hbm, w, sem)
            cp.start()
            slab_stage[...] = jnp.zeros(slab_stage.shape, F32)
            slab_stage[ROW_NORM_G:ROW_NORM_G + 1, :] = ng_ref[...]
            slab_stage[ROW_B_GATE:ROW_B_GATE + 1, :] = bg_ref[:, :D_MODEL]
            slab_stage[ROW_B_GATE + 1:ROW_B_GATE + 2, :] = bg_ref[:, D_MODEL:]
            slab_stage[ROW_LN_G:ROW_LN_G + 1, :D_B] = lng_ref[...]
            slab_stage[ROW_LN_B:ROW_LN_B + 1, :D_B] = lnb_ref[...]
            slab_stage[ROW_FINAL_G:ROW_FINAL_G + 1, :] = fg_ref[...]
            slab_stage[ROW_LOSS:ROW_LOSS + 1, :1] = loss_ref[...]
            slab_stage[ROW_REL:ROW_REL + N_HEADS, :N_REL_PAD] = rel_ref[...]
            eye = (lax.broadcasted_iota(jnp.int32, (SGU_CHUNK, SGU_CHUNK), 0)
                   == lax.broadcasted_iota(jnp.int32, (SGU_CHUNK, SGU_CHUNK), 1))
            for g in range(N_GROUPS):
                row = jnp.sum(jnp.where(eye, bs_ref[g], 0.0), axis=0, keepdims=True)
                slab_stage[ROW_B_S + g:ROW_B_S + g + 1, :SGU_CHUNK] = row
            ws_stage[...] = ws_ref[...]
            for cp_keep in keep:
                cp_keep.start()
            gather.start()
            for rc in _owner_copies(to_chip_refs, from_chip_refs, send_sems, recv_sems):
                rc.start()
            cp.wait()

        @pl.when(i == nt // 2)
        def _():
            gather.pass_on()

        dh = None
        for c in range(N_COL_BLOCKS):
            dz = dqkv_ref[c] if c < 3 else dr_ref[:, (c - 3) * COL_BLOCK:(c - 2) * COL_BLOCK]
            part = _dot(dz, w[c * COL_BLOCK:(c + 1) * COL_BLOCK, :])
            dh = part if dh is None else dh + part
        xf = x_ref[...]
        r = lax.rsqrt(jnp.mean(xf * xf, axis=-1, keepdims=True) + EPS)
        xn = xf * r
        gh = dh * g_ref[...]
        dx_ref[...] = r * (gh - xn * jnp.mean(gh * xn, axis=-1, keepdims=True)) + dx2_ref[...]

        @pl.when(i == nt - 1)
        def _():
            gather.finish()
            for cp_keep in keep:
                cp_keep.wait()
            for rc in _owner_copies(to_chip_refs, from_chip_refs, send_sems, recv_sems):
                rc.wait_recv()
                rc.wait_send()

    hbm = pl.BlockSpec(memory_space=pl.ANY)
    whole = lambda a: pl.BlockSpec(a.shape, lambda i: (0,) * a.ndim)
    lands = ((N_DEV, SLAB_ROWS, D_MODEL), (N_DEV,) + ws_shape)
    return pl.pallas_call(
        body, name="proj_bwd_x",
        grid=(nt,),
        in_specs=[pl.BlockSpec((3, tm, D_A), lambda i: (0, i, 0)),
                  pl.BlockSpec((tm, REST), lambda i: (i, 0)),
                  pl.BlockSpec((tm, D_MODEL), lambda i: (i, 0)),
                  pl.BlockSpec((tm, D_MODEL), lambda i: (i, 0)),
                  pl.BlockSpec((1, D_MODEL), lambda i: (0, 0)),
                  hbm] + [hbm] * n + [whole(a) for a in small],
        out_specs=(pl.BlockSpec((tm, D_MODEL), lambda i: (i, 0)),) + (hbm,) * (n + 2),
        out_shape=(jax.ShapeDtypeStruct((s, D_MODEL), F32),)
        + tuple(jax.ShapeDtypeStruct(t.shape, t.dtype) for t in to_chip)
        + tuple(jax.ShapeDtypeStruct(shape, F32) for shape in lands),
        scratch_shapes=[pltpu.VMEM((D_IN, D_MODEL), BF16)]
        + [pltpu.VMEM(shape[1:], F32) for shape in lands]
        + [pltpu.SemaphoreType.DMA, pltpu.SemaphoreType.DMA((n, 3)), pltpu.SemaphoreType.DMA((n, 3)),
           pltpu.SemaphoreType.DMA((2, N_DEV - 1)), pltpu.SemaphoreType.DMA((2, N_DEV - 1)),
           pltpu.SemaphoreType.DMA((2,))],
        compiler_params=_params(56),
    )(dqkv, drest, x, dx2, norm_g, w_in_t, *to_chip, *small)


def _proj_bwd_w(xn, dqkv, drest, norm_g, w_in_t):
    s = xn.shape[0]
    tk = min(s, 1024)
    nk = s // tk

    def body(xn_ref, dqkv_ref, dr_ref, g_ref, w_ref, o_ref, dg_ref, acc):
        j = pl.program_id(0)
        i = pl.program_id(1)

        @pl.when((j == 0) & (i == 0))
        def _():
            dg_ref[...] = jnp.zeros(dg_ref.shape, F32)

        @pl.when(i == 0)
        def _():
            acc[...] = jnp.zeros(acc.shape, F32)

        @pl.when(j < 3)
        def _():
            acc[...] += _dot(dqkv_ref[...], xn_ref[...], TN)

        @pl.when(j >= 3)
        def _():
            acc[...] += _dot(dr_ref[...], xn_ref[...], TN)

        @pl.when(i == nk - 1)
        def _():
            m = acc[...]
            o_ref[...] = (m * g_ref[...]).astype(BF16)
            dg_ref[...] += jnp.sum(m * w_ref[...].astype(F32), axis=0, keepdims=True)

    return pl.pallas_call(
        body, name="proj_bwd_w",
        grid=(N_COL_BLOCKS, nk),
        in_specs=[pl.BlockSpec((tk, D_MODEL), lambda j, i: (i, 0)),
                  pl.BlockSpec((None, tk, COL_BLOCK),
                               lambda j, i: (jnp.minimum(j, 2), jnp.where(j < 3, i, nk - 1), 0)),
                  pl.BlockSpec((tk, COL_BLOCK),
                               lambda j, i: (jnp.where(j >= 3, i, 0), jnp.maximum(j - 3, 0))),
                  pl.BlockSpec((1, D_MODEL), lambda j, i: (0, 0)),
                  pl.BlockSpec((COL_BLOCK, D_MODEL), lambda j, i: (j, 0))],
        out_specs=(pl.BlockSpec((COL_BLOCK, D_MODEL), lambda j, i: (j, 0)),
                   pl.BlockSpec((1, D_MODEL), lambda j, i: (0, 0))),
        out_shape=(jax.ShapeDtypeStruct((D_IN, D_MODEL), BF16), jax.ShapeDtypeStruct((1, D_MODEL), F32)),
        scratch_shapes=[pltpu.VMEM((COL_BLOCK, D_MODEL), F32)],
        compiler_params=_params(40),
    )(xn, dqkv, drest, norm_g, w_in_t)


def _adamw_math(w, g, m, v):
    c1 = 1.0 - ADAM_B1 ** ADAM_STEP
    c2 = 1.0 - ADAM_B2 ** ADAM_STEP
    nm = ADAM_B1 * m + (1.0 - ADAM_B1) * g
    nv = ADAM_B2 * v + (1.0 - ADAM_B2) * (g * g)
    return -ADAM_LR * ((nm / c1) / (jnp.sqrt(nv / c2) + ADAM_EPS) + ADAM_WD * w), nm, nv


def _adamw(name, w, g, m, v, from_chip):
    rows, cols = w.shape
    tr = rows if rows * cols <= 512 * 1024 else next(t for t in range(256, 7, -8) if rows % t == 0)

    def body(w_ref, g_ref, m_ref, v_ref, t_ref, g_out, d_ref, nm_ref, nv_ref):
        gg = g_ref[...]
        for j in range(3):
            gg = gg + t_ref[j].astype(F32)
        g_out[...] = gg
        d_ref[...], nm_ref[...], nv_ref[...] = _adamw_math(w_ref[...], gg, m_ref[...], v_ref[...])

    spec = pl.BlockSpec((tr, cols), lambda i: (i, 0))
    shape = jax.ShapeDtypeStruct((rows, cols), F32)
    return pl.pallas_call(
        body, name=name,
        grid=(rows // tr,),
        in_specs=[spec] * 4 + [pl.BlockSpec((3, tr, cols), lambda i: (0, i, 0))],
        out_specs=(spec,) * 4, out_shape=(shape,) * 4,
        compiler_params=_params(32),
    )(w, g, m, v, from_chip)


_SMALL = (("norm_g", (1, D_MODEL)), ("b_gate", (1, 2 * D_MODEL)), ("rel_bias", (N_HEADS, N_REL)),
          ("sgu_ln_g", (1, D_B)), ("sgu_ln_b", (1, D_B)), ("w_s", (N_GROUPS * SGU_CHUNK, SGU_CHUNK)),
          ("b_s", (N_GROUPS, SGU_CHUNK)), ("final_g", (1, D_MODEL)))


def _adamw_small(slabs, ws_all, weights, moments_m, moments_v):
    k = len(_SMALL)

    def total(ref):
        acc = ref[0]
        for d in range(1, N_DEV):
            acc = acc + ref[d]
        return acc

    def body(*refs):
        slab_ref, ws_ref = refs[:2]
        w_refs, m_refs, v_refs = refs[2:2 + k], refs[2 + k:2 + 2 * k], refs[2 + 2 * k:2 + 3 * k]
        outs = refs[2 + 3 * k:]
        slab = total(slab_ref)
        grads = {
            "norm_g": slab[ROW_NORM_G:ROW_NORM_G + 1, :],
            "b_gate": jnp.concatenate([slab[ROW_B_GATE:ROW_B_GATE + 1, :], slab[ROW_B_GATE + 1:ROW_B_GATE + 2, :]], axis=1),
            "rel_bias": slab[ROW_REL:ROW_REL + N_HEADS, :N_REL],
            "sgu_ln_g": slab[ROW_LN_G:ROW_LN_G + 1, :D_B],
            "sgu_ln_b": slab[ROW_LN_B:ROW_LN_B + 1, :D_B],
            "w_s": total(ws_ref),
            "b_s": slab[ROW_B_S:ROW_B_S + N_GROUPS, :SGU_CHUNK],
            "final_g": slab[ROW_FINAL_G:ROW_FINAL_G + 1, :],
        }
        for n, (name, _) in enumerate(_SMALL):
            g = grads[name]
            outs[n][...] = g
            outs[k + n][...], outs[2 * k + n][...], outs[3 * k + n][...] = _adamw_math(
                w_refs[n][...], g, m_refs[n][...], v_refs[n][...])
        outs[4 * k][...] = slab[ROW_LOSS:ROW_LOSS + 1, :1]

    vmem = pl.BlockSpec(memory_space=pltpu.VMEM)
    shapes = tuple(jax.ShapeDtypeStruct(shape, F32) for _, shape in _SMALL)
    return pl.pallas_call(
        body, name="adamw_small",
        out_shape=shapes * 4 + (jax.ShapeDtypeStruct((1, 1), F32),),
        in_specs=[vmem] * (2 + 3 * k), out_specs=tuple([vmem] * (4 * k + 1)),
        compiler_params=_params(16),
    )(slabs, ws_all, *weights, *moments_m, *moments_v)


def _pad_rel(a):
    return jnp.pad(a.reshape(N_HEADS, N_REL), ((0, 0), (0, N_REL_PAD - N_REL)))


def kernel(x, norm_g, w_in, b_gate, rel_bias, sgu_ln_g, sgu_ln_b, w_s, b_s, w_pa, w_pb, w_out, final_g, loss_target, m_norm_g, m_w_in, m_b_gate, m_rel_bias, m_sgu_ln_g, m_sgu_ln_b, m_w_s, m_b_s, m_w_pa, m_w_pb, m_w_out, m_final_g, v_norm_g, v_w_in, v_b_gate, v_rel_bias, v_sgu_ln_g, v_sgu_ln_b, v_w_s, v_b_s, v_w_pa, v_w_pb, v_w_out, v_final_g):
    s = x.shape[1]
    xs = x.reshape(s, D_MODEL)
    tgt = loss_target.reshape(s, D_MODEL)

    bias_table = _bias_table(_pad_rel(rel_bias))
    w_in_t = jnp.swapaxes(w_in[0], 0, 1)
    qkv, x_norm, w_in_t_full = _gather_proj_fwd(xs, norm_g, w_in_t)
    attn_out, g_pa, g_pb, g_out = _attn_fwd(qkv, bias_table, (w_pa[0], w_pb[0], w_out[0]))
    w_pa_full = jnp.transpose(g_pa, (1, 0, 2)).reshape(D_A, D_MODEL)
    w_pb_full = jnp.transpose(g_pb, (1, 0, 2)).reshape(D_B, D_MODEL)
    w_out_full = g_out.reshape(D_MODEL, D_MODEL)

    (dx2, d_attn, drest, dw_out, dw_pa, dw_pb, d_bgate, d_fg, d_lng, d_lnb, d_ws, d_bs, loss_part) = _mid_fwd_bwd(
        xs, tgt, attn_out, qkv, w_pa_full, w_pb_full, w_out_full, b_gate, sgu_ln_g, sgu_ln_b, w_s[0],
        b_s.reshape(N_GROUPS, SGU_CHUNK, 1), final_g.reshape(1, D_MODEL))

    own_pa, own_pb, own_out, tc_pa, tc_pb, tc_out = _reduce_chip(
        "reduce_chip_proj", (dw_pa, dw_pb, dw_out), (1, 1, 0))
    dqkv, dbias, fc_pa, fc_pb, fc_out = _attn_bwd(qkv, bias_table, d_attn, (tc_pa, tc_pb, tc_out))
    d_rel = _bias_grad(dbias)
    dw_in_t, d_ng = _proj_bwd_w(x_norm, dqkv, drest, norm_g, w_in_t_full)
    own_in, tc_in = _reduce_chip("reduce_chip_in", (dw_in_t,), (0,))
    grad_x, fc_in, slabs, ws_all = _proj_bwd_x(
        dqkv, drest, xs, dx2, norm_g, w_in_t_full, (tc_in,),
        (d_ng, d_bgate, d_rel, d_lng, d_lnb, d_fg, loss_part, d_bs, d_ws.reshape(N_GROUPS * SGU_CHUNK, SGU_CHUNK)))
    big = {"w_in": tuple(jnp.swapaxes(t, 0, 1)[None] for t in _adamw(
        "adamw_w_in", w_in_t, own_in, jnp.swapaxes(m_w_in[0], 0, 1), jnp.swapaxes(v_w_in[0], 0, 1), fc_in))}
    for name, w, g, fc, m, v in (("w_pa", w_pa, own_pa, fc_pa, m_w_pa, v_w_pa),
                                 ("w_pb", w_pb, own_pb, fc_pb, m_w_pb, v_w_pb),
                                 ("w_out", w_out, own_out, fc_out, m_w_out, v_w_out)):
        big[name] = tuple(t[None] for t in _adamw("adamw_" + name, w[0], g, m[0], v[0], fc))

    as_2d = lambda leaves: [a.reshape(shape) for a, (_, shape) in zip(leaves, _SMALL)]
    small_out = _adamw_small(
        slabs, ws_all, as_2d((norm_g, b_gate, rel_bias, sgu_ln_g, sgu_ln_b, w_s, b_s, final_g)),
        as_2d((m_norm_g, m_b_gate, m_rel_bias, m_sgu_ln_g, m_sgu_ln_b, m_w_s, m_b_s, m_final_g)),
        as_2d((v_norm_g, v_b_gate, v_rel_bias, v_sgu_ln_g, v_sgu_ln_b, v_w_s, v_b_s, v_final_g)))
    small_index = {name: n for n, (name, _) in enumerate(_SMALL)}

    def leaf(kind, name, like):
        if name in big:
            return big[name][kind]
        return small_out[kind * len(_SMALL) + small_index[name]].reshape(like.shape)

    weights = (("norm_g", norm_g), ("w_in", w_in), ("b_gate", b_gate), ("rel_bias", rel_bias), ("sgu_ln_g", sgu_ln_g),
               ("sgu_ln_b", sgu_ln_b), ("w_s", w_s), ("b_s", b_s), ("w_pa", w_pa), ("w_pb", w_pb), ("w_out", w_out),
               ("final_g", final_g))
    outs = [small_out[-1].reshape(()), grad_x.reshape(x.shape)]
    for kind in range(4):
        outs.extend(leaf(kind, name, like) for name, like in weights)
    return tuple(outs)
```

```python
import functools
import math

import jax
import jax.numpy as jnp
from jax import lax
from jax.experimental import pallas as pl
from jax.experimental.pallas import tpu as pltpu

F32 = jnp.float32
BF16 = jnp.bfloat16
MESH = pl.DeviceIdType.MESH
N_DEV = 8

D_MODEL = 1024
D_A = 512
D_B = 512
D_IN = 5632
N_HEADS = 8
HEAD_DIM = 64
N_PREV = 8
REL_CLIP = 128
N_REL = 2 * REL_CLIP + 1
N_REL_PAD = 384
SGU_CHUNK = 128
N_GROUPS = 4
EPS = 1e-6
NEG_INF = -1e30
Q_SCALE = HEAD_DIM ** -0.5

Q_BLOCK = 256
K_SPAN = 768
Z_PAD = K_SPAN - Q_BLOCK
ROLL_W = 1024
COL_BLOCK = 512
N_COL_BLOCKS = D_IN // COL_BLOCK
REST = D_IN - 3 * D_A
TOKEN_TILE = 256

ADAM_LR = 0.001
ADAM_B1 = 0.9
ADAM_B2 = 0.999
ADAM_EPS = 1e-08
ADAM_WD = 0.01
ADAM_STEP = 10

GELU_C = math.sqrt(2.0 / math.pi)
GELU_A = 0.044715

NT = (((1,), (1,)), ((), ()))
TN = (((0,), (0,)), ((), ()))
HIGHEST = lax.Precision.HIGHEST


def _params(vmem_mb, **kw):
    return pltpu.CompilerParams(vmem_limit_bytes=vmem_mb * 1024 * 1024, **kw)


def _dot(a, b, dims=None):
    if dims is None:
        return jnp.dot(a, b, preferred_element_type=F32)
    return lax.dot_general(a, b, dims, preferred_element_type=F32)


def _sigmoid(x):
    return 0.5 * jnp.tanh(0.5 * x) + 0.5


def _gelu_and_grad(u):
    u2 = u * u
    t = jnp.tanh(GELU_C * (u + GELU_A * u * u2))
    half = 0.5 * (1.0 + t)
    g = u * half
    dg = half + 0.5 * u * (1.0 - t * t) * (GELU_C * (1.0 + 3.0 * GELU_A * u2))
    return g, dg


def _my_pos():
    return lax.axis_index("x"), lax.axis_index("y"), lax.axis_index("c")


def _flat_id(pos):
    return 4 * pos[0] + 2 * pos[1] + pos[2]


def _other_chips(pos):
    x, y, _ = pos
    return ((1 - x, y), (x, 1 - y), (1 - x, 1 - y))


class _SlotGather:
    def __init__(self, bufs, send_sems, recv_sems, own=None):
        self.bufs, self.send_sems, self.recv_sems = bufs, send_sems, recv_sems
        self.own = own if own is not None else [None] * len(bufs)
        x, y, c = _my_pos()
        self.c, self.me, self.sib = c, (x, y, c), (x, y, 1 - c)
        self.chips = _other_chips(self.me)

    def _copy(self, a, k, block, to):
        slot = _flat_id(block)
        src = self.own[a] if (k < 4 and self.own[a] is not None) else self.bufs[a].at[slot]
        return pltpu.make_async_remote_copy(
            src_ref=src, dst_ref=self.bufs[a].at[slot],
            send_sem=self.send_sems.at[a, k], recv_sem=self.recv_sems.at[a, k], device_id=to, device_id_type=MESH)

    def _own_sends(self):
        n = len(self.bufs)
        return ([self._copy(a, 1 + j, self.me, (*chip, self.c)) for j, chip in enumerate(self.chips) for a in range(n)]
                + [self._copy(a, 0, self.me, self.sib) for a in range(n)])

    def _passes(self):
        return [self._copy(a, 4 + j, (*chip, self.c), self.sib)
                for j, chip in enumerate(self.chips) for a in range(len(self.bufs))]

    def start(self):
        for cp in self._own_sends():
            cp.start()

    def pass_on(self):
        for j, chip in enumerate(self.chips):
            for a in range(len(self.bufs)):
                self._copy(a, 1 + j, (*chip, self.c), self.me).wait_recv()
                self._copy(a, 4 + j, (*chip, self.c), self.sib).start()

    def finish(self):
        for a in range(len(self.bufs)):
            self._copy(a, 0, self.sib, self.me).wait_recv()
            for j, chip in enumerate(self.chips):
                self._copy(a, 4 + j, (*chip, 1 - self.c), self.me).wait_recv()
        for cp in self._own_sends() + self._passes():
            cp.wait_send()


def _reduce_chip(name, parts, sharded_dim):
    n = len(parts)
    shapes = []
    for p, dim in zip(parts, sharded_dim):
        shape = list(p.shape)
        shape[dim] //= N_DEV
        shapes.append(tuple(shape))

    def body(*refs):
        full, own, to_chip = refs[:n], refs[n:2 * n], refs[2 * n:3 * n]
        ins, from_sib = refs[3 * n:4 * n], refs[4 * n:5 * n]
        send_sems, recv_sems = refs[5 * n], refs[5 * n + 1]
        x, y, c = _my_pos()
        sib = (x, y, 1 - c)
        chips = ((x, y),) + _other_chips((x, y, c))
        for a in range(n):
            rows, cols = shapes[a]
            for d in range(N_DEV):
                if sharded_dim[a] == 0:
                    ins[a][d] = full[a][d * rows:(d + 1) * rows, :].astype(BF16)
                else:
                    ins[a][d] = full[a][:, d * cols:(d + 1) * cols].astype(BF16)

        def to_sibling(a, r):
            return pltpu.make_async_remote_copy(
                src_ref=ins[a].at[_flat_id((*chips[r], 1 - c))], dst_ref=from_sib[a].at[r],
                send_sem=send_sems.at[a, r], recv_sem=recv_sems.at[a, r], device_id=sib, device_id_type=MESH)

        sends = [to_sibling(a, r) for r in (1, 2, 3, 0) for a in range(n)]
        for cp in sends:
            cp.start()
        for r in (1, 2, 3, 0):
            for a in range(n):
                to_sibling(a, r).wait_recv()
                both = ins[a][_flat_id((*chips[r], c))].astype(F32) + from_sib[a][r].astype(F32)
                if r == 0:
                    own[a][...] = both
                else:
                    to_chip[a][r - 1] = both.astype(BF16)
        for cp in sends:
            cp.wait_send()

    vmem = pl.BlockSpec(memory_space=pltpu.VMEM)
    return pl.pallas_call(
        body, name=name,
        out_shape=tuple(jax.ShapeDtypeStruct(sh, F32) for sh in shapes)
        + tuple(jax.ShapeDtypeStruct((3,) + sh, BF16) for sh in shapes),
        in_specs=[vmem] * n, out_specs=tuple([vmem] * (2 * n)),
        scratch_shapes=[pltpu.VMEM((N_DEV,) + sh, BF16) for sh in shapes]
        + [pltpu.VMEM((4,) + sh, BF16) for sh in shapes]
        + [pltpu.SemaphoreType.DMA((n, 4)), pltpu.SemaphoreType.DMA((n, 4))],
        compiler_params=_params(56),
    )(*parts)


def _owner_copies(to_chip, from_chip, send_sems, recv_sems):
    x, y, c = _my_pos()
    return [pltpu.make_async_remote_copy(
        src_ref=to_chip[a].at[j], dst_ref=from_chip[a].at[j],
        send_sem=send_sems.at[a, j], recv_sem=recv_sems.at[a, j], device_id=(*chip, c), device_id_type=MESH)
        for a in range(len(to_chip)) for j, chip in enumerate(_other_chips((x, y, c)))]


ROW_NORM_G, ROW_B_GATE, ROW_LN_G, ROW_LN_B, ROW_FINAL_G, ROW_LOSS, ROW_REL, ROW_B_S, SLAB_ROWS = 0, 1, 3, 4, 5, 6, 8, 16, 24


def _rel_index(e):
    lo, hi = Z_PAD - REL_CLIP, Z_PAD + REL_CLIP
    return jnp.where(e <= lo, 2 * REL_CLIP, jnp.where(e < hi, hi - e, jnp.where(e <= K_SPAN, 0, 2 * REL_CLIP)))


def _bias_table(rel_bias_pad):
    def body(rb_ref, bt_ref):
        c = lax.broadcasted_iota(jnp.int32, (N_REL_PAD, ROLL_W), 1)
        r = lax.broadcasted_iota(jnp.int32, (N_REL_PAD, ROLL_W), 0)
        pick = (r == _rel_index(c)).astype(F32)
        rows = jnp.dot(rb_ref[...], pick, precision=HIGHEST, preferred_element_type=F32)
        qc = lax.broadcasted_iota(jnp.int32, (Q_BLOCK, K_SPAN), 0) >> 6
        kc = lax.broadcasted_iota(jnp.int32, (Q_BLOCK, K_SPAN), 1) >> 6
        band = (kc >= qc) & (kc <= qc + N_PREV)
        for h in range(N_HEADS):
            t = jnp.broadcast_to(rows[h:h + 1, :], (Q_BLOCK, ROLL_W))
            t = pltpu.roll(t, 0, 1, stride=1, stride_axis=0)
            bt_ref[h] = jnp.where(band, t[:, :K_SPAN], NEG_INF)

    return pl.pallas_call(
        body, name="bias_table",
        out_shape=jax.ShapeDtypeStruct((N_HEADS, Q_BLOCK, K_SPAN), F32),
        compiler_params=_params(32),
    )(rel_bias_pad)


def _bias_grad(dbias):
    def body(a_ref, o_ref):
        rr = lax.broadcasted_iota(jnp.int32, (Q_BLOCK, Q_BLOCK), 0)
        cc = lax.broadcasted_iota(jnp.int32, (Q_BLOCK, Q_BLOCK), 1)
        flip = (rr + cc == Q_BLOCK - 1).astype(F32)
        c = lax.broadcasted_iota(jnp.int32, (ROLL_W, N_REL_PAD), 0)
        r = lax.broadcasted_iota(jnp.int32, (ROLL_W, N_REL_PAD), 1)
        e = jnp.where(c >= Q_BLOCK - 1, c - (Q_BLOCK - 1), c + (ROLL_W - Q_BLOCK + 1))
        pick = (r == _rel_index(e)).astype(F32)
        sums = []
        for h in range(N_HEADS):
            a = jnp.dot(flip, a_ref[h], precision=HIGHEST, preferred_element_type=F32)
            a = jnp.concatenate([a, jnp.zeros((Q_BLOCK, ROLL_W - K_SPAN), F32)], axis=1)
            a = pltpu.roll(a, 0, 1, stride=1, stride_axis=0)
            sums.append(jnp.sum(a, axis=0, keepdims=True))
        diag = jnp.concatenate(sums, axis=0)
        o_ref[...] = jnp.dot(diag, pick, precision=HIGHEST, preferred_element_type=F32)

    return pl.pallas_call(
        body, name="bias_grad",
        out_shape=jax.ShapeDtypeStruct((N_HEADS, N_REL_PAD), F32),
        compiler_params=_params(32),
    )(dbias)


def _gather_proj_fwd(x, norm_g, w_in_t):
    s = x.shape[0]
    tm = 512 if s % 512 == 0 else TOKEN_TILE
    nt = s // tm
    n_pad = Z_PAD // tm
    shard_w = w_in_t.shape[0]
    chip_w = 2 * shard_w
    n_chips = N_DEV // 2

    def body(order_ref, x_ref, g_ref, win_hbm, z_ref, xn_ref, wt_hbm, stage, wchip, hb, win_f32, send_sems, recv_sems,
             local_sems):
        j = pl.program_id(0)
        i = pl.program_id(1)
        x_, y_, c_ = _my_pos()
        me, sib = (x_, y_, c_), (x_, y_, 1 - c_)
        near = _other_chips(me)
        pick = lambda a, b: tuple(jnp.where(c_ == 0, u, v) for u, v in zip(a, b))
        passed_from, passed_to = pick(near[0], near[1]), pick(near[1], near[0])

        def rows_of(block):
            return wt_hbm.at[pl.ds(pl.multiple_of(_flat_id(block) * shard_w, 16), shard_w), :]

        def copy(k, block, to, own=False):
            return pltpu.make_async_remote_copy(
                src_ref=stage if own else rows_of(block), dst_ref=rows_of(block),
                send_sem=send_sems.at[k], recv_sem=recv_sems.at[k], device_id=to, device_id_type=MESH)

        def sends():
            return ([copy(0, me, sib, True), copy(1, me, (*near[0], c_), True), copy(2, me, (*near[1], c_), True),
                     copy(3, (*passed_from, c_), (*passed_to, c_))]
                    + [copy(4 + n, (*near[n], c_), sib) for n in range(3)])

        keep = pltpu.make_async_copy(stage, rows_of(me), local_sems.at[0])

        def fetch(chip):
            first = pl.multiple_of((2 * chip[0] + chip[1]) * chip_w, 16)
            cp = pltpu.make_async_copy(wt_hbm.at[pl.ds(first, chip_w), :], wchip, local_sems.at[1])
            cp.start()
            cp.wait()

        @pl.when((j == 0) & (i == 0))
        def _():
            load = pltpu.make_async_copy(win_hbm, win_f32, local_sems.at[1])
            load.start()
            load.wait()
            stage[...] = win_f32[...].astype(BF16)
            keep.start()
            for cp in sends()[:3]:
                cp.start()
            copy(0, sib, me).wait_recv()
            keep.wait()
            fetch((x_, y_))

        @pl.when((j == 1) & (i == 0))
        def _():
            copy(1, (*near[0], c_), me).wait_recv()
            copy(2, (*near[1], c_), me).wait_recv()
            for cp in sends()[3:6]:
                cp.start()
            copy(4, (*near[0], 1 - c_), me).wait_recv()
            fetch(near[0])

        @pl.when((j == 2) & (i == 0))
        def _():
            copy(5, (*near[1], 1 - c_), me).wait_recv()
            fetch(near[1])

        @pl.when((j == 3) & (i == 0))
        def _():
            copy(3, (*near[2], c_), me).wait_recv()
            copy(6, (*near[2], c_), sib).start()
            copy(6, (*near[2], 1 - c_), me).wait_recv()
            fetch(near[2])

        @pl.when(i < n_pad)
        def _():
            z_ref[...] = jnp.zeros(z_ref.shape, BF16)

        @pl.when(i >= n_pad)
        def _():
            rows = pl.ds(pl.multiple_of((i - n_pad) * tm, tm), tm)

            @pl.when(j == 0)
            def _():
                xf = x_ref[...]
                xn = xf * lax.rsqrt(jnp.mean(xf * xf, axis=-1, keepdims=True) + EPS)
                hb[rows, :] = (xn * g_ref[...]).astype(BF16)
                xn_ref[...] = xn.astype(BF16)

            blk = _dot(hb[rows, :], wchip[...], NT)
            q_scale = jnp.where(order_ref[j] == 0, Q_SCALE, 1.0).astype(F32)
            z_ref[:, :D_A] = (blk[:, :D_A] * q_scale).astype(BF16)
            z_ref[:, D_A:] = blk[:, D_A:].astype(BF16)

        @pl.when((j == n_chips - 1) & (i == n_pad + nt - 1))
        def _():
            for cp in sends():
                cp.wait_send()

    pos = _my_pos()
    order = jnp.stack([2 * cx + cy for cx, cy in ((pos[0], pos[1]),) + _other_chips(pos)]).astype(jnp.int32)
    first_pass = lambda j, i: jnp.where(j == 0, jnp.maximum(i - n_pad, 0), nt - 1)
    grid_spec = pltpu.PrefetchScalarGridSpec(
        num_scalar_prefetch=1,
        grid=(n_chips, n_pad + nt),
        in_specs=[pl.BlockSpec((tm, D_MODEL), lambda j, i, o: (first_pass(j, i), 0)),
                  pl.BlockSpec((1, D_MODEL), lambda j, i, o: (0, 0)),
                  pl.BlockSpec(memory_space=pl.ANY)],
        out_specs=(pl.BlockSpec((tm, chip_w), lambda j, i, o: (i, o[j])),
                   pl.BlockSpec((tm, D_MODEL), lambda j, i, o: (first_pass(j, i), 0)),
                   pl.BlockSpec(memory_space=pl.ANY)),
        scratch_shapes=[pltpu.VMEM((shard_w, D_MODEL), BF16), pltpu.VMEM((chip_w, D_MODEL), BF16),
                        pltpu.VMEM((s, D_MODEL), BF16), pltpu.VMEM(w_in_t.shape, F32),
                        pltpu.SemaphoreType.DMA((N_DEV - 1,)), pltpu.SemaphoreType.DMA((N_DEV - 1,)),
                        pltpu.SemaphoreType.DMA((2,))])
    return pl.pallas_call(
        body, name="gather_proj_fwd",
        grid_spec=grid_spec,
        out_shape=(jax.ShapeDtypeStruct((Z_PAD + s, D_IN), BF16), jax.ShapeDtypeStruct((s, D_MODEL), BF16),
                   jax.ShapeDtypeStruct((D_IN, D_MODEL), BF16)),
        compiler_params=_params(60),
    )(order, x, norm_g, w_in_t)


def _attn_specs(rows):
    pairs = N_HEADS // 2
    return ([pl.BlockSpec((rows, 128), functools.partial(lambda which, p: (0, which * pairs + p), which))
             for which in range(3)]
            + [pl.BlockSpec((2, Q_BLOCK, K_SPAN), lambda p: (p, 0, 0))])


def _head_masks():
    lane = lax.broadcasted_iota(jnp.int32, (1, 128), 1)
    first = lane < HEAD_DIM
    return (first, jnp.logical_not(first))


def _stack_heads(x, masks):
    zero = jnp.zeros((), x.dtype)
    return jnp.concatenate([jnp.where(m, x, zero) for m in masks], axis=0)


STRIP = 16


def _softmax_strips(s_ref, bias_ref, b):
    valid = lax.broadcasted_iota(jnp.int32, (1, K_SPAN), 1) >= Z_PAD - b * Q_BLOCK
    for t in range(2 * Q_BLOCK // STRIP):
        hh, r = divmod(t * STRIP, Q_BLOCK)
        st = s_ref[t * STRIP:(t + 1) * STRIP, :] + bias_ref[hh, r:r + STRIP, :]
        st = jnp.where(valid, st, NEG_INF)
        e = jnp.exp(st - jnp.max(st, axis=-1, keepdims=True))
        yield e * (1.0 / jnp.sum(e, axis=-1, keepdims=True))


def _side_by_side_strips(strips):
    half = len(strips) // 2
    return jnp.concatenate([jnp.concatenate([a, c], axis=1) for a, c in zip(strips[:half], strips[half:])], axis=0)


def _attn_fwd(qkv, bias_table, shards):
    s = qkv.shape[0] - Z_PAD
    nb = s // Q_BLOCK
    n = len(shards)
    pairs = N_HEADS // 2

    def body(*refs):
        q_ref, k_ref, v_ref, bt_ref = refs[:4]
        shard_refs = refs[4:4 + n]
        o_ref = refs[4 + n]
        slot_refs = refs[5 + n:5 + 2 * n]
        stages = refs[5 + 2 * n:5 + 3 * n]
        s_scr, send_sems, recv_sems, local_sems = refs[5 + 3 * n:]
        p_id = pl.program_id(0)
        gather = _SlotGather(slot_refs, send_sems, recv_sems, own=stages)
        keep = [pltpu.make_async_copy(stages[a], slot_refs[a].at[_flat_id(_my_pos())], local_sems.at[a])
                for a in range(n)]

        @pl.when(p_id == 0)
        def _():
            for a in range(n):
                stages[a][...] = shard_refs[a][...].astype(BF16)
                keep[a].start()
            gather.start()

        @pl.when(p_id == 1)
        def _():
            gather.pass_on()

        masks = _head_masks()

        def scores(b, half):
            r0 = pl.multiple_of(b * Q_BLOCK, Q_BLOCK)
            q2 = _stack_heads(q_ref[pl.ds(r0 + Z_PAD, Q_BLOCK), :], masks)
            s_scr[half] = _dot(q2, k_ref[pl.ds(r0, K_SPAN), :], NT)

        def finish(b, half):
            r0 = pl.multiple_of(b * Q_BLOCK, Q_BLOCK)
            v2 = _stack_heads(v_ref[pl.ds(r0, K_SPAN), :], masks)
            p = [st.astype(BF16) for st in _softmax_strips(s_scr.at[half], bt_ref, b)]
            o_ref[pl.ds(r0, Q_BLOCK), :] = _dot(_side_by_side_strips(p), v2)

        def two_blocks(i, carry):
            b = 2 * i
            scores(b + 1, 1)
            finish(b, 0)
            scores(jnp.minimum(b + 2, nb - 1), 0)
            finish(b + 1, 1)
            return carry

        scores(0, 0)
        lax.fori_loop(0, nb // 2, two_blocks, 0)

        @pl.when(p_id == pairs - 1)
        def _():
            gather.finish()
            for cp in keep:
                cp.wait()

    hbm = pl.BlockSpec(memory_space=pl.ANY)
    return pl.pallas_call(
        body, name="attn_fwd",
        grid=(pairs,),
        in_specs=_attn_specs(s + Z_PAD) + [pl.BlockSpec(a.shape, lambda p: (0, 0)) for a in shards],
        out_specs=(pl.BlockSpec((s, 128), lambda p: (0, p)),) + (hbm,) * n,
        out_shape=(jax.ShapeDtypeStruct((s, D_A), F32),)
        + tuple(jax.ShapeDtypeStruct((N_DEV,) + a.shape, BF16) for a in shards),
        scratch_shapes=[pltpu.VMEM(a.shape, BF16) for a in shards]
        + [pltpu.VMEM((2, 2 * Q_BLOCK, K_SPAN), F32),
           pltpu.SemaphoreType.DMA((n, N_DEV - 1)), pltpu.SemaphoreType.DMA((n, N_DEV - 1)),
           pltpu.SemaphoreType.DMA((n,))],
        compiler_params=_params(48),
    )(qkv, qkv, qkv, bias_table, *shards)


def _attn_bwd(qkv, bias_table, d_out, to_chip):
    s = qkv.shape[0] - Z_PAD
    nb = s // Q_BLOCK
    n = len(to_chip)
    pairs = N_HEADS // 2

    def body(*refs):
        q_ref, k_ref, v_ref, bt_ref, do_ref = refs[:5]
        to_chip_refs = refs[5:5 + n]
        dqkv_ref, db_ref = refs[5 + n:7 + n]
        from_chip_refs = refs[7 + n:7 + 2 * n]
        dk_acc, dv_acc, s_scr, dp_scr, send_sems, recv_sems = refs[7 + 2 * n:]
        p_id = pl.program_id(0)

        @pl.when(p_id == 0)
        def _():
            for cp in _owner_copies(to_chip_refs, from_chip_refs, send_sems, recv_sems):
                cp.start()

        dk_acc[...] = jnp.zeros(dk_acc.shape, F32)
        dv_acc[...] = jnp.zeros(dv_acc.shape, F32)
        db_ref[...] = jnp.zeros(db_ref.shape, F32)
        masks = _head_masks()

        def operands(b):
            r0 = pl.multiple_of(b * Q_BLOCK, Q_BLOCK)
            q2 = _stack_heads(q_ref[pl.ds(r0 + Z_PAD, Q_BLOCK), :], masks)
            do2 = _stack_heads(do_ref[pl.ds(r0, Q_BLOCK), :], masks)
            return r0, q2, do2, k_ref[pl.ds(r0, K_SPAN), :]

        def ahead(b, half):
            r0, q2, do2, kcat = operands(b)
            s_scr[half] = _dot(q2, kcat, NT)
            dp_scr[half] = _dot(do2, v_ref[pl.ds(r0, K_SPAN), :], NT)

        def finish(b, half):
            r0, q2, do2, kcat = operands(b)
            p_strips, ds_strips = [], []
            for t, p in enumerate(_softmax_strips(s_scr.at[half], bt_ref, b)):
                hh, r = divmod(t * STRIP, Q_BLOCK)
                dp_t = dp_scr[half, t * STRIP:(t + 1) * STRIP, :]
                ds = p * (dp_t - jnp.sum(p * dp_t, axis=-1, keepdims=True))
                db_ref[hh, r:r + STRIP, :] += ds
                p_strips.append(p.astype(BF16))
                ds_strips.append(ds.astype(BF16))
            dq = _dot(_side_by_side_strips(ds_strips), _stack_heads(kcat, masks))
            dqkv_ref[0, pl.ds(r0, Q_BLOCK), :] = (dq * Q_SCALE).astype(BF16)
            dk_acc[pl.ds(r0, K_SPAN), :] += _dot(jnp.concatenate(ds_strips, axis=0), q2, TN)
            dv_acc[pl.ds(r0, K_SPAN), :] += _dot(jnp.concatenate(p_strips, axis=0), do2, TN)

        def two_blocks(i, carry):
            b = 2 * i
            ahead(b + 1, 1)
            finish(b, 0)
            ahead(jnp.minimum(b + 2, nb - 1), 0)
            finish(b + 1, 1)
            return carry

        ahead(0, 0)
        lax.fori_loop(0, nb // 2, two_blocks, 0)
        dqkv_ref[1] = dk_acc[Z_PAD:, :].astype(BF16)
        dqkv_ref[2] = dv_acc[Z_PAD:, :].astype(BF16)

        @pl.when(p_id == pairs - 1)
        def _():
            for cp in _owner_copies(to_chip_refs, from_chip_refs, send_sems, recv_sems):
                cp.wait_recv()
                cp.wait_send()

    hbm = pl.BlockSpec(memory_space=pl.ANY)
    return pl.pallas_call(
        body, name="attn_bwd",
        grid=(pairs,),
        in_specs=_attn_specs(s + Z_PAD) + [pl.BlockSpec((s, 128), lambda p: (0, p))] + [hbm] * n,
        out_specs=(pl.BlockSpec((3, s, 128), lambda p: (0, 0, p)),
                   pl.BlockSpec((2, Q_BLOCK, K_SPAN), lambda p: (p, 0, 0))) + (hbm,) * n,
        out_shape=(jax.ShapeDtypeStruct((3, s, D_A), BF16),
                   jax.ShapeDtypeStruct((N_HEADS, Q_BLOCK, K_SPAN), F32))
        + tuple(jax.ShapeDtypeStruct(t.shape, t.dtype) for t in to_chip),
        scratch_shapes=[pltpu.VMEM((s + Z_PAD, 128), F32), pltpu.VMEM((s + Z_PAD, 128), F32),
                        pltpu.VMEM((2, 2 * Q_BLOCK, K_SPAN), F32), pltpu.VMEM((2, 2 * Q_BLOCK, K_SPAN), F32),
                        pltpu.SemaphoreType.DMA((n, 3)), pltpu.SemaphoreType.DMA((n, 3))],
        compiler_params=_params(56),
    )(qkv, qkv, qkv, bias_table, d_out, *to_chip)


def _mid_fwd_bwd(x, target, attn_out, z, w_pa, w_pb, w_out, b_gate, ln_g, ln_b, w_s, b_s, final_g):
    s = x.shape[0]
    tm = TOKEN_TILE
    nt = s // tm

    def body(x_ref, t_ref, oa_ref, ga_ref, ub_ref, vb_ref, gb_ref, ta0_ref, ta1_ref, tb0_ref, tb1_ref,
             wpa_hbm, wpb_hbm, wout_hbm, bg_ref, lng_ref, lnb_ref, ws_ref, bs_ref, fg_ref,
             dx2_ref, doa_ref, dz_ref, dwout_hbm, dwpa_hbm, dwpb_hbm, dbg_ref, dfg_ref, dlng_ref, dlnb_ref, dws_ref,
             dbs_ref, loss_ref,
             wpa, wpb, wout, wmix, acc_out, acc_pa, acc_pb, sem):
        i = pl.program_id(0)

        @pl.when(i == 0)
        def _():
            loads = [pltpu.make_async_copy(src, dst, sem.at[n])
                     for n, (src, dst) in enumerate(((wpa_hbm, wpa), (wpb_hbm, wpb), (wout_hbm, wout)))]
            for cp in loads:
                cp.start()
            t_idx = lax.broadcasted_iota(jnp.int32, (SGU_CHUNK, SGU_CHUNK), 0)
            s_idx = lax.broadcasted_iota(jnp.int32, (SGU_CHUNK, SGU_CHUNK), 1)
            for g in range(N_GROUPS):
                wmix[g] = jnp.where(s_idx <= t_idx, ws_ref[g], 0.0).astype(BF16)
            for ref in (acc_out, acc_pa, acc_pb, dbg_ref, dfg_ref, dlng_ref, dlnb_ref, dws_ref, dbs_ref, loss_ref):
                ref[...] = jnp.zeros(ref.shape, F32)
            for cp in loads:
                cp.wait()

        def tile_fwd_bwd(rows):
            g_a = ga_ref[rows, :].astype(F32)
            u_b = ub_ref[rows, :].astype(F32)
            v_b = vb_ref[rows, :].astype(F32)
            g_b = gb_ref[rows, :].astype(F32)
            bg = bg_ref[...]
            sg_a = _sigmoid(g_a)
            silu_a = g_a * sg_a
            o_a = oa_ref[rows, :]
            y_a = (o_a * silu_a).astype(BF16)
            ug, dgelu_u = _gelu_and_grad(u_b)
            vg, dgelu_v = _gelu_and_grad(v_b)
            mu = jnp.mean(vg, axis=-1, keepdims=True)
            vc = vg - mu
            rstd = lax.rsqrt(jnp.mean(vc * vc, axis=-1, keepdims=True) + EPS)
            vhat = vc * rstd
            lng = lng_ref[...]
            vn = (vhat * lng + lnb_ref[...]).astype(BF16)
            sg_b = _sigmoid(g_b)
            silu_b = g_b * sg_b
            subs = [slice(n * SGU_CHUNK, (n + 1) * SGU_CHUNK) for n in range(tm // SGU_CHUNK)]
            mixed = jnp.concatenate([jnp.concatenate(
                [_dot(wmix[g], vn[sub, g * 128:(g + 1) * 128]) + bs_ref[g] for g in range(N_GROUPS)], axis=1)
                for sub in subs], axis=0)
            um = ug * mixed
            y_b = (um * silu_b).astype(BF16)
            gate_a = _sigmoid(jnp.concatenate([ta0_ref[rows, :], ta1_ref[rows, :]], axis=1).astype(F32)
                              + bg[:, :D_MODEL])
            gate_b = _sigmoid(jnp.concatenate([tb0_ref[rows, :], tb1_ref[rows, :]], axis=1).astype(F32)
                              + bg[:, D_MODEL:])
            p_a = _dot(y_a, wpa[...])
            p_b = _dot(y_b, wpb[...])
            merged = (gate_a * p_a + gate_b * p_b).astype(BF16)
            x2 = x_ref[rows, :] + _dot(merged, wout[...])
            r2 = lax.rsqrt(jnp.mean(x2 * x2, axis=-1, keepdims=True) + EPS)
            xh = x2 * r2
            fg = fg_ref[...]
            err = xh * fg - t_ref[rows, :]
            loss_ref[...] += jnp.sum(jnp.sum(err * err, axis=-1, keepdims=True), axis=0, keepdims=True) * (0.5 / D_MODEL)
            dy = err * (1.0 / D_MODEL)
            dfg_ref[...] += jnp.sum(dy * xh, axis=0, keepdims=True)
            gy = dy * fg
            dx2 = r2 * (gy - xh * jnp.mean(gy * xh, axis=-1, keepdims=True))
            dx2_ref[rows, :] = dx2
            dx2b = dx2.astype(BF16)
            dmerged = _dot(dx2b, wout[...], NT)
            acc_out[...] += _dot(merged, dx2b, TN)
            dp_a = dmerged * gate_a
            dp_b = dmerged * gate_b
            dgate_a = dp_a * p_a * (1.0 - gate_a)
            dgate_b = dp_b * p_b * (1.0 - gate_b)
            dbg_ref[:, :D_MODEL] += jnp.sum(dgate_a, axis=0, keepdims=True)
            dbg_ref[:, D_MODEL:] += jnp.sum(dgate_b, axis=0, keepdims=True)
            dz_ref[rows, 2048:3072] = dgate_a.astype(BF16)
            dz_ref[rows, 3072:4096] = dgate_b.astype(BF16)
            dp_ab = dp_a.astype(BF16)
            dp_bb = dp_b.astype(BF16)
            dy_a = _dot(dp_ab, wpa[...], NT)
            dy_b = _dot(dp_bb, wpb[...], NT)
            acc_pa[...] += _dot(y_a, dp_ab, TN)
            acc_pb[...] += _dot(y_b, dp_bb, TN)
            doa_ref[rows, :] = (dy_a * silu_a).astype(BF16)
            dz_ref[rows, 0:512] = (dy_a * o_a * (sg_a * (1.0 + g_a * (1.0 - sg_a)))).astype(BF16)
            dz_ref[rows, 1536:2048] = (dy_b * um * (sg_b * (1.0 + g_b * (1.0 - sg_b)))).astype(BF16)
            dys = dy_b * silu_b
            dz_ref[rows, 512:1024] = (dys * mixed * dgelu_u).astype(BF16)
            dmixed = dys * ug
            dmb = dmixed.astype(BF16)
            dvn_rows = []
            for sub in subs:
                dvn_parts = []
                for g in range(N_GROUPS):
                    cols = slice(g * 128, (g + 1) * 128)
                    dws_ref[g] += _dot(dmb[sub, cols], vn[sub, cols], NT)
                    dbs_ref[g] += jnp.sum(dmixed[sub, cols], axis=-1, keepdims=True)
                    dvn_parts.append(_dot(wmix[g], dmb[sub, cols], TN))
                dvn_rows.append(jnp.concatenate(dvn_parts, axis=1))
            dvn = jnp.concatenate(dvn_rows, axis=0)
            dlng_ref[...] += jnp.sum(dvn * vhat, axis=0, keepdims=True)
            dlnb_ref[...] += jnp.sum(dvn, axis=0, keepdims=True)
            dvh = dvn * lng
            dvg = rstd * (dvh - jnp.mean(dvh, axis=-1, keepdims=True)
                          - vhat * jnp.mean(dvh * vhat, axis=-1, keepdims=True))
            dz_ref[rows, 1024:1536] = (dvg * dgelu_v).astype(BF16)

        tile_fwd_bwd(slice(0, tm))

        @pl.when(i == nt - 1)
        def _():
            t_idx = lax.broadcasted_iota(jnp.int32, (SGU_CHUNK, SGU_CHUNK), 0)
            s_idx = lax.broadcasted_iota(jnp.int32, (SGU_CHUNK, SGU_CHUNK), 1)
            for g in range(N_GROUPS):
                dws_ref[g] = jnp.where(s_idx <= t_idx, dws_ref[g], 0.0)
            stores = [pltpu.make_async_copy(src, dst, sem.at[n])
                      for n, (src, dst) in enumerate(((acc_out, dwout_hbm), (acc_pa, dwpa_hbm), (acc_pb, dwpb_hbm)))]
            for cp in stores:
                cp.start()
            for cp in stores:
                cp.wait()

    tile = lambda w: pl.BlockSpec((tm, w), lambda i: (i, 0))
    whole = lambda shape: pl.BlockSpec(shape, lambda i: (0,) * len(shape))
    hbm = pl.BlockSpec(memory_space=pl.ANY)
    return pl.pallas_call(
        body, name="mid_fwd_bwd",
        grid=(nt,),
        in_specs=[tile(D_MODEL), tile(D_MODEL), tile(D_A)]
        + [pl.BlockSpec((tm, COL_BLOCK), functools.partial(lambda c, i: (i + Z_PAD // tm, c), c))
           for c in range(3, N_COL_BLOCKS)]
        + [hbm, hbm, hbm,
                  whole((1, 2 * D_MODEL)), whole((1, D_B)), whole((1, D_B)),
                  whole((N_GROUPS, SGU_CHUNK, SGU_CHUNK)), whole((N_GROUPS, SGU_CHUNK, 1)), whole((1, D_MODEL))],
        out_specs=(tile(D_MODEL), tile(D_A), tile(REST), hbm, hbm, hbm,
                   whole((1, 2 * D_MODEL)), whole((1, D_MODEL)), whole((1, D_B)), whole((1, D_B)),
                   whole((N_GROUPS, SGU_CHUNK, SGU_CHUNK)), whole((N_GROUPS, SGU_CHUNK, 1)), whole((1, 1))),
        out_shape=(jax.ShapeDtypeStruct((s, D_MODEL), F32), jax.ShapeDtypeStruct((s, D_A), BF16),
                   jax.ShapeDtypeStruct((s, REST), BF16),
                   jax.ShapeDtypeStruct((D_MODEL, D_MODEL), F32), jax.ShapeDtypeStruct((D_A, D_MODEL), F32),
                   jax.ShapeDtypeStruct((D_B, D_MODEL), F32),
                   jax.ShapeDtypeStruct((1, 2 * D_MODEL), F32), jax.ShapeDtypeStruct((1, D_MODEL), F32),
                   jax.ShapeDtypeStruct((1, D_B), F32), jax.ShapeDtypeStruct((1, D_B), F32),
                   jax.ShapeDtypeStruct((N_GROUPS, SGU_CHUNK, SGU_CHUNK), F32),
                   jax.ShapeDtypeStruct((N_GROUPS, SGU_CHUNK, 1), F32), jax.ShapeDtypeStruct((1, 1), F32)),
        scratch_shapes=[pltpu.VMEM((D_A, D_MODEL), BF16), pltpu.VMEM((D_B, D_MODEL), BF16),
                        pltpu.VMEM((D_MODEL, D_MODEL), BF16), pltpu.VMEM((N_GROUPS, SGU_CHUNK, SGU_CHUNK), BF16),
                        pltpu.VMEM((D_MODEL, D_MODEL), F32), pltpu.VMEM((D_A, D_MODEL), F32),
                        pltpu.VMEM((D_B, D_MODEL), F32),
                        pltpu.SemaphoreType.DMA((3,))],
        compiler_params=_params(56),
    )(x, target, attn_out, *([z] * (N_COL_BLOCKS - 3)), w_pa, w_pb, w_out, b_gate, ln_g, ln_b, w_s, b_s, final_g)


def _proj_bwd_x(dqkv, drest, x, dx2, norm_g, w_in_t, to_chip, small):
    s = x.shape[0]
    tm = 512 if s % 512 == 0 else TOKEN_TILE
    nt = s // tm
    ws_shape = (N_GROUPS * SGU_CHUNK, SGU_CHUNK)
    rows = to_chip.shape[1]
    half = D_MODEL // 2
    left, right = slice(0, half), slice(half, D_MODEL)

    def body(dqkv_ref, dr_ref, x_ref, dx2_ref, g_ref, w_hbm, tc_ref,
             ng_ref, bg_ref, rel_ref, lng_ref, lnb_ref, fg_ref, loss_ref, bs_ref, ws_ref,
             dx_ref, fc_ref, slab_land, ws_land,
             w, slab_stage, ws_stage, via_x, via_y, mine, out_x, out_y, sem, send_sems, recv_sems,
             gather_send, gather_recv, keep_sems):
        i = pl.program_id(0)
        x_, y_, c_ = _my_pos()
        me = _flat_id((x_, y_, c_))
        xn, yn = (1 - x_, y_, c_), (x_, 1 - y_, c_)
        gather = _SlotGather([slab_land, ws_land], gather_send, gather_recv, own=[slab_stage, ws_stage])
        keep = [pltpu.make_async_copy(stage, land.at[me], keep_sems.at[k]) for k, (stage, land) in enumerate(
            ((slab_stage, slab_land), (ws_stage, ws_land)))]

        def copy(k, src, dst, to):
            return pltpu.make_async_remote_copy(src_ref=src, dst_ref=dst, send_sem=send_sems.at[k],
                                                recv_sem=recv_sems.at[k], device_id=to, device_id_type=MESH)

        first = [copy(0, tc_ref.at[0, :, left], fc_ref.at[0, :, left], xn), copy(1, tc_ref.at[2, :, left], via_x, xn),
                 copy(2, tc_ref.at[1, :, right], fc_ref.at[1, :, right], yn), copy(3, tc_ref.at[2, :, right], via_y, yn)]
        second = [copy(4, out_y, fc_ref.at[1, :, left], yn), copy(5, out_x, fc_ref.at[0, :, right], xn)]

        def add_and_send(arrival, landed, own_half, stage, onward):
            load = pltpu.make_async_copy(own_half, mine, sem)
            load.start()
            arrival.wait_recv()
            load.wait()
            stage[...] = (mine[...].astype(F32) + landed[...].astype(F32)).astype(BF16)
            onward.start()

        @pl.when(i == 0)
        def _():
            cp = pltpu.make_async_copy(w_hbm, w, sem)
            cp.start()
            slab_stage[...] = jnp.zeros(slab_stage.shape, F32)
            slab_stage[ROW_NORM_G:ROW_NORM_G + 1, :] = ng_ref[...]
            slab_stage[ROW_B_GATE:ROW_B_GATE + 1, :] = bg_ref[:, :D_MODEL]
            slab_stage[ROW_B_GATE + 1:ROW_B_GATE + 2, :] = bg_ref[:, D_MODEL:]
            slab_stage[ROW_LN_G:ROW_LN_G + 1, :D_B] = lng_ref[...]
            slab_stage[ROW_LN_B:ROW_LN_B + 1, :D_B] = lnb_ref[...]
            slab_stage[ROW_FINAL_G:ROW_FINAL_G + 1, :] = fg_ref[...]
            slab_stage[ROW_LOSS:ROW_LOSS + 1, :1] = loss_ref[...]
            slab_stage[ROW_REL:ROW_REL + N_HEADS, :N_REL_PAD] = rel_ref[...]
            eye = (lax.broadcasted_iota(jnp.int32, (SGU_CHUNK, SGU_CHUNK), 0)
                   == lax.broadcasted_iota(jnp.int32, (SGU_CHUNK, SGU_CHUNK), 1))
            for g in range(N_GROUPS):
                row = jnp.sum(jnp.where(eye, bs_ref[g], 0.0), axis=0, keepdims=True)
                slab_stage[ROW_B_S + g:ROW_B_S + g + 1, :SGU_CHUNK] = row
            ws_stage[...] = ws_ref[...]
            for cp_keep in keep:
                cp_keep.start()
            gather.start()
            for rc in first:
                rc.start()
            cp.wait()

        @pl.when(i == nt // 2)
        def _():
            gather.pass_on()
            add_and_send(first[1], via_x, tc_ref.at[1, :, left], out_y, second[0])
            add_and_send(first[3], via_y, tc_ref.at[0, :, right], out_x, second[1])

        dh = None
        for c in range(N_COL_BLOCKS):
            dz = dqkv_ref[c] if c < 3 else dr_ref[:, (c - 3) * COL_BLOCK:(c - 2) * COL_BLOCK]
            part = _dot(dz, w[c * COL_BLOCK:(c + 1) * COL_BLOCK, :])
            dh = part if dh is None else dh + part
        xf = x_ref[...]
        r = lax.rsqrt(jnp.mean(xf * xf, axis=-1, keepdims=True) + EPS)
        xn = xf * r
        gh = dh * g_ref[...]
        dx_ref[...] = r * (gh - xn * jnp.mean(gh * xn, axis=-1, keepdims=True)) + dx2_ref[...]

        @pl.when(i == nt - 1)
        def _():
            gather.finish()
            for cp_keep in keep:
                cp_keep.wait()
            for k in (0, 2, 4, 5):
                (first + second)[k].wait_recv()
            for rc in first + second:
                rc.wait_send()

    hbm = pl.BlockSpec(memory_space=pl.ANY)
    whole = lambda a: pl.BlockSpec(a.shape, lambda i: (0,) * a.ndim)
    lands = ((N_DEV, SLAB_ROWS, D_MODEL), (N_DEV,) + ws_shape)
    return pl.pallas_call(
        body, name="proj_bwd_x",
        grid=(nt,),
        in_specs=[pl.BlockSpec((3, tm, D_A), lambda i: (0, i, 0)),
                  pl.BlockSpec((tm, REST), lambda i: (i, 0)),
                  pl.BlockSpec((tm, D_MODEL), lambda i: (i, 0)),
                  pl.BlockSpec((tm, D_MODEL), lambda i: (i, 0)),
                  pl.BlockSpec((1, D_MODEL), lambda i: (0, 0)),
                  hbm, hbm] + [whole(a) for a in small],
        out_specs=(pl.BlockSpec((tm, D_MODEL), lambda i: (i, 0)), hbm, hbm, hbm),
        out_shape=(jax.ShapeDtypeStruct((s, D_MODEL), F32), jax.ShapeDtypeStruct((2, rows, D_MODEL), BF16))
        + tuple(jax.ShapeDtypeStruct(shape, F32) for shape in lands),
        scratch_shapes=[pltpu.VMEM((D_IN, D_MODEL), BF16)]
        + [pltpu.VMEM(shape[1:], F32) for shape in lands]
        + [pltpu.VMEM((rows, half), BF16)] * 5
        + [pltpu.SemaphoreType.DMA, pltpu.SemaphoreType.DMA((6,)), pltpu.SemaphoreType.DMA((6,)),
           pltpu.SemaphoreType.DMA((2, N_DEV - 1)), pltpu.SemaphoreType.DMA((2, N_DEV - 1)),
           pltpu.SemaphoreType.DMA((2,))],
        compiler_params=_params(56),
    )(dqkv, drest, x, dx2, norm_g, w_in_t, to_chip, *small)


def _proj_bwd_w(xn, dqkv, drest, norm_g, w_in_t):
    s = xn.shape[0]
    tk = min(s, 1024)
    nk = s // tk

    def body(xn_ref, dqkv_ref, dr_ref, g_ref, w_ref, o_ref, dg_ref, acc):
        j = pl.program_id(0)
        i = pl.program_id(1)

        @pl.when((j == 0) & (i == 0))
        def _():
            dg_ref[...] = jnp.zeros(dg_ref.shape, F32)

        @pl.when(i == 0)
        def _():
            acc[...] = jnp.zeros(acc.shape, F32)

        @pl.when(j < 3)
        def _():
            acc[...] += _dot(dqkv_ref[...], xn_ref[...], TN)

        @pl.when(j >= 3)
        def _():
            acc[...] += _dot(dr_ref[...], xn_ref[...], TN)

        @pl.when(i == nk - 1)
        def _():
            m = acc[...]
            o_ref[...] = (m * g_ref[...]).astype(BF16)
            dg_ref[...] += jnp.sum(m * w_ref[...].astype(F32), axis=0, keepdims=True)

    return pl.pallas_call(
        body, name="proj_bwd_w",
        grid=(N_COL_BLOCKS, nk),
        in_specs=[pl.BlockSpec((tk, D_MODEL), lambda j, i: (i, 0)),
                  pl.BlockSpec((None, tk, COL_BLOCK),
                               lambda j, i: (jnp.minimum(j, 2), jnp.where(j < 3, i, nk - 1), 0)),
                  pl.BlockSpec((tk, COL_BLOCK),
                               lambda j, i: (jnp.where(j >= 3, i, 0), jnp.maximum(j - 3, 0))),
                  pl.BlockSpec((1, D_MODEL), lambda j, i: (0, 0)),
                  pl.BlockSpec((COL_BLOCK, D_MODEL), lambda j, i: (j, 0))],
        out_specs=(pl.BlockSpec((COL_BLOCK, D_MODEL), lambda j, i: (j, 0)),
                   pl.BlockSpec((1, D_MODEL), lambda j, i: (0, 0))),
        out_shape=(jax.ShapeDtypeStruct((D_IN, D_MODEL), BF16), jax.ShapeDtypeStruct((1, D_MODEL), F32)),
        scratch_shapes=[pltpu.VMEM((COL_BLOCK, D_MODEL), F32)],
        compiler_params=_params(40),
    )(xn, dqkv, drest, norm_g, w_in_t)


def _adamw_math(w, g, m, v):
    c1 = 1.0 - ADAM_B1 ** ADAM_STEP
    c2 = 1.0 - ADAM_B2 ** ADAM_STEP
    nm = ADAM_B1 * m + (1.0 - ADAM_B1) * g
    nv = ADAM_B2 * v + (1.0 - ADAM_B2) * (g * g)
    return -ADAM_LR * ((nm / c1) / (jnp.sqrt(nv / c2) + ADAM_EPS) + ADAM_WD * w), nm, nv


def _adamw(name, w, g, m, v, from_chip):
    rows, cols = w.shape
    tr = rows if rows * cols <= 512 * 1024 else next(t for t in range(256, 7, -8) if rows % t == 0)

    def body(w_ref, g_ref, m_ref, v_ref, t_ref, g_out, d_ref, nm_ref, nv_ref):
        gg = g_ref[...]
        for j in range(from_chip.shape[0]):
            gg = gg + t_ref[j].astype(F32)
        g_out[...] = gg
        d_ref[...], nm_ref[...], nv_ref[...] = _adamw_math(w_ref[...], gg, m_ref[...], v_ref[...])

    spec = pl.BlockSpec((tr, cols), lambda i: (i, 0))
    shape = jax.ShapeDtypeStruct((rows, cols), F32)
    return pl.pallas_call(
        body, name=name,
        grid=(rows // tr,),
        in_specs=[spec] * 4 + [pl.BlockSpec((from_chip.shape[0], tr, cols), lambda i: (0, i, 0))],
        out_specs=(spec,) * 4, out_shape=(shape,) * 4,
        compiler_params=_params(32),
    )(w, g, m, v, from_chip)


_SMALL = (("norm_g", (1, D_MODEL)), ("b_gate", (1, 2 * D_MODEL)), ("rel_bias", (N_HEADS, N_REL)),
          ("sgu_ln_g", (1, D_B)), ("sgu_ln_b", (1, D_B)), ("w_s", (N_GROUPS * SGU_CHUNK, SGU_CHUNK)),
          ("b_s", (N_GROUPS, SGU_CHUNK)), ("final_g", (1, D_MODEL)))


def _adamw_small(slabs, ws_all, weights, moments_m, moments_v):
    k = len(_SMALL)

    def total(ref):
        acc = ref[0]
        for d in range(1, N_DEV):
            acc = acc + ref[d]
        return acc

    def body(*refs):
        slab_ref, ws_ref = refs[:2]
        w_refs, m_refs, v_refs = refs[2:2 + k], refs[2 + k:2 + 2 * k], refs[2 + 2 * k:2 + 3 * k]
        outs = refs[2 + 3 * k:]
        slab = total(slab_ref)
        grads = {
            "norm_g": slab[ROW_NORM_G:ROW_NORM_G + 1, :],
            "b_gate": jnp.concatenate([slab[ROW_B_GATE:ROW_B_GATE + 1, :], slab[ROW_B_GATE + 1:ROW_B_GATE + 2, :]], axis=1),
            "rel_bias": slab[ROW_REL:ROW_REL + N_HEADS, :N_REL],
            "sgu_ln_g": slab[ROW_LN_G:ROW_LN_G + 1, :D_B],
            "sgu_ln_b": slab[ROW_LN_B:ROW_LN_B + 1, :D_B],
            "w_s": total(ws_ref),
            "b_s": slab[ROW_B_S:ROW_B_S + N_GROUPS, :SGU_CHUNK],
            "final_g": slab[ROW_FINAL_G:ROW_FINAL_G + 1, :],
        }
        for n, (name, _) in enumerate(_SMALL):
            g = grads[name]
            outs[n][...] = g
            outs[k + n][...], outs[2 * k + n][...], outs[3 * k + n][...] = _adamw_math(
                w_refs[n][...], g, m_refs[n][...], v_refs[n][...])
        outs[4 * k][...] = slab[ROW_LOSS:ROW_LOSS + 1, :1]

    vmem = pl.BlockSpec(memory_space=pltpu.VMEM)
    shapes = tuple(jax.ShapeDtypeStruct(shape, F32) for _, shape in _SMALL)
    return pl.pallas_call(
        body, name="adamw_small",
        out_shape=shapes * 4 + (jax.ShapeDtypeStruct((1, 1), F32),),
        in_specs=[vmem] * (2 + 3 * k), out_specs=tuple([vmem] * (4 * k + 1)),
        compiler_params=_params(16),
    )(slabs, ws_all, *weights, *moments_m, *moments_v)


def _pad_rel(a):
    return jnp.pad(a.reshape(N_HEADS, N_REL), ((0, 0), (0, N_REL_PAD - N_REL)))


def kernel(x, norm_g, w_in, b_gate, rel_bias, sgu_ln_g, sgu_ln_b, w_s, b_s, w_pa, w_pb, w_out, final_g, loss_target, m_norm_g, m_w_in, m_b_gate, m_rel_bias, m_sgu_ln_g, m_sgu_ln_b, m_w_s, m_b_s, m_w_pa, m_w_pb, m_w_out, m_final_g, v_norm_g, v_w_in, v_b_gate, v_rel_bias, v_sgu_ln_g, v_sgu_ln_b, v_w_s, v_b_s, v_w_pa, v_w_pb, v_w_out, v_final_g):
    s = x.shape[1]
    xs = x.reshape(s, D_MODEL)
    tgt = loss_target.reshape(s, D_MODEL)

    bias_table = _bias_table(_pad_rel(rel_bias))
    w_in_t = jnp.swapaxes(w_in[0], 0, 1)
    qkv, x_norm, w_in_t_full = _gather_proj_fwd(xs, norm_g, w_in_t)
    attn_out, g_pa, g_pb, g_out = _attn_fwd(qkv, bias_table, (w_pa[0], w_pb[0], w_out[0]))
    w_pa_full = jnp.transpose(g_pa, (1, 0, 2)).reshape(D_A, D_MODEL)
    w_pb_full = jnp.transpose(g_pb, (1, 0, 2)).reshape(D_B, D_MODEL)
    w_out_full = g_out.reshape(D_MODEL, D_MODEL)

    (dx2, d_attn, drest, dw_out, dw_pa, dw_pb, d_bgate, d_fg, d_lng, d_lnb, d_ws, d_bs, loss_part) = _mid_fwd_bwd(
        xs, tgt, attn_out, qkv, w_pa_full, w_pb_full, w_out_full, b_gate, sgu_ln_g, sgu_ln_b, w_s[0],
        b_s.reshape(N_GROUPS, SGU_CHUNK, 1), final_g.reshape(1, D_MODEL))

    own_pa, own_pb, own_out, tc_pa, tc_pb, tc_out = _reduce_chip(
        "reduce_chip_proj", (dw_pa, dw_pb, dw_out), (1, 1, 0))
    dqkv, dbias, fc_pa, fc_pb, fc_out = _attn_bwd(qkv, bias_table, d_attn, (tc_pa, tc_pb, tc_out))
    d_rel = _bias_grad(dbias)
    dw_in_t, d_ng = _proj_bwd_w(x_norm, dqkv, drest, norm_g, w_in_t_full)
    own_in, tc_in = _reduce_chip("reduce_chip_in", (dw_in_t,), (0,))
    grad_x, fc_in, slabs, ws_all = _proj_bwd_x(
        dqkv, drest, xs, dx2, norm_g, w_in_t_full, tc_in,
        (d_ng, d_bgate, d_rel, d_lng, d_lnb, d_fg, loss_part, d_bs, d_ws.reshape(N_GROUPS * SGU_CHUNK, SGU_CHUNK)))
    big = {"w_in": tuple(jnp.swapaxes(t, 0, 1)[None] for t in _adamw(
        "adamw_w_in", w_in_t, own_in, jnp.swapaxes(m_w_in[0], 0, 1), jnp.swapaxes(v_w_in[0], 0, 1), fc_in))}
    for name, w, g, fc, m, v in (("w_pa", w_pa, own_pa, fc_pa, m_w_pa, v_w_pa),
                                 ("w_pb", w_pb, own_pb, fc_pb, m_w_pb, v_w_pb),
                                 ("w_out", w_out, own_out, fc_out, m_w_out, v_w_out)):
        big[name] = tuple(t[None] for t in _adamw("adamw_" + name, w[0], g, m[0], v[0], fc))

    as_2d = lambda leaves: [a.reshape(shape) for a, (_, shape) in zip(leaves, _SMALL)]
    small_out = _adamw_small(
        slabs, ws_all, as_2d((norm_g, b_gate, rel_bias, sgu_ln_g, sgu_ln_b, w_s, b_s, final_g)),
        as_2d((m_norm_g, m_b_gate, m_rel_bias, m_sgu_ln_g, m_sgu_ln_b, m_w_s, m_b_s, m_final_g)),
        as_2d((v_norm_g, v_b_gate, v_rel_bias, v_sgu_ln_g, v_sgu_ln_b, v_w_s, v_b_s, v_final_g)))
    small_index = {name: n for n, (name, _) in enumerate(_SMALL)}

    def leaf(kind, name, like):
        if name in big:
            return big[name][kind]
        return small_out[kind * len(_SMALL) + small_index[name]].reshape(like.shape)

    weights = (("norm_g", norm_g), ("w_in", w_in), ("b_gate", b_gate), ("rel_bias", rel_bias), ("sgu_ln_g", sgu_ln_g),
               ("sgu_ln_b", sgu_ln_b), ("w_s", w_s), ("b_s", b_s), ("w_pa", w_pa), ("w_pb", w_pb), ("w_out", w_out),
               ("final_g", final_g))
    outs = [small_out[-1].reshape(()), grad_x.reshape(x.shape)]
    for kind in range(4):
        outs.extend(leaf(kind, name, like) for name, like in weights)
    return tuple(outs)
```

```python
import functools
import math

import jax
import jax.numpy as jnp
from jax import lax
from jax.experimental import pallas as pl
from jax.experimental.pallas import tpu as pltpu

F32 = jnp.float32
BF16 = jnp.bfloat16
MESH = pl.DeviceIdType.MESH
N_DEV = 8

D_MODEL = 1024
D_A = 512
D_B = 512
D_IN = 5632
N_HEADS = 8
HEAD_DIM = 64
N_PREV = 8
REL_CLIP = 128
N_REL = 2 * REL_CLIP + 1
N_REL_PAD = 384
SGU_CHUNK = 128
N_GROUPS = 4
EPS = 1e-6
NEG_INF = -1e30
Q_SCALE = HEAD_DIM ** -0.5

Q_BLOCK = 256
K_SPAN = 768
Z_PAD = K_SPAN - Q_BLOCK
ROLL_W = 1024
COL_BLOCK = 512
N_COL_BLOCKS = D_IN // COL_BLOCK
REST = D_IN - 3 * D_A
TOKEN_TILE = 256

ADAM_LR = 0.001
ADAM_B1 = 0.9
ADAM_B2 = 0.999
ADAM_EPS = 1e-08
ADAM_WD = 0.01
ADAM_STEP = 10

GELU_C = math.sqrt(2.0 / math.pi)
GELU_A = 0.044715

NT = (((1,), (1,)), ((), ()))
TN = (((0,), (0,)), ((), ()))
HIGHEST = lax.Precision.HIGHEST


def _params(vmem_mb, **kw):
    return pltpu.CompilerParams(vmem_limit_bytes=vmem_mb * 1024 * 1024, **kw)


def _dot(a, b, dims=None):
    if dims is None:
        return jnp.dot(a, b, preferred_element_type=F32)
    return lax.dot_general(a, b, dims, preferred_element_type=F32)


def _sigmoid(x):
    return 0.5 * jnp.tanh(0.5 * x) + 0.5


def _gelu_and_grad(u):
    u2 = u * u
    t = jnp.tanh(GELU_C * (u + GELU_A * u * u2))
    half = 0.5 * (1.0 + t)
    g = u * half
    dg = half + 0.5 * u * (1.0 - t * t) * (GELU_C * (1.0 + 3.0 * GELU_A * u2))
    return g, dg


def _my_pos():
    return lax.axis_index("x"), lax.axis_index("y"), lax.axis_index("c")


def _flat_id(pos):
    return 4 * pos[0] + 2 * pos[1] + pos[2]


def _other_chips(pos):
    x, y, _ = pos
    return ((1 - x, y), (x, 1 - y), (1 - x, 1 - y))


class _SlotGather:
    def __init__(self, bufs, send_sems, recv_sems, own=None):
        self.bufs, self.send_sems, self.recv_sems = bufs, send_sems, recv_sems
        self.own = own if own is not None else [None] * len(bufs)
        x, y, c = _my_pos()
        self.c, self.me, self.sib = c, (x, y, c), (x, y, 1 - c)
        self.chips = _other_chips(self.me)

    def _copy(self, a, k, block, to):
        slot = _flat_id(block)
        src = self.own[a] if (k < 4 and self.own[a] is not None) else self.bufs[a].at[slot]
        return pltpu.make_async_remote_copy(
            src_ref=src, dst_ref=self.bufs[a].at[slot],
            send_sem=self.send_sems.at[a, k], recv_sem=self.recv_sems.at[a, k], device_id=to, device_id_type=MESH)

    def _own_sends(self):
        n = len(self.bufs)
        return ([self._copy(a, 1 + j, self.me, (*chip, self.c)) for j, chip in enumerate(self.chips) for a in range(n)]
                + [self._copy(a, 0, self.me, self.sib) for a in range(n)])

    def _passes(self):
        return [self._copy(a, 4 + j, (*chip, self.c), self.sib)
                for j, chip in enumerate(self.chips) for a in range(len(self.bufs))]

    def start(self):
        for cp in self._own_sends():
            cp.start()

    def pass_on(self):
        for j, chip in enumerate(self.chips):
            for a in range(len(self.bufs)):
                self._copy(a, 1 + j, (*chip, self.c), self.me).wait_recv()
                self._copy(a, 4 + j, (*chip, self.c), self.sib).start()

    def finish(self):
        for a in range(len(self.bufs)):
            self._copy(a, 0, self.sib, self.me).wait_recv()
            for j, chip in enumerate(self.chips):
                self._copy(a, 4 + j, (*chip, 1 - self.c), self.me).wait_recv()
        for cp in self._own_sends() + self._passes():
            cp.wait_send()


def _reduce_chip(name, parts, sharded_dim):
    n = len(parts)
    shapes = []
    for p, dim in zip(parts, sharded_dim):
        shape = list(p.shape)
        shape[dim] //= N_DEV
        shapes.append(tuple(shape))

    def body(*refs):
        full, own, to_chip = refs[:n], refs[n:2 * n], refs[2 * n:3 * n]
        ins, from_sib = refs[3 * n:4 * n], refs[4 * n:5 * n]
        send_sems, recv_sems = refs[5 * n], refs[5 * n + 1]
        x, y, c = _my_pos()
        sib = (x, y, 1 - c)
        chips = ((x, y),) + _other_chips((x, y, c))
        for a in range(n):
            rows, cols = shapes[a]
            for d in range(N_DEV):
                if sharded_dim[a] == 0:
                    ins[a][d] = full[a][d * rows:(d + 1) * rows, :].astype(BF16)
                else:
                    ins[a][d] = full[a][:, d * cols:(d + 1) * cols].astype(BF16)

        def to_sibling(a, r):
            return pltpu.make_async_remote_copy(
                src_ref=ins[a].at[_flat_id((*chips[r], 1 - c))], dst_ref=from_sib[a].at[r],
                send_sem=send_sems.at[a, r], recv_sem=recv_sems.at[a, r], device_id=sib, device_id_type=MESH)

        sends = [to_sibling(a, r) for r in (1, 2, 3, 0) for a in range(n)]
        for cp in sends:
            cp.start()
        for r in (1, 2, 3, 0):
            for a in range(n):
                to_sibling(a, r).wait_recv()
                both = ins[a][_flat_id((*chips[r], c))].astype(F32) + from_sib[a][r].astype(F32)
                if r == 0:
                    own[a][...] = both
                else:
                    to_chip[a][r - 1] = both.astype(BF16)
        for cp in sends:
            cp.wait_send()

    vmem = pl.BlockSpec(memory_space=pltpu.VMEM)
    return pl.pallas_call(
        body, name=name,
        out_shape=tuple(jax.ShapeDtypeStruct(sh, F32) for sh in shapes)
        + tuple(jax.ShapeDtypeStruct((3,) + sh, BF16) for sh in shapes),
        in_specs=[vmem] * n, out_specs=tuple([vmem] * (2 * n)),
        scratch_shapes=[pltpu.VMEM((N_DEV,) + sh, BF16) for sh in shapes]
        + [pltpu.VMEM((4,) + sh, BF16) for sh in shapes]
        + [pltpu.SemaphoreType.DMA((n, 4)), pltpu.SemaphoreType.DMA((n, 4))],
        compiler_params=_params(56),
    )(*parts)


def _owner_copies(to_chip, from_chip, send_sems, recv_sems):
    x, y, c = _my_pos()
    return [pltpu.make_async_remote_copy(
        src_ref=to_chip[a].at[j], dst_ref=from_chip[a].at[j],
        send_sem=send_sems.at[a, j], recv_sem=recv_sems.at[a, j], device_id=(*chip, c), device_id_type=MESH)
        for a in range(len(to_chip)) for j, chip in enumerate(_other_chips((x, y, c)))]


ROW_NORM_G, ROW_B_GATE, ROW_LN_G, ROW_LN_B, ROW_FINAL_G, ROW_LOSS, ROW_REL, ROW_B_S, SLAB_ROWS = 0, 1, 3, 4, 5, 6, 8, 16, 24


def _rel_index(e):
    lo, hi = Z_PAD - REL_CLIP, Z_PAD + REL_CLIP
    return jnp.where(e <= lo, 2 * REL_CLIP, jnp.where(e < hi, hi - e, jnp.where(e <= K_SPAN, 0, 2 * REL_CLIP)))


def _bias_table(rel_bias_pad):
    def body(rb_ref, bt_ref):
        c = lax.broadcasted_iota(jnp.int32, (N_REL_PAD, ROLL_W), 1)
        r = lax.broadcasted_iota(jnp.int32, (N_REL_PAD, ROLL_W), 0)
        pick = (r == _rel_index(c)).astype(F32)
        rows = jnp.dot(rb_ref[...], pick, precision=HIGHEST, preferred_element_type=F32)
        qc = lax.broadcasted_iota(jnp.int32, (Q_BLOCK, K_SPAN), 0) >> 6
        kc = lax.broadcasted_iota(jnp.int32, (Q_BLOCK, K_SPAN), 1) >> 6
        band = (kc >= qc) & (kc <= qc + N_PREV)
        for h in range(N_HEADS):
            t = jnp.broadcast_to(rows[h:h + 1, :], (Q_BLOCK, ROLL_W))
            t = pltpu.roll(t, 0, 1, stride=1, stride_axis=0)
            bt_ref[h] = jnp.where(band, t[:, :K_SPAN], NEG_INF)

    return pl.pallas_call(
        body, name="bias_table",
        out_shape=jax.ShapeDtypeStruct((N_HEADS, Q_BLOCK, K_SPAN), F32),
        compiler_params=_params(32),
    )(rel_bias_pad)


def _bias_grad(dbias):
    def body(a_ref, o_ref):
        rr = lax.broadcasted_iota(jnp.int32, (Q_BLOCK, Q_BLOCK), 0)
        cc = lax.broadcasted_iota(jnp.int32, (Q_BLOCK, Q_BLOCK), 1)
        flip = (rr + cc == Q_BLOCK - 1).astype(F32)
        c = lax.broadcasted_iota(jnp.int32, (ROLL_W, N_REL_PAD), 0)
        r = lax.broadcasted_iota(jnp.int32, (ROLL_W, N_REL_PAD), 1)
        e = jnp.where(c >= Q_BLOCK - 1, c - (Q_BLOCK - 1), c + (ROLL_W - Q_BLOCK + 1))
        pick = (r == _rel_index(e)).astype(F32)
        sums = []
        for h in range(N_HEADS):
            a = jnp.dot(flip, a_ref[h], precision=HIGHEST, preferred_element_type=F32)
            a = jnp.concatenate([a, jnp.zeros((Q_BLOCK, ROLL_W - K_SPAN), F32)], axis=1)
            a = pltpu.roll(a, 0, 1, stride=1, stride_axis=0)
            sums.append(jnp.sum(a, axis=0, keepdims=True))
        diag = jnp.concatenate(sums, axis=0)
        o_ref[...] = jnp.dot(diag, pick, precision=HIGHEST, preferred_element_type=F32)

    return pl.pallas_call(
        body, name="bias_grad",
        out_shape=jax.ShapeDtypeStruct((N_HEADS, N_REL_PAD), F32),
        compiler_params=_params(32),
    )(dbias)


def _gather_proj_fwd(x, norm_g, w_in_t):
    s = x.shape[0]
    tm = 512 if s % 512 == 0 else TOKEN_TILE
    nt = s // tm
    n_pad = Z_PAD // tm
    shard_w = w_in_t.shape[0]
    chip_w = 2 * shard_w
    n_chips = N_DEV // 2

    def body(order_ref, x_ref, g_ref, win_hbm, z_ref, xn_ref, wt_hbm, wt, hb, win_f32, send_sems, recv_sems,
             local_sems):
        j = pl.program_id(0)
        i = pl.program_id(1)
        x_, y_, c_ = _my_pos()
        me, sib = (x_, y_, c_), (x_, y_, 1 - c_)
        near = _other_chips(me)
        pick = lambda a, b: tuple(jnp.where(c_ == 0, u, v) for u, v in zip(a, b))
        passed_from, passed_to = pick(near[0], near[1]), pick(near[1], near[0])

        def rows_of(block):
            return wt.at[pl.ds(pl.multiple_of(_flat_id(block) * shard_w, 16), shard_w), :]

        def copy(k, block, to):
            return pltpu.make_async_remote_copy(
                src_ref=rows_of(block), dst_ref=rows_of(block),
                send_sem=send_sems.at[k], recv_sem=recv_sems.at[k], device_id=to, device_id_type=MESH)

        def sends():
            return ([copy(0, me, sib), copy(1, me, (*near[0], c_)), copy(2, me, (*near[1], c_)),
                     copy(3, (*passed_from, c_), (*passed_to, c_))]
                    + [copy(4 + n, (*near[n], c_), sib) for n in range(3)])

        keep = pltpu.make_async_copy(wt, wt_hbm, local_sems.at[0])

        @pl.when((j == 0) & (i == 0))
        def _():
            load = pltpu.make_async_copy(win_hbm, win_f32, local_sems.at[1])
            load.start()
            load.wait()
            rows_of(me)[...] = win_f32[...].astype(BF16)
            for cp in sends()[:3]:
                cp.start()
            copy(0, sib, me).wait_recv()

        @pl.when((j == 1) & (i == 0))
        def _():
            copy(1, (*near[0], c_), me).wait_recv()
            copy(2, (*near[1], c_), me).wait_recv()
            for cp in sends()[3:6]:
                cp.start()
            copy(4, (*near[0], 1 - c_), me).wait_recv()

        @pl.when((j == 2) & (i == 0))
        def _():
            copy(5, (*near[1], 1 - c_), me).wait_recv()

        @pl.when((j == 3) & (i == 0))
        def _():
            copy(3, (*near[2], c_), me).wait_recv()
            copy(6, (*near[2], c_), sib).start()
            copy(6, (*near[2], 1 - c_), me).wait_recv()
            keep.start()

        @pl.when(i < n_pad)
        def _():
            z_ref[...] = jnp.zeros(z_ref.shape, BF16)

        @pl.when(i >= n_pad)
        def _():
            rows = pl.ds(pl.multiple_of((i - n_pad) * tm, tm), tm)

            @pl.when(j == 0)
            def _():
                xf = x_ref[...]
                xn = xf * lax.rsqrt(jnp.mean(xf * xf, axis=-1, keepdims=True) + EPS)
                hb[rows, :] = (xn * g_ref[...]).astype(BF16)
                xn_ref[...] = xn.astype(BF16)

            chip_rows = pl.ds(pl.multiple_of(order_ref[j] * chip_w, 16), chip_w)
            blk = _dot(hb[rows, :], wt[chip_rows, :], NT)
            q_scale = jnp.where(order_ref[j] == 0, Q_SCALE, 1.0).astype(F32)
            z_ref[:, :D_A] = (blk[:, :D_A] * q_scale).astype(BF16)
            z_ref[:, D_A:] = blk[:, D_A:].astype(BF16)

        @pl.when((j == n_chips - 1) & (i == n_pad + nt - 1))
        def _():
            keep.wait()
            for cp in sends():
                cp.wait_send()

    pos = _my_pos()
    order = jnp.stack([2 * cx + cy for cx, cy in ((pos[0], pos[1]),) + _other_chips(pos)]).astype(jnp.int32)
    first_pass = lambda j, i: jnp.where(j == 0, jnp.maximum(i - n_pad, 0), nt - 1)
    grid_spec = pltpu.PrefetchScalarGridSpec(
        num_scalar_prefetch=1,
        grid=(n_chips, n_pad + nt),
        in_specs=[pl.BlockSpec((tm, D_MODEL), lambda j, i, o: (first_pass(j, i), 0)),
                  pl.BlockSpec((1, D_MODEL), lambda j, i, o: (0, 0)),
                  pl.BlockSpec(memory_space=pl.ANY)],
        out_specs=(pl.BlockSpec((tm, chip_w), lambda j, i, o: (i, o[j])),
                   pl.BlockSpec((tm, D_MODEL), lambda j, i, o: (first_pass(j, i), 0)),
                   pl.BlockSpec(memory_space=pl.ANY)),
        scratch_shapes=[pltpu.VMEM((D_IN, D_MODEL), BF16),
                        pltpu.VMEM((s, D_MODEL), BF16), pltpu.VMEM(w_in_t.shape, F32),
                        pltpu.SemaphoreType.DMA((N_DEV - 1,)), pltpu.SemaphoreType.DMA((N_DEV - 1,)),
                        pltpu.SemaphoreType.DMA((2,))])
    return pl.pallas_call(
        body, name="gather_proj_fwd",
        grid_spec=grid_spec,
        out_shape=(jax.ShapeDtypeStruct((Z_PAD + s, D_IN), BF16), jax.ShapeDtypeStruct((s, D_MODEL), BF16),
                   jax.ShapeDtypeStruct((D_IN, D_MODEL), BF16)),
        compiler_params=_params(60),
    )(order, x, norm_g, w_in_t)


def _attn_specs(rows):
    pairs = N_HEADS // 2
    return ([pl.BlockSpec((rows, 128), functools.partial(lambda which, p: (0, which * pairs + p), which))
             for which in range(3)]
            + [pl.BlockSpec((2, Q_BLOCK, K_SPAN), lambda p: (p, 0, 0))])


def _head_masks():
    lane = lax.broadcasted_iota(jnp.int32, (1, 128), 1)
    first = lane < HEAD_DIM
    return (first, jnp.logical_not(first))


def _stack_heads(x, masks):
    zero = jnp.zeros((), x.dtype)
    return jnp.concatenate([jnp.where(m, x, zero) for m in masks], axis=0)


STRIP = 16


def _softmax_strips(s_ref, bias_ref, b):
    valid = lax.broadcasted_iota(jnp.int32, (1, K_SPAN), 1) >= Z_PAD - b * Q_BLOCK
    for t in range(2 * Q_BLOCK // STRIP):
        hh, r = divmod(t * STRIP, Q_BLOCK)
        st = s_ref[t * STRIP:(t + 1) * STRIP, :] + bias_ref[hh, r:r + STRIP, :]
        st = jnp.where(valid, st, NEG_INF)
        e = jnp.exp(st - jnp.max(st, axis=-1, keepdims=True))
        yield e * (1.0 / jnp.sum(e, axis=-1, keepdims=True))


def _side_by_side_strips(strips):
    half = len(strips) // 2
    return jnp.concatenate([jnp.concatenate([a, c], axis=1) for a, c in zip(strips[:half], strips[half:])], axis=0)


def _attn_fwd(qkv, bias_table, shards):
    s = qkv.shape[0] - Z_PAD
    nb = s // Q_BLOCK
    n = len(shards)
    pairs = N_HEADS // 2

    def body(*refs):
        q_ref, k_ref, v_ref, bt_ref = refs[:4]
        shard_refs = refs[4:4 + n]
        o_ref = refs[4 + n]
        slot_refs = refs[5 + n:5 + 2 * n]
        stages = refs[5 + 2 * n:5 + 3 * n]
        s_scr, send_sems, recv_sems, local_sems = refs[5 + 3 * n:]
        p_id = pl.program_id(0)
        gather = _SlotGather(slot_refs, send_sems, recv_sems, own=stages)
        keep = [pltpu.make_async_copy(stages[a], slot_refs[a].at[_flat_id(_my_pos())], local_sems.at[a])
                for a in range(n)]

        @pl.when(p_id == 0)
        def _():
            for a in range(n):
                stages[a][...] = shard_refs[a][...].astype(BF16)
                keep[a].start()
            gather.start()

        @pl.when(p_id == 1)
        def _():
            gather.pass_on()

        masks = _head_masks()

        def scores(b, half):
            r0 = pl.multiple_of(b * Q_BLOCK, Q_BLOCK)
            q2 = _stack_heads(q_ref[pl.ds(r0 + Z_PAD, Q_BLOCK), :], masks)
            s_scr[half] = _dot(q2, k_ref[pl.ds(r0, K_SPAN), :], NT)

        def finish(b, half):
            r0 = pl.multiple_of(b * Q_BLOCK, Q_BLOCK)
            v2 = _stack_heads(v_ref[pl.ds(r0, K_SPAN), :], masks)
            p = [st.astype(BF16) for st in _softmax_strips(s_scr.at[half], bt_ref, b)]
            o_ref[pl.ds(r0, Q_BLOCK), :] = _dot(_side_by_side_strips(p), v2)

        def two_blocks(i, carry):
            b = 2 * i
            scores(b + 1, 1)
            finish(b, 0)
            scores(jnp.minimum(b + 2, nb - 1), 0)
            finish(b + 1, 1)
            return carry

        scores(0, 0)
        lax.fori_loop(0, nb // 2, two_blocks, 0)

        @pl.when(p_id == pairs - 1)
        def _():
            gather.finish()
            for cp in keep:
                cp.wait()

    hbm = pl.BlockSpec(memory_space=pl.ANY)
    return pl.pallas_call(
        body, name="attn_fwd",
        grid=(pairs,),
        in_specs=_attn_specs(s + Z_PAD) + [pl.BlockSpec(a.shape, lambda p: (0, 0)) for a in shards],
        out_specs=(pl.BlockSpec((s, 128), lambda p: (0, p)),) + (hbm,) * n,
        out_shape=(jax.ShapeDtypeStruct((s, D_A), F32),)
        + tuple(jax.ShapeDtypeStruct((N_DEV,) + a.shape, BF16) for a in shards),
        scratch_shapes=[pltpu.VMEM(a.shape, BF16) for a in shards]
        + [pltpu.VMEM((2, 2 * Q_BLOCK, K_SPAN), F32),
           pltpu.SemaphoreType.DMA((n, N_DEV - 1)), pltpu.SemaphoreType.DMA((n, N_DEV - 1)),
           pltpu.SemaphoreType.DMA((n,))],
        compiler_params=_params(48),
    )(qkv, qkv, qkv, bias_table, *shards)


def _attn_bwd(qkv, bias_table, d_out, to_chip):
    s = qkv.shape[0] - Z_PAD
    nb = s // Q_BLOCK
    n = len(to_chip)
    pairs = N_HEADS // 2

    def body(*refs):
        q_ref, k_ref, v_ref, bt_ref, do_ref = refs[:5]
        to_chip_refs = refs[5:5 + n]
        dqkv_ref, db_ref = refs[5 + n:7 + n]
        from_chip_refs = refs[7 + n:7 + 2 * n]
        dk_acc, dv_acc, s_scr, dp_scr, send_sems, recv_sems = refs[7 + 2 * n:]
        p_id = pl.program_id(0)

        @pl.when(p_id == 0)
        def _():
            for cp in _owner_copies(to_chip_refs, from_chip_refs, send_sems, recv_sems):
                cp.start()

        dk_acc[...] = jnp.zeros(dk_acc.shape, F32)
        dv_acc[...] = jnp.zeros(dv_acc.shape, F32)
        db_ref[...] = jnp.zeros(db_ref.shape, F32)
        masks = _head_masks()

        def operands(b):
            r0 = pl.multiple_of(b * Q_BLOCK, Q_BLOCK)
            q2 = _stack_heads(q_ref[pl.ds(r0 + Z_PAD, Q_BLOCK), :], masks)
            do2 = _stack_heads(do_ref[pl.ds(r0, Q_BLOCK), :], masks)
            return r0, q2, do2, k_ref[pl.ds(r0, K_SPAN), :]

        def ahead(b, half):
            r0, q2, do2, kcat = operands(b)
            s_scr[half] = _dot(q2, kcat, NT)
            dp_scr[half] = _dot(do2, v_ref[pl.ds(r0, K_SPAN), :], NT)

        def finish(b, half):
            r0, q2, do2, kcat = operands(b)
            p_strips, ds_strips = [], []
            for t, p in enumerate(_softmax_strips(s_scr.at[half], bt_ref, b)):
                hh, r = divmod(t * STRIP, Q_BLOCK)
                dp_t = dp_scr[half, t * STRIP:(t + 1) * STRIP, :]
                ds = p * (dp_t - jnp.sum(p * dp_t, axis=-1, keepdims=True))
                db_ref[hh, r:r + STRIP, :] += ds
                p_strips.append(p.astype(BF16))
                ds_strips.append(ds.astype(BF16))
            dq = _dot(_side_by_side_strips(ds_strips), _stack_heads(kcat, masks))
            dqkv_ref[0, pl.ds(r0, Q_BLOCK), :] = (dq * Q_SCALE).astype(BF16)
            dk_acc[pl.ds(r0, K_SPAN), :] += _dot(jnp.concatenate(ds_strips, axis=0), q2, TN)
            dv_acc[pl.ds(r0, K_SPAN), :] += _dot(jnp.concatenate(p_strips, axis=0), do2, TN)

        def two_blocks(i, carry):
            b = 2 * i
            ahead(b + 1, 1)
            finish(b, 0)
            ahead(jnp.minimum(b + 2, nb - 1), 0)
            finish(b + 1, 1)
            return carry

        ahead(0, 0)
        lax.fori_loop(0, nb // 2, two_blocks, 0)
        dqkv_ref[1] = dk_acc[Z_PAD:, :].astype(BF16)
        dqkv_ref[2] = dv_acc[Z_PAD:, :].astype(BF16)

        @pl.when(p_id == pairs - 1)
        def _():
            for cp in _owner_copies(to_chip_refs, from_chip_refs, send_sems, recv_sems):
                cp.wait_recv()
                cp.wait_send()

    hbm = pl.BlockSpec(memory_space=pl.ANY)
    return pl.pallas_call(
        body, name="attn_bwd",
        grid=(pairs,),
        in_specs=_attn_specs(s + Z_PAD) + [pl.BlockSpec((s, 128), lambda p: (0, p))] + [hbm] * n,
        out_specs=(pl.BlockSpec((3, s, 128), lambda p: (0, 0, p)),
                   pl.BlockSpec((2, Q_BLOCK, K_SPAN), lambda p: (p, 0, 0))) + (hbm,) * n,
        out_shape=(jax.ShapeDtypeStruct((3, s, D_A), BF16),
                   jax.ShapeDtypeStruct((N_HEADS, Q_BLOCK, K_SPAN), F32))
        + tuple(jax.ShapeDtypeStruct(t.shape, t.dtype) for t in to_chip),
        scratch_shapes=[pltpu.VMEM((s + Z_PAD, 128), F32), pltpu.VMEM((s + Z_PAD, 128), F32),
                        pltpu.VMEM((2, 2 * Q_BLOCK, K_SPAN), F32), pltpu.VMEM((2, 2 * Q_BLOCK, K_SPAN), F32),
                        pltpu.SemaphoreType.DMA((n, 3)), pltpu.SemaphoreType.DMA((n, 3))],
        compiler_params=_params(56),
    )(qkv, qkv, qkv, bias_table, d_out, *to_chip)


def _mid_fwd_bwd(x, target, attn_out, z, w_pa, w_pb, w_out, b_gate, ln_g, ln_b, w_s, b_s, final_g):
    s = x.shape[0]
    tm = TOKEN_TILE
    nt = s // tm

    def body(x_ref, t_ref, oa_ref, ga_ref, ub_ref, vb_ref, gb_ref, ta0_ref, ta1_ref, tb0_ref, tb1_ref,
             wpa_hbm, wpb_hbm, wout_hbm, bg_ref, lng_ref, lnb_ref, ws_ref, bs_ref, fg_ref,
             dx2_ref, doa_ref, dz_ref, dwout_hbm, dwpa_hbm, dwpb_hbm, dbg_ref, dfg_ref, dlng_ref, dlnb_ref, dws_ref,
             dbs_ref, loss_ref,
             wpa, wpb, wout, wmix, acc_out, acc_pa, acc_pb, sem):
        i = pl.program_id(0)

        @pl.when(i == 0)
        def _():
            loads = [pltpu.make_async_copy(src, dst, sem.at[n])
                     for n, (src, dst) in enumerate(((wpa_hbm, wpa), (wpb_hbm, wpb), (wout_hbm, wout)))]
            for cp in loads:
                cp.start()
            t_idx = lax.broadcasted_iota(jnp.int32, (SGU_CHUNK, SGU_CHUNK), 0)
            s_idx = lax.broadcasted_iota(jnp.int32, (SGU_CHUNK, SGU_CHUNK), 1)
            for g in range(N_GROUPS):
                wmix[g] = jnp.where(s_idx <= t_idx, ws_ref[g], 0.0).astype(BF16)
            for ref in (acc_out, acc_pa, acc_pb, dbg_ref, dfg_ref, dlng_ref, dlnb_ref, dws_ref, dbs_ref, loss_ref):
                ref[...] = jnp.zeros(ref.shape, F32)
            for cp in loads:
                cp.wait()

        def tile_fwd_bwd(rows):
            g_a = ga_ref[rows, :].astype(F32)
            u_b = ub_ref[rows, :].astype(F32)
            v_b = vb_ref[rows, :].astype(F32)
            g_b = gb_ref[rows, :].astype(F32)
            bg = bg_ref[...]
            sg_a = _sigmoid(g_a)
            silu_a = g_a * sg_a
            o_a = oa_ref[rows, :]
            y_a = (o_a * silu_a).astype(BF16)
            ug, dgelu_u = _gelu_and_grad(u_b)
            vg, dgelu_v = _gelu_and_grad(v_b)
            mu = jnp.mean(vg, axis=-1, keepdims=True)
            vc = vg - mu
            rstd = lax.rsqrt(jnp.mean(vc * vc, axis=-1, keepdims=True) + EPS)
            vhat = vc * rstd
            lng = lng_ref[...]
            vn = (vhat * lng + lnb_ref[...]).astype(BF16)
            sg_b = _sigmoid(g_b)
            silu_b = g_b * sg_b
            subs = [slice(n * SGU_CHUNK, (n + 1) * SGU_CHUNK) for n in range(tm // SGU_CHUNK)]
            mixed = jnp.concatenate([jnp.concatenate(
                [_dot(wmix[g], vn[sub, g * 128:(g + 1) * 128]) + bs_ref[g] for g in range(N_GROUPS)], axis=1)
                for sub in subs], axis=0)
            um = ug * mixed
            y_b = (um * silu_b).astype(BF16)
            gate_a = _sigmoid(jnp.concatenate([ta0_ref[rows, :], ta1_ref[rows, :]], axis=1).astype(F32)
                              + bg[:, :D_MODEL])
            gate_b = _sigmoid(jnp.concatenate([tb0_ref[rows, :], tb1_ref[rows, :]], axis=1).astype(F32)
                              + bg[:, D_MODEL:])
            p_a = _dot(y_a, wpa[...])
            p_b = _dot(y_b, wpb[...])
            merged = (gate_a * p_a + gate_b * p_b).astype(BF16)
            x2 = x_ref[rows, :] + _dot(merged, wout[...])
            r2 = lax.rsqrt(jnp.mean(x2 * x2, axis=-1, keepdims=True) + EPS)
            xh = x2 * r2
            fg = fg_ref[...]
            err = xh * fg - t_ref[rows, :]
            loss_ref[...] += jnp.sum(jnp.sum(err * err, axis=-1, keepdims=True), axis=0, keepdims=True) * (0.5 / D_MODEL)
            dy = err * (1.0 / D_MODEL)
            dfg_ref[...] += jnp.sum(dy * xh, axis=0, keepdims=True)
            gy = dy * fg
            dx2 = r2 * (gy - xh * jnp.mean(gy * xh, axis=-1, keepdims=True))
            dx2_ref[rows, :] = dx2
            dx2b = dx2.astype(BF16)
            dmerged = _dot(dx2b, wout[...], NT)
            acc_out[...] += _dot(merged, dx2b, TN)
            dp_a = dmerged * gate_a
            dp_b = dmerged * gate_b
            dgate_a = dp_a * p_a * (1.0 - gate_a)
            dgate_b = dp_b * p_b * (1.0 - gate_b)
            dbg_ref[:, :D_MODEL] += jnp.sum(dgate_a, axis=0, keepdims=True)
            dbg_ref[:, D_MODEL:] += jnp.sum(dgate_b, axis=0, keepdims=True)
            dz_ref[rows, 2048:3072] = dgate_a.astype(BF16)
            dz_ref[rows, 3072:4096] = dgate_b.astype(BF16)
            dp_ab = dp_a.astype(BF16)
            dp_bb = dp_b.astype(BF16)
            dy_a = _dot(dp_ab, wpa[...], NT)
            dy_b = _dot(dp_bb, wpb[...], NT)
            acc_pa[...] += _dot(y_a, dp_ab, TN)
            acc_pb[...] += _dot(y_b, dp_bb, TN)
            doa_ref[rows, :] = (dy_a * silu_a).astype(BF16)
            dz_ref[rows, 0:512] = (dy_a * o_a * (sg_a * (1.0 + g_a * (1.0 - sg_a)))).astype(BF16)
            dz_ref[rows, 1536:2048] = (dy_b * um * (sg_b * (1.0 + g_b * (1.0 - sg_b)))).astype(BF16)
            dys = dy_b * silu_b
            dz_ref[rows, 512:1024] = (dys * mixed * dgelu_u).astype(BF16)
            dmixed = dys * ug
            dmb = dmixed.astype(BF16)
            dvn_rows = []
            for sub in subs:
                dvn_parts = []
                for g in range(N_GROUPS):
                    cols = slice(g * 128, (g + 1) * 128)
                    dws_ref[g] += _dot(dmb[sub, cols], vn[sub, cols], NT)
                    dbs_ref[g] += jnp.sum(dmixed[sub, cols], axis=-1, keepdims=True)
                    dvn_parts.append(_dot(wmix[g], dmb[sub, cols], TN))
                dvn_rows.append(jnp.concatenate(dvn_parts, axis=1))
            dvn = jnp.concatenate(dvn_rows, axis=0)
            dlng_ref[...] += jnp.sum(dvn * vhat, axis=0, keepdims=True)
            dlnb_ref[...] += jnp.sum(dvn, axis=0, keepdims=True)
            dvh = dvn * lng
            dvg = rstd * (dvh - jnp.mean(dvh, axis=-1, keepdims=True)
                          - vhat * jnp.mean(dvh * vhat, axis=-1, keepdims=True))
            dz_ref[rows, 1024:1536] = (dvg * dgelu_v).astype(BF16)

        tile_fwd_bwd(slice(0, tm))

        @pl.when(i == nt - 1)
        def _():
            t_idx = lax.broadcasted_iota(jnp.int32, (SGU_CHUNK, SGU_CHUNK), 0)
            s_idx = lax.broadcasted_iota(jnp.int32, (SGU_CHUNK, SGU_CHUNK), 1)
            for g in range(N_GROUPS):
                dws_ref[g] = jnp.where(s_idx <= t_idx, dws_ref[g], 0.0)
            stores = [pltpu.make_async_copy(src, dst, sem.at[n])
                      for n, (src, dst) in enumerate(((acc_out, dwout_hbm), (acc_pa, dwpa_hbm), (acc_pb, dwpb_hbm)))]
            for cp in stores:
                cp.start()
            for cp in stores:
                cp.wait()

    tile = lambda w: pl.BlockSpec((tm, w), lambda i: (i, 0))
    whole = lambda shape: pl.BlockSpec(shape, lambda i: (0,) * len(shape))
    hbm = pl.BlockSpec(memory_space=pl.ANY)
    return pl.pallas_call(
        body, name="mid_fwd_bwd",
        grid=(nt,),
        in_specs=[tile(D_MODEL), tile(D_MODEL), tile(D_A)]
        + [pl.BlockSpec((tm, COL_BLOCK), functools.partial(lambda c, i: (i + Z_PAD // tm, c), c))
           for c in range(3, N_COL_BLOCKS)]
        + [hbm, hbm, hbm,
                  whole((1, 2 * D_MODEL)), whole((1, D_B)), whole((1, D_B)),
                  whole((N_GROUPS, SGU_CHUNK, SGU_CHUNK)), whole((N_GROUPS, SGU_CHUNK, 1)), whole((1, D_MODEL))],
        out_specs=(tile(D_MODEL), tile(D_A), tile(REST), hbm, hbm, hbm,
                   whole((1, 2 * D_MODEL)), whole((1, D_MODEL)), whole((1, D_B)), whole((1, D_B)),
                   whole((N_GROUPS, SGU_CHUNK, SGU_CHUNK)), whole((N_GROUPS, SGU_CHUNK, 1)), whole((1, 1))),
        out_shape=(jax.ShapeDtypeStruct((s, D_MODEL), F32), jax.ShapeDtypeStruct((s, D_A), BF16),
                   jax.ShapeDtypeStruct((s, REST), BF16),
                   jax.ShapeDtypeStruct((D_MODEL, D_MODEL), F32), jax.ShapeDtypeStruct((D_A, D_MODEL), F32),
                   jax.ShapeDtypeStruct((D_B, D_MODEL), F32),
                   jax.ShapeDtypeStruct((1, 2 * D_MODEL), F32), jax.ShapeDtypeStruct((1, D_MODEL), F32),
                   jax.ShapeDtypeStruct((1, D_B), F32), jax.ShapeDtypeStruct((1, D_B), F32),
                   jax.ShapeDtypeStruct((N_GROUPS, SGU_CHUNK, SGU_CHUNK), F32),
                   jax.ShapeDtypeStruct((N_GROUPS, SGU_CHUNK, 1), F32), jax.ShapeDtypeStruct((1, 1), F32)),
        scratch_shapes=[pltpu.VMEM((D_A, D_MODEL), BF16), pltpu.VMEM((D_B, D_MODEL), BF16),
                        pltpu.VMEM((D_MODEL, D_MODEL), BF16), pltpu.VMEM((N_GROUPS, SGU_CHUNK, SGU_CHUNK), BF16),
                        pltpu.VMEM((D_MODEL, D_MODEL), F32), pltpu.VMEM((D_A, D_MODEL), F32),
                        pltpu.VMEM((D_B, D_MODEL), F32),
                        pltpu.SemaphoreType.DMA((3,))],
        compiler_params=_params(56),
    )(x, target, attn_out, *([z] * (N_COL_BLOCKS - 3)), w_pa, w_pb, w_out, b_gate, ln_g, ln_b, w_s, b_s, final_g)


def _proj_bwd_x(dqkv, drest, x, dx2, norm_g, w_in_t, to_chip, small):
    s = x.shape[0]
    tm = 512 if s % 512 == 0 else TOKEN_TILE
    nt = s // tm
    ws_shape = (N_GROUPS * SGU_CHUNK, SGU_CHUNK)
    rows = to_chip.shape[1]
    half = D_MODEL // 2
    left, right = slice(0, half), slice(half, D_MODEL)

    def body(dqkv_ref, dr_ref, x_ref, dx2_ref, g_ref, w_hbm, tc_ref,
             ng_ref, bg_ref, rel_ref, lng_ref, lnb_ref, fg_ref, loss_ref, bs_ref, ws_ref,
             dx_ref, fc_ref, slab_land, ws_land,
             w, slab_stage, ws_stage, via_x, via_y, mine, out_x, out_y, sem, send_sems, recv_sems,
             gather_send, gather_recv, keep_sems):
        i = pl.program_id(0)
        x_, y_, c_ = _my_pos()
        me = _flat_id((x_, y_, c_))
        xn, yn = (1 - x_, y_, c_), (x_, 1 - y_, c_)
        gather = _SlotGather([slab_land, ws_land], gather_send, gather_recv, own=[slab_stage, ws_stage])
        keep = [pltpu.make_async_copy(stage, land.at[me], keep_sems.at[k]) for k, (stage, land) in enumerate(
            ((slab_stage, slab_land), (ws_stage, ws_land)))]

        def copy(k, src, dst, to):
            return pltpu.make_async_remote_copy(src_ref=src, dst_ref=dst, send_sem=send_sems.at[k],
                                                recv_sem=recv_sems.at[k], device_id=to, device_id_type=MESH)

        first = [copy(0, tc_ref.at[0, :, left], fc_ref.at[0, :, left], xn), copy(1, tc_ref.at[2, :, left], via_x, xn),
                 copy(2, tc_ref.at[1, :, right], fc_ref.at[1, :, right], yn), copy(3, tc_ref.at[2, :, right], via_y, yn)]
        second = [copy(4, out_y, fc_ref.at[1, :, left], yn), copy(5, out_x, fc_ref.at[0, :, right], xn)]

        def add_and_send(arrival, landed, own_half, stage, onward):
            load = pltpu.make_async_copy(own_half, mine, sem)
            load.start()
            arrival.wait_recv()
            load.wait()
            stage[...] = (mine[...].astype(F32) + landed[...].astype(F32)).astype(BF16)
            onward.start()

        @pl.when(i == 0)
        def _():
            cp = pltpu.make_async_copy(w_hbm, w, sem)
            cp.start()
            slab_stage[...] = jnp.zeros(slab_stage.shape, F32)
            slab_stage[ROW_NORM_G:ROW_NORM_G + 1, :] = ng_ref[...]
            slab_stage[ROW_B_GATE:ROW_B_GATE + 1, :] = bg_ref[:, :D_MODEL]
            slab_stage[ROW_B_GATE + 1:ROW_B_GATE + 2, :] = bg_ref[:, D_MODEL:]
            slab_stage[ROW_LN_G:ROW_LN_G + 1, :D_B] = lng_ref[...]
            slab_stage[ROW_LN_B:ROW_LN_B + 1, :D_B] = lnb_ref[...]
            slab_stage[ROW_FINAL_G:ROW_FINAL_G + 1, :] = fg_ref[...]
            slab_stage[ROW_LOSS:ROW_LOSS + 1, :1] = loss_ref[...]
            slab_stage[ROW_REL:ROW_REL + N_HEADS, :N_REL_PAD] = rel_ref[...]
            eye = (lax.broadcasted_iota(jnp.int32, (SGU_CHUNK, SGU_CHUNK), 0)
                   == lax.broadcasted_iota(jnp.int32, (SGU_CHUNK, SGU_CHUNK), 1))
            for g in range(N_GROUPS):
                row = jnp.sum(jnp.where(eye, bs_ref[g], 0.0), axis=0, keepdims=True)
                slab_stage[ROW_B_S + g:ROW_B_S + g + 1, :SGU_CHUNK] = row
            ws_stage[...] = ws_ref[...]
            for cp_keep in keep:
                cp_keep.start()
            gather.start()
            for rc in first:
                rc.start()
            cp.wait()

        @pl.when(i == nt // 2)
        def _():
            gather.pass_on()
            add_and_send(first[1], via_x, tc_ref.at[1, :, left], out_y, second[0])
            add_and_send(first[3], via_y, tc_ref.at[0, :, right], out_x, second[1])

        dh = None
        for c in range(N_COL_BLOCKS):
            dz = dqkv_ref[c] if c < 3 else dr_ref[:, (c - 3) * COL_BLOCK:(c - 2) * COL_BLOCK]
            part = _dot(dz, w[c * COL_BLOCK:(c + 1) * COL_BLOCK, :])
            dh = part if dh is None else dh + part
        xf = x_ref[...]
        r = lax.rsqrt(jnp.mean(xf * xf, axis=-1, keepdims=True) + EPS)
        xn = xf * r
        gh = dh * g_ref[...]
        dx_ref[...] = r * (gh - xn * jnp.mean(gh * xn, axis=-1, keepdims=True)) + dx2_ref[...]

        @pl.when(i == nt - 1)
        def _():
            gather.finish()
            for cp_keep in keep:
                cp_keep.wait()
            for k in (0, 2, 4, 5):
                (first + second)[k].wait_recv()
            for rc in first + second:
                rc.wait_send()

    hbm = pl.BlockSpec(memory_space=pl.ANY)
    whole = lambda a: pl.BlockSpec(a.shape, lambda i: (0,) * a.ndim)
    lands = ((N_DEV, SLAB_ROWS, D_MODEL), (N_DEV,) + ws_shape)
    return pl.pallas_call(
        body, name="proj_bwd_x",
        grid=(nt,),
        in_specs=[pl.BlockSpec((3, tm, D_A), lambda i: (0, i, 0)),
                  pl.BlockSpec((tm, REST), lambda i: (i, 0)),
                  pl.BlockSpec((tm, D_MODEL), lambda i: (i, 0)),
                  pl.BlockSpec((tm, D_MODEL), lambda i: (i, 0)),
                  pl.BlockSpec((1, D_MODEL), lambda i: (0, 0)),
                  hbm, hbm] + [whole(a) for a in small],
        out_specs=(pl.BlockSpec((tm, D_MODEL), lambda i: (i, 0)), hbm, hbm, hbm),
        out_shape=(jax.ShapeDtypeStruct((s, D_MODEL), F32), jax.ShapeDtypeStruct((2, rows, D_MODEL), BF16))
        + tuple(jax.ShapeDtypeStruct(shape, F32) for shape in lands),
        scratch_shapes=[pltpu.VMEM((D_IN, D_MODEL), BF16)]
        + [pltpu.VMEM(shape[1:], F32) for shape in lands]
        + [pltpu.VMEM((rows, half), BF16)] * 5
        + [pltpu.SemaphoreType.DMA, pltpu.SemaphoreType.DMA((6,)), pltpu.SemaphoreType.DMA((6,)),
           pltpu.SemaphoreType.DMA((2, N_DEV - 1)), pltpu.SemaphoreType.DMA((2, N_DEV - 1)),
           pltpu.SemaphoreType.DMA((2,))],
        compiler_params=_params(56),
    )(dqkv, drest, x, dx2, norm_g, w_in_t, to_chip, *small)


def _proj_bwd_w(xn, dqkv, drest, norm_g, w_in_t):
    s = xn.shape[0]
    tk = min(s, 1024)
    nk = s // tk

    def body(xn_ref, dqkv_ref, dr_ref, g_ref, w_ref, o_ref, dg_ref, acc):
        j = pl.program_id(0)
        i = pl.program_id(1)

        @pl.when((j == 0) & (i == 0))
        def _():
            dg_ref[...] = jnp.zeros(dg_ref.shape, F32)

        @pl.when(i == 0)
        def _():
            acc[...] = jnp.zeros(acc.shape, F32)

        @pl.when(j < 3)
        def _():
            acc[...] += _dot(dqkv_ref[...], xn_ref[...], TN)

        @pl.when(j >= 3)
        def _():
            acc[...] += _dot(dr_ref[...], xn_ref[...], TN)

        @pl.when(i == nk - 1)
        def _():
            m = acc[...]
            o_ref[...] = (m * g_ref[...]).astype(BF16)
            dg_ref[...] += jnp.sum(m * w_ref[...].astype(F32), axis=0, keepdims=True)

    return pl.pallas_call(
        body, name="proj_bwd_w",
        grid=(N_COL_BLOCKS, nk),
        in_specs=[pl.BlockSpec((tk, D_MODEL), lambda j, i: (i, 0)),
                  pl.BlockSpec((None, tk, COL_BLOCK),
                               lambda j, i: (jnp.minimum(j, 2), jnp.where(j < 3, i, nk - 1), 0)),
                  pl.BlockSpec((tk, COL_BLOCK),
                               lambda j, i: (jnp.where(j >= 3, i, 0), jnp.maximum(j - 3, 0))),
                  pl.BlockSpec((1, D_MODEL), lambda j, i: (0, 0)),
                  pl.BlockSpec((COL_BLOCK, D_MODEL), lambda j, i: (j, 0))],
        out_specs=(pl.BlockSpec((COL_BLOCK, D_MODEL), lambda j, i: (j, 0)),
                   pl.BlockSpec((1, D_MODEL), lambda j, i: (0, 0))),
        out_shape=(jax.ShapeDtypeStruct((D_IN, D_MODEL), BF16), jax.ShapeDtypeStruct((1, D_MODEL), F32)),
        scratch_shapes=[pltpu.VMEM((COL_BLOCK, D_MODEL), F32)],
        compiler_params=_params(40),
    )(xn, dqkv, drest, norm_g, w_in_t)


def _adamw_math(w, g, m, v):
    c1 = 1.0 - ADAM_B1 ** ADAM_STEP
    c2 = 1.0 - ADAM_B2 ** ADAM_STEP
    nm = ADAM_B1 * m + (1.0 - ADAM_B1) * g
    nv = ADAM_B2 * v + (1.0 - ADAM_B2) * (g * g)
    return -ADAM_LR * ((nm / c1) / (jnp.sqrt(nv / c2) + ADAM_EPS) + ADAM_WD * w), nm, nv


def _adamw(name, w, g, m, v, from_chip):
    rows, cols = w.shape
    tr = rows if rows * cols <= 512 * 1024 else next(t for t in range(256, 7, -8) if rows % t == 0)

    def body(w_ref, g_ref, m_ref, v_ref, t_ref, g_out, d_ref, nm_ref, nv_ref):
        gg = g_ref[...]
        for j in range(from_chip.shape[0]):
            gg = gg + t_ref[j].astype(F32)
        g_out[...] = gg
        d_ref[...], nm_ref[...], nv_ref[...] = _adamw_math(w_ref[...], gg, m_ref[...], v_ref[...])

    spec = pl.BlockSpec((tr, cols), lambda i: (i, 0))
    shape = jax.ShapeDtypeStruct((rows, cols), F32)
    return pl.pallas_call(
        body, name=name,
        grid=(rows // tr,),
        in_specs=[spec] * 4 + [pl.BlockSpec((from_chip.shape[0], tr, cols), lambda i: (0, i, 0))],
        out_specs=(spec,) * 4, out_shape=(shape,) * 4,
        compiler_params=_params(32),
    )(w, g, m, v, from_chip)


_SMALL = (("norm_g", (1, D_MODEL)), ("b_gate", (1, 2 * D_MODEL)), ("rel_bias", (N_HEADS, N_REL)),
          ("sgu_ln_g", (1, D_B)), ("sgu_ln_b", (1, D_B)), ("w_s", (N_GROUPS * SGU_CHUNK, SGU_CHUNK)),
          ("b_s", (N_GROUPS, SGU_CHUNK)), ("final_g", (1, D_MODEL)))


def _adamw_small(slabs, ws_all, weights, moments_m, moments_v):
    k = len(_SMALL)

    def total(ref):
        acc = ref[0]
        for d in range(1, N_DEV):
            acc = acc + ref[d]
        return acc

    def body(*refs):
        slab_ref, ws_ref = refs[:2]
        w_refs, m_refs, v_refs = refs[2:2 + k], refs[2 + k:2 + 2 * k], refs[2 + 2 * k:2 + 3 * k]
        outs = refs[2 + 3 * k:]
        slab = total(slab_ref)
        grads = {
            "norm_g": slab[ROW_NORM_G:ROW_NORM_G + 1, :],
            "b_gate": jnp.concatenate([slab[ROW_B_GATE:ROW_B_GATE + 1, :], slab[ROW_B_GATE + 1:ROW_B_GATE + 2, :]], axis=1),
            "rel_bias": slab[ROW_REL:ROW_REL + N_HEADS, :N_REL],
            "sgu_ln_g": slab[ROW_LN_G:ROW_LN_G + 1, :D_B],
            "sgu_ln_b": slab[ROW_LN_B:ROW_LN_B + 1, :D_B],
            "w_s": total(ws_ref),
            "b_s": slab[ROW_B_S:ROW_B_S + N_GROUPS, :SGU_CHUNK],
            "final_g": slab[ROW_FINAL_G:ROW_FINAL_G + 1, :],
        }
        for n, (name, _) in enumerate(_SMALL):
            g = grads[name]
            outs[n][...] = g
            outs[k + n][...], outs[2 * k + n][...], outs[3 * k + n][...] = _adamw_math(
                w_refs[n][...], g, m_refs[n][...], v_refs[n][...])
        outs[4 * k][...] = slab[ROW_LOSS:ROW_LOSS + 1, :1]

    vmem = pl.BlockSpec(memory_space=pltpu.VMEM)
    shapes = tuple(jax.ShapeDtypeStruct(shape, F32) for _, shape in _SMALL)
    return pl.pallas_call(
        body, name="adamw_small",
        out_shape=shapes * 4 + (jax.ShapeDtypeStruct((1, 1), F32),),
        in_specs=[vmem] * (2 + 3 * k), out_specs=tuple([vmem] * (4 * k + 1)),
        compiler_params=_params(16),
    )(slabs, ws_all, *weights, *moments_m, *moments_v)


def _pad_rel(a):
    return jnp.pad(a.reshape(N_HEADS, N_REL), ((0, 0), (0, N_REL_PAD - N_REL)))


def kernel(x, norm_g, w_in, b_gate, rel_bias, sgu_ln_g, sgu_ln_b, w_s, b_s, w_pa, w_pb, w_out, final_g, loss_target, m_norm_g, m_w_in, m_b_gate, m_rel_bias, m_sgu_ln_g, m_sgu_ln_b, m_w_s, m_b_s, m_w_pa, m_w_pb, m_w_out, m_final_g, v_norm_g, v_w_in, v_b_gate, v_rel_bias, v_sgu_ln_g, v_sgu_ln_b, v_w_s, v_b_s, v_w_pa, v_w_pb, v_w_out, v_final_g):
    s = x.shape[1]
    xs = x.reshape(s, D_MODEL)
    tgt = loss_target.reshape(s, D_MODEL)

    bias_table = _bias_table(_pad_rel(rel_bias))
    w_in_t = jnp.swapaxes(w_in[0], 0, 1)
    qkv, x_norm, w_in_t_full = _gather_proj_fwd(xs, norm_g, w_in_t)
    attn_out, g_pa, g_pb, g_out = _attn_fwd(qkv, bias_table, (w_pa[0], w_pb[0], w_out[0]))
    w_pa_full = jnp.transpose(g_pa, (1, 0, 2)).reshape(D_A, D_MODEL)
    w_pb_full = jnp.transpose(g_pb, (1, 0, 2)).reshape(D_B, D_MODEL)
    w_out_full = g_out.reshape(D_MODEL, D_MODEL)

    (dx2, d_attn, drest, dw_out, dw_pa, dw_pb, d_bgate, d_fg, d_lng, d_lnb, d_ws, d_bs, loss_part) = _mid_fwd_bwd(
        xs, tgt, attn_out, qkv, w_pa_full, w_pb_full, w_out_full, b_gate, sgu_ln_g, sgu_ln_b, w_s[0],
        b_s.reshape(N_GROUPS, SGU_CHUNK, 1), final_g.reshape(1, D_MODEL))

    own_pa, own_pb, own_out, tc_pa, tc_pb, tc_out = _reduce_chip(
        "reduce_chip_proj", (dw_pa, dw_pb, dw_out), (1, 1, 0))
    dqkv, dbias, fc_pa, fc_pb, fc_out = _attn_bwd(qkv, bias_table, d_attn, (tc_pa, tc_pb, tc_out))
    d_rel = _bias_grad(dbias)
    dw_in_t, d_ng = _proj_bwd_w(x_norm, dqkv, drest, norm_g, w_in_t_full)
    own_in, tc_in = _reduce_chip("reduce_chip_in", (dw_in_t,), (0,))
    grad_x, fc_in, slabs, ws_all = _proj_bwd_x(
        dqkv, drest, xs, dx2, norm_g, w_in_t_full, tc_in,
        (d_ng, d_bgate, d_rel, d_lng, d_lnb, d_fg, loss_part, d_bs, d_ws.reshape(N_GROUPS * SGU_CHUNK, SGU_CHUNK)))
    big = {"w_in": tuple(jnp.swapaxes(t, 0, 1)[None] for t in _adamw(
        "adamw_w_in", w_in_t, own_in, jnp.swapaxes(m_w_in[0], 0, 1), jnp.swapaxes(v_w_in[0], 0, 1), fc_in))}
    for name, w, g, fc, m, v in (("w_pa", w_pa, own_pa, fc_pa, m_w_pa, v_w_pa),
                                 ("w_pb", w_pb, own_pb, fc_pb, m_w_pb, v_w_pb),
                                 ("w_out", w_out, own_out, fc_out, m_w_out, v_w_out)):
        big[name] = tuple(t[None] for t in _adamw("adamw_" + name, w[0], g, m[0], v[0], fc))

    as_2d = lambda leaves: [a.reshape(shape) for a, (_, shape) in zip(leaves, _SMALL)]
    small_out = _adamw_small(
        slabs, ws_all, as_2d((norm_g, b_gate, rel_bias, sgu_ln_g, sgu_ln_b, w_s, b_s, final_g)),
        as_2d((m_norm_g, m_b_gate, m_rel_bias, m_sgu_ln_g, m_sgu_ln_b, m_w_s, m_b_s, m_final_g)),
        as_2d((v_norm_g, v_b_gate, v_rel_bias, v_sgu_ln_g, v_sgu_ln_b, v_w_s, v_b_s, v_final_g)))
    small_index = {name: n for n, (name, _) in enumerate(_SMALL)}

    def leaf(kind, name, like):
        if name in big:
            return big[name][kind]
        return small_out[kind * len(_SMALL) + small_index[name]].reshape(like.shape)

    weights = (("norm_g", norm_g), ("w_in", w_in), ("b_gate", b_gate), ("rel_bias", rel_bias), ("sgu_ln_g", sgu_ln_g),
               ("sgu_ln_b", sgu_ln_b), ("w_s", w_s), ("b_s", b_s), ("w_pa", w_pa), ("w_pb", w_pb), ("w_out", w_out),
               ("final_g", final_g))
    outs = [small_out[-1].reshape(()), grad_x.reshape(x.shape)]
    for kind in range(4):
        outs.extend(leaf(kind, name, like) for name, like in weights)
    return tuple(outs)
```

```python
import functools
import math

import jax
import jax.numpy as jnp
from jax import lax
from jax.experimental import pallas as pl
from jax.experimental.pallas import tpu as pltpu

F32 = jnp.float32
BF16 = jnp.bfloat16
MESH = pl.DeviceIdType.MESH
N_DEV = 8

D_MODEL = 1024
D_A = 512
D_B = 512
D_IN = 5632
N_HEADS = 8
HEAD_DIM = 64
N_PREV = 8
REL_CLIP = 128
N_REL = 2 * REL_CLIP + 1
N_REL_PAD = 384
SGU_CHUNK = 128
N_GROUPS = 4
EPS = 1e-6
NEG_INF = -1e30
Q_SCALE = HEAD_DIM ** -0.5

Q_BLOCK = 256
K_SPAN = 768
Z_PAD = K_SPAN - Q_BLOCK
ROLL_W = 1024
COL_BLOCK = 512
N_COL_BLOCKS = D_IN // COL_BLOCK
REST = D_IN - 3 * D_A
TOKEN_TILE = 256

ADAM_LR = 0.001
ADAM_B1 = 0.9
ADAM_B2 = 0.999
ADAM_EPS = 1e-08
ADAM_WD = 0.01
ADAM_STEP = 10

GELU_C = math.sqrt(2.0 / math.pi)
GELU_A = 0.044715

NT = (((1,), (1,)), ((), ()))
TN = (((0,), (0,)), ((), ()))
HIGHEST = lax.Precision.HIGHEST


def _params(vmem_mb, **kw):
    return pltpu.CompilerParams(vmem_limit_bytes=vmem_mb * 1024 * 1024, **kw)


def _dot(a, b, dims=None):
    if dims is None:
        return jnp.dot(a, b, preferred_element_type=F32)
    return lax.dot_general(a, b, dims, preferred_element_type=F32)


def _sigmoid(x):
    return 0.5 * jnp.tanh(0.5 * x) + 0.5


def _gelu_and_grad(u):
    u2 = u * u
    t = jnp.tanh(GELU_C * (u + GELU_A * u * u2))
    half = 0.5 * (1.0 + t)
    g = u * half
    dg = half + 0.5 * u * (1.0 - t * t) * (GELU_C * (1.0 + 3.0 * GELU_A * u2))
    return g, dg


def _my_pos():
    return lax.axis_index("x"), lax.axis_index("y"), lax.axis_index("c")


def _flat_id(pos):
    return 4 * pos[0] + 2 * pos[1] + pos[2]


def _other_chips(pos):
    x, y, _ = pos
    return ((1 - x, y), (x, 1 - y), (1 - x, 1 - y))


class _SlotGather:
    def __init__(self, bufs, send_sems, recv_sems, own=None):
        self.bufs, self.send_sems, self.recv_sems = bufs, send_sems, recv_sems
        self.own = own if own is not None else [None] * len(bufs)
        x, y, c = _my_pos()
        self.c, self.me, self.sib = c, (x, y, c), (x, y, 1 - c)
        self.chips = _other_chips(self.me)

    def _copy(self, a, k, block, to):
        slot = _flat_id(block)
        src = self.own[a] if (k < 4 and self.own[a] is not None) else self.bufs[a].at[slot]
        return pltpu.make_async_remote_copy(
            src_ref=src, dst_ref=self.bufs[a].at[slot],
            send_sem=self.send_sems.at[a, k], recv_sem=self.recv_sems.at[a, k], device_id=to, device_id_type=MESH)

    def _own_sends(self):
        n = len(self.bufs)
        return ([self._copy(a, 1 + j, self.me, (*chip, self.c)) for j, chip in enumerate(self.chips) for a in range(n)]
                + [self._copy(a, 0, self.me, self.sib) for a in range(n)])

    def _passes(self):
        return [self._copy(a, 4 + j, (*chip, self.c), self.sib)
                for j, chip in enumerate(self.chips) for a in range(len(self.bufs))]

    def start(self):
        for cp in self._own_sends():
            cp.start()

    def pass_on(self):
        for j, chip in enumerate(self.chips):
            for a in range(len(self.bufs)):
                self._copy(a, 1 + j, (*chip, self.c), self.me).wait_recv()
                self._copy(a, 4 + j, (*chip, self.c), self.sib).start()

    def finish(self):
        for a in range(len(self.bufs)):
            self._copy(a, 0, self.sib, self.me).wait_recv()
            for j, chip in enumerate(self.chips):
                self._copy(a, 4 + j, (*chip, 1 - self.c), self.me).wait_recv()
        for cp in self._own_sends() + self._passes():
            cp.wait_send()


def _reduce_chip(name, parts, sharded_dim):
    n = len(parts)
    shapes = []
    for p, dim in zip(parts, sharded_dim):
        shape = list(p.shape)
        shape[dim] //= N_DEV
        shapes.append(tuple(shape))

    def body(*refs):
        full, own, to_chip = refs[:n], refs[n:2 * n], refs[2 * n:3 * n]
        ins, from_sib = refs[3 * n:4 * n], refs[4 * n:5 * n]
        send_sems, recv_sems = refs[5 * n], refs[5 * n + 1]
        x, y, c = _my_pos()
        sib = (x, y, 1 - c)
        chips = ((x, y),) + _other_chips((x, y, c))
        for a in range(n):
            rows, cols = shapes[a]
            for d in range(N_DEV):
                if sharded_dim[a] == 0:
                    ins[a][d] = full[a][d * rows:(d + 1) * rows, :].astype(BF16)
                else:
                    ins[a][d] = full[a][:, d * cols:(d + 1) * cols].astype(BF16)

        def to_sibling(a, r):
            return pltpu.make_async_remote_copy(
                src_ref=ins[a].at[_flat_id((*chips[r], 1 - c))], dst_ref=from_sib[a].at[r],
                send_sem=send_sems.at[a, r], recv_sem=recv_sems.at[a, r], device_id=sib, device_id_type=MESH)

        sends = [to_sibling(a, r) for r in (1, 2, 3, 0) for a in range(n)]
        for cp in sends:
            cp.start()
        for r in (1, 2, 3, 0):
            for a in range(n):
                to_sibling(a, r).wait_recv()
                both = ins[a][_flat_id((*chips[r], c))].astype(F32) + from_sib[a][r].astype(F32)
                if r == 0:
                    own[a][...] = both
                else:
                    to_chip[a][r - 1] = both.astype(BF16)
        for cp in sends:
            cp.wait_send()

    vmem = pl.BlockSpec(memory_space=pltpu.VMEM)
    return pl.pallas_call(
        body, name=name,
        out_shape=tuple(jax.ShapeDtypeStruct(sh, F32) for sh in shapes)
        + tuple(jax.ShapeDtypeStruct((3,) + sh, BF16) for sh in shapes),
        in_specs=[vmem] * n, out_specs=tuple([vmem] * (2 * n)),
        scratch_shapes=[pltpu.VMEM((N_DEV,) + sh, BF16) for sh in shapes]
        + [pltpu.VMEM((4,) + sh, BF16) for sh in shapes]
        + [pltpu.SemaphoreType.DMA((n, 4)), pltpu.SemaphoreType.DMA((n, 4))],
        compiler_params=_params(56),
    )(*parts)


def _owner_copies(to_chip, from_chip, send_sems, recv_sems):
    x, y, c = _my_pos()
    return [pltpu.make_async_remote_copy(
        src_ref=to_chip[a].at[j], dst_ref=from_chip[a].at[j],
        send_sem=send_sems.at[a, j], recv_sem=recv_sems.at[a, j], device_id=(*chip, c), device_id_type=MESH)
        for a in range(len(to_chip)) for j, chip in enumerate(_other_chips((x, y, c)))]


ROW_NORM_G, ROW_B_GATE, ROW_LN_G, ROW_LN_B, ROW_FINAL_G, ROW_LOSS, ROW_REL, ROW_B_S, SLAB_ROWS = 0, 1, 3, 4, 5, 6, 8, 16, 24


def _rel_index(e):
    lo, hi = Z_PAD - REL_CLIP, Z_PAD + REL_CLIP
    return jnp.where(e <= lo, 2 * REL_CLIP, jnp.where(e < hi, hi - e, jnp.where(e <= K_SPAN, 0, 2 * REL_CLIP)))


def _bias_table(rel_bias_pad):
    def body(rb_ref, bt_ref):
        c = lax.broadcasted_iota(jnp.int32, (N_REL_PAD, ROLL_W), 1)
        r = lax.broadcasted_iota(jnp.int32, (N_REL_PAD, ROLL_W), 0)
        pick = (r == _rel_index(c)).astype(F32)
        rows = jnp.dot(rb_ref[...], pick, precision=HIGHEST, preferred_element_type=F32)
        qc = lax.broadcasted_iota(jnp.int32, (Q_BLOCK, K_SPAN), 0) >> 6
        kc = lax.broadcasted_iota(jnp.int32, (Q_BLOCK, K_SPAN), 1) >> 6
        band = (kc >= qc) & (kc <= qc + N_PREV)
        for h in range(N_HEADS):
            t = jnp.broadcast_to(rows[h:h + 1, :], (Q_BLOCK, ROLL_W))
            t = pltpu.roll(t, 0, 1, stride=1, stride_axis=0)
            bt_ref[h] = jnp.where(band, t[:, :K_SPAN], NEG_INF)

    return pl.pallas_call(
        body, name="bias_table",
        out_shape=jax.ShapeDtypeStruct((N_HEADS, Q_BLOCK, K_SPAN), F32),
        compiler_params=_params(32),
    )(rel_bias_pad)


def _bias_grad(dbias):
    def body(a_ref, o_ref):
        rr = lax.broadcasted_iota(jnp.int32, (Q_BLOCK, Q_BLOCK), 0)
        cc = lax.broadcasted_iota(jnp.int32, (Q_BLOCK, Q_BLOCK), 1)
        flip = (rr + cc == Q_BLOCK - 1).astype(F32)
        c = lax.broadcasted_iota(jnp.int32, (ROLL_W, N_REL_PAD), 0)
        r = lax.broadcasted_iota(jnp.int32, (ROLL_W, N_REL_PAD), 1)
        e = jnp.where(c >= Q_BLOCK - 1, c - (Q_BLOCK - 1), c + (ROLL_W - Q_BLOCK + 1))
        pick = (r == _rel_index(e)).astype(F32)
        sums = []
        for h in range(N_HEADS):
            a = jnp.dot(flip, a_ref[h], precision=HIGHEST, preferred_element_type=F32)
            a = jnp.concatenate([a, jnp.zeros((Q_BLOCK, ROLL_W - K_SPAN), F32)], axis=1)
            a = pltpu.roll(a, 0, 1, stride=1, stride_axis=0)
            sums.append(jnp.sum(a, axis=0, keepdims=True))
        diag = jnp.concatenate(sums, axis=0)
        o_ref[...] = jnp.dot(diag, pick, precision=HIGHEST, preferred_element_type=F32)

    return pl.pallas_call(
        body, name="bias_grad",
        out_shape=jax.ShapeDtypeStruct((N_HEADS, N_REL_PAD), F32),
        compiler_params=_params(32),
    )(dbias)


def _gather_proj_fwd(x, norm_g, w_in_t):
    s = x.shape[0]
    tm = 512 if s % 512 == 0 else TOKEN_TILE
    nt = s // tm
    n_pad = Z_PAD // tm
    shard_w = w_in_t.shape[0]
    chip_w = 2 * shard_w
    n_chips = N_DEV // 2

    def body(order_ref, x_ref, g_ref, win_hbm, z_ref, xn_ref, wt_hbm, wt, hb, win_f32, send_sems, recv_sems,
             local_sems):
        j = pl.program_id(0)
        i = pl.program_id(1)
        x_, y_, c_ = _my_pos()
        me, sib = (x_, y_, c_), (x_, y_, 1 - c_)
        near = _other_chips(me)
        pick = lambda a, b: tuple(jnp.where(c_ == 0, u, v) for u, v in zip(a, b))
        passed_from, passed_to = pick(near[0], near[1]), pick(near[1], near[0])

        def rows_of(block):
            return wt.at[pl.ds(pl.multiple_of(_flat_id(block) * shard_w, 16), shard_w), :]

        def copy(k, block, to):
            return pltpu.make_async_remote_copy(
                src_ref=rows_of(block), dst_ref=rows_of(block),
                send_sem=send_sems.at[k], recv_sem=recv_sems.at[k], device_id=to, device_id_type=MESH)

        def sends():
            return ([copy(0, me, sib), copy(1, me, (*near[0], c_)), copy(2, me, (*near[1], c_)),
                     copy(3, (*passed_from, c_), (*passed_to, c_))]
                    + [copy(4 + n, (*near[n], c_), sib) for n in range(3)])

        keep = pltpu.make_async_copy(wt, wt_hbm, local_sems.at[0])

        @pl.when((j == 0) & (i == 0))
        def _():
            load = pltpu.make_async_copy(win_hbm, win_f32, local_sems.at[1])
            load.start()
            load.wait()
            rows_of(me)[...] = win_f32[...].astype(BF16)
            for cp in sends()[:3]:
                cp.start()
            copy(0, sib, me).wait_recv()

        @pl.when((j == 1) & (i == 0))
        def _():
            copy(1, (*near[0], c_), me).wait_recv()
            copy(2, (*near[1], c_), me).wait_recv()
            for cp in sends()[3:6]:
                cp.start()
            copy(4, (*near[0], 1 - c_), me).wait_recv()

        @pl.when((j == 2) & (i == 0))
        def _():
            copy(5, (*near[1], 1 - c_), me).wait_recv()

        @pl.when((j == 3) & (i == 0))
        def _():
            copy(3, (*near[2], c_), me).wait_recv()
            copy(6, (*near[2], c_), sib).start()
            copy(6, (*near[2], 1 - c_), me).wait_recv()
            keep.start()

        @pl.when(i < n_pad)
        def _():
            z_ref[...] = jnp.zeros(z_ref.shape, BF16)

        @pl.when(i >= n_pad)
        def _():
            rows = pl.ds(pl.multiple_of((i - n_pad) * tm, tm), tm)

            @pl.when(j == 0)
            def _():
                xf = x_ref[...]
                xn = xf * lax.rsqrt(jnp.mean(xf * xf, axis=-1, keepdims=True) + EPS)
                hb[rows, :] = (xn * g_ref[...]).astype(BF16)
                xn_ref[...] = xn.astype(BF16)

            chip_rows = pl.ds(pl.multiple_of(order_ref[j] * chip_w, 16), chip_w)
            blk = _dot(hb[rows, :], wt[chip_rows, :], NT)
            q_scale = jnp.where(order_ref[j] == 0, Q_SCALE, 1.0).astype(F32)
            z_ref[:, :D_A] = (blk[:, :D_A] * q_scale).astype(BF16)
            z_ref[:, D_A:] = blk[:, D_A:].astype(BF16)

        @pl.when((j == n_chips - 1) & (i == n_pad + nt - 1))
        def _():
            keep.wait()
            for cp in sends():
                cp.wait_send()

    pos = _my_pos()
    order = jnp.stack([2 * cx + cy for cx, cy in ((pos[0], pos[1]),) + _other_chips(pos)]).astype(jnp.int32)
    first_pass = lambda j, i: jnp.where(j == 0, jnp.maximum(i - n_pad, 0), nt - 1)
    grid_spec = pltpu.PrefetchScalarGridSpec(
        num_scalar_prefetch=1,
        grid=(n_chips, n_pad + nt),
        in_specs=[pl.BlockSpec((tm, D_MODEL), lambda j, i, o: (first_pass(j, i), 0)),
                  pl.BlockSpec((1, D_MODEL), lambda j, i, o: (0, 0)),
                  pl.BlockSpec(memory_space=pl.ANY)],
        out_specs=(pl.BlockSpec((tm, chip_w), lambda j, i, o: (i, o[j])),
                   pl.BlockSpec((tm, D_MODEL), lambda j, i, o: (first_pass(j, i), 0)),
                   pl.BlockSpec(memory_space=pl.ANY)),
        scratch_shapes=[pltpu.VMEM((D_IN, D_MODEL), BF16),
                        pltpu.VMEM((s, D_MODEL), BF16), pltpu.VMEM(w_in_t.shape, F32),
                        pltpu.SemaphoreType.DMA((N_DEV - 1,)), pltpu.SemaphoreType.DMA((N_DEV - 1,)),
                        pltpu.SemaphoreType.DMA((2,))])
    return pl.pallas_call(
        body, name="gather_proj_fwd",
        grid_spec=grid_spec,
        out_shape=(jax.ShapeDtypeStruct((Z_PAD + s, D_IN), BF16), jax.ShapeDtypeStruct((s, D_MODEL), BF16),
                   jax.ShapeDtypeStruct((D_IN, D_MODEL), BF16)),
        compiler_params=_params(60),
    )(order, x, norm_g, w_in_t)


def _attn_specs(rows):
    pairs = N_HEADS // 2
    return ([pl.BlockSpec((rows, 128), functools.partial(lambda which, p: (0, which * pairs + p), which))
             for which in range(3)]
            + [pl.BlockSpec((2, Q_BLOCK, K_SPAN), lambda p: (p, 0, 0))])


def _head_masks():
    lane = lax.broadcasted_iota(jnp.int32, (1, 128), 1)
    first = lane < HEAD_DIM
    return (first, jnp.logical_not(first))


def _stack_heads(x, masks):
    zero = jnp.zeros((), x.dtype)
    return jnp.concatenate([jnp.where(m, x, zero) for m in masks], axis=0)


STRIP = 16


def _softmax_strips(s_ref, bias_ref, b):
    valid = lax.broadcasted_iota(jnp.int32, (1, K_SPAN), 1) >= Z_PAD - b * Q_BLOCK
    for t in range(2 * Q_BLOCK // STRIP):
        hh, r = divmod(t * STRIP, Q_BLOCK)
        st = s_ref[t * STRIP:(t + 1) * STRIP, :] + bias_ref[hh, r:r + STRIP, :]
        st = jnp.where(valid, st, NEG_INF)
        e = jnp.exp(st - jnp.max(st, axis=-1, keepdims=True))
        yield e * (1.0 / jnp.sum(e, axis=-1, keepdims=True))


def _side_by_side_strips(strips):
    half = len(strips) // 2
    return jnp.concatenate([jnp.concatenate([a, c], axis=1) for a, c in zip(strips[:half], strips[half:])], axis=0)


def _attn_fwd(qkv, bias_table, shards):
    s = qkv.shape[0] - Z_PAD
    nb = s // Q_BLOCK
    n = len(shards)
    pairs = N_HEADS // 2

    def body(*refs):
        q_ref, k_ref, v_ref, bt_ref = refs[:4]
        shard_refs = refs[4:4 + n]
        o_ref = refs[4 + n]
        slot_refs = refs[5 + n:5 + 2 * n]
        stages = refs[5 + 2 * n:5 + 3 * n]
        s_scr, send_sems, recv_sems, local_sems = refs[5 + 3 * n:]
        p_id = pl.program_id(0)
        gather = _SlotGather(slot_refs, send_sems, recv_sems, own=stages)
        keep = [pltpu.make_async_copy(stages[a], slot_refs[a].at[_flat_id(_my_pos())], local_sems.at[a])
                for a in range(n)]

        @pl.when(p_id == 0)
        def _():
            for a in range(n):
                stages[a][...] = shard_refs[a][...].astype(BF16)
                keep[a].start()
            gather.start()

        @pl.when(p_id == 2)
        def _():
            gather.pass_on()

        masks = _head_masks()

        def scores(b, half):
            r0 = pl.multiple_of(b * Q_BLOCK, Q_BLOCK)
            q2 = _stack_heads(q_ref[pl.ds(r0 + Z_PAD, Q_BLOCK), :], masks)
            s_scr[half] = _dot(q2, k_ref[pl.ds(r0, K_SPAN), :], NT)

        def finish(b, half):
            r0 = pl.multiple_of(b * Q_BLOCK, Q_BLOCK)
            v2 = _stack_heads(v_ref[pl.ds(r0, K_SPAN), :], masks)
            p = [st.astype(BF16) for st in _softmax_strips(s_scr.at[half], bt_ref, b)]
            o_ref[pl.ds(r0, Q_BLOCK), :] = _dot(_side_by_side_strips(p), v2)

        def two_blocks(i, carry):
            b = 2 * i
            scores(b + 1, 1)
            finish(b, 0)
            scores(jnp.minimum(b + 2, nb - 1), 0)
            finish(b + 1, 1)
            return carry

        scores(0, 0)
        lax.fori_loop(0, nb // 2, two_blocks, 0)

        @pl.when(p_id == pairs - 1)
        def _():
            gather.finish()
            for cp in keep:
                cp.wait()

    hbm = pl.BlockSpec(memory_space=pl.ANY)
    return pl.pallas_call(
        body, name="attn_fwd",
        grid=(pairs,),
        in_specs=_attn_specs(s + Z_PAD) + [pl.BlockSpec(a.shape, lambda p: (0, 0)) for a in shards],
        out_specs=(pl.BlockSpec((s, 128), lambda p: (0, p)),) + (hbm,) * n,
        out_shape=(jax.ShapeDtypeStruct((s, D_A), F32),)
        + tuple(jax.ShapeDtypeStruct((N_DEV,) + a.shape, BF16) for a in shards),
        scratch_shapes=[pltpu.VMEM(a.shape, BF16) for a in shards]
        + [pltpu.VMEM((2, 2 * Q_BLOCK, K_SPAN), F32),
           pltpu.SemaphoreType.DMA((n, N_DEV - 1)), pltpu.SemaphoreType.DMA((n, N_DEV - 1)),
           pltpu.SemaphoreType.DMA((n,))],
        compiler_params=_params(48),
    )(qkv, qkv, qkv, bias_table, *shards)


def _attn_bwd(qkv, bias_table, d_out, to_chip):
    s = qkv.shape[0] - Z_PAD
    nb = s // Q_BLOCK
    n = len(to_chip)
    pairs = N_HEADS // 2

    def body(*refs):
        q_ref, k_ref, v_ref, bt_ref, do_ref = refs[:5]
        to_chip_refs = refs[5:5 + n]
        dqkv_ref, db_ref = refs[5 + n:7 + n]
        from_chip_refs = refs[7 + n:7 + 2 * n]
        dk_acc, dv_acc, s_scr, dp_scr, send_sems, recv_sems = refs[7 + 2 * n:]
        p_id = pl.program_id(0)

        @pl.when(p_id == 0)
        def _():
            for cp in _owner_copies(to_chip_refs, from_chip_refs, send_sems, recv_sems):
                cp.start()

        dk_acc[...] = jnp.zeros(dk_acc.shape, F32)
        dv_acc[...] = jnp.zeros(dv_acc.shape, F32)
        db_ref[...] = jnp.zeros(db_ref.shape, F32)
        masks = _head_masks()

        def operands(b):
            r0 = pl.multiple_of(b * Q_BLOCK, Q_BLOCK)
            q2 = _stack_heads(q_ref[pl.ds(r0 + Z_PAD, Q_BLOCK), :], masks)
            do2 = _stack_heads(do_ref[pl.ds(r0, Q_BLOCK), :], masks)
            return r0, q2, do2, k_ref[pl.ds(r0, K_SPAN), :]

        def ahead(b, half):
            r0, q2, do2, kcat = operands(b)
            s_scr[half] = _dot(q2, kcat, NT)
            dp_scr[half] = _dot(do2, v_ref[pl.ds(r0, K_SPAN), :], NT)

        def finish(b, half):
            r0, q2, do2, kcat = operands(b)
            p_strips, ds_strips = [], []
            for t, p in enumerate(_softmax_strips(s_scr.at[half], bt_ref, b)):
                hh, r = divmod(t * STRIP, Q_BLOCK)
                dp_t = dp_scr[half, t * STRIP:(t + 1) * STRIP, :]
                ds = p * (dp_t - jnp.sum(p * dp_t, axis=-1, keepdims=True))
                db_ref[hh, r:r + STRIP, :] += ds
                p_strips.append(p.astype(BF16))
                ds_strips.append(ds.astype(BF16))
            dq = _dot(_side_by_side_strips(ds_strips), _stack_heads(kcat, masks))
            dqkv_ref[0, pl.ds(r0, Q_BLOCK), :] = (dq * Q_SCALE).astype(BF16)
            dk_acc[pl.ds(r0, K_SPAN), :] += _dot(jnp.concatenate(ds_strips, axis=0), q2, TN)
            dv_acc[pl.ds(r0, K_SPAN), :] += _dot(jnp.concatenate(p_strips, axis=0), do2, TN)

        def two_blocks(i, carry):
            b = 2 * i
            ahead(b + 1, 1)
            finish(b, 0)
            ahead(jnp.minimum(b + 2, nb - 1), 0)
            finish(b + 1, 1)
            return carry

        ahead(0, 0)
        lax.fori_loop(0, nb // 2, two_blocks, 0)
        dqkv_ref[1] = dk_acc[Z_PAD:, :].astype(BF16)
        dqkv_ref[2] = dv_acc[Z_PAD:, :].astype(BF16)

        @pl.when(p_id == pairs - 1)
        def _():
            for cp in _owner_copies(to_chip_refs, from_chip_refs, send_sems, recv_sems):
                cp.wait_recv()
                cp.wait_send()

    hbm = pl.BlockSpec(memory_space=pl.ANY)
    return pl.pallas_call(
        body, name="attn_bwd",
        grid=(pairs,),
        in_specs=_attn_specs(s + Z_PAD) + [pl.BlockSpec((s, 128), lambda p: (0, p))] + [hbm] * n,
        out_specs=(pl.BlockSpec((3, s, 128), lambda p: (0, 0, p)),
                   pl.BlockSpec((2, Q_BLOCK, K_SPAN), lambda p: (p, 0, 0))) + (hbm,) * n,
        out_shape=(jax.ShapeDtypeStruct((3, s, D_A), BF16),
                   jax.ShapeDtypeStruct((N_HEADS, Q_BLOCK, K_SPAN), F32))
        + tuple(jax.ShapeDtypeStruct(t.shape, t.dtype) for t in to_chip),
        scratch_shapes=[pltpu.VMEM((s + Z_PAD, 128), F32), pltpu.VMEM((s + Z_PAD, 128), F32),
                        pltpu.VMEM((2, 2 * Q_BLOCK, K_SPAN), F32), pltpu.VMEM((2, 2 * Q_BLOCK, K_SPAN), F32),
                        pltpu.SemaphoreType.DMA((n, 3)), pltpu.SemaphoreType.DMA((n, 3))],
        compiler_params=_params(56),
    )(qkv, qkv, qkv, bias_table, d_out, *to_chip)


def _mid_fwd_bwd(x, target, attn_out, z, w_pa, w_pb, w_out, b_gate, ln_g, ln_b, w_s, b_s, final_g):
    s = x.shape[0]
    tm = TOKEN_TILE
    nt = s // tm

    def body(x_ref, t_ref, oa_ref, ga_ref, ub_ref, vb_ref, gb_ref, ta0_ref, ta1_ref, tb0_ref, tb1_ref,
             wpa_hbm, wpb_hbm, wout_hbm, bg_ref, lng_ref, lnb_ref, ws_ref, bs_ref, fg_ref,
             dx2_ref, doa_ref, dz_ref, dwout_hbm, dwpa_hbm, dwpb_hbm, dbg_ref, dfg_ref, dlng_ref, dlnb_ref, dws_ref,
             dbs_ref, loss_ref,
             wpa, wpb, wout, wmix, acc_out, acc_pa, acc_pb, sem):
        i = pl.program_id(0)

        @pl.when(i == 0)
        def _():
            loads = [pltpu.make_async_copy(src, dst, sem.at[n])
                     for n, (src, dst) in enumerate(((wpa_hbm, wpa), (wpb_hbm, wpb), (wout_hbm, wout)))]
            for cp in loads:
                cp.start()
            t_idx = lax.broadcasted_iota(jnp.int32, (SGU_CHUNK, SGU_CHUNK), 0)
            s_idx = lax.broadcasted_iota(jnp.int32, (SGU_CHUNK, SGU_CHUNK), 1)
            for g in range(N_GROUPS):
                wmix[g] = jnp.where(s_idx <= t_idx, ws_ref[g], 0.0).astype(BF16)
            for ref in (acc_out, acc_pa, acc_pb, dbg_ref, dfg_ref, dlng_ref, dlnb_ref, dws_ref, dbs_ref, loss_ref):
                ref[...] = jnp.zeros(ref.shape, F32)
            for cp in loads:
                cp.wait()

        def tile_fwd_bwd(rows):
            g_a = ga_ref[rows, :].astype(F32)
            u_b = ub_ref[rows, :].astype(F32)
            v_b = vb_ref[rows, :].astype(F32)
            g_b = gb_ref[rows, :].astype(F32)
            bg = bg_ref[...]
            sg_a = _sigmoid(g_a)
            silu_a = g_a * sg_a
            o_a = oa_ref[rows, :]
            y_a = (o_a * silu_a).astype(BF16)
            ug, dgelu_u = _gelu_and_grad(u_b)
            vg, dgelu_v = _gelu_and_grad(v_b)
            mu = jnp.mean(vg, axis=-1, keepdims=True)
            vc = vg - mu
            rstd = lax.rsqrt(jnp.mean(vc * vc, axis=-1, keepdims=True) + EPS)
            vhat = vc * rstd
            lng = lng_ref[...]
            vn = (vhat * lng + lnb_ref[...]).astype(BF16)
            sg_b = _sigmoid(g_b)
            silu_b = g_b * sg_b
            subs = [slice(n * SGU_CHUNK, (n + 1) * SGU_CHUNK) for n in range(tm // SGU_CHUNK)]
            mixed = jnp.concatenate([jnp.concatenate(
                [_dot(wmix[g], vn[sub, g * 128:(g + 1) * 128]) + bs_ref[g] for g in range(N_GROUPS)], axis=1)
                for sub in subs], axis=0)
            um = ug * mixed
            y_b = (um * silu_b).astype(BF16)
            gate_a = _sigmoid(jnp.concatenate([ta0_ref[rows, :], ta1_ref[rows, :]], axis=1).astype(F32)
                              + bg[:, :D_MODEL])
            gate_b = _sigmoid(jnp.concatenate([tb0_ref[rows, :], tb1_ref[rows, :]], axis=1).astype(F32)
                              + bg[:, D_MODEL:])
            p_a = _dot(y_a, wpa[...])
            p_b = _dot(y_b, wpb[...])
            merged = (gate_a * p_a + gate_b * p_b).astype(BF16)
            x2 = x_ref[rows, :] + _dot(merged, wout[...])
            r2 = lax.rsqrt(jnp.mean(x2 * x2, axis=-1, keepdims=True) + EPS)
            xh = x2 * r2
            fg = fg_ref[...]
            err = xh * fg - t_ref[rows, :]
            loss_ref[...] += jnp.sum(jnp.sum(err * err, axis=-1, keepdims=True), axis=0, keepdims=True) * (0.5 / D_MODEL)
            dy = err * (1.0 / D_MODEL)
            dfg_ref[...] += jnp.sum(dy * xh, axis=0, keepdims=True)
            gy = dy * fg
            dx2 = r2 * (gy - xh * jnp.mean(gy * xh, axis=-1, keepdims=True))
            dx2_ref[rows, :] = dx2
            dx2b = dx2.astype(BF16)
            dmerged = _dot(dx2b, wout[...], NT)
            acc_out[...] += _dot(merged, dx2b, TN)
            dp_a = dmerged * gate_a
            dp_b = dmerged * gate_b
            dgate_a = dp_a * p_a * (1.0 - gate_a)
            dgate_b = dp_b * p_b * (1.0 - gate_b)
            dbg_ref[:, :D_MODEL] += jnp.sum(dgate_a, axis=0, keepdims=True)
            dbg_ref[:, D_MODEL:] += jnp.sum(dgate_b, axis=0, keepdims=True)
            dz_ref[rows, 2048:3072] = dgate_a.astype(BF16)
            dz_ref[rows, 3072:4096] = dgate_b.astype(BF16)
            dp_ab = dp_a.astype(BF16)
            dp_bb = dp_b.astype(BF16)
            dy_a = _dot(dp_ab, wpa[...], NT)
            dy_b = _dot(dp_bb, wpb[...], NT)
            acc_pa[...] += _dot(y_a, dp_ab, TN)
            acc_pb[...] += _dot(y_b, dp_bb, TN)
            doa_ref[rows, :] = (dy_a * silu_a).astype(BF16)
            dz_ref[rows, 0:512] = (dy_a * o_a * (sg_a * (1.0 + g_a * (1.0 - sg_a)))).astype(BF16)
            dz_ref[rows, 1536:2048] = (dy_b * um * (sg_b * (1.0 + g_b * (1.0 - sg_b)))).astype(BF16)
            dys = dy_b * silu_b
            dz_ref[rows, 512:1024] = (dys * mixed * dgelu_u).astype(BF16)
            dmixed = dys * ug
            dmb = dmixed.astype(BF16)
            dvn_rows = []
            for sub in subs:
                dvn_parts = []
                for g in range(N_GROUPS):
                    cols = slice(g * 128, (g + 1) * 128)
                    dws_ref[g] += _dot(dmb[sub, cols], vn[sub, cols], NT)
                    dbs_ref[g] += jnp.sum(dmixed[sub, cols], axis=-1, keepdims=True)
                    dvn_parts.append(_dot(wmix[g], dmb[sub, cols], TN))
                dvn_rows.append(jnp.concatenate(dvn_parts, axis=1))
            dvn = jnp.concatenate(dvn_rows, axis=0)
            dlng_ref[...] += jnp.sum(dvn * vhat, axis=0, keepdims=True)
            dlnb_ref[...] += jnp.sum(dvn, axis=0, keepdims=True)
            dvh = dvn * lng
            dvg = rstd * (dvh - jnp.mean(dvh, axis=-1, keepdims=True)
                          - vhat * jnp.mean(dvh * vhat, axis=-1, keepdims=True))
            dz_ref[rows, 1024:1536] = (dvg * dgelu_v).astype(BF16)

        tile_fwd_bwd(slice(0, tm))

        @pl.when(i == nt - 1)
        def _():
            t_idx = lax.broadcasted_iota(jnp.int32, (SGU_CHUNK, SGU_CHUNK), 0)
            s_idx = lax.broadcasted_iota(jnp.int32, (SGU_CHUNK, SGU_CHUNK), 1)
            for g in range(N_GROUPS):
                dws_ref[g] = jnp.where(s_idx <= t_idx, dws_ref[g], 0.0)
            stores = [pltpu.make_async_copy(src, dst, sem.at[n])
                      for n, (src, dst) in enumerate(((acc_out, dwout_hbm), (acc_pa, dwpa_hbm), (acc_pb, dwpb_hbm)))]
            for cp in stores:
                cp.start()
            for cp in stores:
                cp.wait()

    tile = lambda w: pl.BlockSpec((tm, w), lambda i: (i, 0))
    whole = lambda shape: pl.BlockSpec(shape, lambda i: (0,) * len(shape))
    hbm = pl.BlockSpec(memory_space=pl.ANY)
    return pl.pallas_call(
        body, name="mid_fwd_bwd",
        grid=(nt,),
        in_specs=[tile(D_MODEL), tile(D_MODEL), tile(D_A)]
        + [pl.BlockSpec((tm, COL_BLOCK), functools.partial(lambda c, i: (i + Z_PAD // tm, c), c))
           for c in range(3, N_COL_BLOCKS)]
        + [hbm, hbm, hbm,
                  whole((1, 2 * D_MODEL)), whole((1, D_B)), whole((1, D_B)),
                  whole((N_GROUPS, SGU_CHUNK, SGU_CHUNK)), whole((N_GROUPS, SGU_CHUNK, 1)), whole((1, D_MODEL))],
        out_specs=(tile(D_MODEL), tile(D_A), tile(REST), hbm, hbm, hbm,
                   whole((1, 2 * D_MODEL)), whole((1, D_MODEL)), whole((1, D_B)), whole((1, D_B)),
                   whole((N_GROUPS, SGU_CHUNK, SGU_CHUNK)), whole((N_GROUPS, SGU_CHUNK, 1)), whole((1, 1))),
        out_shape=(jax.ShapeDtypeStruct((s, D_MODEL), F32), jax.ShapeDtypeStruct((s, D_A), BF16),
                   jax.ShapeDtypeStruct((s, REST), BF16),
                   jax.ShapeDtypeStruct((D_MODEL, D_MODEL), F32), jax.ShapeDtypeStruct((D_A, D_MODEL), F32),
                   jax.ShapeDtypeStruct((D_B, D_MODEL), F32),
                   jax.ShapeDtypeStruct((1, 2 * D_MODEL), F32), jax.ShapeDtypeStruct((1, D_MODEL), F32),
                   jax.ShapeDtypeStruct((1, D_B), F32), jax.ShapeDtypeStruct((1, D_B), F32),
                   jax.ShapeDtypeStruct((N_GROUPS, SGU_CHUNK, SGU_CHUNK), F32),
                   jax.ShapeDtypeStruct((N_GROUPS, SGU_CHUNK, 1), F32), jax.ShapeDtypeStruct((1, 1), F32)),
        scratch_shapes=[pltpu.VMEM((D_A, D_MODEL), BF16), pltpu.VMEM((D_B, D_MODEL), BF16),
                        pltpu.VMEM((D_MODEL, D_MODEL), BF16), pltpu.VMEM((N_GROUPS, SGU_CHUNK, SGU_CHUNK), BF16),
                        pltpu.VMEM((D_MODEL, D_MODEL), F32), pltpu.VMEM((D_A, D_MODEL), F32),
                        pltpu.VMEM((D_B, D_MODEL), F32),
                        pltpu.SemaphoreType.DMA((3,))],
        compiler_params=_params(56),
    )(x, target, attn_out, *([z] * (N_COL_BLOCKS - 3)), w_pa, w_pb, w_out, b_gate, ln_g, ln_b, w_s, b_s, final_g)


def _proj_bwd_x(dqkv, drest, x, dx2, norm_g, w_in_t, to_chip, small):
    s = x.shape[0]
    tm = 512 if s % 512 == 0 else TOKEN_TILE
    nt = s // tm
    ws_shape = (N_GROUPS * SGU_CHUNK, SGU_CHUNK)
    rows = to_chip.shape[1]
    half = D_MODEL // 2
    left, right = slice(0, half), slice(half, D_MODEL)

    def body(dqkv_ref, dr_ref, x_ref, dx2_ref, g_ref, w_hbm, tc_hbm,
             ng_ref, bg_ref, rel_ref, lng_ref, lnb_ref, fg_ref, loss_ref, bs_ref, ws_ref,
             dx_ref, fc_ref, slab_land, ws_land,
             w, tc_ref, slab_stage, ws_stage, via_x, via_y, mine, out_x, out_y, sem, send_sems, recv_sems,
             gather_send, gather_recv, keep_sems):
        i = pl.program_id(0)
        x_, y_, c_ = _my_pos()
        me = _flat_id((x_, y_, c_))
        xn, yn = (1 - x_, y_, c_), (x_, 1 - y_, c_)
        gather = _SlotGather([slab_land, ws_land], gather_send, gather_recv, own=[slab_stage, ws_stage])
        keep = [pltpu.make_async_copy(stage, land.at[me], keep_sems.at[k]) for k, (stage, land) in enumerate(
            ((slab_stage, slab_land), (ws_stage, ws_land)))]

        def copy(k, src, dst, to):
            return pltpu.make_async_remote_copy(src_ref=src, dst_ref=dst, send_sem=send_sems.at[k],
                                                recv_sem=recv_sems.at[k], device_id=to, device_id_type=MESH)

        first = [copy(0, tc_ref.at[0, :, left], fc_ref.at[0, :, left], xn), copy(1, tc_ref.at[2, :, left], via_x, xn),
                 copy(2, tc_ref.at[1, :, right], fc_ref.at[1, :, right], yn), copy(3, tc_ref.at[2, :, right], via_y, yn)]
        second = [copy(4, out_y, fc_ref.at[1, :, left], yn), copy(5, out_x, fc_ref.at[0, :, right], xn)]

        def add_and_send(arrival, landed, own_half, stage, onward):
            load = pltpu.make_async_copy(own_half, mine, sem)
            load.start()
            arrival.wait_recv()
            load.wait()
            stage[...] = (mine[...].astype(F32) + landed[...].astype(F32)).astype(BF16)
            onward.start()

        @pl.when(i == 0)
        def _():
            cp = pltpu.make_async_copy(w_hbm, w, sem)
            cp.start()
            slab_stage[...] = jnp.zeros(slab_stage.shape, F32)
            slab_stage[ROW_NORM_G:ROW_NORM_G + 1, :] = ng_ref[...]
            slab_stage[ROW_B_GATE:ROW_B_GATE + 1, :] = bg_ref[:, :D_MODEL]
            slab_stage[ROW_B_GATE + 1:ROW_B_GATE + 2, :] = bg_ref[:, D_MODEL:]
            slab_stage[ROW_LN_G:ROW_LN_G + 1, :D_B] = lng_ref[...]
            slab_stage[ROW_LN_B:ROW_LN_B + 1, :D_B] = lnb_ref[...]
            slab_stage[ROW_FINAL_G:ROW_FINAL_G + 1, :] = fg_ref[...]
            slab_stage[ROW_LOSS:ROW_LOSS + 1, :1] = loss_ref[...]
            slab_stage[ROW_REL:ROW_REL + N_HEADS, :N_REL_PAD] = rel_ref[...]
            eye = (lax.broadcasted_iota(jnp.int32, (SGU_CHUNK, SGU_CHUNK), 0)
                   == lax.broadcasted_iota(jnp.int32, (SGU_CHUNK, SGU_CHUNK), 1))
            for g in range(N_GROUPS):
                row = jnp.sum(jnp.where(eye, bs_ref[g], 0.0), axis=0, keepdims=True)
                slab_stage[ROW_B_S + g:ROW_B_S + g + 1, :SGU_CHUNK] = row
            ws_stage[...] = ws_ref[...]
            for cp_keep in keep:
                cp_keep.start()
            gather.start()
            stage_in = pltpu.make_async_copy(tc_hbm, tc_ref, keep_sems.at[2])
            stage_in.start()
            stage_in.wait()
            for rc in first:
                rc.start()
            cp.wait()

        @pl.when(i == nt // 2)
        def _():
            gather.pass_on()
            add_and_send(first[1], via_x, tc_ref.at[1, :, left], out_y, second[0])
            add_and_send(first[3], via_y, tc_ref.at[0, :, right], out_x, second[1])

        dh = None
        for c in range(N_COL_BLOCKS):
            dz = dqkv_ref[c] if c < 3 else dr_ref[:, (c - 3) * COL_BLOCK:(c - 2) * COL_BLOCK]
            part = _dot(dz, w[c * COL_BLOCK:(c + 1) * COL_BLOCK, :])
            dh = part if dh is None else dh + part
        xf = x_ref[...]
        r = lax.rsqrt(jnp.mean(xf * xf, axis=-1, keepdims=True) + EPS)
        xn = xf * r
        gh = dh * g_ref[...]
        dx_ref[...] = r * (gh - xn * jnp.mean(gh * xn, axis=-1, keepdims=True)) + dx2_ref[...]

        @pl.when(i == nt - 1)
        def _():
            gather.finish()
            for cp_keep in keep:
                cp_keep.wait()
            for k in (0, 2, 4, 5):
                (first + second)[k].wait_recv()
            for rc in first + second:
                rc.wait_send()

    hbm = pl.BlockSpec(memory_space=pl.ANY)
    whole = lambda a: pl.BlockSpec(a.shape, lambda i: (0,) * a.ndim)
    lands = ((N_DEV, SLAB_ROWS, D_MODEL), (N_DEV,) + ws_shape)
    return pl.pallas_call(
        body, name="proj_bwd_x",
        grid=(nt,),
        in_specs=[pl.BlockSpec((3, tm, D_A), lambda i: (0, i, 0)),
                  pl.BlockSpec((tm, REST), lambda i: (i, 0)),
                  pl.BlockSpec((tm, D_MODEL), lambda i: (i, 0)),
                  pl.BlockSpec((tm, D_MODEL), lambda i: (i, 0)),
                  pl.BlockSpec((1, D_MODEL), lambda i: (0, 0)),
                  hbm, hbm] + [whole(a) for a in small],
        out_specs=(pl.BlockSpec((tm, D_MODEL), lambda i: (i, 0)), hbm, hbm, hbm),
        out_shape=(jax.ShapeDtypeStruct((s, D_MODEL), F32), jax.ShapeDtypeStruct((2, rows, D_MODEL), BF16))
        + tuple(jax.ShapeDtypeStruct(shape, F32) for shape in lands),
        scratch_shapes=[pltpu.VMEM((D_IN, D_MODEL), BF16), pltpu.VMEM(to_chip.shape, BF16)]
        + [pltpu.VMEM(shape[1:], F32) for shape in lands]
        + [pltpu.VMEM((rows, half), BF16)] * 5
        + [pltpu.SemaphoreType.DMA, pltpu.SemaphoreType.DMA((6,)), pltpu.SemaphoreType.DMA((6,)),
           pltpu.SemaphoreType.DMA((2, N_DEV - 1)), pltpu.SemaphoreType.DMA((2, N_DEV - 1)),
           pltpu.SemaphoreType.DMA((3,))],
        compiler_params=_params(56),
    )(dqkv, drest, x, dx2, norm_g, w_in_t, to_chip, *small)


def _proj_bwd_w(xn, dqkv, drest, norm_g, w_in_t):
    s = xn.shape[0]
    tk = min(s, 1024)
    nk = s // tk

    def body(xn_ref, dqkv_ref, dr_ref, g_ref, w_ref, o_ref, dg_ref, acc):
        j = pl.program_id(0)
        i = pl.program_id(1)

        @pl.when((j == 0) & (i == 0))
        def _():
            dg_ref[...] = jnp.zeros(dg_ref.shape, F32)

        @pl.when(i == 0)
        def _():
            acc[...] = jnp.zeros(acc.shape, F32)

        @pl.when(j < 3)
        def _():
            acc[...] += _dot(dqkv_ref[...], xn_ref[...], TN)

        @pl.when(j >= 3)
        def _():
            acc[...] += _dot(dr_ref[...], xn_ref[...], TN)

        @pl.when(i == nk - 1)
        def _():
            m = acc[...]
            o_ref[...] = (m * g_ref[...]).astype(BF16)
            dg_ref[...] += jnp.sum(m * w_ref[...].astype(F32), axis=0, keepdims=True)

    return pl.pallas_call(
        body, name="proj_bwd_w",
        grid=(N_COL_BLOCKS, nk),
        in_specs=[pl.BlockSpec((tk, D_MODEL), lambda j, i: (i, 0)),
                  pl.BlockSpec((None, tk, COL_BLOCK),
                               lambda j, i: (jnp.minimum(j, 2), jnp.where(j < 3, i, nk - 1), 0)),
                  pl.BlockSpec((tk, COL_BLOCK),
                               lambda j, i: (jnp.where(j >= 3, i, 0), jnp.maximum(j - 3, 0))),
                  pl.BlockSpec((1, D_MODEL), lambda j, i: (0, 0)),
                  pl.BlockSpec((COL_BLOCK, D_MODEL), lambda j, i: (j, 0))],
        out_specs=(pl.BlockSpec((COL_BLOCK, D_MODEL), lambda j, i: (j, 0)),
                   pl.BlockSpec((1, D_MODEL), lambda j, i: (0, 0))),
        out_shape=(jax.ShapeDtypeStruct((D_IN, D_MODEL), BF16), jax.ShapeDtypeStruct((1, D_MODEL), F32)),
        scratch_shapes=[pltpu.VMEM((COL_BLOCK, D_MODEL), F32)],
        compiler_params=_params(40),
    )(xn, dqkv, drest, norm_g, w_in_t)


def _adamw_math(w, g, m, v):
    c1 = 1.0 - ADAM_B1 ** ADAM_STEP
    c2 = 1.0 - ADAM_B2 ** ADAM_STEP
    nm = ADAM_B1 * m + (1.0 - ADAM_B1) * g
    nv = ADAM_B2 * v + (1.0 - ADAM_B2) * (g * g)
    return -ADAM_LR * ((nm / c1) / (jnp.sqrt(nv / c2) + ADAM_EPS) + ADAM_WD * w), nm, nv


def _adamw(name, w, g, m, v, from_chip):
    rows, cols = w.shape
    tr = rows if rows * cols <= 512 * 1024 else next(t for t in range(256, 7, -8) if rows % t == 0)

    def body(w_ref, g_ref, m_ref, v_ref, t_ref, g_out, d_ref, nm_ref, nv_ref):
        gg = g_ref[...]
        for j in range(from_chip.shape[0]):
            gg = gg + t_ref[j].astype(F32)
        g_out[...] = gg
        d_ref[...], nm_ref[...], nv_ref[...] = _adamw_math(w_ref[...], gg, m_ref[...], v_ref[...])

    spec = pl.BlockSpec((tr, cols), lambda i: (i, 0))
    shape = jax.ShapeDtypeStruct((rows, cols), F32)
    return pl.pallas_call(
        body, name=name,
        grid=(rows // tr,),
        in_specs=[spec] * 4 + [pl.BlockSpec((from_chip.shape[0], tr, cols), lambda i: (0, i, 0))],
        out_specs=(spec,) * 4, out_shape=(shape,) * 4,
        compiler_params=_params(32),
    )(w, g, m, v, from_chip)


_SMALL = (("norm_g", (1, D_MODEL)), ("b_gate", (1, 2 * D_MODEL)), ("rel_bias", (N_HEADS, N_REL)),
          ("sgu_ln_g", (1, D_B)), ("sgu_ln_b", (1, D_B)), ("w_s", (N_GROUPS * SGU_CHUNK, SGU_CHUNK)),
          ("b_s", (N_GROUPS, SGU_CHUNK)), ("final_g", (1, D_MODEL)))


def _adamw_small(slabs, ws_all, weights, moments_m, moments_v):
    k = len(_SMALL)

    def total(ref):
        acc = ref[0]
        for d in range(1, N_DEV):
            acc = acc + ref[d]
        return acc

    def body(*refs):
        slab_ref, ws_ref = refs[:2]
        w_refs, m_refs, v_refs = refs[2:2 + k], refs[2 + k:2 + 2 * k], refs[2 + 2 * k:2 + 3 * k]
        outs = refs[2 + 3 * k:]
        slab = total(slab_ref)
        grads = {
            "norm_g": slab[ROW_NORM_G:ROW_NORM_G + 1, :],
            "b_gate": jnp.concatenate([slab[ROW_B_GATE:ROW_B_GATE + 1, :], slab[ROW_B_GATE + 1:ROW_B_GATE + 2, :]], axis=1),
            "rel_bias": slab[ROW_REL:ROW_REL + N_HEADS, :N_REL],
            "sgu_ln_g": slab[ROW_LN_G:ROW_LN_G + 1, :D_B],
            "sgu_ln_b": slab[ROW_LN_B:ROW_LN_B + 1, :D_B],
            "w_s": total(ws_ref),
            "b_s": slab[ROW_B_S:ROW_B_S + N_GROUPS, :SGU_CHUNK],
            "final_g": slab[ROW_FINAL_G:ROW_FINAL_G + 1, :],
        }
        for n, (name, _) in enumerate(_SMALL):
            g = grads[name]
            outs[n][...] = g
            outs[k + n][...], outs[2 * k + n][...], outs[3 * k + n][...] = _adamw_math(
                w_refs[n][...], g, m_refs[n][...], v_refs[n][...])
        outs[4 * k][...] = slab[ROW_LOSS:ROW_LOSS + 1, :1]

    vmem = pl.BlockSpec(memory_space=pltpu.VMEM)
    shapes = tuple(jax.ShapeDtypeStruct(shape, F32) for _, shape in _SMALL)
    return pl.pallas_call(
        body, name="adamw_small",
        out_shape=shapes * 4 + (jax.ShapeDtypeStruct((1, 1), F32),),
        in_specs=[vmem] * (2 + 3 * k), out_specs=tuple([vmem] * (4 * k + 1)),
        compiler_params=_params(16),
    )(slabs, ws_all, *weights, *moments_m, *moments_v)


def _pad_rel(a):
    return jnp.pad(a.reshape(N_HEADS, N_REL), ((0, 0), (0, N_REL_PAD - N_REL)))


def kernel(x, norm_g, w_in, b_gate, rel_bias, sgu_ln_g, sgu_ln_b, w_s, b_s, w_pa, w_pb, w_out, final_g, loss_target, m_norm_g, m_w_in, m_b_gate, m_rel_bias, m_sgu_ln_g, m_sgu_ln_b, m_w_s, m_b_s, m_w_pa, m_w_pb, m_w_out, m_final_g, v_norm_g, v_w_in, v_b_gate, v_rel_bias, v_sgu_ln_g, v_sgu_ln_b, v_w_s, v_b_s, v_w_pa, v_w_pb, v_w_out, v_final_g):
    s = x.shape[1]
    xs = x.reshape(s, D_MODEL)
    tgt = loss_target.reshape(s, D_MODEL)

    bias_table = _bias_table(_pad_rel(rel_bias))
    w_in_t = jnp.swapaxes(w_in[0], 0, 1)
    qkv, x_norm, w_in_t_full = _gather_proj_fwd(xs, norm_g, w_in_t)
    attn_out, g_pa, g_pb, g_out = _attn_fwd(qkv, bias_table, (w_pa[0], w_pb[0], w_out[0]))
    w_pa_full = jnp.transpose(g_pa, (1, 0, 2)).reshape(D_A, D_MODEL)
    w_pb_full = jnp.transpose(g_pb, (1, 0, 2)).reshape(D_B, D_MODEL)
    w_out_full = g_out.reshape(D_MODEL, D_MODEL)

    (dx2, d_attn, drest, dw_out, dw_pa, dw_pb, d_bgate, d_fg, d_lng, d_lnb, d_ws, d_bs, loss_part) = _mid_fwd_bwd(
        xs, tgt, attn_out, qkv, w_pa_full, w_pb_full, w_out_full, b_gate, sgu_ln_g, sgu_ln_b, w_s[0],
        b_s.reshape(N_GROUPS, SGU_CHUNK, 1), final_g.reshape(1, D_MODEL))

    own_pa, own_pb, own_out, tc_pa, tc_pb, tc_out = _reduce_chip(
        "reduce_chip_proj", (dw_pa, dw_pb, dw_out), (1, 1, 0))
    dqkv, dbias, fc_pa, fc_pb, fc_out = _attn_bwd(qkv, bias_table, d_attn, (tc_pa, tc_pb, tc_out))
    d_rel = _bias_grad(dbias)
    dw_in_t, d_ng = _proj_bwd_w(x_norm, dqkv, drest, norm_g, w_in_t_full)
    own_in, tc_in = _reduce_chip("reduce_chip_in", (dw_in_t,), (0,))
    grad_x, fc_in, slabs, ws_all = _proj_bwd_x(
        dqkv, drest, xs, dx2, norm_g, w_in_t_full, tc_in,
        (d_ng, d_bgate, d_rel, d_lng, d_lnb, d_fg, loss_part, d_bs, d_ws.reshape(N_GROUPS * SGU_CHUNK, SGU_CHUNK)))
    big = {"w_in": tuple(jnp.swapaxes(t, 0, 1)[None] for t in _adamw(
        "adamw_w_in", w_in_t, own_in, jnp.swapaxes(m_w_in[0], 0, 1), jnp.swapaxes(v_w_in[0], 0, 1), fc_in))}
    for name, w, g, fc, m, v in (("w_pa", w_pa, own_pa, fc_pa, m_w_pa, v_w_pa),
                                 ("w_pb", w_pb, own_pb, fc_pb, m_w_pb, v_w_pb),
                                 ("w_out", w_out, own_out, fc_out, m_w_out, v_w_out)):
        big[name] = tuple(t[None] for t in _adamw("adamw_" + name, w[0], g, m[0], v[0], fc))

    as_2d = lambda leaves: [a.reshape(shape) for a, (_, shape) in zip(leaves, _SMALL)]
    small_out = _adamw_small(
        slabs, ws_all, as_2d((norm_g, b_gate, rel_bias, sgu_ln_g, sgu_ln_b, w_s, b_s, final_g)),
        as_2d((m_norm_g, m_b_gate, m_rel_bias, m_sgu_ln_g, m_sgu_ln_b, m_w_s, m_b_s, m_final_g)),
        as_2d((v_norm_g, v_b_gate, v_rel_bias, v_sgu_ln_g, v_sgu_ln_b, v_w_s, v_b_s, v_final_g)))
    small_index = {name: n for n, (name, _) in enumerate(_SMALL)}

    def leaf(kind, name, like):
        if name in big:
            return big[name][kind]
        return small_out[kind * len(_SMALL) + small_index[name]].reshape(like.shape)

    weights = (("norm_g", norm_g), ("w_in", w_in), ("b_gate", b_gate), ("rel_bias", rel_bias), ("sgu_ln_g", sgu_ln_g),
               ("sgu_ln_b", sgu_ln_b), ("w_s", w_s), ("b_s", b_s), ("w_pa", w_pa), ("w_pb", w_pb), ("w_out", w_out),
               ("final_g", final_g))
    outs = [small_out[-1].reshape(()), grad_x.reshape(x.shape)]
    for kind in range(4):
        outs.extend(leaf(kind, name, like) for name, like in weights)
    return tuple(outs)
```

```python
import functools
import math

import jax
import jax.numpy as jnp
from jax import lax
from jax.experimental import pallas as pl
from jax.experimental.pallas import tpu as pltpu

F32 = jnp.float32
BF16 = jnp.bfloat16
MESH = pl.DeviceIdType.MESH
N_DEV = 8

D_MODEL = 1024
D_A = 512
D_B = 512
D_IN = 5632
N_HEADS = 8
HEAD_DIM = 64
N_PREV = 8
REL_CLIP = 128
N_REL = 2 * REL_CLIP + 1
N_REL_PAD = 384
SGU_CHUNK = 128
N_GROUPS = 4
EPS = 1e-6
NEG_INF = -1e30
Q_SCALE = HEAD_DIM ** -0.5

Q_BLOCK = 256
K_SPAN = 768
Z_PAD = K_SPAN - Q_BLOCK
ROLL_W = 1024
COL_BLOCK = 512
N_COL_BLOCKS = D_IN // COL_BLOCK
REST = D_IN - 3 * D_A
TOKEN_TILE = 256

ADAM_LR = 0.001
ADAM_B1 = 0.9
ADAM_B2 = 0.999
ADAM_EPS = 1e-08
ADAM_WD = 0.01
ADAM_STEP = 10

GELU_C = math.sqrt(2.0 / math.pi)
GELU_A = 0.044715

NT = (((1,), (1,)), ((), ()))
TN = (((0,), (0,)), ((), ()))
HIGHEST = lax.Precision.HIGHEST


def _params(vmem_mb, **kw):
    return pltpu.CompilerParams(vmem_limit_bytes=vmem_mb * 1024 * 1024, **kw)


def _dot(a, b, dims=None):
    if dims is None:
        return jnp.dot(a, b, preferred_element_type=F32)
    return lax.dot_general(a, b, dims, preferred_element_type=F32)


def _sigmoid(x):
    return 0.5 * jnp.tanh(0.5 * x) + 0.5


def _gelu_and_grad(u):
    u2 = u * u
    t = jnp.tanh(GELU_C * (u + GELU_A * u * u2))
    half = 0.5 * (1.0 + t)
    g = u * half
    dg = half + 0.5 * u * (1.0 - t * t) * (GELU_C * (1.0 + 3.0 * GELU_A * u2))
    return g, dg


def _my_pos():
    return lax.axis_index("x"), lax.axis_index("y"), lax.axis_index("c")


def _flat_id(pos):
    return 4 * pos[0] + 2 * pos[1] + pos[2]


def _other_chips(pos):
    x, y, _ = pos
    return ((1 - x, y), (x, 1 - y), (1 - x, 1 - y))


class _SlotGather:
    def __init__(self, bufs, send_sems, recv_sems, own=None):
        self.bufs, self.send_sems, self.recv_sems = bufs, send_sems, recv_sems
        self.own = own if own is not None else [None] * len(bufs)
        x, y, c = _my_pos()
        self.c, self.me, self.sib = c, (x, y, c), (x, y, 1 - c)
        self.chips = _other_chips(self.me)

    def _copy(self, a, k, block, to):
        slot = _flat_id(block)
        src = self.own[a] if (k < 4 and self.own[a] is not None) else self.bufs[a].at[slot]
        return pltpu.make_async_remote_copy(
            src_ref=src, dst_ref=self.bufs[a].at[slot],
            send_sem=self.send_sems.at[a, k], recv_sem=self.recv_sems.at[a, k], device_id=to, device_id_type=MESH)

    def _own_sends(self):
        n = len(self.bufs)
        return ([self._copy(a, 1 + j, self.me, (*chip, self.c)) for j, chip in enumerate(self.chips) for a in range(n)]
                + [self._copy(a, 0, self.me, self.sib) for a in range(n)])

    def _passes(self):
        return [self._copy(a, 4 + j, (*chip, self.c), self.sib)
                for j, chip in enumerate(self.chips) for a in range(len(self.bufs))]

    def start(self):
        for cp in self._own_sends():
            cp.start()

    def pass_on(self):
        for j, chip in enumerate(self.chips):
            for a in range(len(self.bufs)):
                self._copy(a, 1 + j, (*chip, self.c), self.me).wait_recv()
                self._copy(a, 4 + j, (*chip, self.c), self.sib).start()

    def finish(self):
        for a in range(len(self.bufs)):
            self._copy(a, 0, self.sib, self.me).wait_recv()
            for j, chip in enumerate(self.chips):
                self._copy(a, 4 + j, (*chip, 1 - self.c), self.me).wait_recv()
        for cp in self._own_sends() + self._passes():
            cp.wait_send()


def _reduce_chip(name, parts, sharded_dim):
    n = len(parts)
    shapes = []
    for p, dim in zip(parts, sharded_dim):
        shape = list(p.shape)
        shape[dim] //= N_DEV
        shapes.append(tuple(shape))

    def body(*refs):
        full, own, to_chip = refs[:n], refs[n:2 * n], refs[2 * n:3 * n]
        ins, from_sib = refs[3 * n:4 * n], refs[4 * n:5 * n]
        send_sems, recv_sems = refs[5 * n], refs[5 * n + 1]
        x, y, c = _my_pos()
        sib = (x, y, 1 - c)
        chips = ((x, y),) + _other_chips((x, y, c))
        for a in range(n):
            rows, cols = shapes[a]
            for d in range(N_DEV):
                if sharded_dim[a] == 0:
                    ins[a][d] = full[a][d * rows:(d + 1) * rows, :].astype(BF16)
                else:
                    ins[a][d] = full[a][:, d * cols:(d + 1) * cols].astype(BF16)

        def to_sibling(a, r):
            return pltpu.make_async_remote_copy(
                src_ref=ins[a].at[_flat_id((*chips[r], 1 - c))], dst_ref=from_sib[a].at[r],
                send_sem=send_sems.at[a, r], recv_sem=recv_sems.at[a, r], device_id=sib, device_id_type=MESH)

        sends = [to_sibling(a, r) for r in (1, 2, 3, 0) for a in range(n)]
        for cp in sends:
            cp.start()
        for r in (1, 2, 3, 0):
            for a in range(n):
                to_sibling(a, r).wait_recv()
                both = ins[a][_flat_id((*chips[r], c))].astype(F32) + from_sib[a][r].astype(F32)
                if r == 0:
                    own[a][...] = both
                else:
                    to_chip[a][r - 1] = both.astype(BF16)
        for cp in sends:
            cp.wait_send()

    vmem = pl.BlockSpec(memory_space=pltpu.VMEM)
    return pl.pallas_call(
        body, name=name,
        out_shape=tuple(jax.ShapeDtypeStruct(sh, F32) for sh in shapes)
        + tuple(jax.ShapeDtypeStruct((3,) + sh, BF16) for sh in shapes),
        in_specs=[vmem] * n, out_specs=tuple([vmem] * (2 * n)),
        scratch_shapes=[pltpu.VMEM((N_DEV,) + sh, BF16) for sh in shapes]
        + [pltpu.VMEM((4,) + sh, BF16) for sh in shapes]
        + [pltpu.SemaphoreType.DMA((n, 4)), pltpu.SemaphoreType.DMA((n, 4))],
        compiler_params=_params(56),
    )(*parts)


def _owner_copies(to_chip, from_chip, send_sems, recv_sems):
    x, y, c = _my_pos()
    return [pltpu.make_async_remote_copy(
        src_ref=to_chip[a].at[j], dst_ref=from_chip[a].at[j],
        send_sem=send_sems.at[a, j], recv_sem=recv_sems.at[a, j], device_id=(*chip, c), device_id_type=MESH)
        for a in range(len(to_chip)) for j, chip in enumerate(_other_chips((x, y, c)))]


ROW_B_GATE, ROW_LN_G, ROW_LN_B, ROW_FINAL_G, ROW_LOSS, ROW_B_S, SLAB_ROWS = 1, 3, 4, 5, 6, 16, 24
ROW_NORM_G, ROW_REL, LATE_SLAB_ROWS = 0, 8, 16


def _rel_index(e):
    lo, hi = Z_PAD - REL_CLIP, Z_PAD + REL_CLIP
    return jnp.where(e <= lo, 2 * REL_CLIP, jnp.where(e < hi, hi - e, jnp.where(e <= K_SPAN, 0, 2 * REL_CLIP)))


def _bias_table(rel_bias_pad):
    def body(rb_ref, bt_ref):
        c = lax.broadcasted_iota(jnp.int32, (N_REL_PAD, ROLL_W), 1)
        r = lax.broadcasted_iota(jnp.int32, (N_REL_PAD, ROLL_W), 0)
        pick = (r == _rel_index(c)).astype(F32)
        rows = jnp.dot(rb_ref[...], pick, precision=HIGHEST, preferred_element_type=F32)
        qc = lax.broadcasted_iota(jnp.int32, (Q_BLOCK, K_SPAN), 0) >> 6
        kc = lax.broadcasted_iota(jnp.int32, (Q_BLOCK, K_SPAN), 1) >> 6
        band = (kc >= qc) & (kc <= qc + N_PREV)
        for h in range(N_HEADS):
            t = jnp.broadcast_to(rows[h:h + 1, :], (Q_BLOCK, ROLL_W))
            t = pltpu.roll(t, 0, 1, stride=1, stride_axis=0)
            bt_ref[h] = jnp.where(band, t[:, :K_SPAN], NEG_INF)

    return pl.pallas_call(
        body, name="bias_table",
        out_shape=jax.ShapeDtypeStruct((N_HEADS, Q_BLOCK, K_SPAN), F32),
        compiler_params=_params(32),
    )(rel_bias_pad)


def _bias_grad(dbias):
    def body(a_ref, o_ref):
        rr = lax.broadcasted_iota(jnp.int32, (Q_BLOCK, Q_BLOCK), 0)
        cc = lax.broadcasted_iota(jnp.int32, (Q_BLOCK, Q_BLOCK), 1)
        flip = (rr + cc == Q_BLOCK - 1).astype(F32)
        c = lax.broadcasted_iota(jnp.int32, (ROLL_W, N_REL_PAD), 0)
        r = lax.broadcasted_iota(jnp.int32, (ROLL_W, N_REL_PAD), 1)
        e = jnp.where(c >= Q_BLOCK - 1, c - (Q_BLOCK - 1), c + (ROLL_W - Q_BLOCK + 1))
        pick = (r == _rel_index(e)).astype(F32)
        sums = []
        for h in range(N_HEADS):
            a = jnp.dot(flip, a_ref[h], precision=HIGHEST, preferred_element_type=F32)
            a = jnp.concatenate([a, jnp.zeros((Q_BLOCK, ROLL_W - K_SPAN), F32)], axis=1)
            a = pltpu.roll(a, 0, 1, stride=1, stride_axis=0)
            sums.append(jnp.sum(a, axis=0, keepdims=True))
        diag = jnp.concatenate(sums, axis=0)
        o_ref[...] = jnp.dot(diag, pick, precision=HIGHEST, preferred_element_type=F32)

    return pl.pallas_call(
        body, name="bias_grad",
        out_shape=jax.ShapeDtypeStruct((N_HEADS, N_REL_PAD), F32),
        compiler_params=_params(32),
    )(dbias)


def _gather_proj_fwd(x, norm_g, w_in_t):
    s = x.shape[0]
    tm = 512 if s % 512 == 0 else TOKEN_TILE
    nt = s // tm
    n_pad = Z_PAD // tm
    shard_w = w_in_t.shape[0]
    chip_w = 2 * shard_w
    n_chips = N_DEV // 2

    def body(order_ref, x_ref, g_ref, win_hbm, z_ref, xn_ref, wt_hbm, wt, hb, win_f32, send_sems, recv_sems,
             local_sems):
        j = pl.program_id(0)
        i = pl.program_id(1)
        x_, y_, c_ = _my_pos()
        me, sib = (x_, y_, c_), (x_, y_, 1 - c_)
        near = _other_chips(me)
        pick = lambda a, b: tuple(jnp.where(c_ == 0, u, v) for u, v in zip(a, b))
        passed_from, passed_to = pick(near[0], near[1]), pick(near[1], near[0])

        def rows_of(block):
            return wt.at[pl.ds(pl.multiple_of(_flat_id(block) * shard_w, 16), shard_w), :]

        def copy(k, block, to):
            return pltpu.make_async_remote_copy(
                src_ref=rows_of(block), dst_ref=rows_of(block),
                send_sem=send_sems.at[k], recv_sem=recv_sems.at[k], device_id=to, device_id_type=MESH)

        def sends():
            return ([copy(0, me, sib), copy(1, me, (*near[0], c_)), copy(2, me, (*near[1], c_)),
                     copy(3, (*passed_from, c_), (*passed_to, c_))]
                    + [copy(4 + n, (*near[n], c_), sib) for n in range(3)])

        keep = pltpu.make_async_copy(wt, wt_hbm, local_sems.at[0])

        @pl.when((j == 0) & (i == 0))
        def _():
            load = pltpu.make_async_copy(win_hbm, win_f32, local_sems.at[1])
            load.start()
            load.wait()
            rows_of(me)[...] = win_f32[...].astype(BF16)
            for cp in sends()[:3]:
                cp.start()
            copy(0, sib, me).wait_recv()

        @pl.when((j == 1) & (i == 0))
        def _():
            copy(1, (*near[0], c_), me).wait_recv()
            copy(2, (*near[1], c_), me).wait_recv()
            for cp in sends()[3:6]:
                cp.start()
            copy(4, (*near[0], 1 - c_), me).wait_recv()

        @pl.when((j == 2) & (i == 0))
        def _():
            copy(5, (*near[1], 1 - c_), me).wait_recv()

        @pl.when((j == 3) & (i == 0))
        def _():
            copy(3, (*near[2], c_), me).wait_recv()
            copy(6, (*near[2], c_), sib).start()
            copy(6, (*near[2], 1 - c_), me).wait_recv()
            keep.start()

        @pl.when(i < n_pad)
        def _():
            z_ref[...] = jnp.zeros(z_ref.shape, BF16)

        @pl.when(i >= n_pad)
        def _():
            rows = pl.ds(pl.multiple_of((i - n_pad) * tm, tm), tm)

            @pl.when(j == 0)
            def _():
                xf = x_ref[...]
                xn = xf * lax.rsqrt(jnp.mean(xf * xf, axis=-1, keepdims=True) + EPS)
                hb[rows, :] = (xn * g_ref[...]).astype(BF16)
                xn_ref[...] = xn.astype(BF16)

            chip_rows = pl.ds(pl.multiple_of(order_ref[j] * chip_w, 16), chip_w)
            blk = _dot(hb[rows, :], wt[chip_rows, :], NT)
            q_scale = jnp.where(order_ref[j] == 0, Q_SCALE, 1.0).astype(F32)
            z_ref[:, :D_A] = (blk[:, :D_A] * q_scale).astype(BF16)
            z_ref[:, D_A:] = blk[:, D_A:].astype(BF16)

        @pl.when((j == n_chips - 1) & (i == n_pad + nt - 1))
        def _():
            keep.wait()
            for cp in sends():
                cp.wait_send()

    pos = _my_pos()
    order = jnp.stack([2 * cx + cy for cx, cy in ((pos[0], pos[1]),) + _other_chips(pos)]).astype(jnp.int32)
    first_pass = lambda j, i: jnp.where(j == 0, jnp.maximum(i - n_pad, 0), nt - 1)
    grid_spec = pltpu.PrefetchScalarGridSpec(
        num_scalar_prefetch=1,
        grid=(n_chips, n_pad + nt),
        in_specs=[pl.BlockSpec((tm, D_MODEL), lambda j, i, o: (first_pass(j, i), 0)),
                  pl.BlockSpec((1, D_MODEL), lambda j, i, o: (0, 0)),
                  pl.BlockSpec(memory_space=pl.ANY)],
        out_specs=(pl.BlockSpec((tm, chip_w), lambda j, i, o: (i, o[j])),
                   pl.BlockSpec((tm, D_MODEL), lambda j, i, o: (first_pass(j, i), 0)),
                   pl.BlockSpec(memory_space=pl.ANY)),
        scratch_shapes=[pltpu.VMEM((D_IN, D_MODEL), BF16),
                        pltpu.VMEM((s, D_MODEL), BF16), pltpu.VMEM(w_in_t.shape, F32),
                        pltpu.SemaphoreType.DMA((N_DEV - 1,)), pltpu.SemaphoreType.DMA((N_DEV - 1,)),
                        pltpu.SemaphoreType.DMA((2,))])
    return pl.pallas_call(
        body, name="gather_proj_fwd",
        grid_spec=grid_spec,
        out_shape=(jax.ShapeDtypeStruct((Z_PAD + s, D_IN), BF16), jax.ShapeDtypeStruct((s, D_MODEL), BF16),
                   jax.ShapeDtypeStruct((D_IN, D_MODEL), BF16)),
        compiler_params=_params(60),
    )(order, x, norm_g, w_in_t)


def _attn_specs(rows):
    pairs = N_HEADS // 2
    return ([pl.BlockSpec((rows, 128), functools.partial(lambda which, p: (0, which * pairs + p), which))
             for which in range(3)]
            + [pl.BlockSpec((2, Q_BLOCK, K_SPAN), lambda p: (p, 0, 0))])


def _head_masks():
    lane = lax.broadcasted_iota(jnp.int32, (1, 128), 1)
    first = lane < HEAD_DIM
    return (first, jnp.logical_not(first))


def _stack_heads(x, masks):
    zero = jnp.zeros((), x.dtype)
    return jnp.concatenate([jnp.where(m, x, zero) for m in masks], axis=0)


STRIP = 16


def _softmax_strips(s_ref, bias_ref, b):
    valid = lax.broadcasted_iota(jnp.int32, (1, K_SPAN), 1) >= Z_PAD - b * Q_BLOCK
    for t in range(2 * Q_BLOCK // STRIP):
        hh, r = divmod(t * STRIP, Q_BLOCK)
        st = s_ref[t * STRIP:(t + 1) * STRIP, :] + bias_ref[hh, r:r + STRIP, :]
        st = jnp.where(valid, st, NEG_INF)
        e = jnp.exp(st - jnp.max(st, axis=-1, keepdims=True))
        yield e * (1.0 / jnp.sum(e, axis=-1, keepdims=True))


def _side_by_side_strips(strips):
    half = len(strips) // 2
    return jnp.concatenate([jnp.concatenate([a, c], axis=1) for a, c in zip(strips[:half], strips[half:])], axis=0)


def _attn_fwd(qkv, bias_table, shards):
    s = qkv.shape[0] - Z_PAD
    nb = s // Q_BLOCK
    n = len(shards)
    pairs = N_HEADS // 2

    def body(*refs):
        q_ref, k_ref, v_ref, bt_ref = refs[:4]
        shard_refs = refs[4:4 + n]
        o_ref = refs[4 + n]
        slot_refs = refs[5 + n:5 + 2 * n]
        stages = refs[5 + 2 * n:5 + 3 * n]
        s_scr, send_sems, recv_sems, local_sems = refs[5 + 3 * n:]
        p_id = pl.program_id(0)
        gather = _SlotGather(slot_refs, send_sems, recv_sems, own=stages)
        keep = [pltpu.make_async_copy(stages[a], slot_refs[a].at[_flat_id(_my_pos())], local_sems.at[a])
                for a in range(n)]

        @pl.when(p_id == 0)
        def _():
            for a in range(n):
                stages[a][...] = shard_refs[a][...].astype(BF16)
                keep[a].start()
            gather.start()

        @pl.when(p_id == 2)
        def _():
            gather.pass_on()

        masks = _head_masks()

        def scores(b, half):
            r0 = pl.multiple_of(b * Q_BLOCK, Q_BLOCK)
            q2 = _stack_heads(q_ref[pl.ds(r0 + Z_PAD, Q_BLOCK), :], masks)
            s_scr[half] = _dot(q2, k_ref[pl.ds(r0, K_SPAN), :], NT)

        def finish(b, half):
            r0 = pl.multiple_of(b * Q_BLOCK, Q_BLOCK)
            v2 = _stack_heads(v_ref[pl.ds(r0, K_SPAN), :], masks)
            p = [st.astype(BF16) for st in _softmax_strips(s_scr.at[half], bt_ref, b)]
            o_ref[pl.ds(r0, Q_BLOCK), :] = _dot(_side_by_side_strips(p), v2)

        def two_blocks(i, carry):
            b = 2 * i
            scores(b + 1, 1)
            finish(b, 0)
            scores(jnp.minimum(b + 2, nb - 1), 0)
            finish(b + 1, 1)
            return carry

        scores(0, 0)
        lax.fori_loop(0, nb // 2, two_blocks, 0)

        @pl.when(p_id == pairs - 1)
        def _():
            gather.finish()
            for cp in keep:
                cp.wait()

    hbm = pl.BlockSpec(memory_space=pl.ANY)
    return pl.pallas_call(
        body, name="attn_fwd",
        grid=(pairs,),
        in_specs=_attn_specs(s + Z_PAD) + [pl.BlockSpec(a.shape, lambda p: (0, 0)) for a in shards],
        out_specs=(pl.BlockSpec((s, 128), lambda p: (0, p)),) + (hbm,) * n,
        out_shape=(jax.ShapeDtypeStruct((s, D_A), F32),)
        + tuple(jax.ShapeDtypeStruct((N_DEV,) + a.shape, BF16) for a in shards),
        scratch_shapes=[pltpu.VMEM(a.shape, BF16) for a in shards]
        + [pltpu.VMEM((2, 2 * Q_BLOCK, K_SPAN), F32),
           pltpu.SemaphoreType.DMA((n, N_DEV - 1)), pltpu.SemaphoreType.DMA((n, N_DEV - 1)),
           pltpu.SemaphoreType.DMA((n,))],
        compiler_params=_params(48),
    )(qkv, qkv, qkv, bias_table, *shards)


def _fill_slab(stage, rows):
    stage[...] = jnp.zeros(stage.shape, F32)
    for row, ref in rows:
        r, c = ref.shape
        if c > D_MODEL:
            for part in range(c // D_MODEL):
                stage[row + part:row + part + 1, :] = ref[:, part * D_MODEL:(part + 1) * D_MODEL]
        else:
            stage[row:row + r, :c] = ref[...]


def _attn_bwd(qkv, bias_table, d_out, to_chip, small):
    s = qkv.shape[0] - Z_PAD
    nb = s // Q_BLOCK
    n = len(to_chip)
    pairs = N_HEADS // 2
    ws_shape = small[-1].shape

    def body(*refs):
        q_ref, k_ref, v_ref, bt_ref, do_ref = refs[:5]
        to_chip_refs = refs[5:5 + n]
        bg_ref, lng_ref, lnb_ref, fg_ref, loss_ref, bs_ref, ws_ref = refs[5 + n:12 + n]
        dqkv_ref, db_ref = refs[12 + n:14 + n]
        from_chip_refs = refs[14 + n:14 + 2 * n]
        slab_land, ws_land = refs[14 + 2 * n:16 + 2 * n]
        (dk_acc, dv_acc, s_scr, dp_scr, slab_stage, ws_stage, send_sems, recv_sems, gather_send, gather_recv,
         keep_sems) = refs[16 + 2 * n:]
        p_id = pl.program_id(0)
        me = _flat_id(_my_pos())
        gather = _SlotGather([slab_land, ws_land], gather_send, gather_recv, own=[slab_stage, ws_stage])
        keep = [pltpu.make_async_copy(stage, land.at[me], keep_sems.at[k]) for k, (stage, land) in enumerate(
            ((slab_stage, slab_land), (ws_stage, ws_land)))]

        @pl.when(p_id == 0)
        def _():
            _fill_slab(slab_stage, ((ROW_B_GATE, bg_ref), (ROW_LN_G, lng_ref), (ROW_LN_B, lnb_ref),
                                    (ROW_FINAL_G, fg_ref), (ROW_LOSS, loss_ref)))
            eye = (lax.broadcasted_iota(jnp.int32, (SGU_CHUNK, SGU_CHUNK), 0)
                   == lax.broadcasted_iota(jnp.int32, (SGU_CHUNK, SGU_CHUNK), 1))
            for g in range(N_GROUPS):
                row = jnp.sum(jnp.where(eye, bs_ref[g], 0.0), axis=0, keepdims=True)
                slab_stage[ROW_B_S + g:ROW_B_S + g + 1, :SGU_CHUNK] = row
            ws_stage[...] = ws_ref[...]
            for cp in keep:
                cp.start()
            gather.start()
            for cp in _owner_copies(to_chip_refs, from_chip_refs, send_sems, recv_sems):
                cp.start()

        @pl.when(p_id == 2)
        def _():
            gather.pass_on()

        dk_acc[...] = jnp.zeros(dk_acc.shape, F32)
        dv_acc[...] = jnp.zeros(dv_acc.shape, F32)
        db_ref[...] = jnp.zeros(db_ref.shape, F32)
        masks = _head_masks()

        def operands(b):
            r0 = pl.multiple_of(b * Q_BLOCK, Q_BLOCK)
            q2 = _stack_heads(q_ref[pl.ds(r0 + Z_PAD, Q_BLOCK), :], masks)
            do2 = _stack_heads(do_ref[pl.ds(r0, Q_BLOCK), :], masks)
            return r0, q2, do2, k_ref[pl.ds(r0, K_SPAN), :]

        def ahead(b, half):
            r0, q2, do2, kcat = operands(b)
            s_scr[half] = _dot(q2, kcat, NT)
            dp_scr[half] = _dot(do2, v_ref[pl.ds(r0, K_SPAN), :], NT)

        def finish(b, half):
            r0, q2, do2, kcat = operands(b)
            p_strips, ds_strips = [], []
            for t, p in enumerate(_softmax_strips(s_scr.at[half], bt_ref, b)):
                hh, r = divmod(t * STRIP, Q_BLOCK)
                dp_t = dp_scr[half, t * STRIP:(t + 1) * STRIP, :]
                ds = p * (dp_t - jnp.sum(p * dp_t, axis=-1, keepdims=True))
                db_ref[hh, r:r + STRIP, :] += ds
                p_strips.append(p.astype(BF16))
                ds_strips.append(ds.astype(BF16))
            dq = _dot(_side_by_side_strips(ds_strips), _stack_heads(kcat, masks))
            dqkv_ref[0, pl.ds(r0, Q_BLOCK), :] = (dq * Q_SCALE).astype(BF16)
            dk_acc[pl.ds(r0, K_SPAN), :] += _dot(jnp.concatenate(ds_strips, axis=0), q2, TN)
            dv_acc[pl.ds(r0, K_SPAN), :] += _dot(jnp.concatenate(p_strips, axis=0), do2, TN)

        def two_blocks(i, carry):
            b = 2 * i
            ahead(b + 1, 1)
            finish(b, 0)
            ahead(jnp.minimum(b + 2, nb - 1), 0)
            finish(b + 1, 1)
            return carry

        ahead(0, 0)
        lax.fori_loop(0, nb // 2, two_blocks, 0)
        dqkv_ref[1] = dk_acc[Z_PAD:, :].astype(BF16)
        dqkv_ref[2] = dv_acc[Z_PAD:, :].astype(BF16)

        @pl.when(p_id == pairs - 1)
        def _():
            gather.finish()
            for cp in keep:
                cp.wait()
            for cp in _owner_copies(to_chip_refs, from_chip_refs, send_sems, recv_sems):
                cp.wait_recv()
                cp.wait_send()

    hbm = pl.BlockSpec(memory_space=pl.ANY)
    lands = ((N_DEV, SLAB_ROWS, D_MODEL), (N_DEV,) + ws_shape)
    return pl.pallas_call(
        body, name="attn_bwd",
        grid=(pairs,),
        in_specs=_attn_specs(s + Z_PAD) + [pl.BlockSpec((s, 128), lambda p: (0, p))] + [hbm] * n
        + [pl.BlockSpec(a.shape, functools.partial(lambda nd, p: (0,) * nd, a.ndim)) for a in small],
        out_specs=(pl.BlockSpec((3, s, 128), lambda p: (0, 0, p)),
                   pl.BlockSpec((2, Q_BLOCK, K_SPAN), lambda p: (p, 0, 0))) + (hbm,) * (n + 2),
        out_shape=(jax.ShapeDtypeStruct((3, s, D_A), BF16),
                   jax.ShapeDtypeStruct((N_HEADS, Q_BLOCK, K_SPAN), F32))
        + tuple(jax.ShapeDtypeStruct(t.shape, t.dtype) for t in to_chip)
        + tuple(jax.ShapeDtypeStruct(shape, F32) for shape in lands),
        scratch_shapes=[pltpu.VMEM((s + Z_PAD, 128), F32), pltpu.VMEM((s + Z_PAD, 128), F32),
                        pltpu.VMEM((2, 2 * Q_BLOCK, K_SPAN), F32), pltpu.VMEM((2, 2 * Q_BLOCK, K_SPAN), F32)]
        + [pltpu.VMEM(shape[1:], F32) for shape in lands]
        + [pltpu.SemaphoreType.DMA((n, 3)), pltpu.SemaphoreType.DMA((n, 3)),
           pltpu.SemaphoreType.DMA((2, N_DEV - 1)), pltpu.SemaphoreType.DMA((2, N_DEV - 1)),
           pltpu.SemaphoreType.DMA((2,))],
        compiler_params=_params(56),
    )(qkv, qkv, qkv, bias_table, d_out, *to_chip, *small)


def _mid_fwd_bwd(x, target, attn_out, z, w_pa, w_pb, w_out, b_gate, ln_g, ln_b, w_s, b_s, final_g):
    s = x.shape[0]
    tm = TOKEN_TILE
    nt = s // tm

    def body(x_ref, t_ref, oa_ref, ga_ref, ub_ref, vb_ref, gb_ref, ta0_ref, ta1_ref, tb0_ref, tb1_ref,
             wpa_hbm, wpb_hbm, wout_hbm, bg_ref, lng_ref, lnb_ref, ws_ref, bs_ref, fg_ref,
             dx2_ref, doa_ref, dz_ref, dwout_hbm, dwpa_hbm, dwpb_hbm, dbg_ref, dfg_ref, dlng_ref, dlnb_ref, dws_ref,
             dbs_ref, loss_ref,
             wpa, wpb, wout, wmix, acc_out, acc_pa, acc_pb, sem):
        i = pl.program_id(0)

        @pl.when(i == 0)
        def _():
            loads = [pltpu.make_async_copy(src, dst, sem.at[n])
                     for n, (src, dst) in enumerate(((wpa_hbm, wpa), (wpb_hbm, wpb), (wout_hbm, wout)))]
            for cp in loads:
                cp.start()
            t_idx = lax.broadcasted_iota(jnp.int32, (SGU_CHUNK, SGU_CHUNK), 0)
            s_idx = lax.broadcasted_iota(jnp.int32, (SGU_CHUNK, SGU_CHUNK), 1)
            for g in range(N_GROUPS):
                wmix[g] = jnp.where(s_idx <= t_idx, ws_ref[g], 0.0).astype(BF16)
            for ref in (acc_out, acc_pa, acc_pb, dbg_ref, dfg_ref, dlng_ref, dlnb_ref, dws_ref, dbs_ref, loss_ref):
                ref[...] = jnp.zeros(ref.shape, F32)
            for cp in loads:
                cp.wait()

        def tile_fwd_bwd(rows):
            g_a = ga_ref[rows, :].astype(F32)
            u_b = ub_ref[rows, :].astype(F32)
            v_b = vb_ref[rows, :].astype(F32)
            g_b = gb_ref[rows, :].astype(F32)
            bg = bg_ref[...]
            sg_a = _sigmoid(g_a)
            silu_a = g_a * sg_a
            o_a = oa_ref[rows, :]
            y_a = (o_a * silu_a).astype(BF16)
            ug, dgelu_u = _gelu_and_grad(u_b)
            vg, dgelu_v = _gelu_and_grad(v_b)
            mu = jnp.mean(vg, axis=-1, keepdims=True)
            vc = vg - mu
            rstd = lax.rsqrt(jnp.mean(vc * vc, axis=-1, keepdims=True) + EPS)
            vhat = vc * rstd
            lng = lng_ref[...]
            vn = (vhat * lng + lnb_ref[...]).astype(BF16)
            sg_b = _sigmoid(g_b)
            silu_b = g_b * sg_b
            subs = [slice(n * SGU_CHUNK, (n + 1) * SGU_CHUNK) for n in range(tm // SGU_CHUNK)]
            mixed = jnp.concatenate([jnp.concatenate(
                [_dot(wmix[g], vn[sub, g * 128:(g + 1) * 128]) + bs_ref[g] for g in range(N_GROUPS)], axis=1)
                for sub in subs], axis=0)
            um = ug * mixed
            y_b = (um * silu_b).astype(BF16)
            gate_a = _sigmoid(jnp.concatenate([ta0_ref[rows, :], ta1_ref[rows, :]], axis=1).astype(F32)
                              + bg[:, :D_MODEL])
            gate_b = _sigmoid(jnp.concatenate([tb0_ref[rows, :], tb1_ref[rows, :]], axis=1).astype(F32)
                              + bg[:, D_MODEL:])
            p_a = _dot(y_a, wpa[...])
            p_b = _dot(y_b, wpb[...])
            merged = (gate_a * p_a + gate_b * p_b).astype(BF16)
            x2 = x_ref[rows, :] + _dot(merged, wout[...])
            r2 = lax.rsqrt(jnp.mean(x2 * x2, axis=-1, keepdims=True) + EPS)
            xh = x2 * r2
            fg = fg_ref[...]
            err = xh * fg - t_ref[rows, :]
            loss_ref[...] += jnp.sum(jnp.sum(err * err, axis=-1, keepdims=True), axis=0, keepdims=True) * (0.5 / D_MODEL)
            dy = err * (1.0 / D_MODEL)
            dfg_ref[...] += jnp.sum(dy * xh, axis=0, keepdims=True)
            gy = dy * fg
            dx2 = r2 * (gy - xh * jnp.mean(gy * xh, axis=-1, keepdims=True))
            dx2_ref[rows, :] = dx2
            dx2b = dx2.astype(BF16)
            dmerged = _dot(dx2b, wout[...], NT)
            acc_out[...] += _dot(merged, dx2b, TN)
            dp_a = dmerged * gate_a
            dp_b = dmerged * gate_b
            dgate_a = dp_a * p_a * (1.0 - gate_a)
            dgate_b = dp_b * p_b * (1.0 - gate_b)
            dbg_ref[:, :D_MODEL] += jnp.sum(dgate_a, axis=0, keepdims=True)
            dbg_ref[:, D_MODEL:] += jnp.sum(dgate_b, axis=0, keepdims=True)
            dz_ref[rows, 2048:3072] = dgate_a.astype(BF16)
            dz_ref[rows, 3072:4096] = dgate_b.astype(BF16)
            dp_ab = dp_a.astype(BF16)
            dp_bb = dp_b.astype(BF16)
            dy_a = _dot(dp_ab, wpa[...], NT)
            dy_b = _dot(dp_bb, wpb[...], NT)
            acc_pa[...] += _dot(y_a, dp_ab, TN)
            acc_pb[...] += _dot(y_b, dp_bb, TN)
            doa_ref[rows, :] = (dy_a * silu_a).astype(BF16)
            dz_ref[rows, 0:512] = (dy_a * o_a * (sg_a * (1.0 + g_a * (1.0 - sg_a)))).astype(BF16)
            dz_ref[rows, 1536:2048] = (dy_b * um * (sg_b * (1.0 + g_b * (1.0 - sg_b)))).astype(BF16)
            dys = dy_b * silu_b
            dz_ref[rows, 512:1024] = (dys * mixed * dgelu_u).astype(BF16)
            dmixed = dys * ug
            dmb = dmixed.astype(BF16)
            dvn_rows = []
            for sub in subs:
                dvn_parts = []
                for g in range(N_GROUPS):
                    cols = slice(g * 128, (g + 1) * 128)
                    dws_ref[g] += _dot(dmb[sub, cols], vn[sub, cols], NT)
                    dbs_ref[g] += jnp.sum(dmixed[sub, cols], axis=-1, keepdims=True)
                    dvn_parts.append(_dot(wmix[g], dmb[sub, cols], TN))
                dvn_rows.append(jnp.concatenate(dvn_parts, axis=1))
            dvn = jnp.concatenate(dvn_rows, axis=0)
            dlng_ref[...] += jnp.sum(dvn * vhat, axis=0, keepdims=True)
            dlnb_ref[...] += jnp.sum(dvn, axis=0, keepdims=True)
            dvh = dvn * lng
            dvg = rstd * (dvh - jnp.mean(dvh, axis=-1, keepdims=True)
                          - vhat * jnp.mean(dvh * vhat, axis=-1, keepdims=True))
            dz_ref[rows, 1024:1536] = (dvg * dgelu_v).astype(BF16)

        tile_fwd_bwd(slice(0, tm))

        @pl.when(i == nt - 1)
        def _():
            t_idx = lax.broadcasted_iota(jnp.int32, (SGU_CHUNK, SGU_CHUNK), 0)
            s_idx = lax.broadcasted_iota(jnp.int32, (SGU_CHUNK, SGU_CHUNK), 1)
            for g in range(N_GROUPS):
                dws_ref[g] = jnp.where(s_idx <= t_idx, dws_ref[g], 0.0)
            stores = [pltpu.make_async_copy(src, dst, sem.at[n])
                      for n, (src, dst) in enumerate(((acc_out, dwout_hbm), (acc_pa, dwpa_hbm), (acc_pb, dwpb_hbm)))]
            for cp in stores:
                cp.start()
            for cp in stores:
                cp.wait()

    tile = lambda w: pl.BlockSpec((tm, w), lambda i: (i, 0))
    whole = lambda shape: pl.BlockSpec(shape, lambda i: (0,) * len(shape))
    hbm = pl.BlockSpec(memory_space=pl.ANY)
    return pl.pallas_call(
        body, name="mid_fwd_bwd",
        grid=(nt,),
        in_specs=[tile(D_MODEL), tile(D_MODEL), tile(D_A)]
        + [pl.BlockSpec((tm, COL_BLOCK), functools.partial(lambda c, i: (i + Z_PAD // tm, c), c))
           for c in range(3, N_COL_BLOCKS)]
        + [hbm, hbm, hbm,
                  whole((1, 2 * D_MODEL)), whole((1, D_B)), whole((1, D_B)),
                  whole((N_GROUPS, SGU_CHUNK, SGU_CHUNK)), whole((N_GROUPS, SGU_CHUNK, 1)), whole((1, D_MODEL))],
        out_specs=(tile(D_MODEL), tile(D_A), tile(REST), hbm, hbm, hbm,
                   whole((1, 2 * D_MODEL)), whole((1, D_MODEL)), whole((1, D_B)), whole((1, D_B)),
                   whole((N_GROUPS, SGU_CHUNK, SGU_CHUNK)), whole((N_GROUPS, SGU_CHUNK, 1)), whole((1, 1))),
        out_shape=(jax.ShapeDtypeStruct((s, D_MODEL), F32), jax.ShapeDtypeStruct((s, D_A), BF16),
                   jax.ShapeDtypeStruct((s, REST), BF16),
                   jax.ShapeDtypeStruct((D_MODEL, D_MODEL), F32), jax.ShapeDtypeStruct((D_A, D_MODEL), F32),
                   jax.ShapeDtypeStruct((D_B, D_MODEL), F32),
                   jax.ShapeDtypeStruct((1, 2 * D_MODEL), F32), jax.ShapeDtypeStruct((1, D_MODEL), F32),
                   jax.ShapeDtypeStruct((1, D_B), F32), jax.ShapeDtypeStruct((1, D_B), F32),
                   jax.ShapeDtypeStruct((N_GROUPS, SGU_CHUNK, SGU_CHUNK), F32),
                   jax.ShapeDtypeStruct((N_GROUPS, SGU_CHUNK, 1), F32), jax.ShapeDtypeStruct((1, 1), F32)),
        scratch_shapes=[pltpu.VMEM((D_A, D_MODEL), BF16), pltpu.VMEM((D_B, D_MODEL), BF16),
                        pltpu.VMEM((D_MODEL, D_MODEL), BF16), pltpu.VMEM((N_GROUPS, SGU_CHUNK, SGU_CHUNK), BF16),
                        pltpu.VMEM((D_MODEL, D_MODEL), F32), pltpu.VMEM((D_A, D_MODEL), F32),
                        pltpu.VMEM((D_B, D_MODEL), F32),
                        pltpu.SemaphoreType.DMA((3,))],
        compiler_params=_params(56),
    )(x, target, attn_out, *([z] * (N_COL_BLOCKS - 3)), w_pa, w_pb, w_out, b_gate, ln_g, ln_b, w_s, b_s, final_g)


def _proj_bwd_x(dqkv, drest, x, dx2, norm_g, w_in_t, to_chip, small):
    s = x.shape[0]
    tm = 512 if s % 512 == 0 else TOKEN_TILE
    nt = s // tm
    rows = to_chip.shape[1]
    half = D_MODEL // 2
    left, right = slice(0, half), slice(half, D_MODEL)

    def body(dqkv_ref, dr_ref, x_ref, dx2_ref, g_ref, w_hbm, tc_hbm, ng_ref, rel_ref,
             dx_ref, fc_ref, slab_land,
             w, tc_ref, slab_stage, via_x, via_y, mine, out_x, out_y, sem, send_sems, recv_sems,
             gather_send, gather_recv, keep_sems):
        i = pl.program_id(0)
        x_, y_, c_ = _my_pos()
        me = _flat_id((x_, y_, c_))
        xn, yn = (1 - x_, y_, c_), (x_, 1 - y_, c_)
        gather = _SlotGather([slab_land], gather_send, gather_recv, own=[slab_stage])
        keep = [pltpu.make_async_copy(slab_stage, slab_land.at[me], keep_sems.at[0])]

        def copy(k, src, dst, to):
            return pltpu.make_async_remote_copy(src_ref=src, dst_ref=dst, send_sem=send_sems.at[k],
                                                recv_sem=recv_sems.at[k], device_id=to, device_id_type=MESH)

        first = [copy(0, tc_ref.at[0, :, left], fc_ref.at[0, :, left], xn), copy(1, tc_ref.at[2, :, left], via_x, xn),
                 copy(2, tc_ref.at[1, :, right], fc_ref.at[1, :, right], yn), copy(3, tc_ref.at[2, :, right], via_y, yn)]
        second = [copy(4, out_y, fc_ref.at[1, :, left], yn), copy(5, out_x, fc_ref.at[0, :, right], xn)]

        def add_and_send(arrival, landed, own_half, stage, onward):
            load = pltpu.make_async_copy(own_half, mine, sem)
            load.start()
            arrival.wait_recv()
            load.wait()
            stage[...] = (mine[...].astype(F32) + landed[...].astype(F32)).astype(BF16)
            onward.start()

        @pl.when(i == 0)
        def _():
            cp = pltpu.make_async_copy(w_hbm, w, sem)
            cp.start()
            _fill_slab(slab_stage, ((ROW_NORM_G, ng_ref), (ROW_REL, rel_ref)))
            for cp_keep in keep:
                cp_keep.start()
            gather.start()
            stage_in = pltpu.make_async_copy(tc_hbm, tc_ref, keep_sems.at[2])
            stage_in.start()
            stage_in.wait()
            for rc in first:
                rc.start()
            cp.wait()

        @pl.when(i == nt // 2)
        def _():
            gather.pass_on()
            add_and_send(first[1], via_x, tc_ref.at[1, :, left], out_y, second[0])
            add_and_send(first[3], via_y, tc_ref.at[0, :, right], out_x, second[1])

        dh = None
        for c in range(N_COL_BLOCKS):
            dz = dqkv_ref[c] if c < 3 else dr_ref[:, (c - 3) * COL_BLOCK:(c - 2) * COL_BLOCK]
            part = _dot(dz, w[c * COL_BLOCK:(c + 1) * COL_BLOCK, :])
            dh = part if dh is None else dh + part
        xf = x_ref[...]
        r = lax.rsqrt(jnp.mean(xf * xf, axis=-1, keepdims=True) + EPS)
        xn = xf * r
        gh = dh * g_ref[...]
        dx_ref[...] = r * (gh - xn * jnp.mean(gh * xn, axis=-1, keepdims=True)) + dx2_ref[...]

        @pl.when(i == nt - 1)
        def _():
            gather.finish()
            for cp_keep in keep:
                cp_keep.wait()
            for k in (0, 2, 4, 5):
                (first + second)[k].wait_recv()
            for rc in first + second:
                rc.wait_send()

    hbm = pl.BlockSpec(memory_space=pl.ANY)
    whole = lambda a: pl.BlockSpec(a.shape, lambda i: (0,) * a.ndim)
    return pl.pallas_call(
        body, name="proj_bwd_x",
        grid=(nt,),
        in_specs=[pl.BlockSpec((3, tm, D_A), lambda i: (0, i, 0)),
                  pl.BlockSpec((tm, REST), lambda i: (i, 0)),
                  pl.BlockSpec((tm, D_MODEL), lambda i: (i, 0)),
                  pl.BlockSpec((tm, D_MODEL), lambda i: (i, 0)),
                  pl.BlockSpec((1, D_MODEL), lambda i: (0, 0)),
                  hbm, hbm] + [whole(a) for a in small],
        out_specs=(pl.BlockSpec((tm, D_MODEL), lambda i: (i, 0)), hbm, hbm),
        out_shape=(jax.ShapeDtypeStruct((s, D_MODEL), F32), jax.ShapeDtypeStruct((2, rows, D_MODEL), BF16),
                   jax.ShapeDtypeStruct((N_DEV, LATE_SLAB_ROWS, D_MODEL), F32)),
        scratch_shapes=[pltpu.VMEM((D_IN, D_MODEL), BF16), pltpu.VMEM(to_chip.shape, BF16),
                        pltpu.VMEM((LATE_SLAB_ROWS, D_MODEL), F32)]
        + [pltpu.VMEM((rows, half), BF16)] * 5
        + [pltpu.SemaphoreType.DMA, pltpu.SemaphoreType.DMA((6,)), pltpu.SemaphoreType.DMA((6,)),
           pltpu.SemaphoreType.DMA((1, N_DEV - 1)), pltpu.SemaphoreType.DMA((1, N_DEV - 1)),
           pltpu.SemaphoreType.DMA((3,))],
        compiler_params=_params(56),
    )(dqkv, drest, x, dx2, norm_g, w_in_t, to_chip, *small)


def _proj_bwd_w(xn, dqkv, drest, norm_g, w_in_t):
    s = xn.shape[0]
    tk = min(s, 1024)
    nk = s // tk

    def body(xn_ref, dqkv_ref, dr_ref, g_ref, w_ref, o_ref, dg_ref, acc):
        j = pl.program_id(0)
        i = pl.program_id(1)

        @pl.when((j == 0) & (i == 0))
        def _():
            dg_ref[...] = jnp.zeros(dg_ref.shape, F32)

        @pl.when(i == 0)
        def _():
            acc[...] = jnp.zeros(acc.shape, F32)

        @pl.when(j < 3)
        def _():
            acc[...] += _dot(dqkv_ref[...], xn_ref[...], TN)

        @pl.when(j >= 3)
        def _():
            acc[...] += _dot(dr_ref[...], xn_ref[...], TN)

        @pl.when(i == nk - 1)
        def _():
            m = acc[...]
            o_ref[...] = (m * g_ref[...]).astype(BF16)
            dg_ref[...] += jnp.sum(m * w_ref[...].astype(F32), axis=0, keepdims=True)

    return pl.pallas_call(
        body, name="proj_bwd_w",
        grid=(N_COL_BLOCKS, nk),
        in_specs=[pl.BlockSpec((tk, D_MODEL), lambda j, i: (i, 0)),
                  pl.BlockSpec((None, tk, COL_BLOCK),
                               lambda j, i: (jnp.minimum(j, 2), jnp.where(j < 3, i, nk - 1), 0)),
                  pl.BlockSpec((tk, COL_BLOCK),
                               lambda j, i: (jnp.where(j >= 3, i, 0), jnp.maximum(j - 3, 0))),
                  pl.BlockSpec((1, D_MODEL), lambda j, i: (0, 0)),
                  pl.BlockSpec((COL_BLOCK, D_MODEL), lambda j, i: (j, 0))],
        out_specs=(pl.BlockSpec((COL_BLOCK, D_MODEL), lambda j, i: (j, 0)),
                   pl.BlockSpec((1, D_MODEL), lambda j, i: (0, 0))),
        out_shape=(jax.ShapeDtypeStruct((D_IN, D_MODEL), BF16), jax.ShapeDtypeStruct((1, D_MODEL), F32)),
        scratch_shapes=[pltpu.VMEM((COL_BLOCK, D_MODEL), F32)],
        compiler_params=_params(40),
    )(xn, dqkv, drest, norm_g, w_in_t)


def _adamw_math(w, g, m, v):
    c1 = 1.0 - ADAM_B1 ** ADAM_STEP
    c2 = 1.0 - ADAM_B2 ** ADAM_STEP
    nm = ADAM_B1 * m + (1.0 - ADAM_B1) * g
    nv = ADAM_B2 * v + (1.0 - ADAM_B2) * (g * g)
    return -ADAM_LR * ((nm / c1) / (jnp.sqrt(nv / c2) + ADAM_EPS) + ADAM_WD * w), nm, nv


def _adamw(name, w, g, m, v, from_chip):
    rows, cols = w.shape
    tr = rows if rows * cols <= 512 * 1024 else next(t for t in range(256, 7, -8) if rows % t == 0)

    def body(w_ref, g_ref, m_ref, v_ref, t_ref, g_out, d_ref, nm_ref, nv_ref):
        gg = g_ref[...]
        for j in range(from_chip.shape[0]):
            gg = gg + t_ref[j].astype(F32)
        g_out[...] = gg
        d_ref[...], nm_ref[...], nv_ref[...] = _adamw_math(w_ref[...], gg, m_ref[...], v_ref[...])

    spec = pl.BlockSpec((tr, cols), lambda i: (i, 0))
    shape = jax.ShapeDtypeStruct((rows, cols), F32)
    return pl.pallas_call(
        body, name=name,
        grid=(rows // tr,),
        in_specs=[spec] * 4 + [pl.BlockSpec((from_chip.shape[0], tr, cols), lambda i: (0, i, 0))],
        out_specs=(spec,) * 4, out_shape=(shape,) * 4,
        compiler_params=_params(32),
    )(w, g, m, v, from_chip)


_SMALL = (("norm_g", (1, D_MODEL)), ("b_gate", (1, 2 * D_MODEL)), ("rel_bias", (N_HEADS, N_REL)),
          ("sgu_ln_g", (1, D_B)), ("sgu_ln_b", (1, D_B)), ("w_s", (N_GROUPS * SGU_CHUNK, SGU_CHUNK)),
          ("b_s", (N_GROUPS, SGU_CHUNK)), ("final_g", (1, D_MODEL)))


def _adamw_small(slabs, ws_all, late_slabs, weights, moments_m, moments_v):
    k = len(_SMALL)

    def total(ref):
        acc = ref[0]
        for d in range(1, N_DEV):
            acc = acc + ref[d]
        return acc

    def body(*refs):
        slab_ref, ws_ref, late_ref = refs[:3]
        w_refs, m_refs, v_refs = refs[3:3 + k], refs[3 + k:3 + 2 * k], refs[3 + 2 * k:3 + 3 * k]
        outs = refs[3 + 3 * k:]
        slab, late = total(slab_ref), total(late_ref)
        grads = {
            "norm_g": late[ROW_NORM_G:ROW_NORM_G + 1, :],
            "b_gate": jnp.concatenate([slab[ROW_B_GATE:ROW_B_GATE + 1, :], slab[ROW_B_GATE + 1:ROW_B_GATE + 2, :]], axis=1),
            "rel_bias": late[ROW_REL:ROW_REL + N_HEADS, :N_REL],
            "sgu_ln_g": slab[ROW_LN_G:ROW_LN_G + 1, :D_B],
            "sgu_ln_b": slab[ROW_LN_B:ROW_LN_B + 1, :D_B],
            "w_s": total(ws_ref),
            "b_s": slab[ROW_B_S:ROW_B_S + N_GROUPS, :SGU_CHUNK],
            "final_g": slab[ROW_FINAL_G:ROW_FINAL_G + 1, :],
        }
        for n, (name, _) in enumerate(_SMALL):
            g = grads[name]
            outs[n][...] = g
            outs[k + n][...], outs[2 * k + n][...], outs[3 * k + n][...] = _adamw_math(
                w_refs[n][...], g, m_refs[n][...], v_refs[n][...])
        outs[4 * k][...] = slab[ROW_LOSS:ROW_LOSS + 1, :1]

    vmem = pl.BlockSpec(memory_space=pltpu.VMEM)
    shapes = tuple(jax.ShapeDtypeStruct(shape, F32) for _, shape in _SMALL)
    return pl.pallas_call(
        body, name="adamw_small",
        out_shape=shapes * 4 + (jax.ShapeDtypeStruct((1, 1), F32),),
        in_specs=[vmem] * (3 + 3 * k), out_specs=tuple([vmem] * (4 * k + 1)),
        compiler_params=_params(16),
    )(slabs, ws_all, late_slabs, *weights, *moments_m, *moments_v)


def _pad_rel(a):
    return jnp.pad(a.reshape(N_HEADS, N_REL), ((0, 0), (0, N_REL_PAD - N_REL)))


def kernel(x, norm_g, w_in, b_gate, rel_bias, sgu_ln_g, sgu_ln_b, w_s, b_s, w_pa, w_pb, w_out, final_g, loss_target, m_norm_g, m_w_in, m_b_gate, m_rel_bias, m_sgu_ln_g, m_sgu_ln_b, m_w_s, m_b_s, m_w_pa, m_w_pb, m_w_out, m_final_g, v_norm_g, v_w_in, v_b_gate, v_rel_bias, v_sgu_ln_g, v_sgu_ln_b, v_w_s, v_b_s, v_w_pa, v_w_pb, v_w_out, v_final_g):
    s = x.shape[1]
    xs = x.reshape(s, D_MODEL)
    tgt = loss_target.reshape(s, D_MODEL)

    bias_table = _bias_table(_pad_rel(rel_bias))
    w_in_t = jnp.swapaxes(w_in[0], 0, 1)
    qkv, x_norm, w_in_t_full = _gather_proj_fwd(xs, norm_g, w_in_t)
    attn_out, g_pa, g_pb, g_out = _attn_fwd(qkv, bias_table, (w_pa[0], w_pb[0], w_out[0]))
    w_pa_full = jnp.transpose(g_pa, (1, 0, 2)).reshape(D_A, D_MODEL)
    w_pb_full = jnp.transpose(g_pb, (1, 0, 2)).reshape(D_B, D_MODEL)
    w_out_full = g_out.reshape(D_MODEL, D_MODEL)

    (dx2, d_attn, drest, dw_out, dw_pa, dw_pb, d_bgate, d_fg, d_lng, d_lnb, d_ws, d_bs, loss_part) = _mid_fwd_bwd(
        xs, tgt, attn_out, qkv, w_pa_full, w_pb_full, w_out_full, b_gate, sgu_ln_g, sgu_ln_b, w_s[0],
        b_s.reshape(N_GROUPS, SGU_CHUNK, 1), final_g.reshape(1, D_MODEL))

    own_pa, own_pb, own_out, tc_pa, tc_pb, tc_out = _reduce_chip(
        "reduce_chip_proj", (dw_pa, dw_pb, dw_out), (1, 1, 0))
    dqkv, dbias, fc_pa, fc_pb, fc_out, slabs, ws_all = _attn_bwd(
        qkv, bias_table, d_attn, (tc_pa, tc_pb, tc_out),
        (d_bgate, d_lng, d_lnb, d_fg, loss_part, d_bs, d_ws.reshape(N_GROUPS * SGU_CHUNK, SGU_CHUNK)))
    d_rel = _bias_grad(dbias)
    dw_in_t, d_ng = _proj_bwd_w(x_norm, dqkv, drest, norm_g, w_in_t_full)
    own_in, tc_in = _reduce_chip("reduce_chip_in", (dw_in_t,), (0,))
    grad_x, fc_in, late_slabs = _proj_bwd_x(dqkv, drest, xs, dx2, norm_g, w_in_t_full, tc_in, (d_ng, d_rel))
    big = {"w_in": tuple(jnp.swapaxes(t, 0, 1)[None] for t in _adamw(
        "adamw_w_in", w_in_t, own_in, jnp.swapaxes(m_w_in[0], 0, 1), jnp.swapaxes(v_w_in[0], 0, 1), fc_in))}
    for name, w, g, fc, m, v in (("w_pa", w_pa, own_pa, fc_pa, m_w_pa, v_w_pa),
                                 ("w_pb", w_pb, own_pb, fc_pb, m_w_pb, v_w_pb),
                                 ("w_out", w_out, own_out, fc_out, m_w_out, v_w_out)):
        big[name] = tuple(t[None] for t in _adamw("adamw_" + name, w[0], g, m[0], v[0], fc))

    as_2d = lambda leaves: [a.reshape(shape) for a, (_, shape) in zip(leaves, _SMALL)]
    small_out = _adamw_small(
        slabs, ws_all, late_slabs, as_2d((norm_g, b_gate, rel_bias, sgu_ln_g, sgu_ln_b, w_s, b_s, final_g)),
        as_2d((m_norm_g, m_b_gate, m_rel_bias, m_sgu_ln_g, m_sgu_ln_b, m_w_s, m_b_s, m_final_g)),
        as_2d((v_norm_g, v_b_gate, v_rel_bias, v_sgu_ln_g, v_sgu_ln_b, v_w_s, v_b_s, v_final_g)))
    small_index = {name: n for n, (name, _) in enumerate(_SMALL)}

    def leaf(kind, name, like):
        if name in big:
            return big[name][kind]
        return small_out[kind * len(_SMALL) + small_index[name]].reshape(like.shape)

    weights = (("norm_g", norm_g), ("w_in", w_in), ("b_gate", b_gate), ("rel_bias", rel_bias), ("sgu_ln_g", sgu_ln_g),
               ("sgu_ln_b", sgu_ln_b), ("w_s", w_s), ("b_s", b_s), ("w_pa", w_pa), ("w_pb", w_pb), ("w_out", w_out),
               ("final_g", final_g))
    outs = [small_out[-1].reshape(()), grad_x.reshape(x.shape)]
    for kind in range(4):
        outs.extend(leaf(kind, name, like) for name, like in weights)
    return tuple(outs)
```

```python
import functools
import math

import jax
import jax.numpy as jnp
from jax import lax
from jax.experimental import pallas as pl
from jax.experimental.pallas import tpu as pltpu

F32 = jnp.float32
BF16 = jnp.bfloat16
MESH = pl.DeviceIdType.MESH
N_DEV = 8

D_MODEL = 1024
D_A = 512
D_B = 512
D_IN = 5632
N_HEADS = 8
HEAD_DIM = 64
N_PREV = 8
REL_CLIP = 128
N_REL = 2 * REL_CLIP + 1
N_REL_PAD = 384
SGU_CHUNK = 128
N_GROUPS = 4
EPS = 1e-6
NEG_INF = -1e30
Q_SCALE = HEAD_DIM ** -0.5

Q_BLOCK = 256
K_SPAN = 768
Z_PAD = K_SPAN - Q_BLOCK
ROLL_W = 1024
COL_BLOCK = 512
N_COL_BLOCKS = D_IN // COL_BLOCK
REST = D_IN - 3 * D_A
TOKEN_TILE = 256

ADAM_LR = 0.001
ADAM_B1 = 0.9
ADAM_B2 = 0.999
ADAM_EPS = 1e-08
ADAM_WD = 0.01
ADAM_STEP = 10

GELU_C = math.sqrt(2.0 / math.pi)
GELU_A = 0.044715

NT = (((1,), (1,)), ((), ()))
TN = (((0,), (0,)), ((), ()))
HIGHEST = lax.Precision.HIGHEST


def _params(vmem_mb, **kw):
    return pltpu.CompilerParams(vmem_limit_bytes=vmem_mb * 1024 * 1024, **kw)


def _dot(a, b, dims=None):
    if dims is None:
        return jnp.dot(a, b, preferred_element_type=F32)
    return lax.dot_general(a, b, dims, preferred_element_type=F32)


def _sigmoid(x):
    return 0.5 * jnp.tanh(0.5 * x) + 0.5


def _gelu_and_grad(u):
    u2 = u * u
    t = jnp.tanh(GELU_C * (u + GELU_A * u * u2))
    half = 0.5 * (1.0 + t)
    g = u * half
    dg = half + 0.5 * u * (1.0 - t * t) * (GELU_C * (1.0 + 3.0 * GELU_A * u2))
    return g, dg


def _my_pos():
    return lax.axis_index("x"), lax.axis_index("y"), lax.axis_index("c")


def _flat_id(pos):
    return 4 * pos[0] + 2 * pos[1] + pos[2]


def _other_chips(pos):
    x, y, _ = pos
    return ((1 - x, y), (x, 1 - y), (1 - x, 1 - y))


class _SlotGather:
    def __init__(self, bufs, send_sems, recv_sems, own=None):
        self.bufs, self.send_sems, self.recv_sems = bufs, send_sems, recv_sems
        self.own = own if own is not None else [None] * len(bufs)
        x, y, c = _my_pos()
        self.c, self.me, self.sib = c, (x, y, c), (x, y, 1 - c)
        self.chips = _other_chips(self.me)

    def _copy(self, a, k, block, to):
        slot = _flat_id(block)
        src = self.own[a] if (k < 4 and self.own[a] is not None) else self.bufs[a].at[slot]
        return pltpu.make_async_remote_copy(
            src_ref=src, dst_ref=self.bufs[a].at[slot],
            send_sem=self.send_sems.at[a, k], recv_sem=self.recv_sems.at[a, k], device_id=to, device_id_type=MESH)

    def _own_sends(self):
        n = len(self.bufs)
        return ([self._copy(a, 1 + j, self.me, (*chip, self.c)) for j, chip in enumerate(self.chips) for a in range(n)]
                + [self._copy(a, 0, self.me, self.sib) for a in range(n)])

    def _passes(self):
        return [self._copy(a, 4 + j, (*chip, self.c), self.sib)
                for j, chip in enumerate(self.chips) for a in range(len(self.bufs))]

    def start(self):
        for cp in self._own_sends():
            cp.start()

    def pass_on(self):
        for j, chip in enumerate(self.chips):
            for a in range(len(self.bufs)):
                self._copy(a, 1 + j, (*chip, self.c), self.me).wait_recv()
                self._copy(a, 4 + j, (*chip, self.c), self.sib).start()

    def finish(self):
        for a in range(len(self.bufs)):
            self._copy(a, 0, self.sib, self.me).wait_recv()
            for j, chip in enumerate(self.chips):
                self._copy(a, 4 + j, (*chip, 1 - self.c), self.me).wait_recv()
        for cp in self._own_sends() + self._passes():
            cp.wait_send()


def _reduce_chip(name, parts, sharded_dim):
    n = len(parts)
    shapes = []
    for p, dim in zip(parts, sharded_dim):
        shape = list(p.shape)
        shape[dim] //= N_DEV
        shapes.append(tuple(shape))
    staged = [not (dim == 0 and p.dtype == BF16) for p, dim in zip(parts, sharded_dim)]

    def body(*refs):
        full, own, to_chip = refs[:n], refs[n:2 * n], refs[2 * n:3 * n]
        ins, from_sib = refs[3 * n:4 * n], refs[4 * n:5 * n]
        send_sems, recv_sems = refs[5 * n], refs[5 * n + 1]
        x, y, c = _my_pos()
        sib = (x, y, 1 - c)
        chips = ((x, y),) + _other_chips((x, y, c))
        for a in range(n):
            rows, cols = shapes[a]
            for d in range(N_DEV if staged[a] else 0):
                if sharded_dim[a] == 0:
                    ins[a][d] = full[a][d * rows:(d + 1) * rows, :].astype(BF16)
                else:
                    ins[a][d] = full[a][:, d * cols:(d + 1) * cols].astype(BF16)

        def block(a, d):
            if staged[a]:
                return ins[a].at[d]
            rows = shapes[a][0]
            return full[a].at[pl.ds(pl.multiple_of(d * rows, 16), rows), :]

        def to_sibling(a, r):
            return pltpu.make_async_remote_copy(
                src_ref=block(a, _flat_id((*chips[r], 1 - c))), dst_ref=from_sib[a].at[r],
                send_sem=send_sems.at[a, r], recv_sem=recv_sems.at[a, r], device_id=sib, device_id_type=MESH)

        sends = [to_sibling(a, r) for r in (1, 2, 3, 0) for a in range(n)]
        for cp in sends:
            cp.start()
        for r in (1, 2, 3, 0):
            for a in range(n):
                to_sibling(a, r).wait_recv()
                both = block(a, _flat_id((*chips[r], c)))[...].astype(F32) + from_sib[a][r].astype(F32)
                if r == 0:
                    own[a][...] = both
                else:
                    to_chip[a][r - 1] = both.astype(BF16)
        for cp in sends:
            cp.wait_send()

    vmem = pl.BlockSpec(memory_space=pltpu.VMEM)
    return pl.pallas_call(
        body, name=name,
        out_shape=tuple(jax.ShapeDtypeStruct(sh, F32) for sh in shapes)
        + tuple(jax.ShapeDtypeStruct((3,) + sh, BF16) for sh in shapes),
        in_specs=[vmem] * n, out_specs=tuple([vmem] * (2 * n)),
        scratch_shapes=[pltpu.VMEM((N_DEV if st else 1,) + sh, BF16) for sh, st in zip(shapes, staged)]
        + [pltpu.VMEM((4,) + sh, BF16) for sh in shapes]
        + [pltpu.SemaphoreType.DMA((n, 4)), pltpu.SemaphoreType.DMA((n, 4))],
        compiler_params=_params(56),
    )(*parts)


def _owner_copies(to_chip, from_chip, send_sems, recv_sems):
    x, y, c = _my_pos()
    return [pltpu.make_async_remote_copy(
        src_ref=to_chip[a].at[j], dst_ref=from_chip[a].at[j],
        send_sem=send_sems.at[a, j], recv_sem=recv_sems.at[a, j], device_id=(*chip, c), device_id_type=MESH)
        for a in range(len(to_chip)) for j, chip in enumerate(_other_chips((x, y, c)))]


ROW_B_GATE, ROW_LN_G, ROW_LN_B, ROW_FINAL_G, ROW_LOSS, ROW_B_S, SLAB_ROWS = 1, 3, 4, 5, 6, 16, 24
ROW_NORM_G, ROW_REL, LATE_SLAB_ROWS = 0, 8, 16


def _rel_index(e):
    lo, hi = Z_PAD - REL_CLIP, Z_PAD + REL_CLIP
    return jnp.where(e <= lo, 2 * REL_CLIP, jnp.where(e < hi, hi - e, jnp.where(e <= K_SPAN, 0, 2 * REL_CLIP)))


def _bias_table(rel_bias_pad):
    def body(rb_ref, bt_ref):
        c = lax.broadcasted_iota(jnp.int32, (N_REL_PAD, ROLL_W), 1)
        r = lax.broadcasted_iota(jnp.int32, (N_REL_PAD, ROLL_W), 0)
        pick = (r == _rel_index(c)).astype(F32)
        rows = jnp.dot(rb_ref[...], pick, precision=HIGHEST, preferred_element_type=F32)
        qc = lax.broadcasted_iota(jnp.int32, (Q_BLOCK, K_SPAN), 0) >> 6
        kc = lax.broadcasted_iota(jnp.int32, (Q_BLOCK, K_SPAN), 1) >> 6
        band = (kc >= qc) & (kc <= qc + N_PREV)
        for h in range(N_HEADS):
            t = jnp.broadcast_to(rows[h:h + 1, :], (Q_BLOCK, ROLL_W))
            t = pltpu.roll(t, 0, 1, stride=1, stride_axis=0)
            bt_ref[h] = jnp.where(band, t[:, :K_SPAN], NEG_INF)

    return pl.pallas_call(
        body, name="bias_table",
        out_shape=jax.ShapeDtypeStruct((N_HEADS, Q_BLOCK, K_SPAN), F32),
        compiler_params=_params(32),
    )(rel_bias_pad)


def _bias_grad(dbias):
    def body(a_ref, o_ref):
        rr = lax.broadcasted_iota(jnp.int32, (Q_BLOCK, Q_BLOCK), 0)
        cc = lax.broadcasted_iota(jnp.int32, (Q_BLOCK, Q_BLOCK), 1)
        flip = (rr + cc == Q_BLOCK - 1).astype(F32)
        c = lax.broadcasted_iota(jnp.int32, (ROLL_W, N_REL_PAD), 0)
        r = lax.broadcasted_iota(jnp.int32, (ROLL_W, N_REL_PAD), 1)
        e = jnp.where(c >= Q_BLOCK - 1, c - (Q_BLOCK - 1), c + (ROLL_W - Q_BLOCK + 1))
        pick = (r == _rel_index(e)).astype(F32)
        sums = []
        for h in range(N_HEADS):
            a = jnp.dot(flip, a_ref[h], precision=HIGHEST, preferred_element_type=F32)
            a = jnp.concatenate([a, jnp.zeros((Q_BLOCK, ROLL_W - K_SPAN), F32)], axis=1)
            a = pltpu.roll(a, 0, 1, stride=1, stride_axis=0)
            sums.append(jnp.sum(a, axis=0, keepdims=True))
        diag = jnp.concatenate(sums, axis=0)
        o_ref[...] = jnp.dot(diag, pick, precision=HIGHEST, preferred_element_type=F32)

    return pl.pallas_call(
        body, name="bias_grad",
        out_shape=jax.ShapeDtypeStruct((N_HEADS, N_REL_PAD), F32),
        compiler_params=_params(32),
    )(dbias)


def _gather_proj_fwd(x, norm_g, w_in_t):
    s = x.shape[0]
    tm = 512 if s % 512 == 0 else TOKEN_TILE
    nt = s // tm
    n_pad = Z_PAD // tm
    shard_w = w_in_t.shape[0]
    chip_w = 2 * shard_w
    n_chips = N_DEV // 2

    def body(order_ref, x_ref, g_ref, win_hbm, z_ref, xn_ref, wt_hbm, wt, hb, win_f32, send_sems, recv_sems,
             local_sems):
        j = pl.program_id(0)
        i = pl.program_id(1)
        x_, y_, c_ = _my_pos()
        me, sib = (x_, y_, c_), (x_, y_, 1 - c_)
        near = _other_chips(me)
        pick = lambda a, b: tuple(jnp.where(c_ == 0, u, v) for u, v in zip(a, b))
        passed_from, passed_to = pick(near[0], near[1]), pick(near[1], near[0])

        def rows_of(block):
            return wt.at[pl.ds(pl.multiple_of(_flat_id(block) * shard_w, 16), shard_w), :]

        def copy(k, block, to):
            return pltpu.make_async_remote_copy(
                src_ref=rows_of(block), dst_ref=rows_of(block),
                send_sem=send_sems.at[k], recv_sem=recv_sems.at[k], device_id=to, device_id_type=MESH)

        def sends():
            return ([copy(0, me, sib), copy(1, me, (*near[0], c_)), copy(2, me, (*near[1], c_)),
                     copy(3, (*passed_from, c_), (*passed_to, c_))]
                    + [copy(4 + n, (*near[n], c_), sib) for n in range(3)])

        keep = pltpu.make_async_copy(wt, wt_hbm, local_sems.at[0])

        @pl.when((j == 0) & (i == 0))
        def _():
            load = pltpu.make_async_copy(win_hbm, win_f32, local_sems.at[1])
            load.start()
            load.wait()
            rows_of(me)[...] = win_f32[...].astype(BF16)
            for cp in sends()[:3]:
                cp.start()
            copy(0, sib, me).wait_recv()

        @pl.when((j == 1) & (i == 0))
        def _():
            copy(1, (*near[0], c_), me).wait_recv()
            copy(2, (*near[1], c_), me).wait_recv()
            for cp in sends()[3:6]:
                cp.start()
            copy(4, (*near[0], 1 - c_), me).wait_recv()

        @pl.when((j == 2) & (i == 0))
        def _():
            copy(5, (*near[1], 1 - c_), me).wait_recv()

        @pl.when((j == 3) & (i == 0))
        def _():
            copy(3, (*near[2], c_), me).wait_recv()
            copy(6, (*near[2], c_), sib).start()
            copy(6, (*near[2], 1 - c_), me).wait_recv()
            keep.start()

        @pl.when(i < n_pad)
        def _():
            z_ref[...] = jnp.zeros(z_ref.shape, BF16)

        @pl.when(i >= n_pad)
        def _():
            rows = pl.ds(pl.multiple_of((i - n_pad) * tm, tm), tm)

            @pl.when(j == 0)
            def _():
                xf = x_ref[...]
                xn = xf * lax.rsqrt(jnp.mean(xf * xf, axis=-1, keepdims=True) + EPS)
                hb[rows, :] = (xn * g_ref[...]).astype(BF16)
                xn_ref[...] = xn.astype(BF16)

            chip_rows = pl.ds(pl.multiple_of(order_ref[j] * chip_w, 16), chip_w)
            blk = _dot(hb[rows, :], wt[chip_rows, :], NT)
            q_scale = jnp.where(order_ref[j] == 0, Q_SCALE, 1.0).astype(F32)
            z_ref[:, :D_A] = (blk[:, :D_A] * q_scale).astype(BF16)
            z_ref[:, D_A:] = blk[:, D_A:].astype(BF16)

        @pl.when((j == n_chips - 1) & (i == n_pad + nt - 1))
        def _():
            keep.wait()
            for cp in sends():
                cp.wait_send()

    pos = _my_pos()
    order = jnp.stack([2 * cx + cy for cx, cy in ((pos[0], pos[1]),) + _other_chips(pos)]).astype(jnp.int32)
    first_pass = lambda j, i: jnp.where(j == 0, jnp.maximum(i - n_pad, 0), nt - 1)
    grid_spec = pltpu.PrefetchScalarGridSpec(
        num_scalar_prefetch=1,
        grid=(n_chips, n_pad + nt),
        in_specs=[pl.BlockSpec((tm, D_MODEL), lambda j, i, o: (first_pass(j, i), 0)),
                  pl.BlockSpec((1, D_MODEL), lambda j, i, o: (0, 0)),
                  pl.BlockSpec(memory_space=pl.ANY)],
        out_specs=(pl.BlockSpec((tm, chip_w), lambda j, i, o: (i, o[j])),
                   pl.BlockSpec((tm, D_MODEL), lambda j, i, o: (first_pass(j, i), 0)),
                   pl.BlockSpec(memory_space=pl.ANY)),
        scratch_shapes=[pltpu.VMEM((D_IN, D_MODEL), BF16),
                        pltpu.VMEM((s, D_MODEL), BF16), pltpu.VMEM(w_in_t.shape, F32),
                        pltpu.SemaphoreType.DMA((N_DEV - 1,)), pltpu.SemaphoreType.DMA((N_DEV - 1,)),
                        pltpu.SemaphoreType.DMA((2,))])
    return pl.pallas_call(
        body, name="gather_proj_fwd",
        grid_spec=grid_spec,
        out_shape=(jax.ShapeDtypeStruct((Z_PAD + s, D_IN), BF16), jax.ShapeDtypeStruct((s, D_MODEL), BF16),
                   jax.ShapeDtypeStruct((D_IN, D_MODEL), BF16)),
        compiler_params=_params(60),
    )(order, x, norm_g, w_in_t)


def _attn_specs(rows):
    pairs = N_HEADS // 2
    return ([pl.BlockSpec((rows, 128), functools.partial(lambda which, p: (0, which * pairs + p), which))
             for which in range(3)]
            + [pl.BlockSpec((2, Q_BLOCK, K_SPAN), lambda p: (p, 0, 0))])


def _head_masks():
    lane = lax.broadcasted_iota(jnp.int32, (1, 128), 1)
    first = lane < HEAD_DIM
    return (first, jnp.logical_not(first))


def _stack_heads(x, masks):
    zero = jnp.zeros((), x.dtype)
    return jnp.concatenate([jnp.where(m, x, zero) for m in masks], axis=0)


STRIP = 16


def _softmax_strips(s_ref, bias_ref, b):
    valid = lax.broadcasted_iota(jnp.int32, (1, K_SPAN), 1) >= Z_PAD - b * Q_BLOCK
    for t in range(2 * Q_BLOCK // STRIP):
        hh, r = divmod(t * STRIP, Q_BLOCK)
        st = s_ref[t * STRIP:(t + 1) * STRIP, :] + bias_ref[hh, r:r + STRIP, :]
        st = jnp.where(valid, st, NEG_INF)
        e = jnp.exp(st - jnp.max(st, axis=-1, keepdims=True))
        yield e * (1.0 / jnp.sum(e, axis=-1, keepdims=True))


def _side_by_side_strips(strips):
    half = len(strips) // 2
    return jnp.concatenate([jnp.concatenate([a, c], axis=1) for a, c in zip(strips[:half], strips[half:])], axis=0)


def _attn_fwd(qkv, bias_table, shards):
    s = qkv.shape[0] - Z_PAD
    nb = s // Q_BLOCK
    n = len(shards)
    pairs = N_HEADS // 2

    def body(*refs):
        q_ref, k_ref, v_ref, bt_ref = refs[:4]
        shard_refs = refs[4:4 + n]
        o_ref = refs[4 + n]
        slot_refs = refs[5 + n:5 + 2 * n]
        stages = refs[5 + 2 * n:5 + 3 * n]
        s_scr, send_sems, recv_sems, local_sems = refs[5 + 3 * n:]
        p_id = pl.program_id(0)
        gather = _SlotGather(slot_refs, send_sems, recv_sems, own=stages)
        keep = [pltpu.make_async_copy(stages[a], slot_refs[a].at[_flat_id(_my_pos())], local_sems.at[a])
                for a in range(n)]

        @pl.when(p_id == 0)
        def _():
            for a in range(n):
                stages[a][...] = shard_refs[a][...].astype(BF16)
                keep[a].start()
            gather.start()

        @pl.when(p_id == 2)
        def _():
            gather.pass_on()

        masks = _head_masks()

        def scores(b, half):
            r0 = pl.multiple_of(b * Q_BLOCK, Q_BLOCK)
            q2 = _stack_heads(q_ref[pl.ds(r0 + Z_PAD, Q_BLOCK), :], masks)
            s_scr[half] = _dot(q2, k_ref[pl.ds(r0, K_SPAN), :], NT)

        def finish(b, half):
            r0 = pl.multiple_of(b * Q_BLOCK, Q_BLOCK)
            v2 = _stack_heads(v_ref[pl.ds(r0, K_SPAN), :], masks)
            p = [st.astype(BF16) for st in _softmax_strips(s_scr.at[half], bt_ref, b)]
            o_ref[pl.ds(r0, Q_BLOCK), :] = _dot(_side_by_side_strips(p), v2)

        def two_blocks(i, carry):
            b = 2 * i
            scores(b + 1, 1)
            finish(b, 0)
            scores(jnp.minimum(b + 2, nb - 1), 0)
            finish(b + 1, 1)
            return carry

        scores(0, 0)
        lax.fori_loop(0, nb // 2, two_blocks, 0)

        @pl.when(p_id == pairs - 1)
        def _():
            gather.finish()
            for cp in keep:
                cp.wait()

    hbm = pl.BlockSpec(memory_space=pl.ANY)
    return pl.pallas_call(
        body, name="attn_fwd",
        grid=(pairs,),
        in_specs=_attn_specs(s + Z_PAD) + [pl.BlockSpec(a.shape, lambda p: (0, 0)) for a in shards],
        out_specs=(pl.BlockSpec((s, 128), lambda p: (0, p)),) + (hbm,) * n,
        out_shape=(jax.ShapeDtypeStruct((s, D_A), F32),)
        + tuple(jax.ShapeDtypeStruct((N_DEV,) + a.shape, BF16) for a in shards),
        scratch_shapes=[pltpu.VMEM(a.shape, BF16) for a in shards]
        + [pltpu.VMEM((2, 2 * Q_BLOCK, K_SPAN), F32),
           pltpu.SemaphoreType.DMA((n, N_DEV - 1)), pltpu.SemaphoreType.DMA((n, N_DEV - 1)),
           pltpu.SemaphoreType.DMA((n,))],
        compiler_params=_params(48),
    )(qkv, qkv, qkv, bias_table, *shards)


def _fill_slab(stage, rows):
    stage[...] = jnp.zeros(stage.shape, F32)
    for row, ref in rows:
        r, c = ref.shape
        if c > D_MODEL:
            for part in range(c // D_MODEL):
                stage[row + part:row + part + 1, :] = ref[:, part * D_MODEL:(part + 1) * D_MODEL]
        else:
            stage[row:row + r, :c] = ref[...]


def _attn_bwd(qkv, bias_table, d_out, to_chip, small):
    s = qkv.shape[0] - Z_PAD
    nb = s // Q_BLOCK
    n = len(to_chip)
    pairs = N_HEADS // 2
    ws_shape = small[-1].shape

    def body(*refs):
        q_ref, k_ref, v_ref, bt_ref, do_ref = refs[:5]
        to_chip_refs = refs[5:5 + n]
        bg_ref, lng_ref, lnb_ref, fg_ref, loss_ref, bs_ref, ws_ref = refs[5 + n:12 + n]
        dqkv_ref, db_ref = refs[12 + n:14 + n]
        from_chip_refs = refs[14 + n:14 + 2 * n]
        slab_land, ws_land = refs[14 + 2 * n:16 + 2 * n]
        (dk_acc, dv_acc, s_scr, dp_scr, slab_stage, ws_stage, send_sems, recv_sems, gather_send, gather_recv,
         keep_sems) = refs[16 + 2 * n:]
        p_id = pl.program_id(0)
        me = _flat_id(_my_pos())
        gather = _SlotGather([slab_land, ws_land], gather_send, gather_recv, own=[slab_stage, ws_stage])
        keep = [pltpu.make_async_copy(stage, land.at[me], keep_sems.at[k]) for k, (stage, land) in enumerate(
            ((slab_stage, slab_land), (ws_stage, ws_land)))]

        @pl.when(p_id == 0)
        def _():
            _fill_slab(slab_stage, ((ROW_B_GATE, bg_ref), (ROW_LN_G, lng_ref), (ROW_LN_B, lnb_ref),
                                    (ROW_FINAL_G, fg_ref), (ROW_LOSS, loss_ref)))
            eye = (lax.broadcasted_iota(jnp.int32, (SGU_CHUNK, SGU_CHUNK), 0)
                   == lax.broadcasted_iota(jnp.int32, (SGU_CHUNK, SGU_CHUNK), 1))
            for g in range(N_GROUPS):
                row = jnp.sum(jnp.where(eye, bs_ref[g], 0.0), axis=0, keepdims=True)
                slab_stage[ROW_B_S + g:ROW_B_S + g + 1, :SGU_CHUNK] = row
            ws_stage[...] = ws_ref[...]
            for cp in keep:
                cp.start()
            gather.start()
            for cp in _owner_copies(to_chip_refs, from_chip_refs, send_sems, recv_sems):
                cp.start()

        @pl.when(p_id == 2)
        def _():
            gather.pass_on()

        dk_acc[...] = jnp.zeros(dk_acc.shape, F32)
        dv_acc[...] = jnp.zeros(dv_acc.shape, F32)
        db_ref[...] = jnp.zeros(db_ref.shape, F32)
        masks = _head_masks()

        def operands(b):
            r0 = pl.multiple_of(b * Q_BLOCK, Q_BLOCK)
            q2 = _stack_heads(q_ref[pl.ds(r0 + Z_PAD, Q_BLOCK), :], masks)
            do2 = _stack_heads(do_ref[pl.ds(r0, Q_BLOCK), :], masks)
            return r0, q2, do2, k_ref[pl.ds(r0, K_SPAN), :]

        def ahead(b, half):
            r0, q2, do2, kcat = operands(b)
            s_scr[half] = _dot(q2, kcat, NT)
            dp_scr[half] = _dot(do2, v_ref[pl.ds(r0, K_SPAN), :], NT)

        def finish(b, half):
            r0, q2, do2, kcat = operands(b)
            p_strips, ds_strips = [], []
            for t, p in enumerate(_softmax_strips(s_scr.at[half], bt_ref, b)):
                hh, r = divmod(t * STRIP, Q_BLOCK)
                dp_t = dp_scr[half, t * STRIP:(t + 1) * STRIP, :]
                ds = p * (dp_t - jnp.sum(p * dp_t, axis=-1, keepdims=True))
                db_ref[hh, r:r + STRIP, :] += ds
                p_strips.append(p.astype(BF16))
                ds_strips.append(ds.astype(BF16))
            dq = _dot(_side_by_side_strips(ds_strips), _stack_heads(kcat, masks))
            dqkv_ref[0, pl.ds(r0, Q_BLOCK), :] = (dq * Q_SCALE).astype(BF16)
            dk_acc[pl.ds(r0, K_SPAN), :] += _dot(jnp.concatenate(ds_strips, axis=0), q2, TN)
            dv_acc[pl.ds(r0, K_SPAN), :] += _dot(jnp.concatenate(p_strips, axis=0), do2, TN)

        def two_blocks(i, carry):
            b = 2 * i
            ahead(b + 1, 1)
            finish(b, 0)
            ahead(jnp.minimum(b + 2, nb - 1), 0)
            finish(b + 1, 1)
            return carry

        ahead(0, 0)
        lax.fori_loop(0, nb // 2, two_blocks, 0)
        dqkv_ref[1] = dk_acc[Z_PAD:, :].astype(BF16)
        dqkv_ref[2] = dv_acc[Z_PAD:, :].astype(BF16)

        @pl.when(p_id == pairs - 1)
        def _():
            gather.finish()
            for cp in keep:
                cp.wait()
            for cp in _owner_copies(to_chip_refs, from_chip_refs, send_sems, recv_sems):
                cp.wait_recv()
                cp.wait_send()

    hbm = pl.BlockSpec(memory_space=pl.ANY)
    lands = ((N_DEV, SLAB_ROWS, D_MODEL), (N_DEV,) + ws_shape)
    return pl.pallas_call(
        body, name="attn_bwd",
        grid=(pairs,),
        in_specs=_attn_specs(s + Z_PAD) + [pl.BlockSpec((s, 128), lambda p: (0, p))] + [hbm] * n
        + [pl.BlockSpec(a.shape, functools.partial(lambda nd, p: (0,) * nd, a.ndim)) for a in small],
        out_specs=(pl.BlockSpec((3, s, 128), lambda p: (0, 0, p)),
                   pl.BlockSpec((2, Q_BLOCK, K_SPAN), lambda p: (p, 0, 0))) + (hbm,) * (n + 2),
        out_shape=(jax.ShapeDtypeStruct((3, s, D_A), BF16),
                   jax.ShapeDtypeStruct((N_HEADS, Q_BLOCK, K_SPAN), F32))
        + tuple(jax.ShapeDtypeStruct(t.shape, t.dtype) for t in to_chip)
        + tuple(jax.ShapeDtypeStruct(shape, F32) for shape in lands),
        scratch_shapes=[pltpu.VMEM((s + Z_PAD, 128), F32), pltpu.VMEM((s + Z_PAD, 128), F32),
                        pltpu.VMEM((2, 2 * Q_BLOCK, K_SPAN), F32), pltpu.VMEM((2, 2 * Q_BLOCK, K_SPAN), F32)]
        + [pltpu.VMEM(shape[1:], F32) for shape in lands]
        + [pltpu.SemaphoreType.DMA((n, 3)), pltpu.SemaphoreType.DMA((n, 3)),
           pltpu.SemaphoreType.DMA((2, N_DEV - 1)), pltpu.SemaphoreType.DMA((2, N_DEV - 1)),
           pltpu.SemaphoreType.DMA((2,))],
        compiler_params=_params(56),
    )(qkv, qkv, qkv, bias_table, d_out, *to_chip, *small)


def _mid_fwd_bwd(x, target, attn_out, z, w_pa, w_pb, w_out, b_gate, ln_g, ln_b, w_s, b_s, final_g):
    s = x.shape[0]
    tm = TOKEN_TILE
    nt = s // tm

    def body(x_ref, t_ref, oa_ref, ga_ref, ub_ref, vb_ref, gb_ref, ta0_ref, ta1_ref, tb0_ref, tb1_ref,
             wpa_hbm, wpb_hbm, wout_hbm, bg_ref, lng_ref, lnb_ref, ws_ref, bs_ref, fg_ref,
             dx2_ref, doa_ref, dz_ref, dwout_hbm, dwpa_hbm, dwpb_hbm, dbg_ref, dfg_ref, dlng_ref, dlnb_ref, dws_ref,
             dbs_ref, loss_ref,
             wpa, wpb, wout, wmix, acc_out, acc_pa, acc_pb, sem):
        i = pl.program_id(0)

        @pl.when(i == 0)
        def _():
            loads = [pltpu.make_async_copy(src, dst, sem.at[n])
                     for n, (src, dst) in enumerate(((wpa_hbm, wpa), (wpb_hbm, wpb), (wout_hbm, wout)))]
            for cp in loads:
                cp.start()
            t_idx = lax.broadcasted_iota(jnp.int32, (SGU_CHUNK, SGU_CHUNK), 0)
            s_idx = lax.broadcasted_iota(jnp.int32, (SGU_CHUNK, SGU_CHUNK), 1)
            for g in range(N_GROUPS):
                wmix[g] = jnp.where(s_idx <= t_idx, ws_ref[g], 0.0).astype(BF16)
            for ref in (acc_out, acc_pa, acc_pb, dbg_ref, dfg_ref, dlng_ref, dlnb_ref, dws_ref, dbs_ref, loss_ref):
                ref[...] = jnp.zeros(ref.shape, F32)
            for cp in loads:
                cp.wait()

        def tile_fwd_bwd(rows):
            g_a = ga_ref[rows, :].astype(F32)
            u_b = ub_ref[rows, :].astype(F32)
            v_b = vb_ref[rows, :].astype(F32)
            g_b = gb_ref[rows, :].astype(F32)
            bg = bg_ref[...]
            sg_a = _sigmoid(g_a)
            silu_a = g_a * sg_a
            o_a = oa_ref[rows, :]
            y_a = (o_a * silu_a).astype(BF16)
            ug, dgelu_u = _gelu_and_grad(u_b)
            vg, dgelu_v = _gelu_and_grad(v_b)
            mu = jnp.mean(vg, axis=-1, keepdims=True)
            vc = vg - mu
            rstd = lax.rsqrt(jnp.mean(vc * vc, axis=-1, keepdims=True) + EPS)
            vhat = vc * rstd
            lng = lng_ref[...]
            vn = (vhat * lng + lnb_ref[...]).astype(BF16)
            sg_b = _sigmoid(g_b)
            silu_b = g_b * sg_b
            subs = [slice(n * SGU_CHUNK, (n + 1) * SGU_CHUNK) for n in range(tm // SGU_CHUNK)]
            mixed = jnp.concatenate([jnp.concatenate(
                [_dot(wmix[g], vn[sub, g * 128:(g + 1) * 128]) + bs_ref[g] for g in range(N_GROUPS)], axis=1)
                for sub in subs], axis=0)
            um = ug * mixed
            y_b = (um * silu_b).astype(BF16)
            gate_a = _sigmoid(jnp.concatenate([ta0_ref[rows, :], ta1_ref[rows, :]], axis=1).astype(F32)
                              + bg[:, :D_MODEL])
            gate_b = _sigmoid(jnp.concatenate([tb0_ref[rows, :], tb1_ref[rows, :]], axis=1).astype(F32)
                              + bg[:, D_MODEL:])
            p_a = _dot(y_a, wpa[...])
            p_b = _dot(y_b, wpb[...])
            merged = (gate_a * p_a + gate_b * p_b).astype(BF16)
            x2 = x_ref[rows, :] + _dot(merged, wout[...])
            r2 = lax.rsqrt(jnp.mean(x2 * x2, axis=-1, keepdims=True) + EPS)
            xh = x2 * r2
            fg = fg_ref[...]
            err = xh * fg - t_ref[rows, :]
            loss_ref[...] += jnp.sum(jnp.sum(err * err, axis=-1, keepdims=True), axis=0, keepdims=True) * (0.5 / D_MODEL)
            dy = err * (1.0 / D_MODEL)
            dfg_ref[...] += jnp.sum(dy * xh, axis=0, keepdims=True)
            gy = dy * fg
            dx2 = r2 * (gy - xh * jnp.mean(gy * xh, axis=-1, keepdims=True))
            dx2_ref[rows, :] = dx2
            dx2b = dx2.astype(BF16)
            dmerged = _dot(dx2b, wout[...], NT)
            acc_out[...] += _dot(merged, dx2b, TN)
            dp_a = dmerged * gate_a
            dp_b = dmerged * gate_b
            dgate_a = dp_a * p_a * (1.0 - gate_a)
            dgate_b = dp_b * p_b * (1.0 - gate_b)
            dbg_ref[:, :D_MODEL] += jnp.sum(dgate_a, axis=0, keepdims=True)
            dbg_ref[:, D_MODEL:] += jnp.sum(dgate_b, axis=0, keepdims=True)
            dz_ref[rows, 2048:3072] = dgate_a.astype(BF16)
            dz_ref[rows, 3072:4096] = dgate_b.astype(BF16)
            dp_ab = dp_a.astype(BF16)
            dp_bb = dp_b.astype(BF16)
            dy_a = _dot(dp_ab, wpa[...], NT)
            dy_b = _dot(dp_bb, wpb[...], NT)
            acc_pa[...] += _dot(y_a, dp_ab, TN)
            acc_pb[...] += _dot(y_b, dp_bb, TN)
            doa_ref[rows, :] = (dy_a * silu_a).astype(BF16)
            dz_ref[rows, 0:512] = (dy_a * o_a * (sg_a * (1.0 + g_a * (1.0 - sg_a)))).astype(BF16)
            dz_ref[rows, 1536:2048] = (dy_b * um * (sg_b * (1.0 + g_b * (1.0 - sg_b)))).astype(BF16)
            dys = dy_b * silu_b
            dz_ref[rows, 512:1024] = (dys * mixed * dgelu_u).astype(BF16)
            dmixed = dys * ug
            dmb = dmixed.astype(BF16)
            dvn_rows = []
            for sub in subs:
                dvn_parts = []
                for g in range(N_GROUPS):
                    cols = slice(g * 128, (g + 1) * 128)
                    dws_ref[g] += _dot(dmb[sub, cols], vn[sub, cols], NT)
                    dbs_ref[g] += jnp.sum(dmixed[sub, cols], axis=-1, keepdims=True)
                    dvn_parts.append(_dot(wmix[g], dmb[sub, cols], TN))
                dvn_rows.append(jnp.concatenate(dvn_parts, axis=1))
            dvn = jnp.concatenate(dvn_rows, axis=0)
            dlng_ref[...] += jnp.sum(dvn * vhat, axis=0, keepdims=True)
            dlnb_ref[...] += jnp.sum(dvn, axis=0, keepdims=True)
            dvh = dvn * lng
            dvg = rstd * (dvh - jnp.mean(dvh, axis=-1, keepdims=True)
                          - vhat * jnp.mean(dvh * vhat, axis=-1, keepdims=True))
            dz_ref[rows, 1024:1536] = (dvg * dgelu_v).astype(BF16)

        tile_fwd_bwd(slice(0, tm))

        @pl.when(i == nt - 1)
        def _():
            t_idx = lax.broadcasted_iota(jnp.int32, (SGU_CHUNK, SGU_CHUNK), 0)
            s_idx = lax.broadcasted_iota(jnp.int32, (SGU_CHUNK, SGU_CHUNK), 1)
            for g in range(N_GROUPS):
                dws_ref[g] = jnp.where(s_idx <= t_idx, dws_ref[g], 0.0)
            stores = [pltpu.make_async_copy(src, dst, sem.at[n])
                      for n, (src, dst) in enumerate(((acc_out, dwout_hbm), (acc_pa, dwpa_hbm), (acc_pb, dwpb_hbm)))]
            for cp in stores:
                cp.start()
            for cp in stores:
                cp.wait()

    tile = lambda w: pl.BlockSpec((tm, w), lambda i: (i, 0))
    whole = lambda shape: pl.BlockSpec(shape, lambda i: (0,) * len(shape))
    hbm = pl.BlockSpec(memory_space=pl.ANY)
    return pl.pallas_call(
        body, name="mid_fwd_bwd",
        grid=(nt,),
        in_specs=[tile(D_MODEL), tile(D_MODEL), tile(D_A)]
        + [pl.BlockSpec((tm, COL_BLOCK), functools.partial(lambda c, i: (i + Z_PAD // tm, c), c))
           for c in range(3, N_COL_BLOCKS)]
        + [hbm, hbm, hbm,
                  whole((1, 2 * D_MODEL)), whole((1, D_B)), whole((1, D_B)),
                  whole((N_GROUPS, SGU_CHUNK, SGU_CHUNK)), whole((N_GROUPS, SGU_CHUNK, 1)), whole((1, D_MODEL))],
        out_specs=(tile(D_MODEL), tile(D_A), tile(REST), hbm, hbm, hbm,
                   whole((1, 2 * D_MODEL)), whole((1, D_MODEL)), whole((1, D_B)), whole((1, D_B)),
                   whole((N_GROUPS, SGU_CHUNK, SGU_CHUNK)), whole((N_GROUPS, SGU_CHUNK, 1)), whole((1, 1))),
        out_shape=(jax.ShapeDtypeStruct((s, D_MODEL), F32), jax.ShapeDtypeStruct((s, D_A), BF16),
                   jax.ShapeDtypeStruct((s, REST), BF16),
                   jax.ShapeDtypeStruct((D_MODEL, D_MODEL), F32), jax.ShapeDtypeStruct((D_A, D_MODEL), F32),
                   jax.ShapeDtypeStruct((D_B, D_MODEL), F32),
                   jax.ShapeDtypeStruct((1, 2 * D_MODEL), F32), jax.ShapeDtypeStruct((1, D_MODEL), F32),
                   jax.ShapeDtypeStruct((1, D_B), F32), jax.ShapeDtypeStruct((1, D_B), F32),
                   jax.ShapeDtypeStruct((N_GROUPS, SGU_CHUNK, SGU_CHUNK), F32),
                   jax.ShapeDtypeStruct((N_GROUPS, SGU_CHUNK, 1), F32), jax.ShapeDtypeStruct((1, 1), F32)),
        scratch_shapes=[pltpu.VMEM((D_A, D_MODEL), BF16), pltpu.VMEM((D_B, D_MODEL), BF16),
                        pltpu.VMEM((D_MODEL, D_MODEL), BF16), pltpu.VMEM((N_GROUPS, SGU_CHUNK, SGU_CHUNK), BF16),
                        pltpu.VMEM((D_MODEL, D_MODEL), F32), pltpu.VMEM((D_A, D_MODEL), F32),
                        pltpu.VMEM((D_B, D_MODEL), F32),
                        pltpu.SemaphoreType.DMA((3,))],
        compiler_params=_params(56),
    )(x, target, attn_out, *([z] * (N_COL_BLOCKS - 3)), w_pa, w_pb, w_out, b_gate, ln_g, ln_b, w_s, b_s, final_g)


def _proj_bwd_x(dqkv, drest, x, dx2, norm_g, w_in_t, to_chip, small):
    s = x.shape[0]
    tm = 512 if s % 512 == 0 else TOKEN_TILE
    nt = s // tm
    rows = to_chip.shape[1]
    half = D_MODEL // 2
    left, right = slice(0, half), slice(half, D_MODEL)

    def body(dqkv_ref, dr_ref, x_ref, dx2_ref, g_ref, w_hbm, tc_hbm, ng_ref, rel_ref,
             dx_ref, fc_ref, slab_land,
             w, tc_ref, slab_stage, via_x, via_y, mine, out_x, out_y, sem, send_sems, recv_sems,
             gather_send, gather_recv, keep_sems):
        i = pl.program_id(0)
        x_, y_, c_ = _my_pos()
        me = _flat_id((x_, y_, c_))
        xn, yn = (1 - x_, y_, c_), (x_, 1 - y_, c_)
        gather = _SlotGather([slab_land], gather_send, gather_recv, own=[slab_stage])
        keep = [pltpu.make_async_copy(slab_stage, slab_land.at[me], keep_sems.at[0])]

        def copy(k, src, dst, to):
            return pltpu.make_async_remote_copy(src_ref=src, dst_ref=dst, send_sem=send_sems.at[k],
                                                recv_sem=recv_sems.at[k], device_id=to, device_id_type=MESH)

        first = [copy(0, tc_ref.at[0, :, left], fc_ref.at[0, :, left], xn), copy(1, tc_ref.at[2, :, left], via_x, xn),
                 copy(2, tc_ref.at[1, :, right], fc_ref.at[1, :, right], yn), copy(3, tc_ref.at[2, :, right], via_y, yn)]
        second = [copy(4, out_y, fc_ref.at[1, :, left], yn), copy(5, out_x, fc_ref.at[0, :, right], xn)]

        def add_and_send(arrival, landed, own_half, stage, onward):
            load = pltpu.make_async_copy(own_half, mine, sem)
            load.start()
            arrival.wait_recv()
            load.wait()
            stage[...] = (mine[...].astype(F32) + landed[...].astype(F32)).astype(BF16)
            onward.start()

        @pl.when(i == 0)
        def _():
            cp = pltpu.make_async_copy(w_hbm, w, sem)
            cp.start()
            _fill_slab(slab_stage, ((ROW_NORM_G, ng_ref), (ROW_REL, rel_ref)))
            for cp_keep in keep:
                cp_keep.start()
            gather.start()
            stage_in = pltpu.make_async_copy(tc_hbm, tc_ref, keep_sems.at[2])
            stage_in.start()
            stage_in.wait()
            for rc in first:
                rc.start()
            cp.wait()

        @pl.when(i == nt // 2)
        def _():
            gather.pass_on()
            add_and_send(first[1], via_x, tc_ref.at[1, :, left], out_y, second[0])
            add_and_send(first[3], via_y, tc_ref.at[0, :, right], out_x, second[1])

        dh = None
        for c in range(N_COL_BLOCKS):
            dz = dqkv_ref[c] if c < 3 else dr_ref[:, (c - 3) * COL_BLOCK:(c - 2) * COL_BLOCK]
            part = _dot(dz, w[c * COL_BLOCK:(c + 1) * COL_BLOCK, :])
            dh = part if dh is None else dh + part
        xf = x_ref[...]
        r = lax.rsqrt(jnp.mean(xf * xf, axis=-1, keepdims=True) + EPS)
        xn = xf * r
        gh = dh * g_ref[...]
        dx_ref[...] = r * (gh - xn * jnp.mean(gh * xn, axis=-1, keepdims=True)) + dx2_ref[...]

        @pl.when(i == nt - 1)
        def _():
            gather.finish()
            for cp_keep in keep:
                cp_keep.wait()
            for k in (0, 2, 4, 5):
                (first + second)[k].wait_recv()
            for rc in first + second:
                rc.wait_send()

    hbm = pl.BlockSpec(memory_space=pl.ANY)
    whole = lambda a: pl.BlockSpec(a.shape, lambda i: (0,) * a.ndim)
    return pl.pallas_call(
        body, name="proj_bwd_x",
        grid=(nt,),
        in_specs=[pl.BlockSpec((3, tm, D_A), lambda i: (0, i, 0)),
                  pl.BlockSpec((tm, REST), lambda i: (i, 0)),
                  pl.BlockSpec((tm, D_MODEL), lambda i: (i, 0)),
                  pl.BlockSpec((tm, D_MODEL), lambda i: (i, 0)),
                  pl.BlockSpec((1, D_MODEL), lambda i: (0, 0)),
                  hbm, hbm] + [whole(a) for a in small],
        out_specs=(pl.BlockSpec((tm, D_MODEL), lambda i: (i, 0)), hbm, hbm),
        out_shape=(jax.ShapeDtypeStruct((s, D_MODEL), F32), jax.ShapeDtypeStruct((2, rows, D_MODEL), BF16),
                   jax.ShapeDtypeStruct((N_DEV, LATE_SLAB_ROWS, D_MODEL), F32)),
        scratch_shapes=[pltpu.VMEM((D_IN, D_MODEL), BF16), pltpu.VMEM(to_chip.shape, BF16),
                        pltpu.VMEM((LATE_SLAB_ROWS, D_MODEL), F32)]
        + [pltpu.VMEM((rows, half), BF16)] * 5
        + [pltpu.SemaphoreType.DMA, pltpu.SemaphoreType.DMA((6,)), pltpu.SemaphoreType.DMA((6,)),
           pltpu.SemaphoreType.DMA((1, N_DEV - 1)), pltpu.SemaphoreType.DMA((1, N_DEV - 1)),
           pltpu.SemaphoreType.DMA((3,))],
        compiler_params=_params(56),
    )(dqkv, drest, x, dx2, norm_g, w_in_t, to_chip, *small)


def _proj_bwd_w(xn, dqkv, drest, norm_g, w_in_t):
    s = xn.shape[0]
    tk = min(s, 1024)
    nk = s // tk

    def body(xn_ref, dqkv_ref, dr_ref, g_ref, w_ref, o_ref, dg_ref, acc):
        j = pl.program_id(0)
        i = pl.program_id(1)

        @pl.when((j == 0) & (i == 0))
        def _():
            dg_ref[...] = jnp.zeros(dg_ref.shape, F32)

        @pl.when(i == 0)
        def _():
            acc[...] = jnp.zeros(acc.shape, F32)

        @pl.when(j < 3)
        def _():
            acc[...] += _dot(dqkv_ref[...], xn_ref[...], TN)

        @pl.when(j >= 3)
        def _():
            acc[...] += _dot(dr_ref[...], xn_ref[...], TN)

        @pl.when(i == nk - 1)
        def _():
            m = acc[...]
            o_ref[...] = (m * g_ref[...]).astype(BF16)
            dg_ref[...] += jnp.sum(m * w_ref[...].astype(F32), axis=0, keepdims=True)

    return pl.pallas_call(
        body, name="proj_bwd_w",
        grid=(N_COL_BLOCKS, nk),
        in_specs=[pl.BlockSpec((tk, D_MODEL), lambda j, i: (i, 0)),
                  pl.BlockSpec((None, tk, COL_BLOCK),
                               lambda j, i: (jnp.minimum(j, 2), jnp.where(j < 3, i, nk - 1), 0)),
                  pl.BlockSpec((tk, COL_BLOCK),
                               lambda j, i: (jnp.where(j >= 3, i, 0), jnp.maximum(j - 3, 0))),
                  pl.BlockSpec((1, D_MODEL), lambda j, i: (0, 0)),
                  pl.BlockSpec((COL_BLOCK, D_MODEL), lambda j, i: (j, 0))],
        out_specs=(pl.BlockSpec((COL_BLOCK, D_MODEL), lambda j, i: (j, 0)),
                   pl.BlockSpec((1, D_MODEL), lambda j, i: (0, 0))),
        out_shape=(jax.ShapeDtypeStruct((D_IN, D_MODEL), BF16), jax.ShapeDtypeStruct((1, D_MODEL), F32)),
        scratch_shapes=[pltpu.VMEM((COL_BLOCK, D_MODEL), F32)],
        compiler_params=_params(40),
    )(xn, dqkv, drest, norm_g, w_in_t)


def _adamw_math(w, g, m, v):
    c1 = 1.0 - ADAM_B1 ** ADAM_STEP
    c2 = 1.0 - ADAM_B2 ** ADAM_STEP
    nm = ADAM_B1 * m + (1.0 - ADAM_B1) * g
    nv = ADAM_B2 * v + (1.0 - ADAM_B2) * (g * g)
    return -ADAM_LR * ((nm / c1) / (jnp.sqrt(nv / c2) + ADAM_EPS) + ADAM_WD * w), nm, nv


def _adamw(name, w, g, m, v, from_chip):
    rows, cols = w.shape
    tr = rows if rows * cols <= 512 * 1024 else next(t for t in range(256, 7, -8) if rows % t == 0)

    def body(w_ref, g_ref, m_ref, v_ref, t_ref, g_out, d_ref, nm_ref, nv_ref):
        gg = g_ref[...]
        for j in range(from_chip.shape[0]):
            gg = gg + t_ref[j].astype(F32)
        g_out[...] = gg
        d_ref[...], nm_ref[...], nv_ref[...] = _adamw_math(w_ref[...], gg, m_ref[...], v_ref[...])

    spec = pl.BlockSpec((tr, cols), lambda i: (i, 0))
    shape = jax.ShapeDtypeStruct((rows, cols), F32)
    return pl.pallas_call(
        body, name=name,
        grid=(rows // tr,),
        in_specs=[spec] * 4 + [pl.BlockSpec((from_chip.shape[0], tr, cols), lambda i: (0, i, 0))],
        out_specs=(spec,) * 4, out_shape=(shape,) * 4,
        compiler_params=_params(32),
    )(w, g, m, v, from_chip)


_SMALL = (("norm_g", (1, D_MODEL)), ("b_gate", (1, 2 * D_MODEL)), ("rel_bias", (N_HEADS, N_REL)),
          ("sgu_ln_g", (1, D_B)), ("sgu_ln_b", (1, D_B)), ("w_s", (N_GROUPS * SGU_CHUNK, SGU_CHUNK)),
          ("b_s", (N_GROUPS, SGU_CHUNK)), ("final_g", (1, D_MODEL)))


def _adamw_small(slabs, ws_all, late_slabs, weights, moments_m, moments_v):
    k = len(_SMALL)

    def total(ref):
        acc = ref[0]
        for d in range(1, N_DEV):
            acc = acc + ref[d]
        return acc

    def body(*refs):
        slab_ref, ws_ref, late_ref = refs[:3]
        w_refs, m_refs, v_refs = refs[3:3 + k], refs[3 + k:3 + 2 * k], refs[3 + 2 * k:3 + 3 * k]
        outs = refs[3 + 3 * k:]
        slab, late = total(slab_ref), total(late_ref)
        grads = {
            "norm_g": late[ROW_NORM_G:ROW_NORM_G + 1, :],
            "b_gate": jnp.concatenate([slab[ROW_B_GATE:ROW_B_GATE + 1, :], slab[ROW_B_GATE + 1:ROW_B_GATE + 2, :]], axis=1),
            "rel_bias": late[ROW_REL:ROW_REL + N_HEADS, :N_REL],
            "sgu_ln_g": slab[ROW_LN_G:ROW_LN_G + 1, :D_B],
            "sgu_ln_b": slab[ROW_LN_B:ROW_LN_B + 1, :D_B],
            "w_s": total(ws_ref),
            "b_s": slab[ROW_B_S:ROW_B_S + N_GROUPS, :SGU_CHUNK],
            "final_g": slab[ROW_FINAL_G:ROW_FINAL_G + 1, :],
        }
        for n, (name, _) in enumerate(_SMALL):
            g = grads[name]
            outs[n][...] = g
            outs[k + n][...], outs[2 * k + n][...], outs[3 * k + n][...] = _adamw_math(
                w_refs[n][...], g, m_refs[n][...], v_refs[n][...])
        outs[4 * k][...] = slab[ROW_LOSS:ROW_LOSS + 1, :1]

    vmem = pl.BlockSpec(memory_space=pltpu.VMEM)
    shapes = tuple(jax.ShapeDtypeStruct(shape, F32) for _, shape in _SMALL)
    return pl.pallas_call(
        body, name="adamw_small",
        out_shape=shapes * 4 + (jax.ShapeDtypeStruct((1, 1), F32),),
        in_specs=[vmem] * (3 + 3 * k), out_specs=tuple([vmem] * (4 * k + 1)),
        compiler_params=_params(16),
    )(slabs, ws_all, late_slabs, *weights, *moments_m, *moments_v)


def _pad_rel(a):
    return jnp.pad(a.reshape(N_HEADS, N_REL), ((0, 0), (0, N_REL_PAD - N_REL)))


def kernel(x, norm_g, w_in, b_gate, rel_bias, sgu_ln_g, sgu_ln_b, w_s, b_s, w_pa, w_pb, w_out, final_g, loss_target, m_norm_g, m_w_in, m_b_gate, m_rel_bias, m_sgu_ln_g, m_sgu_ln_b, m_w_s, m_b_s, m_w_pa, m_w_pb, m_w_out, m_final_g, v_norm_g, v_w_in, v_b_gate, v_rel_bias, v_sgu_ln_g, v_sgu_ln_b, v_w_s, v_b_s, v_w_pa, v_w_pb, v_w_out, v_final_g):
    s = x.shape[1]
    xs = x.reshape(s, D_MODEL)
    tgt = loss_target.reshape(s, D_MODEL)

    bias_table = _bias_table(_pad_rel(rel_bias))
    w_in_t = jnp.swapaxes(w_in[0], 0, 1)
    qkv, x_norm, w_in_t_full = _gather_proj_fwd(xs, norm_g, w_in_t)
    attn_out, g_pa, g_pb, g_out = _attn_fwd(qkv, bias_table, (w_pa[0], w_pb[0], w_out[0]))
    w_pa_full = jnp.transpose(g_pa, (1, 0, 2)).reshape(D_A, D_MODEL)
    w_pb_full = jnp.transpose(g_pb, (1, 0, 2)).reshape(D_B, D_MODEL)
    w_out_full = g_out.reshape(D_MODEL, D_MODEL)

    (dx2, d_attn, drest, dw_out, dw_pa, dw_pb, d_bgate, d_fg, d_lng, d_lnb, d_ws, d_bs, loss_part) = _mid_fwd_bwd(
        xs, tgt, attn_out, qkv, w_pa_full, w_pb_full, w_out_full, b_gate, sgu_ln_g, sgu_ln_b, w_s[0],
        b_s.reshape(N_GROUPS, SGU_CHUNK, 1), final_g.reshape(1, D_MODEL))

    own_pa, own_pb, own_out, tc_pa, tc_pb, tc_out = _reduce_chip(
        "reduce_chip_proj", (dw_pa, dw_pb, dw_out), (1, 1, 0))
    dqkv, dbias, fc_pa, fc_pb, fc_out, slabs, ws_all = _attn_bwd(
        qkv, bias_table, d_attn, (tc_pa, tc_pb, tc_out),
        (d_bgate, d_lng, d_lnb, d_fg, loss_part, d_bs, d_ws.reshape(N_GROUPS * SGU_CHUNK, SGU_CHUNK)))
    d_rel = _bias_grad(dbias)
    dw_in_t, d_ng = _proj_bwd_w(x_norm, dqkv, drest, norm_g, w_in_t_full)
    own_in, tc_in = _reduce_chip("reduce_chip_in", (dw_in_t,), (0,))
    grad_x, fc_in, late_slabs = _proj_bwd_x(dqkv, drest, xs, dx2, norm_g, w_in_t_full, tc_in, (d_ng, d_rel))
    big = {"w_in": tuple(jnp.swapaxes(t, 0, 1)[None] for t in _adamw(
        "adamw_w_in", w_in_t, own_in, jnp.swapaxes(m_w_in[0], 0, 1), jnp.swapaxes(v_w_in[0], 0, 1), fc_in))}
    for name, w, g, fc, m, v in (("w_pa", w_pa, own_pa, fc_pa, m_w_pa, v_w_pa),
                                 ("w_pb", w_pb, own_pb, fc_pb, m_w_pb, v_w_pb),
                                 ("w_out", w_out, own_out, fc_out, m_w_out, v_w_out)):
        big[name] = tuple(t[None] for t in _adamw("adamw_" + name, w[0], g, m[0], v[0], fc))

    as_2d = lambda leaves: [a.reshape(shape) for a, (_, shape) in zip(leaves, _SMALL)]
    small_out = _adamw_small(
        slabs, ws_all, late_slabs, as_2d((norm_g, b_gate, rel_bias, sgu_ln_g, sgu_ln_b, w_s, b_s, final_g)),
        as_2d((m_norm_g, m_b_gate, m_rel_bias, m_sgu_ln_g, m_sgu_ln_b, m_w_s, m_b_s, m_final_g)),
        as_2d((v_norm_g, v_b_gate, v_rel_bias, v_sgu_ln_g, v_sgu_ln_b, v_w_s, v_b_s, v_final_g)))
    small_index = {name: n for n, (name, _) in enumerate(_SMALL)}

    def leaf(kind, name, like):
        if name in big:
            return big[name][kind]
        return small_out[kind * len(_SMALL) + small_index[name]].reshape(like.shape)

    weights = (("norm_g", norm_g), ("w_in", w_in), ("b_gate", b_gate), ("rel_bias", rel_bias), ("sgu_ln_g", sgu_ln_g),
               ("sgu_ln_b", sgu_ln_b), ("w_s", w_s), ("b_s", b_s), ("w_pa", w_pa), ("w_pb", w_pb), ("w_out", w_out),
               ("final_g", final_g))
    outs = [small_out[-1].reshape(()), grad_x.reshape(x.shape)]
    for kind in range(4):
        outs.extend(leaf(kind, name, like) for name, like in weights)
    return tuple(outs)
```

```python
import functools
import math

import jax
import jax.numpy as jnp
from jax import lax
from jax.experimental import pallas as pl
from jax.experimental.pallas import tpu as pltpu

F32 = jnp.float32
BF16 = jnp.bfloat16
MESH = pl.DeviceIdType.MESH
N_DEV = 8

D_MODEL = 1024
D_A = 512
D_B = 512
D_IN = 5632
N_HEADS = 8
HEAD_DIM = 64
N_PREV = 8
REL_CLIP = 128
N_REL = 2 * REL_CLIP + 1
N_REL_PAD = 384
SGU_CHUNK = 128
N_GROUPS = 4
EPS = 1e-6
NEG_INF = -1e30
Q_SCALE = HEAD_DIM ** -0.5

Q_BLOCK = 256
K_SPAN = 768
Z_PAD = K_SPAN - Q_BLOCK
ROLL_W = 1024
COL_BLOCK = 512
N_COL_BLOCKS = D_IN // COL_BLOCK
REST = D_IN - 3 * D_A
TOKEN_TILE = 256

ADAM_LR = 0.001
ADAM_B1 = 0.9
ADAM_B2 = 0.999
ADAM_EPS = 1e-08
ADAM_WD = 0.01
ADAM_STEP = 10

GELU_C = math.sqrt(2.0 / math.pi)
GELU_A = 0.044715

NT = (((1,), (1,)), ((), ()))
TN = (((0,), (0,)), ((), ()))
HIGHEST = lax.Precision.HIGHEST


def _params(vmem_mb, **kw):
    return pltpu.CompilerParams(vmem_limit_bytes=vmem_mb * 1024 * 1024, **kw)


def _dot(a, b, dims=None):
    if dims is None:
        return jnp.dot(a, b, preferred_element_type=F32)
    return lax.dot_general(a, b, dims, preferred_element_type=F32)


def _sigmoid(x):
    return 0.5 * jnp.tanh(0.5 * x) + 0.5


def _gelu_and_grad(u):
    u2 = u * u
    t = jnp.tanh(GELU_C * (u + GELU_A * u * u2))
    half = 0.5 * (1.0 + t)
    g = u * half
    dg = half + 0.5 * u * (1.0 - t * t) * (GELU_C * (1.0 + 3.0 * GELU_A * u2))
    return g, dg


def _my_pos():
    return lax.axis_index("x"), lax.axis_index("y"), lax.axis_index("c")


def _flat_id(pos):
    return 4 * pos[0] + 2 * pos[1] + pos[2]


def _other_chips(pos):
    x, y, _ = pos
    return ((1 - x, y), (x, 1 - y), (1 - x, 1 - y))


class _SlotGather:
    def __init__(self, bufs, send_sems, recv_sems, own=None):
        self.bufs, self.send_sems, self.recv_sems = bufs, send_sems, recv_sems
        self.own = own if own is not None else [None] * len(bufs)
        x, y, c = _my_pos()
        self.c, self.me, self.sib = c, (x, y, c), (x, y, 1 - c)
        self.chips = _other_chips(self.me)

    def _copy(self, a, k, block, to):
        slot = _flat_id(block)
        src = self.own[a] if (k < 4 and self.own[a] is not None) else self.bufs[a].at[slot]
        return pltpu.make_async_remote_copy(
            src_ref=src, dst_ref=self.bufs[a].at[slot],
            send_sem=self.send_sems.at[a, k], recv_sem=self.recv_sems.at[a, k], device_id=to, device_id_type=MESH)

    def _own_sends(self):
        n = len(self.bufs)
        return ([self._copy(a, 1 + j, self.me, (*chip, self.c)) for j, chip in enumerate(self.chips) for a in range(n)]
                + [self._copy(a, 0, self.me, self.sib) for a in range(n)])

    def _passes(self):
        return [self._copy(a, 4 + j, (*chip, self.c), self.sib)
                for j, chip in enumerate(self.chips) for a in range(len(self.bufs))]

    def start(self):
        for cp in self._own_sends():
            cp.start()

    def pass_on(self):
        for j, chip in enumerate(self.chips):
            for a in range(len(self.bufs)):
                self._copy(a, 1 + j, (*chip, self.c), self.me).wait_recv()
                self._copy(a, 4 + j, (*chip, self.c), self.sib).start()

    def finish(self):
        for a in range(len(self.bufs)):
            self._copy(a, 0, self.sib, self.me).wait_recv()
            for j, chip in enumerate(self.chips):
                self._copy(a, 4 + j, (*chip, 1 - self.c), self.me).wait_recv()
        for cp in self._own_sends() + self._passes():
            cp.wait_send()


def _reduce_chip(name, parts, sharded_dim):
    n = len(parts)
    shapes = []
    for p, dim in zip(parts, sharded_dim):
        shape = list(p.shape)
        shape[dim] //= N_DEV
        shapes.append(tuple(shape))
    staged = [not (dim == 0 and p.dtype == BF16) for p, dim in zip(parts, sharded_dim)]

    def body(*refs):
        full, own, to_chip = refs[:n], refs[n:2 * n], refs[2 * n:3 * n]
        ins, from_sib = refs[3 * n:4 * n], refs[4 * n:5 * n]
        send_sems, recv_sems = refs[5 * n], refs[5 * n + 1]
        x, y, c = _my_pos()
        sib = (x, y, 1 - c)
        chips = ((x, y),) + _other_chips((x, y, c))
        for a in range(n):
            rows, cols = shapes[a]
            for d in range(N_DEV if staged[a] else 0):
                if sharded_dim[a] == 0:
                    ins[a][d] = full[a][d * rows:(d + 1) * rows, :].astype(BF16)
                else:
                    ins[a][d] = full[a][:, d * cols:(d + 1) * cols].astype(BF16)

        def block(a, d):
            if staged[a]:
                return ins[a].at[d]
            rows = shapes[a][0]
            return full[a].at[pl.ds(pl.multiple_of(d * rows, 16), rows), :]

        def to_sibling(a, r):
            return pltpu.make_async_remote_copy(
                src_ref=block(a, _flat_id((*chips[r], 1 - c))), dst_ref=from_sib[a].at[r],
                send_sem=send_sems.at[a, r], recv_sem=recv_sems.at[a, r], device_id=sib, device_id_type=MESH)

        sends = [to_sibling(a, r) for r in (1, 2, 3, 0) for a in range(n)]
        for cp in sends:
            cp.start()
        for r in (1, 2, 3, 0):
            for a in range(n):
                to_sibling(a, r).wait_recv()
                both = block(a, _flat_id((*chips[r], c)))[...].astype(F32) + from_sib[a][r].astype(F32)
                if r == 0:
                    own[a][...] = both
                else:
                    to_chip[a][r - 1] = both.astype(BF16)
        for cp in sends:
            cp.wait_send()

    vmem = pl.BlockSpec(memory_space=pltpu.VMEM)
    return pl.pallas_call(
        body, name=name,
        out_shape=tuple(jax.ShapeDtypeStruct(sh, F32) for sh in shapes)
        + tuple(jax.ShapeDtypeStruct((3,) + sh, BF16) for sh in shapes),
        in_specs=[vmem] * n, out_specs=tuple([vmem] * (2 * n)),
        scratch_shapes=[pltpu.VMEM((N_DEV if st else 1,) + sh, BF16) for sh, st in zip(shapes, staged)]
        + [pltpu.VMEM((4,) + sh, BF16) for sh in shapes]
        + [pltpu.SemaphoreType.DMA((n, 4)), pltpu.SemaphoreType.DMA((n, 4))],
        compiler_params=_params(56),
    )(*parts)


def _owner_copies(to_chip, from_chip, send_sems, recv_sems):
    x, y, c = _my_pos()
    return [pltpu.make_async_remote_copy(
        src_ref=to_chip[a].at[j], dst_ref=from_chip[a].at[j],
        send_sem=send_sems.at[a, j], recv_sem=recv_sems.at[a, j], device_id=(*chip, c), device_id_type=MESH)
        for a in range(len(to_chip)) for j, chip in enumerate(_other_chips((x, y, c)))]


ROW_B_GATE, ROW_LN_G, ROW_LN_B, ROW_FINAL_G, ROW_LOSS, ROW_B_S, SLAB_ROWS = 1, 3, 4, 5, 6, 16, 24
ROW_NORM_G, ROW_REL, LATE_SLAB_ROWS = 0, 8, 16


def _rel_index(e):
    lo, hi = Z_PAD - REL_CLIP, Z_PAD + REL_CLIP
    return jnp.where(e <= lo, 2 * REL_CLIP, jnp.where(e < hi, hi - e, jnp.where(e <= K_SPAN, 0, 2 * REL_CLIP)))


def _bias_table(rel_bias_pad):
    def body(rb_ref, bt_ref):
        c = lax.broadcasted_iota(jnp.int32, (N_REL_PAD, ROLL_W), 1)
        r = lax.broadcasted_iota(jnp.int32, (N_REL_PAD, ROLL_W), 0)
        pick = (r == _rel_index(c)).astype(F32)
        rows = jnp.dot(rb_ref[...], pick, precision=HIGHEST, preferred_element_type=F32)
        qc = lax.broadcasted_iota(jnp.int32, (Q_BLOCK, K_SPAN), 0) >> 6
        kc = lax.broadcasted_iota(jnp.int32, (Q_BLOCK, K_SPAN), 1) >> 6
        band = (kc >= qc) & (kc <= qc + N_PREV)
        for h in range(N_HEADS):
            t = jnp.broadcast_to(rows[h:h + 1, :], (Q_BLOCK, ROLL_W))
            t = pltpu.roll(t, 0, 1, stride=1, stride_axis=0)
            bt_ref[h] = jnp.where(band, t[:, :K_SPAN], NEG_INF)

    return pl.pallas_call(
        body, name="bias_table",
        out_shape=jax.ShapeDtypeStruct((N_HEADS, Q_BLOCK, K_SPAN), F32),
        compiler_params=_params(32),
    )(rel_bias_pad)


def _bias_grad(dbias):
    def body(a_ref, o_ref):
        rr = lax.broadcasted_iota(jnp.int32, (Q_BLOCK, Q_BLOCK), 0)
        cc = lax.broadcasted_iota(jnp.int32, (Q_BLOCK, Q_BLOCK), 1)
        flip = (rr + cc == Q_BLOCK - 1).astype(F32)
        c = lax.broadcasted_iota(jnp.int32, (ROLL_W, N_REL_PAD), 0)
        r = lax.broadcasted_iota(jnp.int32, (ROLL_W, N_REL_PAD), 1)
        e = jnp.where(c >= Q_BLOCK - 1, c - (Q_BLOCK - 1), c + (ROLL_W - Q_BLOCK + 1))
        pick = (r == _rel_index(e)).astype(F32)
        sums = []
        for h in range(N_HEADS):
            a = jnp.dot(flip, a_ref[h], precision=HIGHEST, preferred_element_type=F32)
            a = jnp.concatenate([a, jnp.zeros((Q_BLOCK, ROLL_W - K_SPAN), F32)], axis=1)
            a = pltpu.roll(a, 0, 1, stride=1, stride_axis=0)
            sums.append(jnp.sum(a, axis=0, keepdims=True))
        diag = jnp.concatenate(sums, axis=0)
        o_ref[...] = jnp.dot(diag, pick, precision=HIGHEST, preferred_element_type=F32)

    return pl.pallas_call(
        body, name="bias_grad",
        out_shape=jax.ShapeDtypeStruct((N_HEADS, N_REL_PAD), F32),
        compiler_params=_params(32),
    )(dbias)


def _gather_proj_fwd(x, norm_g, w_in_t):
    s = x.shape[0]
    tm = 512 if s % 512 == 0 else TOKEN_TILE
    nt = s // tm
    n_pad = Z_PAD // tm
    shard_w = w_in_t.shape[0]
    chip_w = 2 * shard_w
    n_chips = N_DEV // 2

    def body(order_ref, x_ref, g_ref, win_hbm, z_ref, xn_ref, wt_hbm, wt, hb, win_f32, send_sems, recv_sems,
             local_sems):
        j = pl.program_id(0)
        i = pl.program_id(1)
        x_, y_, c_ = _my_pos()
        me, sib = (x_, y_, c_), (x_, y_, 1 - c_)
        near = _other_chips(me)
        pick = lambda a, b: tuple(jnp.where(c_ == 0, u, v) for u, v in zip(a, b))
        passed_from, passed_to = pick(near[0], near[1]), pick(near[1], near[0])

        def rows_of(block):
            return wt.at[pl.ds(pl.multiple_of(_flat_id(block) * shard_w, 16), shard_w), :]

        def copy(k, block, to):
            return pltpu.make_async_remote_copy(
                src_ref=rows_of(block), dst_ref=rows_of(block),
                send_sem=send_sems.at[k], recv_sem=recv_sems.at[k], device_id=to, device_id_type=MESH)

        def sends():
            return ([copy(0, me, sib), copy(1, me, (*near[0], c_)), copy(2, me, (*near[1], c_)),
                     copy(3, (*passed_from, c_), (*passed_to, c_))]
                    + [copy(4 + n, (*near[n], c_), sib) for n in range(3)])

        keep = pltpu.make_async_copy(wt, wt_hbm, local_sems.at[0])

        @pl.when((j == 0) & (i == 0))
        def _():
            load = pltpu.make_async_copy(win_hbm, win_f32, local_sems.at[1])
            load.start()
            load.wait()
            rows_of(me)[...] = win_f32[...].astype(BF16)
            for cp in sends()[:3]:
                cp.start()
            copy(0, sib, me).wait_recv()

        @pl.when((j == 1) & (i == 0))
        def _():
            copy(1, (*near[0], c_), me).wait_recv()
            copy(2, (*near[1], c_), me).wait_recv()
            for cp in sends()[3:6]:
                cp.start()
            copy(4, (*near[0], 1 - c_), me).wait_recv()

        @pl.when((j == 2) & (i == 0))
        def _():
            copy(5, (*near[1], 1 - c_), me).wait_recv()

        @pl.when((j == 3) & (i == 0))
        def _():
            copy(3, (*near[2], c_), me).wait_recv()
            copy(6, (*near[2], c_), sib).start()
            copy(6, (*near[2], 1 - c_), me).wait_recv()
            keep.start()

        @pl.when(i < n_pad)
        def _():
            z_ref[...] = jnp.zeros(z_ref.shape, BF16)

        @pl.when(i >= n_pad)
        def _():
            rows = pl.ds(pl.multiple_of((i - n_pad) * tm, tm), tm)

            @pl.when(j == 0)
            def _():
                xf = x_ref[...]
                xn = xf * lax.rsqrt(jnp.mean(xf * xf, axis=-1, keepdims=True) + EPS)
                hb[rows, :] = (xn * g_ref[...]).astype(BF16)
                xn_ref[...] = xn.astype(BF16)

            chip_rows = pl.ds(pl.multiple_of(order_ref[j] * chip_w, 16), chip_w)
            blk = _dot(hb[rows, :], wt[chip_rows, :], NT)
            q_scale = jnp.where(order_ref[j] == 0, Q_SCALE, 1.0).astype(F32)
            z_ref[:, :D_A] = (blk[:, :D_A] * q_scale).astype(BF16)
            z_ref[:, D_A:] = blk[:, D_A:].astype(BF16)

        @pl.when((j == n_chips - 1) & (i == n_pad + nt - 1))
        def _():
            keep.wait()
            for cp in sends():
                cp.wait_send()

    pos = _my_pos()
    order = jnp.stack([2 * cx + cy for cx, cy in ((pos[0], pos[1]),) + _other_chips(pos)]).astype(jnp.int32)
    first_pass = lambda j, i: jnp.where(j == 0, jnp.maximum(i - n_pad, 0), nt - 1)
    grid_spec = pltpu.PrefetchScalarGridSpec(
        num_scalar_prefetch=1,
        grid=(n_chips, n_pad + nt),
        in_specs=[pl.BlockSpec((tm, D_MODEL), lambda j, i, o: (first_pass(j, i), 0)),
                  pl.BlockSpec((1, D_MODEL), lambda j, i, o: (0, 0)),
                  pl.BlockSpec(memory_space=pl.ANY)],
        out_specs=(pl.BlockSpec((tm, chip_w), lambda j, i, o: (i, o[j])),
                   pl.BlockSpec((tm, D_MODEL), lambda j, i, o: (first_pass(j, i), 0)),
                   pl.BlockSpec(memory_space=pl.ANY)),
        scratch_shapes=[pltpu.VMEM((D_IN, D_MODEL), BF16),
                        pltpu.VMEM((s, D_MODEL), BF16), pltpu.VMEM(w_in_t.shape, F32),
                        pltpu.SemaphoreType.DMA((N_DEV - 1,)), pltpu.SemaphoreType.DMA((N_DEV - 1,)),
                        pltpu.SemaphoreType.DMA((2,))])
    return pl.pallas_call(
        body, name="gather_proj_fwd",
        grid_spec=grid_spec,
        out_shape=(jax.ShapeDtypeStruct((Z_PAD + s, D_IN), BF16), jax.ShapeDtypeStruct((s, D_MODEL), BF16),
                   jax.ShapeDtypeStruct((D_IN, D_MODEL), BF16)),
        compiler_params=_params(60),
    )(order, x, norm_g, w_in_t)


def _attn_specs(rows):
    pairs = N_HEADS // 2
    return ([pl.BlockSpec((rows, 128), functools.partial(lambda which, p: (0, which * pairs + p), which))
             for which in range(3)]
            + [pl.BlockSpec((2, Q_BLOCK, K_SPAN), lambda p: (p, 0, 0))])


def _head_masks():
    lane = lax.broadcasted_iota(jnp.int32, (1, 128), 1)
    first = lane < HEAD_DIM
    return (first, jnp.logical_not(first))


def _stack_heads(x, masks):
    zero = jnp.zeros((), x.dtype)
    return jnp.concatenate([jnp.where(m, x, zero) for m in masks], axis=0)


STRIP = 16


def _softmax_strips(s_ref, bias_ref, b):
    valid = lax.broadcasted_iota(jnp.int32, (1, K_SPAN), 1) >= Z_PAD - b * Q_BLOCK
    for t in range(2 * Q_BLOCK // STRIP):
        hh, r = divmod(t * STRIP, Q_BLOCK)
        st = s_ref[t * STRIP:(t + 1) * STRIP, :] + bias_ref[hh, r:r + STRIP, :]
        st = jnp.where(valid, st, NEG_INF)
        e = jnp.exp(st - jnp.max(st, axis=-1, keepdims=True))
        yield e * (1.0 / jnp.sum(e, axis=-1, keepdims=True))


def _side_by_side_strips(strips):
    half = len(strips) // 2
    return jnp.concatenate([jnp.concatenate([a, c], axis=1) for a, c in zip(strips[:half], strips[half:])], axis=0)


def _attn_fwd(qkv, bias_table, shards):
    s = qkv.shape[0] - Z_PAD
    nb = s // Q_BLOCK
    n = len(shards)
    pairs = N_HEADS // 2

    def body(*refs):
        q_ref, k_ref, v_ref, bt_ref = refs[:4]
        shard_refs = refs[4:4 + n]
        o_ref = refs[4 + n]
        slot_refs = refs[5 + n:5 + 2 * n]
        stages = refs[5 + 2 * n:5 + 3 * n]
        s_scr, send_sems, recv_sems, local_sems = refs[5 + 3 * n:]
        p_id = pl.program_id(0)
        gather = _SlotGather(slot_refs, send_sems, recv_sems, own=stages)
        keep = [pltpu.make_async_copy(stages[a], slot_refs[a].at[_flat_id(_my_pos())], local_sems.at[a])
                for a in range(n)]

        @pl.when(p_id == 0)
        def _():
            for a in range(n):
                stages[a][...] = shard_refs[a][...].astype(BF16)
                keep[a].start()
            gather.start()

        @pl.when(p_id == 2)
        def _():
            gather.pass_on()

        masks = _head_masks()

        def scores(b, half):
            r0 = pl.multiple_of(b * Q_BLOCK, Q_BLOCK)
            q2 = _stack_heads(q_ref[pl.ds(r0 + Z_PAD, Q_BLOCK), :], masks)
            s_scr[half] = _dot(q2, k_ref[pl.ds(r0, K_SPAN), :], NT)

        def finish(b, half):
            r0 = pl.multiple_of(b * Q_BLOCK, Q_BLOCK)
            v2 = _stack_heads(v_ref[pl.ds(r0, K_SPAN), :], masks)
            p = [st.astype(BF16) for st in _softmax_strips(s_scr.at[half], bt_ref, b)]
            o_ref[pl.ds(r0, Q_BLOCK), :] = _dot(_side_by_side_strips(p), v2)

        def two_blocks(i, carry):
            b = 2 * i
            scores(b + 1, 1)
            finish(b, 0)
            scores(jnp.minimum(b + 2, nb - 1), 0)
            finish(b + 1, 1)
            return carry

        scores(0, 0)
        lax.fori_loop(0, nb // 2, two_blocks, 0)

        @pl.when(p_id == pairs - 1)
        def _():
            gather.finish()
            for cp in keep:
                cp.wait()

    hbm = pl.BlockSpec(memory_space=pl.ANY)
    return pl.pallas_call(
        body, name="attn_fwd",
        grid=(pairs,),
        in_specs=_attn_specs(s + Z_PAD) + [pl.BlockSpec(a.shape, lambda p: (0, 0)) for a in shards],
        out_specs=(pl.BlockSpec((s, 128), lambda p: (0, p)),) + (hbm,) * n,
        out_shape=(jax.ShapeDtypeStruct((s, D_A), F32),)
        + tuple(jax.ShapeDtypeStruct((N_DEV,) + a.shape, BF16) for a in shards),
        scratch_shapes=[pltpu.VMEM(a.shape, BF16) for a in shards]
        + [pltpu.VMEM((2, 2 * Q_BLOCK, K_SPAN), F32),
           pltpu.SemaphoreType.DMA((n, N_DEV - 1)), pltpu.SemaphoreType.DMA((n, N_DEV - 1)),
           pltpu.SemaphoreType.DMA((n,))],
        compiler_params=_params(48),
    )(qkv, qkv, qkv, bias_table, *shards)


def _fill_slab(stage, rows):
    stage[...] = jnp.zeros(stage.shape, F32)
    for row, ref in rows:
        r, c = ref.shape
        if c > D_MODEL:
            for part in range(c // D_MODEL):
                stage[row + part:row + part + 1, :] = ref[:, part * D_MODEL:(part + 1) * D_MODEL]
        else:
            stage[row:row + r, :c] = ref[...]


def _attn_bwd(qkv, bias_table, d_out, to_chip, small):
    s = qkv.shape[0] - Z_PAD
    nb = s // Q_BLOCK
    n = len(to_chip)
    pairs = N_HEADS // 2
    ws_shape = small[-1].shape

    def body(*refs):
        q_ref, k_ref, v_ref, bt_ref, do_ref = refs[:5]
        to_chip_refs = refs[5:5 + n]
        bg_ref, lng_ref, lnb_ref, fg_ref, loss_ref, bs_ref, ws_ref = refs[5 + n:12 + n]
        dqkv_ref, db_ref = refs[12 + n:14 + n]
        from_chip_refs = refs[14 + n:14 + 2 * n]
        slab_land, ws_land = refs[14 + 2 * n:16 + 2 * n]
        (dk_acc, dv_acc, s_scr, dp_scr, slab_stage, ws_stage, send_sems, recv_sems, gather_send, gather_recv,
         keep_sems) = refs[16 + 2 * n:]
        p_id = pl.program_id(0)
        me = _flat_id(_my_pos())
        gather = _SlotGather([slab_land, ws_land], gather_send, gather_recv, own=[slab_stage, ws_stage])
        keep = [pltpu.make_async_copy(stage, land.at[me], keep_sems.at[k]) for k, (stage, land) in enumerate(
            ((slab_stage, slab_land), (ws_stage, ws_land)))]

        @pl.when(p_id == 0)
        def _():
            _fill_slab(slab_stage, ((ROW_B_GATE, bg_ref), (ROW_LN_G, lng_ref), (ROW_LN_B, lnb_ref),
                                    (ROW_FINAL_G, fg_ref), (ROW_LOSS, loss_ref)))
            eye = (lax.broadcasted_iota(jnp.int32, (SGU_CHUNK, SGU_CHUNK), 0)
                   == lax.broadcasted_iota(jnp.int32, (SGU_CHUNK, SGU_CHUNK), 1))
            for g in range(N_GROUPS):
                row = jnp.sum(jnp.where(eye, bs_ref[g], 0.0), axis=0, keepdims=True)
                slab_stage[ROW_B_S + g:ROW_B_S + g + 1, :SGU_CHUNK] = row
            ws_stage[...] = ws_ref[...]
            for cp in keep:
                cp.start()
            gather.start()
            for cp in _owner_copies(to_chip_refs, from_chip_refs, send_sems, recv_sems):
                cp.start()

        @pl.when(p_id == 2)
        def _():
            gather.pass_on()

        dk_acc[...] = jnp.zeros(dk_acc.shape, F32)
        dv_acc[...] = jnp.zeros(dv_acc.shape, F32)
        db_ref[...] = jnp.zeros(db_ref.shape, F32)
        masks = _head_masks()

        def operands(b):
            r0 = pl.multiple_of(b * Q_BLOCK, Q_BLOCK)
            q2 = _stack_heads(q_ref[pl.ds(r0 + Z_PAD, Q_BLOCK), :], masks)
            do2 = _stack_heads(do_ref[pl.ds(r0, Q_BLOCK), :], masks)
            return r0, q2, do2, k_ref[pl.ds(r0, K_SPAN), :]

        def ahead(b, half):
            r0, q2, do2, kcat = operands(b)
            s_scr[half] = _dot(q2, kcat, NT)
            dp_scr[half] = _dot(do2, v_ref[pl.ds(r0, K_SPAN), :], NT)

        def finish(b, half):
            r0, q2, do2, kcat = operands(b)
            p_strips, ds_strips = [], []
            for t, p in enumerate(_softmax_strips(s_scr.at[half], bt_ref, b)):
                hh, r = divmod(t * STRIP, Q_BLOCK)
                dp_t = dp_scr[half, t * STRIP:(t + 1) * STRIP, :]
                ds = p * (dp_t - jnp.sum(p * dp_t, axis=-1, keepdims=True))
                db_ref[hh, r:r + STRIP, :] += ds
                p_strips.append(p.astype(BF16))
                ds_strips.append(ds.astype(BF16))
            dq = _dot(_side_by_side_strips(ds_strips), _stack_heads(kcat, masks))
            dqkv_ref[0, pl.ds(r0, Q_BLOCK), :] = (dq * Q_SCALE).astype(BF16)
            dk_acc[pl.ds(r0, K_SPAN), :] += _dot(jnp.concatenate(ds_strips, axis=0), q2, TN)
            dv_acc[pl.ds(r0, K_SPAN), :] += _dot(jnp.concatenate(p_strips, axis=0), do2, TN)

        def two_blocks(i, carry):
            b = 2 * i
            ahead(b + 1, 1)
            finish(b, 0)
            ahead(jnp.minimum(b + 2, nb - 1), 0)
            finish(b + 1, 1)
            return carry

        ahead(0, 0)
        lax.fori_loop(0, nb // 2, two_blocks, 0)
        dqkv_ref[1] = dk_acc[Z_PAD:, :].astype(BF16)
        dqkv_ref[2] = dv_acc[Z_PAD:, :].astype(BF16)

        @pl.when(p_id == pairs - 1)
        def _():
            gather.finish()
            for cp in keep:
                cp.wait()
            for cp in _owner_copies(to_chip_refs, from_chip_refs, send_sems, recv_sems):
                cp.wait_recv()
                cp.wait_send()

    hbm = pl.BlockSpec(memory_space=pl.ANY)
    lands = ((N_DEV, SLAB_ROWS, D_MODEL), (N_DEV,) + ws_shape)
    return pl.pallas_call(
        body, name="attn_bwd",
        grid=(pairs,),
        in_specs=_attn_specs(s + Z_PAD) + [pl.BlockSpec((s, 128), lambda p: (0, p))] + [hbm] * n
        + [pl.BlockSpec(a.shape, functools.partial(lambda nd, p: (0,) * nd, a.ndim)) for a in small],
        out_specs=(pl.BlockSpec((3, s, 128), lambda p: (0, 0, p)),
                   pl.BlockSpec((2, Q_BLOCK, K_SPAN), lambda p: (p, 0, 0))) + (hbm,) * (n + 2),
        out_shape=(jax.ShapeDtypeStruct((3, s, D_A), BF16),
                   jax.ShapeDtypeStruct((N_HEADS, Q_BLOCK, K_SPAN), F32))
        + tuple(jax.ShapeDtypeStruct(t.shape, t.dtype) for t in to_chip)
        + tuple(jax.ShapeDtypeStruct(shape, F32) for shape in lands),
        scratch_shapes=[pltpu.VMEM((s + Z_PAD, 128), F32), pltpu.VMEM((s + Z_PAD, 128), F32),
                        pltpu.VMEM((2, 2 * Q_BLOCK, K_SPAN), F32), pltpu.VMEM((2, 2 * Q_BLOCK, K_SPAN), F32)]
        + [pltpu.VMEM(shape[1:], F32) for shape in lands]
        + [pltpu.SemaphoreType.DMA((n, 3)), pltpu.SemaphoreType.DMA((n, 3)),
           pltpu.SemaphoreType.DMA((2, N_DEV - 1)), pltpu.SemaphoreType.DMA((2, N_DEV - 1)),
           pltpu.SemaphoreType.DMA((2,))],
        compiler_params=_params(56),
    )(qkv, qkv, qkv, bias_table, d_out, *to_chip, *small)


def _mid_fwd_bwd(x, target, attn_out, z, w_pa, w_pb, w_out, b_gate, ln_g, ln_b, w_s, b_s, final_g):
    s = x.shape[0]
    tm = TOKEN_TILE
    nt = s // tm

    def body(x_ref, t_ref, oa_ref, ga_ref, ub_ref, vb_ref, gb_ref, ta0_ref, ta1_ref, tb0_ref, tb1_ref,
             wpa_hbm, wpb_hbm, wout_hbm, bg_ref, lng_ref, lnb_ref, ws_ref, bs_ref, fg_ref,
             dx2_ref, doa_ref, dz_ref, dwout_hbm, dwpa_hbm, dwpb_hbm, dbg_ref, dfg_ref, dlng_ref, dlnb_ref, dws_ref,
             dbs_ref, loss_ref,
             wpa, wpb, wout, wmix, acc_out, acc_pa, acc_pb, sem):
        i = pl.program_id(0)

        @pl.when(i == 0)
        def _():
            loads = [pltpu.make_async_copy(src, dst, sem.at[n])
                     for n, (src, dst) in enumerate(((wpa_hbm, wpa), (wpb_hbm, wpb), (wout_hbm, wout)))]
            for cp in loads:
                cp.start()
            t_idx = lax.broadcasted_iota(jnp.int32, (SGU_CHUNK, SGU_CHUNK), 0)
            s_idx = lax.broadcasted_iota(jnp.int32, (SGU_CHUNK, SGU_CHUNK), 1)
            for g in range(N_GROUPS):
                wmix[g] = jnp.where(s_idx <= t_idx, ws_ref[g], 0.0).astype(BF16)
            for ref in (acc_out, acc_pa, acc_pb, dbg_ref, dfg_ref, dlng_ref, dlnb_ref, dws_ref, dbs_ref, loss_ref):
                ref[...] = jnp.zeros(ref.shape, F32)
            for cp in loads:
                cp.wait()

        def tile_fwd_bwd(rows):
            g_a = ga_ref[rows, :].astype(F32)
            u_b = ub_ref[rows, :].astype(F32)
            v_b = vb_ref[rows, :].astype(F32)
            g_b = gb_ref[rows, :].astype(F32)
            bg = bg_ref[...]
            sg_a = _sigmoid(g_a)
            silu_a = g_a * sg_a
            o_a = oa_ref[rows, :]
            y_a = (o_a * silu_a).astype(BF16)
            ug, dgelu_u = _gelu_and_grad(u_b)
            vg, dgelu_v = _gelu_and_grad(v_b)
            mu = jnp.mean(vg, axis=-1, keepdims=True)
            vc = vg - mu
            rstd = lax.rsqrt(jnp.mean(vc * vc, axis=-1, keepdims=True) + EPS)
            vhat = vc * rstd
            lng = lng_ref[...]
            vn = (vhat * lng + lnb_ref[...]).astype(BF16)
            sg_b = _sigmoid(g_b)
            silu_b = g_b * sg_b
            subs = [slice(n * SGU_CHUNK, (n + 1) * SGU_CHUNK) for n in range(tm // SGU_CHUNK)]
            mixed = jnp.concatenate([jnp.concatenate(
                [_dot(wmix[g], vn[sub, g * 128:(g + 1) * 128]) + bs_ref[g] for g in range(N_GROUPS)], axis=1)
                for sub in subs], axis=0)
            um = ug * mixed
            y_b = (um * silu_b).astype(BF16)
            gate_a = _sigmoid(jnp.concatenate([ta0_ref[rows, :], ta1_ref[rows, :]], axis=1).astype(F32)
                              + bg[:, :D_MODEL])
            gate_b = _sigmoid(jnp.concatenate([tb0_ref[rows, :], tb1_ref[rows, :]], axis=1).astype(F32)
                              + bg[:, D_MODEL:])
            p_a = _dot(y_a, wpa[...])
            p_b = _dot(y_b, wpb[...])
            merged = (gate_a * p_a + gate_b * p_b).astype(BF16)
            x2 = x_ref[rows, :] + _dot(merged, wout[...])
            r2 = lax.rsqrt(jnp.mean(x2 * x2, axis=-1, keepdims=True) + EPS)
            xh = x2 * r2
            fg = fg_ref[...]
            err = xh * fg - t_ref[rows, :]
            loss_ref[...] += jnp.sum(jnp.sum(err * err, axis=-1, keepdims=True), axis=0, keepdims=True) * (0.5 / D_MODEL)
            dy = err * (1.0 / D_MODEL)
            dfg_ref[...] += jnp.sum(dy * xh, axis=0, keepdims=True)
            gy = dy * fg
            dx2 = r2 * (gy - xh * jnp.mean(gy * xh, axis=-1, keepdims=True))
            dx2_ref[rows, :] = dx2
            dx2b = dx2.astype(BF16)
            dmerged = _dot(dx2b, wout[...], NT)
            acc_out[...] += _dot(merged, dx2b, TN)
            dp_a = dmerged * gate_a
            dp_b = dmerged * gate_b
            dgate_a = dp_a * p_a * (1.0 - gate_a)
            dgate_b = dp_b * p_b * (1.0 - gate_b)
            dbg_ref[:, :D_MODEL] += jnp.sum(dgate_a, axis=0, keepdims=True)
            dbg_ref[:, D_MODEL:] += jnp.sum(dgate_b, axis=0, keepdims=True)
            dz_ref[rows, 2048:3072] = dgate_a.astype(BF16)
            dz_ref[rows, 3072:4096] = dgate_b.astype(BF16)
            dp_ab = dp_a.astype(BF16)
            dp_bb = dp_b.astype(BF16)
            dy_a = _dot(dp_ab, wpa[...], NT)
            dy_b = _dot(dp_bb, wpb[...], NT)
            acc_pa[...] += _dot(y_a, dp_ab, TN)
            acc_pb[...] += _dot(y_b, dp_bb, TN)
            doa_ref[rows, :] = (dy_a * silu_a).astype(BF16)
            dz_ref[rows, 0:512] = (dy_a * o_a * (sg_a * (1.0 + g_a * (1.0 - sg_a)))).astype(BF16)
            dz_ref[rows, 1536:2048] = (dy_b * um * (sg_b * (1.0 + g_b * (1.0 - sg_b)))).astype(BF16)
            dys = dy_b * silu_b
            dz_ref[rows, 512:1024] = (dys * mixed * dgelu_u).astype(BF16)
            dmixed = dys * ug
            dmb = dmixed.astype(BF16)
            dvn_rows = []
            for sub in subs:
                dvn_parts = []
                for g in range(N_GROUPS):
                    cols = slice(g * 128, (g + 1) * 128)
                    dws_ref[g] += _dot(dmb[sub, cols], vn[sub, cols], NT)
                    dbs_ref[g] += jnp.sum(dmixed[sub, cols], axis=-1, keepdims=True)
                    dvn_parts.append(_dot(wmix[g], dmb[sub, cols], TN))
                dvn_rows.append(jnp.concatenate(dvn_parts, axis=1))
            dvn = jnp.concatenate(dvn_rows, axis=0)
            dlng_ref[...] += jnp.sum(dvn * vhat, axis=0, keepdims=True)
            dlnb_ref[...] += jnp.sum(dvn, axis=0, keepdims=True)
            dvh = dvn * lng
            dvg = rstd * (dvh - jnp.mean(dvh, axis=-1, keepdims=True)
                          - vhat * jnp.mean(dvh * vhat, axis=-1, keepdims=True))
            dz_ref[rows, 1024:1536] = (dvg * dgelu_v).astype(BF16)

        tile_fwd_bwd(slice(0, tm))

        @pl.when(i == nt - 1)
        def _():
            t_idx = lax.broadcasted_iota(jnp.int32, (SGU_CHUNK, SGU_CHUNK), 0)
            s_idx = lax.broadcasted_iota(jnp.int32, (SGU_CHUNK, SGU_CHUNK), 1)
            for g in range(N_GROUPS):
                dws_ref[g] = jnp.where(s_idx <= t_idx, dws_ref[g], 0.0)
            stores = [pltpu.make_async_copy(src, dst, sem.at[n])
                      for n, (src, dst) in enumerate(((acc_out, dwout_hbm), (acc_pa, dwpa_hbm), (acc_pb, dwpb_hbm)))]
            for cp in stores:
                cp.start()
            for cp in stores:
                cp.wait()

    tile = lambda w: pl.BlockSpec((tm, w), lambda i: (i, 0))
    whole = lambda shape: pl.BlockSpec(shape, lambda i: (0,) * len(shape))
    hbm = pl.BlockSpec(memory_space=pl.ANY)
    return pl.pallas_call(
        body, name="mid_fwd_bwd",
        grid=(nt,),
        in_specs=[tile(D_MODEL), tile(D_MODEL), tile(D_A)]
        + [pl.BlockSpec((tm, COL_BLOCK), functools.partial(lambda c, i: (i + Z_PAD // tm, c), c))
           for c in range(3, N_COL_BLOCKS)]
        + [hbm, hbm, hbm,
                  whole((1, 2 * D_MODEL)), whole((1, D_B)), whole((1, D_B)),
                  whole((N_GROUPS, SGU_CHUNK, SGU_CHUNK)), whole((N_GROUPS, SGU_CHUNK, 1)), whole((1, D_MODEL))],
        out_specs=(tile(D_MODEL), tile(D_A), tile(REST), hbm, hbm, hbm,
                   whole((1, 2 * D_MODEL)), whole((1, D_MODEL)), whole((1, D_B)), whole((1, D_B)),
                   whole((N_GROUPS, SGU_CHUNK, SGU_CHUNK)), whole((N_GROUPS, SGU_CHUNK, 1)), whole((1, 1))),
        out_shape=(jax.ShapeDtypeStruct((s, D_MODEL), F32), jax.ShapeDtypeStruct((s, D_A), BF16),
                   jax.ShapeDtypeStruct((s, REST), BF16),
                   jax.ShapeDtypeStruct((D_MODEL, D_MODEL), F32), jax.ShapeDtypeStruct((D_A, D_MODEL), F32),
                   jax.ShapeDtypeStruct((D_B, D_MODEL), F32),
                   jax.ShapeDtypeStruct((1, 2 * D_MODEL), F32), jax.ShapeDtypeStruct((1, D_MODEL), F32),
                   jax.ShapeDtypeStruct((1, D_B), F32), jax.ShapeDtypeStruct((1, D_B), F32),
                   jax.ShapeDtypeStruct((N_GROUPS, SGU_CHUNK, SGU_CHUNK), F32),
                   jax.ShapeDtypeStruct((N_GROUPS, SGU_CHUNK, 1), F32), jax.ShapeDtypeStruct((1, 1), F32)),
        scratch_shapes=[pltpu.VMEM((D_A, D_MODEL), BF16), pltpu.VMEM((D_B, D_MODEL), BF16),
                        pltpu.VMEM((D_MODEL, D_MODEL), BF16), pltpu.VMEM((N_GROUPS, SGU_CHUNK, SGU_CHUNK), BF16),
                        pltpu.VMEM((D_MODEL, D_MODEL), F32), pltpu.VMEM((D_A, D_MODEL), F32),
                        pltpu.VMEM((D_B, D_MODEL), F32),
                        pltpu.SemaphoreType.DMA((3,))],
        compiler_params=_params(56),
    )(x, target, attn_out, *([z] * (N_COL_BLOCKS - 3)), w_pa, w_pb, w_out, b_gate, ln_g, ln_b, w_s, b_s, final_g)


def _proj_bwd_x(dqkv, drest, x, dx2, norm_g, w_in_t, to_chip, small):
    s = x.shape[0]
    tm = 512 if s % 512 == 0 else TOKEN_TILE
    nt = s // tm
    rows = to_chip.shape[1]
    half = D_MODEL // 2
    left, right = slice(0, half), slice(half, D_MODEL)

    def body(dqkv_ref, dr_ref, x_ref, dx2_ref, g_ref, w_hbm, tc_hbm, ng_ref, rel_ref,
             dx_ref, fc_ref, slab_land,
             w, tc_ref, slab_stage, via_x, via_y, mine, out_x, out_y, sem, send_sems, recv_sems,
             gather_send, gather_recv, keep_sems):
        i = pl.program_id(0)
        x_, y_, c_ = _my_pos()
        me = _flat_id((x_, y_, c_))
        xn, yn = (1 - x_, y_, c_), (x_, 1 - y_, c_)
        gather = _SlotGather([slab_land], gather_send, gather_recv, own=[slab_stage])
        keep = [pltpu.make_async_copy(slab_stage, slab_land.at[me], keep_sems.at[0])]

        def copy(k, src, dst, to):
            return pltpu.make_async_remote_copy(src_ref=src, dst_ref=dst, send_sem=send_sems.at[k],
                                                recv_sem=recv_sems.at[k], device_id=to, device_id_type=MESH)

        first = [copy(0, tc_ref.at[0, :, left], fc_ref.at[0, :, left], xn), copy(1, tc_ref.at[2, :, left], via_x, xn),
                 copy(2, tc_ref.at[1, :, right], fc_ref.at[1, :, right], yn), copy(3, tc_ref.at[2, :, right], via_y, yn)]
        second = [copy(4, out_y, fc_ref.at[1, :, left], yn), copy(5, out_x, fc_ref.at[0, :, right], xn)]

        def add_and_send(arrival, landed, own_half, stage, onward):
            load = pltpu.make_async_copy(own_half, mine, sem)
            load.start()
            arrival.wait_recv()
            load.wait()
            stage[...] = (mine[...].astype(F32) + landed[...].astype(F32)).astype(BF16)
            onward.start()

        @pl.when(i == 0)
        def _():
            cp = pltpu.make_async_copy(w_hbm, w, sem)
            cp.start()
            _fill_slab(slab_stage, ((ROW_NORM_G, ng_ref), (ROW_REL, rel_ref)))
            for cp_keep in keep:
                cp_keep.start()
            gather.start()
            stage_in = pltpu.make_async_copy(tc_hbm, tc_ref, keep_sems.at[2])
            stage_in.start()
            stage_in.wait()
            for rc in first:
                rc.start()
            cp.wait()

        @pl.when(i == (5 * nt) // 8)
        def _():
            gather.pass_on()
            add_and_send(first[1], via_x, tc_ref.at[1, :, left], out_y, second[0])
            add_and_send(first[3], via_y, tc_ref.at[0, :, right], out_x, second[1])

        dh = None
        for c in range(N_COL_BLOCKS):
            dz = dqkv_ref[c] if c < 3 else dr_ref[:, (c - 3) * COL_BLOCK:(c - 2) * COL_BLOCK]
            part = _dot(dz, w[c * COL_BLOCK:(c + 1) * COL_BLOCK, :])
            dh = part if dh is None else dh + part
        xf = x_ref[...]
        r = lax.rsqrt(jnp.mean(xf * xf, axis=-1, keepdims=True) + EPS)
        xn = xf * r
        gh = dh * g_ref[...]
        dx_ref[...] = r * (gh - xn * jnp.mean(gh * xn, axis=-1, keepdims=True)) + dx2_ref[...]

        @pl.when(i == nt - 1)
        def _():
            gather.finish()
            for cp_keep in keep:
                cp_keep.wait()
            for k in (0, 2, 4, 5):
                (first + second)[k].wait_recv()
            for rc in first + second:
                rc.wait_send()

    hbm = pl.BlockSpec(memory_space=pl.ANY)
    whole = lambda a: pl.BlockSpec(a.shape, lambda i: (0,) * a.ndim)
    return pl.pallas_call(
        body, name="proj_bwd_x",
        grid=(nt,),
        in_specs=[pl.BlockSpec((3, tm, D_A), lambda i: (0, i, 0)),
                  pl.BlockSpec((tm, REST), lambda i: (i, 0)),
                  pl.BlockSpec((tm, D_MODEL), lambda i: (i, 0)),
                  pl.BlockSpec((tm, D_MODEL), lambda i: (i, 0)),
                  pl.BlockSpec((1, D_MODEL), lambda i: (0, 0)),
                  hbm, hbm] + [whole(a) for a in small],
        out_specs=(pl.BlockSpec((tm, D_MODEL), lambda i: (i, 0)), hbm, hbm),
        out_shape=(jax.ShapeDtypeStruct((s, D_MODEL), F32), jax.ShapeDtypeStruct((2, rows, D_MODEL), BF16),
                   jax.ShapeDtypeStruct((N_DEV, LATE_SLAB_ROWS, D_MODEL), F32)),
        scratch_shapes=[pltpu.VMEM((D_IN, D_MODEL), BF16), pltpu.VMEM(to_chip.shape, BF16),
                        pltpu.VMEM((LATE_SLAB_ROWS, D_MODEL), F32)]
        + [pltpu.VMEM((rows, half), BF16)] * 5
        + [pltpu.SemaphoreType.DMA, pltpu.SemaphoreType.DMA((6,)), pltpu.SemaphoreType.DMA((6,)),
           pltpu.SemaphoreType.DMA((1, N_DEV - 1)), pltpu.SemaphoreType.DMA((1, N_DEV - 1)),
           pltpu.SemaphoreType.DMA((3,))],
        compiler_params=_params(56),
    )(dqkv, drest, x, dx2, norm_g, w_in_t, to_chip, *small)


def _proj_bwd_w(xn, dqkv, drest, norm_g, w_in_t):
    s = xn.shape[0]
    tk = min(s, 1024)
    nk = s // tk

    def body(xn_ref, dqkv_ref, dr_ref, g_ref, w_ref, o_ref, dg_ref, acc):
        j = pl.program_id(0)
        i = pl.program_id(1)

        @pl.when((j == 0) & (i == 0))
        def _():
            dg_ref[...] = jnp.zeros(dg_ref.shape, F32)

        @pl.when(i == 0)
        def _():
            acc[...] = jnp.zeros(acc.shape, F32)

        @pl.when(j < 3)
        def _():
            acc[...] += _dot(dqkv_ref[...], xn_ref[...], TN)

        @pl.when(j >= 3)
        def _():
            acc[...] += _dot(dr_ref[...], xn_ref[...], TN)

        @pl.when(i == nk - 1)
        def _():
            m = acc[...]
            o_ref[...] = (m * g_ref[...]).astype(BF16)
            dg_ref[...] += jnp.sum(m * w_ref[...].astype(F32), axis=0, keepdims=True)

    return pl.pallas_call(
        body, name="proj_bwd_w",
        grid=(N_COL_BLOCKS, nk),
        in_specs=[pl.BlockSpec((tk, D_MODEL), lambda j, i: (i, 0)),
                  pl.BlockSpec((None, tk, COL_BLOCK),
                               lambda j, i: (jnp.minimum(j, 2), jnp.where(j < 3, i, nk - 1), 0)),
                  pl.BlockSpec((tk, COL_BLOCK),
                               lambda j, i: (jnp.where(j >= 3, i, 0), jnp.maximum(j - 3, 0))),
                  pl.BlockSpec((1, D_MODEL), lambda j, i: (0, 0)),
                  pl.BlockSpec((COL_BLOCK, D_MODEL), lambda j, i: (j, 0))],
        out_specs=(pl.BlockSpec((COL_BLOCK, D_MODEL), lambda j, i: (j, 0)),
                   pl.BlockSpec((1, D_MODEL), lambda j, i: (0, 0))),
        out_shape=(jax.ShapeDtypeStruct((D_IN, D_MODEL), BF16), jax.ShapeDtypeStruct((1, D_MODEL), F32)),
        scratch_shapes=[pltpu.VMEM((COL_BLOCK, D_MODEL), F32)],
        compiler_params=_params(40),
    )(xn, dqkv, drest, norm_g, w_in_t)


def _adamw_math(w, g, m, v):
    c1 = 1.0 - ADAM_B1 ** ADAM_STEP
    c2 = 1.0 - ADAM_B2 ** ADAM_STEP
    nm = ADAM_B1 * m + (1.0 - ADAM_B1) * g
    nv = ADAM_B2 * v + (1.0 - ADAM_B2) * (g * g)
    return -ADAM_LR * ((nm / c1) / (jnp.sqrt(nv / c2) + ADAM_EPS) + ADAM_WD * w), nm, nv


def _adamw(name, w, g, m, v, from_chip):
    rows, cols = w.shape
    tr = rows if rows * cols <= 512 * 1024 else next(t for t in range(256, 7, -8) if rows % t == 0)

    def body(w_ref, g_ref, m_ref, v_ref, t_ref, g_out, d_ref, nm_ref, nv_ref):
        gg = g_ref[...]
        for j in range(from_chip.shape[0]):
            gg = gg + t_ref[j].astype(F32)
        g_out[...] = gg
        d_ref[...], nm_ref[...], nv_ref[...] = _adamw_math(w_ref[...], gg, m_ref[...], v_ref[...])

    spec = pl.BlockSpec((tr, cols), lambda i: (i, 0))
    shape = jax.ShapeDtypeStruct((rows, cols), F32)
    return pl.pallas_call(
        body, name=name,
        grid=(rows // tr,),
        in_specs=[spec] * 4 + [pl.BlockSpec((from_chip.shape[0], tr, cols), lambda i: (0, i, 0))],
        out_specs=(spec,) * 4, out_shape=(shape,) * 4,
        compiler_params=_params(32),
    )(w, g, m, v, from_chip)


_SMALL = (("norm_g", (1, D_MODEL)), ("b_gate", (1, 2 * D_MODEL)), ("rel_bias", (N_HEADS, N_REL)),
          ("sgu_ln_g", (1, D_B)), ("sgu_ln_b", (1, D_B)), ("w_s", (N_GROUPS * SGU_CHUNK, SGU_CHUNK)),
          ("b_s", (N_GROUPS, SGU_CHUNK)), ("final_g", (1, D_MODEL)))


def _adamw_small(slabs, ws_all, late_slabs, weights, moments_m, moments_v):
    k = len(_SMALL)

    def total(ref):
        acc = ref[0]
        for d in range(1, N_DEV):
            acc = acc + ref[d]
        return acc

    def body(*refs):
        slab_ref, ws_ref, late_ref = refs[:3]
        w_refs, m_refs, v_refs = refs[3:3 + k], refs[3 + k:3 + 2 * k], refs[3 + 2 * k:3 + 3 * k]
        outs = refs[3 + 3 * k:]
        slab, late = total(slab_ref), total(late_ref)
        grads = {
            "norm_g": late[ROW_NORM_G:ROW_NORM_G + 1, :],
            "b_gate": jnp.concatenate([slab[ROW_B_GATE:ROW_B_GATE + 1, :], slab[ROW_B_GATE + 1:ROW_B_GATE + 2, :]], axis=1),
            "rel_bias": late[ROW_REL:ROW_REL + N_HEADS, :N_REL],
            "sgu_ln_g": slab[ROW_LN_G:ROW_LN_G + 1, :D_B],
            "sgu_ln_b": slab[ROW_LN_B:ROW_LN_B + 1, :D_B],
            "w_s": total(ws_ref),
            "b_s": slab[ROW_B_S:ROW_B_S + N_GROUPS, :SGU_CHUNK],
            "final_g": slab[ROW_FINAL_G:ROW_FINAL_G + 1, :],
        }
        for n, (name, _) in enumerate(_SMALL):
            g = grads[name]
            outs[n][...] = g
            outs[k + n][...], outs[2 * k + n][...], outs[3 * k + n][...] = _adamw_math(
                w_refs[n][...], g, m_refs[n][...], v_refs[n][...])
        outs[4 * k][...] = slab[ROW_LOSS:ROW_LOSS + 1, :1]

    vmem = pl.BlockSpec(memory_space=pltpu.VMEM)
    shapes = tuple(jax.ShapeDtypeStruct(shape, F32) for _, shape in _SMALL)
    return pl.pallas_call(
        body, name="adamw_small",
        out_shape=shapes * 4 + (jax.ShapeDtypeStruct((1, 1), F32),),
        in_specs=[vmem] * (3 + 3 * k), out_specs=tuple([vmem] * (4 * k + 1)),
        compiler_params=_params(16),
    )(slabs, ws_all, late_slabs, *weights, *moments_m, *moments_v)


def _pad_rel(a):
    return jnp.pad(a.reshape(N_HEADS, N_REL), ((0, 0), (0, N_REL_PAD - N_REL)))


def kernel(x, norm_g, w_in, b_gate, rel_bias, sgu_ln_g, sgu_ln_b, w_s, b_s, w_pa, w_pb, w_out, final_g, loss_target, m_norm_g, m_w_in, m_b_gate, m_rel_bias, m_sgu_ln_g, m_sgu_ln_b, m_w_s, m_b_s, m_w_pa, m_w_pb, m_w_out, m_final_g, v_norm_g, v_w_in, v_b_gate, v_rel_bias, v_sgu_ln_g, v_sgu_ln_b, v_w_s, v_b_s, v_w_pa, v_w_pb, v_w_out, v_final_g):
    s = x.shape[1]
    xs = x.reshape(s, D_MODEL)
    tgt = loss_target.reshape(s, D_MODEL)

    bias_table = _bias_table(_pad_rel(rel_bias))
    w_in_t = jnp.swapaxes(w_in[0], 0, 1)
    qkv, x_norm, w_in_t_full = _gather_proj_fwd(xs, norm_g, w_in_t)
    attn_out, g_pa, g_pb, g_out = _attn_fwd(qkv, bias_table, (w_pa[0], w_pb[0], w_out[0]))
    w_pa_full = jnp.transpose(g_pa, (1, 0, 2)).reshape(D_A, D_MODEL)
    w_pb_full = jnp.transpose(g_pb, (1, 0, 2)).reshape(D_B, D_MODEL)
    w_out_full = g_out.reshape(D_MODEL, D_MODEL)

    (dx2, d_attn, drest, dw_out, dw_pa, dw_pb, d_bgate, d_fg, d_lng, d_lnb, d_ws, d_bs, loss_part) = _mid_fwd_bwd(
        xs, tgt, attn_out, qkv, w_pa_full, w_pb_full, w_out_full, b_gate, sgu_ln_g, sgu_ln_b, w_s[0],
        b_s.reshape(N_GROUPS, SGU_CHUNK, 1), final_g.reshape(1, D_MODEL))

    own_pa, own_pb, own_out, tc_pa, tc_pb, tc_out = _reduce_chip(
        "reduce_chip_proj", (dw_pa, dw_pb, dw_out), (1, 1, 0))
    dqkv, dbias, fc_pa, fc_pb, fc_out, slabs, ws_all = _attn_bwd(
        qkv, bias_table, d_attn, (tc_pa, tc_pb, tc_out),
        (d_bgate, d_lng, d_lnb, d_fg, loss_part, d_bs, d_ws.reshape(N_GROUPS * SGU_CHUNK, SGU_CHUNK)))
    d_rel = _bias_grad(dbias)
    dw_in_t, d_ng = _proj_bwd_w(x_norm, dqkv, drest, norm_g, w_in_t_full)
    own_in, tc_in = _reduce_chip("reduce_chip_in", (dw_in_t,), (0,))
    grad_x, fc_in, late_slabs = _proj_bwd_x(dqkv, drest, xs, dx2, norm_g, w_in_t_full, tc_in, (d_ng, d_rel))
    big = {"w_in": tuple(jnp.swapaxes(t, 0, 1)[None] for t in _adamw(
        "adamw_w_in", w_in_t, own_in, jnp.swapaxes(m_w_in[0], 0, 1), jnp.swapaxes(v_w_in[0], 0, 1), fc_in))}
    for name, w, g, fc, m, v in (("w_pa", w_pa, own_pa, fc_pa, m_w_pa, v_w_pa),
                                 ("w_pb", w_pb, own_pb, fc_pb, m_w_pb, v_w_pb),
                                 ("w_out", w_out, own_out, fc_out, m_w_out, v_w_out)):
        big[name] = tuple(t[None] for t in _adamw("adamw_" + name, w[0], g, m[0], v[0], fc))

    as_2d = lambda leaves: [a.reshape(shape) for a, (_, shape) in zip(leaves, _SMALL)]
    small_out = _adamw_small(
        slabs, ws_all, late_slabs, as_2d((norm_g, b_gate, rel_bias, sgu_ln_g, sgu_ln_b, w_s, b_s, final_g)),
        as_2d((m_norm_g, m_b_gate, m_rel_bias, m_sgu_ln_g, m_sgu_ln_b, m_w_s, m_b_s, m_final_g)),
        as_2d((v_norm_g, v_b_gate, v_rel_bias, v_sgu_ln_g, v_sgu_ln_b, v_w_s, v_b_s, v_final_g)))
    small_index = {name: n for n, (name, _) in enumerate(_SMALL)}

    def leaf(kind, name, like):
        if name in big:
            return big[name][kind]
        return small_out[kind * len(_SMALL) + small_index[name]].reshape(like.shape)

    weights = (("norm_g", norm_g), ("w_in", w_in), ("b_gate", b_gate), ("rel_bias", rel_bias), ("sgu_ln_g", sgu_ln_g),
               ("sgu_ln_b", sgu_ln_b), ("w_s", w_s), ("b_s", b_s), ("w_pa", w_pa), ("w_pb", w_pb), ("w_out", w_out),
               ("final_g", final_g))
    outs = [small_out[-1].reshape(()), grad_x.reshape(x.shape)]
    for kind in range(4):
        outs.extend(leaf(kind, name, like) for name, like in weights)
    return tuple(outs)
```

```python
import functools
import math

import jax
import jax.numpy as jnp
from jax import lax
from jax.experimental import pallas as pl
from jax.experimental.pallas import tpu as pltpu

F32 = jnp.float32
BF16 = jnp.bfloat16
MESH = pl.DeviceIdType.MESH
N_DEV = 8

D_MODEL = 1024
D_A = 512
D_B = 512
D_IN = 5632
N_HEADS = 8
HEAD_DIM = 64
N_PREV = 8
REL_CLIP = 128
N_REL = 2 * REL_CLIP + 1
N_REL_PAD = 384
SGU_CHUNK = 128
N_GROUPS = 4
EPS = 1e-6
NEG_INF = -1e30
Q_SCALE = HEAD_DIM ** -0.5

Q_BLOCK = 256
K_SPAN = 768
Z_PAD = K_SPAN - Q_BLOCK
ROLL_W = 1024
COL_BLOCK = 512
N_COL_BLOCKS = D_IN // COL_BLOCK
REST = D_IN - 3 * D_A
TOKEN_TILE = 256

ADAM_LR = 0.001
ADAM_B1 = 0.9
ADAM_B2 = 0.999
ADAM_EPS = 1e-08
ADAM_WD = 0.01
ADAM_STEP = 10

GELU_C = math.sqrt(2.0 / math.pi)
GELU_A = 0.044715

NT = (((1,), (1,)), ((), ()))
TN = (((0,), (0,)), ((), ()))
HIGHEST = lax.Precision.HIGHEST


def _params(vmem_mb, **kw):
    return pltpu.CompilerParams(vmem_limit_bytes=vmem_mb * 1024 * 1024, **kw)


def _dot(a, b, dims=None):
    if dims is None:
        return jnp.dot(a, b, preferred_element_type=F32)
    return lax.dot_general(a, b, dims, preferred_element_type=F32)


def _sigmoid(x):
    return 0.5 * jnp.tanh(0.5 * x) + 0.5


def _gelu_and_grad(u):
    u2 = u * u
    t = jnp.tanh(GELU_C * (u + GELU_A * u * u2))
    half = 0.5 * (1.0 + t)
    g = u * half
    dg = half + 0.5 * u * (1.0 - t * t) * (GELU_C * (1.0 + 3.0 * GELU_A * u2))
    return g, dg


def _my_pos():
    return lax.axis_index("x"), lax.axis_index("y"), lax.axis_index("c")


def _flat_id(pos):
    return 4 * pos[0] + 2 * pos[1] + pos[2]


def _other_chips(pos):
    x, y, _ = pos
    return ((1 - x, y), (x, 1 - y), (1 - x, 1 - y))


class _SlotGather:
    def __init__(self, bufs, send_sems, recv_sems, own=None):
        self.bufs, self.send_sems, self.recv_sems = bufs, send_sems, recv_sems
        self.own = own if own is not None else [None] * len(bufs)
        x, y, c = _my_pos()
        self.c, self.me, self.sib = c, (x, y, c), (x, y, 1 - c)
        self.chips = _other_chips(self.me)

    def _copy(self, a, k, block, to):
        slot = _flat_id(block)
        src = self.own[a] if (k < 4 and self.own[a] is not None) else self.bufs[a].at[slot]
        return pltpu.make_async_remote_copy(
            src_ref=src, dst_ref=self.bufs[a].at[slot],
            send_sem=self.send_sems.at[a, k], recv_sem=self.recv_sems.at[a, k], device_id=to, device_id_type=MESH)

    def _own_sends(self):
        n = len(self.bufs)
        return ([self._copy(a, 1 + j, self.me, (*chip, self.c)) for j, chip in enumerate(self.chips) for a in range(n)]
                + [self._copy(a, 0, self.me, self.sib) for a in range(n)])

    def _passes(self):
        return [self._copy(a, 4 + j, (*chip, self.c), self.sib)
                for j, chip in enumerate(self.chips) for a in range(len(self.bufs))]

    def start(self):
        for cp in self._own_sends():
            cp.start()

    def pass_on(self):
        for j, chip in enumerate(self.chips):
            for a in range(len(self.bufs)):
                self._copy(a, 1 + j, (*chip, self.c), self.me).wait_recv()
                self._copy(a, 4 + j, (*chip, self.c), self.sib).start()

    def finish(self):
        for a in range(len(self.bufs)):
            self._copy(a, 0, self.sib, self.me).wait_recv()
            for j, chip in enumerate(self.chips):
                self._copy(a, 4 + j, (*chip, 1 - self.c), self.me).wait_recv()
        for cp in self._own_sends() + self._passes():
            cp.wait_send()


def _reduce_chip(name, parts, sharded_dim):
    n = len(parts)
    shapes = []
    for p, dim in zip(parts, sharded_dim):
        shape = list(p.shape)
        shape[dim] //= N_DEV
        shapes.append(tuple(shape))
    staged = [not (dim == 0 and p.dtype == BF16) for p, dim in zip(parts, sharded_dim)]

    def body(*refs):
        full, own, to_chip = refs[:n], refs[n:2 * n], refs[2 * n:3 * n]
        ins, from_sib = refs[3 * n:4 * n], refs[4 * n:5 * n]
        send_sems, recv_sems = refs[5 * n], refs[5 * n + 1]
        x, y, c = _my_pos()
        sib = (x, y, 1 - c)
        chips = ((x, y),) + _other_chips((x, y, c))
        for a in range(n):
            rows, cols = shapes[a]
            for d in range(N_DEV if staged[a] else 0):
                if sharded_dim[a] == 0:
                    ins[a][d] = full[a][d * rows:(d + 1) * rows, :].astype(BF16)
                else:
                    ins[a][d] = full[a][:, d * cols:(d + 1) * cols].astype(BF16)

        def block(a, d):
            if staged[a]:
                return ins[a].at[d]
            rows = shapes[a][0]
            return full[a].at[pl.ds(pl.multiple_of(d * rows, 16), rows), :]

        def to_sibling(a, r):
            return pltpu.make_async_remote_copy(
                src_ref=block(a, _flat_id((*chips[r], 1 - c))), dst_ref=from_sib[a].at[r],
                send_sem=send_sems.at[a, r], recv_sem=recv_sems.at[a, r], device_id=sib, device_id_type=MESH)

        sends = [to_sibling(a, r) for r in (1, 2, 3, 0) for a in range(n)]
        for cp in sends:
            cp.start()
        for r in (1, 2, 3, 0):
            for a in range(n):
                to_sibling(a, r).wait_recv()
                both = block(a, _flat_id((*chips[r], c)))[...].astype(F32) + from_sib[a][r].astype(F32)
                if r == 0:
                    own[a][...] = both
                else:
                    to_chip[a][r - 1] = both.astype(BF16)
        for cp in sends:
            cp.wait_send()

    vmem = pl.BlockSpec(memory_space=pltpu.VMEM)
    return pl.pallas_call(
        body, name=name,
        out_shape=tuple(jax.ShapeDtypeStruct(sh, F32) for sh in shapes)
        + tuple(jax.ShapeDtypeStruct((3,) + sh, BF16) for sh in shapes),
        in_specs=[vmem] * n, out_specs=tuple([vmem] * (2 * n)),
        scratch_shapes=[pltpu.VMEM((N_DEV if st else 1,) + sh, BF16) for sh, st in zip(shapes, staged)]
        + [pltpu.VMEM((4,) + sh, BF16) for sh in shapes]
        + [pltpu.SemaphoreType.DMA((n, 4)), pltpu.SemaphoreType.DMA((n, 4))],
        compiler_params=_params(56),
    )(*parts)


def _owner_copies(to_chip, from_chip, send_sems, recv_sems):
    x, y, c = _my_pos()
    return [pltpu.make_async_remote_copy(
        src_ref=to_chip[a].at[j], dst_ref=from_chip[a].at[j],
        send_sem=send_sems.at[a, j], recv_sem=recv_sems.at[a, j], device_id=(*chip, c), device_id_type=MESH)
        for a in range(len(to_chip)) for j, chip in enumerate(_other_chips((x, y, c)))]


ROW_B_GATE, ROW_LN_G, ROW_LN_B, ROW_FINAL_G, ROW_LOSS, ROW_B_S, SLAB_ROWS = 1, 3, 4, 5, 6, 16, 24
ROW_NORM_G, ROW_REL, LATE_SLAB_ROWS = 0, 8, 16


def _rel_index(e):
    lo, hi = Z_PAD - REL_CLIP, Z_PAD + REL_CLIP
    return jnp.where(e <= lo, 2 * REL_CLIP, jnp.where(e < hi, hi - e, jnp.where(e <= K_SPAN, 0, 2 * REL_CLIP)))


def _bias_table(rel_bias_pad):
    def body(rb_ref, bt_ref):
        c = lax.broadcasted_iota(jnp.int32, (N_REL_PAD, ROLL_W), 1)
        r = lax.broadcasted_iota(jnp.int32, (N_REL_PAD, ROLL_W), 0)
        pick = (r == _rel_index(c)).astype(F32)
        rows = jnp.dot(rb_ref[...], pick, precision=HIGHEST, preferred_element_type=F32)
        qc = lax.broadcasted_iota(jnp.int32, (Q_BLOCK, K_SPAN), 0) >> 6
        kc = lax.broadcasted_iota(jnp.int32, (Q_BLOCK, K_SPAN), 1) >> 6
        band = (kc >= qc) & (kc <= qc + N_PREV)
        for h in range(N_HEADS):
            t = jnp.broadcast_to(rows[h:h + 1, :], (Q_BLOCK, ROLL_W))
            t = pltpu.roll(t, 0, 1, stride=1, stride_axis=0)
            bt_ref[h] = jnp.where(band, t[:, :K_SPAN], NEG_INF)

    return pl.pallas_call(
        body, name="bias_table",
        out_shape=jax.ShapeDtypeStruct((N_HEADS, Q_BLOCK, K_SPAN), F32),
        compiler_params=_params(32),
    )(rel_bias_pad)


def _bias_grad(dbias):
    def body(a_ref, o_ref):
        rr = lax.broadcasted_iota(jnp.int32, (Q_BLOCK, Q_BLOCK), 0)
        cc = lax.broadcasted_iota(jnp.int32, (Q_BLOCK, Q_BLOCK), 1)
        flip = (rr + cc == Q_BLOCK - 1).astype(F32)
        c = lax.broadcasted_iota(jnp.int32, (ROLL_W, N_REL_PAD), 0)
        r = lax.broadcasted_iota(jnp.int32, (ROLL_W, N_REL_PAD), 1)
        e = jnp.where(c >= Q_BLOCK - 1, c - (Q_BLOCK - 1), c + (ROLL_W - Q_BLOCK + 1))
        pick = (r == _rel_index(e)).astype(F32)
        sums = []
        for h in range(N_HEADS):
            a = jnp.dot(flip, a_ref[h], precision=HIGHEST, preferred_element_type=F32)
            a = jnp.concatenate([a, jnp.zeros((Q_BLOCK, ROLL_W - K_SPAN), F32)], axis=1)
            a = pltpu.roll(a, 0, 1, stride=1, stride_axis=0)
            sums.append(jnp.sum(a, axis=0, keepdims=True))
        diag = jnp.concatenate(sums, axis=0)
        o_ref[...] = jnp.dot(diag, pick, precision=HIGHEST, preferred_element_type=F32)

    return pl.pallas_call(
        body, name="bias_grad",
        out_shape=jax.ShapeDtypeStruct((N_HEADS, N_REL_PAD), F32),
        compiler_params=_params(32),
    )(dbias)


def _gather_proj_fwd(x, norm_g, w_in_t):
    s = x.shape[0]
    tm = 512 if s % 512 == 0 else TOKEN_TILE
    nt = s // tm
    n_pad = Z_PAD // tm
    shard_w = w_in_t.shape[0]
    chip_w = 2 * shard_w
    n_chips = N_DEV // 2

    def body(order_ref, x_ref, g_ref, win_hbm, z_ref, xn_ref, wt_hbm, wt, hb, win_f32, send_sems, recv_sems,
             local_sems):
        j = pl.program_id(0)
        i = pl.program_id(1)
        x_, y_, c_ = _my_pos()
        me, sib = (x_, y_, c_), (x_, y_, 1 - c_)
        near = _other_chips(me)
        pick = lambda a, b: tuple(jnp.where(c_ == 0, u, v) for u, v in zip(a, b))
        passed_from, passed_to = pick(near[0], near[1]), pick(near[1], near[0])

        def rows_of(block):
            return wt.at[pl.ds(pl.multiple_of(_flat_id(block) * shard_w, 16), shard_w), :]

        def copy(k, block, to):
            return pltpu.make_async_remote_copy(
                src_ref=rows_of(block), dst_ref=rows_of(block),
                send_sem=send_sems.at[k], recv_sem=recv_sems.at[k], device_id=to, device_id_type=MESH)

        def sends():
            return ([copy(0, me, sib), copy(1, me, (*near[0], c_)), copy(2, me, (*near[1], c_)),
                     copy(3, (*passed_from, c_), (*passed_to, c_))]
                    + [copy(4 + n, (*near[n], c_), sib) for n in range(3)])

        keep = pltpu.make_async_copy(wt, wt_hbm, local_sems.at[0])

        @pl.when((j == 0) & (i == 0))
        def _():
            load = pltpu.make_async_copy(win_hbm, win_f32, local_sems.at[1])
            load.start()
            load.wait()
            rows_of(me)[...] = win_f32[...].astype(BF16)
            for cp in sends()[:3]:
                cp.start()
            copy(0, sib, me).wait_recv()

        @pl.when((j == 1) & (i == 0))
        def _():
            copy(1, (*near[0], c_), me).wait_recv()
            copy(2, (*near[1], c_), me).wait_recv()
            for cp in sends()[3:6]:
                cp.start()
            copy(4, (*near[0], 1 - c_), me).wait_recv()

        @pl.when((j == 2) & (i == 0))
        def _():
            copy(5, (*near[1], 1 - c_), me).wait_recv()

        @pl.when((j == 3) & (i == 0))
        def _():
            copy(3, (*near[2], c_), me).wait_recv()
            copy(6, (*near[2], c_), sib).start()
            copy(6, (*near[2], 1 - c_), me).wait_recv()
            keep.start()

        @pl.when(i < n_pad)
        def _():
            z_ref[...] = jnp.zeros(z_ref.shape, BF16)

        @pl.when(i >= n_pad)
        def _():
            rows = pl.ds(pl.multiple_of((i - n_pad) * tm, tm), tm)

            @pl.when(j == 0)
            def _():
                xf = x_ref[...]
                xn = xf * lax.rsqrt(jnp.mean(xf * xf, axis=-1, keepdims=True) + EPS)
                hb[rows, :] = (xn * g_ref[...]).astype(BF16)
                xn_ref[...] = xn.astype(BF16)

            chip_rows = pl.ds(pl.multiple_of(order_ref[j] * chip_w, 16), chip_w)
            blk = _dot(hb[rows, :], wt[chip_rows, :], NT)
            q_scale = jnp.where(order_ref[j] == 0, Q_SCALE, 1.0).astype(F32)
            z_ref[:, :D_A] = (blk[:, :D_A] * q_scale).astype(BF16)
            z_ref[:, D_A:] = blk[:, D_A:].astype(BF16)

        @pl.when((j == n_chips - 1) & (i == n_pad + nt - 1))
        def _():
            keep.wait()
            for cp in sends():
                cp.wait_send()

    pos = _my_pos()
    order = jnp.stack([2 * cx + cy for cx, cy in ((pos[0], pos[1]),) + _other_chips(pos)]).astype(jnp.int32)
    first_pass = lambda j, i: jnp.where(j == 0, jnp.maximum(i - n_pad, 0), nt - 1)
    grid_spec = pltpu.PrefetchScalarGridSpec(
        num_scalar_prefetch=1,
        grid=(n_chips, n_pad + nt),
        in_specs=[pl.BlockSpec((tm, D_MODEL), lambda j, i, o: (first_pass(j, i), 0)),
                  pl.BlockSpec((1, D_MODEL), lambda j, i, o: (0, 0)),
                  pl.BlockSpec(memory_space=pl.ANY)],
        out_specs=(pl.BlockSpec((tm, chip_w), lambda j, i, o: (i, o[j])),
                   pl.BlockSpec((tm, D_MODEL), lambda j, i, o: (first_pass(j, i), 0)),
                   pl.BlockSpec(memory_space=pl.ANY)),
        scratch_shapes=[pltpu.VMEM((D_IN, D_MODEL), BF16),
                        pltpu.VMEM((s, D_MODEL), BF16), pltpu.VMEM(w_in_t.shape, F32),
                        pltpu.SemaphoreType.DMA((N_DEV - 1,)), pltpu.SemaphoreType.DMA((N_DEV - 1,)),
                        pltpu.SemaphoreType.DMA((2,))])
    return pl.pallas_call(
        body, name="gather_proj_fwd",
        grid_spec=grid_spec,
        out_shape=(jax.ShapeDtypeStruct((Z_PAD + s, D_IN), BF16), jax.ShapeDtypeStruct((s, D_MODEL), BF16),
                   jax.ShapeDtypeStruct((D_IN, D_MODEL), BF16)),
        compiler_params=_params(60),
    )(order, x, norm_g, w_in_t)


def _attn_specs(rows):
    pairs = N_HEADS // 2
    return ([pl.BlockSpec((rows, 128), functools.partial(lambda which, p: (0, which * pairs + p), which))
             for which in range(3)]
            + [pl.BlockSpec((2, Q_BLOCK, K_SPAN), lambda p: (p, 0, 0))])


def _head_masks():
    lane = lax.broadcasted_iota(jnp.int32, (1, 128), 1)
    first = lane < HEAD_DIM
    return (first, jnp.logical_not(first))


def _stack_heads(x, masks):
    zero = jnp.zeros((), x.dtype)
    return jnp.concatenate([jnp.where(m, x, zero) for m in masks], axis=0)


STRIP = 16


def _softmax_strips(s_ref, bias_ref, b):
    valid = lax.broadcasted_iota(jnp.int32, (1, K_SPAN), 1) >= Z_PAD - b * Q_BLOCK
    for t in range(2 * Q_BLOCK // STRIP):
        hh, r = divmod(t * STRIP, Q_BLOCK)
        st = s_ref[t * STRIP:(t + 1) * STRIP, :] + bias_ref[hh, r:r + STRIP, :]
        st = jnp.where(valid, st, NEG_INF)
        e = jnp.exp(st - jnp.max(st, axis=-1, keepdims=True))
        yield e * (1.0 / jnp.sum(e, axis=-1, keepdims=True))


def _side_by_side_strips(strips):
    half = len(strips) // 2
    return jnp.concatenate([jnp.concatenate([a, c], axis=1) for a, c in zip(strips[:half], strips[half:])], axis=0)


def _attn_fwd(qkv, bias_table, shards):
    s = qkv.shape[0] - Z_PAD
    nb = s // Q_BLOCK
    n = len(shards)
    pairs = N_HEADS // 2

    def body(*refs):
        q_ref, k_ref, v_ref, bt_ref = refs[:4]
        shard_refs = refs[4:4 + n]
        o_ref = refs[4 + n]
        slot_refs = refs[5 + n:5 + 2 * n]
        stages = refs[5 + 2 * n:5 + 3 * n]
        s_scr, send_sems, recv_sems, local_sems = refs[5 + 3 * n:]
        p_id = pl.program_id(0)
        gather = _SlotGather(slot_refs, send_sems, recv_sems, own=stages)
        keep = [pltpu.make_async_copy(stages[a], slot_refs[a].at[_flat_id(_my_pos())], local_sems.at[a])
                for a in range(n)]

        @pl.when(p_id == 0)
        def _():
            for a in range(n):
                stages[a][...] = shard_refs[a][...].astype(BF16)
                keep[a].start()
            gather.start()

        @pl.when(p_id == 2)
        def _():
            gather.pass_on()

        masks = _head_masks()

        def scores(b, half):
            r0 = pl.multiple_of(b * Q_BLOCK, Q_BLOCK)
            q2 = _stack_heads(q_ref[pl.ds(r0 + Z_PAD, Q_BLOCK), :], masks)
            s_scr[half] = _dot(q2, k_ref[pl.ds(r0, K_SPAN), :], NT)

        def finish(b, half):
            r0 = pl.multiple_of(b * Q_BLOCK, Q_BLOCK)
            v2 = _stack_heads(v_ref[pl.ds(r0, K_SPAN), :], masks)
            p = [st.astype(BF16) for st in _softmax_strips(s_scr.at[half], bt_ref, b)]
            o_ref[pl.ds(r0, Q_BLOCK), :] = _dot(_side_by_side_strips(p), v2)

        def two_blocks(i, carry):
            b = 2 * i
            scores(b + 1, 1)
            finish(b, 0)
            scores(jnp.minimum(b + 2, nb - 1), 0)
            finish(b + 1, 1)
            return carry

        scores(0, 0)
        lax.fori_loop(0, nb // 2, two_blocks, 0)

        @pl.when(p_id == pairs - 1)
        def _():
            gather.finish()
            for cp in keep:
                cp.wait()

    hbm = pl.BlockSpec(memory_space=pl.ANY)
    return pl.pallas_call(
        body, name="attn_fwd",
        grid=(pairs,),
        in_specs=_attn_specs(s + Z_PAD) + [pl.BlockSpec(a.shape, lambda p: (0, 0)) for a in shards],
        out_specs=(pl.BlockSpec((s, 128), lambda p: (0, p)),) + (hbm,) * n,
        out_shape=(jax.ShapeDtypeStruct((s, D_A), F32),)
        + tuple(jax.ShapeDtypeStruct((N_DEV,) + a.shape, BF16) for a in shards),
        scratch_shapes=[pltpu.VMEM(a.shape, BF16) for a in shards]
        + [pltpu.VMEM((2, 2 * Q_BLOCK, K_SPAN), F32),
           pltpu.SemaphoreType.DMA((n, N_DEV - 1)), pltpu.SemaphoreType.DMA((n, N_DEV - 1)),
           pltpu.SemaphoreType.DMA((n,))],
        compiler_params=_params(48),
    )(qkv, qkv, qkv, bias_table, *shards)


def _fill_slab(stage, rows):
    stage[...] = jnp.zeros(stage.shape, F32)
    for row, ref in rows:
        r, c = ref.shape
        if c > D_MODEL:
            for part in range(c // D_MODEL):
                stage[row + part:row + part + 1, :] = ref[:, part * D_MODEL:(part + 1) * D_MODEL]
        else:
            stage[row:row + r, :c] = ref[...]


def _attn_bwd(qkv, bias_table, d_out, to_chip, small):
    s = qkv.shape[0] - Z_PAD
    nb = s // Q_BLOCK
    n = len(to_chip)
    pairs = N_HEADS // 2
    ws_shape = small[-1].shape

    def body(*refs):
        q_ref, k_ref, v_ref, bt_ref, do_ref = refs[:5]
        to_chip_refs = refs[5:5 + n]
        bg_ref, lng_ref, lnb_ref, fg_ref, loss_ref, bs_ref, ws_ref = refs[5 + n:12 + n]
        dqkv_ref, db_ref = refs[12 + n:14 + n]
        from_chip_refs = refs[14 + n:14 + 2 * n]
        slab_land, ws_land = refs[14 + 2 * n:16 + 2 * n]
        (dk_acc, dv_acc, s_scr, dp_scr, slab_stage, ws_stage, send_sems, recv_sems, gather_send, gather_recv,
         keep_sems) = refs[16 + 2 * n:]
        p_id = pl.program_id(0)
        me = _flat_id(_my_pos())
        gather = _SlotGather([slab_land, ws_land], gather_send, gather_recv, own=[slab_stage, ws_stage])
        keep = [pltpu.make_async_copy(stage, land.at[me], keep_sems.at[k]) for k, (stage, land) in enumerate(
            ((slab_stage, slab_land), (ws_stage, ws_land)))]

        @pl.when(p_id == 0)
        def _():
            _fill_slab(slab_stage, ((ROW_B_GATE, bg_ref), (ROW_LN_G, lng_ref), (ROW_LN_B, lnb_ref),
                                    (ROW_FINAL_G, fg_ref), (ROW_LOSS, loss_ref)))
            eye = (lax.broadcasted_iota(jnp.int32, (SGU_CHUNK, SGU_CHUNK), 0)
                   == lax.broadcasted_iota(jnp.int32, (SGU_CHUNK, SGU_CHUNK), 1))
            for g in range(N_GROUPS):
                row = jnp.sum(jnp.where(eye, bs_ref[g], 0.0), axis=0, keepdims=True)
                slab_stage[ROW_B_S + g:ROW_B_S + g + 1, :SGU_CHUNK] = row
            ws_stage[...] = ws_ref[...]
            for cp in keep:
                cp.start()
            gather.start()
            for cp in _owner_copies(to_chip_refs, from_chip_refs, send_sems, recv_sems):
                cp.start()

        @pl.when(p_id == 2)
        def _():
            gather.pass_on()

        dk_acc[...] = jnp.zeros(dk_acc.shape, F32)
        dv_acc[...] = jnp.zeros(dv_acc.shape, F32)
        db_ref[...] = jnp.zeros(db_ref.shape, F32)
        masks = _head_masks()

        def operands(b):
            r0 = pl.multiple_of(b * Q_BLOCK, Q_BLOCK)
            q2 = _stack_heads(q_ref[pl.ds(r0 + Z_PAD, Q_BLOCK), :], masks)
            do2 = _stack_heads(do_ref[pl.ds(r0, Q_BLOCK), :], masks)
            return r0, q2, do2, k_ref[pl.ds(r0, K_SPAN), :]

        def ahead(b, half):
            r0, q2, do2, kcat = operands(b)
            s_scr[half] = _dot(q2, kcat, NT)
            dp_scr[half] = _dot(do2, v_ref[pl.ds(r0, K_SPAN), :], NT)

        def finish(b, half):
            r0, q2, do2, kcat = operands(b)
            p_strips, ds_strips = [], []
            for t, p in enumerate(_softmax_strips(s_scr.at[half], bt_ref, b)):
                hh, r = divmod(t * STRIP, Q_BLOCK)
                dp_t = dp_scr[half, t * STRIP:(t + 1) * STRIP, :]
                ds = p * (dp_t - jnp.sum(p * dp_t, axis=-1, keepdims=True))
                db_ref[hh, r:r + STRIP, :] += ds
                p_strips.append(p.astype(BF16))
                ds_strips.append(ds.astype(BF16))
            dq = _dot(_side_by_side_strips(ds_strips), _stack_heads(kcat, masks))
            dqkv_ref[0, pl.ds(r0, Q_BLOCK), :] = (dq * Q_SCALE).astype(BF16)
            dk_acc[pl.ds(r0, K_SPAN), :] += _dot(jnp.concatenate(ds_strips, axis=0), q2, TN)
            dv_acc[pl.ds(r0, K_SPAN), :] += _dot(jnp.concatenate(p_strips, axis=0), do2, TN)

        def two_blocks(i, carry):
            b = 2 * i
            ahead(b + 1, 1)
            finish(b, 0)
            ahead(jnp.minimum(b + 2, nb - 1), 0)
            finish(b + 1, 1)
            return carry

        ahead(0, 0)
        lax.fori_loop(0, nb // 2, two_blocks, 0)
        dqkv_ref[1] = dk_acc[Z_PAD:, :].astype(BF16)
        dqkv_ref[2] = dv_acc[Z_PAD:, :].astype(BF16)

        @pl.when(p_id == pairs - 1)
        def _():
            gather.finish()
            for cp in keep:
                cp.wait()
            for cp in _owner_copies(to_chip_refs, from_chip_refs, send_sems, recv_sems):
                cp.wait_recv()
                cp.wait_send()

    hbm = pl.BlockSpec(memory_space=pl.ANY)
    lands = ((N_DEV, SLAB_ROWS, D_MODEL), (N_DEV,) + ws_shape)
    return pl.pallas_call(
        body, name="attn_bwd",
        grid=(pairs,),
        in_specs=_attn_specs(s + Z_PAD) + [pl.BlockSpec((s, 128), lambda p: (0, p))] + [hbm] * n
        + [pl.BlockSpec(a.shape, functools.partial(lambda nd, p: (0,) * nd, a.ndim)) for a in small],
        out_specs=(pl.BlockSpec((3, s, 128), lambda p: (0, 0, p)),
                   pl.BlockSpec((2, Q_BLOCK, K_SPAN), lambda p: (p, 0, 0))) + (hbm,) * (n + 2),
        out_shape=(jax.ShapeDtypeStruct((3, s, D_A), BF16),
                   jax.ShapeDtypeStruct((N_HEADS, Q_BLOCK, K_SPAN), F32))
        + tuple(jax.ShapeDtypeStruct(t.shape, t.dtype) for t in to_chip)
        + tuple(jax.ShapeDtypeStruct(shape, F32) for shape in lands),
        scratch_shapes=[pltpu.VMEM((s + Z_PAD, 128), F32), pltpu.VMEM((s + Z_PAD, 128), F32),
                        pltpu.VMEM((2, 2 * Q_BLOCK, K_SPAN), F32), pltpu.VMEM((2, 2 * Q_BLOCK, K_SPAN), F32)]
        + [pltpu.VMEM(shape[1:], F32) for shape in lands]
        + [pltpu.SemaphoreType.DMA((n, 3)), pltpu.SemaphoreType.DMA((n, 3)),
           pltpu.SemaphoreType.DMA((2, N_DEV - 1)), pltpu.SemaphoreType.DMA((2, N_DEV - 1)),
           pltpu.SemaphoreType.DMA((2,))],
        compiler_params=_params(56),
    )(qkv, qkv, qkv, bias_table, d_out, *to_chip, *small)


def _mid_fwd_bwd(x, target, attn_out, z, w_pa, w_pb, w_out, b_gate, ln_g, ln_b, w_s, b_s, final_g):
    s = x.shape[0]
    tm = TOKEN_TILE
    nt = s // tm

    def body(x_ref, t_ref, oa_ref, ga_ref, ub_ref, vb_ref, gb_ref, ta0_ref, ta1_ref, tb0_ref, tb1_ref,
             wpa_hbm, wpb_hbm, wout_hbm, bg_ref, lng_ref, lnb_ref, ws_ref, bs_ref, fg_ref,
             dx2_ref, doa_ref, dz_ref, dwout_hbm, dwpa_hbm, dwpb_hbm, dbg_ref, dfg_ref, dlng_ref, dlnb_ref, dws_ref,
             dbs_ref, loss_ref,
             wpa, wpb, wout, wmix, acc_out, acc_pa, acc_pb, sem):
        i = pl.program_id(0)

        @pl.when(i == 0)
        def _():
            loads = [pltpu.make_async_copy(src, dst, sem.at[n])
                     for n, (src, dst) in enumerate(((wpa_hbm, wpa), (wpb_hbm, wpb), (wout_hbm, wout)))]
            for cp in loads:
                cp.start()
            t_idx = lax.broadcasted_iota(jnp.int32, (SGU_CHUNK, SGU_CHUNK), 0)
            s_idx = lax.broadcasted_iota(jnp.int32, (SGU_CHUNK, SGU_CHUNK), 1)
            for g in range(N_GROUPS):
                wmix[g] = jnp.where(s_idx <= t_idx, ws_ref[g], 0.0).astype(BF16)
            for ref in (acc_out, acc_pa, acc_pb, dbg_ref, dfg_ref, dlng_ref, dlnb_ref, dws_ref, dbs_ref, loss_ref):
                ref[...] = jnp.zeros(ref.shape, F32)
            for cp in loads:
                cp.wait()

        def tile_fwd_bwd(rows):
            g_a = ga_ref[rows, :].astype(F32)
            u_b = ub_ref[rows, :].astype(F32)
            v_b = vb_ref[rows, :].astype(F32)
            g_b = gb_ref[rows, :].astype(F32)
            bg = bg_ref[...]
            sg_a = _sigmoid(g_a)
            silu_a = g_a * sg_a
            o_a = oa_ref[rows, :]
            y_a = (o_a * silu_a).astype(BF16)
            ug, dgelu_u = _gelu_and_grad(u_b)
            vg, dgelu_v = _gelu_and_grad(v_b)
            mu = jnp.mean(vg, axis=-1, keepdims=True)
            vc = vg - mu
            rstd = lax.rsqrt(jnp.mean(vc * vc, axis=-1, keepdims=True) + EPS)
            vhat = vc * rstd
            lng = lng_ref[...]
            vn = (vhat * lng + lnb_ref[...]).astype(BF16)
            sg_b = _sigmoid(g_b)
            silu_b = g_b * sg_b
            subs = [slice(n * SGU_CHUNK, (n + 1) * SGU_CHUNK) for n in range(tm // SGU_CHUNK)]
            mixed = jnp.concatenate([jnp.concatenate(
                [_dot(wmix[g], vn[sub, g * 128:(g + 1) * 128]) + bs_ref[g] for g in range(N_GROUPS)], axis=1)
                for sub in subs], axis=0)
            um = ug * mixed
            y_b = (um * silu_b).astype(BF16)
            gate_a = _sigmoid(jnp.concatenate([ta0_ref[rows, :], ta1_ref[rows, :]], axis=1).astype(F32)
                              + bg[:, :D_MODEL])
            gate_b = _sigmoid(jnp.concatenate([tb0_ref[rows, :], tb1_ref[rows, :]], axis=1).astype(F32)
                              + bg[:, D_MODEL:])
            p_a = _dot(y_a, wpa[...])
            p_b = _dot(y_b, wpb[...])
            merged = (gate_a * p_a + gate_b * p_b).astype(BF16)
            x2 = x_ref[rows, :] + _dot(merged, wout[...])
            r2 = lax.rsqrt(jnp.mean(x2 * x2, axis=-1, keepdims=True) + EPS)
            xh = x2 * r2
            fg = fg_ref[...]
            err = xh * fg - t_ref[rows, :]
            loss_ref[...] += jnp.sum(jnp.sum(err * err, axis=-1, keepdims=True), axis=0, keepdims=True) * (0.5 / D_MODEL)
            dy = err * (1.0 / D_MODEL)
            dfg_ref[...] += jnp.sum(dy * xh, axis=0, keepdims=True)
            gy = dy * fg
            dx2 = r2 * (gy - xh * jnp.mean(gy * xh, axis=-1, keepdims=True))
            dx2_ref[rows, :] = dx2
            dx2b = dx2.astype(BF16)
            dmerged = _dot(dx2b, wout[...], NT)
            acc_out[...] += _dot(merged, dx2b, TN)
            dp_a = dmerged * gate_a
            dp_b = dmerged * gate_b
            dgate_a = dp_a * p_a * (1.0 - gate_a)
            dgate_b = dp_b * p_b * (1.0 - gate_b)
            dbg_ref[:, :D_MODEL] += jnp.sum(dgate_a, axis=0, keepdims=True)
            dbg_ref[:, D_MODEL:] += jnp.sum(dgate_b, axis=0, keepdims=True)
            dz_ref[rows, 2048:3072] = dgate_a.astype(BF16)
            dz_ref[rows, 3072:4096] = dgate_b.astype(BF16)
            dp_ab = dp_a.astype(BF16)
            dp_bb = dp_b.astype(BF16)
            dy_a = _dot(dp_ab, wpa[...], NT)
            dy_b = _dot(dp_bb, wpb[...], NT)
            acc_pa[...] += _dot(y_a, dp_ab, TN)
            acc_pb[...] += _dot(y_b, dp_bb, TN)
            doa_ref[rows, :] = (dy_a * silu_a).astype(BF16)
            dz_ref[rows, 0:512] = (dy_a * o_a * (sg_a * (1.0 + g_a * (1.0 - sg_a)))).astype(BF16)
            dz_ref[rows, 1536:2048] = (dy_b * um * (sg_b * (1.0 + g_b * (1.0 - sg_b)))).astype(BF16)
            dys = dy_b * silu_b
            dz_ref[rows, 512:1024] = (dys * mixed * dgelu_u).astype(BF16)
            dmixed = dys * ug
            dmb = dmixed.astype(BF16)
            dvn_rows = []
            for sub in subs:
                dvn_parts = []
                for g in range(N_GROUPS):
                    cols = slice(g * 128, (g + 1) * 128)
                    dws_ref[g] += _dot(dmb[sub, cols], vn[sub, cols], NT)
                    dbs_ref[g] += jnp.sum(dmixed[sub, cols], axis=-1, keepdims=True)
                    dvn_parts.append(_dot(wmix[g], dmb[sub, cols], TN))
                dvn_rows.append(jnp.concatenate(dvn_parts, axis=1))
            dvn = jnp.concatenate(dvn_rows, axis=0)
            dlng_ref[...] += jnp.sum(dvn * vhat, axis=0, keepdims=True)
            dlnb_ref[...] += jnp.sum(dvn, axis=0, keepdims=True)
            dvh = dvn * lng
            dvg = rstd * (dvh - jnp.mean(dvh, axis=-1, keepdims=True)
                          - vhat * jnp.mean(dvh * vhat, axis=-1, keepdims=True))
            dz_ref[rows, 1024:1536] = (dvg * dgelu_v).astype(BF16)

        tile_fwd_bwd(slice(0, tm))

        @pl.when(i == nt - 1)
        def _():
            t_idx = lax.broadcasted_iota(jnp.int32, (SGU_CHUNK, SGU_CHUNK), 0)
            s_idx = lax.broadcasted_iota(jnp.int32, (SGU_CHUNK, SGU_CHUNK), 1)
            for g in range(N_GROUPS):
                dws_ref[g] = jnp.where(s_idx <= t_idx, dws_ref[g], 0.0)
            stores = [pltpu.make_async_copy(src, dst, sem.at[n])
                      for n, (src, dst) in enumerate(((acc_out, dwout_hbm), (acc_pa, dwpa_hbm), (acc_pb, dwpb_hbm)))]
            for cp in stores:
                cp.start()
            for cp in stores:
                cp.wait()

    tile = lambda w: pl.BlockSpec((tm, w), lambda i: (i, 0))
    whole = lambda shape: pl.BlockSpec(shape, lambda i: (0,) * len(shape))
    hbm = pl.BlockSpec(memory_space=pl.ANY)
    return pl.pallas_call(
        body, name="mid_fwd_bwd",
        grid=(nt,),
        in_specs=[tile(D_MODEL), tile(D_MODEL), tile(D_A)]
        + [pl.BlockSpec((tm, COL_BLOCK), functools.partial(lambda c, i: (i + Z_PAD // tm, c), c))
           for c in range(3, N_COL_BLOCKS)]
        + [hbm, hbm, hbm,
                  whole((1, 2 * D_MODEL)), whole((1, D_B)), whole((1, D_B)),
                  whole((N_GROUPS, SGU_CHUNK, SGU_CHUNK)), whole((N_GROUPS, SGU_CHUNK, 1)), whole((1, D_MODEL))],
        out_specs=(tile(D_MODEL), tile(D_A), tile(REST), hbm, hbm, hbm,
                   whole((1, 2 * D_MODEL)), whole((1, D_MODEL)), whole((1, D_B)), whole((1, D_B)),
                   whole((N_GROUPS, SGU_CHUNK, SGU_CHUNK)), whole((N_GROUPS, SGU_CHUNK, 1)), whole((1, 1))),
        out_shape=(jax.ShapeDtypeStruct((s, D_MODEL), F32), jax.ShapeDtypeStruct((s, D_A), BF16),
                   jax.ShapeDtypeStruct((s, REST), BF16),
                   jax.ShapeDtypeStruct((D_MODEL, D_MODEL), F32), jax.ShapeDtypeStruct((D_A, D_MODEL), F32),
                   jax.ShapeDtypeStruct((D_B, D_MODEL), F32),
                   jax.ShapeDtypeStruct((1, 2 * D_MODEL), F32), jax.ShapeDtypeStruct((1, D_MODEL), F32),
                   jax.ShapeDtypeStruct((1, D_B), F32), jax.ShapeDtypeStruct((1, D_B), F32),
                   jax.ShapeDtypeStruct((N_GROUPS, SGU_CHUNK, SGU_CHUNK), F32),
                   jax.ShapeDtypeStruct((N_GROUPS, SGU_CHUNK, 1), F32), jax.ShapeDtypeStruct((1, 1), F32)),
        scratch_shapes=[pltpu.VMEM((D_A, D_MODEL), BF16), pltpu.VMEM((D_B, D_MODEL), BF16),
                        pltpu.VMEM((D_MODEL, D_MODEL), BF16), pltpu.VMEM((N_GROUPS, SGU_CHUNK, SGU_CHUNK), BF16),
                        pltpu.VMEM((D_MODEL, D_MODEL), F32), pltpu.VMEM((D_A, D_MODEL), F32),
                        pltpu.VMEM((D_B, D_MODEL), F32),
                        pltpu.SemaphoreType.DMA((3,))],
        compiler_params=_params(56),
    )(x, target, attn_out, *([z] * (N_COL_BLOCKS - 3)), w_pa, w_pb, w_out, b_gate, ln_g, ln_b, w_s, b_s, final_g)


def _proj_bwd_x(dqkv, drest, x, dx2, norm_g, w_in_t, to_chip, small):
    s = x.shape[0]
    tm = 512 if s % 512 == 0 else TOKEN_TILE
    nt = s // tm
    rows = to_chip.shape[1]
    half = D_MODEL // 2
    left, right = slice(0, half), slice(half, D_MODEL)

    def body(dqkv_ref, dr_ref, x_ref, dx2_ref, g_ref, w_hbm, tc_hbm, ng_ref, rel_ref,
             dx_ref, fc_ref, slab_land,
             w, tc_ref, slab_stage, via_x, via_y, mine, out_x, out_y, sem, send_sems, recv_sems,
             gather_send, gather_recv, keep_sems):
        i = pl.program_id(0)
        x_, y_, c_ = _my_pos()
        me = _flat_id((x_, y_, c_))
        xn, yn = (1 - x_, y_, c_), (x_, 1 - y_, c_)
        gather = _SlotGather([slab_land], gather_send, gather_recv, own=[slab_stage])
        keep = [pltpu.make_async_copy(slab_stage, slab_land.at[me], keep_sems.at[0])]

        def copy(k, src, dst, to):
            return pltpu.make_async_remote_copy(src_ref=src, dst_ref=dst, send_sem=send_sems.at[k],
                                                recv_sem=recv_sems.at[k], device_id=to, device_id_type=MESH)

        first = [copy(0, tc_ref.at[0, :, left], fc_ref.at[0, :, left], xn), copy(1, tc_ref.at[2, :, left], via_x, xn),
                 copy(2, tc_ref.at[1, :, right], fc_ref.at[1, :, right], yn), copy(3, tc_ref.at[2, :, right], via_y, yn)]
        second = [copy(4, out_y, fc_ref.at[1, :, left], yn), copy(5, out_x, fc_ref.at[0, :, right], xn)]

        def add_and_send(arrival, landed, own_half, stage, onward):
            load = pltpu.make_async_copy(own_half, mine, sem)
            load.start()
            arrival.wait_recv()
            load.wait()
            stage[...] = (mine[...].astype(F32) + landed[...].astype(F32)).astype(BF16)
            onward.start()

        @pl.when(i == 0)
        def _():
            cp = pltpu.make_async_copy(w_hbm, w, sem)
            cp.start()
            _fill_slab(slab_stage, ((ROW_NORM_G, ng_ref), (ROW_REL, rel_ref)))
            for cp_keep in keep:
                cp_keep.start()
            gather.start()
            stage_in = pltpu.make_async_copy(tc_hbm, tc_ref, keep_sems.at[2])
            stage_in.start()
            stage_in.wait()
            for rc in first:
                rc.start()
            cp.wait()

        @pl.when(i == (5 * nt) // 8)
        def _():
            gather.pass_on()
            add_and_send(first[1], via_x, tc_ref.at[1, :, left], out_y, second[0])
            add_and_send(first[3], via_y, tc_ref.at[0, :, right], out_x, second[1])

        dh = None
        for c in range(N_COL_BLOCKS):
            dz = dqkv_ref[c] if c < 3 else dr_ref[:, (c - 3) * COL_BLOCK:(c - 2) * COL_BLOCK]
            part = _dot(dz, w[c * COL_BLOCK:(c + 1) * COL_BLOCK, :])
            dh = part if dh is None else dh + part
        xf = x_ref[...]
        r = lax.rsqrt(jnp.mean(xf * xf, axis=-1, keepdims=True) + EPS)
        xn = xf * r
        gh = dh * g_ref[...]
        dx_ref[...] = r * (gh - xn * jnp.mean(gh * xn, axis=-1, keepdims=True)) + dx2_ref[...]

        @pl.when(i == nt - 1)
        def _():
            gather.finish()
            for cp_keep in keep:
                cp_keep.wait()
            for k in (0, 2, 4, 5):
                (first + second)[k].wait_recv()
            for rc in first + second:
                rc.wait_send()

    hbm = pl.BlockSpec(memory_space=pl.ANY)
    whole = lambda a: pl.BlockSpec(a.shape, lambda i: (0,) * a.ndim)
    return pl.pallas_call(
        body, name="proj_bwd_x",
        grid=(nt,),
        in_specs=[pl.BlockSpec((3, tm, D_A), lambda i: (0, i, 0)),
                  pl.BlockSpec((tm, REST), lambda i: (i, 0)),
                  pl.BlockSpec((tm, D_MODEL), lambda i: (i, 0)),
                  pl.BlockSpec((tm, D_MODEL), lambda i: (i, 0)),
                  pl.BlockSpec((1, D_MODEL), lambda i: (0, 0)),
                  hbm, hbm] + [whole(a) for a in small],
        out_specs=(pl.BlockSpec((tm, D_MODEL), lambda i: (i, 0)), hbm, hbm),
        out_shape=(jax.ShapeDtypeStruct((s, D_MODEL), F32), jax.ShapeDtypeStruct((2, rows, D_MODEL), BF16),
                   jax.ShapeDtypeStruct((N_DEV, LATE_SLAB_ROWS, D_MODEL), F32)),
        scratch_shapes=[pltpu.VMEM((D_IN, D_MODEL), BF16), pltpu.VMEM(to_chip.shape, BF16),
                        pltpu.VMEM((LATE_SLAB_ROWS, D_MODEL), F32)]
        + [pltpu.VMEM((rows, half), BF16)] * 5
        + [pltpu.SemaphoreType.DMA, pltpu.SemaphoreType.DMA((6,)), pltpu.SemaphoreType.DMA((6,)),
           pltpu.SemaphoreType.DMA((1, N_DEV - 1)), pltpu.SemaphoreType.DMA((1, N_DEV - 1)),
           pltpu.SemaphoreType.DMA((3,))],
        compiler_params=_params(56),
    )(dqkv, drest, x, dx2, norm_g, w_in_t, to_chip, *small)


def _proj_bwd_w(xn, dqkv, drest, norm_g, w_in_t):
    s = xn.shape[0]
    tk = 2048 if s % 2048 == 0 else min(s, 1024)
    nk = s // tk

    def body(xn_ref, dqkv_ref, dr_ref, g_ref, w_ref, o_ref, dg_ref, acc):
        j = pl.program_id(0)
        i = pl.program_id(1)

        @pl.when((j == 0) & (i == 0))
        def _():
            dg_ref[...] = jnp.zeros(dg_ref.shape, F32)

        @pl.when(i == 0)
        def _():
            acc[...] = jnp.zeros(acc.shape, F32)

        @pl.when(j < 3)
        def _():
            acc[...] += _dot(dqkv_ref[...], xn_ref[...], TN)

        @pl.when(j >= 3)
        def _():
            acc[...] += _dot(dr_ref[...], xn_ref[...], TN)

        @pl.when(i == nk - 1)
        def _():
            m = acc[...]
            o_ref[...] = (m * g_ref[...]).astype(BF16)
            dg_ref[...] += jnp.sum(m * w_ref[...].astype(F32), axis=0, keepdims=True)

    return pl.pallas_call(
        body, name="proj_bwd_w",
        grid=(N_COL_BLOCKS, nk),
        in_specs=[pl.BlockSpec((tk, D_MODEL), lambda j, i: (i, 0)),
                  pl.BlockSpec((None, tk, COL_BLOCK),
                               lambda j, i: (jnp.minimum(j, 2), jnp.where(j < 3, i, nk - 1), 0)),
                  pl.BlockSpec((tk, COL_BLOCK),
                               lambda j, i: (jnp.where(j >= 3, i, 0), jnp.maximum(j - 3, 0))),
                  pl.BlockSpec((1, D_MODEL), lambda j, i: (0, 0)),
                  pl.BlockSpec((COL_BLOCK, D_MODEL), lambda j, i: (j, 0))],
        out_specs=(pl.BlockSpec((COL_BLOCK, D_MODEL), lambda j, i: (j, 0)),
                   pl.BlockSpec((1, D_MODEL), lambda j, i: (0, 0))),
        out_shape=(jax.ShapeDtypeStruct((D_IN, D_MODEL), BF16), jax.ShapeDtypeStruct((1, D_MODEL), F32)),
        scratch_shapes=[pltpu.VMEM((COL_BLOCK, D_MODEL), F32)],
        compiler_params=_params(40),
    )(xn, dqkv, drest, norm_g, w_in_t)


def _adamw_math(w, g, m, v):
    c1 = 1.0 - ADAM_B1 ** ADAM_STEP
    c2 = 1.0 - ADAM_B2 ** ADAM_STEP
    nm = ADAM_B1 * m + (1.0 - ADAM_B1) * g
    nv = ADAM_B2 * v + (1.0 - ADAM_B2) * (g * g)
    return -ADAM_LR * ((nm / c1) / (jnp.sqrt(nv / c2) + ADAM_EPS) + ADAM_WD * w), nm, nv


def _adamw(name, w, g, m, v, from_chip):
    rows, cols = w.shape
    tr = rows if rows * cols <= 512 * 1024 else next(t for t in range(256, 7, -8) if rows % t == 0)

    def body(w_ref, g_ref, m_ref, v_ref, t_ref, g_out, d_ref, nm_ref, nv_ref):
        gg = g_ref[...]
        for j in range(from_chip.shape[0]):
            gg = gg + t_ref[j].astype(F32)
        g_out[...] = gg
        d_ref[...], nm_ref[...], nv_ref[...] = _adamw_math(w_ref[...], gg, m_ref[...], v_ref[...])

    spec = pl.BlockSpec((tr, cols), lambda i: (i, 0))
    shape = jax.ShapeDtypeStruct((rows, cols), F32)
    return pl.pallas_call(
        body, name=name,
        grid=(rows // tr,),
        in_specs=[spec] * 4 + [pl.BlockSpec((from_chip.shape[0], tr, cols), lambda i: (0, i, 0))],
        out_specs=(spec,) * 4, out_shape=(shape,) * 4,
        compiler_params=_params(32),
    )(w, g, m, v, from_chip)


_SMALL = (("norm_g", (1, D_MODEL)), ("b_gate", (1, 2 * D_MODEL)), ("rel_bias", (N_HEADS, N_REL)),
          ("sgu_ln_g", (1, D_B)), ("sgu_ln_b", (1, D_B)), ("w_s", (N_GROUPS * SGU_CHUNK, SGU_CHUNK)),
          ("b_s", (N_GROUPS, SGU_CHUNK)), ("final_g", (1, D_MODEL)))


def _adamw_small(slabs, ws_all, late_slabs, weights, moments_m, moments_v):
    k = len(_SMALL)

    def total(ref):
        acc = ref[0]
        for d in range(1, N_DEV):
            acc = acc + ref[d]
        return acc

    def body(*refs):
        slab_ref, ws_ref, late_ref = refs[:3]
        w_refs, m_refs, v_refs = refs[3:3 + k], refs[3 + k:3 + 2 * k], refs[3 + 2 * k:3 + 3 * k]
        outs = refs[3 + 3 * k:]
        slab, late = total(slab_ref), total(late_ref)
        grads = {
            "norm_g": late[ROW_NORM_G:ROW_NORM_G + 1, :],
            "b_gate": jnp.concatenate([slab[ROW_B_GATE:ROW_B_GATE + 1, :], slab[ROW_B_GATE + 1:ROW_B_GATE + 2, :]], axis=1),
            "rel_bias": late[ROW_REL:ROW_REL + N_HEADS, :N_REL],
            "sgu_ln_g": slab[ROW_LN_G:ROW_LN_G + 1, :D_B],
            "sgu_ln_b": slab[ROW_LN_B:ROW_LN_B + 1, :D_B],
            "w_s": total(ws_ref),
            "b_s": slab[ROW_B_S:ROW_B_S + N_GROUPS, :SGU_CHUNK],
            "final_g": slab[ROW_FINAL_G:ROW_FINAL_G + 1, :],
        }
        for n, (name, _) in enumerate(_SMALL):
            g = grads[name]
            outs[n][...] = g
            outs[k + n][...], outs[2 * k + n][...], outs[3 * k + n][...] = _adamw_math(
                w_refs[n][...], g, m_refs[n][...], v_refs[n][...])
        outs[4 * k][...] = slab[ROW_LOSS:ROW_LOSS + 1, :1]

    vmem = pl.BlockSpec(memory_space=pltpu.VMEM)
    shapes = tuple(jax.ShapeDtypeStruct(shape, F32) for _, shape in _SMALL)
    return pl.pallas_call(
        body, name="adamw_small",
        out_shape=shapes * 4 + (jax.ShapeDtypeStruct((1, 1), F32),),
        in_specs=[vmem] * (3 + 3 * k), out_specs=tuple([vmem] * (4 * k + 1)),
        compiler_params=_params(16),
    )(slabs, ws_all, late_slabs, *weights, *moments_m, *moments_v)


def _pad_rel(a):
    return jnp.pad(a.reshape(N_HEADS, N_REL), ((0, 0), (0, N_REL_PAD - N_REL)))


def kernel(x, norm_g, w_in, b_gate, rel_bias, sgu_ln_g, sgu_ln_b, w_s, b_s, w_pa, w_pb, w_out, final_g, loss_target, m_norm_g, m_w_in, m_b_gate, m_rel_bias, m_sgu_ln_g, m_sgu_ln_b, m_w_s, m_b_s, m_w_pa, m_w_pb, m_w_out, m_final_g, v_norm_g, v_w_in, v_b_gate, v_rel_bias, v_sgu_ln_g, v_sgu_ln_b, v_w_s, v_b_s, v_w_pa, v_w_pb, v_w_out, v_final_g):
    s = x.shape[1]
    xs = x.reshape(s, D_MODEL)
    tgt = loss_target.reshape(s, D_MODEL)

    bias_table = _bias_table(_pad_rel(rel_bias))
    w_in_t = jnp.swapaxes(w_in[0], 0, 1)
    qkv, x_norm, w_in_t_full = _gather_proj_fwd(xs, norm_g, w_in_t)
    attn_out, g_pa, g_pb, g_out = _attn_fwd(qkv, bias_table, (w_pa[0], w_pb[0], w_out[0]))
    w_pa_full = jnp.transpose(g_pa, (1, 0, 2)).reshape(D_A, D_MODEL)
    w_pb_full = jnp.transpose(g_pb, (1, 0, 2)).reshape(D_B, D_MODEL)
    w_out_full = g_out.reshape(D_MODEL, D_MODEL)

    (dx2, d_attn, drest, dw_out, dw_pa, dw_pb, d_bgate, d_fg, d_lng, d_lnb, d_ws, d_bs, loss_part) = _mid_fwd_bwd(
        xs, tgt, attn_out, qkv, w_pa_full, w_pb_full, w_out_full, b_gate, sgu_ln_g, sgu_ln_b, w_s[0],
        b_s.reshape(N_GROUPS, SGU_CHUNK, 1), final_g.reshape(1, D_MODEL))

    own_pa, own_pb, own_out, tc_pa, tc_pb, tc_out = _reduce_chip(
        "reduce_chip_proj", (dw_pa, dw_pb, dw_out), (1, 1, 0))
    dqkv, dbias, fc_pa, fc_pb, fc_out, slabs, ws_all = _attn_bwd(
        qkv, bias_table, d_attn, (tc_pa, tc_pb, tc_out),
        (d_bgate, d_lng, d_lnb, d_fg, loss_part, d_bs, d_ws.reshape(N_GROUPS * SGU_CHUNK, SGU_CHUNK)))
    d_rel = _bias_grad(dbias)
    dw_in_t, d_ng = _proj_bwd_w(x_norm, dqkv, drest, norm_g, w_in_t_full)
    own_in, tc_in = _reduce_chip("reduce_chip_in", (dw_in_t,), (0,))
    grad_x, fc_in, late_slabs = _proj_bwd_x(dqkv, drest, xs, dx2, norm_g, w_in_t_full, tc_in, (d_ng, d_rel))
    big = {"w_in": tuple(jnp.swapaxes(t, 0, 1)[None] for t in _adamw(
        "adamw_w_in", w_in_t, own_in, jnp.swapaxes(m_w_in[0], 0, 1), jnp.swapaxes(v_w_in[0], 0, 1), fc_in))}
    for name, w, g, fc, m, v in (("w_pa", w_pa, own_pa, fc_pa, m_w_pa, v_w_pa),
                                 ("w_pb", w_pb, own_pb, fc_pb, m_w_pb, v_w_pb),
                                 ("w_out", w_out, own_out, fc_out, m_w_out, v_w_out)):
        big[name] = tuple(t[None] for t in _adamw("adamw_" + name, w[0], g, m[0], v[0], fc))

    as_2d = lambda leaves: [a.reshape(shape) for a, (_, shape) in zip(leaves, _SMALL)]
    small_out = _adamw_small(
        slabs, ws_all, late_slabs, as_2d((norm_g, b_gate, rel_bias, sgu_ln_g, sgu_ln_b, w_s, b_s, final_g)),
        as_2d((m_norm_g, m_b_gate, m_rel_bias, m_sgu_ln_g, m_sgu_ln_b, m_w_s, m_b_s, m_final_g)),
        as_2d((v_norm_g, v_b_gate, v_rel_bias, v_sgu_ln_g, v_sgu_ln_b, v_w_s, v_b_s, v_final_g)))
    small_index = {name: n for n, (name, _) in enumerate(_SMALL)}

    def leaf(kind, name, like):
        if name in big:
            return big[name][kind]
        return small_out[kind * len(_SMALL) + small_index[name]].reshape(like.shape)

    weights = (("norm_g", norm_g), ("w_in", w_in), ("b_gate", b_gate), ("rel_bias", rel_bias), ("sgu_ln_g", sgu_ln_g),
               ("sgu_ln_b", sgu_ln_b), ("w_s", w_s), ("b_s", b_s), ("w_pa", w_pa), ("w_pb", w_pb), ("w_out", w_out),
               ("final_g", final_g))
    outs = [small_out[-1].reshape(()), grad_x.reshape(x.shape)]
    for kind in range(4):
        outs.extend(leaf(kind, name, like) for name, like in weights)
    return tuple(outs)
```

```python
import functools
import math

import jax
import jax.numpy as jnp
from jax import lax
from jax.experimental import pallas as pl
from jax.experimental.pallas import tpu as pltpu

F32 = jnp.float32
BF16 = jnp.bfloat16
MESH = pl.DeviceIdType.MESH
N_DEV = 8

D_MODEL = 1024
D_A = 512
D_B = 512
D_IN = 5632
N_HEADS = 8
HEAD_DIM = 64
N_PREV = 8
REL_CLIP = 128
N_REL = 2 * REL_CLIP + 1
N_REL_PAD = 384
SGU_CHUNK = 128
N_GROUPS = 4
EPS = 1e-6
NEG_INF = -1e30
Q_SCALE = HEAD_DIM ** -0.5

Q_BLOCK = 256
K_SPAN = 768
Z_PAD = K_SPAN - Q_BLOCK
ROLL_W = 1024
COL_BLOCK = 512
N_COL_BLOCKS = D_IN // COL_BLOCK
REST = D_IN - 3 * D_A
TOKEN_TILE = 256

ADAM_LR = 0.001
ADAM_B1 = 0.9
ADAM_B2 = 0.999
ADAM_EPS = 1e-08
ADAM_WD = 0.01
ADAM_STEP = 10

GELU_C = math.sqrt(2.0 / math.pi)
GELU_A = 0.044715

NT = (((1,), (1,)), ((), ()))
TN = (((0,), (0,)), ((), ()))
HIGHEST = lax.Precision.HIGHEST


def _params(vmem_mb, **kw):
    return pltpu.CompilerParams(vmem_limit_bytes=vmem_mb * 1024 * 1024, **kw)


def _dot(a, b, dims=None):
    if dims is None:
        return jnp.dot(a, b, preferred_element_type=F32)
    return lax.dot_general(a, b, dims, preferred_element_type=F32)


def _sigmoid(x):
    return 0.5 * jnp.tanh(0.5 * x) + 0.5


def _gelu_and_grad(u):
    u2 = u * u
    t = jnp.tanh(GELU_C * (u + GELU_A * u * u2))
    half = 0.5 * (1.0 + t)
    g = u * half
    dg = half + 0.5 * u * (1.0 - t * t) * (GELU_C * (1.0 + 3.0 * GELU_A * u2))
    return g, dg


def _my_pos():
    return lax.axis_index("x"), lax.axis_index("y"), lax.axis_index("c")


def _flat_id(pos):
    return 4 * pos[0] + 2 * pos[1] + pos[2]


def _other_chips(pos):
    x, y, _ = pos
    return ((1 - x, y), (x, 1 - y), (1 - x, 1 - y))


class _SlotGather:
    def __init__(self, bufs, send_sems, recv_sems, own=None):
        self.bufs, self.send_sems, self.recv_sems = bufs, send_sems, recv_sems
        self.own = own if own is not None else [None] * len(bufs)
        x, y, c = _my_pos()
        self.c, self.me, self.sib = c, (x, y, c), (x, y, 1 - c)
        self.chips = _other_chips(self.me)

    def _copy(self, a, k, block, to):
        slot = _flat_id(block)
        src = self.own[a] if (k < 4 and self.own[a] is not None) else self.bufs[a].at[slot]
        return pltpu.make_async_remote_copy(
            src_ref=src, dst_ref=self.bufs[a].at[slot],
            send_sem=self.send_sems.at[a, k], recv_sem=self.recv_sems.at[a, k], device_id=to, device_id_type=MESH)

    def _own_sends(self):
        n = len(self.bufs)
        return ([self._copy(a, 1 + j, self.me, (*chip, self.c)) for j, chip in enumerate(self.chips) for a in range(n)]
                + [self._copy(a, 0, self.me, self.sib) for a in range(n)])

    def _passes(self):
        return [self._copy(a, 4 + j, (*chip, self.c), self.sib)
                for j, chip in enumerate(self.chips) for a in range(len(self.bufs))]

    def start(self):
        for cp in self._own_sends():
            cp.start()

    def pass_on(self):
        for j, chip in enumerate(self.chips):
            for a in range(len(self.bufs)):
                self._copy(a, 1 + j, (*chip, self.c), self.me).wait_recv()
                self._copy(a, 4 + j, (*chip, self.c), self.sib).start()

    def finish(self):
        for a in range(len(self.bufs)):
            self._copy(a, 0, self.sib, self.me).wait_recv()
            for j, chip in enumerate(self.chips):
                self._copy(a, 4 + j, (*chip, 1 - self.c), self.me).wait_recv()
        for cp in self._own_sends() + self._passes():
            cp.wait_send()


def _reduce_chip(name, parts, sharded_dim):
    n = len(parts)
    shapes = []
    for p, dim in zip(parts, sharded_dim):
        shape = list(p.shape)
        shape[dim] //= N_DEV
        shapes.append(tuple(shape))
    staged = [not (dim == 0 and p.dtype == BF16) for p, dim in zip(parts, sharded_dim)]

    def body(*refs):
        full, own, to_chip = refs[:n], refs[n:2 * n], refs[2 * n:3 * n]
        ins, from_sib = refs[3 * n:4 * n], refs[4 * n:5 * n]
        send_sems, recv_sems = refs[5 * n], refs[5 * n + 1]
        x, y, c = _my_pos()
        sib = (x, y, 1 - c)
        chips = ((x, y),) + _other_chips((x, y, c))
        for a in range(n):
            rows, cols = shapes[a]
            for d in range(N_DEV if staged[a] else 0):
                if sharded_dim[a] == 0:
                    ins[a][d] = full[a][d * rows:(d + 1) * rows, :].astype(BF16)
                else:
                    ins[a][d] = full[a][:, d * cols:(d + 1) * cols].astype(BF16)

        def block(a, d):
            if staged[a]:
                return ins[a].at[d]
            rows = shapes[a][0]
            return full[a].at[pl.ds(pl.multiple_of(d * rows, 16), rows), :]

        def to_sibling(a, r):
            return pltpu.make_async_remote_copy(
                src_ref=block(a, _flat_id((*chips[r], 1 - c))), dst_ref=from_sib[a].at[r],
                send_sem=send_sems.at[a, r], recv_sem=recv_sems.at[a, r], device_id=sib, device_id_type=MESH)

        sends = [to_sibling(a, r) for r in (1, 2, 3, 0) for a in range(n)]
        for cp in sends:
            cp.start()
        for r in (1, 2, 3, 0):
            for a in range(n):
                to_sibling(a, r).wait_recv()
                both = block(a, _flat_id((*chips[r], c)))[...].astype(F32) + from_sib[a][r].astype(F32)
                if r == 0:
                    own[a][...] = both
                else:
                    to_chip[a][r - 1] = both.astype(BF16)
        for cp in sends:
            cp.wait_send()

    vmem = pl.BlockSpec(memory_space=pltpu.VMEM)
    return pl.pallas_call(
        body, name=name,
        out_shape=tuple(jax.ShapeDtypeStruct(sh, F32) for sh in shapes)
        + tuple(jax.ShapeDtypeStruct((3,) + sh, BF16) for sh in shapes),
        in_specs=[vmem] * n, out_specs=tuple([vmem] * (2 * n)),
        scratch_shapes=[pltpu.VMEM((N_DEV if st else 1,) + sh, BF16) for sh, st in zip(shapes, staged)]
        + [pltpu.VMEM((4,) + sh, BF16) for sh in shapes]
        + [pltpu.SemaphoreType.DMA((n, 4)), pltpu.SemaphoreType.DMA((n, 4))],
        compiler_params=_params(56),
    )(*parts)


def _owner_copies(to_chip, from_chip, send_sems, recv_sems):
    x, y, c = _my_pos()
    return [pltpu.make_async_remote_copy(
        src_ref=to_chip[a].at[j], dst_ref=from_chip[a].at[j],
        send_sem=send_sems.at[a, j], recv_sem=recv_sems.at[a, j], device_id=(*chip, c), device_id_type=MESH)
        for a in range(len(to_chip)) for j, chip in enumerate(_other_chips((x, y, c)))]


ROW_B_GATE, ROW_LN_G, ROW_LN_B, ROW_FINAL_G, ROW_LOSS, ROW_B_S, SLAB_ROWS = 1, 3, 4, 5, 6, 16, 24
ROW_NORM_G, ROW_REL, LATE_SLAB_ROWS = 0, 8, 16


def _rel_index(e):
    lo, hi = Z_PAD - REL_CLIP, Z_PAD + REL_CLIP
    return jnp.where(e <= lo, 2 * REL_CLIP, jnp.where(e < hi, hi - e, jnp.where(e <= K_SPAN, 0, 2 * REL_CLIP)))


def _bias_table(rel_bias_pad):
    def body(rb_ref, bt_ref):
        c = lax.broadcasted_iota(jnp.int32, (N_REL_PAD, ROLL_W), 1)
        r = lax.broadcasted_iota(jnp.int32, (N_REL_PAD, ROLL_W), 0)
        pick = (r == _rel_index(c)).astype(F32)
        rows = jnp.dot(rb_ref[...], pick, precision=HIGHEST, preferred_element_type=F32)
        qc = lax.broadcasted_iota(jnp.int32, (Q_BLOCK, K_SPAN), 0) >> 6
        kc = lax.broadcasted_iota(jnp.int32, (Q_BLOCK, K_SPAN), 1) >> 6
        band = (kc >= qc) & (kc <= qc + N_PREV)
        for h in range(N_HEADS):
            t = jnp.broadcast_to(rows[h:h + 1, :], (Q_BLOCK, ROLL_W))
            t = pltpu.roll(t, 0, 1, stride=1, stride_axis=0)
            bt_ref[h] = jnp.where(band, t[:, :K_SPAN], NEG_INF)

    return pl.pallas_call(
        body, name="bias_table",
        out_shape=jax.ShapeDtypeStruct((N_HEADS, Q_BLOCK, K_SPAN), F32),
        compiler_params=_params(32),
    )(rel_bias_pad)


def _bias_grad(dbias):
    def body(a_ref, o_ref):
        rr = lax.broadcasted_iota(jnp.int32, (Q_BLOCK, Q_BLOCK), 0)
        cc = lax.broadcasted_iota(jnp.int32, (Q_BLOCK, Q_BLOCK), 1)
        flip = (rr + cc == Q_BLOCK - 1).astype(F32)
        c = lax.broadcasted_iota(jnp.int32, (ROLL_W, N_REL_PAD), 0)
        r = lax.broadcasted_iota(jnp.int32, (ROLL_W, N_REL_PAD), 1)
        e = jnp.where(c >= Q_BLOCK - 1, c - (Q_BLOCK - 1), c + (ROLL_W - Q_BLOCK + 1))
        pick = (r == _rel_index(e)).astype(F32)
        sums = []
        for h in range(N_HEADS):
            a = jnp.dot(flip, a_ref[h], precision=HIGHEST, preferred_element_type=F32)
            a = jnp.concatenate([a, jnp.zeros((Q_BLOCK, ROLL_W - K_SPAN), F32)], axis=1)
            a = pltpu.roll(a, 0, 1, stride=1, stride_axis=0)
            sums.append(jnp.sum(a, axis=0, keepdims=True))
        diag = jnp.concatenate(sums, axis=0)
        o_ref[...] = jnp.dot(diag, pick, precision=HIGHEST, preferred_element_type=F32)

    return pl.pallas_call(
        body, name="bias_grad",
        out_shape=jax.ShapeDtypeStruct((N_HEADS, N_REL_PAD), F32),
        compiler_params=_params(32),
    )(dbias)


def _gather_proj_fwd(x, norm_g, w_in_t):
    s = x.shape[0]
    tm = 512 if s % 512 == 0 else TOKEN_TILE
    nt = s // tm
    n_pad = Z_PAD // tm
    shard_w = w_in_t.shape[0]
    chip_w = 2 * shard_w
    n_chips = N_DEV // 2

    def body(order_ref, x_ref, g_ref, win_hbm, z_ref, xn_ref, wt_hbm, wt, hb, win_f32, send_sems, recv_sems,
             local_sems):
        j = pl.program_id(0)
        i = pl.program_id(1)
        x_, y_, c_ = _my_pos()
        me, sib = (x_, y_, c_), (x_, y_, 1 - c_)
        near = _other_chips(me)
        pick = lambda a, b: tuple(jnp.where(c_ == 0, u, v) for u, v in zip(a, b))
        passed_from, passed_to = pick(near[0], near[1]), pick(near[1], near[0])

        def rows_of(block):
            return wt.at[pl.ds(pl.multiple_of(_flat_id(block) * shard_w, 16), shard_w), :]

        def copy(k, block, to):
            return pltpu.make_async_remote_copy(
                src_ref=rows_of(block), dst_ref=rows_of(block),
                send_sem=send_sems.at[k], recv_sem=recv_sems.at[k], device_id=to, device_id_type=MESH)

        def sends():
            return ([copy(0, me, sib), copy(1, me, (*near[0], c_)), copy(2, me, (*near[1], c_)),
                     copy(3, (*passed_from, c_), (*passed_to, c_))]
                    + [copy(4 + n, (*near[n], c_), sib) for n in range(3)])

        keep = pltpu.make_async_copy(wt, wt_hbm, local_sems.at[0])

        @pl.when((j == 0) & (i == 0))
        def _():
            load = pltpu.make_async_copy(win_hbm, win_f32, local_sems.at[1])
            load.start()
            load.wait()
            rows_of(me)[...] = win_f32[...].astype(BF16)
            for cp in sends()[:3]:
                cp.start()
            copy(0, sib, me).wait_recv()

        @pl.when((j == 1) & (i == 0))
        def _():
            copy(1, (*near[0], c_), me).wait_recv()
            copy(2, (*near[1], c_), me).wait_recv()
            for cp in sends()[3:6]:
                cp.start()
            copy(4, (*near[0], 1 - c_), me).wait_recv()

        @pl.when((j == 2) & (i == 0))
        def _():
            copy(5, (*near[1], 1 - c_), me).wait_recv()

        @pl.when((j == 3) & (i == 0))
        def _():
            copy(3, (*near[2], c_), me).wait_recv()
            copy(6, (*near[2], c_), sib).start()
            copy(6, (*near[2], 1 - c_), me).wait_recv()
            keep.start()

        @pl.when(i < n_pad)
        def _():
            z_ref[...] = jnp.zeros(z_ref.shape, BF16)

        @pl.when(i >= n_pad)
        def _():
            rows = pl.ds(pl.multiple_of((i - n_pad) * tm, tm), tm)

            @pl.when(j == 0)
            def _():
                xf = x_ref[...]
                xn = xf * lax.rsqrt(jnp.mean(xf * xf, axis=-1, keepdims=True) + EPS)
                hb[rows, :] = (xn * g_ref[...]).astype(BF16)
                xn_ref[...] = xn.astype(BF16)

            chip_rows = pl.ds(pl.multiple_of(order_ref[j] * chip_w, 16), chip_w)
            blk = _dot(hb[rows, :], wt[chip_rows, :], NT)
            q_scale = jnp.where(order_ref[j] == 0, Q_SCALE, 1.0).astype(F32)
            z_ref[:, :D_A] = (blk[:, :D_A] * q_scale).astype(BF16)
            z_ref[:, D_A:] = blk[:, D_A:].astype(BF16)

        @pl.when((j == n_chips - 1) & (i == n_pad + nt - 1))
        def _():
            keep.wait()
            for cp in sends():
                cp.wait_send()

    pos = _my_pos()
    order = jnp.stack([2 * cx + cy for cx, cy in ((pos[0], pos[1]),) + _other_chips(pos)]).astype(jnp.int32)
    first_pass = lambda j, i: jnp.where(j == 0, jnp.maximum(i - n_pad, 0), nt - 1)
    grid_spec = pltpu.PrefetchScalarGridSpec(
        num_scalar_prefetch=1,
        grid=(n_chips, n_pad + nt),
        in_specs=[pl.BlockSpec((tm, D_MODEL), lambda j, i, o: (first_pass(j, i), 0)),
                  pl.BlockSpec((1, D_MODEL), lambda j, i, o: (0, 0)),
                  pl.BlockSpec(memory_space=pl.ANY)],
        out_specs=(pl.BlockSpec((tm, chip_w), lambda j, i, o: (i, o[j])),
                   pl.BlockSpec((tm, D_MODEL), lambda j, i, o: (first_pass(j, i), 0)),
                   pl.BlockSpec(memory_space=pl.ANY)),
        scratch_shapes=[pltpu.VMEM((D_IN, D_MODEL), BF16),
                        pltpu.VMEM((s, D_MODEL), BF16), pltpu.VMEM(w_in_t.shape, F32),
                        pltpu.SemaphoreType.DMA((N_DEV - 1,)), pltpu.SemaphoreType.DMA((N_DEV - 1,)),
                        pltpu.SemaphoreType.DMA((2,))])
    return pl.pallas_call(
        body, name="gather_proj_fwd",
        grid_spec=grid_spec,
        out_shape=(jax.ShapeDtypeStruct((Z_PAD + s, D_IN), BF16), jax.ShapeDtypeStruct((s, D_MODEL), BF16),
                   jax.ShapeDtypeStruct((D_IN, D_MODEL), BF16)),
        compiler_params=_params(60),
    )(order, x, norm_g, w_in_t)


def _attn_specs(rows):
    pairs = N_HEADS // 2
    return ([pl.BlockSpec((rows, 128), functools.partial(lambda which, p: (0, which * pairs + p), which))
             for which in range(3)]
            + [pl.BlockSpec((2, Q_BLOCK, K_SPAN), lambda p: (p, 0, 0))])


def _head_masks():
    lane = lax.broadcasted_iota(jnp.int32, (1, 128), 1)
    first = lane < HEAD_DIM
    return (first, jnp.logical_not(first))


def _stack_heads(x, masks):
    zero = jnp.zeros((), x.dtype)
    return jnp.concatenate([jnp.where(m, x, zero) for m in masks], axis=0)


STRIP = 16


def _softmax_strips(s_ref, bias_ref, b):
    valid = lax.broadcasted_iota(jnp.int32, (1, K_SPAN), 1) >= Z_PAD - b * Q_BLOCK
    for t in range(2 * Q_BLOCK // STRIP):
        hh, r = divmod(t * STRIP, Q_BLOCK)
        st = s_ref[t * STRIP:(t + 1) * STRIP, :] + bias_ref[hh, r:r + STRIP, :]
        st = jnp.where(valid, st, NEG_INF)
        e = jnp.exp(st - jnp.max(st, axis=-1, keepdims=True))
        yield e * (1.0 / jnp.sum(e, axis=-1, keepdims=True))


def _side_by_side_strips(strips):
    half = len(strips) // 2
    return jnp.concatenate([jnp.concatenate([a, c], axis=1) for a, c in zip(strips[:half], strips[half:])], axis=0)


def _attn_fwd(qkv, bias_table, shards):
    s = qkv.shape[0] - Z_PAD
    nb = s // Q_BLOCK
    n = len(shards)
    pairs = N_HEADS // 2

    def body(*refs):
        q_ref, k_ref, v_ref, bt_ref = refs[:4]
        shard_refs = refs[4:4 + n]
        o_ref = refs[4 + n]
        slot_refs = refs[5 + n:5 + 2 * n]
        stages = refs[5 + 2 * n:5 + 3 * n]
        s_scr, send_sems, recv_sems, local_sems = refs[5 + 3 * n:]
        p_id = pl.program_id(0)
        gather = _SlotGather(slot_refs, send_sems, recv_sems, own=stages)
        keep = [pltpu.make_async_copy(stages[a], slot_refs[a].at[_flat_id(_my_pos())], local_sems.at[a])
                for a in range(n)]

        @pl.when(p_id == 0)
        def _():
            for a in range(n):
                stages[a][...] = shard_refs[a][...].astype(BF16)
                keep[a].start()
            gather.start()

        @pl.when(p_id == 2)
        def _():
            gather.pass_on()

        masks = _head_masks()

        def scores(b, half):
            r0 = pl.multiple_of(b * Q_BLOCK, Q_BLOCK)
            q2 = _stack_heads(q_ref[pl.ds(r0 + Z_PAD, Q_BLOCK), :], masks)
            s_scr[half] = _dot(q2, k_ref[pl.ds(r0, K_SPAN), :], NT)

        def finish(b, half):
            r0 = pl.multiple_of(b * Q_BLOCK, Q_BLOCK)
            v2 = _stack_heads(v_ref[pl.ds(r0, K_SPAN), :], masks)
            p = [st.astype(BF16) for st in _softmax_strips(s_scr.at[half], bt_ref, b)]
            o_ref[pl.ds(r0, Q_BLOCK), :] = _dot(_side_by_side_strips(p), v2)

        def two_blocks(i, carry):
            b = 2 * i
            scores(b + 1, 1)
            finish(b, 0)
            scores(jnp.minimum(b + 2, nb - 1), 0)
            finish(b + 1, 1)
            return carry

        scores(0, 0)
        lax.fori_loop(0, nb // 2, two_blocks, 0)

        @pl.when(p_id == pairs - 1)
        def _():
            gather.finish()
            for cp in keep:
                cp.wait()

    hbm = pl.BlockSpec(memory_space=pl.ANY)
    return pl.pallas_call(
        body, name="attn_fwd",
        grid=(pairs,),
        in_specs=_attn_specs(s + Z_PAD) + [pl.BlockSpec(a.shape, lambda p: (0, 0)) for a in shards],
        out_specs=(pl.BlockSpec((s, 128), lambda p: (0, p)),) + (hbm,) * n,
        out_shape=(jax.ShapeDtypeStruct((s, D_A), F32),)
        + tuple(jax.ShapeDtypeStruct((N_DEV,) + a.shape, BF16) for a in shards),
        scratch_shapes=[pltpu.VMEM(a.shape, BF16) for a in shards]
        + [pltpu.VMEM((2, 2 * Q_BLOCK, K_SPAN), F32),
           pltpu.SemaphoreType.DMA((n, N_DEV - 1)), pltpu.SemaphoreType.DMA((n, N_DEV - 1)),
           pltpu.SemaphoreType.DMA((n,))],
        compiler_params=_params(48),
    )(qkv, qkv, qkv, bias_table, *shards)


def _fill_slab(stage, rows):
    stage[...] = jnp.zeros(stage.shape, F32)
    for row, ref in rows:
        r, c = ref.shape
        if c > D_MODEL:
            for part in range(c // D_MODEL):
                stage[row + part:row + part + 1, :] = ref[:, part * D_MODEL:(part + 1) * D_MODEL]
        else:
            stage[row:row + r, :c] = ref[...]


def _attn_bwd(qkv, bias_table, d_out, to_chip, small):
    s = qkv.shape[0] - Z_PAD
    nb = s // Q_BLOCK
    n = len(to_chip)
    pairs = N_HEADS // 2
    ws_shape = small[-1].shape

    def body(*refs):
        q_ref, k_ref, v_ref, bt_ref, do_ref = refs[:5]
        to_chip_refs = refs[5:5 + n]
        bg_ref, lng_ref, lnb_ref, fg_ref, loss_ref, bs_ref, ws_ref = refs[5 + n:12 + n]
        dqkv_ref, db_ref = refs[12 + n:14 + n]
        from_chip_refs = refs[14 + n:14 + 2 * n]
        slab_land, ws_land = refs[14 + 2 * n:16 + 2 * n]
        (dk_acc, dv_acc, s_scr, dp_scr, slab_stage, ws_stage, send_sems, recv_sems, gather_send, gather_recv,
         keep_sems) = refs[16 + 2 * n:]
        p_id = pl.program_id(0)
        me = _flat_id(_my_pos())
        gather = _SlotGather([slab_land, ws_land], gather_send, gather_recv, own=[slab_stage, ws_stage])
        keep = [pltpu.make_async_copy(stage, land.at[me], keep_sems.at[k]) for k, (stage, land) in enumerate(
            ((slab_stage, slab_land), (ws_stage, ws_land)))]

        @pl.when(p_id == 0)
        def _():
            _fill_slab(slab_stage, ((ROW_B_GATE, bg_ref), (ROW_LN_G, lng_ref), (ROW_LN_B, lnb_ref),
                                    (ROW_FINAL_G, fg_ref), (ROW_LOSS, loss_ref)))
            eye = (lax.broadcasted_iota(jnp.int32, (SGU_CHUNK, SGU_CHUNK), 0)
                   == lax.broadcasted_iota(jnp.int32, (SGU_CHUNK, SGU_CHUNK), 1))
            for g in range(N_GROUPS):
                row = jnp.sum(jnp.where(eye, bs_ref[g], 0.0), axis=0, keepdims=True)
                slab_stage[ROW_B_S + g:ROW_B_S + g + 1, :SGU_CHUNK] = row
            ws_stage[...] = ws_ref[...]
            for cp in keep:
                cp.start()
            gather.start()
            for cp in _owner_copies(to_chip_refs, from_chip_refs, send_sems, recv_sems):
                cp.start()

        @pl.when(p_id == 2)
        def _():
            gather.pass_on()

        dk_acc[...] = jnp.zeros(dk_acc.shape, F32)
        dv_acc[...] = jnp.zeros(dv_acc.shape, F32)
        db_ref[...] = jnp.zeros(db_ref.shape, F32)
        masks = _head_masks()

        def operands(b):
            r0 = pl.multiple_of(b * Q_BLOCK, Q_BLOCK)
            q2 = _stack_heads(q_ref[pl.ds(r0 + Z_PAD, Q_BLOCK), :], masks)
            do2 = _stack_heads(do_ref[pl.ds(r0, Q_BLOCK), :], masks)
            return r0, q2, do2, k_ref[pl.ds(r0, K_SPAN), :]

        def ahead(b, half):
            r0, q2, do2, kcat = operands(b)
            s_scr[half] = _dot(q2, kcat, NT)
            dp_scr[half] = _dot(do2, v_ref[pl.ds(r0, K_SPAN), :], NT)

        def finish(b, half):
            r0, q2, do2, kcat = operands(b)
            p_strips, ds_strips = [], []
            for t, p in enumerate(_softmax_strips(s_scr.at[half], bt_ref, b)):
                hh, r = divmod(t * STRIP, Q_BLOCK)
                dp_t = dp_scr[half, t * STRIP:(t + 1) * STRIP, :]
                ds = p * (dp_t - jnp.sum(p * dp_t, axis=-1, keepdims=True))
                db_ref[hh, r:r + STRIP, :] += ds
                p_strips.append(p.astype(BF16))
                ds_strips.append(ds.astype(BF16))
            dq = _dot(_side_by_side_strips(ds_strips), _stack_heads(kcat, masks))
            dqkv_ref[0, pl.ds(r0, Q_BLOCK), :] = (dq * Q_SCALE).astype(BF16)
            dk_acc[pl.ds(r0, K_SPAN), :] += _dot(jnp.concatenate(ds_strips, axis=0), q2, TN)
            dv_acc[pl.ds(r0, K_SPAN), :] += _dot(jnp.concatenate(p_strips, axis=0), do2, TN)

        def two_blocks(i, carry):
            b = 2 * i
            ahead(b + 1, 1)
            finish(b, 0)
            ahead(jnp.minimum(b + 2, nb - 1), 0)
            finish(b + 1, 1)
            return carry

        ahead(0, 0)
        lax.fori_loop(0, nb // 2, two_blocks, 0)
        dqkv_ref[1] = dk_acc[Z_PAD:, :].astype(BF16)
        dqkv_ref[2] = dv_acc[Z_PAD:, :].astype(BF16)

        @pl.when(p_id == pairs - 1)
        def _():
            gather.finish()
            for cp in keep:
                cp.wait()
            for cp in _owner_copies(to_chip_refs, from_chip_refs, send_sems, recv_sems):
                cp.wait_recv()
                cp.wait_send()

    hbm = pl.BlockSpec(memory_space=pl.ANY)
    lands = ((N_DEV, SLAB_ROWS, D_MODEL), (N_DEV,) + ws_shape)
    return pl.pallas_call(
        body, name="attn_bwd",
        grid=(pairs,),
        in_specs=_attn_specs(s + Z_PAD) + [pl.BlockSpec((s, 128), lambda p: (0, p))] + [hbm] * n
        + [pl.BlockSpec(a.shape, functools.partial(lambda nd, p: (0,) * nd, a.ndim)) for a in small],
        out_specs=(pl.BlockSpec((3, s, 128), lambda p: (0, 0, p)),
                   pl.BlockSpec((2, Q_BLOCK, K_SPAN), lambda p: (p, 0, 0))) + (hbm,) * (n + 2),
        out_shape=(jax.ShapeDtypeStruct((3, s, D_A), BF16),
                   jax.ShapeDtypeStruct((N_HEADS, Q_BLOCK, K_SPAN), F32))
        + tuple(jax.ShapeDtypeStruct(t.shape, t.dtype) for t in to_chip)
        + tuple(jax.ShapeDtypeStruct(shape, F32) for shape in lands),
        scratch_shapes=[pltpu.VMEM((s + Z_PAD, 128), F32), pltpu.VMEM((s + Z_PAD, 128), F32),
                        pltpu.VMEM((2, 2 * Q_BLOCK, K_SPAN), F32), pltpu.VMEM((2, 2 * Q_BLOCK, K_SPAN), F32)]
        + [pltpu.VMEM(shape[1:], F32) for shape in lands]
        + [pltpu.SemaphoreType.DMA((n, 3)), pltpu.SemaphoreType.DMA((n, 3)),
           pltpu.SemaphoreType.DMA((2, N_DEV - 1)), pltpu.SemaphoreType.DMA((2, N_DEV - 1)),
           pltpu.SemaphoreType.DMA((2,))],
        compiler_params=_params(56),
    )(qkv, qkv, qkv, bias_table, d_out, *to_chip, *small)


def _mid_fwd_bwd(x, target, attn_out, z, w_pa, w_pb, w_out, b_gate, ln_g, ln_b, w_s, b_s, final_g):
    s = x.shape[0]
    tm = TOKEN_TILE
    nt = s // tm

    def body(x_ref, t_ref, oa_ref, ga_ref, ub_ref, vb_ref, gb_ref, ta0_ref, ta1_ref, tb0_ref, tb1_ref,
             wpa_hbm, wpb_hbm, wout_hbm, bg_ref, lng_ref, lnb_ref, ws_ref, bs_ref, fg_ref,
             dx2_ref, doa_ref, dz_ref, dwout_hbm, dwpa_hbm, dwpb_hbm, dbg_ref, dfg_ref, dlng_ref, dlnb_ref, dws_ref,
             dbs_ref, loss_ref,
             wpa, wpb, wout, wmix, acc_out, acc_pa, acc_pb, sem):
        i = pl.program_id(0)

        @pl.when(i == 0)
        def _():
            loads = [pltpu.make_async_copy(src, dst, sem.at[n])
                     for n, (src, dst) in enumerate(((wpa_hbm, wpa), (wpb_hbm, wpb), (wout_hbm, wout)))]
            for cp in loads:
                cp.start()
            t_idx = lax.broadcasted_iota(jnp.int32, (SGU_CHUNK, SGU_CHUNK), 0)
            s_idx = lax.broadcasted_iota(jnp.int32, (SGU_CHUNK, SGU_CHUNK), 1)
            for g in range(N_GROUPS):
                wmix[g] = jnp.where(s_idx <= t_idx, ws_ref[g], 0.0).astype(BF16)
            for ref in (acc_out, acc_pa, acc_pb, dbg_ref, dfg_ref, dlng_ref, dlnb_ref, dws_ref, dbs_ref, loss_ref):
                ref[...] = jnp.zeros(ref.shape, F32)
            for cp in loads:
                cp.wait()

        def tile_fwd_bwd(rows):
            g_a = ga_ref[rows, :].astype(F32)
            u_b = ub_ref[rows, :].astype(F32)
            v_b = vb_ref[rows, :].astype(F32)
            g_b = gb_ref[rows, :].astype(F32)
            bg = bg_ref[...]
            sg_a = _sigmoid(g_a)
            silu_a = g_a * sg_a
            o_a = oa_ref[rows, :]
            y_a = (o_a * silu_a).astype(BF16)
            ug, dgelu_u = _gelu_and_grad(u_b)
            vg, dgelu_v = _gelu_and_grad(v_b)
            mu = jnp.mean(vg, axis=-1, keepdims=True)
            vc = vg - mu
            rstd = lax.rsqrt(jnp.mean(vc * vc, axis=-1, keepdims=True) + EPS)
            vhat = vc * rstd
            lng = lng_ref[...]
            vn = (vhat * lng + lnb_ref[...]).astype(BF16)
            sg_b = _sigmoid(g_b)
            silu_b = g_b * sg_b
            subs = [slice(n * SGU_CHUNK, (n + 1) * SGU_CHUNK) for n in range(tm // SGU_CHUNK)]
            mixed = jnp.concatenate([jnp.concatenate(
                [_dot(wmix[g], vn[sub, g * 128:(g + 1) * 128]) + bs_ref[g] for g in range(N_GROUPS)], axis=1)
                for sub in subs], axis=0)
            um = ug * mixed
            y_b = (um * silu_b).astype(BF16)
            gate_a = _sigmoid(jnp.concatenate([ta0_ref[rows, :], ta1_ref[rows, :]], axis=1).astype(F32)
                              + bg[:, :D_MODEL])
            gate_b = _sigmoid(jnp.concatenate([tb0_ref[rows, :], tb1_ref[rows, :]], axis=1).astype(F32)
                              + bg[:, D_MODEL:])
            p_a = _dot(y_a, wpa[...])
            p_b = _dot(y_b, wpb[...])
            merged = (gate_a * p_a + gate_b * p_b).astype(BF16)
            x2 = x_ref[rows, :] + _dot(merged, wout[...])
            r2 = lax.rsqrt(jnp.mean(x2 * x2, axis=-1, keepdims=True) + EPS)
            xh = x2 * r2
            fg = fg_ref[...]
            err = xh * fg - t_ref[rows, :]
            loss_ref[...] += jnp.sum(jnp.sum(err * err, axis=-1, keepdims=True), axis=0, keepdims=True) * (0.5 / D_MODEL)
            dy = err * (1.0 / D_MODEL)
            dfg_ref[...] += jnp.sum(dy * xh, axis=0, keepdims=True)
            gy = dy * fg
            dx2 = r2 * (gy - xh * jnp.mean(gy * xh, axis=-1, keepdims=True))
            dx2_ref[rows, :] = dx2
            dx2b = dx2.astype(BF16)
            dmerged = _dot(dx2b, wout[...], NT)
            acc_out[...] += _dot(merged, dx2b, TN)
            dp_a = dmerged * gate_a
            dp_b = dmerged * gate_b
            dgate_a = dp_a * p_a * (1.0 - gate_a)
            dgate_b = dp_b * p_b * (1.0 - gate_b)
            dbg_ref[:, :D_MODEL] += jnp.sum(dgate_a, axis=0, keepdims=True)
            dbg_ref[:, D_MODEL:] += jnp.sum(dgate_b, axis=0, keepdims=True)
            dz_ref[rows, 2048:3072] = dgate_a.astype(BF16)
            dz_ref[rows, 3072:4096] = dgate_b.astype(BF16)
            dp_ab = dp_a.astype(BF16)
            dp_bb = dp_b.astype(BF16)
            dy_a = _dot(dp_ab, wpa[...], NT)
            dy_b = _dot(dp_bb, wpb[...], NT)
            acc_pa[...] += _dot(y_a, dp_ab, TN)
            acc_pb[...] += _dot(y_b, dp_bb, TN)
            doa_ref[rows, :] = (dy_a * silu_a).astype(BF16)
            dz_ref[rows, 0:512] = (dy_a * o_a * (sg_a * (1.0 + g_a * (1.0 - sg_a)))).astype(BF16)
            dz_ref[rows, 1536:2048] = (dy_b * um * (sg_b * (1.0 + g_b * (1.0 - sg_b)))).astype(BF16)
            dys = dy_b * silu_b
            dz_ref[rows, 512:1024] = (dys * mixed * dgelu_u).astype(BF16)
            dmixed = dys * ug
            dmb = dmixed.astype(BF16)
            dvn_rows = []
            for sub in subs:
                dvn_parts = []
                for g in range(N_GROUPS):
                    cols = slice(g * 128, (g + 1) * 128)
                    dws_ref[g] += _dot(dmb[sub, cols], vn[sub, cols], NT)
                    dbs_ref[g] += jnp.sum(dmixed[sub, cols], axis=-1, keepdims=True)
                    dvn_parts.append(_dot(wmix[g], dmb[sub, cols], TN))
                dvn_rows.append(jnp.concatenate(dvn_parts, axis=1))
            dvn = jnp.concatenate(dvn_rows, axis=0)
            dlng_ref[...] += jnp.sum(dvn * vhat, axis=0, keepdims=True)
            dlnb_ref[...] += jnp.sum(dvn, axis=0, keepdims=True)
            dvh = dvn * lng
            dvg = rstd * (dvh - jnp.mean(dvh, axis=-1, keepdims=True)
                          - vhat * jnp.mean(dvh * vhat, axis=-1, keepdims=True))
            dz_ref[rows, 1024:1536] = (dvg * dgelu_v).astype(BF16)

        tile_fwd_bwd(slice(0, tm))

        @pl.when(i == nt - 1)
        def _():
            t_idx = lax.broadcasted_iota(jnp.int32, (SGU_CHUNK, SGU_CHUNK), 0)
            s_idx = lax.broadcasted_iota(jnp.int32, (SGU_CHUNK, SGU_CHUNK), 1)
            for g in range(N_GROUPS):
                dws_ref[g] = jnp.where(s_idx <= t_idx, dws_ref[g], 0.0)
            stores = [pltpu.make_async_copy(src, dst, sem.at[n])
                      for n, (src, dst) in enumerate(((acc_out, dwout_hbm), (acc_pa, dwpa_hbm), (acc_pb, dwpb_hbm)))]
            for cp in stores:
                cp.start()
            for cp in stores:
                cp.wait()

    tile = lambda w: pl.BlockSpec((tm, w), lambda i: (i, 0))
    whole = lambda shape: pl.BlockSpec(shape, lambda i: (0,) * len(shape))
    hbm = pl.BlockSpec(memory_space=pl.ANY)
    return pl.pallas_call(
        body, name="mid_fwd_bwd",
        grid=(nt,),
        in_specs=[tile(D_MODEL), tile(D_MODEL), tile(D_A)]
        + [pl.BlockSpec((tm, COL_BLOCK), functools.partial(lambda c, i: (i + Z_PAD // tm, c), c))
           for c in range(3, N_COL_BLOCKS)]
        + [hbm, hbm, hbm,
                  whole((1, 2 * D_MODEL)), whole((1, D_B)), whole((1, D_B)),
                  whole((N_GROUPS, SGU_CHUNK, SGU_CHUNK)), whole((N_GROUPS, SGU_CHUNK, 1)), whole((1, D_MODEL))],
        out_specs=(tile(D_MODEL), tile(D_A), tile(REST), hbm, hbm, hbm,
                   whole((1, 2 * D_MODEL)), whole((1, D_MODEL)), whole((1, D_B)), whole((1, D_B)),
                   whole((N_GROUPS, SGU_CHUNK, SGU_CHUNK)), whole((N_GROUPS, SGU_CHUNK, 1)), whole((1, 1))),
        out_shape=(jax.ShapeDtypeStruct((s, D_MODEL), F32), jax.ShapeDtypeStruct((s, D_A), BF16),
                   jax.ShapeDtypeStruct((s, REST), BF16),
                   jax.ShapeDtypeStruct((D_MODEL, D_MODEL), F32), jax.ShapeDtypeStruct((D_A, D_MODEL), F32),
                   jax.ShapeDtypeStruct((D_B, D_MODEL), F32),
                   jax.ShapeDtypeStruct((1, 2 * D_MODEL), F32), jax.ShapeDtypeStruct((1, D_MODEL), F32),
                   jax.ShapeDtypeStruct((1, D_B), F32), jax.ShapeDtypeStruct((1, D_B), F32),
                   jax.ShapeDtypeStruct((N_GROUPS, SGU_CHUNK, SGU_CHUNK), F32),
                   jax.ShapeDtypeStruct((N_GROUPS, SGU_CHUNK, 1), F32), jax.ShapeDtypeStruct((1, 1), F32)),
        scratch_shapes=[pltpu.VMEM((D_A, D_MODEL), BF16), pltpu.VMEM((D_B, D_MODEL), BF16),
                        pltpu.VMEM((D_MODEL, D_MODEL), BF16), pltpu.VMEM((N_GROUPS, SGU_CHUNK, SGU_CHUNK), BF16),
                        pltpu.VMEM((D_MODEL, D_MODEL), F32), pltpu.VMEM((D_A, D_MODEL), F32),
                        pltpu.VMEM((D_B, D_MODEL), F32),
                        pltpu.SemaphoreType.DMA((3,))],
        compiler_params=_params(56),
    )(x, target, attn_out, *([z] * (N_COL_BLOCKS - 3)), w_pa, w_pb, w_out, b_gate, ln_g, ln_b, w_s, b_s, final_g)


def _proj_bwd_x(dqkv, drest, x, dx2, norm_g, w_in_t, to_chip, small):
    s = x.shape[0]
    tm = 512 if s % 512 == 0 else TOKEN_TILE
    nt = s // tm
    rows = to_chip.shape[1]
    half = D_MODEL // 2
    left, right = slice(0, half), slice(half, D_MODEL)

    def body(dqkv_ref, dr_ref, x_ref, dx2_ref, g_ref, w_hbm, tc_hbm, ng_ref, rel_ref,
             dx_ref, fc_ref, slab_land,
             w, tc_ref, slab_stage, via_x, via_y, mine, out_x, out_y, sem, send_sems, recv_sems,
             gather_send, gather_recv, keep_sems):
        i = pl.program_id(0)
        x_, y_, c_ = _my_pos()
        me = _flat_id((x_, y_, c_))
        xn, yn = (1 - x_, y_, c_), (x_, 1 - y_, c_)
        gather = _SlotGather([slab_land], gather_send, gather_recv, own=[slab_stage])
        keep = [pltpu.make_async_copy(slab_stage, slab_land.at[me], keep_sems.at[0])]

        def copy(k, src, dst, to):
            return pltpu.make_async_remote_copy(src_ref=src, dst_ref=dst, send_sem=send_sems.at[k],
                                                recv_sem=recv_sems.at[k], device_id=to, device_id_type=MESH)

        first = [copy(0, tc_ref.at[0, :, left], fc_ref.at[0, :, left], xn), copy(1, tc_ref.at[2, :, left], via_x, xn),
                 copy(2, tc_ref.at[1, :, right], fc_ref.at[1, :, right], yn), copy(3, tc_ref.at[2, :, right], via_y, yn)]
        second = [copy(4, out_y, fc_ref.at[1, :, left], yn), copy(5, out_x, fc_ref.at[0, :, right], xn)]

        def add_and_send(arrival, landed, own_half, stage, onward):
            load = pltpu.make_async_copy(own_half, mine, sem)
            load.start()
            arrival.wait_recv()
            load.wait()
            stage[...] = (mine[...].astype(F32) + landed[...].astype(F32)).astype(BF16)
            onward.start()

        @pl.when(i == 0)
        def _():
            cp = pltpu.make_async_copy(w_hbm, w, sem)
            cp.start()
            _fill_slab(slab_stage, ((ROW_NORM_G, ng_ref), (ROW_REL, rel_ref)))
            for cp_keep in keep:
                cp_keep.start()
            gather.start()
            stage_in = pltpu.make_async_copy(tc_hbm, tc_ref, keep_sems.at[2])
            stage_in.start()
            stage_in.wait()
            for rc in first:
                rc.start()
            cp.wait()

        @pl.when(i == (5 * nt) // 8)
        def _():
            gather.pass_on()
            add_and_send(first[1], via_x, tc_ref.at[1, :, left], out_y, second[0])
            add_and_send(first[3], via_y, tc_ref.at[0, :, right], out_x, second[1])

        dh = None
        for c in range(N_COL_BLOCKS):
            dz = dqkv_ref[c] if c < 3 else dr_ref[:, (c - 3) * COL_BLOCK:(c - 2) * COL_BLOCK]
            part = _dot(dz, w[c * COL_BLOCK:(c + 1) * COL_BLOCK, :])
            dh = part if dh is None else dh + part
        xf = x_ref[...]
        r = lax.rsqrt(jnp.mean(xf * xf, axis=-1, keepdims=True) + EPS)
        xn = xf * r
        gh = dh * g_ref[...]
        dx_ref[...] = r * (gh - xn * jnp.mean(gh * xn, axis=-1, keepdims=True)) + dx2_ref[...]

        @pl.when(i == nt - 1)
        def _():
            gather.finish()
            for cp_keep in keep:
                cp_keep.wait()
            for k in (0, 2, 4, 5):
                (first + second)[k].wait_recv()
            for rc in first + second:
                rc.wait_send()

    hbm = pl.BlockSpec(memory_space=pl.ANY)
    whole = lambda a: pl.BlockSpec(a.shape, lambda i: (0,) * a.ndim)
    return pl.pallas_call(
        body, name="proj_bwd_x",
        grid=(nt,),
        in_specs=[pl.BlockSpec((3, tm, D_A), lambda i: (0, i, 0)),
                  pl.BlockSpec((tm, REST), lambda i: (i, 0)),
                  pl.BlockSpec((tm, D_MODEL), lambda i: (i, 0)),
                  pl.BlockSpec((tm, D_MODEL), lambda i: (i, 0)),
                  pl.BlockSpec((1, D_MODEL), lambda i: (0, 0)),
                  hbm, hbm] + [whole(a) for a in small],
        out_specs=(pl.BlockSpec((tm, D_MODEL), lambda i: (i, 0)), hbm, hbm),
        out_shape=(jax.ShapeDtypeStruct((s, D_MODEL), F32), jax.ShapeDtypeStruct((2, rows, D_MODEL), BF16),
                   jax.ShapeDtypeStruct((N_DEV, LATE_SLAB_ROWS, D_MODEL), F32)),
        scratch_shapes=[pltpu.VMEM((D_IN, D_MODEL), BF16), pltpu.VMEM(to_chip.shape, BF16),
                        pltpu.VMEM((LATE_SLAB_ROWS, D_MODEL), F32)]
        + [pltpu.VMEM((rows, half), BF16)] * 5
        + [pltpu.SemaphoreType.DMA, pltpu.SemaphoreType.DMA((6,)), pltpu.SemaphoreType.DMA((6,)),
           pltpu.SemaphoreType.DMA((1, N_DEV - 1)), pltpu.SemaphoreType.DMA((1, N_DEV - 1)),
           pltpu.SemaphoreType.DMA((3,))],
        compiler_params=_params(56),
    )(dqkv, drest, x, dx2, norm_g, w_in_t, to_chip, *small)


def _proj_bwd_w(xn, dqkv, drest, norm_g, w_in_t):
    s = xn.shape[0]
    tk = s
    nk = s // tk

    def body(xn_ref, dqkv_ref, dr_ref, g_ref, w_ref, o_ref, dg_ref, acc):
        j = pl.program_id(0)
        i = pl.program_id(1)

        @pl.when((j == 0) & (i == 0))
        def _():
            dg_ref[...] = jnp.zeros(dg_ref.shape, F32)

        @pl.when(i == 0)
        def _():
            acc[...] = jnp.zeros(acc.shape, F32)

        @pl.when(j < 3)
        def _():
            acc[...] += _dot(dqkv_ref[...], xn_ref[...], TN)

        @pl.when(j >= 3)
        def _():
            acc[...] += _dot(dr_ref[...], xn_ref[...], TN)

        @pl.when(i == nk - 1)
        def _():
            m = acc[...]
            o_ref[...] = (m * g_ref[...]).astype(BF16)
            dg_ref[...] += jnp.sum(m * w_ref[...].astype(F32), axis=0, keepdims=True)

    return pl.pallas_call(
        body, name="proj_bwd_w",
        grid=(N_COL_BLOCKS, nk),
        in_specs=[pl.BlockSpec((tk, D_MODEL), lambda j, i: (i, 0)),
                  pl.BlockSpec((None, tk, COL_BLOCK),
                               lambda j, i: (jnp.minimum(j, 2), jnp.where(j < 3, i, nk - 1), 0)),
                  pl.BlockSpec((tk, COL_BLOCK),
                               lambda j, i: (jnp.where(j >= 3, i, 0), jnp.maximum(j - 3, 0))),
                  pl.BlockSpec((1, D_MODEL), lambda j, i: (0, 0)),
                  pl.BlockSpec((COL_BLOCK, D_MODEL), lambda j, i: (j, 0))],
        out_specs=(pl.BlockSpec((COL_BLOCK, D_MODEL), lambda j, i: (j, 0)),
                   pl.BlockSpec((1, D_MODEL), lambda j, i: (0, 0))),
        out_shape=(jax.ShapeDtypeStruct((D_IN, D_MODEL), BF16), jax.ShapeDtypeStruct((1, D_MODEL), F32)),
        scratch_shapes=[pltpu.VMEM((COL_BLOCK, D_MODEL), F32)],
        compiler_params=_params(56),
    )(xn, dqkv, drest, norm_g, w_in_t)


def _adamw_math(w, g, m, v):
    c1 = 1.0 - ADAM_B1 ** ADAM_STEP
    c2 = 1.0 - ADAM_B2 ** ADAM_STEP
    nm = ADAM_B1 * m + (1.0 - ADAM_B1) * g
    nv = ADAM_B2 * v + (1.0 - ADAM_B2) * (g * g)
    return -ADAM_LR * ((nm / c1) / (jnp.sqrt(nv / c2) + ADAM_EPS) + ADAM_WD * w), nm, nv


def _adamw(name, w, g, m, v, from_chip):
    rows, cols = w.shape
    tr = rows if rows * cols <= 512 * 1024 else next(t for t in range(256, 7, -8) if rows % t == 0)

    def body(w_ref, g_ref, m_ref, v_ref, t_ref, g_out, d_ref, nm_ref, nv_ref):
        gg = g_ref[...]
        for j in range(from_chip.shape[0]):
            gg = gg + t_ref[j].astype(F32)
        g_out[...] = gg
        d_ref[...], nm_ref[...], nv_ref[...] = _adamw_math(w_ref[...], gg, m_ref[...], v_ref[...])

    spec = pl.BlockSpec((tr, cols), lambda i: (i, 0))
    shape = jax.ShapeDtypeStruct((rows, cols), F32)
    return pl.pallas_call(
        body, name=name,
        grid=(rows // tr,),
        in_specs=[spec] * 4 + [pl.BlockSpec((from_chip.shape[0], tr, cols), lambda i: (0, i, 0))],
        out_specs=(spec,) * 4, out_shape=(shape,) * 4,
        compiler_params=_params(32),
    )(w, g, m, v, from_chip)


_SMALL = (("norm_g", (1, D_MODEL)), ("b_gate", (1, 2 * D_MODEL)), ("rel_bias", (N_HEADS, N_REL)),
          ("sgu_ln_g", (1, D_B)), ("sgu_ln_b", (1, D_B)), ("w_s", (N_GROUPS * SGU_CHUNK, SGU_CHUNK)),
          ("b_s", (N_GROUPS, SGU_CHUNK)), ("final_g", (1, D_MODEL)))


def _adamw_small(slabs, ws_all, late_slabs, weights, moments_m, moments_v):
    k = len(_SMALL)

    def total(ref):
        acc = ref[0]
        for d in range(1, N_DEV):
            acc = acc + ref[d]
        return acc

    def body(*refs):
        slab_ref, ws_ref, late_ref = refs[:3]
        w_refs, m_refs, v_refs = refs[3:3 + k], refs[3 + k:3 + 2 * k], refs[3 + 2 * k:3 + 3 * k]
        outs = refs[3 + 3 * k:]
        slab, late = total(slab_ref), total(late_ref)
        grads = {
            "norm_g": late[ROW_NORM_G:ROW_NORM_G + 1, :],
            "b_gate": jnp.concatenate([slab[ROW_B_GATE:ROW_B_GATE + 1, :], slab[ROW_B_GATE + 1:ROW_B_GATE + 2, :]], axis=1),
            "rel_bias": late[ROW_REL:ROW_REL + N_HEADS, :N_REL],
            "sgu_ln_g": slab[ROW_LN_G:ROW_LN_G + 1, :D_B],
            "sgu_ln_b": slab[ROW_LN_B:ROW_LN_B + 1, :D_B],
            "w_s": total(ws_ref),
            "b_s": slab[ROW_B_S:ROW_B_S + N_GROUPS, :SGU_CHUNK],
            "final_g": slab[ROW_FINAL_G:ROW_FINAL_G + 1, :],
        }
        for n, (name, _) in enumerate(_SMALL):
            g = grads[name]
            outs[n][...] = g
            outs[k + n][...], outs[2 * k + n][...], outs[3 * k + n][...] = _adamw_math(
                w_refs[n][...], g, m_refs[n][...], v_refs[n][...])
        outs[4 * k][...] = slab[ROW_LOSS:ROW_LOSS + 1, :1]

    vmem = pl.BlockSpec(memory_space=pltpu.VMEM)
    shapes = tuple(jax.ShapeDtypeStruct(shape, F32) for _, shape in _SMALL)
    return pl.pallas_call(
        body, name="adamw_small",
        out_shape=shapes * 4 + (jax.ShapeDtypeStruct((1, 1), F32),),
        in_specs=[vmem] * (3 + 3 * k), out_specs=tuple([vmem] * (4 * k + 1)),
        compiler_params=_params(16),
    )(slabs, ws_all, late_slabs, *weights, *moments_m, *moments_v)


def _pad_rel(a):
    return jnp.pad(a.reshape(N_HEADS, N_REL), ((0, 0), (0, N_REL_PAD - N_REL)))


def kernel(x, norm_g, w_in, b_gate, rel_bias, sgu_ln_g, sgu_ln_b, w_s, b_s, w_pa, w_pb, w_out, final_g, loss_target, m_norm_g, m_w_in, m_b_gate, m_rel_bias, m_sgu_ln_g, m_sgu_ln_b, m_w_s, m_b_s, m_w_pa, m_w_pb, m_w_out, m_final_g, v_norm_g, v_w_in, v_b_gate, v_rel_bias, v_sgu_ln_g, v_sgu_ln_b, v_w_s, v_b_s, v_w_pa, v_w_pb, v_w_out, v_final_g):
    s = x.shape[1]
    xs = x.reshape(s, D_MODEL)
    tgt = loss_target.reshape(s, D_MODEL)

    bias_table = _bias_table(_pad_rel(rel_bias))
    w_in_t = jnp.swapaxes(w_in[0], 0, 1)
    qkv, x_norm, w_in_t_full = _gather_proj_fwd(xs, norm_g, w_in_t)
    attn_out, g_pa, g_pb, g_out = _attn_fwd(qkv, bias_table, (w_pa[0], w_pb[0], w_out[0]))
    w_pa_full = jnp.transpose(g_pa, (1, 0, 2)).reshape(D_A, D_MODEL)
    w_pb_full = jnp.transpose(g_pb, (1, 0, 2)).reshape(D_B, D_MODEL)
    w_out_full = g_out.reshape(D_MODEL, D_MODEL)

    (dx2, d_attn, drest, dw_out, dw_pa, dw_pb, d_bgate, d_fg, d_lng, d_lnb, d_ws, d_bs, loss_part) = _mid_fwd_bwd(
        xs, tgt, attn_out, qkv, w_pa_full, w_pb_full, w_out_full, b_gate, sgu_ln_g, sgu_ln_b, w_s[0],
        b_s.reshape(N_GROUPS, SGU_CHUNK, 1), final_g.reshape(1, D_MODEL))

    own_pa, own_pb, own_out, tc_pa, tc_pb, tc_out = _reduce_chip(
        "reduce_chip_proj", (dw_pa, dw_pb, dw_out), (1, 1, 0))
    dqkv, dbias, fc_pa, fc_pb, fc_out, slabs, ws_all = _attn_bwd(
        qkv, bias_table, d_attn, (tc_pa, tc_pb, tc_out),
        (d_bgate, d_lng, d_lnb, d_fg, loss_part, d_bs, d_ws.reshape(N_GROUPS * SGU_CHUNK, SGU_CHUNK)))
    d_rel = _bias_grad(dbias)
    dw_in_t, d_ng = _proj_bwd_w(x_norm, dqkv, drest, norm_g, w_in_t_full)
    own_in, tc_in = _reduce_chip("reduce_chip_in", (dw_in_t,), (0,))
    grad_x, fc_in, late_slabs = _proj_bwd_x(dqkv, drest, xs, dx2, norm_g, w_in_t_full, tc_in, (d_ng, d_rel))
    big = {"w_in": tuple(jnp.swapaxes(t, 0, 1)[None] for t in _adamw(
        "adamw_w_in", w_in_t, own_in, jnp.swapaxes(m_w_in[0], 0, 1), jnp.swapaxes(v_w_in[0], 0, 1), fc_in))}
    for name, w, g, fc, m, v in (("w_pa", w_pa, own_pa, fc_pa, m_w_pa, v_w_pa),
                                 ("w_pb", w_pb, own_pb, fc_pb, m_w_pb, v_w_pb),
                                 ("w_out", w_out, own_out, fc_out, m_w_out, v_w_out)):
        big[name] = tuple(t[None] for t in _adamw("adamw_" + name, w[0], g, m[0], v[0], fc))

    as_2d = lambda leaves: [a.reshape(shape) for a, (_, shape) in zip(leaves, _SMALL)]
    small_out = _adamw_small(
        slabs, ws_all, late_slabs, as_2d((norm_g, b_gate, rel_bias, sgu_ln_g, sgu_ln_b, w_s, b_s, final_g)),
        as_2d((m_norm_g, m_b_gate, m_rel_bias, m_sgu_ln_g, m_sgu_ln_b, m_w_s, m_b_s, m_final_g)),
        as_2d((v_norm_g, v_b_gate, v_rel_bias, v_sgu_ln_g, v_sgu_ln_b, v_w_s, v_b_s, v_final_g)))
    small_index = {name: n for n, (name, _) in enumerate(_SMALL)}

    def leaf(kind, name, like):
        if name in big:
            return big[name][kind]
        return small_out[kind * len(_SMALL) + small_index[name]].reshape(like.shape)

    weights = (("norm_g", norm_g), ("w_in", w_in), ("b_gate", b_gate), ("rel_bias", rel_bias), ("sgu_ln_g", sgu_ln_g),
               ("sgu_ln_b", sgu_ln_b), ("w_s", w_s), ("b_s", b_s), ("w_pa", w_pa), ("w_pb", w_pb), ("w_out", w_out),
               ("final_g", final_g))
    outs = [small_out[-1].reshape(()), grad_x.reshape(x.shape)]
    for kind in range(4):
        outs.extend(leaf(kind, name, like) for name, like in weights)
    return tuple(outs)
```

```python
import functools
import math

import jax
import jax.numpy as jnp
from jax import lax
from jax.experimental import pallas as pl
from jax.experimental.pallas import tpu as pltpu

F32 = jnp.float32
BF16 = jnp.bfloat16
MESH = pl.DeviceIdType.MESH
N_DEV = 8

D_MODEL = 1024
D_A = 512
D_B = 512
D_IN = 5632
N_HEADS = 8
HEAD_DIM = 64
N_PREV = 8
REL_CLIP = 128
N_REL = 2 * REL_CLIP + 1
N_REL_PAD = 384
SGU_CHUNK = 128
N_GROUPS = 4
EPS = 1e-6
NEG_INF = -1e30
Q_SCALE = HEAD_DIM ** -0.5

Q_BLOCK = 256
K_SPAN = 768
Z_PAD = K_SPAN - Q_BLOCK
ROLL_W = 1024
COL_BLOCK = 512
N_COL_BLOCKS = D_IN // COL_BLOCK
REST = D_IN - 3 * D_A
TOKEN_TILE = 256

ADAM_LR = 0.001
ADAM_B1 = 0.9
ADAM_B2 = 0.999
ADAM_EPS = 1e-08
ADAM_WD = 0.01
ADAM_STEP = 10

GELU_C = math.sqrt(2.0 / math.pi)
GELU_A = 0.044715

NT = (((1,), (1,)), ((), ()))
TN = (((0,), (0,)), ((), ()))
HIGHEST = lax.Precision.HIGHEST


def _params(vmem_mb, **kw):
    return pltpu.CompilerParams(vmem_limit_bytes=vmem_mb * 1024 * 1024, **kw)


def _dot(a, b, dims=None):
    if dims is None:
        return jnp.dot(a, b, preferred_element_type=F32)
    return lax.dot_general(a, b, dims, preferred_element_type=F32)


def _sigmoid(x):
    return 0.5 * jnp.tanh(0.5 * x) + 0.5


def _gelu_and_grad(u):
    u2 = u * u
    t = jnp.tanh(GELU_C * (u + GELU_A * u * u2))
    half = 0.5 * (1.0 + t)
    g = u * half
    dg = half + 0.5 * u * (1.0 - t * t) * (GELU_C * (1.0 + 3.0 * GELU_A * u2))
    return g, dg


def _my_pos():
    return lax.axis_index("x"), lax.axis_index("y"), lax.axis_index("c")


def _flat_id(pos):
    return 4 * pos[0] + 2 * pos[1] + pos[2]


def _other_chips(pos):
    x, y, _ = pos
    return ((1 - x, y), (x, 1 - y), (1 - x, 1 - y))


class _SlotGather:
    def __init__(self, bufs, send_sems, recv_sems, own=None):
        self.bufs, self.send_sems, self.recv_sems = bufs, send_sems, recv_sems
        self.own = own if own is not None else [None] * len(bufs)
        x, y, c = _my_pos()
        self.c, self.me, self.sib = c, (x, y, c), (x, y, 1 - c)
        self.chips = _other_chips(self.me)

    def _copy(self, a, k, block, to):
        slot = _flat_id(block)
        src = self.own[a] if (k < 4 and self.own[a] is not None) else self.bufs[a].at[slot]
        return pltpu.make_async_remote_copy(
            src_ref=src, dst_ref=self.bufs[a].at[slot],
            send_sem=self.send_sems.at[a, k], recv_sem=self.recv_sems.at[a, k], device_id=to, device_id_type=MESH)

    def _own_sends(self):
        n = len(self.bufs)
        return ([self._copy(a, 1 + j, self.me, (*chip, self.c)) for j, chip in enumerate(self.chips) for a in range(n)]
                + [self._copy(a, 0, self.me, self.sib) for a in range(n)])

    def _passes(self):
        return [self._copy(a, 4 + j, (*chip, self.c), self.sib)
                for j, chip in enumerate(self.chips) for a in range(len(self.bufs))]

    def start(self):
        for cp in self._own_sends():
            cp.start()

    def pass_on(self):
        for j, chip in enumerate(self.chips):
            for a in range(len(self.bufs)):
                self._copy(a, 1 + j, (*chip, self.c), self.me).wait_recv()
                self._copy(a, 4 + j, (*chip, self.c), self.sib).start()

    def finish(self):
        for a in range(len(self.bufs)):
            self._copy(a, 0, self.sib, self.me).wait_recv()
            for j, chip in enumerate(self.chips):
                self._copy(a, 4 + j, (*chip, 1 - self.c), self.me).wait_recv()
        for cp in self._own_sends() + self._passes():
            cp.wait_send()


def _reduce_chip(name, parts, sharded_dim):
    n = len(parts)
    shapes = []
    for p, dim in zip(parts, sharded_dim):
        shape = list(p.shape)
        shape[dim] //= N_DEV
        shapes.append(tuple(shape))
    staged = [not (dim == 0 and p.dtype == BF16) for p, dim in zip(parts, sharded_dim)]

    def body(*refs):
        full, own, to_chip = refs[:n], refs[n:2 * n], refs[2 * n:3 * n]
        ins, from_sib = refs[3 * n:4 * n], refs[4 * n:5 * n]
        send_sems, recv_sems = refs[5 * n], refs[5 * n + 1]
        x, y, c = _my_pos()
        sib = (x, y, 1 - c)
        chips = ((x, y),) + _other_chips((x, y, c))
        for a in range(n):
            rows, cols = shapes[a]
            for d in range(N_DEV if staged[a] else 0):
                if sharded_dim[a] == 0:
                    ins[a][d] = full[a][d * rows:(d + 1) * rows, :].astype(BF16)
                else:
                    ins[a][d] = full[a][:, d * cols:(d + 1) * cols].astype(BF16)

        def block(a, d):
            if staged[a]:
                return ins[a].at[d]
            rows = shapes[a][0]
            return full[a].at[pl.ds(pl.multiple_of(d * rows, 16), rows), :]

        def to_sibling(a, r):
            return pltpu.make_async_remote_copy(
                src_ref=block(a, _flat_id((*chips[r], 1 - c))), dst_ref=from_sib[a].at[r],
                send_sem=send_sems.at[a, r], recv_sem=recv_sems.at[a, r], device_id=sib, device_id_type=MESH)

        sends = [to_sibling(a, r) for r in (1, 2, 3, 0) for a in range(n)]
        for cp in sends:
            cp.start()
        for r in (1, 2, 3, 0):
            for a in range(n):
                to_sibling(a, r).wait_recv()
                both = block(a, _flat_id((*chips[r], c)))[...].astype(F32) + from_sib[a][r].astype(F32)
                if r == 0:
                    own[a][...] = both
                else:
                    to_chip[a][r - 1] = both.astype(BF16)
        for cp in sends:
            cp.wait_send()

    vmem = pl.BlockSpec(memory_space=pltpu.VMEM)
    return pl.pallas_call(
        body, name=name,
        out_shape=tuple(jax.ShapeDtypeStruct(sh, F32) for sh in shapes)
        + tuple(jax.ShapeDtypeStruct((3,) + sh, BF16) for sh in shapes),
        in_specs=[vmem] * n, out_specs=tuple([vmem] * (2 * n)),
        scratch_shapes=[pltpu.VMEM((N_DEV if st else 1,) + sh, BF16) for sh, st in zip(shapes, staged)]
        + [pltpu.VMEM((4,) + sh, BF16) for sh in shapes]
        + [pltpu.SemaphoreType.DMA((n, 4)), pltpu.SemaphoreType.DMA((n, 4))],
        compiler_params=_params(56),
    )(*parts)


def _owner_copies(to_chip, from_chip, send_sems, recv_sems):
    x, y, c = _my_pos()
    return [pltpu.make_async_remote_copy(
        src_ref=to_chip[a].at[j], dst_ref=from_chip[a].at[j],
        send_sem=send_sems.at[a, j], recv_sem=recv_sems.at[a, j], device_id=(*chip, c), device_id_type=MESH)
        for a in range(len(to_chip)) for j, chip in enumerate(_other_chips((x, y, c)))]


ROW_B_GATE, ROW_LN_G, ROW_LN_B, ROW_FINAL_G, ROW_LOSS, ROW_B_S, SLAB_ROWS = 1, 3, 4, 5, 6, 16, 24
ROW_NORM_G, ROW_REL, LATE_SLAB_ROWS = 0, 8, 16


def _rel_index(e):
    lo, hi = Z_PAD - REL_CLIP, Z_PAD + REL_CLIP
    return jnp.where(e <= lo, 2 * REL_CLIP, jnp.where(e < hi, hi - e, jnp.where(e <= K_SPAN, 0, 2 * REL_CLIP)))


def _bias_table(rel_bias_pad):
    def body(rb_ref, bt_ref):
        c = lax.broadcasted_iota(jnp.int32, (N_REL_PAD, ROLL_W), 1)
        r = lax.broadcasted_iota(jnp.int32, (N_REL_PAD, ROLL_W), 0)
        pick = (r == _rel_index(c)).astype(F32)
        rows = jnp.dot(rb_ref[...], pick, precision=HIGHEST, preferred_element_type=F32)
        qc = lax.broadcasted_iota(jnp.int32, (Q_BLOCK, K_SPAN), 0) >> 6
        kc = lax.broadcasted_iota(jnp.int32, (Q_BLOCK, K_SPAN), 1) >> 6
        band = (kc >= qc) & (kc <= qc + N_PREV)
        for h in range(N_HEADS):
            t = jnp.broadcast_to(rows[h:h + 1, :], (Q_BLOCK, ROLL_W))
            t = pltpu.roll(t, 0, 1, stride=1, stride_axis=0)
            bt_ref[h] = jnp.where(band, t[:, :K_SPAN], NEG_INF)

    return pl.pallas_call(
        body, name="bias_table",
        out_shape=jax.ShapeDtypeStruct((N_HEADS, Q_BLOCK, K_SPAN), F32),
        compiler_params=_params(32),
    )(rel_bias_pad)


def _bias_grad(dbias):
    def body(a_ref, o_ref):
        rr = lax.broadcasted_iota(jnp.int32, (Q_BLOCK, Q_BLOCK), 0)
        cc = lax.broadcasted_iota(jnp.int32, (Q_BLOCK, Q_BLOCK), 1)
        flip = (rr + cc == Q_BLOCK - 1).astype(F32)
        c = lax.broadcasted_iota(jnp.int32, (ROLL_W, N_REL_PAD), 0)
        r = lax.broadcasted_iota(jnp.int32, (ROLL_W, N_REL_PAD), 1)
        e = jnp.where(c >= Q_BLOCK - 1, c - (Q_BLOCK - 1), c + (ROLL_W - Q_BLOCK + 1))
        pick = (r == _rel_index(e)).astype(F32)
        sums = []
        for h in range(N_HEADS):
            a = jnp.dot(flip, a_ref[h], precision=HIGHEST, preferred_element_type=F32)
            a = jnp.concatenate([a, jnp.zeros((Q_BLOCK, ROLL_W - K_SPAN), F32)], axis=1)
            a = pltpu.roll(a, 0, 1, stride=1, stride_axis=0)
            sums.append(jnp.sum(a, axis=0, keepdims=True))
        diag = jnp.concatenate(sums, axis=0)
        o_ref[...] = jnp.dot(diag, pick, precision=HIGHEST, preferred_element_type=F32)

    return pl.pallas_call(
        body, name="bias_grad",
        out_shape=jax.ShapeDtypeStruct((N_HEADS, N_REL_PAD), F32),
        compiler_params=_params(32),
    )(dbias)


def _gather_proj_fwd(x, norm_g, w_in_t):
    s = x.shape[0]
    tm = 512 if s % 512 == 0 else TOKEN_TILE
    nt = s // tm
    n_pad = Z_PAD // tm
    shard_w = w_in_t.shape[0]
    chip_w = 2 * shard_w
    n_chips = N_DEV // 2

    def body(order_ref, x_ref, g_ref, win_hbm, z_ref, xn_ref, wt_hbm, wt, hb, win_f32, send_sems, recv_sems,
             local_sems):
        j = pl.program_id(0)
        i = pl.program_id(1)
        x_, y_, c_ = _my_pos()
        me, sib = (x_, y_, c_), (x_, y_, 1 - c_)
        near = _other_chips(me)
        pick = lambda a, b: tuple(jnp.where(c_ == 0, u, v) for u, v in zip(a, b))
        passed_from, passed_to = pick(near[0], near[1]), pick(near[1], near[0])

        def rows_of(block):
            return wt.at[pl.ds(pl.multiple_of(_flat_id(block) * shard_w, 16), shard_w), :]

        def copy(k, block, to):
            return pltpu.make_async_remote_copy(
                src_ref=rows_of(block), dst_ref=rows_of(block),
                send_sem=send_sems.at[k], recv_sem=recv_sems.at[k], device_id=to, device_id_type=MESH)

        def sends():
            return ([copy(0, me, sib), copy(1, me, (*near[0], c_)), copy(2, me, (*near[1], c_)),
                     copy(3, (*passed_from, c_), (*passed_to, c_))]
                    + [copy(4 + n, (*near[n], c_), sib) for n in range(3)])

        keep = pltpu.make_async_copy(wt, wt_hbm, local_sems.at[0])

        @pl.when((j == 0) & (i == 0))
        def _():
            load = pltpu.make_async_copy(win_hbm, win_f32, local_sems.at[1])
            load.start()
            load.wait()
            rows_of(me)[...] = win_f32[...].astype(BF16)
            for cp in sends()[:3]:
                cp.start()
            copy(0, sib, me).wait_recv()

        @pl.when((j == 1) & (i == 0))
        def _():
            copy(1, (*near[0], c_), me).wait_recv()
            copy(2, (*near[1], c_), me).wait_recv()
            for cp in sends()[3:6]:
                cp.start()
            copy(4, (*near[0], 1 - c_), me).wait_recv()

        @pl.when((j == 2) & (i == 0))
        def _():
            copy(5, (*near[1], 1 - c_), me).wait_recv()

        @pl.when((j == 3) & (i == 0))
        def _():
            copy(3, (*near[2], c_), me).wait_recv()
            copy(6, (*near[2], c_), sib).start()
            copy(6, (*near[2], 1 - c_), me).wait_recv()
            keep.start()

        @pl.when(i < n_pad)
        def _():
            z_ref[...] = jnp.zeros(z_ref.shape, BF16)

        @pl.when(i >= n_pad)
        def _():
            rows = pl.ds(pl.multiple_of((i - n_pad) * tm, tm), tm)

            @pl.when(j == 0)
            def _():
                xf = x_ref[...]
                xn = xf * lax.rsqrt(jnp.mean(xf * xf, axis=-1, keepdims=True) + EPS)
                hb[rows, :] = (xn * g_ref[...]).astype(BF16)
                xn_ref[...] = xn.astype(BF16)

            chip_rows = pl.ds(pl.multiple_of(order_ref[j] * chip_w, 16), chip_w)
            blk = _dot(hb[rows, :], wt[chip_rows, :], NT)
            q_scale = jnp.where(order_ref[j] == 0, Q_SCALE, 1.0).astype(F32)
            z_ref[:, :D_A] = (blk[:, :D_A] * q_scale).astype(BF16)
            z_ref[:, D_A:] = blk[:, D_A:].astype(BF16)

        @pl.when((j == n_chips - 1) & (i == n_pad + nt - 1))
        def _():
            keep.wait()
            for cp in sends():
                cp.wait_send()

    pos = _my_pos()
    order = jnp.stack([2 * cx + cy for cx, cy in ((pos[0], pos[1]),) + _other_chips(pos)]).astype(jnp.int32)
    first_pass = lambda j, i: jnp.where(j == 0, jnp.maximum(i - n_pad, 0), nt - 1)
    grid_spec = pltpu.PrefetchScalarGridSpec(
        num_scalar_prefetch=1,
        grid=(n_chips, n_pad + nt),
        in_specs=[pl.BlockSpec((tm, D_MODEL), lambda j, i, o: (first_pass(j, i), 0)),
                  pl.BlockSpec((1, D_MODEL), lambda j, i, o: (0, 0)),
                  pl.BlockSpec(memory_space=pl.ANY)],
        out_specs=(pl.BlockSpec((tm, chip_w), lambda j, i, o: (i, o[j])),
                   pl.BlockSpec((tm, D_MODEL), lambda j, i, o: (first_pass(j, i), 0)),
                   pl.BlockSpec(memory_space=pl.ANY)),
        scratch_shapes=[pltpu.VMEM((D_IN, D_MODEL), BF16),
                        pltpu.VMEM((s, D_MODEL), BF16), pltpu.VMEM(w_in_t.shape, F32),
                        pltpu.SemaphoreType.DMA((N_DEV - 1,)), pltpu.SemaphoreType.DMA((N_DEV - 1,)),
                        pltpu.SemaphoreType.DMA((2,))])
    return pl.pallas_call(
        body, name="gather_proj_fwd",
        grid_spec=grid_spec,
        out_shape=(jax.ShapeDtypeStruct((Z_PAD + s, D_IN), BF16), jax.ShapeDtypeStruct((s, D_MODEL), BF16),
                   jax.ShapeDtypeStruct((D_IN, D_MODEL), BF16)),
        compiler_params=_params(60),
    )(order, x, norm_g, w_in_t)


def _attn_specs(rows):
    pairs = N_HEADS // 2
    return ([pl.BlockSpec((rows, 128), functools.partial(lambda which, p: (0, which * pairs + p), which))
             for which in range(3)]
            + [pl.BlockSpec((2, Q_BLOCK, K_SPAN), lambda p: (p, 0, 0))])


def _head_masks():
    lane = lax.broadcasted_iota(jnp.int32, (1, 128), 1)
    first = lane < HEAD_DIM
    return (first, jnp.logical_not(first))


def _stack_heads(x, masks):
    zero = jnp.zeros((), x.dtype)
    return jnp.concatenate([jnp.where(m, x, zero) for m in masks], axis=0)


STRIP = 16


def _softmax_strips(s_ref, bias_ref, b):
    valid = lax.broadcasted_iota(jnp.int32, (1, K_SPAN), 1) >= Z_PAD - b * Q_BLOCK
    for t in range(2 * Q_BLOCK // STRIP):
        hh, r = divmod(t * STRIP, Q_BLOCK)
        st = s_ref[t * STRIP:(t + 1) * STRIP, :] + bias_ref[hh, r:r + STRIP, :]
        st = jnp.where(valid, st, NEG_INF)
        e = jnp.exp(st - jnp.max(st, axis=-1, keepdims=True))
        yield e * (1.0 / jnp.sum(e, axis=-1, keepdims=True))


def _side_by_side_strips(strips):
    half = len(strips) // 2
    return jnp.concatenate([jnp.concatenate([a, c], axis=1) for a, c in zip(strips[:half], strips[half:])], axis=0)


def _attn_fwd(qkv, bias_table, shards):
    s = qkv.shape[0] - Z_PAD
    nb = s // Q_BLOCK
    n = len(shards)
    pairs = N_HEADS // 2

    def body(*refs):
        q_ref, k_ref, v_ref, bt_ref = refs[:4]
        shard_refs = refs[4:4 + n]
        o_ref = refs[4 + n]
        slot_refs = refs[5 + n:5 + 2 * n]
        stages = refs[5 + 2 * n:5 + 3 * n]
        s_scr, send_sems, recv_sems, local_sems = refs[5 + 3 * n:]
        p_id = pl.program_id(0)
        gather = _SlotGather(slot_refs, send_sems, recv_sems, own=stages)
        keep = [pltpu.make_async_copy(stages[a], slot_refs[a].at[_flat_id(_my_pos())], local_sems.at[a])
                for a in range(n)]

        @pl.when(p_id == 0)
        def _():
            for a in range(n):
                stages[a][...] = shard_refs[a][...].astype(BF16)
                keep[a].start()
            gather.start()

        @pl.when(p_id == 2)
        def _():
            gather.pass_on()

        masks = _head_masks()

        def scores(b, half):
            r0 = pl.multiple_of(b * Q_BLOCK, Q_BLOCK)
            q2 = _stack_heads(q_ref[pl.ds(r0 + Z_PAD, Q_BLOCK), :], masks)
            s_scr[half] = _dot(q2, k_ref[pl.ds(r0, K_SPAN), :], NT)

        def finish(b, half):
            r0 = pl.multiple_of(b * Q_BLOCK, Q_BLOCK)
            v2 = _stack_heads(v_ref[pl.ds(r0, K_SPAN), :], masks)
            p = [st.astype(BF16) for st in _softmax_strips(s_scr.at[half], bt_ref, b)]
            o_ref[pl.ds(r0, Q_BLOCK), :] = _dot(_side_by_side_strips(p), v2)

        def two_blocks(i, carry):
            b = 2 * i
            scores(b + 1, 1)
            finish(b, 0)
            scores(jnp.minimum(b + 2, nb - 1), 0)
            finish(b + 1, 1)
            return carry

        scores(0, 0)
        lax.fori_loop(0, nb // 2, two_blocks, 0)

        @pl.when(p_id == pairs - 1)
        def _():
            gather.finish()
            for cp in keep:
                cp.wait()

    hbm = pl.BlockSpec(memory_space=pl.ANY)
    return pl.pallas_call(
        body, name="attn_fwd",
        grid=(pairs,),
        in_specs=_attn_specs(s + Z_PAD) + [pl.BlockSpec(a.shape, lambda p: (0, 0)) for a in shards],
        out_specs=(pl.BlockSpec((s, 128), lambda p: (0, p)),) + (hbm,) * n,
        out_shape=(jax.ShapeDtypeStruct((s, D_A), F32),)
        + tuple(jax.ShapeDtypeStruct((N_DEV,) + a.shape, BF16) for a in shards),
        scratch_shapes=[pltpu.VMEM(a.shape, BF16) for a in shards]
        + [pltpu.VMEM((2, 2 * Q_BLOCK, K_SPAN), F32),
           pltpu.SemaphoreType.DMA((n, N_DEV - 1)), pltpu.SemaphoreType.DMA((n, N_DEV - 1)),
           pltpu.SemaphoreType.DMA((n,))],
        compiler_params=_params(48),
    )(qkv, qkv, qkv, bias_table, *shards)


def _fill_slab(stage, rows):
    stage[...] = jnp.zeros(stage.shape, F32)
    for row, ref in rows:
        r, c = ref.shape
        if c > D_MODEL:
            for part in range(c // D_MODEL):
                stage[row + part:row + part + 1, :] = ref[:, part * D_MODEL:(part + 1) * D_MODEL]
        else:
            stage[row:row + r, :c] = ref[...]


def _attn_bwd(qkv, bias_table, d_out, to_chip, small):
    s = qkv.shape[0] - Z_PAD
    nb = s // Q_BLOCK
    n = len(to_chip)
    pairs = N_HEADS // 2
    ws_shape = small[-1].shape

    def body(*refs):
        q_ref, k_ref, v_ref, bt_ref, do_ref = refs[:5]
        to_chip_refs = refs[5:5 + n]
        bg_ref, lng_ref, lnb_ref, fg_ref, loss_ref, bs_ref, ws_ref = refs[5 + n:12 + n]
        dqkv_ref, db_ref = refs[12 + n:14 + n]
        from_chip_refs = refs[14 + n:14 + 2 * n]
        slab_land, ws_land = refs[14 + 2 * n:16 + 2 * n]
        (dk_acc, dv_acc, s_scr, dp_scr, slab_stage, ws_stage, send_sems, recv_sems, gather_send, gather_recv,
         keep_sems) = refs[16 + 2 * n:]
        p_id = pl.program_id(0)
        me = _flat_id(_my_pos())
        gather = _SlotGather([slab_land, ws_land], gather_send, gather_recv, own=[slab_stage, ws_stage])
        keep = [pltpu.make_async_copy(stage, land.at[me], keep_sems.at[k]) for k, (stage, land) in enumerate(
            ((slab_stage, slab_land), (ws_stage, ws_land)))]

        @pl.when(p_id == 0)
        def _():
            _fill_slab(slab_stage, ((ROW_B_GATE, bg_ref), (ROW_LN_G, lng_ref), (ROW_LN_B, lnb_ref),
                                    (ROW_FINAL_G, fg_ref), (ROW_LOSS, loss_ref)))
            eye = (lax.broadcasted_iota(jnp.int32, (SGU_CHUNK, SGU_CHUNK), 0)
                   == lax.broadcasted_iota(jnp.int32, (SGU_CHUNK, SGU_CHUNK), 1))
            for g in range(N_GROUPS):
                row = jnp.sum(jnp.where(eye, bs_ref[g], 0.0), axis=0, keepdims=True)
                slab_stage[ROW_B_S + g:ROW_B_S + g + 1, :SGU_CHUNK] = row
            ws_stage[...] = ws_ref[...]
            for cp in keep:
                cp.start()
            gather.start()
            for cp in _owner_copies(to_chip_refs, from_chip_refs, send_sems, recv_sems):
                cp.start()

        @pl.when(p_id == 2)
        def _():
            gather.pass_on()

        dk_acc[...] = jnp.zeros(dk_acc.shape, F32)
        dv_acc[...] = jnp.zeros(dv_acc.shape, F32)
        db_ref[...] = jnp.zeros(db_ref.shape, F32)
        masks = _head_masks()

        def operands(b):
            r0 = pl.multiple_of(b * Q_BLOCK, Q_BLOCK)
            q2 = _stack_heads(q_ref[pl.ds(r0 + Z_PAD, Q_BLOCK), :], masks)
            do2 = _stack_heads(do_ref[pl.ds(r0, Q_BLOCK), :], masks)
            return r0, q2, do2, k_ref[pl.ds(r0, K_SPAN), :]

        def ahead(b, half):
            r0, q2, do2, kcat = operands(b)
            s_scr[half] = _dot(q2, kcat, NT)
            dp_scr[half] = _dot(do2, v_ref[pl.ds(r0, K_SPAN), :], NT)

        def finish(b, half):
            r0, q2, do2, kcat = operands(b)
            p_strips, ds_strips = [], []
            for t, p in enumerate(_softmax_strips(s_scr.at[half], bt_ref, b)):
                hh, r = divmod(t * STRIP, Q_BLOCK)
                dp_t = dp_scr[half, t * STRIP:(t + 1) * STRIP, :]
                ds = p * (dp_t - jnp.sum(p * dp_t, axis=-1, keepdims=True))
                db_ref[hh, r:r + STRIP, :] += ds
                p_strips.append(p.astype(BF16))
                ds_strips.append(ds.astype(BF16))
            dq = _dot(_side_by_side_strips(ds_strips), _stack_heads(kcat, masks))
            dqkv_ref[0, pl.ds(r0, Q_BLOCK), :] = (dq * Q_SCALE).astype(BF16)
            dk_acc[pl.ds(r0, K_SPAN), :] += _dot(jnp.concatenate(ds_strips, axis=0), q2, TN)
            dv_acc[pl.ds(r0, K_SPAN), :] += _dot(jnp.concatenate(p_strips, axis=0), do2, TN)

        def two_blocks(i, carry):
            b = 2 * i
            ahead(b + 1, 1)
            finish(b, 0)
            ahead(jnp.minimum(b + 2, nb - 1), 0)
            finish(b + 1, 1)
            return carry

        ahead(0, 0)
        lax.fori_loop(0, nb // 2, two_blocks, 0)
        dqkv_ref[1] = dk_acc[Z_PAD:, :].astype(BF16)
        dqkv_ref[2] = dv_acc[Z_PAD:, :].astype(BF16)

        @pl.when(p_id == pairs - 1)
        def _():
            gather.finish()
            for cp in keep:
                cp.wait()
            for cp in _owner_copies(to_chip_refs, from_chip_refs, send_sems, recv_sems):
                cp.wait_recv()
                cp.wait_send()

    hbm = pl.BlockSpec(memory_space=pl.ANY)
    lands = ((N_DEV, SLAB_ROWS, D_MODEL), (N_DEV,) + ws_shape)
    return pl.pallas_call(
        body, name="attn_bwd",
        grid=(pairs,),
        in_specs=_attn_specs(s + Z_PAD) + [pl.BlockSpec((s, 128), lambda p: (0, p))] + [hbm] * n
        + [pl.BlockSpec(a.shape, functools.partial(lambda nd, p: (0,) * nd, a.ndim)) for a in small],
        out_specs=(pl.BlockSpec((3, s, 128), lambda p: (0, 0, p)),
                   pl.BlockSpec((2, Q_BLOCK, K_SPAN), lambda p: (p, 0, 0))) + (hbm,) * (n + 2),
        out_shape=(jax.ShapeDtypeStruct((3, s, D_A), BF16),
                   jax.ShapeDtypeStruct((N_HEADS, Q_BLOCK, K_SPAN), F32))
        + tuple(jax.ShapeDtypeStruct(t.shape, t.dtype) for t in to_chip)
        + tuple(jax.ShapeDtypeStruct(shape, F32) for shape in lands),
        scratch_shapes=[pltpu.VMEM((s + Z_PAD, 128), F32), pltpu.VMEM((s + Z_PAD, 128), F32),
                        pltpu.VMEM((2, 2 * Q_BLOCK, K_SPAN), F32), pltpu.VMEM((2, 2 * Q_BLOCK, K_SPAN), F32)]
        + [pltpu.VMEM(shape[1:], F32) for shape in lands]
        + [pltpu.SemaphoreType.DMA((n, 3)), pltpu.SemaphoreType.DMA((n, 3)),
           pltpu.SemaphoreType.DMA((2, N_DEV - 1)), pltpu.SemaphoreType.DMA((2, N_DEV - 1)),
           pltpu.SemaphoreType.DMA((2,))],
        compiler_params=_params(56),
    )(qkv, qkv, qkv, bias_table, d_out, *to_chip, *small)


def _mid_fwd_bwd(x, target, attn_out, z, w_pa, w_pb, w_out, b_gate, ln_g, ln_b, w_s, b_s, final_g):
    s = x.shape[0]
    tm = TOKEN_TILE
    nt = s // tm

    def body(x_ref, t_ref, oa_ref, ga_ref, ub_ref, vb_ref, gb_ref, ta0_ref, ta1_ref, tb0_ref, tb1_ref,
             wpa_hbm, wpb_hbm, wout_hbm, bg_ref, lng_ref, lnb_ref, ws_ref, bs_ref, fg_ref,
             dx2_ref, doa_ref, dz_ref, dwout_hbm, dwpa_hbm, dwpb_hbm, dbg_ref, dfg_ref, dlng_ref, dlnb_ref, dws_ref,
             dbs_ref, loss_ref,
             wpa, wpb, wout, wmix, acc_out, acc_pa, acc_pb, sem):
        i = pl.program_id(0)

        @pl.when(i == 0)
        def _():
            loads = [pltpu.make_async_copy(src, dst, sem.at[n])
                     for n, (src, dst) in enumerate(((wpa_hbm, wpa), (wpb_hbm, wpb), (wout_hbm, wout)))]
            for cp in loads:
                cp.start()
            t_idx = lax.broadcasted_iota(jnp.int32, (SGU_CHUNK, SGU_CHUNK), 0)
            s_idx = lax.broadcasted_iota(jnp.int32, (SGU_CHUNK, SGU_CHUNK), 1)
            for g in range(N_GROUPS):
                wmix[g] = jnp.where(s_idx <= t_idx, ws_ref[g], 0.0).astype(BF16)
            for ref in (acc_out, acc_pa, acc_pb, dbg_ref, dfg_ref, dlng_ref, dlnb_ref, dws_ref, dbs_ref, loss_ref):
                ref[...] = jnp.zeros(ref.shape, F32)
            for cp in loads:
                cp.wait()

        def tile_fwd_bwd(rows):
            g_a = ga_ref[rows, :].astype(F32)
            u_b = ub_ref[rows, :].astype(F32)
            v_b = vb_ref[rows, :].astype(F32)
            g_b = gb_ref[rows, :].astype(F32)
            bg = bg_ref[...]
            sg_a = _sigmoid(g_a)
            silu_a = g_a * sg_a
            o_a = oa_ref[rows, :]
            y_a = (o_a * silu_a).astype(BF16)
            ug, dgelu_u = _gelu_and_grad(u_b)
            vg, dgelu_v = _gelu_and_grad(v_b)
            mu = jnp.mean(vg, axis=-1, keepdims=True)
            vc = vg - mu
            rstd = lax.rsqrt(jnp.mean(vc * vc, axis=-1, keepdims=True) + EPS)
            vhat = vc * rstd
            lng = lng_ref[...]
            vn = (vhat * lng + lnb_ref[...]).astype(BF16)
            sg_b = _sigmoid(g_b)
            silu_b = g_b * sg_b
            subs = [slice(n * SGU_CHUNK, (n + 1) * SGU_CHUNK) for n in range(tm // SGU_CHUNK)]
            mixed = jnp.concatenate([jnp.concatenate(
                [_dot(wmix[g], vn[sub, g * 128:(g + 1) * 128]) + bs_ref[g] for g in range(N_GROUPS)], axis=1)
                for sub in subs], axis=0)
            um = ug * mixed
            y_b = (um * silu_b).astype(BF16)
            gate_a = _sigmoid(jnp.concatenate([ta0_ref[rows, :], ta1_ref[rows, :]], axis=1).astype(F32)
                              + bg[:, :D_MODEL])
            gate_b = _sigmoid(jnp.concatenate([tb0_ref[rows, :], tb1_ref[rows, :]], axis=1).astype(F32)
                              + bg[:, D_MODEL:])
            p_a = _dot(y_a, wpa[...])
            p_b = _dot(y_b, wpb[...])
            merged = (gate_a * p_a + gate_b * p_b).astype(BF16)
            x2 = x_ref[rows, :] + _dot(merged, wout[...])
            r2 = lax.rsqrt(jnp.mean(x2 * x2, axis=-1, keepdims=True) + EPS)
            xh = x2 * r2
            fg = fg_ref[...]
            err = xh * fg - t_ref[rows, :]
            loss_ref[...] += jnp.sum(jnp.sum(err * err, axis=-1, keepdims=True), axis=0, keepdims=True) * (0.5 / D_MODEL)
            dy = err * (1.0 / D_MODEL)
            dfg_ref[...] += jnp.sum(dy * xh, axis=0, keepdims=True)
            gy = dy * fg
            dx2 = r2 * (gy - xh * jnp.mean(gy * xh, axis=-1, keepdims=True))
            dx2_ref[rows, :] = dx2
            dx2b = dx2.astype(BF16)
            dmerged = _dot(dx2b, wout[...], NT)
            acc_out[...] += _dot(merged, dx2b, TN)
            dp_a = dmerged * gate_a
            dp_b = dmerged * gate_b
            dgate_a = dp_a * p_a * (1.0 - gate_a)
            dgate_b = dp_b * p_b * (1.0 - gate_b)
            dbg_ref[:, :D_MODEL] += jnp.sum(dgate_a, axis=0, keepdims=True)
            dbg_ref[:, D_MODEL:] += jnp.sum(dgate_b, axis=0, keepdims=True)
            dz_ref[rows, 2048:3072] = dgate_a.astype(BF16)
            dz_ref[rows, 3072:4096] = dgate_b.astype(BF16)
            dp_ab = dp_a.astype(BF16)
            dp_bb = dp_b.astype(BF16)
            dy_a = _dot(dp_ab, wpa[...], NT)
            dy_b = _dot(dp_bb, wpb[...], NT)
            acc_pa[...] += _dot(y_a, dp_ab, TN)
            acc_pb[...] += _dot(y_b, dp_bb, TN)
            doa_ref[rows, :] = (dy_a * silu_a).astype(BF16)
            dz_ref[rows, 0:512] = (dy_a * o_a * (sg_a * (1.0 + g_a * (1.0 - sg_a)))).astype(BF16)
            dz_ref[rows, 1536:2048] = (dy_b * um * (sg_b * (1.0 + g_b * (1.0 - sg_b)))).astype(BF16)
            dys = dy_b * silu_b
            dz_ref[rows, 512:1024] = (dys * mixed * dgelu_u).astype(BF16)
            dmixed = dys * ug
            dmb = dmixed.astype(BF16)
            dvn_rows = []
            for sub in subs:
                dvn_parts = []
                for g in range(N_GROUPS):
                    cols = slice(g * 128, (g + 1) * 128)
                    dws_ref[g] += _dot(dmb[sub, cols], vn[sub, cols], NT)
                    dbs_ref[g] += jnp.sum(dmixed[sub, cols], axis=-1, keepdims=True)
                    dvn_parts.append(_dot(wmix[g], dmb[sub, cols], TN))
                dvn_rows.append(jnp.concatenate(dvn_parts, axis=1))
            dvn = jnp.concatenate(dvn_rows, axis=0)
            dlng_ref[...] += jnp.sum(dvn * vhat, axis=0, keepdims=True)
            dlnb_ref[...] += jnp.sum(dvn, axis=0, keepdims=True)
            dvh = dvn * lng
            dvg = rstd * (dvh - jnp.mean(dvh, axis=-1, keepdims=True)
                          - vhat * jnp.mean(dvh * vhat, axis=-1, keepdims=True))
            dz_ref[rows, 1024:1536] = (dvg * dgelu_v).astype(BF16)

        tile_fwd_bwd(slice(0, tm))

        @pl.when(i == nt - 1)
        def _():
            t_idx = lax.broadcasted_iota(jnp.int32, (SGU_CHUNK, SGU_CHUNK), 0)
            s_idx = lax.broadcasted_iota(jnp.int32, (SGU_CHUNK, SGU_CHUNK), 1)
            for g in range(N_GROUPS):
                dws_ref[g] = jnp.where(s_idx <= t_idx, dws_ref[g], 0.0)
            stores = [pltpu.make_async_copy(src, dst, sem.at[n])
                      for n, (src, dst) in enumerate(((acc_out, dwout_hbm), (acc_pa, dwpa_hbm), (acc_pb, dwpb_hbm)))]
            for cp in stores:
                cp.start()
            for cp in stores:
                cp.wait()

    tile = lambda w: pl.BlockSpec((tm, w), lambda i: (i, 0))
    whole = lambda shape: pl.BlockSpec(shape, lambda i: (0,) * len(shape))
    hbm = pl.BlockSpec(memory_space=pl.ANY)
    return pl.pallas_call(
        body, name="mid_fwd_bwd",
        grid=(nt,),
        in_specs=[tile(D_MODEL), tile(D_MODEL), tile(D_A)]
        + [pl.BlockSpec((tm, COL_BLOCK), functools.partial(lambda c, i: (i + Z_PAD // tm, c), c))
           for c in range(3, N_COL_BLOCKS)]
        + [hbm, hbm, hbm,
                  whole((1, 2 * D_MODEL)), whole((1, D_B)), whole((1, D_B)),
                  whole((N_GROUPS, SGU_CHUNK, SGU_CHUNK)), whole((N_GROUPS, SGU_CHUNK, 1)), whole((1, D_MODEL))],
        out_specs=(tile(D_MODEL), tile(D_A), tile(REST), hbm, hbm, hbm,
                   whole((1, 2 * D_MODEL)), whole((1, D_MODEL)), whole((1, D_B)), whole((1, D_B)),
                   whole((N_GROUPS, SGU_CHUNK, SGU_CHUNK)), whole((N_GROUPS, SGU_CHUNK, 1)), whole((1, 1))),
        out_shape=(jax.ShapeDtypeStruct((s, D_MODEL), F32), jax.ShapeDtypeStruct((s, D_A), BF16),
                   jax.ShapeDtypeStruct((s, REST), BF16),
                   jax.ShapeDtypeStruct((D_MODEL, D_MODEL), F32), jax.ShapeDtypeStruct((D_A, D_MODEL), F32),
                   jax.ShapeDtypeStruct((D_B, D_MODEL), F32),
                   jax.ShapeDtypeStruct((1, 2 * D_MODEL), F32), jax.ShapeDtypeStruct((1, D_MODEL), F32),
                   jax.ShapeDtypeStruct((1, D_B), F32), jax.ShapeDtypeStruct((1, D_B), F32),
                   jax.ShapeDtypeStruct((N_GROUPS, SGU_CHUNK, SGU_CHUNK), F32),
                   jax.ShapeDtypeStruct((N_GROUPS, SGU_CHUNK, 1), F32), jax.ShapeDtypeStruct((1, 1), F32)),
        scratch_shapes=[pltpu.VMEM((D_A, D_MODEL), BF16), pltpu.VMEM((D_B, D_MODEL), BF16),
                        pltpu.VMEM((D_MODEL, D_MODEL), BF16), pltpu.VMEM((N_GROUPS, SGU_CHUNK, SGU_CHUNK), BF16),
                        pltpu.VMEM((D_MODEL, D_MODEL), F32), pltpu.VMEM((D_A, D_MODEL), F32),
                        pltpu.VMEM((D_B, D_MODEL), F32),
                        pltpu.SemaphoreType.DMA((3,))],
        compiler_params=_params(56),
    )(x, target, attn_out, *([z] * (N_COL_BLOCKS - 3)), w_pa, w_pb, w_out, b_gate, ln_g, ln_b, w_s, b_s, final_g)


def _proj_bwd_x(dqkv, drest, x, dx2, norm_g, w_in_t, to_chip, small):
    s = x.shape[0]
    tm = 512 if s % 512 == 0 else TOKEN_TILE
    nt = s // tm
    rows = to_chip.shape[1]
    half = D_MODEL // 2
    left, right = slice(0, half), slice(half, D_MODEL)

    def body(dqkv_ref, dr_ref, x_ref, dx2_ref, g_ref, w_hbm, tc_hbm, ng_ref, rel_ref,
             dx_ref, fc_ref, slab_land,
             w, tc_ref, slab_stage, via_x, via_y, mine, out_x, out_y, sem, send_sems, recv_sems,
             gather_send, gather_recv, keep_sems):
        i = pl.program_id(0)
        x_, y_, c_ = _my_pos()
        me = _flat_id((x_, y_, c_))
        xn, yn = (1 - x_, y_, c_), (x_, 1 - y_, c_)
        gather = _SlotGather([slab_land], gather_send, gather_recv, own=[slab_stage])
        keep = [pltpu.make_async_copy(slab_stage, slab_land.at[me], keep_sems.at[0])]

        def copy(k, src, dst, to):
            return pltpu.make_async_remote_copy(src_ref=src, dst_ref=dst, send_sem=send_sems.at[k],
                                                recv_sem=recv_sems.at[k], device_id=to, device_id_type=MESH)

        first = [copy(0, tc_ref.at[0, :, left], fc_ref.at[0, :, left], xn), copy(1, tc_ref.at[2, :, left], via_x, xn),
                 copy(2, tc_ref.at[1, :, right], fc_ref.at[1, :, right], yn), copy(3, tc_ref.at[2, :, right], via_y, yn)]
        second = [copy(4, out_y, fc_ref.at[1, :, left], yn), copy(5, out_x, fc_ref.at[0, :, right], xn)]

        def add_and_send(arrival, landed, own_half, stage, onward):
            load = pltpu.make_async_copy(own_half, mine, sem)
            load.start()
            arrival.wait_recv()
            load.wait()
            stage[...] = (mine[...].astype(F32) + landed[...].astype(F32)).astype(BF16)
            onward.start()

        @pl.when(i == 0)
        def _():
            cp = pltpu.make_async_copy(w_hbm, w, sem)
            cp.start()
            _fill_slab(slab_stage, ((ROW_NORM_G, ng_ref), (ROW_REL, rel_ref)))
            for cp_keep in keep:
                cp_keep.start()
            gather.start()
            stage_in = pltpu.make_async_copy(tc_hbm, tc_ref, keep_sems.at[2])
            stage_in.start()
            stage_in.wait()
            for rc in first:
                rc.start()
            cp.wait()

        @pl.when(i == (5 * nt) // 8)
        def _():
            gather.pass_on()
            add_and_send(first[1], via_x, tc_ref.at[1, :, left], out_y, second[0])
            add_and_send(first[3], via_y, tc_ref.at[0, :, right], out_x, second[1])

        dh = None
        for c in range(N_COL_BLOCKS):
            dz = dqkv_ref[c] if c < 3 else dr_ref[:, (c - 3) * COL_BLOCK:(c - 2) * COL_BLOCK]
            part = _dot(dz, w[c * COL_BLOCK:(c + 1) * COL_BLOCK, :])
            dh = part if dh is None else dh + part
        xf = x_ref[...]
        r = lax.rsqrt(jnp.mean(xf * xf, axis=-1, keepdims=True) + EPS)
        xn = xf * r
        gh = dh * g_ref[...]
        dx_ref[...] = r * (gh - xn * jnp.mean(gh * xn, axis=-1, keepdims=True)) + dx2_ref[...]

        @pl.when(i == nt - 1)
        def _():
            gather.finish()
            for cp_keep in keep:
                cp_keep.wait()
            for k in (0, 2, 4, 5):
                (first + second)[k].wait_recv()
            for rc in first + second:
                rc.wait_send()

    hbm = pl.BlockSpec(memory_space=pl.ANY)
    whole = lambda a: pl.BlockSpec(a.shape, lambda i: (0,) * a.ndim)
    return pl.pallas_call(
        body, name="proj_bwd_x",
        grid=(nt,),
        in_specs=[pl.BlockSpec((3, tm, D_A), lambda i: (0, i, 0)),
                  pl.BlockSpec((tm, REST), lambda i: (i, 0)),
                  pl.BlockSpec((tm, D_MODEL), lambda i: (i, 0)),
                  pl.BlockSpec((tm, D_MODEL), lambda i: (i, 0)),
                  pl.BlockSpec((1, D_MODEL), lambda i: (0, 0)),
                  hbm, hbm] + [whole(a) for a in small],
        out_specs=(pl.BlockSpec((tm, D_MODEL), lambda i: (i, 0)), hbm, hbm),
        out_shape=(jax.ShapeDtypeStruct((s, D_MODEL), F32), jax.ShapeDtypeStruct((2, rows, D_MODEL), BF16),
                   jax.ShapeDtypeStruct((N_DEV, LATE_SLAB_ROWS, D_MODEL), F32)),
        scratch_shapes=[pltpu.VMEM((D_IN, D_MODEL), BF16), pltpu.VMEM(to_chip.shape, BF16),
                        pltpu.VMEM((LATE_SLAB_ROWS, D_MODEL), F32)]
        + [pltpu.VMEM((rows, half), BF16)] * 5
        + [pltpu.SemaphoreType.DMA, pltpu.SemaphoreType.DMA((6,)), pltpu.SemaphoreType.DMA((6,)),
           pltpu.SemaphoreType.DMA((1, N_DEV - 1)), pltpu.SemaphoreType.DMA((1, N_DEV - 1)),
           pltpu.SemaphoreType.DMA((3,))],
        compiler_params=_params(56),
    )(dqkv, drest, x, dx2, norm_g, w_in_t, to_chip, *small)


def _proj_bwd_w(xn, dqkv, drest, norm_g, w_in_t):
    s = xn.shape[0]
    tk = s
    nk = s // tk

    def body(xn_ref, dqkv_ref, dr_ref, g_ref, w_ref, o_ref, dg_ref, acc):
        j = pl.program_id(0)
        i = pl.program_id(1)

        @pl.when((j == 0) & (i == 0))
        def _():
            dg_ref[...] = jnp.zeros(dg_ref.shape, F32)

        @pl.when(i == 0)
        def _():
            acc[...] = jnp.zeros(acc.shape, F32)

        @pl.when(j < 3)
        def _():
            acc[...] += _dot(dqkv_ref[...], xn_ref[...], TN)

        @pl.when(j >= 3)
        def _():
            acc[...] += _dot(dr_ref[...], xn_ref[...], TN)

        @pl.when(i == nk - 1)
        def _():
            m = acc[...]
            o_ref[...] = (m * g_ref[...]).astype(BF16)
            dg_ref[...] += jnp.sum(m * w_ref[...].astype(F32), axis=0, keepdims=True)

    return pl.pallas_call(
        body, name="proj_bwd_w",
        grid=(N_COL_BLOCKS, nk),
        in_specs=[pl.BlockSpec((tk, D_MODEL), lambda j, i: (i, 0)),
                  pl.BlockSpec((None, tk, COL_BLOCK),
                               lambda j, i: (jnp.minimum(j, 2), jnp.where(j < 3, i, nk - 1), 0)),
                  pl.BlockSpec((tk, COL_BLOCK),
                               lambda j, i: (jnp.where(j >= 3, i, 0), jnp.maximum(j - 3, 0))),
                  pl.BlockSpec((1, D_MODEL), lambda j, i: (0, 0)),
                  pl.BlockSpec((COL_BLOCK, D_MODEL), lambda j, i: (j, 0))],
        out_specs=(pl.BlockSpec((COL_BLOCK, D_MODEL), lambda j, i: (j, 0)),
                   pl.BlockSpec((1, D_MODEL), lambda j, i: (0, 0))),
        out_shape=(jax.ShapeDtypeStruct((D_IN, D_MODEL), BF16), jax.ShapeDtypeStruct((1, D_MODEL), F32)),
        scratch_shapes=[pltpu.VMEM((COL_BLOCK, D_MODEL), F32)],
        compiler_params=_params(56),
    )(xn, dqkv, drest, norm_g, w_in_t)


def _adamw_math(w, g, m, v):
    c1 = 1.0 - ADAM_B1 ** ADAM_STEP
    c2 = 1.0 - ADAM_B2 ** ADAM_STEP
    nm = ADAM_B1 * m + (1.0 - ADAM_B1) * g
    nv = ADAM_B2 * v + (1.0 - ADAM_B2) * (g * g)
    return -ADAM_LR * ((nm / c1) / (jnp.sqrt(nv / c2) + ADAM_EPS) + ADAM_WD * w), nm, nv


def _adamw(name, w, g, m, v, from_chip):
    rows, cols = w.shape
    tr = rows if rows * cols <= 512 * 1024 else next(t for t in range(256, 7, -8) if rows % t == 0)

    def body(w_ref, g_ref, m_ref, v_ref, t_ref, g_out, d_ref, nm_ref, nv_ref):
        gg = g_ref[...]
        for j in range(from_chip.shape[0]):
            gg = gg + t_ref[j].astype(F32)
        g_out[...] = gg
        d_ref[...], nm_ref[...], nv_ref[...] = _adamw_math(w_ref[...], gg, m_ref[...], v_ref[...])

    spec = pl.BlockSpec((tr, cols), lambda i: (i, 0))
    shape = jax.ShapeDtypeStruct((rows, cols), F32)
    return pl.pallas_call(
        body, name=name,
        grid=(rows // tr,),
        in_specs=[spec] * 4 + [pl.BlockSpec((from_chip.shape[0], tr, cols), lambda i: (0, i, 0))],
        out_specs=(spec,) * 4, out_shape=(shape,) * 4,
        compiler_params=_params(32),
    )(w, g, m, v, from_chip)


_SMALL = (("norm_g", (1, D_MODEL)), ("b_gate", (1, 2 * D_MODEL)), ("rel_bias", (N_HEADS, N_REL)),
          ("sgu_ln_g", (1, D_B)), ("sgu_ln_b", (1, D_B)), ("w_s", (N_GROUPS * SGU_CHUNK, SGU_CHUNK)),
          ("b_s", (N_GROUPS, SGU_CHUNK)), ("final_g", (1, D_MODEL)))


def _adamw_small(slabs, ws_all, late_slabs, weights, moments_m, moments_v):
    k = len(_SMALL)

    def total(ref):
        acc = ref[0]
        for d in range(1, N_DEV):
            acc = acc + ref[d]
        return acc

    def body(*refs):
        slab_ref, ws_ref, late_ref = refs[:3]
        w_refs, m_refs, v_refs = refs[3:3 + k], refs[3 + k:3 + 2 * k], refs[3 + 2 * k:3 + 3 * k]
        outs = refs[3 + 3 * k:]
        slab, late = total(slab_ref), total(late_ref)
        grads = {
            "norm_g": late[ROW_NORM_G:ROW_NORM_G + 1, :],
            "b_gate": jnp.concatenate([slab[ROW_B_GATE:ROW_B_GATE + 1, :], slab[ROW_B_GATE + 1:ROW_B_GATE + 2, :]], axis=1),
            "rel_bias": late[ROW_REL:ROW_REL + N_HEADS, :N_REL],
            "sgu_ln_g": slab[ROW_LN_G:ROW_LN_G + 1, :D_B],
            "sgu_ln_b": slab[ROW_LN_B:ROW_LN_B + 1, :D_B],
            "w_s": total(ws_ref),
            "b_s": slab[ROW_B_S:ROW_B_S + N_GROUPS, :SGU_CHUNK],
            "final_g": slab[ROW_FINAL_G:ROW_FINAL_G + 1, :],
        }
        for n, (name, _) in enumerate(_SMALL):
            g = grads[name]
            outs[n][...] = g
            outs[k + n][...], outs[2 * k + n][...], outs[3 * k + n][...] = _adamw_math(
                w_refs[n][...], g, m_refs[n][...], v_refs[n][...])
        outs[4 * k][...] = slab[ROW_LOSS:ROW_LOSS + 1, :1]

    vmem = pl.BlockSpec(memory_space=pltpu.VMEM)
    shapes = tuple(jax.ShapeDtypeStruct(shape, F32) for _, shape in _SMALL)
    return pl.pallas_call(
        body, name="adamw_small",
        out_shape=shapes * 4 + (jax.ShapeDtypeStruct((1, 1), F32),),
        in_specs=[vmem] * (3 + 3 * k), out_specs=tuple([vmem] * (4 * k + 1)),
        compiler_params=_params(16),
    )(slabs, ws_all, late_slabs, *weights, *moments_m, *moments_v)


def _pad_rel(a):
    return jnp.pad(a.reshape(N_HEADS, N_REL), ((0, 0), (0, N_REL_PAD - N_REL)))


def kernel(x, norm_g, w_in, b_gate, rel_bias, sgu_ln_g, sgu_ln_b, w_s, b_s, w_pa, w_pb, w_out, final_g, loss_target, m_norm_g, m_w_in, m_b_gate, m_rel_bias, m_sgu_ln_g, m_sgu_ln_b, m_w_s, m_b_s, m_w_pa, m_w_pb, m_w_out, m_final_g, v_norm_g, v_w_in, v_b_gate, v_rel_bias, v_sgu_ln_g, v_sgu_ln_b, v_w_s, v_b_s, v_w_pa, v_w_pb, v_w_out, v_final_g):
    s = x.shape[1]
    xs = x.reshape(s, D_MODEL)
    tgt = loss_target.reshape(s, D_MODEL)

    in_hbm = lambda a: pltpu.with_memory_space_constraint(a, pltpu.HBM)
    bias_table = in_hbm(_bias_table(_pad_rel(rel_bias)))
    w_in_t = jnp.swapaxes(w_in[0], 0, 1)
    qkv, x_norm, w_in_t_full = _gather_proj_fwd(xs, norm_g, w_in_t)
    x_norm = in_hbm(x_norm)
    attn_out, g_pa, g_pb, g_out = _attn_fwd(qkv, bias_table, (w_pa[0], w_pb[0], w_out[0]))
    w_pa_full = jnp.transpose(g_pa, (1, 0, 2)).reshape(D_A, D_MODEL)
    w_pb_full = jnp.transpose(g_pb, (1, 0, 2)).reshape(D_B, D_MODEL)
    w_out_full = g_out.reshape(D_MODEL, D_MODEL)

    (dx2, d_attn, drest, dw_out, dw_pa, dw_pb, d_bgate, d_fg, d_lng, d_lnb, d_ws, d_bs, loss_part) = _mid_fwd_bwd(
        xs, tgt, attn_out, qkv, w_pa_full, w_pb_full, w_out_full, b_gate, sgu_ln_g, sgu_ln_b, w_s[0],
        b_s.reshape(N_GROUPS, SGU_CHUNK, 1), final_g.reshape(1, D_MODEL))

    own_pa, own_pb, own_out, tc_pa, tc_pb, tc_out = _reduce_chip(
        "reduce_chip_proj", (dw_pa, dw_pb, dw_out), (1, 1, 0))
    dqkv, dbias, fc_pa, fc_pb, fc_out, slabs, ws_all = _attn_bwd(
        qkv, bias_table, d_attn, (tc_pa, tc_pb, tc_out),
        (d_bgate, d_lng, d_lnb, d_fg, loss_part, d_bs, d_ws.reshape(N_GROUPS * SGU_CHUNK, SGU_CHUNK)))
    d_rel = _bias_grad(dbias)
    dw_in_t, d_ng = _proj_bwd_w(x_norm, dqkv, drest, norm_g, w_in_t_full)
    own_in, tc_in = _reduce_chip("reduce_chip_in", (dw_in_t,), (0,))
    grad_x, fc_in, late_slabs = _proj_bwd_x(dqkv, drest, xs, dx2, norm_g, w_in_t_full, tc_in, (d_ng, d_rel))
    big = {"w_in": tuple(jnp.swapaxes(t, 0, 1)[None] for t in _adamw(
        "adamw_w_in", w_in_t, own_in, jnp.swapaxes(m_w_in[0], 0, 1), jnp.swapaxes(v_w_in[0], 0, 1), fc_in))}
    for name, w, g, fc, m, v in (("w_pa", w_pa, own_pa, fc_pa, m_w_pa, v_w_pa),
                                 ("w_pb", w_pb, own_pb, fc_pb, m_w_pb, v_w_pb),
                                 ("w_out", w_out, own_out, fc_out, m_w_out, v_w_out)):
        big[name] = tuple(t[None] for t in _adamw("adamw_" + name, w[0], g, m[0], v[0], fc))

    as_2d = lambda leaves: [a.reshape(shape) for a, (_, shape) in zip(leaves, _SMALL)]
    small_out = _adamw_small(
        slabs, ws_all, late_slabs, as_2d((norm_g, b_gate, rel_bias, sgu_ln_g, sgu_ln_b, w_s, b_s, final_g)),
        as_2d((m_norm_g, m_b_gate, m_rel_bias, m_sgu_ln_g, m_sgu_ln_b, m_w_s, m_b_s, m_final_g)),
        as_2d((v_norm_g, v_b_gate, v_rel_bias, v_sgu_ln_g, v_sgu_ln_b, v_w_s, v_b_s, v_final_g)))
    small_index = {name: n for n, (name, _) in enumerate(_SMALL)}

    def leaf(kind, name, like):
        if name in big:
            return big[name][kind]
        return small_out[kind * len(_SMALL) + small_index[name]].reshape(like.shape)

    weights = (("norm_g", norm_g), ("w_in", w_in), ("b_gate", b_gate), ("rel_bias", rel_bias), ("sgu_ln_g", sgu_ln_g),
               ("sgu_ln_b", sgu_ln_b), ("w_s", w_s), ("b_s", b_s), ("w_pa", w_pa), ("w_pb", w_pb), ("w_out", w_out),
               ("final_g", final_g))
    outs = [small_out[-1].reshape(()), grad_x.reshape(x.shape)]
    for kind in range(4):
        outs.extend(leaf(kind, name, like) for name, like in weights)
    return tuple(outs)
```

```python
import functools
import math

import jax
import jax.numpy as jnp
from jax import lax
from jax.experimental import pallas as pl
from jax.experimental.pallas import tpu as pltpu

F32 = jnp.float32
BF16 = jnp.bfloat16
MESH = pl.DeviceIdType.MESH
N_DEV = 8

D_MODEL = 1024
D_A = 512
D_B = 512
D_IN = 5632
N_HEADS = 8
HEAD_DIM = 64
N_PREV = 8
REL_CLIP = 128
N_REL = 2 * REL_CLIP + 1
N_REL_PAD = 384
SGU_CHUNK = 128
N_GROUPS = 4
EPS = 1e-6
NEG_INF = -1e30
Q_SCALE = HEAD_DIM ** -0.5

Q_BLOCK = 256
K_SPAN = 768
Z_PAD = K_SPAN - Q_BLOCK
ROLL_W = 1024
COL_BLOCK = 512
N_COL_BLOCKS = D_IN // COL_BLOCK
REST = D_IN - 3 * D_A
TOKEN_TILE = 256

ADAM_LR = 0.001
ADAM_B1 = 0.9
ADAM_B2 = 0.999
ADAM_EPS = 1e-08
ADAM_WD = 0.01
ADAM_STEP = 10

GELU_C = math.sqrt(2.0 / math.pi)
GELU_A = 0.044715

NT = (((1,), (1,)), ((), ()))
TN = (((0,), (0,)), ((), ()))
HIGHEST = lax.Precision.HIGHEST


def _params(vmem_mb, **kw):
    return pltpu.CompilerParams(vmem_limit_bytes=vmem_mb * 1024 * 1024, **kw)


def _dot(a, b, dims=None):
    if dims is None:
        return jnp.dot(a, b, preferred_element_type=F32)
    return lax.dot_general(a, b, dims, preferred_element_type=F32)


def _sigmoid(x):
    return 0.5 * jnp.tanh(0.5 * x) + 0.5


def _gelu_and_grad(u):
    u2 = u * u
    t = jnp.tanh(GELU_C * (u + GELU_A * u * u2))
    half = 0.5 * (1.0 + t)
    g = u * half
    dg = half + 0.5 * u * (1.0 - t * t) * (GELU_C * (1.0 + 3.0 * GELU_A * u2))
    return g, dg


def _my_pos():
    return lax.axis_index("x"), lax.axis_index("y"), lax.axis_index("c")


def _flat_id(pos):
    return 4 * pos[0] + 2 * pos[1] + pos[2]


def _other_chips(pos):
    x, y, _ = pos
    return ((1 - x, y), (x, 1 - y), (1 - x, 1 - y))


class _SlotGather:
    def __init__(self, bufs, send_sems, recv_sems, own=None):
        self.bufs, self.send_sems, self.recv_sems = bufs, send_sems, recv_sems
        self.own = own if own is not None else [None] * len(bufs)
        x, y, c = _my_pos()
        self.c, self.me, self.sib = c, (x, y, c), (x, y, 1 - c)
        self.chips = _other_chips(self.me)

    def _copy(self, a, k, block, to):
        slot = _flat_id(block)
        src = self.own[a] if (k < 4 and self.own[a] is not None) else self.bufs[a].at[slot]
        return pltpu.make_async_remote_copy(
            src_ref=src, dst_ref=self.bufs[a].at[slot],
            send_sem=self.send_sems.at[a, k], recv_sem=self.recv_sems.at[a, k], device_id=to, device_id_type=MESH)

    def _own_sends(self):
        n = len(self.bufs)
        return ([self._copy(a, 1 + j, self.me, (*chip, self.c)) for j, chip in enumerate(self.chips) for a in range(n)]
                + [self._copy(a, 0, self.me, self.sib) for a in range(n)])

    def _passes(self):
        return [self._copy(a, 4 + j, (*chip, self.c), self.sib)
                for j, chip in enumerate(self.chips) for a in range(len(self.bufs))]

    def start(self):
        for cp in self._own_sends():
            cp.start()

    def pass_on(self):
        for j, chip in enumerate(self.chips):
            for a in range(len(self.bufs)):
                self._copy(a, 1 + j, (*chip, self.c), self.me).wait_recv()
                self._copy(a, 4 + j, (*chip, self.c), self.sib).start()

    def finish(self):
        for a in range(len(self.bufs)):
            self._copy(a, 0, self.sib, self.me).wait_recv()
            for j, chip in enumerate(self.chips):
                self._copy(a, 4 + j, (*chip, 1 - self.c), self.me).wait_recv()
        for cp in self._own_sends() + self._passes():
            cp.wait_send()


def _reduce_chip(name, parts, sharded_dim):
    n = len(parts)
    shapes = []
    for p, dim in zip(parts, sharded_dim):
        shape = list(p.shape)
        shape[dim] //= N_DEV
        shapes.append(tuple(shape))
    staged = [not (dim == 0 and p.dtype == BF16) for p, dim in zip(parts, sharded_dim)]

    def body(*refs):
        full, own, to_chip = refs[:n], refs[n:2 * n], refs[2 * n:3 * n]
        ins, from_sib = refs[3 * n:4 * n], refs[4 * n:5 * n]
        send_sems, recv_sems, load_sems = refs[5 * n:]
        x, y, c = _my_pos()
        sib = (x, y, 1 - c)
        chips = ((x, y),) + _other_chips((x, y, c))
        for a in range(n):
            rows, cols = shapes[a]
            for d in range(N_DEV if staged[a] else 0):
                if sharded_dim[a] == 0:
                    ins[a][d] = full[a][d * rows:(d + 1) * rows, :].astype(BF16)
                else:
                    ins[a][d] = full[a][:, d * cols:(d + 1) * cols].astype(BF16)

        def block(a, d):
            if staged[a]:
                return ins[a].at[d]
            rows = shapes[a][0]
            return full[a].at[pl.ds(pl.multiple_of(d * rows, 16), rows), :]

        def to_sibling(a, r):
            return pltpu.make_async_remote_copy(
                src_ref=block(a, _flat_id((*chips[r], 1 - c))), dst_ref=from_sib[a].at[r],
                send_sem=send_sems.at[a, r], recv_sem=recv_sems.at[a, r], device_id=sib, device_id_type=MESH)

        def load(a, r):
            return pltpu.make_async_copy(block(a, _flat_id((*chips[r], c))), ins[a].at[r], load_sems.at[a, r])

        sends = [to_sibling(a, r) for r in (1, 2, 3, 0) for a in range(n)]
        loads = [load(a, r) for r in (1, 2, 3, 0) for a in range(n) if not staged[a]]
        for cp in sends + loads:
            cp.start()
        for r in (1, 2, 3, 0):
            for a in range(n):
                if staged[a]:
                    mine = block(a, _flat_id((*chips[r], c)))[...]
                else:
                    load(a, r).wait()
                    mine = ins[a][r]
                to_sibling(a, r).wait_recv()
                both = mine.astype(F32) + from_sib[a][r].astype(F32)
                if r == 0:
                    own[a][...] = both
                else:
                    to_chip[a][r - 1] = both.astype(BF16)
        for cp in sends:
            cp.wait_send()

    vmem = pl.BlockSpec(memory_space=pltpu.VMEM)
    return pl.pallas_call(
        body, name=name,
        out_shape=tuple(jax.ShapeDtypeStruct(sh, F32) for sh in shapes)
        + tuple(jax.ShapeDtypeStruct((3,) + sh, BF16) for sh in shapes),
        in_specs=[vmem if st else pl.BlockSpec(memory_space=pl.ANY) for st in staged],
        out_specs=tuple([vmem] * (2 * n)),
        scratch_shapes=[pltpu.VMEM((N_DEV if st else 4,) + sh, BF16) for sh, st in zip(shapes, staged)]
        + [pltpu.VMEM((4,) + sh, BF16) for sh in shapes]
        + [pltpu.SemaphoreType.DMA((n, 4))] * 3,
        compiler_params=_params(56),
    )(*parts)


def _owner_copies(to_chip, from_chip, send_sems, recv_sems):
    x, y, c = _my_pos()
    return [pltpu.make_async_remote_copy(
        src_ref=to_chip[a].at[j], dst_ref=from_chip[a].at[j],
        send_sem=send_sems.at[a, j], recv_sem=recv_sems.at[a, j], device_id=(*chip, c), device_id_type=MESH)
        for a in range(len(to_chip)) for j, chip in enumerate(_other_chips((x, y, c)))]


ROW_B_GATE, ROW_LN_G, ROW_LN_B, ROW_FINAL_G, ROW_LOSS, ROW_B_S, SLAB_ROWS = 1, 3, 4, 5, 6, 16, 24
ROW_NORM_G, ROW_REL, LATE_SLAB_ROWS = 0, 8, 16


def _rel_index(e):
    lo, hi = Z_PAD - REL_CLIP, Z_PAD + REL_CLIP
    return jnp.where(e <= lo, 2 * REL_CLIP, jnp.where(e < hi, hi - e, jnp.where(e <= K_SPAN, 0, 2 * REL_CLIP)))


def _bias_table(rel_bias_pad):
    def body(rb_ref, bt_ref):
        c = lax.broadcasted_iota(jnp.int32, (N_REL_PAD, ROLL_W), 1)
        r = lax.broadcasted_iota(jnp.int32, (N_REL_PAD, ROLL_W), 0)
        pick = (r == _rel_index(c)).astype(F32)
        rows = jnp.dot(rb_ref[...], pick, precision=HIGHEST, preferred_element_type=F32)
        qc = lax.broadcasted_iota(jnp.int32, (Q_BLOCK, K_SPAN), 0) >> 6
        kc = lax.broadcasted_iota(jnp.int32, (Q_BLOCK, K_SPAN), 1) >> 6
        band = (kc >= qc) & (kc <= qc + N_PREV)
        for h in range(N_HEADS):
            t = jnp.broadcast_to(rows[h:h + 1, :], (Q_BLOCK, ROLL_W))
            t = pltpu.roll(t, 0, 1, stride=1, stride_axis=0)
            bt_ref[h] = jnp.where(band, t[:, :K_SPAN], NEG_INF)

    return pl.pallas_call(
        body, name="bias_table",
        out_shape=jax.ShapeDtypeStruct((N_HEADS, Q_BLOCK, K_SPAN), F32),
        compiler_params=_params(32),
    )(rel_bias_pad)


def _bias_grad(dbias):
    def body(a_ref, o_ref):
        rr = lax.broadcasted_iota(jnp.int32, (Q_BLOCK, Q_BLOCK), 0)
        cc = lax.broadcasted_iota(jnp.int32, (Q_BLOCK, Q_BLOCK), 1)
        flip = (rr + cc == Q_BLOCK - 1).astype(F32)
        c = lax.broadcasted_iota(jnp.int32, (ROLL_W, N_REL_PAD), 0)
        r = lax.broadcasted_iota(jnp.int32, (ROLL_W, N_REL_PAD), 1)
        e = jnp.where(c >= Q_BLOCK - 1, c - (Q_BLOCK - 1), c + (ROLL_W - Q_BLOCK + 1))
        pick = (r == _rel_index(e)).astype(F32)
        sums = []
        for h in range(N_HEADS):
            a = jnp.dot(flip, a_ref[h], precision=HIGHEST, preferred_element_type=F32)
            a = jnp.concatenate([a, jnp.zeros((Q_BLOCK, ROLL_W - K_SPAN), F32)], axis=1)
            a = pltpu.roll(a, 0, 1, stride=1, stride_axis=0)
            sums.append(jnp.sum(a, axis=0, keepdims=True))
        diag = jnp.concatenate(sums, axis=0)
        o_ref[...] = jnp.dot(diag, pick, precision=HIGHEST, preferred_element_type=F32)

    return pl.pallas_call(
        body, name="bias_grad",
        out_shape=jax.ShapeDtypeStruct((N_HEADS, N_REL_PAD), F32),
        compiler_params=_params(32),
    )(dbias)


def _gather_proj_fwd(x, norm_g, w_in_t):
    s = x.shape[0]
    tm = 512 if s % 512 == 0 else TOKEN_TILE
    nt = s // tm
    n_pad = Z_PAD // tm
    shard_w = w_in_t.shape[0]
    chip_w = 2 * shard_w
    n_chips = N_DEV // 2

    def body(order_ref, x_ref, g_ref, win_hbm, z_ref, xn_ref, wt_hbm, wt, hb, win_f32, send_sems, recv_sems,
             local_sems):
        j = pl.program_id(0)
        i = pl.program_id(1)
        x_, y_, c_ = _my_pos()
        me, sib = (x_, y_, c_), (x_, y_, 1 - c_)
        near = _other_chips(me)
        pick = lambda a, b: tuple(jnp.where(c_ == 0, u, v) for u, v in zip(a, b))
        passed_from, passed_to = pick(near[0], near[1]), pick(near[1], near[0])

        def rows_of(block):
            return wt.at[pl.ds(pl.multiple_of(_flat_id(block) * shard_w, 16), shard_w), :]

        def copy(k, block, to):
            return pltpu.make_async_remote_copy(
                src_ref=rows_of(block), dst_ref=rows_of(block),
                send_sem=send_sems.at[k], recv_sem=recv_sems.at[k], device_id=to, device_id_type=MESH)

        def sends():
            return ([copy(0, me, sib), copy(1, me, (*near[0], c_)), copy(2, me, (*near[1], c_)),
                     copy(3, (*passed_from, c_), (*passed_to, c_))]
                    + [copy(4 + n, (*near[n], c_), sib) for n in range(3)])

        keep = pltpu.make_async_copy(wt, wt_hbm, local_sems.at[0])

        @pl.when((j == 0) & (i == 0))
        def _():
            load = pltpu.make_async_copy(win_hbm, win_f32, local_sems.at[1])
            load.start()
            load.wait()
            rows_of(me)[...] = win_f32[...].astype(BF16)
            for cp in sends()[:3]:
                cp.start()
            copy(0, sib, me).wait_recv()

        @pl.when((j == 1) & (i == 0))
        def _():
            copy(1, (*near[0], c_), me).wait_recv()
            copy(2, (*near[1], c_), me).wait_recv()
            for cp in sends()[3:6]:
                cp.start()
            copy(4, (*near[0], 1 - c_), me).wait_recv()

        @pl.when((j == 2) & (i == 0))
        def _():
            copy(5, (*near[1], 1 - c_), me).wait_recv()

        @pl.when((j == 3) & (i == 0))
        def _():
            copy(3, (*near[2], c_), me).wait_recv()
            copy(6, (*near[2], c_), sib).start()
            copy(6, (*near[2], 1 - c_), me).wait_recv()
            keep.start()

        @pl.when(i < n_pad)
        def _():
            z_ref[...] = jnp.zeros(z_ref.shape, BF16)

        @pl.when(i >= n_pad)
        def _():
            rows = pl.ds(pl.multiple_of((i - n_pad) * tm, tm), tm)

            @pl.when(j == 0)
            def _():
                xf = x_ref[...]
                xn = xf * lax.rsqrt(jnp.mean(xf * xf, axis=-1, keepdims=True) + EPS)
                hb[rows, :] = (xn * g_ref[...]).astype(BF16)
                xn_ref[...] = xn.astype(BF16)

            chip_rows = pl.ds(pl.multiple_of(order_ref[j] * chip_w, 16), chip_w)
            blk = _dot(hb[rows, :], wt[chip_rows, :], NT)
            q_scale = jnp.where(order_ref[j] == 0, Q_SCALE, 1.0).astype(F32)
            z_ref[:, :D_A] = (blk[:, :D_A] * q_scale).astype(BF16)
            z_ref[:, D_A:] = blk[:, D_A:].astype(BF16)

        @pl.when((j == n_chips - 1) & (i == n_pad + nt - 1))
        def _():
            keep.wait()
            for cp in sends():
                cp.wait_send()

    pos = _my_pos()
    order = jnp.stack([2 * cx + cy for cx, cy in ((pos[0], pos[1]),) + _other_chips(pos)]).astype(jnp.int32)
    first_pass = lambda j, i: jnp.where(j == 0, jnp.maximum(i - n_pad, 0), nt - 1)
    grid_spec = pltpu.PrefetchScalarGridSpec(
        num_scalar_prefetch=1,
        grid=(n_chips, n_pad + nt),
        in_specs=[pl.BlockSpec((tm, D_MODEL), lambda j, i, o: (first_pass(j, i), 0)),
                  pl.BlockSpec((1, D_MODEL), lambda j, i, o: (0, 0)),
                  pl.BlockSpec(memory_space=pl.ANY)],
        out_specs=(pl.BlockSpec((tm, chip_w), lambda j, i, o: (i, o[j])),
                   pl.BlockSpec((tm, D_MODEL), lambda j, i, o: (first_pass(j, i), 0)),
                   pl.BlockSpec(memory_space=pl.ANY)),
        scratch_shapes=[pltpu.VMEM((D_IN, D_MODEL), BF16),
                        pltpu.VMEM((s, D_MODEL), BF16), pltpu.VMEM(w_in_t.shape, F32),
                        pltpu.SemaphoreType.DMA((N_DEV - 1,)), pltpu.SemaphoreType.DMA((N_DEV - 1,)),
                        pltpu.SemaphoreType.DMA((2,))])
    return pl.pallas_call(
        body, name="gather_proj_fwd",
        grid_spec=grid_spec,
        out_shape=(jax.ShapeDtypeStruct((Z_PAD + s, D_IN), BF16), jax.ShapeDtypeStruct((s, D_MODEL), BF16),
                   jax.ShapeDtypeStruct((D_IN, D_MODEL), BF16)),
        compiler_params=_params(60),
    )(order, x, norm_g, w_in_t)


def _attn_specs(rows):
    pairs = N_HEADS // 2
    return ([pl.BlockSpec((rows, 128), functools.partial(lambda which, p: (0, which * pairs + p), which))
             for which in range(3)]
            + [pl.BlockSpec((2, Q_BLOCK, K_SPAN), lambda p: (p, 0, 0))])


def _head_masks():
    lane = lax.broadcasted_iota(jnp.int32, (1, 128), 1)
    first = lane < HEAD_DIM
    return (first, jnp.logical_not(first))


def _stack_heads(x, masks):
    zero = jnp.zeros((), x.dtype)
    return jnp.concatenate([jnp.where(m, x, zero) for m in masks], axis=0)


STRIP = 16


def _softmax_strips(s_ref, bias_ref, b):
    valid = lax.broadcasted_iota(jnp.int32, (1, K_SPAN), 1) >= Z_PAD - b * Q_BLOCK
    for t in range(2 * Q_BLOCK // STRIP):
        hh, r = divmod(t * STRIP, Q_BLOCK)
        st = s_ref[t * STRIP:(t + 1) * STRIP, :] + bias_ref[hh, r:r + STRIP, :]
        st = jnp.where(valid, st, NEG_INF)
        e = jnp.exp(st - jnp.max(st, axis=-1, keepdims=True))
        yield e * (1.0 / jnp.sum(e, axis=-1, keepdims=True))


def _side_by_side_strips(strips):
    half = len(strips) // 2
    return jnp.concatenate([jnp.concatenate([a, c], axis=1) for a, c in zip(strips[:half], strips[half:])], axis=0)


def _attn_fwd(qkv, bias_table, shards):
    s = qkv.shape[0] - Z_PAD
    nb = s // Q_BLOCK
    n = len(shards)
    pairs = N_HEADS // 2

    def body(*refs):
        q_ref, k_ref, v_ref, bt_ref = refs[:4]
        shard_refs = refs[4:4 + n]
        o_ref = refs[4 + n]
        slot_refs = refs[5 + n:5 + 2 * n]
        stages = refs[5 + 2 * n:5 + 3 * n]
        s_scr, send_sems, recv_sems, local_sems = refs[5 + 3 * n:]
        p_id = pl.program_id(0)
        gather = _SlotGather(slot_refs, send_sems, recv_sems, own=stages)
        keep = [pltpu.make_async_copy(stages[a], slot_refs[a].at[_flat_id(_my_pos())], local_sems.at[a])
                for a in range(n)]

        @pl.when(p_id == 0)
        def _():
            for a in range(n):
                stages[a][...] = shard_refs[a][...].astype(BF16)
                keep[a].start()
            gather.start()

        @pl.when(p_id == 2)
        def _():
            gather.pass_on()

        masks = _head_masks()

        def scores(b, half):
            r0 = pl.multiple_of(b * Q_BLOCK, Q_BLOCK)
            q2 = _stack_heads(q_ref[pl.ds(r0 + Z_PAD, Q_BLOCK), :], masks)
            s_scr[half] = _dot(q2, k_ref[pl.ds(r0, K_SPAN), :], NT)

        def finish(b, half):
            r0 = pl.multiple_of(b * Q_BLOCK, Q_BLOCK)
            v2 = _stack_heads(v_ref[pl.ds(r0, K_SPAN), :], masks)
            p = [st.astype(BF16) for st in _softmax_strips(s_scr.at[half], bt_ref, b)]
            o_ref[pl.ds(r0, Q_BLOCK), :] = _dot(_side_by_side_strips(p), v2)

        def two_blocks(i, carry):
            b = 2 * i
            scores(b + 1, 1)
            finish(b, 0)
            scores(jnp.minimum(b + 2, nb - 1), 0)
            finish(b + 1, 1)
            return carry

        scores(0, 0)
        lax.fori_loop(0, nb // 2, two_blocks, 0)

        @pl.when(p_id == pairs - 1)
        def _():
            gather.finish()
            for cp in keep:
                cp.wait()

    hbm = pl.BlockSpec(memory_space=pl.ANY)
    return pl.pallas_call(
        body, name="attn_fwd",
        grid=(pairs,),
        in_specs=_attn_specs(s + Z_PAD) + [pl.BlockSpec(a.shape, lambda p: (0, 0)) for a in shards],
        out_specs=(pl.BlockSpec((s, 128), lambda p: (0, p)),) + (hbm,) * n,
        out_shape=(jax.ShapeDtypeStruct((s, D_A), F32),)
        + tuple(jax.ShapeDtypeStruct((N_DEV,) + a.shape, BF16) for a in shards),
        scratch_shapes=[pltpu.VMEM(a.shape, BF16) for a in shards]
        + [pltpu.VMEM((2, 2 * Q_BLOCK, K_SPAN), F32),
           pltpu.SemaphoreType.DMA((n, N_DEV - 1)), pltpu.SemaphoreType.DMA((n, N_DEV - 1)),
           pltpu.SemaphoreType.DMA((n,))],
        compiler_params=_params(48),
    )(qkv, qkv, qkv, bias_table, *shards)


def _fill_slab(stage, rows):
    stage[...] = jnp.zeros(stage.shape, F32)
    for row, ref in rows:
        r, c = ref.shape
        if c > D_MODEL:
            for part in range(c // D_MODEL):
                stage[row + part:row + part + 1, :] = ref[:, part * D_MODEL:(part + 1) * D_MODEL]
        else:
            stage[row:row + r, :c] = ref[...]


def _attn_bwd(qkv, bias_table, d_out, to_chip, small):
    s = qkv.shape[0] - Z_PAD
    nb = s // Q_BLOCK
    n = len(to_chip)
    pairs = N_HEADS // 2
    ws_shape = small[-1].shape

    def body(*refs):
        q_ref, k_ref, v_ref, bt_ref, do_ref = refs[:5]
        to_chip_refs = refs[5:5 + n]
        bg_ref, lng_ref, lnb_ref, fg_ref, loss_ref, bs_ref, ws_ref = refs[5 + n:12 + n]
        dqkv_ref, db_ref = refs[12 + n:14 + n]
        from_chip_refs = refs[14 + n:14 + 2 * n]
        slab_land, ws_land = refs[14 + 2 * n:16 + 2 * n]
        (dk_acc, dv_acc, s_scr, dp_scr, slab_stage, ws_stage, send_sems, recv_sems, gather_send, gather_recv,
         keep_sems) = refs[16 + 2 * n:]
        p_id = pl.program_id(0)
        me = _flat_id(_my_pos())
        gather = _SlotGather([slab_land, ws_land], gather_send, gather_recv, own=[slab_stage, ws_stage])
        keep = [pltpu.make_async_copy(stage, land.at[me], keep_sems.at[k]) for k, (stage, land) in enumerate(
            ((slab_stage, slab_land), (ws_stage, ws_land)))]

        @pl.when(p_id == 0)
        def _():
            _fill_slab(slab_stage, ((ROW_B_GATE, bg_ref), (ROW_LN_G, lng_ref), (ROW_LN_B, lnb_ref),
                                    (ROW_FINAL_G, fg_ref), (ROW_LOSS, loss_ref)))
            eye = (lax.broadcasted_iota(jnp.int32, (SGU_CHUNK, SGU_CHUNK), 0)
                   == lax.broadcasted_iota(jnp.int32, (SGU_CHUNK, SGU_CHUNK), 1))
            for g in range(N_GROUPS):
                row = jnp.sum(jnp.where(eye, bs_ref[g], 0.0), axis=0, keepdims=True)
                slab_stage[ROW_B_S + g:ROW_B_S + g + 1, :SGU_CHUNK] = row
            ws_stage[...] = ws_ref[...]
            for cp in keep:
                cp.start()
            gather.start()
            for cp in _owner_copies(to_chip_refs, from_chip_refs, send_sems, recv_sems):
                cp.start()

        @pl.when(p_id == 2)
        def _():
            gather.pass_on()

        dk_acc[...] = jnp.zeros(dk_acc.shape, F32)
        dv_acc[...] = jnp.zeros(dv_acc.shape, F32)
        db_ref[...] = jnp.zeros(db_ref.shape, F32)
        masks = _head_masks()

        def operands(b):
            r0 = pl.multiple_of(b * Q_BLOCK, Q_BLOCK)
            q2 = _stack_heads(q_ref[pl.ds(r0 + Z_PAD, Q_BLOCK), :], masks)
            do2 = _stack_heads(do_ref[pl.ds(r0, Q_BLOCK), :], masks)
            return r0, q2, do2, k_ref[pl.ds(r0, K_SPAN), :]

        def ahead(b, half):
            r0, q2, do2, kcat = operands(b)
            s_scr[half] = _dot(q2, kcat, NT)
            dp_scr[half] = _dot(do2, v_ref[pl.ds(r0, K_SPAN), :], NT)

        def finish(b, half):
            r0, q2, do2, kcat = operands(b)
            p_strips, ds_strips = [], []
            for t, p in enumerate(_softmax_strips(s_scr.at[half], bt_ref, b)):
                hh, r = divmod(t * STRIP, Q_BLOCK)
                dp_t = dp_scr[half, t * STRIP:(t + 1) * STRIP, :]
                ds = p * (dp_t - jnp.sum(p * dp_t, axis=-1, keepdims=True))
                db_ref[hh, r:r + STRIP, :] += ds
                p_strips.append(p.astype(BF16))
                ds_strips.append(ds.astype(BF16))
            dq = _dot(_side_by_side_strips(ds_strips), _stack_heads(kcat, masks))
            dqkv_ref[0, pl.ds(r0, Q_BLOCK), :] = (dq * Q_SCALE).astype(BF16)
            dk_acc[pl.ds(r0, K_SPAN), :] += _dot(jnp.concatenate(ds_strips, axis=0), q2, TN)
            dv_acc[pl.ds(r0, K_SPAN), :] += _dot(jnp.concatenate(p_strips, axis=0), do2, TN)

        def two_blocks(i, carry):
            b = 2 * i
            ahead(b + 1, 1)
            finish(b, 0)
            ahead(jnp.minimum(b + 2, nb - 1), 0)
            finish(b + 1, 1)
            return carry

        ahead(0, 0)
        lax.fori_loop(0, nb // 2, two_blocks, 0)
        dqkv_ref[1] = dk_acc[Z_PAD:, :].astype(BF16)
        dqkv_ref[2] = dv_acc[Z_PAD:, :].astype(BF16)

        @pl.when(p_id == pairs - 1)
        def _():
            gather.finish()
            for cp in keep:
                cp.wait()
            for cp in _owner_copies(to_chip_refs, from_chip_refs, send_sems, recv_sems):
                cp.wait_recv()
                cp.wait_send()

    hbm = pl.BlockSpec(memory_space=pl.ANY)
    lands = ((N_DEV, SLAB_ROWS, D_MODEL), (N_DEV,) + ws_shape)
    return pl.pallas_call(
        body, name="attn_bwd",
        grid=(pairs,),
        in_specs=_attn_specs(s + Z_PAD) + [pl.BlockSpec((s, 128), lambda p: (0, p))] + [hbm] * n
        + [pl.BlockSpec(a.shape, functools.partial(lambda nd, p: (0,) * nd, a.ndim)) for a in small],
        out_specs=(pl.BlockSpec((3, s, 128), lambda p: (0, 0, p)),
                   pl.BlockSpec((2, Q_BLOCK, K_SPAN), lambda p: (p, 0, 0))) + (hbm,) * (n + 2),
        out_shape=(jax.ShapeDtypeStruct((3, s, D_A), BF16),
                   jax.ShapeDtypeStruct((N_HEADS, Q_BLOCK, K_SPAN), F32))
        + tuple(jax.ShapeDtypeStruct(t.shape, t.dtype) for t in to_chip)
        + tuple(jax.ShapeDtypeStruct(shape, F32) for shape in lands),
        scratch_shapes=[pltpu.VMEM((s + Z_PAD, 128), F32), pltpu.VMEM((s + Z_PAD, 128), F32),
                        pltpu.VMEM((2, 2 * Q_BLOCK, K_SPAN), F32), pltpu.VMEM((2, 2 * Q_BLOCK, K_SPAN), F32)]
        + [pltpu.VMEM(shape[1:], F32) for shape in lands]
        + [pltpu.SemaphoreType.DMA((n, 3)), pltpu.SemaphoreType.DMA((n, 3)),
           pltpu.SemaphoreType.DMA((2, N_DEV - 1)), pltpu.SemaphoreType.DMA((2, N_DEV - 1)),
           pltpu.SemaphoreType.DMA((2,))],
        compiler_params=_params(56),
    )(qkv, qkv, qkv, bias_table, d_out, *to_chip, *small)


def _mid_fwd_bwd(x, target, attn_out, z, w_pa, w_pb, w_out, b_gate, ln_g, ln_b, w_s, b_s, final_g):
    s = x.shape[0]
    tm = TOKEN_TILE
    nt = s // tm

    def body(x_ref, t_ref, oa_ref, ga_ref, ub_ref, vb_ref, gb_ref, ta0_ref, ta1_ref, tb0_ref, tb1_ref,
             wpa_hbm, wpb_hbm, wout_hbm, bg_ref, lng_ref, lnb_ref, ws_ref, bs_ref, fg_ref,
             dx2_ref, doa_ref, dz_ref, dwout_hbm, dwpa_hbm, dwpb_hbm, dbg_ref, dfg_ref, dlng_ref, dlnb_ref, dws_ref,
             dbs_ref, loss_ref,
             wpa, wpb, wout, wmix, acc_out, acc_pa, acc_pb, sem):
        i = pl.program_id(0)

        @pl.when(i == 0)
        def _():
            loads = [pltpu.make_async_copy(src, dst, sem.at[n])
                     for n, (src, dst) in enumerate(((wpa_hbm, wpa), (wpb_hbm, wpb), (wout_hbm, wout)))]
            for cp in loads:
                cp.start()
            t_idx = lax.broadcasted_iota(jnp.int32, (SGU_CHUNK, SGU_CHUNK), 0)
            s_idx = lax.broadcasted_iota(jnp.int32, (SGU_CHUNK, SGU_CHUNK), 1)
            for g in range(N_GROUPS):
                wmix[g] = jnp.where(s_idx <= t_idx, ws_ref[g], 0.0).astype(BF16)
            for ref in (acc_out, acc_pa, acc_pb, dbg_ref, dfg_ref, dlng_ref, dlnb_ref, dws_ref, dbs_ref, loss_ref):
                ref[...] = jnp.zeros(ref.shape, F32)
            for cp in loads:
                cp.wait()

        def tile_fwd_bwd(rows):
            g_a = ga_ref[rows, :].astype(F32)
            u_b = ub_ref[rows, :].astype(F32)
            v_b = vb_ref[rows, :].astype(F32)
            g_b = gb_ref[rows, :].astype(F32)
            bg = bg_ref[...]
            sg_a = _sigmoid(g_a)
            silu_a = g_a * sg_a
            o_a = oa_ref[rows, :]
            y_a = (o_a * silu_a).astype(BF16)
            ug, dgelu_u = _gelu_and_grad(u_b)
            vg, dgelu_v = _gelu_and_grad(v_b)
            mu = jnp.mean(vg, axis=-1, keepdims=True)
            vc = vg - mu
            rstd = lax.rsqrt(jnp.mean(vc * vc, axis=-1, keepdims=True) + EPS)
            vhat = vc * rstd
            lng = lng_ref[...]
            vn = (vhat * lng + lnb_ref[...]).astype(BF16)
            sg_b = _sigmoid(g_b)
            silu_b = g_b * sg_b
            subs = [slice(n * SGU_CHUNK, (n + 1) * SGU_CHUNK) for n in range(tm // SGU_CHUNK)]
            mixed = jnp.concatenate([jnp.concatenate(
                [_dot(wmix[g], vn[sub, g * 128:(g + 1) * 128]) + bs_ref[g] for g in range(N_GROUPS)], axis=1)
                for sub in subs], axis=0)
            um = ug * mixed
            y_b = (um * silu_b).astype(BF16)
            gate_a = _sigmoid(jnp.concatenate([ta0_ref[rows, :], ta1_ref[rows, :]], axis=1).astype(F32)
                              + bg[:, :D_MODEL])
            gate_b = _sigmoid(jnp.concatenate([tb0_ref[rows, :], tb1_ref[rows, :]], axis=1).astype(F32)
                              + bg[:, D_MODEL:])
            p_a = _dot(y_a, wpa[...])
            p_b = _dot(y_b, wpb[...])
            merged = (gate_a * p_a + gate_b * p_b).astype(BF16)
            x2 = x_ref[rows, :] + _dot(merged, wout[...])
            r2 = lax.rsqrt(jnp.mean(x2 * x2, axis=-1, keepdims=True) + EPS)
            xh = x2 * r2
            fg = fg_ref[...]
            err = xh * fg - t_ref[rows, :]
            loss_ref[...] += jnp.sum(jnp.sum(err * err, axis=-1, keepdims=True), axis=0, keepdims=True) * (0.5 / D_MODEL)
            dy = err * (1.0 / D_MODEL)
            dfg_ref[...] += jnp.sum(dy * xh, axis=0, keepdims=True)
            gy = dy * fg
            dx2 = r2 * (gy - xh * jnp.mean(gy * xh, axis=-1, keepdims=True))
            dx2_ref[rows, :] = dx2
            dx2b = dx2.astype(BF16)
            dmerged = _dot(dx2b, wout[...], NT)
            acc_out[...] += _dot(merged, dx2b, TN)
            dp_a = dmerged * gate_a
            dp_b = dmerged * gate_b
            dgate_a = dp_a * p_a * (1.0 - gate_a)
            dgate_b = dp_b * p_b * (1.0 - gate_b)
            dbg_ref[:, :D_MODEL] += jnp.sum(dgate_a, axis=0, keepdims=True)
            dbg_ref[:, D_MODEL:] += jnp.sum(dgate_b, axis=0, keepdims=True)
            dz_ref[rows, 2048:3072] = dgate_a.astype(BF16)
            dz_ref[rows, 3072:4096] = dgate_b.astype(BF16)
            dp_ab = dp_a.astype(BF16)
            dp_bb = dp_b.astype(BF16)
            dy_a = _dot(dp_ab, wpa[...], NT)
            dy_b = _dot(dp_bb, wpb[...], NT)
            acc_pa[...] += _dot(y_a, dp_ab, TN)
            acc_pb[...] += _dot(y_b, dp_bb, TN)
            doa_ref[rows, :] = (dy_a * silu_a).astype(BF16)
            dz_ref[rows, 0:512] = (dy_a * o_a * (sg_a * (1.0 + g_a * (1.0 - sg_a)))).astype(BF16)
            dz_ref[rows, 1536:2048] = (dy_b * um * (sg_b * (1.0 + g_b * (1.0 - sg_b)))).astype(BF16)
            dys = dy_b * silu_b
            dz_ref[rows, 512:1024] = (dys * mixed * dgelu_u).astype(BF16)
            dmixed = dys * ug
            dmb = dmixed.astype(BF16)
            dvn_rows = []
            for sub in subs:
                dvn_parts = []
                for g in range(N_GROUPS):
                    cols = slice(g * 128, (g + 1) * 128)
                    dws_ref[g] += _dot(dmb[sub, cols], vn[sub, cols], NT)
                    dbs_ref[g] += jnp.sum(dmixed[sub, cols], axis=-1, keepdims=True)
                    dvn_parts.append(_dot(wmix[g], dmb[sub, cols], TN))
                dvn_rows.append(jnp.concatenate(dvn_parts, axis=1))
            dvn = jnp.concatenate(dvn_rows, axis=0)
            dlng_ref[...] += jnp.sum(dvn * vhat, axis=0, keepdims=True)
            dlnb_ref[...] += jnp.sum(dvn, axis=0, keepdims=True)
            dvh = dvn * lng
            dvg = rstd * (dvh - jnp.mean(dvh, axis=-1, keepdims=True)
                          - vhat * jnp.mean(dvh * vhat, axis=-1, keepdims=True))
            dz_ref[rows, 1024:1536] = (dvg * dgelu_v).astype(BF16)

        tile_fwd_bwd(slice(0, tm))

        @pl.when(i == nt - 1)
        def _():
            t_idx = lax.broadcasted_iota(jnp.int32, (SGU_CHUNK, SGU_CHUNK), 0)
            s_idx = lax.broadcasted_iota(jnp.int32, (SGU_CHUNK, SGU_CHUNK), 1)
            for g in range(N_GROUPS):
                dws_ref[g] = jnp.where(s_idx <= t_idx, dws_ref[g], 0.0)
            stores = [pltpu.make_async_copy(src, dst, sem.at[n])
                      for n, (src, dst) in enumerate(((acc_out, dwout_hbm), (acc_pa, dwpa_hbm), (acc_pb, dwpb_hbm)))]
            for cp in stores:
                cp.start()
            for cp in stores:
                cp.wait()

    tile = lambda w: pl.BlockSpec((tm, w), lambda i: (i, 0))
    whole = lambda shape: pl.BlockSpec(shape, lambda i: (0,) * len(shape))
    hbm = pl.BlockSpec(memory_space=pl.ANY)
    return pl.pallas_call(
        body, name="mid_fwd_bwd",
        grid=(nt,),
        in_specs=[tile(D_MODEL), tile(D_MODEL), tile(D_A)]
        + [pl.BlockSpec((tm, COL_BLOCK), functools.partial(lambda c, i: (i + Z_PAD // tm, c), c))
           for c in range(3, N_COL_BLOCKS)]
        + [hbm, hbm, hbm,
                  whole((1, 2 * D_MODEL)), whole((1, D_B)), whole((1, D_B)),
                  whole((N_GROUPS, SGU_CHUNK, SGU_CHUNK)), whole((N_GROUPS, SGU_CHUNK, 1)), whole((1, D_MODEL))],
        out_specs=(tile(D_MODEL), tile(D_A), tile(REST), hbm, hbm, hbm,
                   whole((1, 2 * D_MODEL)), whole((1, D_MODEL)), whole((1, D_B)), whole((1, D_B)),
                   whole((N_GROUPS, SGU_CHUNK, SGU_CHUNK)), whole((N_GROUPS, SGU_CHUNK, 1)), whole((1, 1))),
        out_shape=(jax.ShapeDtypeStruct((s, D_MODEL), F32), jax.ShapeDtypeStruct((s, D_A), BF16),
                   jax.ShapeDtypeStruct((s, REST), BF16),
                   jax.ShapeDtypeStruct((D_MODEL, D_MODEL), F32), jax.ShapeDtypeStruct((D_A, D_MODEL), F32),
                   jax.ShapeDtypeStruct((D_B, D_MODEL), F32),
                   jax.ShapeDtypeStruct((1, 2 * D_MODEL), F32), jax.ShapeDtypeStruct((1, D_MODEL), F32),
                   jax.ShapeDtypeStruct((1, D_B), F32), jax.ShapeDtypeStruct((1, D_B), F32),
                   jax.ShapeDtypeStruct((N_GROUPS, SGU_CHUNK, SGU_CHUNK), F32),
                   jax.ShapeDtypeStruct((N_GROUPS, SGU_CHUNK, 1), F32), jax.ShapeDtypeStruct((1, 1), F32)),
        scratch_shapes=[pltpu.VMEM((D_A, D_MODEL), BF16), pltpu.VMEM((D_B, D_MODEL), BF16),
                        pltpu.VMEM((D_MODEL, D_MODEL), BF16), pltpu.VMEM((N_GROUPS, SGU_CHUNK, SGU_CHUNK), BF16),
                        pltpu.VMEM((D_MODEL, D_MODEL), F32), pltpu.VMEM((D_A, D_MODEL), F32),
                        pltpu.VMEM((D_B, D_MODEL), F32),
                        pltpu.SemaphoreType.DMA((3,))],
        compiler_params=_params(56),
    )(x, target, attn_out, *([z] * (N_COL_BLOCKS - 3)), w_pa, w_pb, w_out, b_gate, ln_g, ln_b, w_s, b_s, final_g)


def _proj_bwd_x(dqkv, drest, x, dx2, norm_g, w_in_t, to_chip, small):
    s = x.shape[0]
    tm = 512 if s % 512 == 0 else TOKEN_TILE
    nt = s // tm
    rows = to_chip.shape[1]
    half = D_MODEL // 2
    left, right = slice(0, half), slice(half, D_MODEL)

    def body(dqkv_ref, dr_ref, x_ref, dx2_ref, g_ref, w_hbm, tc_hbm, ng_ref, rel_ref,
             dx_ref, fc_ref, slab_land,
             w, tc_ref, slab_stage, via_x, via_y, mine, out_x, out_y, sem, send_sems, recv_sems,
             gather_send, gather_recv, keep_sems):
        i = pl.program_id(0)
        x_, y_, c_ = _my_pos()
        me = _flat_id((x_, y_, c_))
        xn, yn = (1 - x_, y_, c_), (x_, 1 - y_, c_)
        gather = _SlotGather([slab_land], gather_send, gather_recv, own=[slab_stage])
        keep = [pltpu.make_async_copy(slab_stage, slab_land.at[me], keep_sems.at[0])]

        def copy(k, src, dst, to):
            return pltpu.make_async_remote_copy(src_ref=src, dst_ref=dst, send_sem=send_sems.at[k],
                                                recv_sem=recv_sems.at[k], device_id=to, device_id_type=MESH)

        first = [copy(0, tc_ref.at[0, :, left], fc_ref.at[0, :, left], xn), copy(1, tc_ref.at[2, :, left], via_x, xn),
                 copy(2, tc_ref.at[1, :, right], fc_ref.at[1, :, right], yn), copy(3, tc_ref.at[2, :, right], via_y, yn)]
        second = [copy(4, out_y, fc_ref.at[1, :, left], yn), copy(5, out_x, fc_ref.at[0, :, right], xn)]

        def add_and_send(arrival, landed, own_half, stage, onward):
            load = pltpu.make_async_copy(own_half, mine, sem)
            load.start()
            arrival.wait_recv()
            load.wait()
            stage[...] = (mine[...].astype(F32) + landed[...].astype(F32)).astype(BF16)
            onward.start()

        @pl.when(i == 0)
        def _():
            cp = pltpu.make_async_copy(w_hbm, w, sem)
            cp.start()
            _fill_slab(slab_stage, ((ROW_NORM_G, ng_ref), (ROW_REL, rel_ref)))
            for cp_keep in keep:
                cp_keep.start()
            gather.start()
            stage_in = pltpu.make_async_copy(tc_hbm, tc_ref, keep_sems.at[2])
            stage_in.start()
            stage_in.wait()
            for rc in first:
                rc.start()
            cp.wait()

        @pl.when(i == (5 * nt) // 8)
        def _():
            gather.pass_on()
            add_and_send(first[1], via_x, tc_ref.at[1, :, left], out_y, second[0])
            add_and_send(first[3], via_y, tc_ref.at[0, :, right], out_x, second[1])

        dh = None
        for c in range(N_COL_BLOCKS):
            dz = dqkv_ref[c] if c < 3 else dr_ref[:, (c - 3) * COL_BLOCK:(c - 2) * COL_BLOCK]
            part = _dot(dz, w[c * COL_BLOCK:(c + 1) * COL_BLOCK, :])
            dh = part if dh is None else dh + part
        xf = x_ref[...]
        r = lax.rsqrt(jnp.mean(xf * xf, axis=-1, keepdims=True) + EPS)
        xn = xf * r
        gh = dh * g_ref[...]
        dx_ref[...] = r * (gh - xn * jnp.mean(gh * xn, axis=-1, keepdims=True)) + dx2_ref[...]

        @pl.when(i == nt - 1)
        def _():
            gather.finish()
            for cp_keep in keep:
                cp_keep.wait()
            for k in (0, 2, 4, 5):
                (first + second)[k].wait_recv()
            for rc in first + second:
                rc.wait_send()

    hbm = pl.BlockSpec(memory_space=pl.ANY)
    whole = lambda a: pl.BlockSpec(a.shape, lambda i: (0,) * a.ndim)
    return pl.pallas_call(
        body, name="proj_bwd_x",
        grid=(nt,),
        in_specs=[pl.BlockSpec((3, tm, D_A), lambda i: (0, i, 0)),
                  pl.BlockSpec((tm, REST), lambda i: (i, 0)),
                  pl.BlockSpec((tm, D_MODEL), lambda i: (i, 0)),
                  pl.BlockSpec((tm, D_MODEL), lambda i: (i, 0)),
                  pl.BlockSpec((1, D_MODEL), lambda i: (0, 0)),
                  hbm, hbm] + [whole(a) for a in small],
        out_specs=(pl.BlockSpec((tm, D_MODEL), lambda i: (i, 0)), hbm, hbm),
        out_shape=(jax.ShapeDtypeStruct((s, D_MODEL), F32), jax.ShapeDtypeStruct((2, rows, D_MODEL), BF16),
                   jax.ShapeDtypeStruct((N_DEV, LATE_SLAB_ROWS, D_MODEL), F32)),
        scratch_shapes=[pltpu.VMEM((D_IN, D_MODEL), BF16), pltpu.VMEM(to_chip.shape, BF16),
                        pltpu.VMEM((LATE_SLAB_ROWS, D_MODEL), F32)]
        + [pltpu.VMEM((rows, half), BF16)] * 5
        + [pltpu.SemaphoreType.DMA, pltpu.SemaphoreType.DMA((6,)), pltpu.SemaphoreType.DMA((6,)),
           pltpu.SemaphoreType.DMA((1, N_DEV - 1)), pltpu.SemaphoreType.DMA((1, N_DEV - 1)),
           pltpu.SemaphoreType.DMA((3,))],
        compiler_params=_params(56),
    )(dqkv, drest, x, dx2, norm_g, w_in_t, to_chip, *small)


def _proj_bwd_w(xn, dqkv, drest, norm_g, w_in_t):
    s = xn.shape[0]
    tk = s
    nk = s // tk

    def body(xn_ref, dqkv_ref, dr_ref, g_ref, w_ref, o_ref, dg_ref, acc):
        j = pl.program_id(0)
        i = pl.program_id(1)

        @pl.when((j == 0) & (i == 0))
        def _():
            dg_ref[...] = jnp.zeros(dg_ref.shape, F32)

        @pl.when(i == 0)
        def _():
            acc[...] = jnp.zeros(acc.shape, F32)

        @pl.when(j < 3)
        def _():
            acc[...] += _dot(dqkv_ref[...], xn_ref[...], TN)

        @pl.when(j >= 3)
        def _():
            acc[...] += _dot(dr_ref[...], xn_ref[...], TN)

        @pl.when(i == nk - 1)
        def _():
            m = acc[...]
            o_ref[...] = (m * g_ref[...]).astype(BF16)
            dg_ref[...] += jnp.sum(m * w_ref[...].astype(F32), axis=0, keepdims=True)

    return pl.pallas_call(
        body, name="proj_bwd_w",
        grid=(N_COL_BLOCKS, nk),
        in_specs=[pl.BlockSpec((tk, D_MODEL), lambda j, i: (i, 0)),
                  pl.BlockSpec((None, tk, COL_BLOCK),
                               lambda j, i: (jnp.minimum(j, 2), jnp.where(j < 3, i, nk - 1), 0)),
                  pl.BlockSpec((tk, COL_BLOCK),
                               lambda j, i: (jnp.where(j >= 3, i, 0), jnp.maximum(j - 3, 0))),
                  pl.BlockSpec((1, D_MODEL), lambda j, i: (0, 0)),
                  pl.BlockSpec((COL_BLOCK, D_MODEL), lambda j, i: (j, 0))],
        out_specs=(pl.BlockSpec((COL_BLOCK, D_MODEL), lambda j, i: (j, 0)),
                   pl.BlockSpec((1, D_MODEL), lambda j, i: (0, 0))),
        out_shape=(jax.ShapeDtypeStruct((D_IN, D_MODEL), BF16), jax.ShapeDtypeStruct((1, D_MODEL), F32)),
        scratch_shapes=[pltpu.VMEM((COL_BLOCK, D_MODEL), F32)],
        compiler_params=_params(56),
    )(xn, dqkv, drest, norm_g, w_in_t)


def _adamw_math(w, g, m, v):
    c1 = 1.0 - ADAM_B1 ** ADAM_STEP
    c2 = 1.0 - ADAM_B2 ** ADAM_STEP
    nm = ADAM_B1 * m + (1.0 - ADAM_B1) * g
    nv = ADAM_B2 * v + (1.0 - ADAM_B2) * (g * g)
    return -ADAM_LR * ((nm / c1) / (jnp.sqrt(nv / c2) + ADAM_EPS) + ADAM_WD * w), nm, nv


def _adamw(name, w, g, m, v, from_chip):
    rows, cols = w.shape
    tr = rows if rows * cols <= 512 * 1024 else next(t for t in range(256, 7, -8) if rows % t == 0)

    def body(w_ref, g_ref, m_ref, v_ref, t_ref, g_out, d_ref, nm_ref, nv_ref):
        gg = g_ref[...]
        for j in range(from_chip.shape[0]):
            gg = gg + t_ref[j].astype(F32)
        g_out[...] = gg
        d_ref[...], nm_ref[...], nv_ref[...] = _adamw_math(w_ref[...], gg, m_ref[...], v_ref[...])

    spec = pl.BlockSpec((tr, cols), lambda i: (i, 0))
    shape = jax.ShapeDtypeStruct((rows, cols), F32)
    return pl.pallas_call(
        body, name=name,
        grid=(rows // tr,),
        in_specs=[spec] * 4 + [pl.BlockSpec((from_chip.shape[0], tr, cols), lambda i: (0, i, 0))],
        out_specs=(spec,) * 4, out_shape=(shape,) * 4,
        compiler_params=_params(32),
    )(w, g, m, v, from_chip)


_SMALL = (("norm_g", (1, D_MODEL)), ("b_gate", (1, 2 * D_MODEL)), ("rel_bias", (N_HEADS, N_REL)),
          ("sgu_ln_g", (1, D_B)), ("sgu_ln_b", (1, D_B)), ("w_s", (N_GROUPS * SGU_CHUNK, SGU_CHUNK)),
          ("b_s", (N_GROUPS, SGU_CHUNK)), ("final_g", (1, D_MODEL)))


def _adamw_small(slabs, ws_all, late_slabs, weights, moments_m, moments_v):
    k = len(_SMALL)

    def total(ref):
        acc = ref[0]
        for d in range(1, N_DEV):
            acc = acc + ref[d]
        return acc

    def body(*refs):
        slab_ref, ws_ref, late_ref = refs[:3]
        w_refs, m_refs, v_refs = refs[3:3 + k], refs[3 + k:3 + 2 * k], refs[3 + 2 * k:3 + 3 * k]
        outs = refs[3 + 3 * k:]
        slab, late = total(slab_ref), total(late_ref)
        grads = {
            "norm_g": late[ROW_NORM_G:ROW_NORM_G + 1, :],
            "b_gate": jnp.concatenate([slab[ROW_B_GATE:ROW_B_GATE + 1, :], slab[ROW_B_GATE + 1:ROW_B_GATE + 2, :]], axis=1),
            "rel_bias": late[ROW_REL:ROW_REL + N_HEADS, :N_REL],
            "sgu_ln_g": slab[ROW_LN_G:ROW_LN_G + 1, :D_B],
            "sgu_ln_b": slab[ROW_LN_B:ROW_LN_B + 1, :D_B],
            "w_s": total(ws_ref),
            "b_s": slab[ROW_B_S:ROW_B_S + N_GROUPS, :SGU_CHUNK],
            "final_g": slab[ROW_FINAL_G:ROW_FINAL_G + 1, :],
        }
        for n, (name, _) in enumerate(_SMALL):
            g = grads[name]
            outs[n][...] = g
            outs[k + n][...], outs[2 * k + n][...], outs[3 * k + n][...] = _adamw_math(
                w_refs[n][...], g, m_refs[n][...], v_refs[n][...])
        outs[4 * k][...] = slab[ROW_LOSS:ROW_LOSS + 1, :1]

    vmem = pl.BlockSpec(memory_space=pltpu.VMEM)
    shapes = tuple(jax.ShapeDtypeStruct(shape, F32) for _, shape in _SMALL)
    return pl.pallas_call(
        body, name="adamw_small",
        out_shape=shapes * 4 + (jax.ShapeDtypeStruct((1, 1), F32),),
        in_specs=[vmem] * (3 + 3 * k), out_specs=tuple([vmem] * (4 * k + 1)),
        compiler_params=_params(16),
    )(slabs, ws_all, late_slabs, *weights, *moments_m, *moments_v)


def _pad_rel(a):
    return jnp.pad(a.reshape(N_HEADS, N_REL), ((0, 0), (0, N_REL_PAD - N_REL)))


def kernel(x, norm_g, w_in, b_gate, rel_bias, sgu_ln_g, sgu_ln_b, w_s, b_s, w_pa, w_pb, w_out, final_g, loss_target, m_norm_g, m_w_in, m_b_gate, m_rel_bias, m_sgu_ln_g, m_sgu_ln_b, m_w_s, m_b_s, m_w_pa, m_w_pb, m_w_out, m_final_g, v_norm_g, v_w_in, v_b_gate, v_rel_bias, v_sgu_ln_g, v_sgu_ln_b, v_w_s, v_b_s, v_w_pa, v_w_pb, v_w_out, v_final_g):
    s = x.shape[1]
    xs = x.reshape(s, D_MODEL)
    tgt = loss_target.reshape(s, D_MODEL)

    bias_table = _bias_table(_pad_rel(rel_bias))
    w_in_t = jnp.swapaxes(w_in[0], 0, 1)
    qkv, x_norm, w_in_t_full = _gather_proj_fwd(xs, norm_g, w_in_t)
    attn_out, g_pa, g_pb, g_out = _attn_fwd(qkv, bias_table, (w_pa[0], w_pb[0], w_out[0]))
    w_pa_full = jnp.transpose(g_pa, (1, 0, 2)).reshape(D_A, D_MODEL)
    w_pb_full = jnp.transpose(g_pb, (1, 0, 2)).reshape(D_B, D_MODEL)
    w_out_full = g_out.reshape(D_MODEL, D_MODEL)

    (dx2, d_attn, drest, dw_out, dw_pa, dw_pb, d_bgate, d_fg, d_lng, d_lnb, d_ws, d_bs, loss_part) = _mid_fwd_bwd(
        xs, tgt, attn_out, qkv, w_pa_full, w_pb_full, w_out_full, b_gate, sgu_ln_g, sgu_ln_b, w_s[0],
        b_s.reshape(N_GROUPS, SGU_CHUNK, 1), final_g.reshape(1, D_MODEL))

    own_pa, own_pb, own_out, tc_pa, tc_pb, tc_out = _reduce_chip(
        "reduce_chip_proj", (dw_pa, dw_pb, dw_out), (1, 1, 0))
    dqkv, dbias, fc_pa, fc_pb, fc_out, slabs, ws_all = _attn_bwd(
        qkv, bias_table, d_attn, (tc_pa, tc_pb, tc_out),
        (d_bgate, d_lng, d_lnb, d_fg, loss_part, d_bs, d_ws.reshape(N_GROUPS * SGU_CHUNK, SGU_CHUNK)))
    d_rel = _bias_grad(dbias)
    dw_in_t, d_ng = _proj_bwd_w(x_norm, dqkv, drest, norm_g, w_in_t_full)
    own_in, tc_in = _reduce_chip("reduce_chip_in", (dw_in_t,), (0,))
    grad_x, fc_in, late_slabs = _proj_bwd_x(dqkv, drest, xs, dx2, norm_g, w_in_t_full, tc_in, (d_ng, d_rel))
    big = {"w_in": tuple(jnp.swapaxes(t, 0, 1)[None] for t in _adamw(
        "adamw_w_in", w_in_t, own_in, jnp.swapaxes(m_w_in[0], 0, 1), jnp.swapaxes(v_w_in[0], 0, 1), fc_in))}
    for name, w, g, fc, m, v in (("w_pa", w_pa, own_pa, fc_pa, m_w_pa, v_w_pa),
                                 ("w_pb", w_pb, own_pb, fc_pb, m_w_pb, v_w_pb),
                                 ("w_out", w_out, own_out, fc_out, m_w_out, v_w_out)):
        big[name] = tuple(t[None] for t in _adamw("adamw_" + name, w[0], g, m[0], v[0], fc))

    as_2d = lambda leaves: [a.reshape(shape) for a, (_, shape) in zip(leaves, _SMALL)]
    small_out = _adamw_small(
        slabs, ws_all, late_slabs, as_2d((norm_g, b_gate, rel_bias, sgu_ln_g, sgu_ln_b, w_s, b_s, final_g)),
        as_2d((m_norm_g, m_b_gate, m_rel_bias, m_sgu_ln_g, m_sgu_ln_b, m_w_s, m_b_s, m_final_g)),
        as_2d((v_norm_g, v_b_gate, v_rel_bias, v_sgu_ln_g, v_sgu_ln_b, v_w_s, v_b_s, v_final_g)))
    small_index = {name: n for n, (name, _) in enumerate(_SMALL)}

    def leaf(kind, name, like):
        if name in big:
            return big[name][kind]
        return small_out[kind * len(_SMALL) + small_index[name]].reshape(like.shape)

    weights = (("norm_g", norm_g), ("w_in", w_in), ("b_gate", b_gate), ("rel_bias", rel_bias), ("sgu_ln_g", sgu_ln_g),
               ("sgu_ln_b", sgu_ln_b), ("w_s", w_s), ("b_s", b_s), ("w_pa", w_pa), ("w_pb", w_pb), ("w_out", w_out),
               ("final_g", final_g))
    outs = [small_out[-1].reshape(()), grad_x.reshape(x.shape)]
    for kind in range(4):
        outs.extend(leaf(kind, name, like) for name, like in weights)
    return tuple(outs)
```

```python
import functools
import math

import jax
import jax.numpy as jnp
from jax import lax
from jax.experimental import pallas as pl
from jax.experimental.pallas import tpu as pltpu

F32 = jnp.float32
BF16 = jnp.bfloat16
MESH = pl.DeviceIdType.MESH
N_DEV = 8

D_MODEL = 1024
D_A = 512
D_B = 512
D_IN = 5632
N_HEADS = 8
HEAD_DIM = 64
N_PREV = 8
REL_CLIP = 128
N_REL = 2 * REL_CLIP + 1
N_REL_PAD = 384
SGU_CHUNK = 128
N_GROUPS = 4
EPS = 1e-6
NEG_INF = -1e30
Q_SCALE = HEAD_DIM ** -0.5

Q_BLOCK = 256
K_SPAN = 768
Z_PAD = K_SPAN - Q_BLOCK
ROLL_W = 1024
COL_BLOCK = 512
N_COL_BLOCKS = D_IN // COL_BLOCK
REST = D_IN - 3 * D_A
TOKEN_TILE = 256
SHARD = D_MODEL // N_DEV

ADAM_LR = 0.001
ADAM_B1 = 0.9
ADAM_B2 = 0.999
ADAM_EPS = 1e-08
ADAM_WD = 0.01
ADAM_STEP = 10

GELU_C = math.sqrt(2.0 / math.pi)
GELU_A = 0.044715

NT = (((1,), (1,)), ((), ()))
TN = (((0,), (0,)), ((), ()))
HIGHEST = lax.Precision.HIGHEST


def _params(vmem_mb, **kw):
    return pltpu.CompilerParams(vmem_limit_bytes=vmem_mb * 1024 * 1024, **kw)


def _dot(a, b, dims=None):
    if dims is None:
        return jnp.dot(a, b, preferred_element_type=F32)
    return lax.dot_general(a, b, dims, preferred_element_type=F32)


def _sigmoid(x):
    return 0.5 * jnp.tanh(0.5 * x) + 0.5


def _gelu_and_grad(u):
    u2 = u * u
    t = jnp.tanh(GELU_C * (u + GELU_A * u * u2))
    half = 0.5 * (1.0 + t)
    g = u * half
    dg = half + 0.5 * u * (1.0 - t * t) * (GELU_C * (1.0 + 3.0 * GELU_A * u2))
    return g, dg


def _my_pos():
    return lax.axis_index("x"), lax.axis_index("y"), lax.axis_index("c")


def _flat_id(pos):
    return 4 * pos[0] + 2 * pos[1] + pos[2]


def _other_chips(pos):
    x, y, _ = pos
    return ((1 - x, y), (x, 1 - y), (1 - x, 1 - y))


class _SlotGather:
    def __init__(self, bufs, send_sems, recv_sems, own=None):
        self.bufs, self.send_sems, self.recv_sems = bufs, send_sems, recv_sems
        self.own = own if own is not None else [None] * len(bufs)
        x, y, c = _my_pos()
        self.c, self.me, self.sib = c, (x, y, c), (x, y, 1 - c)
        self.chips = _other_chips(self.me)

    def _copy(self, a, k, block, to):
        slot = _flat_id(block)
        src = self.own[a] if (k < 4 and self.own[a] is not None) else self.bufs[a].at[slot]
        return pltpu.make_async_remote_copy(
            src_ref=src, dst_ref=self.bufs[a].at[slot],
            send_sem=self.send_sems.at[a, k], recv_sem=self.recv_sems.at[a, k], device_id=to, device_id_type=MESH)

    def _own_sends(self):
        n = len(self.bufs)
        return ([self._copy(a, 1 + j, self.me, (*chip, self.c)) for j, chip in enumerate(self.chips) for a in range(n)]
                + [self._copy(a, 0, self.me, self.sib) for a in range(n)])

    def _passes(self):
        return [self._copy(a, 4 + j, (*chip, self.c), self.sib)
                for j, chip in enumerate(self.chips) for a in range(len(self.bufs))]

    def start(self):
        for cp in self._own_sends():
            cp.start()

    def pass_on(self):
        for j, chip in enumerate(self.chips):
            for a in range(len(self.bufs)):
                self._copy(a, 1 + j, (*chip, self.c), self.me).wait_recv()
                self._copy(a, 4 + j, (*chip, self.c), self.sib).start()

    def finish(self):
        for a in range(len(self.bufs)):
            self._copy(a, 0, self.sib, self.me).wait_recv()
            for j, chip in enumerate(self.chips):
                self._copy(a, 4 + j, (*chip, 1 - self.c), self.me).wait_recv()
        for cp in self._own_sends() + self._passes():
            cp.wait_send()


def _reduce_chip(name, parts, sharded_dim):
    n = len(parts)
    shapes = []
    for p, dim in zip(parts, sharded_dim):
        shape = list(p.shape)
        shape[dim] //= N_DEV
        shapes.append(tuple(shape))
    staged = [not (dim == 0 and p.dtype == BF16) for p, dim in zip(parts, sharded_dim)]

    def body(*refs):
        full, own, to_chip = refs[:n], refs[n:2 * n], refs[2 * n:3 * n]
        ins, from_sib = refs[3 * n:4 * n], refs[4 * n:5 * n]
        send_sems, recv_sems, load_sems = refs[5 * n:]
        x, y, c = _my_pos()
        sib = (x, y, 1 - c)
        chips = ((x, y),) + _other_chips((x, y, c))
        for a in range(n):
            rows, cols = shapes[a]
            for d in range(N_DEV if staged[a] else 0):
                if sharded_dim[a] == 0:
                    ins[a][d] = full[a][d * rows:(d + 1) * rows, :].astype(BF16)
                else:
                    ins[a][d] = full[a][:, d * cols:(d + 1) * cols].astype(BF16)

        def block(a, d):
            if staged[a]:
                return ins[a].at[d]
            rows = shapes[a][0]
            return full[a].at[pl.ds(pl.multiple_of(d * rows, 16), rows), :]

        def to_sibling(a, r):
            return pltpu.make_async_remote_copy(
                src_ref=block(a, _flat_id((*chips[r], 1 - c))), dst_ref=from_sib[a].at[r],
                send_sem=send_sems.at[a, r], recv_sem=recv_sems.at[a, r], device_id=sib, device_id_type=MESH)

        def load(a, r):
            return pltpu.make_async_copy(block(a, _flat_id((*chips[r], c))), ins[a].at[r], load_sems.at[a, r])

        sends = [to_sibling(a, r) for r in (1, 2, 3, 0) for a in range(n)]
        loads = [load(a, r) for r in (1, 2, 3, 0) for a in range(n) if not staged[a]]
        for cp in sends + loads:
            cp.start()
        for r in (1, 2, 3, 0):
            for a in range(n):
                if staged[a]:
                    mine = block(a, _flat_id((*chips[r], c)))[...]
                else:
                    load(a, r).wait()
                    mine = ins[a][r]
                to_sibling(a, r).wait_recv()
                both = mine.astype(F32) + from_sib[a][r].astype(F32)
                if r == 0:
                    own[a][...] = both
                else:
                    to_chip[a][r - 1] = both.astype(BF16)
        for cp in sends:
            cp.wait_send()

    vmem = pl.BlockSpec(memory_space=pltpu.VMEM)
    return pl.pallas_call(
        body, name=name,
        out_shape=tuple(jax.ShapeDtypeStruct(sh, F32) for sh in shapes)
        + tuple(jax.ShapeDtypeStruct((3,) + sh, BF16) for sh in shapes),
        in_specs=[vmem if st else pl.BlockSpec(memory_space=pl.ANY) for st in staged],
        out_specs=tuple([vmem] * (2 * n)),
        scratch_shapes=[pltpu.VMEM((N_DEV if st else 4,) + sh, BF16) for sh, st in zip(shapes, staged)]
        + [pltpu.VMEM((4,) + sh, BF16) for sh in shapes]
        + [pltpu.SemaphoreType.DMA((n, 4))] * 3,
        compiler_params=_params(56),
    )(*parts)


def _owner_copies(to_chip, from_chip, send_sems, recv_sems):
    x, y, c = _my_pos()
    return [pltpu.make_async_remote_copy(
        src_ref=to_chip[a].at[j], dst_ref=from_chip[a].at[j],
        send_sem=send_sems.at[a, j], recv_sem=recv_sems.at[a, j], device_id=(*chip, c), device_id_type=MESH)
        for a in range(len(to_chip)) for j, chip in enumerate(_other_chips((x, y, c)))]


ROW_B_GATE, ROW_LN_G, ROW_LN_B, ROW_FINAL_G, ROW_LOSS, ROW_B_S, SLAB_ROWS = 1, 3, 4, 5, 6, 16, 24
ROW_NORM_G, ROW_REL, LATE_SLAB_ROWS = 0, 8, 16


def _rel_index(e):
    lo, hi = Z_PAD - REL_CLIP, Z_PAD + REL_CLIP
    return jnp.where(e <= lo, 2 * REL_CLIP, jnp.where(e < hi, hi - e, jnp.where(e <= K_SPAN, 0, 2 * REL_CLIP)))


def _bias_table(rel_bias_pad):
    def body(rb_ref, bt_ref):
        c = lax.broadcasted_iota(jnp.int32, (N_REL_PAD, ROLL_W), 1)
        r = lax.broadcasted_iota(jnp.int32, (N_REL_PAD, ROLL_W), 0)
        pick = (r == _rel_index(c)).astype(F32)
        rows = jnp.dot(rb_ref[...], pick, precision=HIGHEST, preferred_element_type=F32)
        qc = lax.broadcasted_iota(jnp.int32, (Q_BLOCK, K_SPAN), 0) >> 6
        kc = lax.broadcasted_iota(jnp.int32, (Q_BLOCK, K_SPAN), 1) >> 6
        band = (kc >= qc) & (kc <= qc + N_PREV)
        for h in range(N_HEADS):
            t = jnp.broadcast_to(rows[h:h + 1, :], (Q_BLOCK, ROLL_W))
            t = pltpu.roll(t, 0, 1, stride=1, stride_axis=0)
            bt_ref[h] = jnp.where(band, t[:, :K_SPAN], NEG_INF)

    return pl.pallas_call(
        body, name="bias_table",
        out_shape=jax.ShapeDtypeStruct((N_HEADS, Q_BLOCK, K_SPAN), F32),
        compiler_params=_params(32),
    )(rel_bias_pad)


def _bias_grad(dbias):
    def body(a_ref, o_ref):
        rr = lax.broadcasted_iota(jnp.int32, (Q_BLOCK, Q_BLOCK), 0)
        cc = lax.broadcasted_iota(jnp.int32, (Q_BLOCK, Q_BLOCK), 1)
        flip = (rr + cc == Q_BLOCK - 1).astype(F32)
        c = lax.broadcasted_iota(jnp.int32, (ROLL_W, N_REL_PAD), 0)
        r = lax.broadcasted_iota(jnp.int32, (ROLL_W, N_REL_PAD), 1)
        e = jnp.where(c >= Q_BLOCK - 1, c - (Q_BLOCK - 1), c + (ROLL_W - Q_BLOCK + 1))
        pick = (r == _rel_index(e)).astype(F32)
        sums = []
        for h in range(N_HEADS):
            a = jnp.dot(flip, a_ref[h], precision=HIGHEST, preferred_element_type=F32)
            a = jnp.concatenate([a, jnp.zeros((Q_BLOCK, ROLL_W - K_SPAN), F32)], axis=1)
            a = pltpu.roll(a, 0, 1, stride=1, stride_axis=0)
            sums.append(jnp.sum(a, axis=0, keepdims=True))
        diag = jnp.concatenate(sums, axis=0)
        o_ref[...] = jnp.dot(diag, pick, precision=HIGHEST, preferred_element_type=F32)

    return pl.pallas_call(
        body, name="bias_grad",
        out_shape=jax.ShapeDtypeStruct((N_HEADS, N_REL_PAD), F32),
        compiler_params=_params(32),
    )(dbias)


def _gather_proj_fwd(x, norm_g, w_in_t):
    s = x.shape[0]
    tm = 512 if s % 512 == 0 else TOKEN_TILE
    nt = s // tm
    n_pad = Z_PAD // tm
    shard_w = w_in_t.shape[0]
    chip_w = 2 * shard_w
    n_chips = N_DEV // 2

    def body(order_ref, x_ref, g_ref, win_hbm, z_ref, xn_ref, wt_hbm, wt, hb, win_f32, send_sems, recv_sems,
             local_sems):
        j = pl.program_id(0)
        i = pl.program_id(1)
        x_, y_, c_ = _my_pos()
        me, sib = (x_, y_, c_), (x_, y_, 1 - c_)
        near = _other_chips(me)
        pick = lambda a, b: tuple(jnp.where(c_ == 0, u, v) for u, v in zip(a, b))
        passed_from, passed_to = pick(near[0], near[1]), pick(near[1], near[0])

        def rows_of(block):
            return wt.at[pl.ds(pl.multiple_of(_flat_id(block) * shard_w, 16), shard_w), :]

        def copy(k, block, to):
            return pltpu.make_async_remote_copy(
                src_ref=rows_of(block), dst_ref=rows_of(block),
                send_sem=send_sems.at[k], recv_sem=recv_sems.at[k], device_id=to, device_id_type=MESH)

        def sends():
            return ([copy(0, me, sib), copy(1, me, (*near[0], c_)), copy(2, me, (*near[1], c_)),
                     copy(3, (*passed_from, c_), (*passed_to, c_))]
                    + [copy(4 + n, (*near[n], c_), sib) for n in range(3)])

        keep = pltpu.make_async_copy(wt, wt_hbm, local_sems.at[0])

        @pl.when((j == 0) & (i == 0))
        def _():
            load = pltpu.make_async_copy(win_hbm, win_f32, local_sems.at[1])
            load.start()
            load.wait()
            rows_of(me)[...] = win_f32[...].astype(BF16)
            for cp in sends()[:3]:
                cp.start()
            copy(0, sib, me).wait_recv()

        @pl.when((j == 1) & (i == 0))
        def _():
            copy(1, (*near[0], c_), me).wait_recv()
            copy(2, (*near[1], c_), me).wait_recv()
            for cp in sends()[3:6]:
                cp.start()
            copy(4, (*near[0], 1 - c_), me).wait_recv()

        @pl.when((j == 2) & (i == 0))
        def _():
            copy(5, (*near[1], 1 - c_), me).wait_recv()

        @pl.when((j == 3) & (i == 0))
        def _():
            copy(3, (*near[2], c_), me).wait_recv()
            copy(6, (*near[2], c_), sib).start()
            copy(6, (*near[2], 1 - c_), me).wait_recv()
            keep.start()

        @pl.when(i < n_pad)
        def _():
            z_ref[...] = jnp.zeros(z_ref.shape, BF16)

        @pl.when(i >= n_pad)
        def _():
            rows = pl.ds(pl.multiple_of((i - n_pad) * tm, tm), tm)

            @pl.when(j == 0)
            def _():
                xf = x_ref[...]
                xn = xf * lax.rsqrt(jnp.mean(xf * xf, axis=-1, keepdims=True) + EPS)
                hb[rows, :] = (xn * g_ref[...]).astype(BF16)
                xn_ref[...] = xn.astype(BF16)

            chip_rows = pl.ds(pl.multiple_of(order_ref[j] * chip_w, 16), chip_w)
            blk = _dot(hb[rows, :], wt[chip_rows, :], NT)
            q_scale = jnp.where(order_ref[j] == 0, Q_SCALE, 1.0).astype(F32)
            z_ref[:, :D_A] = (blk[:, :D_A] * q_scale).astype(BF16)
            z_ref[:, D_A:] = blk[:, D_A:].astype(BF16)

        @pl.when((j == n_chips - 1) & (i == n_pad + nt - 1))
        def _():
            keep.wait()
            for cp in sends():
                cp.wait_send()

    pos = _my_pos()
    order = jnp.stack([2 * cx + cy for cx, cy in ((pos[0], pos[1]),) + _other_chips(pos)]).astype(jnp.int32)
    first_pass = lambda j, i: jnp.where(j == 0, jnp.maximum(i - n_pad, 0), nt - 1)
    grid_spec = pltpu.PrefetchScalarGridSpec(
        num_scalar_prefetch=1,
        grid=(n_chips, n_pad + nt),
        in_specs=[pl.BlockSpec((tm, D_MODEL), lambda j, i, o: (first_pass(j, i), 0)),
                  pl.BlockSpec((1, D_MODEL), lambda j, i, o: (0, 0)),
                  pl.BlockSpec(memory_space=pl.ANY)],
        out_specs=(pl.BlockSpec((tm, chip_w), lambda j, i, o: (i, o[j])),
                   pl.BlockSpec((tm, D_MODEL), lambda j, i, o: (first_pass(j, i), 0)),
                   pl.BlockSpec(memory_space=pl.ANY)),
        scratch_shapes=[pltpu.VMEM((D_IN, D_MODEL), BF16),
                        pltpu.VMEM((s, D_MODEL), BF16), pltpu.VMEM(w_in_t.shape, F32),
                        pltpu.SemaphoreType.DMA((N_DEV - 1,)), pltpu.SemaphoreType.DMA((N_DEV - 1,)),
                        pltpu.SemaphoreType.DMA((2,))])
    return pl.pallas_call(
        body, name="gather_proj_fwd",
        grid_spec=grid_spec,
        out_shape=(jax.ShapeDtypeStruct((Z_PAD + s, D_IN), BF16), jax.ShapeDtypeStruct((s, D_MODEL), BF16),
                   jax.ShapeDtypeStruct((D_IN, D_MODEL), BF16)),
        compiler_params=_params(60),
    )(order, x, norm_g, w_in_t)


def _attn_specs(rows):
    pairs = N_HEADS // 2
    return ([pl.BlockSpec((rows, 128), functools.partial(lambda which, p: (0, which * pairs + p), which))
             for which in range(3)]
            + [pl.BlockSpec((2, Q_BLOCK, K_SPAN), lambda p: (p, 0, 0))])


def _head_masks():
    lane = lax.broadcasted_iota(jnp.int32, (1, 128), 1)
    first = lane < HEAD_DIM
    return (first, jnp.logical_not(first))


def _stack_heads(x, masks):
    zero = jnp.zeros((), x.dtype)
    return jnp.concatenate([jnp.where(m, x, zero) for m in masks], axis=0)


STRIP = 16


def _softmax_strips(s_ref, bias_ref, b):
    valid = lax.broadcasted_iota(jnp.int32, (1, K_SPAN), 1) >= Z_PAD - b * Q_BLOCK
    for t in range(2 * Q_BLOCK // STRIP):
        hh, r = divmod(t * STRIP, Q_BLOCK)
        st = s_ref[t * STRIP:(t + 1) * STRIP, :] + bias_ref[hh, r:r + STRIP, :]
        st = jnp.where(valid, st, NEG_INF)
        e = jnp.exp(st - jnp.max(st, axis=-1, keepdims=True))
        yield e * (1.0 / jnp.sum(e, axis=-1, keepdims=True))


def _side_by_side_strips(strips):
    half = len(strips) // 2
    return jnp.concatenate([jnp.concatenate([a, c], axis=1) for a, c in zip(strips[:half], strips[half:])], axis=0)


def _attn_fwd(qkv, bias_table, shards):
    s = qkv.shape[0] - Z_PAD
    nb = s // Q_BLOCK
    n = len(shards)
    pairs = N_HEADS // 2

    def body(*refs):
        q_ref, k_ref, v_ref, bt_ref = refs[:4]
        shard_refs = refs[4:4 + n]
        o_ref = refs[4 + n]
        slot_refs = refs[5 + n:5 + 2 * n]
        stages = refs[5 + 2 * n:5 + 3 * n]
        s_scr, send_sems, recv_sems, local_sems = refs[5 + 3 * n:]
        p_id = pl.program_id(0)
        gather = _SlotGather(slot_refs, send_sems, recv_sems, own=stages)
        keep = [pltpu.make_async_copy(stages[a], slot_refs[a].at[_flat_id(_my_pos())], local_sems.at[a])
                for a in range(n)]

        @pl.when(p_id == 0)
        def _():
            for a in range(n):
                stages[a][...] = shard_refs[a][...].astype(BF16)
                keep[a].start()
            gather.start()

        @pl.when(p_id == 2)
        def _():
            gather.pass_on()

        masks = _head_masks()

        def scores(b, half):
            r0 = pl.multiple_of(b * Q_BLOCK, Q_BLOCK)
            q2 = _stack_heads(q_ref[pl.ds(r0 + Z_PAD, Q_BLOCK), :], masks)
            s_scr[half] = _dot(q2, k_ref[pl.ds(r0, K_SPAN), :], NT)

        def finish(b, half):
            r0 = pl.multiple_of(b * Q_BLOCK, Q_BLOCK)
            v2 = _stack_heads(v_ref[pl.ds(r0, K_SPAN), :], masks)
            p = [st.astype(BF16) for st in _softmax_strips(s_scr.at[half], bt_ref, b)]
            o_ref[pl.ds(r0, Q_BLOCK), :] = _dot(_side_by_side_strips(p), v2)

        def two_blocks(i, carry):
            b = 2 * i
            scores(b + 1, 1)
            finish(b, 0)
            scores(jnp.minimum(b + 2, nb - 1), 0)
            finish(b + 1, 1)
            return carry

        scores(0, 0)
        lax.fori_loop(0, nb // 2, two_blocks, 0)

        @pl.when(p_id == pairs - 1)
        def _():
            gather.finish()
            for cp in keep:
                cp.wait()

    hbm = pl.BlockSpec(memory_space=pl.ANY)
    return pl.pallas_call(
        body, name="attn_fwd",
        grid=(pairs,),
        in_specs=_attn_specs(s + Z_PAD) + [pl.BlockSpec(a.shape, lambda p: (0, 0)) for a in shards],
        out_specs=(pl.BlockSpec((s, 128), lambda p: (0, p)),) + (hbm,) * n,
        out_shape=(jax.ShapeDtypeStruct((s, D_A), F32),)
        + tuple(jax.ShapeDtypeStruct((N_DEV,) + a.shape, BF16) for a in shards),
        scratch_shapes=[pltpu.VMEM(a.shape, BF16) for a in shards]
        + [pltpu.VMEM((2, 2 * Q_BLOCK, K_SPAN), F32),
           pltpu.SemaphoreType.DMA((n, N_DEV - 1)), pltpu.SemaphoreType.DMA((n, N_DEV - 1)),
           pltpu.SemaphoreType.DMA((n,))],
        compiler_params=_params(48),
    )(qkv, qkv, qkv, bias_table, *shards)


def _fill_slab(stage, rows):
    stage[...] = jnp.zeros(stage.shape, F32)
    for row, ref in rows:
        r, c = ref.shape
        if c > D_MODEL:
            for part in range(c // D_MODEL):
                stage[row + part:row + part + 1, :] = ref[:, part * D_MODEL:(part + 1) * D_MODEL]
        else:
            stage[row:row + r, :c] = ref[...]


def _attn_bwd(qkv, bias_table, d_out, cuts, small):
    s = qkv.shape[0] - Z_PAD
    nb = s // Q_BLOCK
    n = len(cuts)
    pairs = N_HEADS // 2
    ws_shape = small[-1].shape

    def body(*refs):
        q_ref, k_ref, v_ref, bt_ref, do_ref = refs[:5]
        cut_refs = refs[5:5 + n]
        bg_ref, lng_ref, lnb_ref, fg_ref, loss_ref, bs_ref, ws_ref = refs[5 + n:12 + n]
        dqkv_ref, db_ref = refs[12 + n:14 + n]
        from_chip_refs = refs[14 + n:14 + 2 * n]
        slab_land, ws_land = refs[14 + 2 * n:16 + 2 * n]
        own_refs = refs[16 + 2 * n:16 + 3 * n]
        (dk_acc, dv_acc, s_scr, dp_scr, slab_stage, ws_stage, send_sems, recv_sems, gather_send, gather_recv,
         keep_sems) = refs[16 + 3 * n:27 + 3 * n]
        mine, from_sib, to_chip_refs = (refs[27 + k * n:27 + (k + 1) * n] for k in (3, 4, 5))
        sib_send, sib_recv, load_sems = refs[27 + 6 * n:]
        p_id = pl.program_id(0)
        x_, y_, c_ = _my_pos()
        me = _flat_id((x_, y_, c_))
        chips = ((x_, y_),) + _other_chips((x_, y_, c_))

        def to_sibling(a, r):
            return pltpu.make_async_remote_copy(
                src_ref=cut_refs[a].at[_flat_id((*chips[r], 1 - c_))], dst_ref=from_sib[a].at[r],
                send_sem=sib_send.at[a, r], recv_sem=sib_recv.at[a, r], device_id=(x_, y_, 1 - c_),
                device_id_type=MESH)

        def load(a, r):
            return pltpu.make_async_copy(cut_refs[a].at[_flat_id((*chips[r], c_))], mine[a].at[r], load_sems.at[a, r])

        pieces = [(a, r) for r in (1, 2, 3, 0) for a in range(n)]
        gather = _SlotGather([slab_land, ws_land], gather_send, gather_recv, own=[slab_stage, ws_stage])
        keep = [pltpu.make_async_copy(stage, land.at[me], keep_sems.at[k]) for k, (stage, land) in enumerate(
            ((slab_stage, slab_land), (ws_stage, ws_land)))]

        @pl.when(p_id == 0)
        def _():
            _fill_slab(slab_stage, ((ROW_B_GATE, bg_ref), (ROW_LN_G, lng_ref), (ROW_LN_B, lnb_ref),
                                    (ROW_FINAL_G, fg_ref), (ROW_LOSS, loss_ref)))
            eye = (lax.broadcasted_iota(jnp.int32, (SGU_CHUNK, SGU_CHUNK), 0)
                   == lax.broadcasted_iota(jnp.int32, (SGU_CHUNK, SGU_CHUNK), 1))
            for g in range(N_GROUPS):
                row = jnp.sum(jnp.where(eye, bs_ref[g], 0.0), axis=0, keepdims=True)
                slab_stage[ROW_B_S + g:ROW_B_S + g + 1, :SGU_CHUNK] = row
            ws_stage[...] = ws_ref[...]
            for cp in keep:
                cp.start()
            gather.start()
            for a, r in pieces:
                to_sibling(a, r).start()
                load(a, r).start()

        @pl.when(p_id == 1)
        def _():
            for a, r in pieces:
                load(a, r).wait()
                to_sibling(a, r).wait_recv()
                both = mine[a][r].astype(F32) + from_sib[a][r].astype(F32)
                if r == 0:
                    own_refs[a][...] = both
                else:
                    to_chip_refs[a][r - 1] = both.astype(BF16)
            for cp in _owner_copies(to_chip_refs, from_chip_refs, send_sems, recv_sems):
                cp.start()

        @pl.when(p_id == 2)
        def _():
            gather.pass_on()

        dk_acc[...] = jnp.zeros(dk_acc.shape, F32)
        dv_acc[...] = jnp.zeros(dv_acc.shape, F32)
        db_ref[...] = jnp.zeros(db_ref.shape, F32)
        masks = _head_masks()

        def operands(b):
            r0 = pl.multiple_of(b * Q_BLOCK, Q_BLOCK)
            q2 = _stack_heads(q_ref[pl.ds(r0 + Z_PAD, Q_BLOCK), :], masks)
            do2 = _stack_heads(do_ref[pl.ds(r0, Q_BLOCK), :], masks)
            return r0, q2, do2, k_ref[pl.ds(r0, K_SPAN), :]

        def ahead(b, half):
            r0, q2, do2, kcat = operands(b)
            s_scr[half] = _dot(q2, kcat, NT)
            dp_scr[half] = _dot(do2, v_ref[pl.ds(r0, K_SPAN), :], NT)

        def finish(b, half):
            r0, q2, do2, kcat = operands(b)
            p_strips, ds_strips = [], []
            for t, p in enumerate(_softmax_strips(s_scr.at[half], bt_ref, b)):
                hh, r = divmod(t * STRIP, Q_BLOCK)
                dp_t = dp_scr[half, t * STRIP:(t + 1) * STRIP, :]
                ds = p * (dp_t - jnp.sum(p * dp_t, axis=-1, keepdims=True))
                db_ref[hh, r:r + STRIP, :] += ds
                p_strips.append(p.astype(BF16))
                ds_strips.append(ds.astype(BF16))
            dq = _dot(_side_by_side_strips(ds_strips), _stack_heads(kcat, masks))
            dqkv_ref[0, pl.ds(r0, Q_BLOCK), :] = (dq * Q_SCALE).astype(BF16)
            dk_acc[pl.ds(r0, K_SPAN), :] += _dot(jnp.concatenate(ds_strips, axis=0), q2, TN)
            dv_acc[pl.ds(r0, K_SPAN), :] += _dot(jnp.concatenate(p_strips, axis=0), do2, TN)

        def two_blocks(i, carry):
            b = 2 * i
            ahead(b + 1, 1)
            finish(b, 0)
            ahead(jnp.minimum(b + 2, nb - 1), 0)
            finish(b + 1, 1)
            return carry

        ahead(0, 0)
        lax.fori_loop(0, nb // 2, two_blocks, 0)
        dqkv_ref[1] = dk_acc[Z_PAD:, :].astype(BF16)
        dqkv_ref[2] = dv_acc[Z_PAD:, :].astype(BF16)

        @pl.when(p_id == pairs - 1)
        def _():
            gather.finish()
            for cp in keep:
                cp.wait()
            for cp in _owner_copies(to_chip_refs, from_chip_refs, send_sems, recv_sems):
                cp.wait_recv()
                cp.wait_send()
            for a, r in pieces:
                to_sibling(a, r).wait_send()

    hbm = pl.BlockSpec(memory_space=pl.ANY)
    lands = ((N_DEV, SLAB_ROWS, D_MODEL), (N_DEV,) + ws_shape)
    blocks = [c.shape[1:] for c in cuts]
    return pl.pallas_call(
        body, name="attn_bwd",
        grid=(pairs,),
        in_specs=_attn_specs(s + Z_PAD) + [pl.BlockSpec((s, 128), lambda p: (0, p))] + [hbm] * n
        + [pl.BlockSpec(a.shape, functools.partial(lambda nd, p: (0,) * nd, a.ndim)) for a in small],
        out_specs=(pl.BlockSpec((3, s, 128), lambda p: (0, 0, p)),
                   pl.BlockSpec((2, Q_BLOCK, K_SPAN), lambda p: (p, 0, 0))) + (hbm,) * (n + 2)
        + tuple(pl.BlockSpec(b, lambda p: (0, 0)) for b in blocks),
        out_shape=(jax.ShapeDtypeStruct((3, s, D_A), BF16),
                   jax.ShapeDtypeStruct((N_HEADS, Q_BLOCK, K_SPAN), F32))
        + tuple(jax.ShapeDtypeStruct((3,) + b, BF16) for b in blocks)
        + tuple(jax.ShapeDtypeStruct(shape, F32) for shape in lands)
        + tuple(jax.ShapeDtypeStruct(b, F32) for b in blocks),
        scratch_shapes=[pltpu.VMEM((s + Z_PAD, 128), F32), pltpu.VMEM((s + Z_PAD, 128), F32),
                        pltpu.VMEM((2, 2 * Q_BLOCK, K_SPAN), F32), pltpu.VMEM((2, 2 * Q_BLOCK, K_SPAN), F32)]
        + [pltpu.VMEM(shape[1:], F32) for shape in lands]
        + [pltpu.SemaphoreType.DMA((n, 3)), pltpu.SemaphoreType.DMA((n, 3)),
           pltpu.SemaphoreType.DMA((2, N_DEV - 1)), pltpu.SemaphoreType.DMA((2, N_DEV - 1)),
           pltpu.SemaphoreType.DMA((2,))]
        + [pltpu.VMEM((4,) + b, BF16) for b in blocks] * 2 + [pltpu.VMEM((3,) + b, BF16) for b in blocks]
        + [pltpu.SemaphoreType.DMA((n, 4))] * 3,
        compiler_params=_params(56),
    )(qkv, qkv, qkv, bias_table, d_out, *cuts, *small)


def _mid_fwd_bwd(x, target, attn_out, z, w_pa, w_pb, w_out, b_gate, ln_g, ln_b, w_s, b_s, final_g):
    s = x.shape[0]
    tm = TOKEN_TILE
    nt = s // tm

    def body(x_ref, t_ref, oa_ref, ga_ref, ub_ref, vb_ref, gb_ref, ta0_ref, ta1_ref, tb0_ref, tb1_ref,
             wpa_hbm, wpb_hbm, wout_hbm, bg_ref, lng_ref, lnb_ref, ws_ref, bs_ref, fg_ref,
             dx2_ref, doa_ref, dz_ref, dwout_hbm, dwpa_hbm, dwpb_hbm, dbg_ref, dfg_ref, dlng_ref, dlnb_ref, dws_ref,
             dbs_ref, loss_ref,
             wpa, wpb, wout, wmix, acc_out, acc_pa, acc_pb, cut_out, cut_pa, cut_pb, sem):
        i = pl.program_id(0)

        @pl.when(i == 0)
        def _():
            loads = [pltpu.make_async_copy(src, dst, sem.at[n])
                     for n, (src, dst) in enumerate(((wpa_hbm, wpa), (wpb_hbm, wpb), (wout_hbm, wout)))]
            for cp in loads:
                cp.start()
            t_idx = lax.broadcasted_iota(jnp.int32, (SGU_CHUNK, SGU_CHUNK), 0)
            s_idx = lax.broadcasted_iota(jnp.int32, (SGU_CHUNK, SGU_CHUNK), 1)
            for g in range(N_GROUPS):
                wmix[g] = jnp.where(s_idx <= t_idx, ws_ref[g], 0.0).astype(BF16)
            for ref in (acc_out, acc_pa, acc_pb, dbg_ref, dfg_ref, dlng_ref, dlnb_ref, dws_ref, dbs_ref, loss_ref):
                ref[...] = jnp.zeros(ref.shape, F32)
            for cp in loads:
                cp.wait()

        def tile_fwd_bwd(rows):
            g_a = ga_ref[rows, :].astype(F32)
            u_b = ub_ref[rows, :].astype(F32)
            v_b = vb_ref[rows, :].astype(F32)
            g_b = gb_ref[rows, :].astype(F32)
            bg = bg_ref[...]
            sg_a = _sigmoid(g_a)
            silu_a = g_a * sg_a
            o_a = oa_ref[rows, :]
            y_a = (o_a * silu_a).astype(BF16)
            ug, dgelu_u = _gelu_and_grad(u_b)
            vg, dgelu_v = _gelu_and_grad(v_b)
            mu = jnp.mean(vg, axis=-1, keepdims=True)
            vc = vg - mu
            rstd = lax.rsqrt(jnp.mean(vc * vc, axis=-1, keepdims=True) + EPS)
            vhat = vc * rstd
            lng = lng_ref[...]
            vn = (vhat * lng + lnb_ref[...]).astype(BF16)
            sg_b = _sigmoid(g_b)
            silu_b = g_b * sg_b
            subs = [slice(n * SGU_CHUNK, (n + 1) * SGU_CHUNK) for n in range(tm // SGU_CHUNK)]
            mixed = jnp.concatenate([jnp.concatenate(
                [_dot(wmix[g], vn[sub, g * 128:(g + 1) * 128]) + bs_ref[g] for g in range(N_GROUPS)], axis=1)
                for sub in subs], axis=0)
            um = ug * mixed
            y_b = (um * silu_b).astype(BF16)
            gate_a = _sigmoid(jnp.concatenate([ta0_ref[rows, :], ta1_ref[rows, :]], axis=1).astype(F32)
                              + bg[:, :D_MODEL])
            gate_b = _sigmoid(jnp.concatenate([tb0_ref[rows, :], tb1_ref[rows, :]], axis=1).astype(F32)
                              + bg[:, D_MODEL:])
            p_a = _dot(y_a, wpa[...])
            p_b = _dot(y_b, wpb[...])
            merged = (gate_a * p_a + gate_b * p_b).astype(BF16)
            x2 = x_ref[rows, :] + _dot(merged, wout[...])
            r2 = lax.rsqrt(jnp.mean(x2 * x2, axis=-1, keepdims=True) + EPS)
            xh = x2 * r2
            fg = fg_ref[...]
            err = xh * fg - t_ref[rows, :]
            loss_ref[...] += jnp.sum(jnp.sum(err * err, axis=-1, keepdims=True), axis=0, keepdims=True) * (0.5 / D_MODEL)
            dy = err * (1.0 / D_MODEL)
            dfg_ref[...] += jnp.sum(dy * xh, axis=0, keepdims=True)
            gy = dy * fg
            dx2 = r2 * (gy - xh * jnp.mean(gy * xh, axis=-1, keepdims=True))
            dx2_ref[rows, :] = dx2
            dx2b = dx2.astype(BF16)
            dmerged = _dot(dx2b, wout[...], NT)
            acc_out[...] += _dot(merged, dx2b, TN)
            dp_a = dmerged * gate_a
            dp_b = dmerged * gate_b
            dgate_a = dp_a * p_a * (1.0 - gate_a)
            dgate_b = dp_b * p_b * (1.0 - gate_b)
            dbg_ref[:, :D_MODEL] += jnp.sum(dgate_a, axis=0, keepdims=True)
            dbg_ref[:, D_MODEL:] += jnp.sum(dgate_b, axis=0, keepdims=True)
            dz_ref[rows, 2048:3072] = dgate_a.astype(BF16)
            dz_ref[rows, 3072:4096] = dgate_b.astype(BF16)
            dp_ab = dp_a.astype(BF16)
            dp_bb = dp_b.astype(BF16)
            dy_a = _dot(dp_ab, wpa[...], NT)
            dy_b = _dot(dp_bb, wpb[...], NT)
            acc_pa[...] += _dot(y_a, dp_ab, TN)
            acc_pb[...] += _dot(y_b, dp_bb, TN)
            doa_ref[rows, :] = (dy_a * silu_a).astype(BF16)
            dz_ref[rows, 0:512] = (dy_a * o_a * (sg_a * (1.0 + g_a * (1.0 - sg_a)))).astype(BF16)
            dz_ref[rows, 1536:2048] = (dy_b * um * (sg_b * (1.0 + g_b * (1.0 - sg_b)))).astype(BF16)
            dys = dy_b * silu_b
            dz_ref[rows, 512:1024] = (dys * mixed * dgelu_u).astype(BF16)
            dmixed = dys * ug
            dmb = dmixed.astype(BF16)
            dvn_rows = []
            for sub in subs:
                dvn_parts = []
                for g in range(N_GROUPS):
                    cols = slice(g * 128, (g + 1) * 128)
                    dws_ref[g] += _dot(dmb[sub, cols], vn[sub, cols], NT)
                    dbs_ref[g] += jnp.sum(dmixed[sub, cols], axis=-1, keepdims=True)
                    dvn_parts.append(_dot(wmix[g], dmb[sub, cols], TN))
                dvn_rows.append(jnp.concatenate(dvn_parts, axis=1))
            dvn = jnp.concatenate(dvn_rows, axis=0)
            dlng_ref[...] += jnp.sum(dvn * vhat, axis=0, keepdims=True)
            dlnb_ref[...] += jnp.sum(dvn, axis=0, keepdims=True)
            dvh = dvn * lng
            dvg = rstd * (dvh - jnp.mean(dvh, axis=-1, keepdims=True)
                          - vhat * jnp.mean(dvh * vhat, axis=-1, keepdims=True))
            dz_ref[rows, 1024:1536] = (dvg * dgelu_v).astype(BF16)

        tile_fwd_bwd(slice(0, tm))

        @pl.when(i == nt - 1)
        def _():
            t_idx = lax.broadcasted_iota(jnp.int32, (SGU_CHUNK, SGU_CHUNK), 0)
            s_idx = lax.broadcasted_iota(jnp.int32, (SGU_CHUNK, SGU_CHUNK), 1)
            for g in range(N_GROUPS):
                dws_ref[g] = jnp.where(s_idx <= t_idx, dws_ref[g], 0.0)
            for d in range(N_DEV):
                cut_out[d] = acc_out[d * SHARD:(d + 1) * SHARD, :].astype(BF16)
                cut_pa[d] = acc_pa[:, d * SHARD:(d + 1) * SHARD].astype(BF16)
                cut_pb[d] = acc_pb[:, d * SHARD:(d + 1) * SHARD].astype(BF16)
            stores = [pltpu.make_async_copy(src, dst, sem.at[n])
                      for n, (src, dst) in enumerate(((cut_out, dwout_hbm), (cut_pa, dwpa_hbm), (cut_pb, dwpb_hbm)))]
            for cp in stores:
                cp.start()
            for cp in stores:
                cp.wait()

    tile = lambda w: pl.BlockSpec((tm, w), lambda i: (i, 0))
    whole = lambda shape: pl.BlockSpec(shape, lambda i: (0,) * len(shape))
    hbm = pl.BlockSpec(memory_space=pl.ANY)
    cut_shapes = ((N_DEV, SHARD, D_MODEL), (N_DEV, D_A, SHARD), (N_DEV, D_B, SHARD))
    return pl.pallas_call(
        body, name="mid_fwd_bwd",
        grid=(nt,),
        in_specs=[tile(D_MODEL), tile(D_MODEL), tile(D_A)]
        + [pl.BlockSpec((tm, COL_BLOCK), functools.partial(lambda c, i: (i + Z_PAD // tm, c), c))
           for c in range(3, N_COL_BLOCKS)]
        + [hbm, hbm, hbm,
                  whole((1, 2 * D_MODEL)), whole((1, D_B)), whole((1, D_B)),
                  whole((N_GROUPS, SGU_CHUNK, SGU_CHUNK)), whole((N_GROUPS, SGU_CHUNK, 1)), whole((1, D_MODEL))],
        out_specs=(tile(D_MODEL), tile(D_A), tile(REST), hbm, hbm, hbm,
                   whole((1, 2 * D_MODEL)), whole((1, D_MODEL)), whole((1, D_B)), whole((1, D_B)),
                   whole((N_GROUPS, SGU_CHUNK, SGU_CHUNK)), whole((N_GROUPS, SGU_CHUNK, 1)), whole((1, 1))),
        out_shape=(jax.ShapeDtypeStruct((s, D_MODEL), F32), jax.ShapeDtypeStruct((s, D_A), BF16),
                   jax.ShapeDtypeStruct((s, REST), BF16),
                   *(jax.ShapeDtypeStruct(shape, BF16) for shape in cut_shapes),
                   jax.ShapeDtypeStruct((1, 2 * D_MODEL), F32), jax.ShapeDtypeStruct((1, D_MODEL), F32),
                   jax.ShapeDtypeStruct((1, D_B), F32), jax.ShapeDtypeStruct((1, D_B), F32),
                   jax.ShapeDtypeStruct((N_GROUPS, SGU_CHUNK, SGU_CHUNK), F32),
                   jax.ShapeDtypeStruct((N_GROUPS, SGU_CHUNK, 1), F32), jax.ShapeDtypeStruct((1, 1), F32)),
        scratch_shapes=[pltpu.VMEM((D_A, D_MODEL), BF16), pltpu.VMEM((D_B, D_MODEL), BF16),
                        pltpu.VMEM((D_MODEL, D_MODEL), BF16), pltpu.VMEM((N_GROUPS, SGU_CHUNK, SGU_CHUNK), BF16),
                        pltpu.VMEM((D_MODEL, D_MODEL), F32), pltpu.VMEM((D_A, D_MODEL), F32),
                        pltpu.VMEM((D_B, D_MODEL), F32)]
        + [pltpu.VMEM(shape, BF16) for shape in cut_shapes]
        + [pltpu.SemaphoreType.DMA((3,))],
        compiler_params=_params(56),
    )(x, target, attn_out, *([z] * (N_COL_BLOCKS - 3)), w_pa, w_pb, w_out, b_gate, ln_g, ln_b, w_s, b_s, final_g)


def _proj_bwd_x(dqkv, drest, x, dx2, norm_g, w_in_t, to_chip, small):
    s = x.shape[0]
    tm = 512 if s % 512 == 0 else TOKEN_TILE
    nt = s // tm
    rows = to_chip.shape[1]
    half = D_MODEL // 2
    left, right = slice(0, half), slice(half, D_MODEL)

    def body(dqkv_ref, dr_ref, x_ref, dx2_ref, g_ref, w_hbm, tc_hbm, ng_ref, rel_ref,
             dx_ref, fc_ref, slab_land,
             w, tc_ref, slab_stage, via_x, via_y, mine, out_x, out_y, sem, send_sems, recv_sems,
             gather_send, gather_recv, keep_sems):
        i = pl.program_id(0)
        x_, y_, c_ = _my_pos()
        me = _flat_id((x_, y_, c_))
        xn, yn = (1 - x_, y_, c_), (x_, 1 - y_, c_)
        gather = _SlotGather([slab_land], gather_send, gather_recv, own=[slab_stage])
        keep = [pltpu.make_async_copy(slab_stage, slab_land.at[me], keep_sems.at[0])]

        def copy(k, src, dst, to):
            return pltpu.make_async_remote_copy(src_ref=src, dst_ref=dst, send_sem=send_sems.at[k],
                                                recv_sem=recv_sems.at[k], device_id=to, device_id_type=MESH)

        first = [copy(0, tc_ref.at[0, :, left], fc_ref.at[0, :, left], xn), copy(1, tc_ref.at[2, :, left], via_x, xn),
                 copy(2, tc_ref.at[1, :, right], fc_ref.at[1, :, right], yn), copy(3, tc_ref.at[2, :, right], via_y, yn)]
        second = [copy(4, out_y, fc_ref.at[1, :, left], yn), copy(5, out_x, fc_ref.at[0, :, right], xn)]

        def add_and_send(arrival, landed, own_half, stage, onward):
            load = pltpu.make_async_copy(own_half, mine, sem)
            load.start()
            arrival.wait_recv()
            load.wait()
            stage[...] = (mine[...].astype(F32) + landed[...].astype(F32)).astype(BF16)
            onward.start()

        @pl.when(i == 0)
        def _():
            cp = pltpu.make_async_copy(w_hbm, w, sem)
            cp.start()
            _fill_slab(slab_stage, ((ROW_NORM_G, ng_ref), (ROW_REL, rel_ref)))
            for cp_keep in keep:
                cp_keep.start()
            gather.start()
            stage_in = pltpu.make_async_copy(tc_hbm, tc_ref, keep_sems.at[2])
            stage_in.start()
            stage_in.wait()
            for rc in first:
                rc.start()
            cp.wait()

        @pl.when(i == (5 * nt) // 8)
        def _():
            gather.pass_on()
            add_and_send(first[1], via_x, tc_ref.at[1, :, left], out_y, second[0])
            add_and_send(first[3], via_y, tc_ref.at[0, :, right], out_x, second[1])

        dh = None
        for c in range(N_COL_BLOCKS):
            dz = dqkv_ref[c] if c < 3 else dr_ref[:, (c - 3) * COL_BLOCK:(c - 2) * COL_BLOCK]
            part = _dot(dz, w[c * COL_BLOCK:(c + 1) * COL_BLOCK, :])
            dh = part if dh is None else dh + part
        xf = x_ref[...]
        r = lax.rsqrt(jnp.mean(xf * xf, axis=-1, keepdims=True) + EPS)
        xn = xf * r
        gh = dh * g_ref[...]
        dx_ref[...] = r * (gh - xn * jnp.mean(gh * xn, axis=-1, keepdims=True)) + dx2_ref[...]

        @pl.when(i == nt - 1)
        def _():
            gather.finish()
            for cp_keep in keep:
                cp_keep.wait()
            for k in (0, 2, 4, 5):
                (first + second)[k].wait_recv()
            for rc in first + second:
                rc.wait_send()

    hbm = pl.BlockSpec(memory_space=pl.ANY)
    whole = lambda a: pl.BlockSpec(a.shape, lambda i: (0,) * a.ndim)
    return pl.pallas_call(
        body, name="proj_bwd_x",
        grid=(nt,),
        in_specs=[pl.BlockSpec((3, tm, D_A), lambda i: (0, i, 0)),
                  pl.BlockSpec((tm, REST), lambda i: (i, 0)),
                  pl.BlockSpec((tm, D_MODEL), lambda i: (i, 0)),
                  pl.BlockSpec((tm, D_MODEL), lambda i: (i, 0)),
                  pl.BlockSpec((1, D_MODEL), lambda i: (0, 0)),
                  hbm, hbm] + [whole(a) for a in small],
        out_specs=(pl.BlockSpec((tm, D_MODEL), lambda i: (i, 0)), hbm, hbm),
        out_shape=(jax.ShapeDtypeStruct((s, D_MODEL), F32), jax.ShapeDtypeStruct((2, rows, D_MODEL), BF16),
                   jax.ShapeDtypeStruct((N_DEV, LATE_SLAB_ROWS, D_MODEL), F32)),
        scratch_shapes=[pltpu.VMEM((D_IN, D_MODEL), BF16), pltpu.VMEM(to_chip.shape, BF16),
                        pltpu.VMEM((LATE_SLAB_ROWS, D_MODEL), F32)]
        + [pltpu.VMEM((rows, half), BF16)] * 5
        + [pltpu.SemaphoreType.DMA, pltpu.SemaphoreType.DMA((6,)), pltpu.SemaphoreType.DMA((6,)),
           pltpu.SemaphoreType.DMA((1, N_DEV - 1)), pltpu.SemaphoreType.DMA((1, N_DEV - 1)),
           pltpu.SemaphoreType.DMA((3,))],
        compiler_params=_params(56),
    )(dqkv, drest, x, dx2, norm_g, w_in_t, to_chip, *small)


def _proj_bwd_w(xn, dqkv, drest, norm_g, w_in_t):
    s = xn.shape[0]
    tk = s
    nk = s // tk

    def body(xn_ref, dqkv_ref, dr_ref, g_ref, w_ref, o_ref, dg_ref, acc):
        j = pl.program_id(0)
        i = pl.program_id(1)

        @pl.when((j == 0) & (i == 0))
        def _():
            dg_ref[...] = jnp.zeros(dg_ref.shape, F32)

        @pl.when(i == 0)
        def _():
            acc[...] = jnp.zeros(acc.shape, F32)

        @pl.when(j < 3)
        def _():
            acc[...] += _dot(dqkv_ref[...], xn_ref[...], TN)

        @pl.when(j >= 3)
        def _():
            acc[...] += _dot(dr_ref[...], xn_ref[...], TN)

        @pl.when(i == nk - 1)
        def _():
            m = acc[...]
            o_ref[...] = (m * g_ref[...]).astype(BF16)
            dg_ref[...] += jnp.sum(m * w_ref[...].astype(F32), axis=0, keepdims=True)

    return pl.pallas_call(
        body, name="proj_bwd_w",
        grid=(N_COL_BLOCKS, nk),
        in_specs=[pl.BlockSpec((tk, D_MODEL), lambda j, i: (i, 0)),
                  pl.BlockSpec((None, tk, COL_BLOCK),
                               lambda j, i: (jnp.minimum(j, 2), jnp.where(j < 3, i, nk - 1), 0)),
                  pl.BlockSpec((tk, COL_BLOCK),
                               lambda j, i: (jnp.where(j >= 3, i, 0), jnp.maximum(j - 3, 0))),
                  pl.BlockSpec((1, D_MODEL), lambda j, i: (0, 0)),
                  pl.BlockSpec((COL_BLOCK, D_MODEL), lambda j, i: (j, 0))],
        out_specs=(pl.BlockSpec((COL_BLOCK, D_MODEL), lambda j, i: (j, 0)),
                   pl.BlockSpec((1, D_MODEL), lambda j, i: (0, 0))),
        out_shape=(jax.ShapeDtypeStruct((D_IN, D_MODEL), BF16), jax.ShapeDtypeStruct((1, D_MODEL), F32)),
        scratch_shapes=[pltpu.VMEM((COL_BLOCK, D_MODEL), F32)],
        compiler_params=_params(56),
    )(xn, dqkv, drest, norm_g, w_in_t)


def _adamw_math(w, g, m, v):
    c1 = 1.0 - ADAM_B1 ** ADAM_STEP
    c2 = 1.0 - ADAM_B2 ** ADAM_STEP
    nm = ADAM_B1 * m + (1.0 - ADAM_B1) * g
    nv = ADAM_B2 * v + (1.0 - ADAM_B2) * (g * g)
    return -ADAM_LR * ((nm / c1) / (jnp.sqrt(nv / c2) + ADAM_EPS) + ADAM_WD * w), nm, nv


def _adamw(name, w, g, m, v, from_chip):
    rows, cols = w.shape
    tr = rows if rows * cols <= 512 * 1024 else next(t for t in range(256, 7, -8) if rows % t == 0)

    def body(w_ref, g_ref, m_ref, v_ref, t_ref, g_out, d_ref, nm_ref, nv_ref):
        gg = g_ref[...]
        for j in range(from_chip.shape[0]):
            gg = gg + t_ref[j].astype(F32)
        g_out[...] = gg
        d_ref[...], nm_ref[...], nv_ref[...] = _adamw_math(w_ref[...], gg, m_ref[...], v_ref[...])

    spec = pl.BlockSpec((tr, cols), lambda i: (i, 0))
    shape = jax.ShapeDtypeStruct((rows, cols), F32)
    return pl.pallas_call(
        body, name=name,
        grid=(rows // tr,),
        in_specs=[spec] * 4 + [pl.BlockSpec((from_chip.shape[0], tr, cols), lambda i: (0, i, 0))],
        out_specs=(spec,) * 4, out_shape=(shape,) * 4,
        compiler_params=_params(32),
    )(w, g, m, v, from_chip)


_SMALL = (("norm_g", (1, D_MODEL)), ("b_gate", (1, 2 * D_MODEL)), ("rel_bias", (N_HEADS, N_REL)),
          ("sgu_ln_g", (1, D_B)), ("sgu_ln_b", (1, D_B)), ("w_s", (N_GROUPS * SGU_CHUNK, SGU_CHUNK)),
          ("b_s", (N_GROUPS, SGU_CHUNK)), ("final_g", (1, D_MODEL)))


def _adamw_small(slabs, ws_all, late_slabs, weights, moments_m, moments_v):
    k = len(_SMALL)

    def total(ref):
        acc = ref[0]
        for d in range(1, N_DEV):
            acc = acc + ref[d]
        return acc

    def body(*refs):
        slab_ref, ws_ref, late_ref = refs[:3]
        w_refs, m_refs, v_refs = refs[3:3 + k], refs[3 + k:3 + 2 * k], refs[3 + 2 * k:3 + 3 * k]
        outs = refs[3 + 3 * k:]
        slab, late = total(slab_ref), total(late_ref)
        grads = {
            "norm_g": late[ROW_NORM_G:ROW_NORM_G + 1, :],
            "b_gate": jnp.concatenate([slab[ROW_B_GATE:ROW_B_GATE + 1, :], slab[ROW_B_GATE + 1:ROW_B_GATE + 2, :]], axis=1),
            "rel_bias": late[ROW_REL:ROW_REL + N_HEADS, :N_REL],
            "sgu_ln_g": slab[ROW_LN_G:ROW_LN_G + 1, :D_B],
            "sgu_ln_b": slab[ROW_LN_B:ROW_LN_B + 1, :D_B],
            "w_s": total(ws_ref),
            "b_s": slab[ROW_B_S:ROW_B_S + N_GROUPS, :SGU_CHUNK],
            "final_g": slab[ROW_FINAL_G:ROW_FINAL_G + 1, :],
        }
        for n, (name, _) in enumerate(_SMALL):
            g = grads[name]
            outs[n][...] = g
            outs[k + n][...], outs[2 * k + n][...], outs[3 * k + n][...] = _adamw_math(
                w_refs[n][...], g, m_refs[n][...], v_refs[n][...])
        outs[4 * k][...] = slab[ROW_LOSS:ROW_LOSS + 1, :1]

    vmem = pl.BlockSpec(memory_space=pltpu.VMEM)
    shapes = tuple(jax.ShapeDtypeStruct(shape, F32) for _, shape in _SMALL)
    return pl.pallas_call(
        body, name="adamw_small",
        out_shape=shapes * 4 + (jax.ShapeDtypeStruct((1, 1), F32),),
        in_specs=[vmem] * (3 + 3 * k), out_specs=tuple([vmem] * (4 * k + 1)),
        compiler_params=_params(16),
    )(slabs, ws_all, late_slabs, *weights, *moments_m, *moments_v)


def _pad_rel(a):
    return jnp.pad(a.reshape(N_HEADS, N_REL), ((0, 0), (0, N_REL_PAD - N_REL)))


def kernel(x, norm_g, w_in, b_gate, rel_bias, sgu_ln_g, sgu_ln_b, w_s, b_s, w_pa, w_pb, w_out, final_g, loss_target, m_norm_g, m_w_in, m_b_gate, m_rel_bias, m_sgu_ln_g, m_sgu_ln_b, m_w_s, m_b_s, m_w_pa, m_w_pb, m_w_out, m_final_g, v_norm_g, v_w_in, v_b_gate, v_rel_bias, v_sgu_ln_g, v_sgu_ln_b, v_w_s, v_b_s, v_w_pa, v_w_pb, v_w_out, v_final_g):
    s = x.shape[1]
    xs = x.reshape(s, D_MODEL)
    tgt = loss_target.reshape(s, D_MODEL)

    bias_table = _bias_table(_pad_rel(rel_bias))
    w_in_t = jnp.swapaxes(w_in[0], 0, 1)
    qkv, x_norm, w_in_t_full = _gather_proj_fwd(xs, norm_g, w_in_t)
    attn_out, g_pa, g_pb, g_out = _attn_fwd(qkv, bias_table, (w_pa[0], w_pb[0], w_out[0]))
    w_pa_full = jnp.transpose(g_pa, (1, 0, 2)).reshape(D_A, D_MODEL)
    w_pb_full = jnp.transpose(g_pb, (1, 0, 2)).reshape(D_B, D_MODEL)
    w_out_full = g_out.reshape(D_MODEL, D_MODEL)

    (dx2, d_attn, drest, dw_out, dw_pa, dw_pb, d_bgate, d_fg, d_lng, d_lnb, d_ws, d_bs, loss_part) = _mid_fwd_bwd(
        xs, tgt, attn_out, qkv, w_pa_full, w_pb_full, w_out_full, b_gate, sgu_ln_g, sgu_ln_b, w_s[0],
        b_s.reshape(N_GROUPS, SGU_CHUNK, 1), final_g.reshape(1, D_MODEL))

    dqkv, dbias, fc_pa, fc_pb, fc_out, slabs, ws_all, own_pa, own_pb, own_out = _attn_bwd(
        qkv, bias_table, d_attn, (dw_pa, dw_pb, dw_out),
        (d_bgate, d_lng, d_lnb, d_fg, loss_part, d_bs, d_ws.reshape(N_GROUPS * SGU_CHUNK, SGU_CHUNK)))
    d_rel = _bias_grad(dbias)
    dw_in_t, d_ng = _proj_bwd_w(x_norm, dqkv, drest, norm_g, w_in_t_full)
    own_in, tc_in = _reduce_chip("reduce_chip_in", (dw_in_t,), (0,))
    grad_x, fc_in, late_slabs = _proj_bwd_x(dqkv, drest, xs, dx2, norm_g, w_in_t_full, tc_in, (d_ng, d_rel))
    big = {"w_in": tuple(jnp.swapaxes(t, 0, 1)[None] for t in _adamw(
        "adamw_w_in", w_in_t, own_in, jnp.swapaxes(m_w_in[0], 0, 1), jnp.swapaxes(v_w_in[0], 0, 1), fc_in))}
    for name, w, g, fc, m, v in (("w_pa", w_pa, own_pa, fc_pa, m_w_pa, v_w_pa),
                                 ("w_pb", w_pb, own_pb, fc_pb, m_w_pb, v_w_pb),
                                 ("w_out", w_out, own_out, fc_out, m_w_out, v_w_out)):
        big[name] = tuple(t[None] for t in _adamw("adamw_" + name, w[0], g, m[0], v[0], fc))

    as_2d = lambda leaves: [a.reshape(shape) for a, (_, shape) in zip(leaves, _SMALL)]
    small_out = _adamw_small(
        slabs, ws_all, late_slabs, as_2d((norm_g, b_gate, rel_bias, sgu_ln_g, sgu_ln_b, w_s, b_s, final_g)),
        as_2d((m_norm_g, m_b_gate, m_rel_bias, m_sgu_ln_g, m_sgu_ln_b, m_w_s, m_b_s, m_final_g)),
        as_2d((v_norm_g, v_b_gate, v_rel_bias, v_sgu_ln_g, v_sgu_ln_b, v_w_s, v_b_s, v_final_g)))
    small_index = {name: n for n, (name, _) in enumerate(_SMALL)}

    def leaf(kind, name, like):
        if name in big:
            return big[name][kind]
        return small_out[kind * len(_SMALL) + small_index[name]].reshape(like.shape)

    weights = (("norm_g", norm_g), ("w_in", w_in), ("b_gate", b_gate), ("rel_bias", rel_bias), ("sgu_ln_g", sgu_ln_g),
               ("sgu_ln_b", sgu_ln_b), ("w_s", w_s), ("b_s", b_s), ("w_pa", w_pa), ("w_pb", w_pb), ("w_out", w_out),
               ("final_g", final_g))
    outs = [small_out[-1].reshape(()), grad_x.reshape(x.shape)]
    for kind in range(4):
        outs.extend(leaf(kind, name, like) for name, like in weights)
    return tuple(outs)
```

```python
import functools
import math

import jax
import jax.numpy as jnp
from jax import lax
from jax.experimental import pallas as pl
from jax.experimental.pallas import tpu as pltpu

F32 = jnp.float32
BF16 = jnp.bfloat16
MESH = pl.DeviceIdType.MESH
N_DEV = 8

D_MODEL = 1024
D_A = 512
D_B = 512
D_IN = 5632
N_HEADS = 8
HEAD_DIM = 64
N_PREV = 8
REL_CLIP = 128
N_REL = 2 * REL_CLIP + 1
N_REL_PAD = 384
SGU_CHUNK = 128
N_GROUPS = 4
EPS = 1e-6
NEG_INF = -1e30
Q_SCALE = HEAD_DIM ** -0.5

Q_BLOCK = 256
K_SPAN = 768
Z_PAD = K_SPAN - Q_BLOCK
ROLL_W = 1024
COL_BLOCK = 512
N_COL_BLOCKS = D_IN // COL_BLOCK
REST = D_IN - 3 * D_A
TOKEN_TILE = 256
SHARD = D_MODEL // N_DEV

ADAM_LR = 0.001
ADAM_B1 = 0.9
ADAM_B2 = 0.999
ADAM_EPS = 1e-08
ADAM_WD = 0.01
ADAM_STEP = 10

GELU_C = math.sqrt(2.0 / math.pi)
GELU_A = 0.044715

NT = (((1,), (1,)), ((), ()))
TN = (((0,), (0,)), ((), ()))
HIGHEST = lax.Precision.HIGHEST


def _params(vmem_mb, **kw):
    return pltpu.CompilerParams(vmem_limit_bytes=vmem_mb * 1024 * 1024, **kw)


def _dot(a, b, dims=None):
    if dims is None:
        return jnp.dot(a, b, preferred_element_type=F32)
    return lax.dot_general(a, b, dims, preferred_element_type=F32)


def _sigmoid(x):
    return 0.5 * jnp.tanh(0.5 * x) + 0.5


def _gelu_and_grad(u):
    u2 = u * u
    t = jnp.tanh(GELU_C * (u + GELU_A * u * u2))
    half = 0.5 * (1.0 + t)
    g = u * half
    dg = half + 0.5 * u * (1.0 - t * t) * (GELU_C * (1.0 + 3.0 * GELU_A * u2))
    return g, dg


def _my_pos():
    return lax.axis_index("x"), lax.axis_index("y"), lax.axis_index("c")


def _flat_id(pos):
    return 4 * pos[0] + 2 * pos[1] + pos[2]


def _other_chips(pos):
    x, y, _ = pos
    return ((1 - x, y), (x, 1 - y), (1 - x, 1 - y))


class _SlotGather:
    def __init__(self, bufs, send_sems, recv_sems, own=None):
        self.bufs, self.send_sems, self.recv_sems = bufs, send_sems, recv_sems
        self.own = own if own is not None else [None] * len(bufs)
        x, y, c = _my_pos()
        self.c, self.me, self.sib = c, (x, y, c), (x, y, 1 - c)
        self.chips = _other_chips(self.me)

    def _copy(self, a, k, block, to):
        slot = _flat_id(block)
        src = self.own[a] if (k < 4 and self.own[a] is not None) else self.bufs[a].at[slot]
        return pltpu.make_async_remote_copy(
            src_ref=src, dst_ref=self.bufs[a].at[slot],
            send_sem=self.send_sems.at[a, k], recv_sem=self.recv_sems.at[a, k], device_id=to, device_id_type=MESH)

    def _own_sends(self):
        n = len(self.bufs)
        return ([self._copy(a, 1 + j, self.me, (*chip, self.c)) for j, chip in enumerate(self.chips) for a in range(n)]
                + [self._copy(a, 0, self.me, self.sib) for a in range(n)])

    def _passes(self):
        return [self._copy(a, 4 + j, (*chip, self.c), self.sib)
                for j, chip in enumerate(self.chips) for a in range(len(self.bufs))]

    def start(self):
        for cp in self._own_sends():
            cp.start()

    def pass_on(self):
        for j, chip in enumerate(self.chips):
            for a in range(len(self.bufs)):
                self._copy(a, 1 + j, (*chip, self.c), self.me).wait_recv()
                self._copy(a, 4 + j, (*chip, self.c), self.sib).start()

    def finish(self):
        for a in range(len(self.bufs)):
            self._copy(a, 0, self.sib, self.me).wait_recv()
            for j, chip in enumerate(self.chips):
                self._copy(a, 4 + j, (*chip, 1 - self.c), self.me).wait_recv()
        for cp in self._own_sends() + self._passes():
            cp.wait_send()


def _owner_copies(to_chip, from_chip, send_sems, recv_sems):
    x, y, c = _my_pos()
    return [pltpu.make_async_remote_copy(
        src_ref=to_chip[a].at[j], dst_ref=from_chip[a].at[j],
        send_sem=send_sems.at[a, j], recv_sem=recv_sems.at[a, j], device_id=(*chip, c), device_id_type=MESH)
        for a in range(len(to_chip)) for j, chip in enumerate(_other_chips((x, y, c)))]


ROW_B_GATE, ROW_LN_G, ROW_LN_B, ROW_FINAL_G, ROW_LOSS, ROW_B_S, SLAB_ROWS = 1, 3, 4, 5, 6, 16, 24
ROW_NORM_G, ROW_REL, LATE_SLAB_ROWS = 0, 8, 16


def _rel_index(e):
    lo, hi = Z_PAD - REL_CLIP, Z_PAD + REL_CLIP
    return jnp.where(e <= lo, 2 * REL_CLIP, jnp.where(e < hi, hi - e, jnp.where(e <= K_SPAN, 0, 2 * REL_CLIP)))


def _bias_table(rel_bias_pad):
    def body(rb_ref, bt_ref):
        c = lax.broadcasted_iota(jnp.int32, (N_REL_PAD, ROLL_W), 1)
        r = lax.broadcasted_iota(jnp.int32, (N_REL_PAD, ROLL_W), 0)
        pick = (r == _rel_index(c)).astype(F32)
        rows = jnp.dot(rb_ref[...], pick, precision=HIGHEST, preferred_element_type=F32)
        qc = lax.broadcasted_iota(jnp.int32, (Q_BLOCK, K_SPAN), 0) >> 6
        kc = lax.broadcasted_iota(jnp.int32, (Q_BLOCK, K_SPAN), 1) >> 6
        band = (kc >= qc) & (kc <= qc + N_PREV)
        for h in range(N_HEADS):
            t = jnp.broadcast_to(rows[h:h + 1, :], (Q_BLOCK, ROLL_W))
            t = pltpu.roll(t, 0, 1, stride=1, stride_axis=0)
            bt_ref[h] = jnp.where(band, t[:, :K_SPAN], NEG_INF)

    return pl.pallas_call(
        body, name="bias_table",
        out_shape=jax.ShapeDtypeStruct((N_HEADS, Q_BLOCK, K_SPAN), F32),
        compiler_params=_params(32),
    )(rel_bias_pad)


def _bias_grad(dbias):
    def body(a_ref, o_ref):
        rr = lax.broadcasted_iota(jnp.int32, (Q_BLOCK, Q_BLOCK), 0)
        cc = lax.broadcasted_iota(jnp.int32, (Q_BLOCK, Q_BLOCK), 1)
        flip = (rr + cc == Q_BLOCK - 1).astype(F32)
        c = lax.broadcasted_iota(jnp.int32, (ROLL_W, N_REL_PAD), 0)
        r = lax.broadcasted_iota(jnp.int32, (ROLL_W, N_REL_PAD), 1)
        e = jnp.where(c >= Q_BLOCK - 1, c - (Q_BLOCK - 1), c + (ROLL_W - Q_BLOCK + 1))
        pick = (r == _rel_index(e)).astype(F32)
        sums = []
        for h in range(N_HEADS):
            a = jnp.dot(flip, a_ref[h], precision=HIGHEST, preferred_element_type=F32)
            a = jnp.concatenate([a, jnp.zeros((Q_BLOCK, ROLL_W - K_SPAN), F32)], axis=1)
            a = pltpu.roll(a, 0, 1, stride=1, stride_axis=0)
            sums.append(jnp.sum(a, axis=0, keepdims=True))
        diag = jnp.concatenate(sums, axis=0)
        o_ref[...] = jnp.dot(diag, pick, precision=HIGHEST, preferred_element_type=F32)

    return pl.pallas_call(
        body, name="bias_grad",
        out_shape=jax.ShapeDtypeStruct((N_HEADS, N_REL_PAD), F32),
        compiler_params=_params(32),
    )(dbias)


def _gather_proj_fwd(x, norm_g, w_in_t):
    s = x.shape[0]
    tm = 512 if s % 512 == 0 else TOKEN_TILE
    nt = s // tm
    n_pad = Z_PAD // tm
    shard_w = w_in_t.shape[0]
    chip_w = 2 * shard_w
    n_chips = N_DEV // 2

    def body(order_ref, x_ref, g_ref, win_hbm, z_ref, xn_ref, wt_hbm, wt, hb, win_f32, send_sems, recv_sems,
             local_sems):
        j = pl.program_id(0)
        i = pl.program_id(1)
        x_, y_, c_ = _my_pos()
        me, sib = (x_, y_, c_), (x_, y_, 1 - c_)
        near = _other_chips(me)
        pick = lambda a, b: tuple(jnp.where(c_ == 0, u, v) for u, v in zip(a, b))
        passed_from, passed_to = pick(near[0], near[1]), pick(near[1], near[0])

        def rows_of(block):
            return wt.at[pl.ds(pl.multiple_of(_flat_id(block) * shard_w, 16), shard_w), :]

        def copy(k, block, to):
            return pltpu.make_async_remote_copy(
                src_ref=rows_of(block), dst_ref=rows_of(block),
                send_sem=send_sems.at[k], recv_sem=recv_sems.at[k], device_id=to, device_id_type=MESH)

        def sends():
            return ([copy(0, me, sib), copy(1, me, (*near[0], c_)), copy(2, me, (*near[1], c_)),
                     copy(3, (*passed_from, c_), (*passed_to, c_))]
                    + [copy(4 + n, (*near[n], c_), sib) for n in range(3)])

        keep = pltpu.make_async_copy(wt, wt_hbm, local_sems.at[0])

        @pl.when((j == 0) & (i == 0))
        def _():
            load = pltpu.make_async_copy(win_hbm, win_f32, local_sems.at[1])
            load.start()
            load.wait()
            rows_of(me)[...] = win_f32[...].astype(BF16)
            for cp in sends()[:3]:
                cp.start()
            copy(0, sib, me).wait_recv()

        @pl.when((j == 1) & (i == 0))
        def _():
            copy(1, (*near[0], c_), me).wait_recv()
            copy(2, (*near[1], c_), me).wait_recv()
            for cp in sends()[3:6]:
                cp.start()
            copy(4, (*near[0], 1 - c_), me).wait_recv()

        @pl.when((j == 2) & (i == 0))
        def _():
            copy(5, (*near[1], 1 - c_), me).wait_recv()

        @pl.when((j == 3) & (i == 0))
        def _():
            copy(3, (*near[2], c_), me).wait_recv()
            copy(6, (*near[2], c_), sib).start()
            copy(6, (*near[2], 1 - c_), me).wait_recv()
            keep.start()

        @pl.when(i < n_pad)
        def _():
            z_ref[...] = jnp.zeros(z_ref.shape, BF16)

        @pl.when(i >= n_pad)
        def _():
            rows = pl.ds(pl.multiple_of((i - n_pad) * tm, tm), tm)

            @pl.when(j == 0)
            def _():
                xf = x_ref[...]
                xn = xf * lax.rsqrt(jnp.mean(xf * xf, axis=-1, keepdims=True) + EPS)
                hb[rows, :] = (xn * g_ref[...]).astype(BF16)
                xn_ref[...] = xn.astype(BF16)

            chip_rows = pl.ds(pl.multiple_of(order_ref[j] * chip_w, 16), chip_w)
            blk = _dot(hb[rows, :], wt[chip_rows, :], NT)
            q_scale = jnp.where(order_ref[j] == 0, Q_SCALE, 1.0).astype(F32)
            z_ref[:, :D_A] = (blk[:, :D_A] * q_scale).astype(BF16)
            z_ref[:, D_A:] = blk[:, D_A:].astype(BF16)

        @pl.when((j == n_chips - 1) & (i == n_pad + nt - 1))
        def _():
            keep.wait()
            for cp in sends():
                cp.wait_send()

    pos = _my_pos()
    order = jnp.stack([2 * cx + cy for cx, cy in ((pos[0], pos[1]),) + _other_chips(pos)]).astype(jnp.int32)
    first_pass = lambda j, i: jnp.where(j == 0, jnp.maximum(i - n_pad, 0), nt - 1)
    grid_spec = pltpu.PrefetchScalarGridSpec(
        num_scalar_prefetch=1,
        grid=(n_chips, n_pad + nt),
        in_specs=[pl.BlockSpec((tm, D_MODEL), lambda j, i, o: (first_pass(j, i), 0)),
                  pl.BlockSpec((1, D_MODEL), lambda j, i, o: (0, 0)),
                  pl.BlockSpec(memory_space=pl.ANY)],
        out_specs=(pl.BlockSpec((tm, chip_w), lambda j, i, o: (i, o[j])),
                   pl.BlockSpec((tm, D_MODEL), lambda j, i, o: (first_pass(j, i), 0)),
                   pl.BlockSpec(memory_space=pl.ANY)),
        scratch_shapes=[pltpu.VMEM((D_IN, D_MODEL), BF16),
                        pltpu.VMEM((s, D_MODEL), BF16), pltpu.VMEM(w_in_t.shape, F32),
                        pltpu.SemaphoreType.DMA((N_DEV - 1,)), pltpu.SemaphoreType.DMA((N_DEV - 1,)),
                        pltpu.SemaphoreType.DMA((2,))])
    return pl.pallas_call(
        body, name="gather_proj_fwd",
        grid_spec=grid_spec,
        out_shape=(jax.ShapeDtypeStruct((Z_PAD + s, D_IN), BF16), jax.ShapeDtypeStruct((s, D_MODEL), BF16),
                   jax.ShapeDtypeStruct((D_IN, D_MODEL), BF16)),
        compiler_params=_params(60),
    )(order, x, norm_g, w_in_t)


def _attn_specs(rows):
    pairs = N_HEADS // 2
    return ([pl.BlockSpec((rows, 128), functools.partial(lambda which, p: (0, which * pairs + p), which))
             for which in range(3)]
            + [pl.BlockSpec((2, Q_BLOCK, K_SPAN), lambda p: (p, 0, 0))])


def _head_masks():
    lane = lax.broadcasted_iota(jnp.int32, (1, 128), 1)
    first = lane < HEAD_DIM
    return (first, jnp.logical_not(first))


def _stack_heads(x, masks):
    zero = jnp.zeros((), x.dtype)
    return jnp.concatenate([jnp.where(m, x, zero) for m in masks], axis=0)


STRIP = 16


def _softmax_strips(s_ref, bias_ref, b):
    valid = lax.broadcasted_iota(jnp.int32, (1, K_SPAN), 1) >= Z_PAD - b * Q_BLOCK
    for t in range(2 * Q_BLOCK // STRIP):
        hh, r = divmod(t * STRIP, Q_BLOCK)
        st = s_ref[t * STRIP:(t + 1) * STRIP, :] + bias_ref[hh, r:r + STRIP, :]
        st = jnp.where(valid, st, NEG_INF)
        e = jnp.exp(st - jnp.max(st, axis=-1, keepdims=True))
        yield e * (1.0 / jnp.sum(e, axis=-1, keepdims=True))


def _side_by_side_strips(strips):
    half = len(strips) // 2
    return jnp.concatenate([jnp.concatenate([a, c], axis=1) for a, c in zip(strips[:half], strips[half:])], axis=0)


def _attn_fwd(qkv, bias_table, shards):
    s = qkv.shape[0] - Z_PAD
    nb = s // Q_BLOCK
    n = len(shards)
    pairs = N_HEADS // 2

    def body(*refs):
        q_ref, k_ref, v_ref, bt_ref = refs[:4]
        shard_refs = refs[4:4 + n]
        o_ref = refs[4 + n]
        slot_refs = refs[5 + n:5 + 2 * n]
        stages = refs[5 + 2 * n:5 + 3 * n]
        s_scr, send_sems, recv_sems, local_sems = refs[5 + 3 * n:]
        p_id = pl.program_id(0)
        gather = _SlotGather(slot_refs, send_sems, recv_sems, own=stages)
        keep = [pltpu.make_async_copy(stages[a], slot_refs[a].at[_flat_id(_my_pos())], local_sems.at[a])
                for a in range(n)]

        @pl.when(p_id == 0)
        def _():
            for a in range(n):
                stages[a][...] = shard_refs[a][...].astype(BF16)
                keep[a].start()
            gather.start()

        @pl.when(p_id == 2)
        def _():
            gather.pass_on()

        masks = _head_masks()

        def scores(b, half):
            r0 = pl.multiple_of(b * Q_BLOCK, Q_BLOCK)
            q2 = _stack_heads(q_ref[pl.ds(r0 + Z_PAD, Q_BLOCK), :], masks)
            s_scr[half] = _dot(q2, k_ref[pl.ds(r0, K_SPAN), :], NT)

        def finish(b, half):
            r0 = pl.multiple_of(b * Q_BLOCK, Q_BLOCK)
            v2 = _stack_heads(v_ref[pl.ds(r0, K_SPAN), :], masks)
            p = [st.astype(BF16) for st in _softmax_strips(s_scr.at[half], bt_ref, b)]
            o_ref[pl.ds(r0, Q_BLOCK), :] = _dot(_side_by_side_strips(p), v2)

        def two_blocks(i, carry):
            b = 2 * i
            scores(b + 1, 1)
            finish(b, 0)
            scores(jnp.minimum(b + 2, nb - 1), 0)
            finish(b + 1, 1)
            return carry

        scores(0, 0)
        lax.fori_loop(0, nb // 2, two_blocks, 0)

        @pl.when(p_id == pairs - 1)
        def _():
            gather.finish()
            for cp in keep:
                cp.wait()

    hbm = pl.BlockSpec(memory_space=pl.ANY)
    return pl.pallas_call(
        body, name="attn_fwd",
        grid=(pairs,),
        in_specs=_attn_specs(s + Z_PAD) + [pl.BlockSpec(a.shape, lambda p: (0, 0)) for a in shards],
        out_specs=(pl.BlockSpec((s, 128), lambda p: (0, p)),) + (hbm,) * n,
        out_shape=(jax.ShapeDtypeStruct((s, D_A), F32),)
        + tuple(jax.ShapeDtypeStruct((N_DEV,) + a.shape, BF16) for a in shards),
        scratch_shapes=[pltpu.VMEM(a.shape, BF16) for a in shards]
        + [pltpu.VMEM((2, 2 * Q_BLOCK, K_SPAN), F32),
           pltpu.SemaphoreType.DMA((n, N_DEV - 1)), pltpu.SemaphoreType.DMA((n, N_DEV - 1)),
           pltpu.SemaphoreType.DMA((n,))],
        compiler_params=_params(48),
    )(qkv, qkv, qkv, bias_table, *shards)


def _fill_slab(stage, rows):
    stage[...] = jnp.zeros(stage.shape, F32)
    for row, ref in rows:
        r, c = ref.shape
        if c > D_MODEL:
            for part in range(c // D_MODEL):
                stage[row + part:row + part + 1, :] = ref[:, part * D_MODEL:(part + 1) * D_MODEL]
        else:
            stage[row:row + r, :c] = ref[...]


def _attn_bwd(qkv, bias_table, d_out, cuts, small):
    s = qkv.shape[0] - Z_PAD
    nb = s // Q_BLOCK
    n = len(cuts)
    pairs = N_HEADS // 2
    ws_shape = small[-1].shape

    def body(*refs):
        q_ref, k_ref, v_ref, bt_ref, do_ref = refs[:5]
        cut_refs = refs[5:5 + n]
        bg_ref, lng_ref, lnb_ref, fg_ref, loss_ref, bs_ref, ws_ref = refs[5 + n:12 + n]
        dqkv_ref, db_ref = refs[12 + n:14 + n]
        from_chip_refs = refs[14 + n:14 + 2 * n]
        slab_land, ws_land = refs[14 + 2 * n:16 + 2 * n]
        own_refs = refs[16 + 2 * n:16 + 3 * n]
        (dk_acc, dv_acc, s_scr, dp_scr, slab_stage, ws_stage, send_sems, recv_sems, gather_send, gather_recv,
         keep_sems) = refs[16 + 3 * n:27 + 3 * n]
        mine, from_sib, to_chip_refs = (refs[27 + k * n:27 + (k + 1) * n] for k in (3, 4, 5))
        sib_send, sib_recv, load_sems = refs[27 + 6 * n:]
        p_id = pl.program_id(0)
        x_, y_, c_ = _my_pos()
        me = _flat_id((x_, y_, c_))
        chips = ((x_, y_),) + _other_chips((x_, y_, c_))

        def to_sibling(a, r):
            return pltpu.make_async_remote_copy(
                src_ref=cut_refs[a].at[_flat_id((*chips[r], 1 - c_))], dst_ref=from_sib[a].at[r],
                send_sem=sib_send.at[a, r], recv_sem=sib_recv.at[a, r], device_id=(x_, y_, 1 - c_),
                device_id_type=MESH)

        def load(a, r):
            return pltpu.make_async_copy(cut_refs[a].at[_flat_id((*chips[r], c_))], mine[a].at[r], load_sems.at[a, r])

        pieces = [(a, r) for r in (1, 2, 3, 0) for a in range(n)]
        gather = _SlotGather([slab_land, ws_land], gather_send, gather_recv, own=[slab_stage, ws_stage])
        keep = [pltpu.make_async_copy(stage, land.at[me], keep_sems.at[k]) for k, (stage, land) in enumerate(
            ((slab_stage, slab_land), (ws_stage, ws_land)))]

        @pl.when(p_id == 0)
        def _():
            _fill_slab(slab_stage, ((ROW_B_GATE, bg_ref), (ROW_LN_G, lng_ref), (ROW_LN_B, lnb_ref),
                                    (ROW_FINAL_G, fg_ref), (ROW_LOSS, loss_ref)))
            eye = (lax.broadcasted_iota(jnp.int32, (SGU_CHUNK, SGU_CHUNK), 0)
                   == lax.broadcasted_iota(jnp.int32, (SGU_CHUNK, SGU_CHUNK), 1))
            for g in range(N_GROUPS):
                row = jnp.sum(jnp.where(eye, bs_ref[g], 0.0), axis=0, keepdims=True)
                slab_stage[ROW_B_S + g:ROW_B_S + g + 1, :SGU_CHUNK] = row
            ws_stage[...] = ws_ref[...]
            for cp in keep:
                cp.start()
            gather.start()
            for a, r in pieces:
                to_sibling(a, r).start()
                load(a, r).start()

        @pl.when(p_id == 1)
        def _():
            for a, r in pieces:
                load(a, r).wait()
                to_sibling(a, r).wait_recv()
                both = mine[a][r].astype(F32) + from_sib[a][r].astype(F32)
                if r == 0:
                    own_refs[a][...] = both
                else:
                    to_chip_refs[a][r - 1] = both.astype(BF16)
            for cp in _owner_copies(to_chip_refs, from_chip_refs, send_sems, recv_sems):
                cp.start()

        @pl.when(p_id == 2)
        def _():
            gather.pass_on()

        dk_acc[...] = jnp.zeros(dk_acc.shape, F32)
        dv_acc[...] = jnp.zeros(dv_acc.shape, F32)
        db_ref[...] = jnp.zeros(db_ref.shape, F32)
        masks = _head_masks()

        def operands(b):
            r0 = pl.multiple_of(b * Q_BLOCK, Q_BLOCK)
            q2 = _stack_heads(q_ref[pl.ds(r0 + Z_PAD, Q_BLOCK), :], masks)
            do2 = _stack_heads(do_ref[pl.ds(r0, Q_BLOCK), :], masks)
            return r0, q2, do2, k_ref[pl.ds(r0, K_SPAN), :]

        def ahead(b, half):
            r0, q2, do2, kcat = operands(b)
            s_scr[half] = _dot(q2, kcat, NT)
            dp_scr[half] = _dot(do2, v_ref[pl.ds(r0, K_SPAN), :], NT)

        def finish(b, half):
            r0, q2, do2, kcat = operands(b)
            p_strips, ds_strips = [], []
            for t, p in enumerate(_softmax_strips(s_scr.at[half], bt_ref, b)):
                hh, r = divmod(t * STRIP, Q_BLOCK)
                dp_t = dp_scr[half, t * STRIP:(t + 1) * STRIP, :]
                ds = p * (dp_t - jnp.sum(p * dp_t, axis=-1, keepdims=True))
                db_ref[hh, r:r + STRIP, :] += ds
                p_strips.append(p.astype(BF16))
                ds_strips.append(ds.astype(BF16))
            dq = _dot(_side_by_side_strips(ds_strips), _stack_heads(kcat, masks))
            dqkv_ref[0, pl.ds(r0, Q_BLOCK), :] = (dq * Q_SCALE).astype(BF16)
            dk_acc[pl.ds(r0, K_SPAN), :] += _dot(jnp.concatenate(ds_strips, axis=0), q2, TN)
            dv_acc[pl.ds(r0, K_SPAN), :] += _dot(jnp.concatenate(p_strips, axis=0), do2, TN)

        def two_blocks(i, carry):
            b = 2 * i
            ahead(b + 1, 1)
            finish(b, 0)
            ahead(jnp.minimum(b + 2, nb - 1), 0)
            finish(b + 1, 1)
            return carry

        ahead(0, 0)
        lax.fori_loop(0, nb // 2, two_blocks, 0)
        dqkv_ref[1] = dk_acc[Z_PAD:, :].astype(BF16)
        dqkv_ref[2] = dv_acc[Z_PAD:, :].astype(BF16)

        @pl.when(p_id == pairs - 1)
        def _():
            gather.finish()
            for cp in keep:
                cp.wait()
            for cp in _owner_copies(to_chip_refs, from_chip_refs, send_sems, recv_sems):
                cp.wait_recv()
                cp.wait_send()
            for a, r in pieces:
                to_sibling(a, r).wait_send()

    hbm = pl.BlockSpec(memory_space=pl.ANY)
    lands = ((N_DEV, SLAB_ROWS, D_MODEL), (N_DEV,) + ws_shape)
    blocks = [c.shape[1:] for c in cuts]
    return pl.pallas_call(
        body, name="attn_bwd",
        grid=(pairs,),
        in_specs=_attn_specs(s + Z_PAD) + [pl.BlockSpec((s, 128), lambda p: (0, p))] + [hbm] * n
        + [pl.BlockSpec(a.shape, functools.partial(lambda nd, p: (0,) * nd, a.ndim)) for a in small],
        out_specs=(pl.BlockSpec((3, s, 128), lambda p: (0, 0, p)),
                   pl.BlockSpec((2, Q_BLOCK, K_SPAN), lambda p: (p, 0, 0))) + (hbm,) * (n + 2)
        + tuple(pl.BlockSpec(b, lambda p: (0, 0)) for b in blocks),
        out_shape=(jax.ShapeDtypeStruct((3, s, D_A), BF16),
                   jax.ShapeDtypeStruct((N_HEADS, Q_BLOCK, K_SPAN), F32))
        + tuple(jax.ShapeDtypeStruct((3,) + b, BF16) for b in blocks)
        + tuple(jax.ShapeDtypeStruct(shape, F32) for shape in lands)
        + tuple(jax.ShapeDtypeStruct(b, F32) for b in blocks),
        scratch_shapes=[pltpu.VMEM((s + Z_PAD, 128), F32), pltpu.VMEM((s + Z_PAD, 128), F32),
                        pltpu.VMEM((2, 2 * Q_BLOCK, K_SPAN), F32), pltpu.VMEM((2, 2 * Q_BLOCK, K_SPAN), F32)]
        + [pltpu.VMEM(shape[1:], F32) for shape in lands]
        + [pltpu.SemaphoreType.DMA((n, 3)), pltpu.SemaphoreType.DMA((n, 3)),
           pltpu.SemaphoreType.DMA((2, N_DEV - 1)), pltpu.SemaphoreType.DMA((2, N_DEV - 1)),
           pltpu.SemaphoreType.DMA((2,))]
        + [pltpu.VMEM((4,) + b, BF16) for b in blocks] * 2 + [pltpu.VMEM((3,) + b, BF16) for b in blocks]
        + [pltpu.SemaphoreType.DMA((n, 4))] * 3,
        compiler_params=_params(56),
    )(qkv, qkv, qkv, bias_table, d_out, *cuts, *small)


def _mid_fwd_bwd(x, target, attn_out, z, w_pa, w_pb, w_out, b_gate, ln_g, ln_b, w_s, b_s, final_g):
    s = x.shape[0]
    tm = TOKEN_TILE
    nt = s // tm

    def body(x_ref, t_ref, oa_ref, ga_ref, ub_ref, vb_ref, gb_ref, ta0_ref, ta1_ref, tb0_ref, tb1_ref,
             wpa_hbm, wpb_hbm, wout_hbm, bg_ref, lng_ref, lnb_ref, ws_ref, bs_ref, fg_ref,
             dx2_ref, doa_ref, dz_ref, dwout_hbm, dwpa_hbm, dwpb_hbm, dbg_ref, dfg_ref, dlng_ref, dlnb_ref, dws_ref,
             dbs_ref, loss_ref,
             wpa, wpb, wout, wmix, acc_out, acc_pa, acc_pb, cut_out, cut_pa, cut_pb, sem):
        i = pl.program_id(0)

        @pl.when(i == 0)
        def _():
            loads = [pltpu.make_async_copy(src, dst, sem.at[n])
                     for n, (src, dst) in enumerate(((wpa_hbm, wpa), (wpb_hbm, wpb), (wout_hbm, wout)))]
            for cp in loads:
                cp.start()
            t_idx = lax.broadcasted_iota(jnp.int32, (SGU_CHUNK, SGU_CHUNK), 0)
            s_idx = lax.broadcasted_iota(jnp.int32, (SGU_CHUNK, SGU_CHUNK), 1)
            for g in range(N_GROUPS):
                wmix[g] = jnp.where(s_idx <= t_idx, ws_ref[g], 0.0).astype(BF16)
            for ref in (acc_out, acc_pa, acc_pb, dbg_ref, dfg_ref, dlng_ref, dlnb_ref, dws_ref, dbs_ref, loss_ref):
                ref[...] = jnp.zeros(ref.shape, F32)
            for cp in loads:
                cp.wait()

        def tile_fwd_bwd(rows):
            g_a = ga_ref[rows, :].astype(F32)
            u_b = ub_ref[rows, :].astype(F32)
            v_b = vb_ref[rows, :].astype(F32)
            g_b = gb_ref[rows, :].astype(F32)
            bg = bg_ref[...]
            sg_a = _sigmoid(g_a)
            silu_a = g_a * sg_a
            o_a = oa_ref[rows, :]
            y_a = (o_a * silu_a).astype(BF16)
            ug, dgelu_u = _gelu_and_grad(u_b)
            vg, dgelu_v = _gelu_and_grad(v_b)
            mu = jnp.mean(vg, axis=-1, keepdims=True)
            vc = vg - mu
            rstd = lax.rsqrt(jnp.mean(vc * vc, axis=-1, keepdims=True) + EPS)
            vhat = vc * rstd
            lng = lng_ref[...]
            vn = (vhat * lng + lnb_ref[...]).astype(BF16)
            sg_b = _sigmoid(g_b)
            silu_b = g_b * sg_b
            subs = [slice(n * SGU_CHUNK, (n + 1) * SGU_CHUNK) for n in range(tm // SGU_CHUNK)]
            mixed = jnp.concatenate([jnp.concatenate(
                [_dot(wmix[g], vn[sub, g * 128:(g + 1) * 128]) + bs_ref[g] for g in range(N_GROUPS)], axis=1)
                for sub in subs], axis=0)
            um = ug * mixed
            y_b = (um * silu_b).astype(BF16)
            gate_a = _sigmoid(jnp.concatenate([ta0_ref[rows, :], ta1_ref[rows, :]], axis=1).astype(F32)
                              + bg[:, :D_MODEL])
            gate_b = _sigmoid(jnp.concatenate([tb0_ref[rows, :], tb1_ref[rows, :]], axis=1).astype(F32)
                              + bg[:, D_MODEL:])
            p_a = _dot(y_a, wpa[...])
            p_b = _dot(y_b, wpb[...])
            merged = (gate_a * p_a + gate_b * p_b).astype(BF16)
            x2 = x_ref[rows, :] + _dot(merged, wout[...])
            r2 = lax.rsqrt(jnp.mean(x2 * x2, axis=-1, keepdims=True) + EPS)
            xh = x2 * r2
            fg = fg_ref[...]
            err = xh * fg - t_ref[rows, :]
            loss_ref[...] += jnp.sum(jnp.sum(err * err, axis=-1, keepdims=True), axis=0, keepdims=True) * (0.5 / D_MODEL)
            dy = err * (1.0 / D_MODEL)
            dfg_ref[...] += jnp.sum(dy * xh, axis=0, keepdims=True)
            gy = dy * fg
            dx2 = r2 * (gy - xh * jnp.mean(gy * xh, axis=-1, keepdims=True))
            dx2_ref[rows, :] = dx2
            dx2b = dx2.astype(BF16)
            dmerged = _dot(dx2b, wout[...], NT)
            acc_out[...] += _dot(merged, dx2b, TN)
            dp_a = dmerged * gate_a
            dp_b = dmerged * gate_b
            dgate_a = dp_a * p_a * (1.0 - gate_a)
            dgate_b = dp_b * p_b * (1.0 - gate_b)
            dbg_ref[:, :D_MODEL] += jnp.sum(dgate_a, axis=0, keepdims=True)
            dbg_ref[:, D_MODEL:] += jnp.sum(dgate_b, axis=0, keepdims=True)
            dz_ref[rows, 2048:3072] = dgate_a.astype(BF16)
            dz_ref[rows, 3072:4096] = dgate_b.astype(BF16)
            dp_ab = dp_a.astype(BF16)
            dp_bb = dp_b.astype(BF16)
            dy_a = _dot(dp_ab, wpa[...], NT)
            dy_b = _dot(dp_bb, wpb[...], NT)
            acc_pa[...] += _dot(y_a, dp_ab, TN)
            acc_pb[...] += _dot(y_b, dp_bb, TN)
            doa_ref[rows, :] = (dy_a * silu_a).astype(BF16)
            dz_ref[rows, 0:512] = (dy_a * o_a * (sg_a * (1.0 + g_a * (1.0 - sg_a)))).astype(BF16)
            dz_ref[rows, 1536:2048] = (dy_b * um * (sg_b * (1.0 + g_b * (1.0 - sg_b)))).astype(BF16)
            dys = dy_b * silu_b
            dz_ref[rows, 512:1024] = (dys * mixed * dgelu_u).astype(BF16)
            dmixed = dys * ug
            dmb = dmixed.astype(BF16)
            dvn_rows = []
            for sub in subs:
                dvn_parts = []
                for g in range(N_GROUPS):
                    cols = slice(g * 128, (g + 1) * 128)
                    dws_ref[g] += _dot(dmb[sub, cols], vn[sub, cols], NT)
                    dbs_ref[g] += jnp.sum(dmixed[sub, cols], axis=-1, keepdims=True)
                    dvn_parts.append(_dot(wmix[g], dmb[sub, cols], TN))
                dvn_rows.append(jnp.concatenate(dvn_parts, axis=1))
            dvn = jnp.concatenate(dvn_rows, axis=0)
            dlng_ref[...] += jnp.sum(dvn * vhat, axis=0, keepdims=True)
            dlnb_ref[...] += jnp.sum(dvn, axis=0, keepdims=True)
            dvh = dvn * lng
            dvg = rstd * (dvh - jnp.mean(dvh, axis=-1, keepdims=True)
                          - vhat * jnp.mean(dvh * vhat, axis=-1, keepdims=True))
            dz_ref[rows, 1024:1536] = (dvg * dgelu_v).astype(BF16)

        tile_fwd_bwd(slice(0, tm))

        @pl.when(i == nt - 1)
        def _():
            t_idx = lax.broadcasted_iota(jnp.int32, (SGU_CHUNK, SGU_CHUNK), 0)
            s_idx = lax.broadcasted_iota(jnp.int32, (SGU_CHUNK, SGU_CHUNK), 1)
            for g in range(N_GROUPS):
                dws_ref[g] = jnp.where(s_idx <= t_idx, dws_ref[g], 0.0)
            for d in range(N_DEV):
                cut_out[d] = acc_out[d * SHARD:(d + 1) * SHARD, :].astype(BF16)
                cut_pa[d] = acc_pa[:, d * SHARD:(d + 1) * SHARD].astype(BF16)
                cut_pb[d] = acc_pb[:, d * SHARD:(d + 1) * SHARD].astype(BF16)
            stores = [pltpu.make_async_copy(src, dst, sem.at[n])
                      for n, (src, dst) in enumerate(((cut_out, dwout_hbm), (cut_pa, dwpa_hbm), (cut_pb, dwpb_hbm)))]
            for cp in stores:
                cp.start()
            for cp in stores:
                cp.wait()

    tile = lambda w: pl.BlockSpec((tm, w), lambda i: (i, 0))
    whole = lambda shape: pl.BlockSpec(shape, lambda i: (0,) * len(shape))
    hbm = pl.BlockSpec(memory_space=pl.ANY)
    cut_shapes = ((N_DEV, SHARD, D_MODEL), (N_DEV, D_A, SHARD), (N_DEV, D_B, SHARD))
    return pl.pallas_call(
        body, name="mid_fwd_bwd",
        grid=(nt,),
        in_specs=[tile(D_MODEL), tile(D_MODEL), tile(D_A)]
        + [pl.BlockSpec((tm, COL_BLOCK), functools.partial(lambda c, i: (i + Z_PAD // tm, c), c))
           for c in range(3, N_COL_BLOCKS)]
        + [hbm, hbm, hbm,
                  whole((1, 2 * D_MODEL)), whole((1, D_B)), whole((1, D_B)),
                  whole((N_GROUPS, SGU_CHUNK, SGU_CHUNK)), whole((N_GROUPS, SGU_CHUNK, 1)), whole((1, D_MODEL))],
        out_specs=(tile(D_MODEL), tile(D_A), tile(REST), hbm, hbm, hbm,
                   whole((1, 2 * D_MODEL)), whole((1, D_MODEL)), whole((1, D_B)), whole((1, D_B)),
                   whole((N_GROUPS, SGU_CHUNK, SGU_CHUNK)), whole((N_GROUPS, SGU_CHUNK, 1)), whole((1, 1))),
        out_shape=(jax.ShapeDtypeStruct((s, D_MODEL), F32), jax.ShapeDtypeStruct((s, D_A), BF16),
                   jax.ShapeDtypeStruct((s, REST), BF16),
                   *(jax.ShapeDtypeStruct(shape, BF16) for shape in cut_shapes),
                   jax.ShapeDtypeStruct((1, 2 * D_MODEL), F32), jax.ShapeDtypeStruct((1, D_MODEL), F32),
                   jax.ShapeDtypeStruct((1, D_B), F32), jax.ShapeDtypeStruct((1, D_B), F32),
                   jax.ShapeDtypeStruct((N_GROUPS, SGU_CHUNK, SGU_CHUNK), F32),
                   jax.ShapeDtypeStruct((N_GROUPS, SGU_CHUNK, 1), F32), jax.ShapeDtypeStruct((1, 1), F32)),
        scratch_shapes=[pltpu.VMEM((D_A, D_MODEL), BF16), pltpu.VMEM((D_B, D_MODEL), BF16),
                        pltpu.VMEM((D_MODEL, D_MODEL), BF16), pltpu.VMEM((N_GROUPS, SGU_CHUNK, SGU_CHUNK), BF16),
                        pltpu.VMEM((D_MODEL, D_MODEL), F32), pltpu.VMEM((D_A, D_MODEL), F32),
                        pltpu.VMEM((D_B, D_MODEL), F32)]
        + [pltpu.VMEM(shape, BF16) for shape in cut_shapes]
        + [pltpu.SemaphoreType.DMA((3,))],
        compiler_params=_params(56),
    )(x, target, attn_out, *([z] * (N_COL_BLOCKS - 3)), w_pa, w_pb, w_out, b_gate, ln_g, ln_b, w_s, b_s, final_g)


def _proj_bwd_x(dqkv, drest, x, dx2, norm_g, w_in_t, dw, small):
    s = x.shape[0]
    tm = 512 if s % 512 == 0 else TOKEN_TILE
    nt = s // tm
    rows = dw.shape[0] // N_DEV
    half = D_MODEL // 2
    left, right = slice(0, half), slice(half, D_MODEL)

    def body(dqkv_ref, dr_ref, x_ref, dx2_ref, g_ref, w_hbm, dw_hbm, ng_ref, rel_ref,
             dx_ref, fc_ref, slab_land, own_hbm,
             w, tc_ref, slab_stage, via_x, via_y, mine, out_x, out_y, from_sib, my_block, own_stage,
             sem, send_sems, recv_sems, gather_send, gather_recv, keep_sems, sib_send, sib_recv):
        i = pl.program_id(0)
        x_, y_, c_ = _my_pos()
        me = _flat_id((x_, y_, c_))
        xn, yn = (1 - x_, y_, c_), (x_, 1 - y_, c_)
        chips = ((x_, y_),) + _other_chips((x_, y_, c_))

        def block(pos):
            return dw_hbm.at[pl.ds(pl.multiple_of(_flat_id(pos) * rows, 16), rows), :]

        def to_sibling(r):
            return pltpu.make_async_remote_copy(
                src_ref=block((*chips[r], 1 - c_)), dst_ref=from_sib.at[r], send_sem=sib_send.at[r],
                recv_sem=sib_recv.at[r], device_id=(x_, y_, 1 - c_), device_id_type=MESH)

        own_out = pltpu.make_async_copy(own_stage, own_hbm, keep_sems.at[3])
        gather = _SlotGather([slab_land], gather_send, gather_recv, own=[slab_stage])
        keep = [pltpu.make_async_copy(slab_stage, slab_land.at[me], keep_sems.at[0])]

        def copy(k, src, dst, to):
            return pltpu.make_async_remote_copy(src_ref=src, dst_ref=dst, send_sem=send_sems.at[k],
                                                recv_sem=recv_sems.at[k], device_id=to, device_id_type=MESH)

        first = [copy(0, tc_ref.at[0, :, left], fc_ref.at[0, :, left], xn), copy(1, tc_ref.at[2, :, left], via_x, xn),
                 copy(2, tc_ref.at[1, :, right], fc_ref.at[1, :, right], yn), copy(3, tc_ref.at[2, :, right], via_y, yn)]
        second = [copy(4, out_y, fc_ref.at[1, :, left], yn), copy(5, out_x, fc_ref.at[0, :, right], xn)]

        def add_and_send(arrival, landed, own_half, stage, onward):
            load = pltpu.make_async_copy(own_half, mine, sem)
            load.start()
            arrival.wait_recv()
            load.wait()
            stage[...] = (mine[...].astype(F32) + landed[...].astype(F32)).astype(BF16)
            onward.start()

        @pl.when(i == 0)
        def _():
            cp = pltpu.make_async_copy(w_hbm, w, sem)
            cp.start()
            _fill_slab(slab_stage, ((ROW_NORM_G, ng_ref), (ROW_REL, rel_ref)))
            for cp_keep in keep:
                cp_keep.start()
            gather.start()
            for r in (1, 2, 3, 0):
                to_sibling(r).start()
            cp.wait()

        @pl.when(i == 1)
        def _():
            for r in (1, 2, 3, 0):
                load = pltpu.make_async_copy(block((*chips[r], c_)), my_block, keep_sems.at[2])
                load.start()
                to_sibling(r).wait_recv()
                load.wait()
                both = my_block[...].astype(F32) + from_sib[r].astype(F32)
                if r == 0:
                    own_stage[...] = both
                else:
                    tc_ref[r - 1] = both.astype(BF16)
            own_out.start()
            for rc in first:
                rc.start()

        @pl.when(i == (5 * nt) // 8)
        def _():
            gather.pass_on()
            add_and_send(first[1], via_x, tc_ref.at[1, :, left], out_y, second[0])
            add_and_send(first[3], via_y, tc_ref.at[0, :, right], out_x, second[1])

        dh = None
        for c in range(N_COL_BLOCKS):
            dz = dqkv_ref[c] if c < 3 else dr_ref[:, (c - 3) * COL_BLOCK:(c - 2) * COL_BLOCK]
            part = _dot(dz, w[c * COL_BLOCK:(c + 1) * COL_BLOCK, :])
            dh = part if dh is None else dh + part
        xf = x_ref[...]
        r = lax.rsqrt(jnp.mean(xf * xf, axis=-1, keepdims=True) + EPS)
        xn = xf * r
        gh = dh * g_ref[...]
        dx_ref[...] = r * (gh - xn * jnp.mean(gh * xn, axis=-1, keepdims=True)) + dx2_ref[...]

        @pl.when(i == nt - 1)
        def _():
            gather.finish()
            for cp_keep in keep:
                cp_keep.wait()
            for k in (0, 2, 4, 5):
                (first + second)[k].wait_recv()
            for rc in first + second:
                rc.wait_send()
            own_out.wait()
            for r in range(4):
                to_sibling(r).wait_send()

    hbm = pl.BlockSpec(memory_space=pl.ANY)
    whole = lambda a: pl.BlockSpec(a.shape, lambda i: (0,) * a.ndim)
    return pl.pallas_call(
        body, name="proj_bwd_x",
        grid=(nt,),
        in_specs=[pl.BlockSpec((3, tm, D_A), lambda i: (0, i, 0)),
                  pl.BlockSpec((tm, REST), lambda i: (i, 0)),
                  pl.BlockSpec((tm, D_MODEL), lambda i: (i, 0)),
                  pl.BlockSpec((tm, D_MODEL), lambda i: (i, 0)),
                  pl.BlockSpec((1, D_MODEL), lambda i: (0, 0)),
                  hbm, hbm] + [whole(a) for a in small],
        out_specs=(pl.BlockSpec((tm, D_MODEL), lambda i: (i, 0)), hbm, hbm, hbm),
        out_shape=(jax.ShapeDtypeStruct((s, D_MODEL), F32), jax.ShapeDtypeStruct((2, rows, D_MODEL), BF16),
                   jax.ShapeDtypeStruct((N_DEV, LATE_SLAB_ROWS, D_MODEL), F32),
                   jax.ShapeDtypeStruct((rows, D_MODEL), F32)),
        scratch_shapes=[pltpu.VMEM((D_IN, D_MODEL), BF16), pltpu.VMEM((3, rows, D_MODEL), BF16),
                        pltpu.VMEM((LATE_SLAB_ROWS, D_MODEL), F32)]
        + [pltpu.VMEM((rows, half), BF16)] * 5
        + [pltpu.VMEM((4, rows, D_MODEL), BF16), pltpu.VMEM((rows, D_MODEL), BF16), pltpu.VMEM((rows, D_MODEL), F32)]
        + [pltpu.SemaphoreType.DMA, pltpu.SemaphoreType.DMA((6,)), pltpu.SemaphoreType.DMA((6,)),
           pltpu.SemaphoreType.DMA((1, N_DEV - 1)), pltpu.SemaphoreType.DMA((1, N_DEV - 1)),
           pltpu.SemaphoreType.DMA((4,)), pltpu.SemaphoreType.DMA((4,)), pltpu.SemaphoreType.DMA((4,))],
        compiler_params=_params(60),
    )(dqkv, drest, x, dx2, norm_g, w_in_t, dw, *small)


def _proj_bwd_w(xn, dqkv, drest, norm_g, w_in_t):
    s = xn.shape[0]
    tk = s
    nk = s // tk

    def body(xn_ref, dqkv_ref, dr_ref, g_ref, w_ref, o_ref, dg_ref, acc):
        j = pl.program_id(0)
        i = pl.program_id(1)

        @pl.when((j == 0) & (i == 0))
        def _():
            dg_ref[...] = jnp.zeros(dg_ref.shape, F32)

        @pl.when(i == 0)
        def _():
            acc[...] = jnp.zeros(acc.shape, F32)

        @pl.when(j < 3)
        def _():
            acc[...] += _dot(dqkv_ref[...], xn_ref[...], TN)

        @pl.when(j >= 3)
        def _():
            acc[...] += _dot(dr_ref[...], xn_ref[...], TN)

        @pl.when(i == nk - 1)
        def _():
            m = acc[...]
            o_ref[...] = (m * g_ref[...]).astype(BF16)
            dg_ref[...] += jnp.sum(m * w_ref[...].astype(F32), axis=0, keepdims=True)

    return pl.pallas_call(
        body, name="proj_bwd_w",
        grid=(N_COL_BLOCKS, nk),
        in_specs=[pl.BlockSpec((tk, D_MODEL), lambda j, i: (i, 0)),
                  pl.BlockSpec((None, tk, COL_BLOCK),
                               lambda j, i: (jnp.minimum(j, 2), jnp.where(j < 3, i, nk - 1), 0)),
                  pl.BlockSpec((tk, COL_BLOCK),
                               lambda j, i: (jnp.where(j >= 3, i, 0), jnp.maximum(j - 3, 0))),
                  pl.BlockSpec((1, D_MODEL), lambda j, i: (0, 0)),
                  pl.BlockSpec((COL_BLOCK, D_MODEL), lambda j, i: (j, 0))],
        out_specs=(pl.BlockSpec((COL_BLOCK, D_MODEL), lambda j, i: (j, 0)),
                   pl.BlockSpec((1, D_MODEL), lambda j, i: (0, 0))),
        out_shape=(jax.ShapeDtypeStruct((D_IN, D_MODEL), BF16), jax.ShapeDtypeStruct((1, D_MODEL), F32)),
        scratch_shapes=[pltpu.VMEM((COL_BLOCK, D_MODEL), F32)],
        compiler_params=_params(56),
    )(xn, dqkv, drest, norm_g, w_in_t)


def _adamw_math(w, g, m, v):
    c1 = 1.0 - ADAM_B1 ** ADAM_STEP
    c2 = 1.0 - ADAM_B2 ** ADAM_STEP
    nm = ADAM_B1 * m + (1.0 - ADAM_B1) * g
    nv = ADAM_B2 * v + (1.0 - ADAM_B2) * (g * g)
    return -ADAM_LR * ((nm / c1) / (jnp.sqrt(nv / c2) + ADAM_EPS) + ADAM_WD * w), nm, nv


def _adamw(name, w, g, m, v, from_chip):
    rows, cols = w.shape
    tr = rows if rows * cols <= 512 * 1024 else next(t for t in range(256, 7, -8) if rows % t == 0)

    def body(w_ref, g_ref, m_ref, v_ref, t_ref, g_out, d_ref, nm_ref, nv_ref):
        gg = g_ref[...]
        for j in range(from_chip.shape[0]):
            gg = gg + t_ref[j].astype(F32)
        g_out[...] = gg
        d_ref[...], nm_ref[...], nv_ref[...] = _adamw_math(w_ref[...], gg, m_ref[...], v_ref[...])

    spec = pl.BlockSpec((tr, cols), lambda i: (i, 0))
    shape = jax.ShapeDtypeStruct((rows, cols), F32)
    return pl.pallas_call(
        body, name=name,
        grid=(rows // tr,),
        in_specs=[spec] * 4 + [pl.BlockSpec((from_chip.shape[0], tr, cols), lambda i: (0, i, 0))],
        out_specs=(spec,) * 4, out_shape=(shape,) * 4,
        compiler_params=_params(32),
    )(w, g, m, v, from_chip)


_SMALL = (("norm_g", (1, D_MODEL)), ("b_gate", (1, 2 * D_MODEL)), ("rel_bias", (N_HEADS, N_REL)),
          ("sgu_ln_g", (1, D_B)), ("sgu_ln_b", (1, D_B)), ("w_s", (N_GROUPS * SGU_CHUNK, SGU_CHUNK)),
          ("b_s", (N_GROUPS, SGU_CHUNK)), ("final_g", (1, D_MODEL)))


def _adamw_small(slabs, ws_all, late_slabs, weights, moments_m, moments_v):
    k = len(_SMALL)

    def total(ref):
        acc = ref[0]
        for d in range(1, N_DEV):
            acc = acc + ref[d]
        return acc

    def body(*refs):
        slab_ref, ws_ref, late_ref = refs[:3]
        w_refs, m_refs, v_refs = refs[3:3 + k], refs[3 + k:3 + 2 * k], refs[3 + 2 * k:3 + 3 * k]
        outs = refs[3 + 3 * k:]
        slab, late = total(slab_ref), total(late_ref)
        grads = {
            "norm_g": late[ROW_NORM_G:ROW_NORM_G + 1, :],
            "b_gate": jnp.concatenate([slab[ROW_B_GATE:ROW_B_GATE + 1, :], slab[ROW_B_GATE + 1:ROW_B_GATE + 2, :]], axis=1),
            "rel_bias": late[ROW_REL:ROW_REL + N_HEADS, :N_REL],
            "sgu_ln_g": slab[ROW_LN_G:ROW_LN_G + 1, :D_B],
            "sgu_ln_b": slab[ROW_LN_B:ROW_LN_B + 1, :D_B],
            "w_s": total(ws_ref),
            "b_s": slab[ROW_B_S:ROW_B_S + N_GROUPS, :SGU_CHUNK],
            "final_g": slab[ROW_FINAL_G:ROW_FINAL_G + 1, :],
        }
        for n, (name, _) in enumerate(_SMALL):
            g = grads[name]
            outs[n][...] = g
            outs[k + n][...], outs[2 * k + n][...], outs[3 * k + n][...] = _adamw_math(
                w_refs[n][...], g, m_refs[n][...], v_refs[n][...])
        outs[4 * k][...] = slab[ROW_LOSS:ROW_LOSS + 1, :1]

    vmem = pl.BlockSpec(memory_space=pltpu.VMEM)
    shapes = tuple(jax.ShapeDtypeStruct(shape, F32) for _, shape in _SMALL)
    return pl.pallas_call(
        body, name="adamw_small",
        out_shape=shapes * 4 + (jax.ShapeDtypeStruct((1, 1), F32),),
        in_specs=[vmem] * (3 + 3 * k), out_specs=tuple([vmem] * (4 * k + 1)),
        compiler_params=_params(16),
    )(slabs, ws_all, late_slabs, *weights, *moments_m, *moments_v)


def _pad_rel(a):
    return jnp.pad(a.reshape(N_HEADS, N_REL), ((0, 0), (0, N_REL_PAD - N_REL)))


def kernel(x, norm_g, w_in, b_gate, rel_bias, sgu_ln_g, sgu_ln_b, w_s, b_s, w_pa, w_pb, w_out, final_g, loss_target, m_norm_g, m_w_in, m_b_gate, m_rel_bias, m_sgu_ln_g, m_sgu_ln_b, m_w_s, m_b_s, m_w_pa, m_w_pb, m_w_out, m_final_g, v_norm_g, v_w_in, v_b_gate, v_rel_bias, v_sgu_ln_g, v_sgu_ln_b, v_w_s, v_b_s, v_w_pa, v_w_pb, v_w_out, v_final_g):
    s = x.shape[1]
    xs = x.reshape(s, D_MODEL)
    tgt = loss_target.reshape(s, D_MODEL)

    bias_table = _bias_table(_pad_rel(rel_bias))
    w_in_t = jnp.swapaxes(w_in[0], 0, 1)
    qkv, x_norm, w_in_t_full = _gather_proj_fwd(xs, norm_g, w_in_t)
    attn_out, g_pa, g_pb, g_out = _attn_fwd(qkv, bias_table, (w_pa[0], w_pb[0], w_out[0]))
    w_pa_full = jnp.transpose(g_pa, (1, 0, 2)).reshape(D_A, D_MODEL)
    w_pb_full = jnp.transpose(g_pb, (1, 0, 2)).reshape(D_B, D_MODEL)
    w_out_full = g_out.reshape(D_MODEL, D_MODEL)

    (dx2, d_attn, drest, dw_out, dw_pa, dw_pb, d_bgate, d_fg, d_lng, d_lnb, d_ws, d_bs, loss_part) = _mid_fwd_bwd(
        xs, tgt, attn_out, qkv, w_pa_full, w_pb_full, w_out_full, b_gate, sgu_ln_g, sgu_ln_b, w_s[0],
        b_s.reshape(N_GROUPS, SGU_CHUNK, 1), final_g.reshape(1, D_MODEL))

    dqkv, dbias, fc_pa, fc_pb, fc_out, slabs, ws_all, own_pa, own_pb, own_out = _attn_bwd(
        qkv, bias_table, d_attn, (dw_pa, dw_pb, dw_out),
        (d_bgate, d_lng, d_lnb, d_fg, loss_part, d_bs, d_ws.reshape(N_GROUPS * SGU_CHUNK, SGU_CHUNK)))
    d_rel = _bias_grad(dbias)
    dw_in_t, d_ng = _proj_bwd_w(x_norm, dqkv, drest, norm_g, w_in_t_full)
    grad_x, fc_in, late_slabs, own_in = _proj_bwd_x(dqkv, drest, xs, dx2, norm_g, w_in_t_full, dw_in_t, (d_ng, d_rel))
    big = {"w_in": tuple(jnp.swapaxes(t, 0, 1)[None] for t in _adamw(
        "adamw_w_in", w_in_t, own_in, jnp.swapaxes(m_w_in[0], 0, 1), jnp.swapaxes(v_w_in[0], 0, 1), fc_in))}
    for name, w, g, fc, m, v in (("w_pa", w_pa, own_pa, fc_pa, m_w_pa, v_w_pa),
                                 ("w_pb", w_pb, own_pb, fc_pb, m_w_pb, v_w_pb),
                                 ("w_out", w_out, own_out, fc_out, m_w_out, v_w_out)):
        big[name] = tuple(t[None] for t in _adamw("adamw_" + name, w[0], g, m[0], v[0], fc))

    as_2d = lambda leaves: [a.reshape(shape) for a, (_, shape) in zip(leaves, _SMALL)]
    small_out = _adamw_small(
        slabs, ws_all, late_slabs, as_2d((norm_g, b_gate, rel_bias, sgu_ln_g, sgu_ln_b, w_s, b_s, final_g)),
        as_2d((m_norm_g, m_b_gate, m_rel_bias, m_sgu_ln_g, m_sgu_ln_b, m_w_s, m_b_s, m_final_g)),
        as_2d((v_norm_g, v_b_gate, v_rel_bias, v_sgu_ln_g, v_sgu_ln_b, v_w_s, v_b_s, v_final_g)))
    small_index = {name: n for n, (name, _) in enumerate(_SMALL)}

    def leaf(kind, name, like):
        if name in big:
            return big[name][kind]
        return small_out[kind * len(_SMALL) + small_index[name]].reshape(like.shape)

    weights = (("norm_g", norm_g), ("w_in", w_in), ("b_gate", b_gate), ("rel_bias", rel_bias), ("sgu_ln_g", sgu_ln_g),
               ("sgu_ln_b", sgu_ln_b), ("w_s", w_s), ("b_s", b_s), ("w_pa", w_pa), ("w_pb", w_pb), ("w_out", w_out),
               ("final_g", final_g))
    outs = [small_out[-1].reshape(()), grad_x.reshape(x.shape)]
    for kind in range(4):
        outs.extend(leaf(kind, name, like) for name, like in weights)
    return tuple(outs)
```

```python
import functools
import math

import jax
import jax.numpy as jnp
from jax import lax
from jax.experimental import pallas as pl
from jax.experimental.pallas import tpu as pltpu

F32 = jnp.float32
BF16 = jnp.bfloat16
MESH = pl.DeviceIdType.MESH
N_DEV = 8

D_MODEL = 1024
D_A = 512
D_B = 512
D_IN = 5632
N_HEADS = 8
HEAD_DIM = 64
N_PREV = 8
REL_CLIP = 128
N_REL = 2 * REL_CLIP + 1
N_REL_PAD = 384
SGU_CHUNK = 128
N_GROUPS = 4
EPS = 1e-6
NEG_INF = -1e30
Q_SCALE = HEAD_DIM ** -0.5

Q_BLOCK = 256
K_SPAN = 768
Z_PAD = K_SPAN - Q_BLOCK
ROLL_W = 1024
COL_BLOCK = 512
N_COL_BLOCKS = D_IN // COL_BLOCK
REST = D_IN - 3 * D_A
TOKEN_TILE = 256
SHARD = D_MODEL // N_DEV

ADAM_LR = 0.001
ADAM_B1 = 0.9
ADAM_B2 = 0.999
ADAM_EPS = 1e-08
ADAM_WD = 0.01
ADAM_STEP = 10

GELU_C = math.sqrt(2.0 / math.pi)
GELU_A = 0.044715

NT = (((1,), (1,)), ((), ()))
TN = (((0,), (0,)), ((), ()))
HIGHEST = lax.Precision.HIGHEST


def _params(vmem_mb, **kw):
    return pltpu.CompilerParams(vmem_limit_bytes=vmem_mb * 1024 * 1024, **kw)


def _dot(a, b, dims=None):
    if dims is None:
        return jnp.dot(a, b, preferred_element_type=F32)
    return lax.dot_general(a, b, dims, preferred_element_type=F32)


def _sigmoid(x):
    return 0.5 * jnp.tanh(0.5 * x) + 0.5


def _gelu_and_grad(u):
    u2 = u * u
    t = jnp.tanh(GELU_C * (u + GELU_A * u * u2))
    half = 0.5 * (1.0 + t)
    g = u * half
    dg = half + 0.5 * u * (1.0 - t * t) * (GELU_C * (1.0 + 3.0 * GELU_A * u2))
    return g, dg


def _my_pos():
    return lax.axis_index("x"), lax.axis_index("y"), lax.axis_index("c")


def _flat_id(pos):
    return 4 * pos[0] + 2 * pos[1] + pos[2]


def _other_chips(pos):
    x, y, _ = pos
    return ((1 - x, y), (x, 1 - y), (1 - x, 1 - y))


class _SlotGather:
    def __init__(self, bufs, send_sems, recv_sems, own=None):
        self.bufs, self.send_sems, self.recv_sems = bufs, send_sems, recv_sems
        self.own = own if own is not None else [None] * len(bufs)
        x, y, c = _my_pos()
        self.c, self.me, self.sib = c, (x, y, c), (x, y, 1 - c)
        self.chips = _other_chips(self.me)

    def _copy(self, a, k, block, to):
        slot = _flat_id(block)
        src = self.own[a] if (k < 4 and self.own[a] is not None) else self.bufs[a].at[slot]
        return pltpu.make_async_remote_copy(
            src_ref=src, dst_ref=self.bufs[a].at[slot],
            send_sem=self.send_sems.at[a, k], recv_sem=self.recv_sems.at[a, k], device_id=to, device_id_type=MESH)

    def _own_sends(self):
        n = len(self.bufs)
        return ([self._copy(a, 1 + j, self.me, (*chip, self.c)) for j, chip in enumerate(self.chips) for a in range(n)]
                + [self._copy(a, 0, self.me, self.sib) for a in range(n)])

    def _passes(self):
        return [self._copy(a, 4 + j, (*chip, self.c), self.sib)
                for j, chip in enumerate(self.chips) for a in range(len(self.bufs))]

    def start(self):
        for cp in self._own_sends():
            cp.start()

    def pass_on(self):
        for j, chip in enumerate(self.chips):
            for a in range(len(self.bufs)):
                self._copy(a, 1 + j, (*chip, self.c), self.me).wait_recv()
                self._copy(a, 4 + j, (*chip, self.c), self.sib).start()

    def finish(self):
        for a in range(len(self.bufs)):
            self._copy(a, 0, self.sib, self.me).wait_recv()
            for j, chip in enumerate(self.chips):
                self._copy(a, 4 + j, (*chip, 1 - self.c), self.me).wait_recv()
        for cp in self._own_sends() + self._passes():
            cp.wait_send()


def _owner_copies(to_chip, from_chip, send_sems, recv_sems):
    x, y, c = _my_pos()
    return [pltpu.make_async_remote_copy(
        src_ref=to_chip[a].at[j], dst_ref=from_chip[a].at[j],
        send_sem=send_sems.at[a, j], recv_sem=recv_sems.at[a, j], device_id=(*chip, c), device_id_type=MESH)
        for a in range(len(to_chip)) for j, chip in enumerate(_other_chips((x, y, c)))]


ROW_B_GATE, ROW_LN_G, ROW_LN_B, ROW_FINAL_G, ROW_LOSS, ROW_B_S, SLAB_ROWS = 1, 3, 4, 5, 6, 16, 24
ROW_NORM_G, ROW_REL, LATE_SLAB_ROWS = 0, 8, 16


def _rel_index(e):
    lo, hi = Z_PAD - REL_CLIP, Z_PAD + REL_CLIP
    return jnp.where(e <= lo, 2 * REL_CLIP, jnp.where(e < hi, hi - e, jnp.where(e <= K_SPAN, 0, 2 * REL_CLIP)))


def _bias_table(rel_bias_pad):
    def body(rb_ref, bt_ref):
        c = lax.broadcasted_iota(jnp.int32, (N_REL_PAD, ROLL_W), 1)
        r = lax.broadcasted_iota(jnp.int32, (N_REL_PAD, ROLL_W), 0)
        pick = (r == _rel_index(c)).astype(F32)
        rows = jnp.dot(rb_ref[...], pick, precision=HIGHEST, preferred_element_type=F32)
        qc = lax.broadcasted_iota(jnp.int32, (Q_BLOCK, K_SPAN), 0) >> 6
        kc = lax.broadcasted_iota(jnp.int32, (Q_BLOCK, K_SPAN), 1) >> 6
        band = (kc >= qc) & (kc <= qc + N_PREV)
        for h in range(N_HEADS):
            t = jnp.broadcast_to(rows[h:h + 1, :], (Q_BLOCK, ROLL_W))
            t = pltpu.roll(t, 0, 1, stride=1, stride_axis=0)
            bt_ref[h] = jnp.where(band, t[:, :K_SPAN], NEG_INF)

    return pl.pallas_call(
        body, name="bias_table",
        out_shape=jax.ShapeDtypeStruct((N_HEADS, Q_BLOCK, K_SPAN), F32),
        compiler_params=_params(32),
    )(rel_bias_pad)


def _bias_grad(dbias):
    def body(a_ref, o_ref):
        rr = lax.broadcasted_iota(jnp.int32, (Q_BLOCK, Q_BLOCK), 0)
        cc = lax.broadcasted_iota(jnp.int32, (Q_BLOCK, Q_BLOCK), 1)
        flip = (rr + cc == Q_BLOCK - 1).astype(F32)
        c = lax.broadcasted_iota(jnp.int32, (ROLL_W, N_REL_PAD), 0)
        r = lax.broadcasted_iota(jnp.int32, (ROLL_W, N_REL_PAD), 1)
        e = jnp.where(c >= Q_BLOCK - 1, c - (Q_BLOCK - 1), c + (ROLL_W - Q_BLOCK + 1))
        pick = (r == _rel_index(e)).astype(F32)
        sums = []
        for h in range(N_HEADS):
            a = jnp.dot(flip, a_ref[h], precision=HIGHEST, preferred_element_type=F32)
            a = jnp.concatenate([a, jnp.zeros((Q_BLOCK, ROLL_W - K_SPAN), F32)], axis=1)
            a = pltpu.roll(a, 0, 1, stride=1, stride_axis=0)
            sums.append(jnp.sum(a, axis=0, keepdims=True))
        diag = jnp.concatenate(sums, axis=0)
        o_ref[...] = jnp.dot(diag, pick, precision=HIGHEST, preferred_element_type=F32)

    return pl.pallas_call(
        body, name="bias_grad",
        out_shape=jax.ShapeDtypeStruct((N_HEADS, N_REL_PAD), F32),
        compiler_params=_params(32),
    )(dbias)


def _gather_proj_fwd(x, norm_g, w_in_t):
    s = x.shape[0]
    tm = 512 if s % 512 == 0 else TOKEN_TILE
    nt = s // tm
    n_pad = Z_PAD // tm
    shard_w = w_in_t.shape[0]
    chip_w = 2 * shard_w
    n_chips = N_DEV // 2

    def body(order_ref, x_ref, g_ref, win_hbm, z_ref, xn_ref, wt_hbm, wt, hb, win_f32, send_sems, recv_sems,
             local_sems):
        j = pl.program_id(0)
        i = pl.program_id(1)
        x_, y_, c_ = _my_pos()
        me, sib = (x_, y_, c_), (x_, y_, 1 - c_)
        near = _other_chips(me)
        pick = lambda a, b: tuple(jnp.where(c_ == 0, u, v) for u, v in zip(a, b))
        passed_from, passed_to = pick(near[0], near[1]), pick(near[1], near[0])

        def rows_of(block):
            return wt.at[pl.ds(pl.multiple_of(_flat_id(block) * shard_w, 16), shard_w), :]

        def copy(k, block, to):
            return pltpu.make_async_remote_copy(
                src_ref=rows_of(block), dst_ref=rows_of(block),
                send_sem=send_sems.at[k], recv_sem=recv_sems.at[k], device_id=to, device_id_type=MESH)

        def sends():
            return ([copy(0, me, sib), copy(1, me, (*near[0], c_)), copy(2, me, (*near[1], c_)),
                     copy(3, (*passed_from, c_), (*passed_to, c_))]
                    + [copy(4 + n, (*near[n], c_), sib) for n in range(3)])

        keep = pltpu.make_async_copy(wt, wt_hbm, local_sems.at[0])

        @pl.when((j == 0) & (i == 0))
        def _():
            load = pltpu.make_async_copy(win_hbm, win_f32, local_sems.at[1])
            load.start()
            load.wait()
            rows_of(me)[...] = win_f32[...].astype(BF16)
            for cp in sends()[:3]:
                cp.start()
            copy(0, sib, me).wait_recv()

        @pl.when((j == 1) & (i == 0))
        def _():
            copy(1, (*near[0], c_), me).wait_recv()
            copy(2, (*near[1], c_), me).wait_recv()
            for cp in sends()[3:6]:
                cp.start()
            copy(4, (*near[0], 1 - c_), me).wait_recv()

        @pl.when((j == 2) & (i == 0))
        def _():
            copy(5, (*near[1], 1 - c_), me).wait_recv()

        @pl.when((j == 3) & (i == 0))
        def _():
            copy(3, (*near[2], c_), me).wait_recv()
            copy(6, (*near[2], c_), sib).start()
            copy(6, (*near[2], 1 - c_), me).wait_recv()
            keep.start()

        @pl.when(i < n_pad)
        def _():
            z_ref[...] = jnp.zeros(z_ref.shape, BF16)

        @pl.when(i >= n_pad)
        def _():
            rows = pl.ds(pl.multiple_of((i - n_pad) * tm, tm), tm)

            @pl.when(j == 0)
            def _():
                xf = x_ref[...]
                xn = xf * lax.rsqrt(jnp.mean(xf * xf, axis=-1, keepdims=True) + EPS)
                hb[rows, :] = (xn * g_ref[...]).astype(BF16)
                xn_ref[...] = xn.astype(BF16)

            chip_rows = pl.ds(pl.multiple_of(order_ref[j] * chip_w, 16), chip_w)
            blk = _dot(hb[rows, :], wt[chip_rows, :], NT)
            q_scale = jnp.where(order_ref[j] == 0, Q_SCALE, 1.0).astype(F32)
            z_ref[:, :D_A] = (blk[:, :D_A] * q_scale).astype(BF16)
            z_ref[:, D_A:] = blk[:, D_A:].astype(BF16)

        @pl.when((j == n_chips - 1) & (i == n_pad + nt - 1))
        def _():
            keep.wait()
            for cp in sends():
                cp.wait_send()

    pos = _my_pos()
    order = jnp.stack([2 * cx + cy for cx, cy in ((pos[0], pos[1]),) + _other_chips(pos)]).astype(jnp.int32)
    first_pass = lambda j, i: jnp.where(j == 0, jnp.maximum(i - n_pad, 0), nt - 1)
    grid_spec = pltpu.PrefetchScalarGridSpec(
        num_scalar_prefetch=1,
        grid=(n_chips, n_pad + nt),
        in_specs=[pl.BlockSpec((tm, D_MODEL), lambda j, i, o: (first_pass(j, i), 0)),
                  pl.BlockSpec((1, D_MODEL), lambda j, i, o: (0, 0)),
                  pl.BlockSpec(memory_space=pl.ANY)],
        out_specs=(pl.BlockSpec((tm, chip_w), lambda j, i, o: (i, o[j])),
                   pl.BlockSpec((tm, D_MODEL), lambda j, i, o: (first_pass(j, i), 0)),
                   pl.BlockSpec(memory_space=pl.ANY)),
        scratch_shapes=[pltpu.VMEM((D_IN, D_MODEL), BF16),
                        pltpu.VMEM((s, D_MODEL), BF16), pltpu.VMEM(w_in_t.shape, F32),
                        pltpu.SemaphoreType.DMA((N_DEV - 1,)), pltpu.SemaphoreType.DMA((N_DEV - 1,)),
                        pltpu.SemaphoreType.DMA((2,))])
    return pl.pallas_call(
        body, name="gather_proj_fwd",
        grid_spec=grid_spec,
        out_shape=(jax.ShapeDtypeStruct((Z_PAD + s, D_IN), BF16), jax.ShapeDtypeStruct((s, D_MODEL), BF16),
                   jax.ShapeDtypeStruct((D_IN, D_MODEL), BF16)),
        compiler_params=_params(60),
    )(order, x, norm_g, w_in_t)


def _attn_specs(rows):
    pairs = N_HEADS // 2
    return ([pl.BlockSpec((rows, 128), functools.partial(lambda which, p: (0, which * pairs + p), which))
             for which in range(3)]
            + [pl.BlockSpec((2, Q_BLOCK, K_SPAN), lambda p: (p, 0, 0))])


def _head_masks():
    lane = lax.broadcasted_iota(jnp.int32, (1, 128), 1)
    first = lane < HEAD_DIM
    return (first, jnp.logical_not(first))


def _stack_heads(x, masks):
    zero = jnp.zeros((), x.dtype)
    return jnp.concatenate([jnp.where(m, x, zero) for m in masks], axis=0)


STRIP = 16


def _softmax_strips(s_ref, bias_ref, b):
    valid = lax.broadcasted_iota(jnp.int32, (1, K_SPAN), 1) >= Z_PAD - b * Q_BLOCK
    for t in range(2 * Q_BLOCK // STRIP):
        hh, r = divmod(t * STRIP, Q_BLOCK)
        st = s_ref[t * STRIP:(t + 1) * STRIP, :] + bias_ref[hh, r:r + STRIP, :]
        st = jnp.where(valid, st, NEG_INF)
        e = jnp.exp(st - jnp.max(st, axis=-1, keepdims=True))
        yield e * (1.0 / jnp.sum(e, axis=-1, keepdims=True))


def _side_by_side_strips(strips):
    half = len(strips) // 2
    return jnp.concatenate([jnp.concatenate([a, c], axis=1) for a, c in zip(strips[:half], strips[half:])], axis=0)


def _attn_fwd(qkv, bias_table, shards):
    s = qkv.shape[0] - Z_PAD
    nb = s // Q_BLOCK
    n = len(shards)
    pairs = N_HEADS // 2

    def body(*refs):
        q_ref, k_ref, v_ref, bt_ref = refs[:4]
        shard_refs = refs[4:4 + n]
        o_ref = refs[4 + n]
        slot_refs = refs[5 + n:5 + 2 * n]
        stages = refs[5 + 2 * n:5 + 3 * n]
        s_scr, send_sems, recv_sems, local_sems = refs[5 + 3 * n:]
        p_id = pl.program_id(0)
        gather = _SlotGather(slot_refs, send_sems, recv_sems, own=stages)
        keep = [pltpu.make_async_copy(stages[a], slot_refs[a].at[_flat_id(_my_pos())], local_sems.at[a])
                for a in range(n)]

        @pl.when(p_id == 0)
        def _():
            for a in range(n):
                stages[a][...] = shard_refs[a][...].astype(BF16)
                keep[a].start()
            gather.start()

        @pl.when(p_id == 2)
        def _():
            gather.pass_on()

        masks = _head_masks()

        def scores(b, half):
            r0 = pl.multiple_of(b * Q_BLOCK, Q_BLOCK)
            q2 = _stack_heads(q_ref[pl.ds(r0 + Z_PAD, Q_BLOCK), :], masks)
            s_scr[half] = _dot(q2, k_ref[pl.ds(r0, K_SPAN), :], NT)

        def finish(b, half):
            r0 = pl.multiple_of(b * Q_BLOCK, Q_BLOCK)
            v2 = _stack_heads(v_ref[pl.ds(r0, K_SPAN), :], masks)
            p = [st.astype(BF16) for st in _softmax_strips(s_scr.at[half], bt_ref, b)]
            o_ref[pl.ds(r0, Q_BLOCK), :] = _dot(_side_by_side_strips(p), v2)

        def two_blocks(i, carry):
            b = 2 * i
            scores(b + 1, 1)
            finish(b, 0)
            scores(jnp.minimum(b + 2, nb - 1), 0)
            finish(b + 1, 1)
            return carry

        scores(0, 0)
        lax.fori_loop(0, nb // 2, two_blocks, 0)

        @pl.when(p_id == pairs - 1)
        def _():
            gather.finish()
            for cp in keep:
                cp.wait()

    hbm = pl.BlockSpec(memory_space=pl.ANY)
    return pl.pallas_call(
        body, name="attn_fwd",
        grid=(pairs,),
        in_specs=_attn_specs(s + Z_PAD) + [pl.BlockSpec(a.shape, lambda p: (0, 0)) for a in shards],
        out_specs=(pl.BlockSpec((s, 128), lambda p: (0, p)),) + (hbm,) * n,
        out_shape=(jax.ShapeDtypeStruct((s, D_A), F32),)
        + tuple(jax.ShapeDtypeStruct((N_DEV,) + a.shape, BF16) for a in shards),
        scratch_shapes=[pltpu.VMEM(a.shape, BF16) for a in shards]
        + [pltpu.VMEM((2, 2 * Q_BLOCK, K_SPAN), F32),
           pltpu.SemaphoreType.DMA((n, N_DEV - 1)), pltpu.SemaphoreType.DMA((n, N_DEV - 1)),
           pltpu.SemaphoreType.DMA((n,))],
        compiler_params=_params(48),
    )(qkv, qkv, qkv, bias_table, *shards)


def _fill_slab(stage, rows):
    stage[...] = jnp.zeros(stage.shape, F32)
    for row, ref in rows:
        r, c = ref.shape
        if c > D_MODEL:
            for part in range(c // D_MODEL):
                stage[row + part:row + part + 1, :] = ref[:, part * D_MODEL:(part + 1) * D_MODEL]
        else:
            stage[row:row + r, :c] = ref[...]


def _attn_bwd(qkv, bias_table, d_out, cuts, small):
    s = qkv.shape[0] - Z_PAD
    nb = s // Q_BLOCK
    n = len(cuts)
    pairs = N_HEADS // 2
    ws_shape = small[-1].shape

    def body(*refs):
        q_ref, k_ref, v_ref, bt_ref, do_ref = refs[:5]
        cut_refs = refs[5:5 + n]
        bg_ref, lng_ref, lnb_ref, fg_ref, loss_ref, bs_ref, ws_ref = refs[5 + n:12 + n]
        dqkv_ref, db_ref = refs[12 + n:14 + n]
        from_chip_refs = refs[14 + n:14 + 2 * n]
        slab_land, ws_land = refs[14 + 2 * n:16 + 2 * n]
        own_refs = refs[16 + 2 * n:16 + 3 * n]
        (dk_acc, dv_acc, s_scr, dp_scr, slab_stage, ws_stage, send_sems, recv_sems, gather_send, gather_recv,
         keep_sems) = refs[16 + 3 * n:27 + 3 * n]
        mine, from_sib, to_chip_refs = (refs[27 + k * n:27 + (k + 1) * n] for k in (3, 4, 5))
        sib_send, sib_recv, load_sems = refs[27 + 6 * n:]
        p_id = pl.program_id(0)
        x_, y_, c_ = _my_pos()
        me = _flat_id((x_, y_, c_))
        chips = ((x_, y_),) + _other_chips((x_, y_, c_))

        def to_sibling(a, r):
            return pltpu.make_async_remote_copy(
                src_ref=cut_refs[a].at[_flat_id((*chips[r], 1 - c_))], dst_ref=from_sib[a].at[r],
                send_sem=sib_send.at[a, r], recv_sem=sib_recv.at[a, r], device_id=(x_, y_, 1 - c_),
                device_id_type=MESH)

        def load(a, r):
            return pltpu.make_async_copy(cut_refs[a].at[_flat_id((*chips[r], c_))], mine[a].at[r], load_sems.at[a, r])

        pieces = [(a, r) for r in (1, 2, 3, 0) for a in range(n)]
        gather = _SlotGather([slab_land, ws_land], gather_send, gather_recv, own=[slab_stage, ws_stage])
        keep = [pltpu.make_async_copy(stage, land.at[me], keep_sems.at[k]) for k, (stage, land) in enumerate(
            ((slab_stage, slab_land), (ws_stage, ws_land)))]

        @pl.when(p_id == 0)
        def _():
            _fill_slab(slab_stage, ((ROW_B_GATE, bg_ref), (ROW_LN_G, lng_ref), (ROW_LN_B, lnb_ref),
                                    (ROW_FINAL_G, fg_ref), (ROW_LOSS, loss_ref)))
            eye = (lax.broadcasted_iota(jnp.int32, (SGU_CHUNK, SGU_CHUNK), 0)
                   == lax.broadcasted_iota(jnp.int32, (SGU_CHUNK, SGU_CHUNK), 1))
            for g in range(N_GROUPS):
                row = jnp.sum(jnp.where(eye, bs_ref[g], 0.0), axis=0, keepdims=True)
                slab_stage[ROW_B_S + g:ROW_B_S + g + 1, :SGU_CHUNK] = row
            ws_stage[...] = ws_ref[...]
            for cp in keep:
                cp.start()
            gather.start()
            for a, r in pieces:
                to_sibling(a, r).start()
                load(a, r).start()

        @pl.when(p_id == 1)
        def _():
            for a, r in pieces:
                load(a, r).wait()
                to_sibling(a, r).wait_recv()
                both = mine[a][r].astype(F32) + from_sib[a][r].astype(F32)
                if r == 0:
                    own_refs[a][...] = both
                else:
                    to_chip_refs[a][r - 1] = both.astype(BF16)
            for cp in _owner_copies(to_chip_refs, from_chip_refs, send_sems, recv_sems):
                cp.start()

        @pl.when(p_id == 2)
        def _():
            gather.pass_on()

        dk_acc[...] = jnp.zeros(dk_acc.shape, F32)
        dv_acc[...] = jnp.zeros(dv_acc.shape, F32)
        db_ref[...] = jnp.zeros(db_ref.shape, F32)
        masks = _head_masks()

        def operands(b):
            r0 = pl.multiple_of(b * Q_BLOCK, Q_BLOCK)
            q2 = _stack_heads(q_ref[pl.ds(r0 + Z_PAD, Q_BLOCK), :], masks)
            do2 = _stack_heads(do_ref[pl.ds(r0, Q_BLOCK), :], masks)
            return r0, q2, do2, k_ref[pl.ds(r0, K_SPAN), :]

        def ahead(b, half):
            r0, q2, do2, kcat = operands(b)
            s_scr[half] = _dot(q2, kcat, NT)
            dp_scr[half] = _dot(do2, v_ref[pl.ds(r0, K_SPAN), :], NT)

        def finish(b, half):
            r0, q2, do2, kcat = operands(b)
            p_strips, ds_strips = [], []
            for t, p in enumerate(_softmax_strips(s_scr.at[half], bt_ref, b)):
                hh, r = divmod(t * STRIP, Q_BLOCK)
                dp_t = dp_scr[half, t * STRIP:(t + 1) * STRIP, :]
                ds = p * (dp_t - jnp.sum(p * dp_t, axis=-1, keepdims=True))
                db_ref[hh, r:r + STRIP, :] += ds
                p_strips.append(p.astype(BF16))
                ds_strips.append(ds.astype(BF16))
            dq = _dot(_side_by_side_strips(ds_strips), _stack_heads(kcat, masks))
            dqkv_ref[0, pl.ds(r0, Q_BLOCK), :] = (dq * Q_SCALE).astype(BF16)
            dk_acc[pl.ds(r0, K_SPAN), :] += _dot(jnp.concatenate(ds_strips, axis=0), q2, TN)
            dv_acc[pl.ds(r0, K_SPAN), :] += _dot(jnp.concatenate(p_strips, axis=0), do2, TN)

        def two_blocks(i, carry):
            b = 2 * i
            ahead(b + 1, 1)
            finish(b, 0)
            ahead(jnp.minimum(b + 2, nb - 1), 0)
            finish(b + 1, 1)
            return carry

        ahead(0, 0)
        lax.fori_loop(0, nb // 2, two_blocks, 0)
        dqkv_ref[1] = dk_acc[Z_PAD:, :].astype(BF16)
        dqkv_ref[2] = dv_acc[Z_PAD:, :].astype(BF16)

        @pl.when(p_id == pairs - 1)
        def _():
            gather.finish()
            for cp in keep:
                cp.wait()
            for cp in _owner_copies(to_chip_refs, from_chip_refs, send_sems, recv_sems):
                cp.wait_recv()
                cp.wait_send()
            for a, r in pieces:
                to_sibling(a, r).wait_send()

    hbm = pl.BlockSpec(memory_space=pl.ANY)
    lands = ((N_DEV, SLAB_ROWS, D_MODEL), (N_DEV,) + ws_shape)
    blocks = [c.shape[1:] for c in cuts]
    return pl.pallas_call(
        body, name="attn_bwd",
        grid=(pairs,),
        in_specs=_attn_specs(s + Z_PAD) + [pl.BlockSpec((s, 128), lambda p: (0, p))] + [hbm] * n
        + [pl.BlockSpec(a.shape, functools.partial(lambda nd, p: (0,) * nd, a.ndim)) for a in small],
        out_specs=(pl.BlockSpec((3, s, 128), lambda p: (0, 0, p)),
                   pl.BlockSpec((2, Q_BLOCK, K_SPAN), lambda p: (p, 0, 0))) + (hbm,) * (n + 2)
        + tuple(pl.BlockSpec(b, lambda p: (0, 0)) for b in blocks),
        out_shape=(jax.ShapeDtypeStruct((3, s, D_A), BF16),
                   jax.ShapeDtypeStruct((N_HEADS, Q_BLOCK, K_SPAN), F32))
        + tuple(jax.ShapeDtypeStruct((3,) + b, BF16) for b in blocks)
        + tuple(jax.ShapeDtypeStruct(shape, F32) for shape in lands)
        + tuple(jax.ShapeDtypeStruct(b, F32) for b in blocks),
        scratch_shapes=[pltpu.VMEM((s + Z_PAD, 128), F32), pltpu.VMEM((s + Z_PAD, 128), F32),
                        pltpu.VMEM((2, 2 * Q_BLOCK, K_SPAN), F32), pltpu.VMEM((2, 2 * Q_BLOCK, K_SPAN), F32)]
        + [pltpu.VMEM(shape[1:], F32) for shape in lands]
        + [pltpu.SemaphoreType.DMA((n, 3)), pltpu.SemaphoreType.DMA((n, 3)),
           pltpu.SemaphoreType.DMA((2, N_DEV - 1)), pltpu.SemaphoreType.DMA((2, N_DEV - 1)),
           pltpu.SemaphoreType.DMA((2,))]
        + [pltpu.VMEM((4,) + b, BF16) for b in blocks] * 2 + [pltpu.VMEM((3,) + b, BF16) for b in blocks]
        + [pltpu.SemaphoreType.DMA((n, 4))] * 3,
        compiler_params=_params(56),
    )(qkv, qkv, qkv, bias_table, d_out, *cuts, *small)


def _mid_fwd_bwd(x, target, attn_out, z, w_pa, w_pb, w_out, b_gate, ln_g, ln_b, w_s, b_s, final_g):
    s = x.shape[0]
    tm = TOKEN_TILE
    nt = s // tm

    def body(x_ref, t_ref, oa_ref, ga_ref, ub_ref, vb_ref, gb_ref, ta0_ref, ta1_ref, tb0_ref, tb1_ref,
             wpa_hbm, wpb_hbm, wout_hbm, bg_ref, lng_ref, lnb_ref, ws_ref, bs_ref, fg_ref,
             dx2_ref, doa_ref, dz_ref, dwout_hbm, dwpa_hbm, dwpb_hbm, dbg_ref, dfg_ref, dlng_ref, dlnb_ref, dws_ref,
             dbs_ref, loss_ref,
             wpa, wpb, wout, wmix, acc_out, acc_pa, acc_pb, cut_out, cut_pa, cut_pb, sem):
        i = pl.program_id(0)

        @pl.when(i == 0)
        def _():
            loads = [pltpu.make_async_copy(src, dst, sem.at[n])
                     for n, (src, dst) in enumerate(((wpa_hbm, wpa), (wpb_hbm, wpb), (wout_hbm, wout)))]
            for cp in loads:
                cp.start()
            t_idx = lax.broadcasted_iota(jnp.int32, (SGU_CHUNK, SGU_CHUNK), 0)
            s_idx = lax.broadcasted_iota(jnp.int32, (SGU_CHUNK, SGU_CHUNK), 1)
            for g in range(N_GROUPS):
                wmix[g] = jnp.where(s_idx <= t_idx, ws_ref[g], 0.0).astype(BF16)
            for ref in (acc_out, acc_pa, acc_pb, dbg_ref, dfg_ref, dlng_ref, dlnb_ref, dws_ref, dbs_ref, loss_ref):
                ref[...] = jnp.zeros(ref.shape, F32)
            for cp in loads:
                cp.wait()

        def tile_fwd_bwd(rows):
            g_a = ga_ref[rows, :].astype(F32)
            u_b = ub_ref[rows, :].astype(F32)
            v_b = vb_ref[rows, :].astype(F32)
            g_b = gb_ref[rows, :].astype(F32)
            bg = bg_ref[...]
            sg_a = _sigmoid(g_a)
            silu_a = g_a * sg_a
            o_a = oa_ref[rows, :]
            y_a = (o_a * silu_a).astype(BF16)
            ug, dgelu_u = _gelu_and_grad(u_b)
            vg, dgelu_v = _gelu_and_grad(v_b)
            mu = jnp.mean(vg, axis=-1, keepdims=True)
            vc = vg - mu
            rstd = lax.rsqrt(jnp.mean(vc * vc, axis=-1, keepdims=True) + EPS)
            vhat = vc * rstd
            lng = lng_ref[...]
            vn = (vhat * lng + lnb_ref[...]).astype(BF16)
            sg_b = _sigmoid(g_b)
            silu_b = g_b * sg_b
            subs = [slice(n * SGU_CHUNK, (n + 1) * SGU_CHUNK) for n in range(tm // SGU_CHUNK)]
            mixed = jnp.concatenate([jnp.concatenate(
                [_dot(wmix[g], vn[sub, g * 128:(g + 1) * 128]) + bs_ref[g] for g in range(N_GROUPS)], axis=1)
                for sub in subs], axis=0)
            um = ug * mixed
            y_b = (um * silu_b).astype(BF16)
            gate_a = _sigmoid(jnp.concatenate([ta0_ref[rows, :], ta1_ref[rows, :]], axis=1).astype(F32)
                              + bg[:, :D_MODEL])
            gate_b = _sigmoid(jnp.concatenate([tb0_ref[rows, :], tb1_ref[rows, :]], axis=1).astype(F32)
                              + bg[:, D_MODEL:])
            p_a = _dot(y_a, wpa[...])
            p_b = _dot(y_b, wpb[...])
            merged = (gate_a * p_a + gate_b * p_b).astype(BF16)
            x2 = x_ref[rows, :] + _dot(merged, wout[...])
            r2 = lax.rsqrt(jnp.mean(x2 * x2, axis=-1, keepdims=True) + EPS)
            xh = x2 * r2
            fg = fg_ref[...]
            err = xh * fg - t_ref[rows, :]
            loss_ref[...] += jnp.sum(jnp.sum(err * err, axis=-1, keepdims=True), axis=0, keepdims=True) * (0.5 / D_MODEL)
            dy = err * (1.0 / D_MODEL)
            dfg_ref[...] += jnp.sum(dy * xh, axis=0, keepdims=True)
            gy = dy * fg
            dx2 = r2 * (gy - xh * jnp.mean(gy * xh, axis=-1, keepdims=True))
            dx2_ref[rows, :] = dx2
            dx2b = dx2.astype(BF16)
            dmerged = _dot(dx2b, wout[...], NT)
            acc_out[...] += _dot(merged, dx2b, TN)
            dp_a = dmerged * gate_a
            dp_b = dmerged * gate_b
            dgate_a = dp_a * p_a * (1.0 - gate_a)
            dgate_b = dp_b * p_b * (1.0 - gate_b)
            dbg_ref[:, :D_MODEL] += jnp.sum(dgate_a, axis=0, keepdims=True)
            dbg_ref[:, D_MODEL:] += jnp.sum(dgate_b, axis=0, keepdims=True)
            dz_ref[rows, 2048:3072] = dgate_a.astype(BF16)
            dz_ref[rows, 3072:4096] = dgate_b.astype(BF16)
            dp_ab = dp_a.astype(BF16)
            dp_bb = dp_b.astype(BF16)
            dy_a = _dot(dp_ab, wpa[...], NT)
            dy_b = _dot(dp_bb, wpb[...], NT)
            acc_pa[...] += _dot(y_a, dp_ab, TN)
            acc_pb[...] += _dot(y_b, dp_bb, TN)
            doa_ref[rows, :] = (dy_a * silu_a).astype(BF16)
            dz_ref[rows, 0:512] = (dy_a * o_a * (sg_a * (1.0 + g_a * (1.0 - sg_a)))).astype(BF16)
            dz_ref[rows, 1536:2048] = (dy_b * um * (sg_b * (1.0 + g_b * (1.0 - sg_b)))).astype(BF16)
            dys = dy_b * silu_b
            dz_ref[rows, 512:1024] = (dys * mixed * dgelu_u).astype(BF16)
            dmixed = dys * ug
            dmb = dmixed.astype(BF16)
            dvn_rows = []
            for sub in subs:
                dvn_parts = []
                for g in range(N_GROUPS):
                    cols = slice(g * 128, (g + 1) * 128)
                    dws_ref[g] += _dot(dmb[sub, cols], vn[sub, cols], NT)
                    dbs_ref[g] += jnp.sum(dmixed[sub, cols], axis=-1, keepdims=True)
                    dvn_parts.append(_dot(wmix[g], dmb[sub, cols], TN))
                dvn_rows.append(jnp.concatenate(dvn_parts, axis=1))
            dvn = jnp.concatenate(dvn_rows, axis=0)
            dlng_ref[...] += jnp.sum(dvn * vhat, axis=0, keepdims=True)
            dlnb_ref[...] += jnp.sum(dvn, axis=0, keepdims=True)
            dvh = dvn * lng
            dvg = rstd * (dvh - jnp.mean(dvh, axis=-1, keepdims=True)
                          - vhat * jnp.mean(dvh * vhat, axis=-1, keepdims=True))
            dz_ref[rows, 1024:1536] = (dvg * dgelu_v).astype(BF16)

        tile_fwd_bwd(slice(0, tm))

        @pl.when(i == nt - 1)
        def _():
            t_idx = lax.broadcasted_iota(jnp.int32, (SGU_CHUNK, SGU_CHUNK), 0)
            s_idx = lax.broadcasted_iota(jnp.int32, (SGU_CHUNK, SGU_CHUNK), 1)
            for g in range(N_GROUPS):
                dws_ref[g] = jnp.where(s_idx <= t_idx, dws_ref[g], 0.0)
            for d in range(N_DEV):
                cut_out[d] = acc_out[d * SHARD:(d + 1) * SHARD, :].astype(BF16)
                cut_pa[d] = acc_pa[:, d * SHARD:(d + 1) * SHARD].astype(BF16)
                cut_pb[d] = acc_pb[:, d * SHARD:(d + 1) * SHARD].astype(BF16)
            stores = [pltpu.make_async_copy(src, dst, sem.at[n])
                      for n, (src, dst) in enumerate(((cut_out, dwout_hbm), (cut_pa, dwpa_hbm), (cut_pb, dwpb_hbm)))]
            for cp in stores:
                cp.start()
            for cp in stores:
                cp.wait()

    tile = lambda w: pl.BlockSpec((tm, w), lambda i: (i, 0))
    whole = lambda shape: pl.BlockSpec(shape, lambda i: (0,) * len(shape))
    hbm = pl.BlockSpec(memory_space=pl.ANY)
    cut_shapes = ((N_DEV, SHARD, D_MODEL), (N_DEV, D_A, SHARD), (N_DEV, D_B, SHARD))
    return pl.pallas_call(
        body, name="mid_fwd_bwd",
        grid=(nt,),
        in_specs=[tile(D_MODEL), tile(D_MODEL), tile(D_A)]
        + [pl.BlockSpec((tm, COL_BLOCK), functools.partial(lambda c, i: (i + Z_PAD // tm, c), c))
           for c in range(3, N_COL_BLOCKS)]
        + [hbm, hbm, hbm,
                  whole((1, 2 * D_MODEL)), whole((1, D_B)), whole((1, D_B)),
                  whole((N_GROUPS, SGU_CHUNK, SGU_CHUNK)), whole((N_GROUPS, SGU_CHUNK, 1)), whole((1, D_MODEL))],
        out_specs=(tile(D_MODEL), tile(D_A), tile(REST), hbm, hbm, hbm,
                   whole((1, 2 * D_MODEL)), whole((1, D_MODEL)), whole((1, D_B)), whole((1, D_B)),
                   whole((N_GROUPS, SGU_CHUNK, SGU_CHUNK)), whole((N_GROUPS, SGU_CHUNK, 1)), whole((1, 1))),
        out_shape=(jax.ShapeDtypeStruct((s, D_MODEL), F32), jax.ShapeDtypeStruct((s, D_A), BF16),
                   jax.ShapeDtypeStruct((s, REST), BF16),
                   *(jax.ShapeDtypeStruct(shape, BF16) for shape in cut_shapes),
                   jax.ShapeDtypeStruct((1, 2 * D_MODEL), F32), jax.ShapeDtypeStruct((1, D_MODEL), F32),
                   jax.ShapeDtypeStruct((1, D_B), F32), jax.ShapeDtypeStruct((1, D_B), F32),
                   jax.ShapeDtypeStruct((N_GROUPS, SGU_CHUNK, SGU_CHUNK), F32),
                   jax.ShapeDtypeStruct((N_GROUPS, SGU_CHUNK, 1), F32), jax.ShapeDtypeStruct((1, 1), F32)),
        scratch_shapes=[pltpu.VMEM((D_A, D_MODEL), BF16), pltpu.VMEM((D_B, D_MODEL), BF16),
                        pltpu.VMEM((D_MODEL, D_MODEL), BF16), pltpu.VMEM((N_GROUPS, SGU_CHUNK, SGU_CHUNK), BF16),
                        pltpu.VMEM((D_MODEL, D_MODEL), F32), pltpu.VMEM((D_A, D_MODEL), F32),
                        pltpu.VMEM((D_B, D_MODEL), F32)]
        + [pltpu.VMEM(shape, BF16) for shape in cut_shapes]
        + [pltpu.SemaphoreType.DMA((3,))],
        compiler_params=_params(56),
    )(x, target, attn_out, *([z] * (N_COL_BLOCKS - 3)), w_pa, w_pb, w_out, b_gate, ln_g, ln_b, w_s, b_s, final_g)


def _proj_bwd_x(dqkv, drest, x, dx2, norm_g, w_in_t, dw, small):
    s = x.shape[0]
    tm = 512 if s % 512 == 0 else TOKEN_TILE
    nt = s // tm
    rows = dw.shape[0] // N_DEV
    half = D_MODEL // 2
    left, right = slice(0, half), slice(half, D_MODEL)

    def body(dqkv_ref, dr_ref, x_ref, dx2_ref, g_ref, w_hbm, dw_hbm, ng_ref, rel_ref,
             dx_ref, fc_ref, slab_land, own_hbm,
             w, tc_ref, slab_stage, via_x, via_y, mine, out_x, out_y, from_sib, my_block, own_stage,
             sem, send_sems, recv_sems, gather_send, gather_recv, keep_sems, sib_send, sib_recv):
        i = pl.program_id(0)
        x_, y_, c_ = _my_pos()
        me = _flat_id((x_, y_, c_))
        xn, yn = (1 - x_, y_, c_), (x_, 1 - y_, c_)
        chips = ((x_, y_),) + _other_chips((x_, y_, c_))

        def block(pos):
            return dw_hbm.at[pl.ds(pl.multiple_of(_flat_id(pos) * rows, 16), rows), :]

        def to_sibling(r):
            return pltpu.make_async_remote_copy(
                src_ref=block((*chips[r], 1 - c_)), dst_ref=from_sib.at[r], send_sem=sib_send.at[r],
                recv_sem=sib_recv.at[r], device_id=(x_, y_, 1 - c_), device_id_type=MESH)

        own_out = pltpu.make_async_copy(own_stage, own_hbm, keep_sems.at[3])
        gather = _SlotGather([slab_land], gather_send, gather_recv, own=[slab_stage])
        keep = [pltpu.make_async_copy(slab_stage, slab_land.at[me], keep_sems.at[0])]

        def copy(k, src, dst, to):
            return pltpu.make_async_remote_copy(src_ref=src, dst_ref=dst, send_sem=send_sems.at[k],
                                                recv_sem=recv_sems.at[k], device_id=to, device_id_type=MESH)

        first = [copy(0, tc_ref.at[0, :, left], fc_ref.at[0, :, left], xn), copy(1, tc_ref.at[2, :, left], via_x, xn),
                 copy(2, tc_ref.at[1, :, right], fc_ref.at[1, :, right], yn), copy(3, tc_ref.at[2, :, right], via_y, yn)]
        second = [copy(4, out_y, fc_ref.at[1, :, left], yn), copy(5, out_x, fc_ref.at[0, :, right], xn)]

        def add_and_send(arrival, landed, own_half, stage, onward):
            load = pltpu.make_async_copy(own_half, mine, sem)
            load.start()
            arrival.wait_recv()
            load.wait()
            stage[...] = (mine[...].astype(F32) + landed[...].astype(F32)).astype(BF16)
            onward.start()

        @pl.when(i == 0)
        def _():
            cp = pltpu.make_async_copy(w_hbm, w, sem)
            cp.start()
            _fill_slab(slab_stage, ((ROW_NORM_G, ng_ref), (ROW_REL, rel_ref)))
            for cp_keep in keep:
                cp_keep.start()
            gather.start()
            for r in (1, 2, 3, 0):
                to_sibling(r).start()
            onward = {1: first[:1], 2: first[2:3], 3: [first[1], first[3]], 0: []}
            for r in (1, 2, 3, 0):
                load = pltpu.make_async_copy(block((*chips[r], c_)), my_block, keep_sems.at[2])
                load.start()
                to_sibling(r).wait_recv()
                load.wait()
                both = my_block[...].astype(F32) + from_sib[r].astype(F32)
                if r == 0:
                    own_stage[...] = both
                else:
                    tc_ref[r - 1] = both.astype(BF16)
                for rc in onward[r]:
                    rc.start()
            own_out.start()
            cp.wait()

        @pl.when(i == (5 * nt) // 8)
        def _():
            gather.pass_on()
            add_and_send(first[1], via_x, tc_ref.at[1, :, left], out_y, second[0])
            add_and_send(first[3], via_y, tc_ref.at[0, :, right], out_x, second[1])

        dh = None
        for c in range(N_COL_BLOCKS):
            dz = dqkv_ref[c] if c < 3 else dr_ref[:, (c - 3) * COL_BLOCK:(c - 2) * COL_BLOCK]
            part = _dot(dz, w[c * COL_BLOCK:(c + 1) * COL_BLOCK, :])
            dh = part if dh is None else dh + part
        xf = x_ref[...]
        r = lax.rsqrt(jnp.mean(xf * xf, axis=-1, keepdims=True) + EPS)
        xn = xf * r
        gh = dh * g_ref[...]
        dx_ref[...] = r * (gh - xn * jnp.mean(gh * xn, axis=-1, keepdims=True)) + dx2_ref[...]

        @pl.when(i == nt - 1)
        def _():
            gather.finish()
            for cp_keep in keep:
                cp_keep.wait()
            for k in (0, 2, 4, 5):
                (first + second)[k].wait_recv()
            for rc in first + second:
                rc.wait_send()
            own_out.wait()
            for r in range(4):
                to_sibling(r).wait_send()

    hbm = pl.BlockSpec(memory_space=pl.ANY)
    whole = lambda a: pl.BlockSpec(a.shape, lambda i: (0,) * a.ndim)
    return pl.pallas_call(
        body, name="proj_bwd_x",
        grid=(nt,),
        in_specs=[pl.BlockSpec((3, tm, D_A), lambda i: (0, i, 0)),
                  pl.BlockSpec((tm, REST), lambda i: (i, 0)),
                  pl.BlockSpec((tm, D_MODEL), lambda i: (i, 0)),
                  pl.BlockSpec((tm, D_MODEL), lambda i: (i, 0)),
                  pl.BlockSpec((1, D_MODEL), lambda i: (0, 0)),
                  hbm, hbm] + [whole(a) for a in small],
        out_specs=(pl.BlockSpec((tm, D_MODEL), lambda i: (i, 0)), hbm, hbm, hbm),
        out_shape=(jax.ShapeDtypeStruct((s, D_MODEL), F32), jax.ShapeDtypeStruct((2, rows, D_MODEL), BF16),
                   jax.ShapeDtypeStruct((N_DEV, LATE_SLAB_ROWS, D_MODEL), F32),
                   jax.ShapeDtypeStruct((rows, D_MODEL), F32)),
        scratch_shapes=[pltpu.VMEM((D_IN, D_MODEL), BF16), pltpu.VMEM((3, rows, D_MODEL), BF16),
                        pltpu.VMEM((LATE_SLAB_ROWS, D_MODEL), F32)]
        + [pltpu.VMEM((rows, half), BF16)] * 5
        + [pltpu.VMEM((4, rows, D_MODEL), BF16), pltpu.VMEM((rows, D_MODEL), BF16), pltpu.VMEM((rows, D_MODEL), F32)]
        + [pltpu.SemaphoreType.DMA, pltpu.SemaphoreType.DMA((6,)), pltpu.SemaphoreType.DMA((6,)),
           pltpu.SemaphoreType.DMA((1, N_DEV - 1)), pltpu.SemaphoreType.DMA((1, N_DEV - 1)),
           pltpu.SemaphoreType.DMA((4,)), pltpu.SemaphoreType.DMA((4,)), pltpu.SemaphoreType.DMA((4,))],
        compiler_params=_params(60),
    )(dqkv, drest, x, dx2, norm_g, w_in_t, dw, *small)


def _proj_bwd_w(xn, dqkv, drest, norm_g, w_in_t):
    s = xn.shape[0]
    tk = s
    nk = s // tk

    def body(xn_ref, dqkv_ref, dr_ref, g_ref, w_ref, o_ref, dg_ref, acc):
        j = pl.program_id(0)
        i = pl.program_id(1)

        @pl.when((j == 0) & (i == 0))
        def _():
            dg_ref[...] = jnp.zeros(dg_ref.shape, F32)

        @pl.when(i == 0)
        def _():
            acc[...] = jnp.zeros(acc.shape, F32)

        @pl.when(j < 3)
        def _():
            acc[...] += _dot(dqkv_ref[...], xn_ref[...], TN)

        @pl.when(j >= 3)
        def _():
            acc[...] += _dot(dr_ref[...], xn_ref[...], TN)

        @pl.when(i == nk - 1)
        def _():
            m = acc[...]
            o_ref[...] = (m * g_ref[...]).astype(BF16)
            dg_ref[...] += jnp.sum(m * w_ref[...].astype(F32), axis=0, keepdims=True)

    return pl.pallas_call(
        body, name="proj_bwd_w",
        grid=(N_COL_BLOCKS, nk),
        in_specs=[pl.BlockSpec((tk, D_MODEL), lambda j, i: (i, 0)),
                  pl.BlockSpec((None, tk, COL_BLOCK),
                               lambda j, i: (jnp.minimum(j, 2), jnp.where(j < 3, i, nk - 1), 0)),
                  pl.BlockSpec((tk, COL_BLOCK),
                               lambda j, i: (jnp.where(j >= 3, i, 0), jnp.maximum(j - 3, 0))),
                  pl.BlockSpec((1, D_MODEL), lambda j, i: (0, 0)),
                  pl.BlockSpec((COL_BLOCK, D_MODEL), lambda j, i: (j, 0))],
        out_specs=(pl.BlockSpec((COL_BLOCK, D_MODEL), lambda j, i: (j, 0)),
                   pl.BlockSpec((1, D_MODEL), lambda j, i: (0, 0))),
        out_shape=(jax.ShapeDtypeStruct((D_IN, D_MODEL), BF16), jax.ShapeDtypeStruct((1, D_MODEL), F32)),
        scratch_shapes=[pltpu.VMEM((COL_BLOCK, D_MODEL), F32)],
        compiler_params=_params(56),
    )(xn, dqkv, drest, norm_g, w_in_t)


def _adamw_math(w, g, m, v):
    c1 = 1.0 - ADAM_B1 ** ADAM_STEP
    c2 = 1.0 - ADAM_B2 ** ADAM_STEP
    nm = ADAM_B1 * m + (1.0 - ADAM_B1) * g
    nv = ADAM_B2 * v + (1.0 - ADAM_B2) * (g * g)
    return -ADAM_LR * ((nm / c1) / (jnp.sqrt(nv / c2) + ADAM_EPS) + ADAM_WD * w), nm, nv


def _adamw(name, w, g, m, v, from_chip):
    rows, cols = w.shape
    tr = rows if rows * cols <= 512 * 1024 else next(t for t in range(256, 7, -8) if rows % t == 0)

    def body(w_ref, g_ref, m_ref, v_ref, t_ref, g_out, d_ref, nm_ref, nv_ref):
        gg = g_ref[...]
        for j in range(from_chip.shape[0]):
            gg = gg + t_ref[j].astype(F32)
        g_out[...] = gg
        d_ref[...], nm_ref[...], nv_ref[...] = _adamw_math(w_ref[...], gg, m_ref[...], v_ref[...])

    spec = pl.BlockSpec((tr, cols), lambda i: (i, 0))
    shape = jax.ShapeDtypeStruct((rows, cols), F32)
    return pl.pallas_call(
        body, name=name,
        grid=(rows // tr,),
        in_specs=[spec] * 4 + [pl.BlockSpec((from_chip.shape[0], tr, cols), lambda i: (0, i, 0))],
        out_specs=(spec,) * 4, out_shape=(shape,) * 4,
        compiler_params=_params(32),
    )(w, g, m, v, from_chip)


_SMALL = (("norm_g", (1, D_MODEL)), ("b_gate", (1, 2 * D_MODEL)), ("rel_bias", (N_HEADS, N_REL)),
          ("sgu_ln_g", (1, D_B)), ("sgu_ln_b", (1, D_B)), ("w_s", (N_GROUPS * SGU_CHUNK, SGU_CHUNK)),
          ("b_s", (N_GROUPS, SGU_CHUNK)), ("final_g", (1, D_MODEL)))


def _adamw_small(slabs, ws_all, late_slabs, weights, moments_m, moments_v):
    k = len(_SMALL)

    def total(ref):
        acc = ref[0]
        for d in range(1, N_DEV):
            acc = acc + ref[d]
        return acc

    def body(*refs):
        slab_ref, ws_ref, late_ref = refs[:3]
        w_refs, m_refs, v_refs = refs[3:3 + k], refs[3 + k:3 + 2 * k], refs[3 + 2 * k:3 + 3 * k]
        outs = refs[3 + 3 * k:]
        slab, late = total(slab_ref), total(late_ref)
        grads = {
            "norm_g": late[ROW_NORM_G:ROW_NORM_G + 1, :],
            "b_gate": jnp.concatenate([slab[ROW_B_GATE:ROW_B_GATE + 1, :], slab[ROW_B_GATE + 1:ROW_B_GATE + 2, :]], axis=1),
            "rel_bias": late[ROW_REL:ROW_REL + N_HEADS, :N_REL],
            "sgu_ln_g": slab[ROW_LN_G:ROW_LN_G + 1, :D_B],
            "sgu_ln_b": slab[ROW_LN_B:ROW_LN_B + 1, :D_B],
            "w_s": total(ws_ref),
            "b_s": slab[ROW_B_S:ROW_B_S + N_GROUPS, :SGU_CHUNK],
            "final_g": slab[ROW_FINAL_G:ROW_FINAL_G + 1, :],
        }
        for n, (name, _) in enumerate(_SMALL):
            g = grads[name]
            outs[n][...] = g
            outs[k + n][...], outs[2 * k + n][...], outs[3 * k + n][...] = _adamw_math(
                w_refs[n][...], g, m_refs[n][...], v_refs[n][...])
        outs[4 * k][...] = slab[ROW_LOSS:ROW_LOSS + 1, :1]

    vmem = pl.BlockSpec(memory_space=pltpu.VMEM)
    shapes = tuple(jax.ShapeDtypeStruct(shape, F32) for _, shape in _SMALL)
    return pl.pallas_call(
        body, name="adamw_small",
        out_shape=shapes * 4 + (jax.ShapeDtypeStruct((1, 1), F32),),
        in_specs=[vmem] * (3 + 3 * k), out_specs=tuple([vmem] * (4 * k + 1)),
        compiler_params=_params(16),
    )(slabs, ws_all, late_slabs, *weights, *moments_m, *moments_v)


def _pad_rel(a):
    return jnp.pad(a.reshape(N_HEADS, N_REL), ((0, 0), (0, N_REL_PAD - N_REL)))


def kernel(x, norm_g, w_in, b_gate, rel_bias, sgu_ln_g, sgu_ln_b, w_s, b_s, w_pa, w_pb, w_out, final_g, loss_target, m_norm_g, m_w_in, m_b_gate, m_rel_bias, m_sgu_ln_g, m_sgu_ln_b, m_w_s, m_b_s, m_w_pa, m_w_pb, m_w_out, m_final_g, v_norm_g, v_w_in, v_b_gate, v_rel_bias, v_sgu_ln_g, v_sgu_ln_b, v_w_s, v_b_s, v_w_pa, v_w_pb, v_w_out, v_final_g):
    s = x.shape[1]
    xs = x.reshape(s, D_MODEL)
    tgt = loss_target.reshape(s, D_MODEL)

    bias_table = _bias_table(_pad_rel(rel_bias))
    w_in_t = jnp.swapaxes(w_in[0], 0, 1)
    qkv, x_norm, w_in_t_full = _gather_proj_fwd(xs, norm_g, w_in_t)
    attn_out, g_pa, g_pb, g_out = _attn_fwd(qkv, bias_table, (w_pa[0], w_pb[0], w_out[0]))
    w_pa_full = jnp.transpose(g_pa, (1, 0, 2)).reshape(D_A, D_MODEL)
    w_pb_full = jnp.transpose(g_pb, (1, 0, 2)).reshape(D_B, D_MODEL)
    w_out_full = g_out.reshape(D_MODEL, D_MODEL)

    (dx2, d_attn, drest, dw_out, dw_pa, dw_pb, d_bgate, d_fg, d_lng, d_lnb, d_ws, d_bs, loss_part) = _mid_fwd_bwd(
        xs, tgt, attn_out, qkv, w_pa_full, w_pb_full, w_out_full, b_gate, sgu_ln_g, sgu_ln_b, w_s[0],
        b_s.reshape(N_GROUPS, SGU_CHUNK, 1), final_g.reshape(1, D_MODEL))

    dqkv, dbias, fc_pa, fc_pb, fc_out, slabs, ws_all, own_pa, own_pb, own_out = _attn_bwd(
        qkv, bias_table, d_attn, (dw_pa, dw_pb, dw_out),
        (d_bgate, d_lng, d_lnb, d_fg, loss_part, d_bs, d_ws.reshape(N_GROUPS * SGU_CHUNK, SGU_CHUNK)))
    d_rel = _bias_grad(dbias)
    dw_in_t, d_ng = _proj_bwd_w(x_norm, dqkv, drest, norm_g, w_in_t_full)
    grad_x, fc_in, late_slabs, own_in = _proj_bwd_x(dqkv, drest, xs, dx2, norm_g, w_in_t_full, dw_in_t, (d_ng, d_rel))
    big = {"w_in": tuple(jnp.swapaxes(t, 0, 1)[None] for t in _adamw(
        "adamw_w_in", w_in_t, own_in, jnp.swapaxes(m_w_in[0], 0, 1), jnp.swapaxes(v_w_in[0], 0, 1), fc_in))}
    for name, w, g, fc, m, v in (("w_pa", w_pa, own_pa, fc_pa, m_w_pa, v_w_pa),
                                 ("w_pb", w_pb, own_pb, fc_pb, m_w_pb, v_w_pb),
                                 ("w_out", w_out, own_out, fc_out, m_w_out, v_w_out)):
        big[name] = tuple(t[None] for t in _adamw("adamw_" + name, w[0], g, m[0], v[0], fc))

    as_2d = lambda leaves: [a.reshape(shape) for a, (_, shape) in zip(leaves, _SMALL)]
    small_out = _adamw_small(
        slabs, ws_all, late_slabs, as_2d((norm_g, b_gate, rel_bias, sgu_ln_g, sgu_ln_b, w_s, b_s, final_g)),
        as_2d((m_norm_g, m_b_gate, m_rel_bias, m_sgu_ln_g, m_sgu_ln_b, m_w_s, m_b_s, m_final_g)),
        as_2d((v_norm_g, v_b_gate, v_rel_bias, v_sgu_ln_g, v_sgu_ln_b, v_w_s, v_b_s, v_final_g)))
    small_index = {name: n for n, (name, _) in enumerate(_SMALL)}

    def leaf(kind, name, like):
        if name in big:
            return big[name][kind]
        return small_out[kind * len(_SMALL) + small_index[name]].reshape(like.shape)

    weights = (("norm_g", norm_g), ("w_in", w_in), ("b_gate", b_gate), ("rel_bias", rel_bias), ("sgu_ln_g", sgu_ln_g),
               ("sgu_ln_b", sgu_ln_b), ("w_s", w_s), ("b_s", b_s), ("w_pa", w_pa), ("w_pb", w_pb), ("w_out", w_out),
               ("final_g", final_g))
    outs = [small_out[-1].reshape(()), grad_x.reshape(x.shape)]
    for kind in range(4):
        outs.extend(leaf(kind, name, like) for name, like in weights)
    return tuple(outs)
```

```python
import functools
import math

import jax
import jax.numpy as jnp
from jax import lax
from jax.experimental import pallas as pl
from jax.experimental.pallas import tpu as pltpu

F32 = jnp.float32
BF16 = jnp.bfloat16
MESH = pl.DeviceIdType.MESH
N_DEV = 8

D_MODEL = 1024
D_A = 512
D_B = 512
D_IN = 5632
N_HEADS = 8
HEAD_DIM = 64
N_PREV = 8
REL_CLIP = 128
N_REL = 2 * REL_CLIP + 1
N_REL_PAD = 384
SGU_CHUNK = 128
N_GROUPS = 4
EPS = 1e-6
NEG_INF = -1e30
Q_SCALE = HEAD_DIM ** -0.5

Q_BLOCK = 256
K_SPAN = 768
Z_PAD = K_SPAN - Q_BLOCK
ROLL_W = 1024
COL_BLOCK = 512
N_COL_BLOCKS = D_IN // COL_BLOCK
REST = D_IN - 3 * D_A
TOKEN_TILE = 256
SHARD = D_MODEL // N_DEV

ADAM_LR = 0.001
ADAM_B1 = 0.9
ADAM_B2 = 0.999
ADAM_EPS = 1e-08
ADAM_WD = 0.01
ADAM_STEP = 10

GELU_C = math.sqrt(2.0 / math.pi)
GELU_A = 0.044715

NT = (((1,), (1,)), ((), ()))
TN = (((0,), (0,)), ((), ()))
HIGHEST = lax.Precision.HIGHEST


def _params(vmem_mb, **kw):
    return pltpu.CompilerParams(vmem_limit_bytes=vmem_mb * 1024 * 1024, **kw)


def _dot(a, b, dims=None):
    if dims is None:
        return jnp.dot(a, b, preferred_element_type=F32)
    return lax.dot_general(a, b, dims, preferred_element_type=F32)


def _sigmoid(x):
    return 0.5 * jnp.tanh(0.5 * x) + 0.5


def _gelu_and_grad(u):
    u2 = u * u
    t = jnp.tanh(GELU_C * (u + GELU_A * u * u2))
    half = 0.5 * (1.0 + t)
    g = u * half
    dg = half + 0.5 * u * (1.0 - t * t) * (GELU_C * (1.0 + 3.0 * GELU_A * u2))
    return g, dg


def _my_pos():
    return lax.axis_index("x"), lax.axis_index("y"), lax.axis_index("c")


def _flat_id(pos):
    return 4 * pos[0] + 2 * pos[1] + pos[2]


def _other_chips(pos):
    x, y, _ = pos
    return ((1 - x, y), (x, 1 - y), (1 - x, 1 - y))


class _SlotGather:
    def __init__(self, bufs, send_sems, recv_sems, own=None):
        self.bufs, self.send_sems, self.recv_sems = bufs, send_sems, recv_sems
        self.own = own if own is not None else [None] * len(bufs)
        x, y, c = _my_pos()
        self.c, self.me, self.sib = c, (x, y, c), (x, y, 1 - c)
        self.chips = _other_chips(self.me)

    def _copy(self, a, k, block, to):
        slot = _flat_id(block)
        src = self.own[a] if (k < 4 and self.own[a] is not None) else self.bufs[a].at[slot]
        return pltpu.make_async_remote_copy(
            src_ref=src, dst_ref=self.bufs[a].at[slot],
            send_sem=self.send_sems.at[a, k], recv_sem=self.recv_sems.at[a, k], device_id=to, device_id_type=MESH)

    def _own_sends(self):
        n = len(self.bufs)
        return ([self._copy(a, 1 + j, self.me, (*chip, self.c)) for j, chip in enumerate(self.chips) for a in range(n)]
                + [self._copy(a, 0, self.me, self.sib) for a in range(n)])

    def _passes(self):
        return [self._copy(a, 4 + j, (*chip, self.c), self.sib)
                for j, chip in enumerate(self.chips) for a in range(len(self.bufs))]

    def start(self):
        for cp in self._own_sends():
            cp.start()

    def pass_on(self):
        for j, chip in enumerate(self.chips):
            for a in range(len(self.bufs)):
                self._copy(a, 1 + j, (*chip, self.c), self.me).wait_recv()
                self._copy(a, 4 + j, (*chip, self.c), self.sib).start()

    def finish(self):
        for a in range(len(self.bufs)):
            self._copy(a, 0, self.sib, self.me).wait_recv()
            for j, chip in enumerate(self.chips):
                self._copy(a, 4 + j, (*chip, 1 - self.c), self.me).wait_recv()
        for cp in self._own_sends() + self._passes():
            cp.wait_send()


def _owner_copies(to_chip, from_chip, send_sems, recv_sems):
    x, y, c = _my_pos()
    return [pltpu.make_async_remote_copy(
        src_ref=to_chip[a].at[j], dst_ref=from_chip[a].at[j],
        send_sem=send_sems.at[a, j], recv_sem=recv_sems.at[a, j], device_id=(*chip, c), device_id_type=MESH)
        for a in range(len(to_chip)) for j, chip in enumerate(_other_chips((x, y, c)))]


ROW_B_GATE, ROW_LN_G, ROW_LN_B, ROW_FINAL_G, ROW_LOSS, ROW_B_S, SLAB_ROWS = 1, 3, 4, 5, 6, 16, 24
ROW_NORM_G, ROW_REL, LATE_SLAB_ROWS = 0, 8, 16


def _rel_index(e):
    lo, hi = Z_PAD - REL_CLIP, Z_PAD + REL_CLIP
    return jnp.where(e <= lo, 2 * REL_CLIP, jnp.where(e < hi, hi - e, jnp.where(e <= K_SPAN, 0, 2 * REL_CLIP)))


def _bias_table(rel_bias_pad):
    def body(rb_ref, bt_ref):
        c = lax.broadcasted_iota(jnp.int32, (N_REL_PAD, ROLL_W), 1)
        r = lax.broadcasted_iota(jnp.int32, (N_REL_PAD, ROLL_W), 0)
        pick = (r == _rel_index(c)).astype(F32)
        rows = jnp.dot(rb_ref[...], pick, precision=HIGHEST, preferred_element_type=F32)
        qc = lax.broadcasted_iota(jnp.int32, (Q_BLOCK, K_SPAN), 0) >> 6
        kc = lax.broadcasted_iota(jnp.int32, (Q_BLOCK, K_SPAN), 1) >> 6
        band = (kc >= qc) & (kc <= qc + N_PREV)
        for h in range(N_HEADS):
            t = jnp.broadcast_to(rows[h:h + 1, :], (Q_BLOCK, ROLL_W))
            t = pltpu.roll(t, 0, 1, stride=1, stride_axis=0)
            bt_ref[h] = jnp.where(band, t[:, :K_SPAN], NEG_INF)

    return pl.pallas_call(
        body, name="bias_table",
        out_shape=jax.ShapeDtypeStruct((N_HEADS, Q_BLOCK, K_SPAN), F32),
        compiler_params=_params(32),
    )(rel_bias_pad)


def _bias_grad(dbias):
    def body(a_ref, o_ref):
        rr = lax.broadcasted_iota(jnp.int32, (Q_BLOCK, Q_BLOCK), 0)
        cc = lax.broadcasted_iota(jnp.int32, (Q_BLOCK, Q_BLOCK), 1)
        flip = (rr + cc == Q_BLOCK - 1).astype(F32)
        c = lax.broadcasted_iota(jnp.int32, (ROLL_W, N_REL_PAD), 0)
        r = lax.broadcasted_iota(jnp.int32, (ROLL_W, N_REL_PAD), 1)
        e = jnp.where(c >= Q_BLOCK - 1, c - (Q_BLOCK - 1), c + (ROLL_W - Q_BLOCK + 1))
        pick = (r == _rel_index(e)).astype(F32)
        sums = []
        for h in range(N_HEADS):
            a = jnp.dot(flip, a_ref[h], precision=HIGHEST, preferred_element_type=F32)
            a = jnp.concatenate([a, jnp.zeros((Q_BLOCK, ROLL_W - K_SPAN), F32)], axis=1)
            a = pltpu.roll(a, 0, 1, stride=1, stride_axis=0)
            sums.append(jnp.sum(a, axis=0, keepdims=True))
        diag = jnp.concatenate(sums, axis=0)
        o_ref[...] = jnp.dot(diag, pick, precision=HIGHEST, preferred_element_type=F32)

    return pl.pallas_call(
        body, name="bias_grad",
        out_shape=jax.ShapeDtypeStruct((N_HEADS, N_REL_PAD), F32),
        compiler_params=_params(32),
    )(dbias)


def _gather_proj_fwd(x, norm_g, w_in_t):
    s = x.shape[0]
    tm = 512 if s % 512 == 0 else TOKEN_TILE
    nt = s // tm
    n_pad = Z_PAD // tm
    shard_w = w_in_t.shape[0]
    chip_w = 2 * shard_w
    n_chips = N_DEV // 2

    def body(order_ref, x_ref, g_ref, win_hbm, z_ref, xn_ref, wt_hbm, wt, hb, win_f32, send_sems, recv_sems,
             local_sems):
        j = pl.program_id(0)
        i = pl.program_id(1)
        x_, y_, c_ = _my_pos()
        me, sib = (x_, y_, c_), (x_, y_, 1 - c_)
        near = _other_chips(me)
        pick = lambda a, b: tuple(jnp.where(c_ == 0, u, v) for u, v in zip(a, b))
        passed_from, passed_to = pick(near[0], near[1]), pick(near[1], near[0])

        def rows_of(block):
            return wt.at[pl.ds(pl.multiple_of(_flat_id(block) * shard_w, 16), shard_w), :]

        def copy(k, block, to):
            return pltpu.make_async_remote_copy(
                src_ref=rows_of(block), dst_ref=rows_of(block),
                send_sem=send_sems.at[k], recv_sem=recv_sems.at[k], device_id=to, device_id_type=MESH)

        def sends():
            return ([copy(0, me, sib), copy(1, me, (*near[0], c_)), copy(2, me, (*near[1], c_)),
                     copy(3, (*passed_from, c_), (*passed_to, c_))]
                    + [copy(4 + n, (*near[n], c_), sib) for n in range(3)])

        keep = pltpu.make_async_copy(wt, wt_hbm, local_sems.at[0])

        @pl.when((j == 0) & (i == 0))
        def _():
            load = pltpu.make_async_copy(win_hbm, win_f32, local_sems.at[1])
            load.start()
            load.wait()
            rows_of(me)[...] = win_f32[...].astype(BF16)
            for cp in sends()[:3]:
                cp.start()
            copy(0, sib, me).wait_recv()

        @pl.when((j == 1) & (i == 0))
        def _():
            copy(1, (*near[0], c_), me).wait_recv()
            copy(2, (*near[1], c_), me).wait_recv()
            for cp in sends()[3:6]:
                cp.start()
            copy(4, (*near[0], 1 - c_), me).wait_recv()

        @pl.when((j == 2) & (i == 0))
        def _():
            copy(5, (*near[1], 1 - c_), me).wait_recv()

        @pl.when((j == 3) & (i == 0))
        def _():
            copy(3, (*near[2], c_), me).wait_recv()
            copy(6, (*near[2], c_), sib).start()
            copy(6, (*near[2], 1 - c_), me).wait_recv()
            keep.start()

        @pl.when(i < n_pad)
        def _():
            z_ref[...] = jnp.zeros(z_ref.shape, BF16)

        @pl.when(i >= n_pad)
        def _():
            rows = pl.ds(pl.multiple_of((i - n_pad) * tm, tm), tm)

            @pl.when(j == 0)
            def _():
                xf = x_ref[...]
                xn = xf * lax.rsqrt(jnp.mean(xf * xf, axis=-1, keepdims=True) + EPS)
                hb[rows, :] = (xn * g_ref[...]).astype(BF16)
                xn_ref[...] = xn.astype(BF16)

            chip_rows = pl.ds(pl.multiple_of(order_ref[j] * chip_w, 16), chip_w)
            blk = _dot(hb[rows, :], wt[chip_rows, :], NT)
            q_scale = jnp.where(order_ref[j] == 0, Q_SCALE, 1.0).astype(F32)
            z_ref[:, :D_A] = (blk[:, :D_A] * q_scale).astype(BF16)
            z_ref[:, D_A:] = blk[:, D_A:].astype(BF16)

        @pl.when((j == n_chips - 1) & (i == n_pad + nt - 1))
        def _():
            keep.wait()
            for cp in sends():
                cp.wait_send()

    pos = _my_pos()
    order = jnp.stack([2 * cx + cy for cx, cy in ((pos[0], pos[1]),) + _other_chips(pos)]).astype(jnp.int32)
    first_pass = lambda j, i: jnp.where(j == 0, jnp.maximum(i - n_pad, 0), nt - 1)
    grid_spec = pltpu.PrefetchScalarGridSpec(
        num_scalar_prefetch=1,
        grid=(n_chips, n_pad + nt),
        in_specs=[pl.BlockSpec((tm, D_MODEL), lambda j, i, o: (first_pass(j, i), 0)),
                  pl.BlockSpec((1, D_MODEL), lambda j, i, o: (0, 0)),
                  pl.BlockSpec(memory_space=pl.ANY)],
        out_specs=(pl.BlockSpec((tm, chip_w), lambda j, i, o: (i, o[j])),
                   pl.BlockSpec((tm, D_MODEL), lambda j, i, o: (first_pass(j, i), 0)),
                   pl.BlockSpec(memory_space=pl.ANY)),
        scratch_shapes=[pltpu.VMEM((D_IN, D_MODEL), BF16),
                        pltpu.VMEM((s, D_MODEL), BF16), pltpu.VMEM(w_in_t.shape, F32),
                        pltpu.SemaphoreType.DMA((N_DEV - 1,)), pltpu.SemaphoreType.DMA((N_DEV - 1,)),
                        pltpu.SemaphoreType.DMA((2,))])
    return pl.pallas_call(
        body, name="gather_proj_fwd",
        grid_spec=grid_spec,
        out_shape=(jax.ShapeDtypeStruct((Z_PAD + s, D_IN), BF16), jax.ShapeDtypeStruct((s, D_MODEL), BF16),
                   jax.ShapeDtypeStruct((D_IN, D_MODEL), BF16)),
        compiler_params=_params(60),
    )(order, x, norm_g, w_in_t)


def _attn_specs(rows):
    pairs = N_HEADS // 2
    return ([pl.BlockSpec((rows, 128), functools.partial(lambda which, p: (0, which * pairs + p), which))
             for which in range(3)]
            + [pl.BlockSpec((2, Q_BLOCK, K_SPAN), lambda p: (p, 0, 0))])


def _head_masks():
    lane = lax.broadcasted_iota(jnp.int32, (1, 128), 1)
    first = lane < HEAD_DIM
    return (first, jnp.logical_not(first))


def _stack_heads(x, masks):
    zero = jnp.zeros((), x.dtype)
    return jnp.concatenate([jnp.where(m, x, zero) for m in masks], axis=0)


STRIP = 16


def _softmax_strips(s_ref, bias_ref, b):
    valid = lax.broadcasted_iota(jnp.int32, (1, K_SPAN), 1) >= Z_PAD - b * Q_BLOCK
    for t in range(2 * Q_BLOCK // STRIP):
        hh, r = divmod(t * STRIP, Q_BLOCK)
        st = s_ref[t * STRIP:(t + 1) * STRIP, :] + bias_ref[hh, r:r + STRIP, :]
        st = jnp.where(valid, st, NEG_INF)
        e = jnp.exp(st - jnp.max(st, axis=-1, keepdims=True))
        yield e * (1.0 / jnp.sum(e, axis=-1, keepdims=True))


def _side_by_side_strips(strips):
    half = len(strips) // 2
    return jnp.concatenate([jnp.concatenate([a, c], axis=1) for a, c in zip(strips[:half], strips[half:])], axis=0)


def _attn_fwd(qkv, bias_table, shards):
    s = qkv.shape[0] - Z_PAD
    nb = s // Q_BLOCK
    n = len(shards)
    pairs = N_HEADS // 2

    def body(*refs):
        q_ref, k_ref, v_ref, bt_ref = refs[:4]
        shard_refs = refs[4:4 + n]
        o_ref = refs[4 + n]
        slot_refs = refs[5 + n:5 + 2 * n]
        stages = refs[5 + 2 * n:5 + 3 * n]
        s_scr, send_sems, recv_sems, local_sems = refs[5 + 3 * n:]
        p_id = pl.program_id(0)
        gather = _SlotGather(slot_refs, send_sems, recv_sems, own=stages)
        keep = [pltpu.make_async_copy(stages[a], slot_refs[a].at[_flat_id(_my_pos())], local_sems.at[a])
                for a in range(n)]

        @pl.when(p_id == 0)
        def _():
            for a in range(n):
                stages[a][...] = shard_refs[a][...].astype(BF16)
                keep[a].start()
            gather.start()

        @pl.when(p_id == 2)
        def _():
            gather.pass_on()

        masks = _head_masks()

        def scores(b, half):
            r0 = pl.multiple_of(b * Q_BLOCK, Q_BLOCK)
            q2 = _stack_heads(q_ref[pl.ds(r0 + Z_PAD, Q_BLOCK), :], masks)
            s_scr[half] = _dot(q2, k_ref[pl.ds(r0, K_SPAN), :], NT)

        def finish(b, half):
            r0 = pl.multiple_of(b * Q_BLOCK, Q_BLOCK)
            v2 = _stack_heads(v_ref[pl.ds(r0, K_SPAN), :], masks)
            p = [st.astype(BF16) for st in _softmax_strips(s_scr.at[half], bt_ref, b)]
            o_ref[pl.ds(r0, Q_BLOCK), :] = _dot(_side_by_side_strips(p), v2)

        def two_blocks(i, carry):
            b = 2 * i
            scores(b + 1, 1)
            finish(b, 0)
            scores(jnp.minimum(b + 2, nb - 1), 0)
            finish(b + 1, 1)
            return carry

        scores(0, 0)
        lax.fori_loop(0, nb // 2, two_blocks, 0)

        @pl.when(p_id == pairs - 1)
        def _():
            gather.finish()
            for cp in keep:
                cp.wait()

    hbm = pl.BlockSpec(memory_space=pl.ANY)
    return pl.pallas_call(
        body, name="attn_fwd",
        grid=(pairs,),
        in_specs=_attn_specs(s + Z_PAD) + [pl.BlockSpec(a.shape, lambda p: (0, 0)) for a in shards],
        out_specs=(pl.BlockSpec((s, 128), lambda p: (0, p)),) + (hbm,) * n,
        out_shape=(jax.ShapeDtypeStruct((s, D_A), F32),)
        + tuple(jax.ShapeDtypeStruct((N_DEV,) + a.shape, BF16) for a in shards),
        scratch_shapes=[pltpu.VMEM(a.shape, BF16) for a in shards]
        + [pltpu.VMEM((2, 2 * Q_BLOCK, K_SPAN), F32),
           pltpu.SemaphoreType.DMA((n, N_DEV - 1)), pltpu.SemaphoreType.DMA((n, N_DEV - 1)),
           pltpu.SemaphoreType.DMA((n,))],
        compiler_params=_params(48),
    )(qkv, qkv, qkv, bias_table, *shards)


def _fill_slab(stage, rows):
    stage[...] = jnp.zeros(stage.shape, F32)
    for row, ref in rows:
        r, c = ref.shape
        if c > D_MODEL:
            for part in range(c // D_MODEL):
                stage[row + part:row + part + 1, :] = ref[:, part * D_MODEL:(part + 1) * D_MODEL]
        else:
            stage[row:row + r, :c] = ref[...]


def _attn_bwd(qkv, bias_table, d_out, cuts, small):
    s = qkv.shape[0] - Z_PAD
    nb = s // Q_BLOCK
    n = len(cuts)
    pairs = N_HEADS // 2
    ws_shape = small[-1].shape

    def body(*refs):
        q_ref, k_ref, v_ref, bt_ref, do_ref = refs[:5]
        cut_refs = refs[5:5 + n]
        bg_ref, lng_ref, lnb_ref, fg_ref, loss_ref, bs_ref, ws_ref = refs[5 + n:12 + n]
        dqkv_ref, db_ref = refs[12 + n:14 + n]
        from_chip_refs = refs[14 + n:14 + 2 * n]
        slab_land, ws_land = refs[14 + 2 * n:16 + 2 * n]
        own_refs = refs[16 + 2 * n:16 + 3 * n]
        (dk_acc, dv_acc, s_scr, dp_scr, slab_stage, ws_stage, send_sems, recv_sems, gather_send, gather_recv,
         keep_sems) = refs[16 + 3 * n:27 + 3 * n]
        mine, from_sib, to_chip_refs = (refs[27 + k * n:27 + (k + 1) * n] for k in (3, 4, 5))
        sib_send, sib_recv, load_sems = refs[27 + 6 * n:]
        p_id = pl.program_id(0)
        x_, y_, c_ = _my_pos()
        me = _flat_id((x_, y_, c_))
        chips = ((x_, y_),) + _other_chips((x_, y_, c_))

        def to_sibling(a, r):
            return pltpu.make_async_remote_copy(
                src_ref=cut_refs[a].at[_flat_id((*chips[r], 1 - c_))], dst_ref=from_sib[a].at[r],
                send_sem=sib_send.at[a, r], recv_sem=sib_recv.at[a, r], device_id=(x_, y_, 1 - c_),
                device_id_type=MESH)

        def load(a, r):
            return pltpu.make_async_copy(cut_refs[a].at[_flat_id((*chips[r], c_))], mine[a].at[r], load_sems.at[a, r])

        pieces = [(a, r) for r in (1, 2, 3, 0) for a in range(n)]
        gather = _SlotGather([slab_land, ws_land], gather_send, gather_recv, own=[slab_stage, ws_stage])
        keep = [pltpu.make_async_copy(stage, land.at[me], keep_sems.at[k]) for k, (stage, land) in enumerate(
            ((slab_stage, slab_land), (ws_stage, ws_land)))]

        @pl.when(p_id == 0)
        def _():
            _fill_slab(slab_stage, ((ROW_B_GATE, bg_ref), (ROW_LN_G, lng_ref), (ROW_LN_B, lnb_ref),
                                    (ROW_FINAL_G, fg_ref), (ROW_LOSS, loss_ref)))
            eye = (lax.broadcasted_iota(jnp.int32, (SGU_CHUNK, SGU_CHUNK), 0)
                   == lax.broadcasted_iota(jnp.int32, (SGU_CHUNK, SGU_CHUNK), 1))
            for g in range(N_GROUPS):
                row = jnp.sum(jnp.where(eye, bs_ref[g], 0.0), axis=0, keepdims=True)
                slab_stage[ROW_B_S + g:ROW_B_S + g + 1, :SGU_CHUNK] = row
            ws_stage[...] = ws_ref[...]
            for cp in keep:
                cp.start()
            gather.start()
            for a, r in pieces:
                to_sibling(a, r).start()
                load(a, r).start()

        @pl.when(p_id == 1)
        def _():
            for a, r in pieces:
                load(a, r).wait()
                to_sibling(a, r).wait_recv()
                both = mine[a][r].astype(F32) + from_sib[a][r].astype(F32)
                if r == 0:
                    own_refs[a][...] = both
                else:
                    to_chip_refs[a][r - 1] = both.astype(BF16)
            for cp in _owner_copies(to_chip_refs, from_chip_refs, send_sems, recv_sems):
                cp.start()

        @pl.when(p_id == 2)
        def _():
            gather.pass_on()

        dk_acc[...] = jnp.zeros(dk_acc.shape, F32)
        dv_acc[...] = jnp.zeros(dv_acc.shape, F32)
        db_ref[...] = jnp.zeros(db_ref.shape, F32)
        masks = _head_masks()

        def operands(b):
            r0 = pl.multiple_of(b * Q_BLOCK, Q_BLOCK)
            q2 = _stack_heads(q_ref[pl.ds(r0 + Z_PAD, Q_BLOCK), :], masks)
            do2 = _stack_heads(do_ref[pl.ds(r0, Q_BLOCK), :], masks)
            return r0, q2, do2, k_ref[pl.ds(r0, K_SPAN), :]

        def ahead(b, half):
            r0, q2, do2, kcat = operands(b)
            s_scr[half] = _dot(q2, kcat, NT)
            dp_scr[half] = _dot(do2, v_ref[pl.ds(r0, K_SPAN), :], NT)

        def finish(b, half):
            r0, q2, do2, kcat = operands(b)
            p_strips, ds_strips = [], []
            for t, p in enumerate(_softmax_strips(s_scr.at[half], bt_ref, b)):
                hh, r = divmod(t * STRIP, Q_BLOCK)
                dp_t = dp_scr[half, t * STRIP:(t + 1) * STRIP, :]
                ds = p * (dp_t - jnp.sum(p * dp_t, axis=-1, keepdims=True))
                db_ref[hh, r:r + STRIP, :] += ds
                p_strips.append(p.astype(BF16))
                ds_strips.append(ds.astype(BF16))
            dq = _dot(_side_by_side_strips(ds_strips), _stack_heads(kcat, masks))
            dqkv_ref[0, pl.ds(r0, Q_BLOCK), :] = (dq * Q_SCALE).astype(BF16)
            dk_acc[pl.ds(r0, K_SPAN), :] += _dot(jnp.concatenate(ds_strips, axis=0), q2, TN)
            dv_acc[pl.ds(r0, K_SPAN), :] += _dot(jnp.concatenate(p_strips, axis=0), do2, TN)

        def two_blocks(i, carry):
            b = 2 * i
            ahead(b + 1, 1)
            finish(b, 0)
            ahead(jnp.minimum(b + 2, nb - 1), 0)
            finish(b + 1, 1)
            return carry

        ahead(0, 0)
        lax.fori_loop(0, nb // 2, two_blocks, 0)
        dqkv_ref[1] = dk_acc[Z_PAD:, :].astype(BF16)
        dqkv_ref[2] = dv_acc[Z_PAD:, :].astype(BF16)

        @pl.when(p_id == pairs - 1)
        def _():
            gather.finish()
            for cp in keep:
                cp.wait()
            for cp in _owner_copies(to_chip_refs, from_chip_refs, send_sems, recv_sems):
                cp.wait_recv()
                cp.wait_send()
            for a, r in pieces:
                to_sibling(a, r).wait_send()

    hbm = pl.BlockSpec(memory_space=pl.ANY)
    lands = ((N_DEV, SLAB_ROWS, D_MODEL), (N_DEV,) + ws_shape)
    blocks = [c.shape[1:] for c in cuts]
    return pl.pallas_call(
        body, name="attn_bwd",
        grid=(pairs,),
        in_specs=_attn_specs(s + Z_PAD) + [pl.BlockSpec((s, 128), lambda p: (0, p))] + [hbm] * n
        + [pl.BlockSpec(a.shape, functools.partial(lambda nd, p: (0,) * nd, a.ndim)) for a in small],
        out_specs=(pl.BlockSpec((3, s, 128), lambda p: (0, 0, p)),
                   pl.BlockSpec((2, Q_BLOCK, K_SPAN), lambda p: (p, 0, 0))) + (hbm,) * (n + 2)
        + tuple(pl.BlockSpec(b, lambda p: (0, 0)) for b in blocks),
        out_shape=(jax.ShapeDtypeStruct((3, s, D_A), BF16),
                   jax.ShapeDtypeStruct((N_HEADS, Q_BLOCK, K_SPAN), F32))
        + tuple(jax.ShapeDtypeStruct((3,) + b, BF16) for b in blocks)
        + tuple(jax.ShapeDtypeStruct(shape, F32) for shape in lands)
        + tuple(jax.ShapeDtypeStruct(b, F32) for b in blocks),
        scratch_shapes=[pltpu.VMEM((s + Z_PAD, 128), F32), pltpu.VMEM((s + Z_PAD, 128), F32),
                        pltpu.VMEM((2, 2 * Q_BLOCK, K_SPAN), F32), pltpu.VMEM((2, 2 * Q_BLOCK, K_SPAN), F32)]
        + [pltpu.VMEM(shape[1:], F32) for shape in lands]
        + [pltpu.SemaphoreType.DMA((n, 3)), pltpu.SemaphoreType.DMA((n, 3)),
           pltpu.SemaphoreType.DMA((2, N_DEV - 1)), pltpu.SemaphoreType.DMA((2, N_DEV - 1)),
           pltpu.SemaphoreType.DMA((2,))]
        + [pltpu.VMEM((4,) + b, BF16) for b in blocks] * 2 + [pltpu.VMEM((3,) + b, BF16) for b in blocks]
        + [pltpu.SemaphoreType.DMA((n, 4))] * 3,
        compiler_params=_params(56),
    )(qkv, qkv, qkv, bias_table, d_out, *cuts, *small)


def _mid_fwd_bwd(x, target, attn_out, z, w_pa, w_pb, w_out, b_gate, ln_g, ln_b, w_s, b_s, final_g):
    s = x.shape[0]
    tm = TOKEN_TILE
    nt = s // tm

    def body(x_ref, t_ref, oa_ref, ga_ref, ub_ref, vb_ref, gb_ref, ta0_ref, ta1_ref, tb0_ref, tb1_ref,
             wpa_hbm, wpb_hbm, wout_hbm, bg_ref, lng_ref, lnb_ref, ws_ref, bs_ref, fg_ref,
             dx2_ref, doa_ref, dz_ref, dwout_hbm, dwpa_hbm, dwpb_hbm, dbg_ref, dfg_ref, dlng_ref, dlnb_ref, dws_ref,
             dbs_ref, loss_ref,
             wpa, wpb, wout, wmix, acc_out, acc_pa, acc_pb, cut_out, cut_pa, cut_pb, sem):
        i = pl.program_id(0)

        @pl.when(i == 0)
        def _():
            loads = [pltpu.make_async_copy(src, dst, sem.at[n])
                     for n, (src, dst) in enumerate(((wpa_hbm, wpa), (wpb_hbm, wpb), (wout_hbm, wout)))]
            for cp in loads:
                cp.start()
            t_idx = lax.broadcasted_iota(jnp.int32, (SGU_CHUNK, SGU_CHUNK), 0)
            s_idx = lax.broadcasted_iota(jnp.int32, (SGU_CHUNK, SGU_CHUNK), 1)
            for g in range(N_GROUPS):
                wmix[g] = jnp.where(s_idx <= t_idx, ws_ref[g], 0.0).astype(BF16)
            for ref in (acc_out, acc_pa, acc_pb, dbg_ref, dfg_ref, dlng_ref, dlnb_ref, dws_ref, dbs_ref, loss_ref):
                ref[...] = jnp.zeros(ref.shape, F32)
            for cp in loads:
                cp.wait()

        def tile_fwd_bwd(rows):
            g_a = ga_ref[rows, :].astype(F32)
            u_b = ub_ref[rows, :].astype(F32)
            v_b = vb_ref[rows, :].astype(F32)
            g_b = gb_ref[rows, :].astype(F32)
            bg = bg_ref[...]
            sg_a = _sigmoid(g_a)
            silu_a = g_a * sg_a
            o_a = oa_ref[rows, :]
            y_a = (o_a * silu_a).astype(BF16)
            ug, dgelu_u = _gelu_and_grad(u_b)
            vg, dgelu_v = _gelu_and_grad(v_b)
            mu = jnp.mean(vg, axis=-1, keepdims=True)
            vc = vg - mu
            rstd = lax.rsqrt(jnp.mean(vc * vc, axis=-1, keepdims=True) + EPS)
            vhat = vc * rstd
            lng = lng_ref[...]
            vn = (vhat * lng + lnb_ref[...]).astype(BF16)
            sg_b = _sigmoid(g_b)
            silu_b = g_b * sg_b
            subs = [slice(n * SGU_CHUNK, (n + 1) * SGU_CHUNK) for n in range(tm // SGU_CHUNK)]
            mixed = jnp.concatenate([jnp.concatenate(
                [_dot(wmix[g], vn[sub, g * 128:(g + 1) * 128]) + bs_ref[g] for g in range(N_GROUPS)], axis=1)
                for sub in subs], axis=0)
            um = ug * mixed
            y_b = (um * silu_b).astype(BF16)
            gate_a = _sigmoid(jnp.concatenate([ta0_ref[rows, :], ta1_ref[rows, :]], axis=1).astype(F32)
                              + bg[:, :D_MODEL])
            gate_b = _sigmoid(jnp.concatenate([tb0_ref[rows, :], tb1_ref[rows, :]], axis=1).astype(F32)
                              + bg[:, D_MODEL:])
            p_a = _dot(y_a, wpa[...])
            p_b = _dot(y_b, wpb[...])
            merged = (gate_a * p_a + gate_b * p_b).astype(BF16)
            x2 = x_ref[rows, :] + _dot(merged, wout[...])
            r2 = lax.rsqrt(jnp.mean(x2 * x2, axis=-1, keepdims=True) + EPS)
            xh = x2 * r2
            fg = fg_ref[...]
            err = xh * fg - t_ref[rows, :]
            loss_ref[...] += jnp.sum(jnp.sum(err * err, axis=-1, keepdims=True), axis=0, keepdims=True) * (0.5 / D_MODEL)
            dy = err * (1.0 / D_MODEL)
            dfg_ref[...] += jnp.sum(dy * xh, axis=0, keepdims=True)
            gy = dy * fg
            dx2 = r2 * (gy - xh * jnp.mean(gy * xh, axis=-1, keepdims=True))
            dx2_ref[rows, :] = dx2
            dx2b = dx2.astype(BF16)
            dmerged = _dot(dx2b, wout[...], NT)
            acc_out[...] += _dot(merged, dx2b, TN)
            dp_a = dmerged * gate_a
            dp_b = dmerged * gate_b
            dgate_a = dp_a * p_a * (1.0 - gate_a)
            dgate_b = dp_b * p_b * (1.0 - gate_b)
            dbg_ref[:, :D_MODEL] += jnp.sum(dgate_a, axis=0, keepdims=True)
            dbg_ref[:, D_MODEL:] += jnp.sum(dgate_b, axis=0, keepdims=True)
            dz_ref[rows, 2048:3072] = dgate_a.astype(BF16)
            dz_ref[rows, 3072:4096] = dgate_b.astype(BF16)
            dp_ab = dp_a.astype(BF16)
            dp_bb = dp_b.astype(BF16)
            dy_a = _dot(dp_ab, wpa[...], NT)
            dy_b = _dot(dp_bb, wpb[...], NT)
            acc_pa[...] += _dot(y_a, dp_ab, TN)
            acc_pb[...] += _dot(y_b, dp_bb, TN)
            doa_ref[rows, :] = (dy_a * silu_a).astype(BF16)
            dz_ref[rows, 0:512] = (dy_a * o_a * (sg_a * (1.0 + g_a * (1.0 - sg_a)))).astype(BF16)
            dz_ref[rows, 1536:2048] = (dy_b * um * (sg_b * (1.0 + g_b * (1.0 - sg_b)))).astype(BF16)
            dys = dy_b * silu_b
            dz_ref[rows, 512:1024] = (dys * mixed * dgelu_u).astype(BF16)
            dmixed = dys * ug
            dmb = dmixed.astype(BF16)
            dvn_rows = []
            for sub in subs:
                dvn_parts = []
                for g in range(N_GROUPS):
                    cols = slice(g * 128, (g + 1) * 128)
                    dws_ref[g] += _dot(dmb[sub, cols], vn[sub, cols], NT)
                    dbs_ref[g] += jnp.sum(dmixed[sub, cols], axis=-1, keepdims=True)
                    dvn_parts.append(_dot(wmix[g], dmb[sub, cols], TN))
                dvn_rows.append(jnp.concatenate(dvn_parts, axis=1))
            dvn = jnp.concatenate(dvn_rows, axis=0)
            dlng_ref[...] += jnp.sum(dvn * vhat, axis=0, keepdims=True)
            dlnb_ref[...] += jnp.sum(dvn, axis=0, keepdims=True)
            dvh = dvn * lng
            dvg = rstd * (dvh - jnp.mean(dvh, axis=-1, keepdims=True)
                          - vhat * jnp.mean(dvh * vhat, axis=-1, keepdims=True))
            dz_ref[rows, 1024:1536] = (dvg * dgelu_v).astype(BF16)

        tile_fwd_bwd(slice(0, tm))

        @pl.when(i == nt - 1)
        def _():
            t_idx = lax.broadcasted_iota(jnp.int32, (SGU_CHUNK, SGU_CHUNK), 0)
            s_idx = lax.broadcasted_iota(jnp.int32, (SGU_CHUNK, SGU_CHUNK), 1)
            for g in range(N_GROUPS):
                dws_ref[g] = jnp.where(s_idx <= t_idx, dws_ref[g], 0.0)
            for d in range(N_DEV):
                cut_out[d] = acc_out[d * SHARD:(d + 1) * SHARD, :].astype(BF16)
                cut_pa[d] = acc_pa[:, d * SHARD:(d + 1) * SHARD].astype(BF16)
                cut_pb[d] = acc_pb[:, d * SHARD:(d + 1) * SHARD].astype(BF16)
            stores = [pltpu.make_async_copy(src, dst, sem.at[n])
                      for n, (src, dst) in enumerate(((cut_out, dwout_hbm), (cut_pa, dwpa_hbm), (cut_pb, dwpb_hbm)))]
            for cp in stores:
                cp.start()
            for cp in stores:
                cp.wait()

    tile = lambda w: pl.BlockSpec((tm, w), lambda i: (i, 0))
    whole = lambda shape: pl.BlockSpec(shape, lambda i: (0,) * len(shape))
    hbm = pl.BlockSpec(memory_space=pl.ANY)
    cut_shapes = ((N_DEV, SHARD, D_MODEL), (N_DEV, D_A, SHARD), (N_DEV, D_B, SHARD))
    return pl.pallas_call(
        body, name="mid_fwd_bwd",
        grid=(nt,),
        in_specs=[tile(D_MODEL), tile(D_MODEL), tile(D_A)]
        + [pl.BlockSpec((tm, COL_BLOCK), functools.partial(lambda c, i: (i + Z_PAD // tm, c), c))
           for c in range(3, N_COL_BLOCKS)]
        + [hbm, hbm, hbm,
                  whole((1, 2 * D_MODEL)), whole((1, D_B)), whole((1, D_B)),
                  whole((N_GROUPS, SGU_CHUNK, SGU_CHUNK)), whole((N_GROUPS, SGU_CHUNK, 1)), whole((1, D_MODEL))],
        out_specs=(tile(D_MODEL), tile(D_A), tile(REST), hbm, hbm, hbm,
                   whole((1, 2 * D_MODEL)), whole((1, D_MODEL)), whole((1, D_B)), whole((1, D_B)),
                   whole((N_GROUPS, SGU_CHUNK, SGU_CHUNK)), whole((N_GROUPS, SGU_CHUNK, 1)), whole((1, 1))),
        out_shape=(jax.ShapeDtypeStruct((s, D_MODEL), F32), jax.ShapeDtypeStruct((s, D_A), BF16),
                   jax.ShapeDtypeStruct((s, REST), BF16),
                   *(jax.ShapeDtypeStruct(shape, BF16) for shape in cut_shapes),
                   jax.ShapeDtypeStruct((1, 2 * D_MODEL), F32), jax.ShapeDtypeStruct((1, D_MODEL), F32),
                   jax.ShapeDtypeStruct((1, D_B), F32), jax.ShapeDtypeStruct((1, D_B), F32),
                   jax.ShapeDtypeStruct((N_GROUPS, SGU_CHUNK, SGU_CHUNK), F32),
                   jax.ShapeDtypeStruct((N_GROUPS, SGU_CHUNK, 1), F32), jax.ShapeDtypeStruct((1, 1), F32)),
        scratch_shapes=[pltpu.VMEM((D_A, D_MODEL), BF16), pltpu.VMEM((D_B, D_MODEL), BF16),
                        pltpu.VMEM((D_MODEL, D_MODEL), BF16), pltpu.VMEM((N_GROUPS, SGU_CHUNK, SGU_CHUNK), BF16),
                        pltpu.VMEM((D_MODEL, D_MODEL), F32), pltpu.VMEM((D_A, D_MODEL), F32),
                        pltpu.VMEM((D_B, D_MODEL), F32)]
        + [pltpu.VMEM(shape, BF16) for shape in cut_shapes]
        + [pltpu.SemaphoreType.DMA((3,))],
        compiler_params=_params(56),
    )(x, target, attn_out, *([z] * (N_COL_BLOCKS - 3)), w_pa, w_pb, w_out, b_gate, ln_g, ln_b, w_s, b_s, final_g)


def _proj_bwd_x(dqkv, drest, x, dx2, norm_g, w_in_t, dw, small):
    s = x.shape[0]
    tm = 512 if s % 512 == 0 else TOKEN_TILE
    nt = s // tm
    rows = dw.shape[0] // N_DEV
    half = D_MODEL // 2
    left, right = slice(0, half), slice(half, D_MODEL)

    def body(dqkv_ref, dr_ref, x_ref, dx2_ref, g_ref, w_hbm, dw_hbm, ng_ref, rel_ref,
             dx_ref, fc_ref, slab_land, own_hbm,
             w, tc_ref, slab_stage, via_x, via_y, mine, out_x, out_y, from_sib, my_block, own_stage,
             sem, send_sems, recv_sems, gather_send, gather_recv, keep_sems, sib_send, sib_recv):
        i = pl.program_id(0)
        x_, y_, c_ = _my_pos()
        me = _flat_id((x_, y_, c_))
        xn, yn = (1 - x_, y_, c_), (x_, 1 - y_, c_)
        chips = ((x_, y_),) + _other_chips((x_, y_, c_))

        def block(pos):
            return dw_hbm.at[pl.ds(pl.multiple_of(_flat_id(pos) * rows, 16), rows), :]

        def to_sibling(r):
            return pltpu.make_async_remote_copy(
                src_ref=block((*chips[r], 1 - c_)), dst_ref=from_sib.at[r], send_sem=sib_send.at[r],
                recv_sem=sib_recv.at[r], device_id=(x_, y_, 1 - c_), device_id_type=MESH)

        own_out = pltpu.make_async_copy(own_stage, own_hbm, keep_sems.at[3])
        gather = _SlotGather([slab_land], gather_send, gather_recv, own=[slab_stage])
        keep = [pltpu.make_async_copy(slab_stage, slab_land.at[me], keep_sems.at[0])]

        def copy(k, src, dst, to):
            return pltpu.make_async_remote_copy(src_ref=src, dst_ref=dst, send_sem=send_sems.at[k],
                                                recv_sem=recv_sems.at[k], device_id=to, device_id_type=MESH)

        first = [copy(0, tc_ref.at[0, :, left], fc_ref.at[0, :, left], xn), copy(1, tc_ref.at[2, :, left], via_x, xn),
                 copy(2, tc_ref.at[1, :, right], fc_ref.at[1, :, right], yn), copy(3, tc_ref.at[2, :, right], via_y, yn)]
        second = [copy(4, out_y, fc_ref.at[1, :, left], yn), copy(5, out_x, fc_ref.at[0, :, right], xn)]

        def add_and_send(arrival, landed, own_half, stage, onward):
            load = pltpu.make_async_copy(own_half, mine, sem)
            load.start()
            arrival.wait_recv()
            load.wait()
            stage[...] = (mine[...].astype(F32) + landed[...].astype(F32)).astype(BF16)
            onward.start()

        @pl.when(i == 0)
        def _():
            for r in (1, 2, 3, 0):
                to_sibling(r).start()
            cp = pltpu.make_async_copy(w_hbm, w, sem)
            cp.start()
            _fill_slab(slab_stage, ((ROW_NORM_G, ng_ref), (ROW_REL, rel_ref)))
            for cp_keep in keep:
                cp_keep.start()
            gather.start()
            onward = {1: first[:1], 2: first[2:3], 3: [first[1], first[3]], 0: []}
            for r in (1, 2, 3, 0):
                load = pltpu.make_async_copy(block((*chips[r], c_)), my_block, keep_sems.at[2])
                load.start()
                to_sibling(r).wait_recv()
                load.wait()
                both = my_block[...].astype(F32) + from_sib[r].astype(F32)
                if r == 0:
                    own_stage[...] = both
                else:
                    tc_ref[r - 1] = both.astype(BF16)
                for rc in onward[r]:
                    rc.start()
            own_out.start()
            cp.wait()

        @pl.when(i == (5 * nt) // 8)
        def _():
            gather.pass_on()
            add_and_send(first[1], via_x, tc_ref.at[1, :, left], out_y, second[0])
            add_and_send(first[3], via_y, tc_ref.at[0, :, right], out_x, second[1])

        dh = None
        for c in range(N_COL_BLOCKS):
            dz = dqkv_ref[c] if c < 3 else dr_ref[:, (c - 3) * COL_BLOCK:(c - 2) * COL_BLOCK]
            part = _dot(dz, w[c * COL_BLOCK:(c + 1) * COL_BLOCK, :])
            dh = part if dh is None else dh + part
        xf = x_ref[...]
        r = lax.rsqrt(jnp.mean(xf * xf, axis=-1, keepdims=True) + EPS)
        xn = xf * r
        gh = dh * g_ref[...]
        dx_ref[...] = r * (gh - xn * jnp.mean(gh * xn, axis=-1, keepdims=True)) + dx2_ref[...]

        @pl.when(i == nt - 1)
        def _():
            gather.finish()
            for cp_keep in keep:
                cp_keep.wait()
            for k in (0, 2, 4, 5):
                (first + second)[k].wait_recv()
            for rc in first + second:
                rc.wait_send()
            own_out.wait()
            for r in range(4):
                to_sibling(r).wait_send()

    hbm = pl.BlockSpec(memory_space=pl.ANY)
    whole = lambda a: pl.BlockSpec(a.shape, lambda i: (0,) * a.ndim)
    return pl.pallas_call(
        body, name="proj_bwd_x",
        grid=(nt,),
        in_specs=[pl.BlockSpec((3, tm, D_A), lambda i: (0, i, 0)),
                  pl.BlockSpec((tm, REST), lambda i: (i, 0)),
                  pl.BlockSpec((tm, D_MODEL), lambda i: (i, 0)),
                  pl.BlockSpec((tm, D_MODEL), lambda i: (i, 0)),
                  pl.BlockSpec((1, D_MODEL), lambda i: (0, 0)),
                  hbm, hbm] + [whole(a) for a in small],
        out_specs=(pl.BlockSpec((tm, D_MODEL), lambda i: (i, 0)), hbm, hbm, hbm),
        out_shape=(jax.ShapeDtypeStruct((s, D_MODEL), F32), jax.ShapeDtypeStruct((2, rows, D_MODEL), BF16),
                   jax.ShapeDtypeStruct((N_DEV, LATE_SLAB_ROWS, D_MODEL), F32),
                   jax.ShapeDtypeStruct((rows, D_MODEL), F32)),
        scratch_shapes=[pltpu.VMEM((D_IN, D_MODEL), BF16), pltpu.VMEM((3, rows, D_MODEL), BF16),
                        pltpu.VMEM((LATE_SLAB_ROWS, D_MODEL), F32)]
        + [pltpu.VMEM((rows, half), BF16)] * 5
        + [pltpu.VMEM((4, rows, D_MODEL), BF16), pltpu.VMEM((rows, D_MODEL), BF16), pltpu.VMEM((rows, D_MODEL), F32)]
        + [pltpu.SemaphoreType.DMA, pltpu.SemaphoreType.DMA((6,)), pltpu.SemaphoreType.DMA((6,)),
           pltpu.SemaphoreType.DMA((1, N_DEV - 1)), pltpu.SemaphoreType.DMA((1, N_DEV - 1)),
           pltpu.SemaphoreType.DMA((4,)), pltpu.SemaphoreType.DMA((4,)), pltpu.SemaphoreType.DMA((4,))],
        compiler_params=_params(60),
    )(dqkv, drest, x, dx2, norm_g, w_in_t, dw, *small)


def _proj_bwd_w(xn, dqkv, drest, norm_g, w_in_t):
    s = xn.shape[0]
    tk = s
    nk = s // tk

    def body(xn_ref, dqkv_ref, dr_ref, g_ref, w_ref, o_ref, dg_ref, acc):
        j = pl.program_id(0)
        i = pl.program_id(1)

        @pl.when((j == 0) & (i == 0))
        def _():
            dg_ref[...] = jnp.zeros(dg_ref.shape, F32)

        @pl.when(i == 0)
        def _():
            acc[...] = jnp.zeros(acc.shape, F32)

        @pl.when(j < 3)
        def _():
            acc[...] += _dot(dqkv_ref[...], xn_ref[...], TN)

        @pl.when(j >= 3)
        def _():
            acc[...] += _dot(dr_ref[...], xn_ref[...], TN)

        @pl.when(i == nk - 1)
        def _():
            m = acc[...]
            o_ref[...] = (m * g_ref[...]).astype(BF16)
            dg_ref[...] += jnp.sum(m * w_ref[...].astype(F32), axis=0, keepdims=True)

    return pl.pallas_call(
        body, name="proj_bwd_w",
        grid=(N_COL_BLOCKS, nk),
        in_specs=[pl.BlockSpec((tk, D_MODEL), lambda j, i: (i, 0)),
                  pl.BlockSpec((None, tk, COL_BLOCK),
                               lambda j, i: (jnp.minimum(j, 2), jnp.where(j < 3, i, nk - 1), 0)),
                  pl.BlockSpec((tk, COL_BLOCK),
                               lambda j, i: (jnp.where(j >= 3, i, 0), jnp.maximum(j - 3, 0))),
                  pl.BlockSpec((1, D_MODEL), lambda j, i: (0, 0)),
                  pl.BlockSpec((COL_BLOCK, D_MODEL), lambda j, i: (j, 0))],
        out_specs=(pl.BlockSpec((COL_BLOCK, D_MODEL), lambda j, i: (j, 0)),
                   pl.BlockSpec((1, D_MODEL), lambda j, i: (0, 0))),
        out_shape=(jax.ShapeDtypeStruct((D_IN, D_MODEL), BF16), jax.ShapeDtypeStruct((1, D_MODEL), F32)),
        scratch_shapes=[pltpu.VMEM((COL_BLOCK, D_MODEL), F32)],
        compiler_params=_params(56),
    )(xn, dqkv, drest, norm_g, w_in_t)


def _adamw_math(w, g, m, v):
    c1 = 1.0 - ADAM_B1 ** ADAM_STEP
    c2 = 1.0 - ADAM_B2 ** ADAM_STEP
    nm = ADAM_B1 * m + (1.0 - ADAM_B1) * g
    nv = ADAM_B2 * v + (1.0 - ADAM_B2) * (g * g)
    return -ADAM_LR * ((nm / c1) / (jnp.sqrt(nv / c2) + ADAM_EPS) + ADAM_WD * w), nm, nv


def _adamw(name, w, g, m, v, from_chip):
    rows, cols = w.shape
    tr = rows if rows * cols <= 512 * 1024 else next(t for t in range(256, 7, -8) if rows % t == 0)

    def body(w_ref, g_ref, m_ref, v_ref, t_ref, g_out, d_ref, nm_ref, nv_ref):
        gg = g_ref[...]
        for j in range(from_chip.shape[0]):
            gg = gg + t_ref[j].astype(F32)
        g_out[...] = gg
        d_ref[...], nm_ref[...], nv_ref[...] = _adamw_math(w_ref[...], gg, m_ref[...], v_ref[...])

    spec = pl.BlockSpec((tr, cols), lambda i: (i, 0))
    shape = jax.ShapeDtypeStruct((rows, cols), F32)
    return pl.pallas_call(
        body, name=name,
        grid=(rows // tr,),
        in_specs=[spec] * 4 + [pl.BlockSpec((from_chip.shape[0], tr, cols), lambda i: (0, i, 0))],
        out_specs=(spec,) * 4, out_shape=(shape,) * 4,
        compiler_params=_params(32),
    )(w, g, m, v, from_chip)


_SMALL = (("norm_g", (1, D_MODEL)), ("b_gate", (1, 2 * D_MODEL)), ("rel_bias", (N_HEADS, N_REL)),
          ("sgu_ln_g", (1, D_B)), ("sgu_ln_b", (1, D_B)), ("w_s", (N_GROUPS * SGU_CHUNK, SGU_CHUNK)),
          ("b_s", (N_GROUPS, SGU_CHUNK)), ("final_g", (1, D_MODEL)))


def _adamw_small(slabs, ws_all, late_slabs, weights, moments_m, moments_v):
    k = len(_SMALL)

    def total(ref):
        acc = ref[0]
        for d in range(1, N_DEV):
            acc = acc + ref[d]
        return acc

    def body(*refs):
        slab_ref, ws_ref, late_ref = refs[:3]
        w_refs, m_refs, v_refs = refs[3:3 + k], refs[3 + k:3 + 2 * k], refs[3 + 2 * k:3 + 3 * k]
        outs = refs[3 + 3 * k:]
        slab, late = total(slab_ref), total(late_ref)
        grads = {
            "norm_g": late[ROW_NORM_G:ROW_NORM_G + 1, :],
            "b_gate": jnp.concatenate([slab[ROW_B_GATE:ROW_B_GATE + 1, :], slab[ROW_B_GATE + 1:ROW_B_GATE + 2, :]], axis=1),
            "rel_bias": late[ROW_REL:ROW_REL + N_HEADS, :N_REL],
            "sgu_ln_g": slab[ROW_LN_G:ROW_LN_G + 1, :D_B],
            "sgu_ln_b": slab[ROW_LN_B:ROW_LN_B + 1, :D_B],
            "w_s": total(ws_ref),
            "b_s": slab[ROW_B_S:ROW_B_S + N_GROUPS, :SGU_CHUNK],
            "final_g": slab[ROW_FINAL_G:ROW_FINAL_G + 1, :],
        }
        for n, (name, _) in enumerate(_SMALL):
            g = grads[name]
            outs[n][...] = g
            outs[k + n][...], outs[2 * k + n][...], outs[3 * k + n][...] = _adamw_math(
                w_refs[n][...], g, m_refs[n][...], v_refs[n][...])
        outs[4 * k][...] = slab[ROW_LOSS:ROW_LOSS + 1, :1]

    vmem = pl.BlockSpec(memory_space=pltpu.VMEM)
    shapes = tuple(jax.ShapeDtypeStruct(shape, F32) for _, shape in _SMALL)
    return pl.pallas_call(
        body, name="adamw_small",
        out_shape=shapes * 4 + (jax.ShapeDtypeStruct((1, 1), F32),),
        in_specs=[vmem] * (3 + 3 * k), out_specs=tuple([vmem] * (4 * k + 1)),
        compiler_params=_params(16),
    )(slabs, ws_all, late_slabs, *weights, *moments_m, *moments_v)


def _pad_rel(a):
    return jnp.pad(a.reshape(N_HEADS, N_REL), ((0, 0), (0, N_REL_PAD - N_REL)))


def kernel(x, norm_g, w_in, b_gate, rel_bias, sgu_ln_g, sgu_ln_b, w_s, b_s, w_pa, w_pb, w_out, final_g, loss_target, m_norm_g, m_w_in, m_b_gate, m_rel_bias, m_sgu_ln_g, m_sgu_ln_b, m_w_s, m_b_s, m_w_pa, m_w_pb, m_w_out, m_final_g, v_norm_g, v_w_in, v_b_gate, v_rel_bias, v_sgu_ln_g, v_sgu_ln_b, v_w_s, v_b_s, v_w_pa, v_w_pb, v_w_out, v_final_g):
    s = x.shape[1]
    xs = x.reshape(s, D_MODEL)
    tgt = loss_target.reshape(s, D_MODEL)

    bias_table = _bias_table(_pad_rel(rel_bias))
    w_in_t = jnp.swapaxes(w_in[0], 0, 1)
    qkv, x_norm, w_in_t_full = _gather_proj_fwd(xs, norm_g, w_in_t)
    attn_out, g_pa, g_pb, g_out = _attn_fwd(qkv, bias_table, (w_pa[0], w_pb[0], w_out[0]))
    w_pa_full = jnp.transpose(g_pa, (1, 0, 2)).reshape(D_A, D_MODEL)
    w_pb_full = jnp.transpose(g_pb, (1, 0, 2)).reshape(D_B, D_MODEL)
    w_out_full = g_out.reshape(D_MODEL, D_MODEL)

    (dx2, d_attn, drest, dw_out, dw_pa, dw_pb, d_bgate, d_fg, d_lng, d_lnb, d_ws, d_bs, loss_part) = _mid_fwd_bwd(
        xs, tgt, attn_out, qkv, w_pa_full, w_pb_full, w_out_full, b_gate, sgu_ln_g, sgu_ln_b, w_s[0],
        b_s.reshape(N_GROUPS, SGU_CHUNK, 1), final_g.reshape(1, D_MODEL))

    dqkv, dbias, fc_pa, fc_pb, fc_out, slabs, ws_all, own_pa, own_pb, own_out = _attn_bwd(
        qkv, bias_table, d_attn, (dw_pa, dw_pb, dw_out),
        (d_bgate, d_lng, d_lnb, d_fg, loss_part, d_bs, d_ws.reshape(N_GROUPS * SGU_CHUNK, SGU_CHUNK)))
    d_rel = _bias_grad(dbias)
    dw_in_t, d_ng = _proj_bwd_w(x_norm, dqkv, drest, norm_g, w_in_t_full)
    grad_x, fc_in, late_slabs, own_in = _proj_bwd_x(dqkv, drest, xs, dx2, norm_g, w_in_t_full, dw_in_t, (d_ng, d_rel))
    big = {"w_in": tuple(jnp.swapaxes(t, 0, 1)[None] for t in _adamw(
        "adamw_w_in", w_in_t, own_in, jnp.swapaxes(m_w_in[0], 0, 1), jnp.swapaxes(v_w_in[0], 0, 1), fc_in))}
    for name, w, g, fc, m, v in (("w_pa", w_pa, own_pa, fc_pa, m_w_pa, v_w_pa),
                                 ("w_pb", w_pb, own_pb, fc_pb, m_w_pb, v_w_pb),
                                 ("w_out", w_out, own_out, fc_out, m_w_out, v_w_out)):
        big[name] = tuple(t[None] for t in _adamw("adamw_" + name, w[0], g, m[0], v[0], fc))

    as_2d = lambda leaves: [a.reshape(shape) for a, (_, shape) in zip(leaves, _SMALL)]
    small_out = _adamw_small(
        slabs, ws_all, late_slabs, as_2d((norm_g, b_gate, rel_bias, sgu_ln_g, sgu_ln_b, w_s, b_s, final_g)),
        as_2d((m_norm_g, m_b_gate, m_rel_bias, m_sgu_ln_g, m_sgu_ln_b, m_w_s, m_b_s, m_final_g)),
        as_2d((v_norm_g, v_b_gate, v_rel_bias, v_sgu_ln_g, v_sgu_ln_b, v_w_s, v_b_s, v_final_g)))
    small_index = {name: n for n, (name, _) in enumerate(_SMALL)}

    def leaf(kind, name, like):
        if name in big:
            return big[name][kind]
        return small_out[kind * len(_SMALL) + small_index[name]].reshape(like.shape)

    weights = (("norm_g", norm_g), ("w_in", w_in), ("b_gate", b_gate), ("rel_bias", rel_bias), ("sgu_ln_g", sgu_ln_g),
               ("sgu_ln_b", sgu_ln_b), ("w_s", w_s), ("b_s", b_s), ("w_pa", w_pa), ("w_pb", w_pb), ("w_out", w_out),
               ("final_g", final_g))
    outs = [small_out[-1].reshape(()), grad_x.reshape(x.shape)]
    for kind in range(4):
        outs.extend(leaf(kind, name, like) for name, like in weights)
    return tuple(outs)
```
